```python
import jax, jax.numpy as jnp
from jax import lax
import numpy as np

D_MODEL = 1024
BATCH = 2
SEQ = 8192
DEPTH = 1

RWKV_HEADS = 8
RWKV_HEAD_DIM = 64
RWKV_WIDTH = RWKV_HEADS * RWKV_HEAD_DIM
W_LORA = 64
A_LORA = 64
G_LORA = 128
GN_EPS = 64e-5
NSA_HEADS = 8
NSA_KV_GROUPS = 2
NSA_HEADS_PER_GROUP = NSA_HEADS // NSA_KV_GROUPS
NSA_HEAD_DIM = 64
NSA_WIDTH = NSA_HEADS * NSA_HEAD_DIM
NSA_KV_WIDTH = NSA_KV_GROUPS * NSA_HEAD_DIM
CMP_BLOCK = 32
CMP_STRIDE = 16
CMP_HIDDEN = 256
SEL_BLOCK = 64
N_SELECT = 16
WINDOW = 512
Q_BLOCK = 128
ROPE_THETA = 10000.0
RWKV_IN_W = 3 * RWKV_WIDTH + W_LORA + A_LORA + G_LORA
NSA_IN_W = NSA_WIDTH + 6 * NSA_KV_WIDTH + 3 * NSA_HEADS
IN_W = RWKV_IN_W + NSA_IN_W + 2 * D_MODEL
N_EXPERTS = 256
TOP_K = 8
N_GROUPS = 8
TOPK_GROUPS = 4
EXPERT_DIM = 256
SHARED_DIM = 256
ROUTED_SCALE = 2.5
MOE_BLOCK = 128
LN_EPS = 1e-5
DEEPNORM_ALPHA = (2 * DEPTH) ** 0.25
DEEPNORM_BETA = (8 * DEPTH) ** -0.25
NEG_INF = -1e30
FORCE_BONUS = 1e4

kernel_name = "rwkv7_nsa_gated_hybrid_moe_deepnorm"


def layer_norm(x, g, b):
    xf = x.astype(jnp.float32)
    mu = xf.mean(-1, keepdims=True)
    var = jnp.square(xf - mu).mean(-1, keepdims=True)
    return ((xf - mu) * lax.rsqrt(var + LN_EPS) * g + b).astype(x.dtype)


def rope(x, pos):
    half = x.shape[-1] // 2
    inv = ROPE_THETA ** (-jnp.arange(half, dtype=jnp.float32) / half)
    ang = pos.astype(jnp.float32)[:, None] * inv
    cos = jnp.cos(ang)[:, None, :]
    sin = jnp.sin(ang)[:, None, :]
    x1 = x[..., :half].astype(jnp.float32)
    x2 = x[..., half:].astype(jnp.float32)
    return jnp.concatenate([x1 * cos - x2 * sin, x2 * cos + x1 * sin], -1).astype(x.dtype)


def masked_softmax(s, mask):
    return jax.nn.softmax(jnp.where(mask, s, NEG_INF), axis=-1) * mask


def swiglu(h, w_gu, w_down):
    gate, up = jnp.split(h @ w_gu, 2, axis=-1)
    return (jax.nn.silu(gate) * up) @ w_down


def rwkv7_time_mix(p, mu, w0, w2, a0, a2, g2, k_k, k_a, r_k, lnx_w, lnx_b):
    B_, S_, _ = p.shape
    H, N = RWKV_HEADS, RWKV_HEAD_DIM
    f32 = jnp.float32
    prev = jnp.pad(p[:, :-1], ((0, 0), (1, 0), (0, 0)))
    p = p + (prev - p) * mu
    cut = [RWKV_WIDTH, 2 * RWKV_WIDTH, 3 * RWKV_WIDTH, 3 * RWKV_WIDTH + W_LORA, 3 * RWKV_WIDTH + W_LORA + A_LORA]
    r, k, v, wl, al, gl = jnp.split(p, cut, axis=-1)
    w = -jax.nn.softplus(-(w0 + jnp.tanh(wl) @ w2)) - 0.5
    a = jax.nn.sigmoid(a0 + al @ a2)
    g = jax.nn.sigmoid(gl) @ g2
    hs = lambda t: t.reshape(B_, S_, H, N).astype(f32)
    r, w, k, v, a = hs(r), hs(w), hs(k), hs(v), hs(a)
    kk = k * k_k.reshape(H, N).astype(f32)
    kk = kk / jnp.maximum(jnp.linalg.norm(kk, axis=-1, keepdims=True), 1e-12)
    k = k * (1.0 + (a - 1.0) * k_a.reshape(H, N).astype(f32))
    decay = jnp.exp(-jnp.exp(w))

    def step(state, inp):
        r_t, d_t, k_t, v_t, kk_t, b_t = inp
        sa = jnp.einsum('bhvk,bhk->bhv', state, -kk_t)
        state = (state * d_t[:, :, None, :] + sa[..., None] * b_t[:, :, None, :]
                 + v_t[..., None] * k_t[:, :, None, :])
        return state, jnp.einsum('bhvk,bhk->bhv', state, r_t)

    xs = tuple(jnp.moveaxis(t, 1, 0) for t in (r, decay, k, v, kk, kk * a))
    _, o = lax.scan(step, jnp.zeros((B_, H, N, N), f32), xs)
    o = jnp.moveaxis(o, 0, 1)
    m = o.mean(-1, keepdims=True)
    var = jnp.square(o - m).mean(-1, keepdims=True)
    o = ((o - m) * lax.rsqrt(var + GN_EPS)).reshape(B_, S_, RWKV_WIDTH) * lnx_w + lnx_b
    bonus = (r * k * r_k.astype(f32)).sum(-1, keepdims=True) * v
    o = (o + bonus.reshape(B_, S_, RWKV_WIDTH)) * g.astype(f32)
    return o.astype(p.dtype)


def compress_blocks(t, pe, w1, w2):
    B_, S_, G_, Dh = t.shape
    sub = t.reshape(B_, S_ // CMP_STRIDE, CMP_STRIDE, G_, Dh)
    n_sub = CMP_BLOCK // CMP_STRIDE
    n_cmp = S_ // CMP_STRIDE - n_sub + 1
    blocks = jnp.concatenate([sub[:, i:i + n_cmp] for i in range(n_sub)], axis=2)
    blocks = blocks + pe[None, None, :, None, :]
    blocks = blocks.transpose(0, 1, 3, 2, 4).reshape(B_, n_cmp, G_, CMP_BLOCK * Dh)
    return jax.nn.gelu(blocks @ w1) @ w2


def nsa_attention(p, pe_k, w1_k, w2_k, pe_v, w1_v, w2_v):
    B_, S_, _ = p.shape
    G, HPG, HD = NSA_KV_GROUPS, NSA_HEADS_PER_GROUP, NSA_HEAD_DIM
    f32 = jnp.float32
    cut = [NSA_WIDTH + i * NSA_KV_WIDTH for i in range(7)]
    q, kc_raw, vc_raw, ks, vs, kw, vw, gates = jnp.split(p, cut, axis=-1)
    pos = jnp.arange(S_)
    kv = lambda t: t.reshape(B_, S_, G, HD)
    q = rope(q.reshape(B_, S_, NSA_HEADS, HD), pos)
    ks, kw = rope(kv(ks), pos), rope(kv(kw), pos)
    vs, vw = kv(vs), kv(vw)
    n_sub = CMP_BLOCK // CMP_STRIDE
    n_cmp = S_ // CMP_STRIDE - n_sub + 1
    cmp_end = jnp.arange(n_cmp) * CMP_STRIDE + CMP_BLOCK - 1
    kc = rope(compress_blocks(kv(kc_raw), pe_k, w1_k, w2_k), cmp_end)
    vc = compress_blocks(kv(vc_raw), pe_v, w1_v, w2_v)
    n_sel = S_ // SEL_BLOCK
    ratio = SEL_BLOCK // CMP_STRIDE
    ci = jnp.arange(n_cmp)[:, None]
    sj = jnp.arange(n_sel)[None, :]
    cmp_to_sel = sum(((ci + n) // ratio == sj).astype(f32) for n in range(n_sub))
    n_pick = min(N_SELECT, n_sel)
    ks_t = ks.reshape(B_, n_sel, SEL_BLOCK, G, HD).transpose(0, 3, 1, 2, 4)
    vs_t = vs.reshape(B_, n_sel, SEL_BLOCK, G, HD).transpose(0, 3, 1, 2, 4)
    kw_pad = jnp.pad(kw, ((0, 0), (WINDOW, 0), (0, 0), (0, 0)))
    vw_pad = jnp.pad(vw, ((0, 0), (WINDOW, 0), (0, 0), (0, 0)))
    b_ix = jnp.arange(B_)[:, None, None, None]
    g_ix = jnp.arange(G)[None, :, None, None]
    scale = HD ** -0.5
    n_qb = S_ // Q_BLOCK
    qg = q.reshape(B_, n_qb, Q_BLOCK, G, HPG, HD).transpose(1, 0, 2, 3, 4, 5)
    gg = jax.nn.sigmoid(gates.astype(f32)).reshape(B_, n_qb, Q_BLOCK, G, HPG, 3).transpose(1, 0, 2, 3, 4, 5)

    def query_block(args):
        qb, q_blk, g_blk = args
        t = qb * Q_BLOCK + jnp.arange(Q_BLOCK)
        s = jnp.einsum('bqgnd,bcgd->bgnqc', q_blk, kc, preferred_element_type=f32) * scale
        p_c = masked_softmax(s, cmp_end[None, :] <= t[:, None])
        o_c = jnp.einsum('bgnqc,bcgd->bqgnd', p_c.astype(vc.dtype), vc)
        imp = jnp.einsum('bgnqc,cj->bgqj', p_c, cmp_to_sel)
        j = jnp.arange(n_sel)[None, :]
        cur = (t // SEL_BLOCK)[:, None]
        valid = j * SEL_BLOCK <= t[:, None]
        forced = ((j == 0) | (j == cur) | (j == cur - 1)).astype(f32)
        score = jnp.where(valid, imp + FORCE_BONUS * forced, -1.0)
        val, idx = lax.top_k(score, n_pick)
        ks_g = ks_t[b_ix, g_ix, idx]
        vs_g = vs_t[b_ix, g_ix, idx]
        key_pos = idx[..., None] * SEL_BLOCK + jnp.arange(SEL_BLOCK)
        m_s = (val >= 0.0)[..., None] & (key_pos <= t[None, None, :, None, None])
        s = jnp.einsum('bqgnd,bgqkld->bgnqkl', q_blk, ks_g, preferred_element_type=f32) * scale
        p_s = masked_softmax(s.reshape(B_, G, HPG, Q_BLOCK, n_pick * SEL_BLOCK),
                             m_s.reshape(B_, G, 1, Q_BLOCK, n_pick * SEL_BLOCK))
        p_s = p_s.reshape(B_, G, HPG, Q_BLOCK, n_pick, SEL_BLOCK)
        o_s = jnp.einsum('bgnqkl,bgqkld->bqgnd', p_s.astype(vs.dtype), vs_g)
        kwb = lax.dynamic_slice_in_dim(kw_pad, qb * Q_BLOCK, WINDOW + Q_BLOCK, axis=1)
        vwb = lax.dynamic_slice_in_dim(vw_pad, qb * Q_BLOCK, WINDOW + Q_BLOCK, axis=1)
        s_pos = qb * Q_BLOCK - WINDOW + jnp.arange(WINDOW + Q_BLOCK)
        d = t[:, None] - s_pos[None, :]
        m_w = (d >= 0) & (d < WINDOW) & (s_pos[None, :] >= 0)
        s = jnp.einsum('bqgnd,bkgd->bgnqk', q_blk, kwb, preferred_element_type=f32) * scale
        p_w = masked_softmax(s, m_w)
        o_w = jnp.einsum('bgnqk,bkgd->bqgnd', p_w.astype(vw.dtype), vwb)
        return g_blk[..., 0:1] * o_c + g_blk[..., 1:2] * o_s + g_blk[..., 2:3] * o_w

    o = lax.map(query_block, (jnp.arange(n_qb), qg, gg))
    return o.transpose(1, 0, 2, 3, 4, 5).reshape(B_, S_, NSA_WIDTH).astype(p.dtype)


def moe_ffn(x, router_w, router_bias, w_gu, w_down, sw_gu, sw_down):
    B_, S_, D_ = x.shape
    T = B_ * S_
    f32 = jnp.float32
    xf = x.reshape(T, D_)
    scores = jax.nn.sigmoid(jnp.matmul(xf, router_w, preferred_element_type=f32))
    choice = scores + router_bias.astype(f32)
    grp_score = lax.top_k(choice.reshape(T, N_GROUPS, N_EXPERTS // N_GROUPS), 2)[0].sum(-1)
    _, gidx = lax.top_k(grp_score, TOPK_GROUPS)
    gmask = jax.nn.one_hot(gidx, N_GROUPS, dtype=f32).sum(-2) > 0
    emask = jnp.repeat(gmask, N_EXPERTS // N_GROUPS, axis=-1)
    _, eidx = lax.top_k(jnp.where(emask, choice, NEG_INF), TOP_K)
    wts = jnp.take_along_axis(scores, eidx, axis=-1)
    wts = wts / wts.sum(-1, keepdims=True) * ROUTED_SCALE
    n_assign = T * TOP_K
    flat_e = eidx.reshape(-1)
    flat_tok = jnp.repeat(jnp.arange(T, dtype=jnp.int32), TOP_K)
    flat_w = wts.reshape(-1)
    counts = jnp.zeros(N_EXPERTS, jnp.int32).at[flat_e].add(1)
    padded = (counts + MOE_BLOCK - 1) // MOE_BLOCK * MOE_BLOCK
    pad_end = jnp.cumsum(padded)
    pad_start = pad_end - padded
    start = jnp.cumsum(counts) - counts
    order = jnp.argsort(flat_e)
    se = flat_e[order]
    dest = pad_start[se] + jnp.arange(n_assign, dtype=jnp.int32) - start[se]
    n_rows = n_assign + N_EXPERTS * MOE_BLOCK
    n_blocks = n_rows // MOE_BLOCK
    row_tok = jnp.zeros(n_rows, jnp.int32).at[dest].set(flat_tok[order])
    row_w = jnp.zeros(n_rows, f32).at[dest].set(flat_w[order])
    blk_e = jnp.minimum(jnp.searchsorted(pad_end, jnp.arange(n_blocks, dtype=jnp.int32) * MOE_BLOCK, side='right'),
                        N_EXPERTS - 1)

    def expert_block(args):
        tok, w, e = args
        return swiglu(xf[tok], w_gu[e], w_down[e]).astype(f32) * w[:, None]

    ys = lax.map(expert_block, (row_tok.reshape(n_blocks, MOE_BLOCK), row_w.reshape(n_blocks, MOE_BLOCK), blk_e))
    routed = jax.ops.segment_sum(ys.reshape(n_rows, D_), row_tok, num_segments=T)
    shared = swiglu(xf, sw_gu, sw_down).astype(f32)
    return (routed + shared).reshape(B_, S_, D_).astype(x.dtype)


def hybrid_mixer(x, w_in, tshift_mu, rwkv_w0, rwkv_w2, rwkv_a0, rwkv_a2, rwkv_g2, rwkv_k_k, rwkv_k_a,
                 rwkv_r_k, rwkv_lnx_w, rwkv_lnx_b, cmp_pe_k, cmp_w1_k, cmp_w2_k, cmp_pe_v, cmp_w1_v,
                 cmp_w2_v, w_branch_a, w_branch_b, w_out):
    p = x @ w_in
    p_a, p_b, p_g = jnp.split(p, [RWKV_IN_W, RWKV_IN_W + NSA_IN_W], axis=-1)
    y_a = rwkv7_time_mix(p_a, tshift_mu, rwkv_w0, rwkv_w2, rwkv_a0, rwkv_a2, rwkv_g2, rwkv_k_k,
                         rwkv_k_a, rwkv_r_k, rwkv_lnx_w, rwkv_lnx_b) @ w_branch_a
    y_b = nsa_attention(p_b, cmp_pe_k, cmp_w1_k, cmp_w2_k, cmp_pe_v, cmp_w1_v, cmp_w2_v) @ w_branch_b
    g_a, g_b = jnp.split(jax.nn.sigmoid(p_g), 2, axis=-1)
    return (g_a * y_a + g_b * y_b) @ w_out


def setup_inputs(seed: int = 0) -> dict:
    key = jax.random.key(seed)
    keys = iter(jax.random.split(key, 40))
    f32 = jnp.float32
    L = DEPTH
    nrm = lambda shape, s: jax.random.normal(next(keys), shape, f32) * s
    return {
        "x": nrm((BATCH, SEQ, D_MODEL), 1.0),
        "w_in": nrm((L, D_MODEL, IN_W), D_MODEL ** -0.5),
        "tshift_mu": jax.random.uniform(next(keys), (L, RWKV_IN_W), f32),
        "rwkv_w0": jax.random.uniform(next(keys), (L, RWKV_WIDTH), f32, -6.0, -1.0),
        "rwkv_w2": nrm((L, W_LORA, RWKV_WIDTH), 0.1 * W_LORA ** -0.5),
        "rwkv_a0": nrm((L, RWKV_WIDTH), 0.1),
        "rwkv_a2": nrm((L, A_LORA, RWKV_WIDTH), 0.1 * A_LORA ** -0.5),
        "rwkv_g2": nrm((L, G_LORA, RWKV_WIDTH), G_LORA ** -0.5),
        "rwkv_k_k": 0.85 + nrm((L, RWKV_WIDTH), 0.02),
        "rwkv_k_a": 1.0 + nrm((L, RWKV_WIDTH), 0.02),
        "rwkv_r_k": nrm((L, RWKV_HEADS, RWKV_HEAD_DIM), 0.1),
        "rwkv_lnx_w": 1.0 + nrm((L, RWKV_WIDTH), 0.02),
        "rwkv_lnx_b": nrm((L, RWKV_WIDTH), 0.02),
        "cmp_pe_k": nrm((L, CMP_BLOCK, NSA_HEAD_DIM), 0.02),
        "cmp_w1_k": nrm((L, CMP_BLOCK * NSA_HEAD_DIM, CMP_HIDDEN), (CMP_BLOCK * NSA_HEAD_DIM) ** -0.5),
        "cmp_w2_k": nrm((L, CMP_HIDDEN, NSA_HEAD_DIM), CMP_HIDDEN ** -0.5),
        "cmp_pe_v": nrm((L, CMP_BLOCK, NSA_HEAD_DIM), 0.02),
        "cmp_w1_v": nrm((L, CMP_BLOCK * NSA_HEAD_DIM, CMP_HIDDEN), (CMP_BLOCK * NSA_HEAD_DIM) ** -0.5),
        "cmp_w2_v": nrm((L, CMP_HIDDEN, NSA_HEAD_DIM), CMP_HIDDEN ** -0.5),
        "w_branch_a": nrm((L, RWKV_WIDTH, D_MODEL), RWKV_WIDTH ** -0.5),
        "w_branch_b": nrm((L, NSA_WIDTH, D_MODEL), NSA_WIDTH ** -0.5),
        "w_out": nrm((L, D_MODEL, D_MODEL), DEEPNORM_BETA * D_MODEL ** -0.5),
        "ln1_g": 1.0 + nrm((L, D_MODEL), 0.02),
        "ln1_b": nrm((L, D_MODEL), 0.02),
        "router_w": nrm((L, D_MODEL, N_EXPERTS), D_MODEL ** -0.5),
        "router_bias": nrm((L, N_EXPERTS), 0.01),
        "exp_w_gu": nrm((L, N_EXPERTS, D_MODEL, 2 * EXPERT_DIM), D_MODEL ** -0.5),
        "exp_w_down": nrm((L, N_EXPERTS, EXPERT_DIM, D_MODEL), DEEPNORM_BETA * EXPERT_DIM ** -0.5),
        "shared_w_gu": nrm((L, D_MODEL, 2 * SHARED_DIM), D_MODEL ** -0.5),
        "shared_w_down": nrm((L, SHARED_DIM, D_MODEL), DEEPNORM_BETA * SHARED_DIM ** -0.5),
        "ln2_g": 1.0 + nrm((L, D_MODEL), 0.02),
        "ln2_b": nrm((L, D_MODEL), 0.02),
    }


def reference(x, w_in, tshift_mu, rwkv_w0, rwkv_w2, rwkv_a0, rwkv_a2, rwkv_g2, rwkv_k_k, rwkv_k_a,
              rwkv_r_k, rwkv_lnx_w, rwkv_lnx_b, cmp_pe_k, cmp_w1_k, cmp_w2_k, cmp_pe_v, cmp_w1_v, cmp_w2_v,
              w_branch_a, w_branch_b, w_out, ln1_g, ln1_b, router_w, router_bias, exp_w_gu, exp_w_down,
              shared_w_gu, shared_w_down, ln2_g, ln2_b):
    for l in range(DEPTH):
        mix = hybrid_mixer(x, w_in[l], tshift_mu[l], rwkv_w0[l], rwkv_w2[l], rwkv_a0[l], rwkv_a2[l],
                           rwkv_g2[l], rwkv_k_k[l], rwkv_k_a[l], rwkv_r_k[l], rwkv_lnx_w[l], rwkv_lnx_b[l],
                           cmp_pe_k[l], cmp_w1_k[l], cmp_w2_k[l], cmp_pe_v[l], cmp_w1_v[l], cmp_w2_v[l],
                           w_branch_a[l], w_branch_b[l], w_out[l])
        x = layer_norm(DEEPNORM_ALPHA * x + mix, ln1_g[l], ln1_b[l])
        ffn = moe_ffn(x, router_w[l], router_bias[l], exp_w_gu[l], exp_w_down[l], shared_w_gu[l], shared_w_down[l])
        x = layer_norm(DEEPNORM_ALPHA * x + ffn, ln2_g[l], ln2_b[l])
    return x
```

```python
import functools

import numpy as np
import jax
import jax.numpy as jnp
from jax import lax
from jax.experimental import pallas as pl
from jax.experimental.pallas import tpu as pltpu

F32 = jnp.float32
BF16 = jnp.bfloat16
HIGHEST = lax.Precision.HIGHEST

RWKV_HEADS = 8
HEAD_DIM = 64
RWKV_WIDTH = RWKV_HEADS * HEAD_DIM
W_LORA = 64
A_LORA = 64
G_LORA = 128
GN_EPS = 64e-5
NSA_HEADS = 8
NSA_GROUPS = 2
NSA_HPG = NSA_HEADS // NSA_GROUPS
NSA_WIDTH = NSA_HEADS * HEAD_DIM
NSA_KV_WIDTH = NSA_GROUPS * HEAD_DIM
CMP_BLOCK = 32
CMP_STRIDE = 16
CMP_HIDDEN = 256
SEL_BLOCK = 64
N_SELECT = 16
WINDOW = 512
Q_BLOCK = 128
ROPE_THETA = 10000.0
RWKV_IN_W = 3 * RWKV_WIDTH + W_LORA + A_LORA + G_LORA
N_EXPERTS = 256
TOP_K = 8
N_GROUPS = 8
TOPK_GROUPS = 4
EXPERT_DIM = 256
ROUTED_SCALE = 2.5
LN_EPS = 1e-5
NEG_INF = -1e30
FORCE_BONUS = 1e4

RWKV_CHUNK = 64
VMEM_LIMIT = 56 * 1024 * 1024


def _bdot(a, b):
    return jnp.dot(a.astype(BF16), b.astype(BF16), preferred_element_type=F32)


def _bdot_nt(a, b):
    return lax.dot_general(a.astype(BF16), b.astype(BF16), (((1,), (1,)), ((), ())),
                           preferred_element_type=F32)


def _bdot_tn(a, b):
    return lax.dot_general(a.astype(BF16), b.astype(BF16), (((0,), (0,)), ((), ())),
                           preferred_element_type=F32)


def _hdot(a, b):
    return jnp.dot(a, b, precision=HIGHEST, preferred_element_type=F32)


def _sigmoid(x):
    return 1.0 / (1.0 + jnp.exp(-x))


def _matmul_kernel(x_ref, w_ref, o_ref):
    o_ref[...] = jnp.dot(x_ref[...].astype(BF16), w_ref[...], preferred_element_type=F32)


def _matmul(x, w, tm, tn):
    M, K = x.shape
    N = w.shape[1]
    return pl.pallas_call(
        _matmul_kernel,
        grid=(M // tm, N // tn),
        in_specs=[pl.BlockSpec((tm, K), lambda i, j: (i, 0)),
                  pl.BlockSpec((K, tn), lambda i, j: (0, j))],
        out_specs=pl.BlockSpec((tm, tn), lambda i, j: (i, j)),
        out_shape=jax.ShapeDtypeStruct((M, N), F32),
        compiler_params=pltpu.CompilerParams(
            dimension_semantics=("parallel", "parallel"), vmem_limit_bytes=VMEM_LIMIT),
        name="dense_proj",
    )(x, w)


def _rwkv_kernel(p_ref, mu_ref, w0_ref, w2_ref, a0_ref, a2_ref, g2_ref, kk_ref, ka_ref, rk_ref,
                 lnw_ref, lnb_ref, o_ref, carry_ref, state_ref):
    C, H, N = RWKV_CHUNK, RWKV_HEADS, HEAD_DIM
    W = RWKV_WIDTH

    @pl.when(pl.program_id(1) == 0)
    def _():
        carry_ref[...] = jnp.zeros_like(carry_ref)
        state_ref[...] = jnp.zeros_like(state_ref)

    p = p_ref[...]
    row = lax.broadcasted_iota(jnp.int32, p.shape, 0)
    prev = jnp.where(row == 0, carry_ref[...], pltpu.roll(p, 1, axis=0))
    carry_ref[...] = p[C - 1:C, :]
    xs = p + (prev - p) * mu_ref[...]
    r = xs[:, 0:W]
    k = xs[:, W:2 * W]
    v = xs[:, 2 * W:3 * W]
    wl = xs[:, 3 * W:3 * W + W_LORA]
    al = xs[:, 3 * W + W_LORA:3 * W + W_LORA + A_LORA]
    gl = xs[:, 3 * W + W_LORA + A_LORA:]

    z = -(w0_ref[...] + _hdot(jnp.tanh(wl), w2_ref[...]))
    softplus = jnp.maximum(z, 0.0) + jnp.log1p(jnp.exp(-jnp.abs(z)))
    logd = -jnp.exp(-softplus - 0.5)
    a = _sigmoid(a0_ref[...] + _hdot(al, a2_ref[...]))
    g = _hdot(_sigmoid(gl), g2_ref[...])

    kk = k * kk_ref[...]
    knew = k * (1.0 + (a - 1.0) * ka_ref[...])

    ti = lax.broadcasted_iota(jnp.int32, (C, C), 0)
    tj = lax.broadcasted_iota(jnp.int32, (C, C), 1)
    strict = ti > tj
    incl = ti >= tj
    eye = (ti == tj).astype(F32)
    cl = _hdot(incl.astype(F32), logd)
    cl_end = cl[C - 1:C, :]
    gam = jnp.exp(cl)
    gam_prev = jnp.exp(cl - logd)
    inv_gam = jnp.exp(-cl)
    to_end = jnp.exp(cl_end - cl)
    gam_end = jnp.exp(cl_end)

    outs = []
    for h in range(H):
        sl = slice(h * N, (h + 1) * N)
        kk_h = kk[:, sl]
        nrm = jnp.sqrt(jnp.sum(kk_h * kk_h, axis=-1, keepdims=True))
        kk_h = kk_h / jnp.maximum(nrm, 1e-12)
        b_h = kk_h * a[:, sl]
        r_h, k_h, v_h = r[:, sl], knew[:, sl], v[:, sl]
        a_hat = -kk_h * gam_prev[:, sl]
        r_hat = r_h * gam[:, sl]
        b_til = b_h * inv_gam[:, sl]
        k_til = k_h * inv_gam[:, sl]
        s0 = state_ref[h]

        m4 = _bdot_nt(jnp.concatenate([a_hat, r_hat], axis=0),
                      jnp.concatenate([b_til, k_til], axis=0))
        n_ab = jnp.where(strict, m4[:C, :C], 0.0)
        l_ak = jnp.where(strict, m4[:C, C:], 0.0)
        m_rb = jnp.where(incl, m4[C:, :C], 0.0)
        m_rk = jnp.where(incl, m4[C:, C:], 0.0)

        pw = n_ab
        tinv = eye + n_ab
        step = 2
        while step < C:
            pw = _bdot(pw, pw)
            tinv = tinv + _bdot(tinv, pw)
            step *= 2

        rhs = _bdot_nt(a_hat, s0) + _bdot(l_ak, v_h)
        u = _bdot(tinv, rhs)
        uv = jnp.concatenate([u, v_h], axis=0)
        o_h = _bdot_nt(r_hat, s0) + _bdot(jnp.concatenate([m_rb, m_rk], axis=1), uv)
        bk_end = jnp.concatenate([b_h * to_end[:, sl], k_h * to_end[:, sl]], axis=0)
        state_ref[h] = s0 * gam_end[:, sl] + _bdot_tn(uv, bk_end)

        mean = jnp.mean(o_h, axis=-1, keepdims=True)
        var = jnp.mean(jnp.square(o_h - mean), axis=-1, keepdims=True)
        o_h = (o_h - mean) * lax.rsqrt(var + GN_EPS)
        o_h = o_h * lnw_ref[:, sl] + lnb_ref[:, sl]
        bonus = jnp.sum(r_h * k_h * rk_ref[:, sl], axis=-1, keepdims=True) * v_h
        outs.append((o_h + bonus) * g[:, sl])
    o_ref[...] = jnp.concatenate(outs, axis=-1)


def _hdot_nt(a, b):
    return lax.dot_general(a, b, (((1,), (1,)), ((), ())), precision=HIGHEST,
                           preferred_element_type=F32)


def _hdot_tn(a, b):
    return lax.dot_general(a, b, (((0,), (0,)), ((), ())), precision=HIGHEST,
                           preferred_element_type=F32)


def _rwkv_time_mix(p_a, mu, w0, w2, a0, a2, g2, k_k, k_a, r_k, lnx_w, lnx_b):
    B, S, _ = p_a.shape
    C = RWKV_CHUNK
    row = lambda t: t.reshape(1, -1)
    full = lambda shape: pl.BlockSpec(shape, lambda b, s: (0,) * len(shape))
    return pl.pallas_call(
        _rwkv_kernel,
        grid=(B, S // C),
        in_specs=[pl.BlockSpec((None, C, RWKV_IN_W), lambda b, s: (b, s, 0)),
                  full((1, RWKV_IN_W)), full((1, RWKV_WIDTH)), full((W_LORA, RWKV_WIDTH)),
                  full((1, RWKV_WIDTH)), full((A_LORA, RWKV_WIDTH)), full((G_LORA, RWKV_WIDTH)),
                  full((1, RWKV_WIDTH)), full((1, RWKV_WIDTH)), full((1, RWKV_WIDTH)),
                  full((1, RWKV_WIDTH)), full((1, RWKV_WIDTH))],
        out_specs=pl.BlockSpec((None, C, RWKV_WIDTH), lambda b, s: (b, s, 0)),
        out_shape=jax.ShapeDtypeStruct((B, S, RWKV_WIDTH), F32),
        scratch_shapes=[pltpu.VMEM((1, RWKV_IN_W), F32),
                        pltpu.VMEM((RWKV_HEADS, HEAD_DIM, HEAD_DIM), F32)],
        compiler_params=pltpu.CompilerParams(
            dimension_semantics=("arbitrary", "arbitrary"), vmem_limit_bytes=VMEM_LIMIT),
        name="rwkv7_chunked",
    )(p_a, row(mu), row(w0), w2, row(a0), a2, g2, row(k_k), row(k_a), row(r_k), row(lnx_w), row(lnx_b))


NSA_KV_TILE = 1024
SEL_KEY_TILE = 512
SEL_LANES = 128


def _rope_tables(pos, reps):
    half = HEAD_DIM // 2
    inv = ROPE_THETA ** (-jnp.arange(half, dtype=F32) / half)
    ang = pos.astype(F32)[:, None] * inv
    cos, sin = jnp.cos(ang), jnp.sin(ang)
    cosf = jnp.concatenate([cos, cos], -1)
    sinf = jnp.concatenate([-sin, sin], -1)
    return jnp.tile(cosf, (1, reps)), jnp.tile(sinf, (1, reps))


def _rope(x, cosf, sinf):
    width = x.shape[-1]
    lane = lax.broadcasted_iota(jnp.int32, x.shape, 1)
    first_half = (lane % HEAD_DIM) < HEAD_DIM // 2
    rot = jnp.where(first_half, pltpu.roll(x, width - HEAD_DIM // 2, axis=1),
                    pltpu.roll(x, HEAD_DIM // 2, axis=1))
    return x * cosf + rot * sinf


def _kv_layout_kernel(p_ref, cos_ref, sin_ref, kc_ref, vc_ref, ks_ref, vs_ref, kw_ref, vw_ref):
    outs = (kc_ref, vc_ref, ks_ref, vs_ref, kw_ref, vw_ref)
    roped = (False, False, True, False, True, False)
    for i, (o_ref, use_rope) in enumerate(zip(outs, roped)):
        t = p_ref[:, i * NSA_KV_WIDTH:(i + 1) * NSA_KV_WIDTH]
        if use_rope:
            t = _rope(t, cos_ref[...], sin_ref[...])
        for g in range(NSA_GROUPS):
            o_ref[g] = t[:, g * HEAD_DIM:(g + 1) * HEAD_DIM].astype(o_ref.dtype)


def _kv_layout(p_b, cos2, sin2):
    B, S, _ = p_b.shape
    ts = min(NSA_KV_TILE, S)
    out_spec = pl.BlockSpec((None, NSA_GROUPS, ts, HEAD_DIM), lambda b, s: (b, 0, s, 0))
    shp = lambda dt: jax.ShapeDtypeStruct((B, NSA_GROUPS, S, HEAD_DIM), dt)
    return pl.pallas_call(
        _kv_layout_kernel,
        grid=(B, S // ts),
        in_specs=[pl.BlockSpec((None, ts, 6 * NSA_KV_WIDTH), lambda b, s: (b, s, 0)),
                  pl.BlockSpec((ts, NSA_KV_WIDTH), lambda b, s: (s, 0)),
                  pl.BlockSpec((ts, NSA_KV_WIDTH), lambda b, s: (s, 0))],
        out_specs=[out_spec] * 6,
        out_shape=[shp(F32), shp(F32), shp(BF16), shp(BF16), shp(BF16), shp(BF16)],
        compiler_params=pltpu.CompilerParams(
            dimension_semantics=("parallel", "parallel"), vmem_limit_bytes=VMEM_LIMIT),
        name="nsa_kv_layout",
    )(p_b, cos2, sin2)


def _compress_kernel(subk_ref, subv_ref, pek_ref, w1k_ref, w2k_ref, pev_ref, w1v_ref, w2v_ref,
                     cos_ref, sin_ref, kc_ref, vc_ref):
    n_sub = subk_ref.shape[0]
    half = CMP_STRIDE * HEAD_DIM

    def mlp(sub_ref, pe_ref, w1_ref, w2_ref):
        sub = sub_ref[...]
        top = _bdot(sub, w1_ref[:half, :])
        bot = _bdot(sub, w1_ref[half:, :])
        bias = _bdot(jnp.broadcast_to(pe_ref[...], (8, 2 * half)), w1_ref[...])[0:1, :]
        h = top + pltpu.roll(bot, n_sub - 1, axis=0) + bias
        return _bdot(jax.nn.gelu(h), w2_ref[...])

    kc = mlp(subk_ref, pek_ref, w1k_ref, w2k_ref)
    rot = jnp.concatenate([kc[:, HEAD_DIM // 2:], kc[:, :HEAD_DIM // 2]], axis=-1)
    kc_ref[...] = (kc * cos_ref[...] + rot * sin_ref[...]).astype(kc_ref.dtype)
    vc_ref[...] = mlp(subv_ref, pev_ref, w1v_ref, w2v_ref).astype(vc_ref.dtype)


def _compress(subk, subv, pe_k, w1_k, w2_k, pe_v, w1_v, w2_v, cos_c, sin_c):
    B, G, n_sub, width = subk.shape
    sub_spec = pl.BlockSpec((None, None, n_sub, width), lambda b, g: (b, g, 0, 0))
    full = lambda a: pl.BlockSpec(a.shape, lambda b, g: (0,) * a.ndim)
    out_spec = pl.BlockSpec((None, None, n_sub, HEAD_DIM), lambda b, g: (b, g, 0, 0))
    pe_k, pe_v = pe_k.reshape(1, -1), pe_v.reshape(1, -1)
    args = (pe_k, w1_k, w2_k, pe_v, w1_v, w2_v, cos_c, sin_c)
    return pl.pallas_call(
        _compress_kernel,
        grid=(B, G),
        in_specs=[sub_spec, sub_spec] + [full(a) for a in args],
        out_specs=[out_spec, out_spec],
        out_shape=[jax.ShapeDtypeStruct((B, G, n_sub, HEAD_DIM), BF16)] * 2,
        compiler_params=pltpu.CompilerParams(
            dimension_semantics=("parallel", "parallel"), vmem_limit_bytes=VMEM_LIMIT),
        name="nsa_compress",
    )(subk, subv, *args)


def _masked_softmax_rows(s, mask):
    sm = jnp.where(mask, s, NEG_INF)
    m = jnp.max(sm, axis=-1, keepdims=True)
    e = jnp.where(mask, jnp.exp(sm - m), 0.0)
    den = jnp.sum(e, axis=-1, keepdims=True)
    return e / jnp.where(den > 0.0, den, 1.0)


def _nsa_kernel(q_ref, gate_ref, cos_ref, sin_ref, kc_ref, vc_ref, ks_ref, vs_ref, kw_ref, vw_ref,
                msel_ref, o_ref, *, n_pick):
    QB, HP, D = Q_BLOCK, NSA_HPG, HEAD_DIM
    qb = pl.program_id(2)
    n_cmp = kc_ref.shape[0]
    scale = D ** -0.5

    q = _rope(q_ref[...], cos_ref[...], sin_ref[...]) * scale
    q4 = jnp.concatenate([q[:, n * D:(n + 1) * D] for n in range(HP)], axis=0).astype(BF16)
    t_col = qb * QB + lax.broadcasted_iota(jnp.int32, (QB, 1), 0)
    t4 = jnp.concatenate([t_col] * HP, axis=0)

    s = _bdot_nt(q4, kc_ref[...])
    cmp_end = lax.broadcasted_iota(jnp.int32, (1, n_cmp), 1) * CMP_STRIDE + (CMP_BLOCK - 1)
    p_c = _masked_softmax_rows(s, cmp_end <= t4)
    o_c = _bdot(p_c, vc_ref[...])
    p_sum = p_c[0:QB]
    for n in range(1, HP):
        p_sum = p_sum + p_c[n * QB:(n + 1) * QB]
    p_hi = p_sum.astype(BF16)
    p_lo = (p_sum - p_hi.astype(F32)).astype(BF16)
    imp = (jnp.dot(p_hi, msel_ref[...], preferred_element_type=F32)
           + jnp.dot(p_lo, msel_ref[...], preferred_element_type=F32))

    j = lax.broadcasted_iota(jnp.int32, (QB, SEL_LANES), 1)
    cur = t_col // SEL_BLOCK
    valid = j * SEL_BLOCK <= t_col
    forced = (j == 0) | (j == cur) | (j == cur - 1)
    score = jnp.where(valid, imp + jnp.where(forced, FORCE_BONUS, 0.0), -1.0)
    sel = jnp.zeros((QB, SEL_LANES), F32)
    for _ in range(n_pick):
        m = jnp.max(score, axis=-1, keepdims=True)
        idx = jnp.min(jnp.where(score == m, j, SEL_LANES), axis=-1, keepdims=True)
        hit = j == idx
        sel = jnp.where(hit & (m >= 0.0), 1.0, sel)
        score = jnp.where(hit, -2.0, score)
    sel_b = sel.astype(BF16)

    KT = SEL_KEY_TILE
    blocks_per_tile = KT // SEL_BLOCK
    n_tiles = (qb * QB + QB + KT - 1) // KT

    def sel_step(kt, carry):
        m_i, l_i, acc = carry
        start = pl.multiple_of(kt * KT, KT)
        k_t = ks_ref[pl.ds(start, KT), :]
        v_t = vs_ref[pl.ds(start, KT), :]
        jrow = lax.broadcasted_iota(jnp.int32, (SEL_LANES, KT), 0)
        ccol = lax.broadcasted_iota(jnp.int32, (SEL_LANES, KT), 1)
        expand = (jrow == kt * blocks_per_tile + ccol // SEL_BLOCK).astype(BF16)
        picked = jnp.dot(sel_b, expand, preferred_element_type=F32) > 0.5
        kpos = start + lax.broadcasted_iota(jnp.int32, (1, KT), 1)
        mask = jnp.concatenate([picked & (kpos <= t_col)] * HP, axis=0)
        sm = jnp.where(mask, _bdot_nt(q4, k_t), NEG_INF)
        m_new = jnp.maximum(m_i, jnp.max(sm, axis=-1, keepdims=True))
        e = jnp.where(mask, jnp.exp(sm - m_new), 0.0)
        alpha = jnp.exp(m_i - m_new)
        l_new = alpha * l_i + jnp.sum(e, axis=-1, keepdims=True)
        acc_new = alpha * acc + _bdot(e, v_t)
        return m_new, l_new, acc_new

    init = (jnp.full((HP * QB, 1), NEG_INF, F32), jnp.zeros((HP * QB, 1), F32),
            jnp.zeros((HP * QB, D), F32))
    _, l_s, acc_s = lax.fori_loop(0, n_tiles, sel_step, init)
    o_s = acc_s / jnp.where(l_s > 0.0, l_s, 1.0)

    span = WINDOW + QB
    w_start = pl.multiple_of(jnp.maximum(qb * QB - WINDOW, 0), QB)
    kpos = w_start + lax.broadcasted_iota(jnp.int32, (1, span), 1)
    dist = t4 - kpos
    p_w = _masked_softmax_rows(_bdot_nt(q4, kw_ref[pl.ds(w_start, span), :]),
                               (dist >= 0) & (dist < WINDOW))
    o_w = _bdot(p_w, vw_ref[pl.ds(w_start, span), :])

    gates = _sigmoid(gate_ref[...])
    for n in range(HP):
        rows = slice(n * QB, (n + 1) * QB)
        o_ref[:, n * D:(n + 1) * D] = (gates[:, 3 * n:3 * n + 1] * o_c[rows]
                                       + gates[:, 3 * n + 1:3 * n + 2] * o_s[rows]
                                       + gates[:, 3 * n + 2:3 * n + 3] * o_w[rows])


def _cmp_to_sel_matrix(n_cmp_rows, n_sel):
    ratio = SEL_BLOCK // CMP_STRIDE
    ci = np.arange(n_cmp_rows)[:, None]
    sj = np.arange(SEL_LANES)[None, :]
    m = sum(((ci + n) // ratio == sj).astype(np.float32) for n in range(CMP_BLOCK // CMP_STRIDE))
    m = m * (sj < n_sel) * (ci < n_cmp_rows - 1)
    return jnp.asarray(m, BF16)


def _nsa_attention(p_b, kc, vc, ks, vs, kw, vw, cos_q, sin_q):
    B, S, _ = p_b.shape
    n_sub = kc.shape[2]
    n_sel = S // SEL_BLOCK
    gw = NSA_HPG * HEAD_DIM
    q_col0 = 6 * NSA_KV_WIDTH // gw
    gate_col0 = (6 * NSA_KV_WIDTH + NSA_WIDTH) // 128
    msel = _cmp_to_sel_matrix(n_sub, n_sel)
    cmp_spec = pl.BlockSpec((None, None, n_sub, HEAD_DIM), lambda b, g, i: (b, g, 0, 0))
    kv_spec = pl.BlockSpec((None, None, S, HEAD_DIM), lambda b, g, i: (b, g, 0, 0))
    return pl.pallas_call(
        functools.partial(_nsa_kernel, n_pick=min(N_SELECT, n_sel)),
        grid=(B, NSA_GROUPS, S // Q_BLOCK),
        in_specs=[pl.BlockSpec((None, Q_BLOCK, gw), lambda b, g, i: (b, i, q_col0 + g)),
                  pl.BlockSpec((None, Q_BLOCK, 128), lambda b, g, i: (b, i, gate_col0 + g)),
                  pl.BlockSpec((Q_BLOCK, gw), lambda b, g, i: (i, 0)),
                  pl.BlockSpec((Q_BLOCK, gw), lambda b, g, i: (i, 0)),
                  cmp_spec, cmp_spec, kv_spec, kv_spec, kv_spec, kv_spec,
                  pl.BlockSpec(msel.shape, lambda b, g, i: (0, 0))],
        out_specs=pl.BlockSpec((None, Q_BLOCK, gw), lambda b, g, i: (b, i, g)),
        out_shape=jax.ShapeDtypeStruct((B, S, NSA_WIDTH), F32),
        compiler_params=pltpu.CompilerParams(
            dimension_semantics=("parallel", "parallel", "arbitrary"), vmem_limit_bytes=VMEM_LIMIT),
        name="nsa_attention",
    )(p_b, p_b, cos_q, sin_q, kc, vc, ks, vs, kw, vw, msel)


def _nsa_branch(p_b, cmp_pe_k, cmp_w1_k, cmp_w2_k, cmp_pe_v, cmp_w1_v, cmp_w2_v):
    B, S, _ = p_b.shape
    pos = jnp.arange(S)
    cos2, sin2 = _rope_tables(pos, NSA_GROUPS)
    kc_raw, vc_raw, ks, vs, kw, vw = _kv_layout(p_b, cos2, sin2)
    n_sub = S // CMP_STRIDE
    sub = lambda t: t.reshape(B, NSA_GROUPS, n_sub, CMP_STRIDE * HEAD_DIM)
    cos_c, sin_c = _rope_tables(jnp.arange(n_sub) * CMP_STRIDE + CMP_BLOCK - 1, 1)
    kc, vc = _compress(sub(kc_raw), sub(vc_raw), cmp_pe_k, cmp_w1_k, cmp_w2_k,
                       cmp_pe_v, cmp_w1_v, cmp_w2_v, cos_c, sin_c)
    cos_q, sin_q = _rope_tables(pos, NSA_HPG)
    return _nsa_attention(p_b, kc, vc, ks, vs, kw, vw, cos_q, sin_q)


def _nsa_weight_columns(w_nsa):
    K = w_nsa.shape[0]
    q = w_nsa[:, :NSA_WIDTH]
    kv = w_nsa[:, NSA_WIDTH:NSA_WIDTH + 6 * NSA_KV_WIDTH]
    gates = w_nsa[:, NSA_WIDTH + 6 * NSA_KV_WIDTH:]
    per_group = NSA_HPG * 3
    gate_blocks = [jnp.pad(gates[:, g * per_group:(g + 1) * per_group], ((0, 0), (0, 128 - per_group)))
                   for g in range(NSA_GROUPS)]
    return jnp.concatenate([kv, q] + gate_blocks, axis=1)


def _layer_norm(h, g, b):
    mu = jnp.mean(h, axis=-1, keepdims=True)
    var = jnp.mean(jnp.square(h - mu), axis=-1, keepdims=True)
    return (h - mu) * lax.rsqrt(var + LN_EPS) * g + b


def _mixer_out_kernel(x_ref, ya_ref, yb_ref, pg_ref, wa_ref, wb_ref, wo_ref, g_ref, b_ref, o_ref, *, alpha):
    d = x_ref.shape[-1]
    gate_a = _sigmoid(pg_ref[:, :d])
    gate_b = _sigmoid(pg_ref[:, d:])
    mixed = gate_a * _bdot(ya_ref[...], wa_ref[...]) + gate_b * _bdot(yb_ref[...], wb_ref[...])
    h = alpha * x_ref[...] + _bdot(mixed, wo_ref[...])
    o_ref[...] = _layer_norm(h, g_ref[...], b_ref[...])


def _mixer_out(xf, ya, yb, p_g, wa, wb, wo, ln_g, ln_b, alpha, tm=512):
    T, D = xf.shape
    rows = lambda w: pl.BlockSpec((tm, w), lambda i: (i, 0))
    full = lambda a: pl.BlockSpec(a.shape, lambda i: (0,) * a.ndim)
    ln_g, ln_b = ln_g.reshape(1, D), ln_b.reshape(1, D)
    return pl.pallas_call(
        functools.partial(_mixer_out_kernel, alpha=alpha),
        grid=(T // tm,),
        in_specs=[rows(D), rows(ya.shape[1]), rows(yb.shape[1]), rows(2 * D),
                  full(wa), full(wb), full(wo), full(ln_g), full(ln_b)],
        out_specs=rows(D),
        out_shape=jax.ShapeDtypeStruct((T, D), F32),
        compiler_params=pltpu.CompilerParams(
            dimension_semantics=("parallel",), vmem_limit_bytes=VMEM_LIMIT),
        name="mixer_out_ln",
    )(xf, ya, yb, p_g, wa, wb, wo, ln_g, ln_b)


ROUTER_TILE = 256
EXPERT_ROWS = 256
GATHER_ROWS = 256
PICK_LANES = 128
LOWEST = -3.0e38


def _router_kernel(x_ref, rw_ref, bias_ref, eidx_ref, wts_ref, pos_ref, cnt_ref, carry_ref):
    tm, E = x_ref.shape[0], rw_ref.shape[1]
    per_group = E // N_GROUPS

    @pl.when(pl.program_id(0) == 0)
    def _():
        carry_ref[...] = jnp.zeros_like(carry_ref)

    scores = _sigmoid(_hdot(x_ref[...], rw_ref[...]))
    choice = scores + bias_ref[...]
    lane = lax.broadcasted_iota(jnp.int32, (tm, E), 1)
    grp = lane // per_group

    def first_max(vals):
        m = jnp.max(vals, axis=-1, keepdims=True)
        return m, jnp.min(jnp.where(vals == m, lane, E), axis=-1, keepdims=True)

    group_score = []
    for g in range(N_GROUPS):
        cg = jnp.where(grp == g, choice, LOWEST)
        m1, i1 = first_max(cg)
        m2 = jnp.max(jnp.where(lane == i1, LOWEST, cg), axis=-1, keepdims=True)
        group_score.append(m1 + m2)
    allowed = jnp.zeros((tm, E), jnp.bool_)
    for g in range(N_GROUPS):
        rank = jnp.zeros((tm, 1), jnp.int32)
        for o in range(N_GROUPS):
            if o != g:
                ahead = (group_score[o] > group_score[g]) if o > g else (group_score[o] >= group_score[g])
                rank = rank + ahead.astype(jnp.int32)
        allowed = allowed | ((grp == g) & (rank < TOPK_GROUPS))

    cur = jnp.where(allowed, choice, NEG_INF)
    sel = jnp.zeros((tm, E), F32)
    picks = []
    for _ in range(TOP_K):
        _, idx = first_max(cur)
        hit = lane == idx
        picks.append(idx)
        sel = jnp.where(hit, 1.0, sel)
        cur = jnp.where(hit, LOWEST, cur)
    gate = scores * sel
    gate = gate / jnp.sum(gate, axis=-1, keepdims=True) * ROUTED_SCALE

    ri = lax.broadcasted_iota(jnp.int32, (tm, tm), 0)
    ci = lax.broadcasted_iota(jnp.int32, (tm, tm), 1)
    before = jnp.dot((ri > ci).astype(BF16), sel.astype(BF16), preferred_element_type=F32)
    queue_pos = before + carry_ref[...]
    carry_ref[...] = carry_ref[...] + jnp.sum(sel, axis=0, keepdims=True)
    cnt_ref[...] = jnp.broadcast_to(carry_ref[...], cnt_ref.shape)

    out_lane = lax.broadcasted_iota(jnp.int32, (tm, PICK_LANES), 1)
    eidx = jnp.zeros((tm, PICK_LANES), jnp.int32)
    wts = jnp.zeros((tm, PICK_LANES), F32)
    pos = jnp.zeros((tm, PICK_LANES), F32)
    for kk, idx in enumerate(picks):
        hit = lane == idx
        eidx = jnp.where(out_lane == kk, idx, eidx)
        wts = jnp.where(out_lane == kk, jnp.sum(jnp.where(hit, gate, 0.0), axis=-1, keepdims=True), wts)
        pos = jnp.where(out_lane == kk, jnp.sum(jnp.where(hit, queue_pos, 0.0), axis=-1, keepdims=True), pos)
    eidx_ref[...] = eidx
    wts_ref[...] = wts
    pos_ref[...] = pos.astype(jnp.int32)


def _router(xf, router_w, router_bias):
    T, D = xf.shape
    E = router_w.shape[1]
    tm = ROUTER_TILE
    picks = lambda dt: jax.ShapeDtypeStruct((T, PICK_LANES), dt)
    pick_spec = pl.BlockSpec((tm, PICK_LANES), lambda i: (i, 0))
    return pl.pallas_call(
        _router_kernel,
        grid=(T // tm,),
        in_specs=[pl.BlockSpec((tm, D), lambda i: (i, 0)),
                  pl.BlockSpec((D, E), lambda i: (0, 0)),
                  pl.BlockSpec((1, E), lambda i: (0, 0))],
        out_specs=[pick_spec, pick_spec, pick_spec, pl.BlockSpec((8, E), lambda i: (0, 0))],
        out_shape=[picks(jnp.int32), picks(F32), picks(jnp.int32), jax.ShapeDtypeStruct((8, E), F32)],
        scratch_shapes=[pltpu.VMEM((1, E), F32)],
        compiler_params=pltpu.CompilerParams(
            dimension_semantics=("arbitrary",), vmem_limit_bytes=VMEM_LIMIT),
        name="moe_router",
    )(xf, router_w, router_bias.reshape(1, E))


def _gather_rows_kernel(idx_ref, src_ref, o_ref, sem):
    n = o_ref.shape[0]

    def row_copy(r, src_row):
        return pltpu.make_async_copy(src_ref.at[pl.ds(src_row, 1), :], o_ref.at[pl.ds(r, 1), :], sem)

    def issue(r, c):
        row_copy(r, idx_ref[0, r]).start()
        return c

    def drain(r, c):
        row_copy(r, 0).wait()
        return c

    lax.fori_loop(0, n, issue, 0)
    lax.fori_loop(0, n, drain, 0)


def _gather_rows(src, idx):
    M = idx.shape[0]
    D = src.shape[1]
    R = GATHER_ROWS
    return pl.pallas_call(
        _gather_rows_kernel,
        grid=(M // R,),
        in_specs=[pl.BlockSpec((None, 1, R), lambda i: (i, 0, 0), memory_space=pltpu.SMEM),
                  pl.BlockSpec(memory_space=pl.ANY)],
        out_specs=pl.BlockSpec((R, D), lambda i: (i, 0)),
        out_shape=jax.ShapeDtypeStruct((M, D), src.dtype),
        scratch_shapes=[pltpu.SemaphoreType.DMA],
        compiler_params=pltpu.CompilerParams(
            dimension_semantics=("arbitrary",), vmem_limit_bytes=VMEM_LIMIT),
        name="gather_rows",
    )(idx.reshape(M // R, 1, R), src)


def _expert_kernel(blk_e_ref, n_used_ref, x_ref, wgu_ref, wd_ref, o_ref):
    live = pl.program_id(0) < n_used_ref[0]

    @pl.when(live)
    def _():
        hidden = wd_ref.shape[0]
        h = _bdot(x_ref[...], wgu_ref[...])
        gate, up = h[:, :hidden], h[:, hidden:]
        act = gate * _sigmoid(gate) * up
        o_ref[...] = _bdot(act, wd_ref[...])

    @pl.when(jnp.logical_not(live))
    def _():
        o_ref[...] = jnp.zeros_like(o_ref)


def _expert_ffn(xs, blk_e, n_used, w_gu, w_down):
    n_rows, D = xs.shape
    E, _, two_h = w_gu.shape
    n_blocks = n_rows // EXPERT_ROWS
    live = lambda i, n_used: jnp.minimum(i, n_used[0] - 1)
    grid_spec = pltpu.PrefetchScalarGridSpec(
        num_scalar_prefetch=2,
        grid=(n_blocks,),
        in_specs=[pl.BlockSpec((EXPERT_ROWS, D), lambda i, be, nu: (live(i, nu), 0)),
                  pl.BlockSpec((None, D, two_h), lambda i, be, nu: (be[live(i, nu)], 0, 0)),
                  pl.BlockSpec((None, two_h // 2, D), lambda i, be, nu: (be[live(i, nu)], 0, 0))],
        out_specs=pl.BlockSpec((EXPERT_ROWS, D), lambda i, be, nu: (i, 0)),
    )
    return pl.pallas_call(
        _expert_kernel,
        grid_spec=grid_spec,
        out_shape=jax.ShapeDtypeStruct((n_rows, D), F32),
        compiler_params=pltpu.CompilerParams(
            dimension_semantics=("arbitrary",), vmem_limit_bytes=VMEM_LIMIT),
        name="moe_experts",
    )(blk_e, n_used, xs, w_gu, w_down)


def _moe_out_kernel(x_ref, yk_ref, wts_ref, sgu_ref, sd_ref, g_ref, b_ref, o_ref, *, alpha):
    x = x_ref[...]
    d = x.shape[-1]
    hidden = sd_ref.shape[0]
    h = _bdot(x, sgu_ref[...])
    gate, up = h[:, :hidden], h[:, hidden:]
    ffn = _bdot(gate * _sigmoid(gate) * up, sd_ref[...])
    wts = wts_ref[...]
    for kk in range(TOP_K):
        ffn = ffn + wts[:, kk:kk + 1] * yk_ref[:, kk * d:(kk + 1) * d]
    o_ref[...] = _layer_norm(alpha * x + ffn, g_ref[...], b_ref[...])


def _moe_out(xf, yk, wts, sw_gu, sw_down, ln_g, ln_b, alpha, tm=128):
    T, D = xf.shape
    rows = lambda w: pl.BlockSpec((tm, w), lambda i: (i, 0))
    full = lambda a: pl.BlockSpec(a.shape, lambda i: (0,) * a.ndim)
    ln_g, ln_b = ln_g.reshape(1, D), ln_b.reshape(1, D)
    return pl.pallas_call(
        functools.partial(_moe_out_kernel, alpha=alpha),
        grid=(T // tm,),
        in_specs=[rows(D), rows(TOP_K * D), rows(PICK_LANES), full(sw_gu), full(sw_down),
                  full(ln_g), full(ln_b)],
        out_specs=rows(D),
        out_shape=jax.ShapeDtypeStruct((T, D), F32),
        compiler_params=pltpu.CompilerParams(
            dimension_semantics=("parallel",), vmem_limit_bytes=VMEM_LIMIT),
        name="moe_combine_ln",
    )(xf, yk, wts, sw_gu, sw_down, ln_g, ln_b)


def _moe_ffn_ln(xf, router_w, router_bias, w_gu, w_down, sw_gu, sw_down, ln_g, ln_b, alpha):
    T, D = xf.shape
    E = router_w.shape[1]
    BM = EXPERT_ROWS
    eidx, wts, pos, cnt = _router(xf, router_w, router_bias)
    counts = cnt[0].astype(jnp.int32)
    padded = (counts + BM - 1) // BM * BM
    pad_end = jnp.cumsum(padded)
    pad_start = pad_end - padded
    n_rows = T * TOP_K + E * BM
    n_blocks = n_rows // BM
    dest = pad_start[eidx[:, :TOP_K]] + pos[:, :TOP_K]
    tok = jnp.broadcast_to(jnp.arange(T, dtype=jnp.int32)[:, None], (T, TOP_K))
    row_tok = jnp.zeros((n_rows,), jnp.int32).at[dest.reshape(-1)].set(tok.reshape(-1))
    blk_e = jnp.minimum(jnp.searchsorted(pad_end, jnp.arange(n_blocks, dtype=jnp.int32) * BM, side='right'),
                        E - 1).astype(jnp.int32)
    n_used = (pad_end[-1:] // BM).astype(jnp.int32)
    xs = _gather_rows(xf, row_tok)
    ys = _expert_ffn(xs, blk_e, n_used, w_gu, w_down)
    yk = _gather_rows(ys, dest.reshape(-1)).reshape(T, TOP_K * D)
    return _moe_out(xf, yk, wts, sw_gu, sw_down, ln_g, ln_b, alpha)


def kernel(x, w_in, tshift_mu, rwkv_w0, rwkv_w2, rwkv_a0, rwkv_a2, rwkv_g2, rwkv_k_k, rwkv_k_a, rwkv_r_k, rwkv_lnx_w, rwkv_lnx_b, cmp_pe_k, cmp_w1_k, cmp_w2_k, cmp_pe_v, cmp_w1_v, cmp_w2_v, w_branch_a, w_branch_b, w_out, ln1_g, ln1_b, router_w, router_bias, exp_w_gu, exp_w_down, shared_w_gu, shared_w_down, ln2_g, ln2_b):
    B, S, D = x.shape
    depth = w_in.shape[0]
    alpha = (2 * depth) ** 0.25
    nsa_w = w_in.shape[2] - RWKV_IN_W - 2 * D
    for l in range(depth):
        xf = x.reshape(B * S, D)
        w_l = w_in[l]
        w_a = w_l[:, :RWKV_IN_W].astype(BF16)
        w_b = _nsa_weight_columns(w_l[:, RWKV_IN_W:RWKV_IN_W + nsa_w]).astype(BF16)
        w_g = w_l[:, RWKV_IN_W + nsa_w:].astype(BF16)
        p_a = _matmul(xf, w_a, 512, RWKV_IN_W // 2).reshape(B, S, -1)
        p_b = _matmul(xf, w_b, 512, w_b.shape[1] // 2).reshape(B, S, -1)
        p_g = _matmul(xf, w_g, 512, D)
        y_a = _rwkv_time_mix(p_a, tshift_mu[l], rwkv_w0[l], rwkv_w2[l], rwkv_a0[l], rwkv_a2[l], rwkv_g2[l],
                             rwkv_k_k[l], rwkv_k_a[l], rwkv_r_k[l].reshape(-1), rwkv_lnx_w[l], rwkv_lnx_b[l])
        y_b = _nsa_branch(p_b, cmp_pe_k[l], cmp_w1_k[l], cmp_w2_k[l], cmp_pe_v[l], cmp_w1_v[l], cmp_w2_v[l])
        x1 = _mixer_out(xf, y_a.reshape(B * S, -1), y_b.reshape(B * S, -1), p_g,
                        w_branch_a[l].astype(BF16), w_branch_b[l].astype(BF16), w_out[l].astype(BF16),
                        ln1_g[l], ln1_b[l], alpha)
        x2 = _moe_ffn_ln(x1, router_w[l], router_bias[l], exp_w_gu[l], exp_w_down[l],
                         shared_w_gu[l].astype(BF16), shared_w_down[l].astype(BF16), ln2_g[l], ln2_b[l], alpha)
        x = x2.reshape(B, S, D)
    return x
```

```python
import functools

import numpy as np
import jax
import jax.numpy as jnp
from jax import lax
from jax.experimental import pallas as pl
from jax.experimental.pallas import tpu as pltpu
from jax.experimental.pallas import tpu_sc as plsc

F32 = jnp.float32
BF16 = jnp.bfloat16
HIGHEST = lax.Precision.HIGHEST

RWKV_HEADS = 8
HEAD_DIM = 64
RWKV_WIDTH = RWKV_HEADS * HEAD_DIM
W_LORA = 64
A_LORA = 64
G_LORA = 128
GN_EPS = 64e-5
NSA_HEADS = 8
NSA_GROUPS = 2
NSA_HPG = NSA_HEADS // NSA_GROUPS
NSA_WIDTH = NSA_HEADS * HEAD_DIM
NSA_KV_WIDTH = NSA_GROUPS * HEAD_DIM
CMP_BLOCK = 32
CMP_STRIDE = 16
CMP_HIDDEN = 256
SEL_BLOCK = 64
N_SELECT = 16
WINDOW = 512
Q_BLOCK = 128
ROPE_THETA = 10000.0
RWKV_IN_W = 3 * RWKV_WIDTH + W_LORA + A_LORA + G_LORA
N_EXPERTS = 256
TOP_K = 8
N_GROUPS = 8
TOPK_GROUPS = 4
EXPERT_DIM = 256
ROUTED_SCALE = 2.5
LN_EPS = 1e-5
NEG_INF = -1e30
FORCE_BONUS = 1e4

RWKV_CHUNK = 64
VMEM_LIMIT = 56 * 1024 * 1024


def _bdot(a, b):
    return jnp.dot(a.astype(BF16), b.astype(BF16), preferred_element_type=F32)


def _bdot_nt(a, b):
    return lax.dot_general(a.astype(BF16), b.astype(BF16), (((1,), (1,)), ((), ())),
                           preferred_element_type=F32)


def _bdot_tn(a, b):
    return lax.dot_general(a.astype(BF16), b.astype(BF16), (((0,), (0,)), ((), ())),
                           preferred_element_type=F32)


def _hdot(a, b):
    return jnp.dot(a, b, precision=HIGHEST, preferred_element_type=F32)


def _sigmoid(x):
    return 1.0 / (1.0 + jnp.exp(-x))


def _matmul_kernel(x_ref, w_ref, o_ref):
    o_ref[...] = jnp.dot(x_ref[...].astype(BF16), w_ref[...], preferred_element_type=F32)


def _matmul(x, w, tm, tn):
    M, K = x.shape
    N = w.shape[1]
    return pl.pallas_call(
        _matmul_kernel,
        grid=(M // tm, N // tn),
        in_specs=[pl.BlockSpec((tm, K), lambda i, j: (i, 0)),
                  pl.BlockSpec((K, tn), lambda i, j: (0, j))],
        out_specs=pl.BlockSpec((tm, tn), lambda i, j: (i, j)),
        out_shape=jax.ShapeDtypeStruct((M, N), F32),
        compiler_params=pltpu.CompilerParams(
            dimension_semantics=("parallel", "parallel"), vmem_limit_bytes=VMEM_LIMIT),
        name="dense_proj",
    )(x, w)


def _rwkv_kernel(p_ref, mu_ref, w0_ref, w2_ref, a0_ref, a2_ref, g2_ref, kk_ref, ka_ref, rk_ref,
                 lnw_ref, lnb_ref, o_ref, carry_ref, state_ref):
    C, H, N = RWKV_CHUNK, RWKV_HEADS, HEAD_DIM
    W = RWKV_WIDTH

    @pl.when(pl.program_id(1) == 0)
    def _():
        carry_ref[...] = jnp.zeros_like(carry_ref)
        state_ref[...] = jnp.zeros_like(state_ref)

    p = p_ref[...]
    row = lax.broadcasted_iota(jnp.int32, p.shape, 0)
    prev = jnp.where(row == 0, carry_ref[...], pltpu.roll(p, 1, axis=0))
    carry_ref[...] = p[C - 1:C, :]
    xs = p + (prev - p) * mu_ref[...]
    r = xs[:, 0:W]
    k = xs[:, W:2 * W]
    v = xs[:, 2 * W:3 * W]
    wl = xs[:, 3 * W:3 * W + W_LORA]
    al = xs[:, 3 * W + W_LORA:3 * W + W_LORA + A_LORA]
    gl = xs[:, 3 * W + W_LORA + A_LORA:]

    z = -(w0_ref[...] + _hdot(jnp.tanh(wl), w2_ref[...]))
    softplus = jnp.maximum(z, 0.0) + jnp.log1p(jnp.exp(-jnp.abs(z)))
    logd = -jnp.exp(-softplus - 0.5)
    a = _sigmoid(a0_ref[...] + _hdot(al, a2_ref[...]))
    g = _hdot(_sigmoid(gl), g2_ref[...])

    kk = k * kk_ref[...]
    knew = k * (1.0 + (a - 1.0) * ka_ref[...])

    ti = lax.broadcasted_iota(jnp.int32, (C, C), 0)
    tj = lax.broadcasted_iota(jnp.int32, (C, C), 1)
    strict = ti > tj
    incl = ti >= tj
    eye = (ti == tj).astype(F32)
    cl = _hdot(incl.astype(F32), logd)
    cl_end = cl[C - 1:C, :]
    gam = jnp.exp(cl)
    gam_prev = jnp.exp(cl - logd)
    inv_gam = jnp.exp(-cl)
    to_end = jnp.exp(cl_end - cl)
    gam_end = jnp.exp(cl_end)

    outs = []
    for h in range(H):
        sl = slice(h * N, (h + 1) * N)
        kk_h = kk[:, sl]
        nrm = jnp.sqrt(jnp.sum(kk_h * kk_h, axis=-1, keepdims=True))
        kk_h = kk_h / jnp.maximum(nrm, 1e-12)
        b_h = kk_h * a[:, sl]
        r_h, k_h, v_h = r[:, sl], knew[:, sl], v[:, sl]
        a_hat = -kk_h * gam_prev[:, sl]
        r_hat = r_h * gam[:, sl]
        b_til = b_h * inv_gam[:, sl]
        k_til = k_h * inv_gam[:, sl]
        s0 = state_ref[h]

        m4 = _bdot_nt(jnp.concatenate([a_hat, r_hat], axis=0),
                      jnp.concatenate([b_til, k_til], axis=0))
        n_ab = jnp.where(strict, m4[:C, :C], 0.0)
        l_ak = jnp.where(strict, m4[:C, C:], 0.0)
        m_rb = jnp.where(incl, m4[C:, :C], 0.0)
        m_rk = jnp.where(incl, m4[C:, C:], 0.0)

        pw = n_ab
        tinv = eye + n_ab
        step = 2
        while step < C:
            pw = _bdot(pw, pw)
            tinv = tinv + _bdot(tinv, pw)
            step *= 2

        rhs = _bdot_nt(a_hat, s0) + _bdot(l_ak, v_h)
        u = _bdot(tinv, rhs)
        uv = jnp.concatenate([u, v_h], axis=0)
        o_h = _bdot_nt(r_hat, s0) + _bdot(jnp.concatenate([m_rb, m_rk], axis=1), uv)
        bk_end = jnp.concatenate([b_h * to_end[:, sl], k_h * to_end[:, sl]], axis=0)
        state_ref[h] = s0 * gam_end[:, sl] + _bdot_tn(uv, bk_end)

        mean = jnp.mean(o_h, axis=-1, keepdims=True)
        var = jnp.mean(jnp.square(o_h - mean), axis=-1, keepdims=True)
        o_h = (o_h - mean) * lax.rsqrt(var + GN_EPS)
        o_h = o_h * lnw_ref[:, sl] + lnb_ref[:, sl]
        bonus = jnp.sum(r_h * k_h * rk_ref[:, sl], axis=-1, keepdims=True) * v_h
        outs.append((o_h + bonus) * g[:, sl])
    o_ref[...] = jnp.concatenate(outs, axis=-1)


def _hdot_nt(a, b):
    return lax.dot_general(a, b, (((1,), (1,)), ((), ())), precision=HIGHEST,
                           preferred_element_type=F32)


def _hdot_tn(a, b):
    return lax.dot_general(a, b, (((0,), (0,)), ((), ())), precision=HIGHEST,
                           preferred_element_type=F32)


def _rwkv_time_mix(p_a, mu, w0, w2, a0, a2, g2, k_k, k_a, r_k, lnx_w, lnx_b):
    B, S, _ = p_a.shape
    C = RWKV_CHUNK
    row = lambda t: t.reshape(1, -1)
    full = lambda shape: pl.BlockSpec(shape, lambda b, s: (0,) * len(shape))
    return pl.pallas_call(
        _rwkv_kernel,
        grid=(B, S // C),
        in_specs=[pl.BlockSpec((None, C, RWKV_IN_W), lambda b, s: (b, s, 0)),
                  full((1, RWKV_IN_W)), full((1, RWKV_WIDTH)), full((W_LORA, RWKV_WIDTH)),
                  full((1, RWKV_WIDTH)), full((A_LORA, RWKV_WIDTH)), full((G_LORA, RWKV_WIDTH)),
                  full((1, RWKV_WIDTH)), full((1, RWKV_WIDTH)), full((1, RWKV_WIDTH)),
                  full((1, RWKV_WIDTH)), full((1, RWKV_WIDTH))],
        out_specs=pl.BlockSpec((None, C, RWKV_WIDTH), lambda b, s: (b, s, 0)),
        out_shape=jax.ShapeDtypeStruct((B, S, RWKV_WIDTH), F32),
        scratch_shapes=[pltpu.VMEM((1, RWKV_IN_W), F32),
                        pltpu.VMEM((RWKV_HEADS, HEAD_DIM, HEAD_DIM), F32)],
        compiler_params=pltpu.CompilerParams(
            dimension_semantics=("arbitrary", "arbitrary"), vmem_limit_bytes=VMEM_LIMIT),
        name="rwkv7_chunked",
    )(p_a, row(mu), row(w0), w2, row(a0), a2, g2, row(k_k), row(k_a), row(r_k), row(lnx_w), row(lnx_b))


NSA_KV_TILE = 1024
SEL_KEY_TILE = 512
SEL_LANES = 128


def _rope_tables(pos, reps):
    half = HEAD_DIM // 2
    inv = ROPE_THETA ** (-jnp.arange(half, dtype=F32) / half)
    ang = pos.astype(F32)[:, None] * inv
    cos, sin = jnp.cos(ang), jnp.sin(ang)
    cosf = jnp.concatenate([cos, cos], -1)
    sinf = jnp.concatenate([-sin, sin], -1)
    return jnp.tile(cosf, (1, reps)), jnp.tile(sinf, (1, reps))


def _rope(x, cosf, sinf):
    width = x.shape[-1]
    lane = lax.broadcasted_iota(jnp.int32, x.shape, 1)
    first_half = (lane % HEAD_DIM) < HEAD_DIM // 2
    rot = jnp.where(first_half, pltpu.roll(x, width - HEAD_DIM // 2, axis=1),
                    pltpu.roll(x, HEAD_DIM // 2, axis=1))
    return x * cosf + rot * sinf


def _kv_layout_kernel(p_ref, cos_ref, sin_ref, kc_ref, vc_ref, ks_ref, vs_ref, kw_ref, vw_ref):
    outs = (kc_ref, vc_ref, ks_ref, vs_ref, kw_ref, vw_ref)
    roped = (False, False, True, False, True, False)
    for i, (o_ref, use_rope) in enumerate(zip(outs, roped)):
        t = p_ref[:, i * NSA_KV_WIDTH:(i + 1) * NSA_KV_WIDTH]
        if use_rope:
            t = _rope(t, cos_ref[...], sin_ref[...])
        for g in range(NSA_GROUPS):
            o_ref[g] = t[:, g * HEAD_DIM:(g + 1) * HEAD_DIM].astype(o_ref.dtype)


def _kv_layout(p_b, cos2, sin2):
    B, S, _ = p_b.shape
    ts = min(NSA_KV_TILE, S)
    out_spec = pl.BlockSpec((None, NSA_GROUPS, ts, HEAD_DIM), lambda b, s: (b, 0, s, 0))
    shp = lambda dt: jax.ShapeDtypeStruct((B, NSA_GROUPS, S, HEAD_DIM), dt)
    return pl.pallas_call(
        _kv_layout_kernel,
        grid=(B, S // ts),
        in_specs=[pl.BlockSpec((None, ts, 6 * NSA_KV_WIDTH), lambda b, s: (b, s, 0)),
                  pl.BlockSpec((ts, NSA_KV_WIDTH), lambda b, s: (s, 0)),
                  pl.BlockSpec((ts, NSA_KV_WIDTH), lambda b, s: (s, 0))],
        out_specs=[out_spec] * 6,
        out_shape=[shp(F32), shp(F32), shp(BF16), shp(BF16), shp(BF16), shp(BF16)],
        compiler_params=pltpu.CompilerParams(
            dimension_semantics=("parallel", "parallel"), vmem_limit_bytes=VMEM_LIMIT),
        name="nsa_kv_layout",
    )(p_b, cos2, sin2)


def _compress_kernel(subk_ref, subv_ref, pek_ref, w1k_ref, w2k_ref, pev_ref, w1v_ref, w2v_ref,
                     cos_ref, sin_ref, kc_ref, vc_ref):
    n_sub = subk_ref.shape[0]
    half = CMP_STRIDE * HEAD_DIM

    def mlp(sub_ref, pe_ref, w1_ref, w2_ref):
        sub = sub_ref[...]
        top = _bdot(sub, w1_ref[:half, :])
        bot = _bdot(sub, w1_ref[half:, :])
        bias = _bdot(jnp.broadcast_to(pe_ref[...], (8, 2 * half)), w1_ref[...])[0:1, :]
        h = top + pltpu.roll(bot, n_sub - 1, axis=0) + bias
        return _bdot(jax.nn.gelu(h), w2_ref[...])

    kc = mlp(subk_ref, pek_ref, w1k_ref, w2k_ref)
    rot = jnp.concatenate([kc[:, HEAD_DIM // 2:], kc[:, :HEAD_DIM // 2]], axis=-1)
    kc_ref[...] = (kc * cos_ref[...] + rot * sin_ref[...]).astype(kc_ref.dtype)
    vc_ref[...] = mlp(subv_ref, pev_ref, w1v_ref, w2v_ref).astype(vc_ref.dtype)


def _compress(subk, subv, pe_k, w1_k, w2_k, pe_v, w1_v, w2_v, cos_c, sin_c):
    B, G, n_sub, width = subk.shape
    sub_spec = pl.BlockSpec((None, None, n_sub, width), lambda b, g: (b, g, 0, 0))
    full = lambda a: pl.BlockSpec(a.shape, lambda b, g: (0,) * a.ndim)
    out_spec = pl.BlockSpec((None, None, n_sub, HEAD_DIM), lambda b, g: (b, g, 0, 0))
    pe_k, pe_v = pe_k.reshape(1, -1), pe_v.reshape(1, -1)
    args = (pe_k, w1_k, w2_k, pe_v, w1_v, w2_v, cos_c, sin_c)
    return pl.pallas_call(
        _compress_kernel,
        grid=(B, G),
        in_specs=[sub_spec, sub_spec] + [full(a) for a in args],
        out_specs=[out_spec, out_spec],
        out_shape=[jax.ShapeDtypeStruct((B, G, n_sub, HEAD_DIM), BF16)] * 2,
        compiler_params=pltpu.CompilerParams(
            dimension_semantics=("parallel", "parallel"), vmem_limit_bytes=VMEM_LIMIT),
        name="nsa_compress",
    )(subk, subv, *args)


def _masked_softmax_rows(s, mask):
    sm = jnp.where(mask, s, NEG_INF)
    m = jnp.max(sm, axis=-1, keepdims=True)
    e = jnp.where(mask, jnp.exp(sm - m), 0.0)
    den = jnp.sum(e, axis=-1, keepdims=True)
    return e / jnp.where(den > 0.0, den, 1.0)


def _nsa_kernel(q_ref, gate_ref, cos_ref, sin_ref, kc_ref, vc_ref, ks_ref, vs_ref, kw_ref, vw_ref,
                msel_ref, o_ref, *, n_pick):
    QB, HP, D = Q_BLOCK, NSA_HPG, HEAD_DIM
    qb = pl.program_id(2)
    n_cmp = kc_ref.shape[0]
    scale = D ** -0.5

    q = _rope(q_ref[...], cos_ref[...], sin_ref[...]) * scale
    q4 = jnp.concatenate([q[:, n * D:(n + 1) * D] for n in range(HP)], axis=0).astype(BF16)
    t_col = qb * QB + lax.broadcasted_iota(jnp.int32, (QB, 1), 0)
    t4 = jnp.concatenate([t_col] * HP, axis=0)

    s = _bdot_nt(q4, kc_ref[...])
    cmp_end = lax.broadcasted_iota(jnp.int32, (1, n_cmp), 1) * CMP_STRIDE + (CMP_BLOCK - 1)
    p_c = _masked_softmax_rows(s, cmp_end <= t4)
    o_c = _bdot(p_c, vc_ref[...])
    p_sum = p_c[0:QB]
    for n in range(1, HP):
        p_sum = p_sum + p_c[n * QB:(n + 1) * QB]
    p_hi = p_sum.astype(BF16)
    p_lo = (p_sum - p_hi.astype(F32)).astype(BF16)
    imp = (jnp.dot(p_hi, msel_ref[...], preferred_element_type=F32)
           + jnp.dot(p_lo, msel_ref[...], preferred_element_type=F32))

    j = lax.broadcasted_iota(jnp.int32, (QB, SEL_LANES), 1)
    cur = t_col // SEL_BLOCK
    valid = j * SEL_BLOCK <= t_col
    forced = (j == 0) | (j == cur) | (j == cur - 1)
    score = jnp.where(valid, imp + jnp.where(forced, FORCE_BONUS, 0.0), -1.0)
    sel = jnp.zeros((QB, SEL_LANES), F32)
    for _ in range(n_pick):
        m = jnp.max(score, axis=-1, keepdims=True)
        idx = jnp.min(jnp.where(score == m, j, SEL_LANES), axis=-1, keepdims=True)
        hit = j == idx
        sel = jnp.where(hit & (m >= 0.0), 1.0, sel)
        score = jnp.where(hit, -2.0, score)
    sel_b = sel.astype(BF16)

    KT = SEL_KEY_TILE
    blocks_per_tile = KT // SEL_BLOCK
    n_tiles = (qb * QB + QB + KT - 1) // KT

    def sel_step(kt, carry):
        m_i, l_i, acc = carry
        start = pl.multiple_of(kt * KT, KT)
        k_t = ks_ref[pl.ds(start, KT), :]
        v_t = vs_ref[pl.ds(start, KT), :]
        jrow = lax.broadcasted_iota(jnp.int32, (SEL_LANES, KT), 0)
        ccol = lax.broadcasted_iota(jnp.int32, (SEL_LANES, KT), 1)
        expand = (jrow == kt * blocks_per_tile + ccol // SEL_BLOCK).astype(BF16)
        picked = jnp.dot(sel_b, expand, preferred_element_type=F32) > 0.5
        kpos = start + lax.broadcasted_iota(jnp.int32, (1, KT), 1)
        mask = jnp.concatenate([picked & (kpos <= t_col)] * HP, axis=0)
        sm = jnp.where(mask, _bdot_nt(q4, k_t), NEG_INF)
        m_new = jnp.maximum(m_i, jnp.max(sm, axis=-1, keepdims=True))
        e = jnp.where(mask, jnp.exp(sm - m_new), 0.0)
        alpha = jnp.exp(m_i - m_new)
        l_new = alpha * l_i + jnp.sum(e, axis=-1, keepdims=True)
        acc_new = alpha * acc + _bdot(e, v_t)
        return m_new, l_new, acc_new

    init = (jnp.full((HP * QB, 1), NEG_INF, F32), jnp.zeros((HP * QB, 1), F32),
            jnp.zeros((HP * QB, D), F32))
    _, l_s, acc_s = lax.fori_loop(0, n_tiles, sel_step, init)
    o_s = acc_s / jnp.where(l_s > 0.0, l_s, 1.0)

    span = WINDOW + QB
    w_start = pl.multiple_of(jnp.maximum(qb * QB - WINDOW, 0), QB)
    kpos = w_start + lax.broadcasted_iota(jnp.int32, (1, span), 1)
    dist = t4 - kpos
    p_w = _masked_softmax_rows(_bdot_nt(q4, kw_ref[pl.ds(w_start, span), :]),
                               (dist >= 0) & (dist < WINDOW))
    o_w = _bdot(p_w, vw_ref[pl.ds(w_start, span), :])

    gates = _sigmoid(gate_ref[...])
    for n in range(HP):
        rows = slice(n * QB, (n + 1) * QB)
        o_ref[:, n * D:(n + 1) * D] = (gates[:, 3 * n:3 * n + 1] * o_c[rows]
                                       + gates[:, 3 * n + 1:3 * n + 2] * o_s[rows]
                                       + gates[:, 3 * n + 2:3 * n + 3] * o_w[rows])


def _cmp_to_sel_matrix(n_cmp_rows, n_sel):
    ratio = SEL_BLOCK // CMP_STRIDE
    ci = np.arange(n_cmp_rows)[:, None]
    sj = np.arange(SEL_LANES)[None, :]
    m = sum(((ci + n) // ratio == sj).astype(np.float32) for n in range(CMP_BLOCK // CMP_STRIDE))
    m = m * (sj < n_sel) * (ci < n_cmp_rows - 1)
    return jnp.asarray(m, BF16)


def _nsa_attention(p_b, kc, vc, ks, vs, kw, vw, cos_q, sin_q):
    B, S, _ = p_b.shape
    n_sub = kc.shape[2]
    n_sel = S // SEL_BLOCK
    gw = NSA_HPG * HEAD_DIM
    q_col0 = 6 * NSA_KV_WIDTH // gw
    gate_col0 = (6 * NSA_KV_WIDTH + NSA_WIDTH) // 128
    msel = _cmp_to_sel_matrix(n_sub, n_sel)
    cmp_spec = pl.BlockSpec((None, None, n_sub, HEAD_DIM), lambda b, g, i: (b, g, 0, 0))
    kv_spec = pl.BlockSpec((None, None, S, HEAD_DIM), lambda b, g, i: (b, g, 0, 0))
    return pl.pallas_call(
        functools.partial(_nsa_kernel, n_pick=min(N_SELECT, n_sel)),
        grid=(B, NSA_GROUPS, S // Q_BLOCK),
        in_specs=[pl.BlockSpec((None, Q_BLOCK, gw), lambda b, g, i: (b, i, q_col0 + g)),
                  pl.BlockSpec((None, Q_BLOCK, 128), lambda b, g, i: (b, i, gate_col0 + g)),
                  pl.BlockSpec((Q_BLOCK, gw), lambda b, g, i: (i, 0)),
                  pl.BlockSpec((Q_BLOCK, gw), lambda b, g, i: (i, 0)),
                  cmp_spec, cmp_spec, kv_spec, kv_spec, kv_spec, kv_spec,
                  pl.BlockSpec(msel.shape, lambda b, g, i: (0, 0))],
        out_specs=pl.BlockSpec((None, Q_BLOCK, gw), lambda b, g, i: (b, i, g)),
        out_shape=jax.ShapeDtypeStruct((B, S, NSA_WIDTH), F32),
        compiler_params=pltpu.CompilerParams(
            dimension_semantics=("parallel", "parallel", "arbitrary"), vmem_limit_bytes=VMEM_LIMIT),
        name="nsa_attention",
    )(p_b, p_b, cos_q, sin_q, kc, vc, ks, vs, kw, vw, msel)


def _nsa_branch(p_b, cmp_pe_k, cmp_w1_k, cmp_w2_k, cmp_pe_v, cmp_w1_v, cmp_w2_v):
    B, S, _ = p_b.shape
    pos = jnp.arange(S)
    cos2, sin2 = _rope_tables(pos, NSA_GROUPS)
    kc_raw, vc_raw, ks, vs, kw, vw = _kv_layout(p_b, cos2, sin2)
    n_sub = S // CMP_STRIDE
    sub = lambda t: t.reshape(B, NSA_GROUPS, n_sub, CMP_STRIDE * HEAD_DIM)
    cos_c, sin_c = _rope_tables(jnp.arange(n_sub) * CMP_STRIDE + CMP_BLOCK - 1, 1)
    kc, vc = _compress(sub(kc_raw), sub(vc_raw), cmp_pe_k, cmp_w1_k, cmp_w2_k,
                       cmp_pe_v, cmp_w1_v, cmp_w2_v, cos_c, sin_c)
    cos_q, sin_q = _rope_tables(pos, NSA_HPG)
    return _nsa_attention(p_b, kc, vc, ks, vs, kw, vw, cos_q, sin_q)


def _nsa_weight_columns(w_nsa):
    K = w_nsa.shape[0]
    q = w_nsa[:, :NSA_WIDTH]
    kv = w_nsa[:, NSA_WIDTH:NSA_WIDTH + 6 * NSA_KV_WIDTH]
    gates = w_nsa[:, NSA_WIDTH + 6 * NSA_KV_WIDTH:]
    per_group = NSA_HPG * 3
    gate_blocks = [jnp.pad(gates[:, g * per_group:(g + 1) * per_group], ((0, 0), (0, 128 - per_group)))
                   for g in range(NSA_GROUPS)]
    return jnp.concatenate([kv, q] + gate_blocks, axis=1)


def _layer_norm(h, g, b):
    mu = jnp.mean(h, axis=-1, keepdims=True)
    var = jnp.mean(jnp.square(h - mu), axis=-1, keepdims=True)
    return (h - mu) * lax.rsqrt(var + LN_EPS) * g + b


def _pack_bf16_halves(x):
    n = x.shape[-1] // 2
    bits = lax.bitcast_convert_type(x.astype(BF16).astype(F32), jnp.uint32)
    return (bits[:, n:] & jnp.uint32(0xFFFF0000)) | (bits[:, :n] >> 16)


def _unpack_bf16_halves(u):
    left = lax.bitcast_convert_type(u << 16, F32)
    right = lax.bitcast_convert_type(u & jnp.uint32(0xFFFF0000), F32)
    return left, right


def _mixer_out_kernel(x_ref, ya_ref, yb_ref, pg_ref, wa_ref, wb_ref, wo_ref, g_ref, b_ref, o_ref, op_ref,
                      *, alpha):
    d = x_ref.shape[-1]
    gate_a = _sigmoid(pg_ref[:, :d])
    gate_b = _sigmoid(pg_ref[:, d:])
    mixed = gate_a * _bdot(ya_ref[...], wa_ref[...]) + gate_b * _bdot(yb_ref[...], wb_ref[...])
    h = alpha * x_ref[...] + _bdot(mixed, wo_ref[...])
    out = _layer_norm(h, g_ref[...], b_ref[...])
    o_ref[...] = out
    op_ref[...] = _pack_bf16_halves(out)


def _mixer_out(xf, ya, yb, p_g, wa, wb, wo, ln_g, ln_b, alpha, tm=512):
    T, D = xf.shape
    rows = lambda w: pl.BlockSpec((tm, w), lambda i: (i, 0))
    full = lambda a: pl.BlockSpec(a.shape, lambda i: (0,) * a.ndim)
    ln_g, ln_b = ln_g.reshape(1, D), ln_b.reshape(1, D)
    return pl.pallas_call(
        functools.partial(_mixer_out_kernel, alpha=alpha),
        grid=(T // tm,),
        in_specs=[rows(D), rows(ya.shape[1]), rows(yb.shape[1]), rows(2 * D),
                  full(wa), full(wb), full(wo), full(ln_g), full(ln_b)],
        out_specs=[rows(D), rows(D // 2)],
        out_shape=[jax.ShapeDtypeStruct((T, D), F32), jax.ShapeDtypeStruct((T, D // 2), jnp.uint32)],
        compiler_params=pltpu.CompilerParams(
            dimension_semantics=("parallel",), vmem_limit_bytes=VMEM_LIMIT),
        name="mixer_out_ln",
    )(xf, ya, yb, p_g, wa, wb, wo, ln_g, ln_b)


ROUTER_TILE = 256
EXPERT_ROWS = 256
SC_TOKEN_CHUNK = 64
SC_ROW_CHUNK = 32
PICK_LANES = 128
LOWEST = -3.0e38


def _router_kernel(x_ref, rw_ref, bias_ref, eidx_ref, wts_ref, pos_ref, cnt_ref, carry_ref):
    tm, E = x_ref.shape[0], rw_ref.shape[1]
    per_group = E // N_GROUPS

    @pl.when(pl.program_id(0) == 0)
    def _():
        carry_ref[...] = jnp.zeros_like(carry_ref)

    scores = _sigmoid(_hdot(x_ref[...], rw_ref[...]))
    choice = scores + bias_ref[...]
    lane = lax.broadcasted_iota(jnp.int32, (tm, E), 1)
    grp = lane // per_group

    def first_max(vals):
        m = jnp.max(vals, axis=-1, keepdims=True)
        return m, jnp.min(jnp.where(vals == m, lane, E), axis=-1, keepdims=True)

    group_score = []
    for g in range(N_GROUPS):
        cg = jnp.where(grp == g, choice, LOWEST)
        m1, i1 = first_max(cg)
        m2 = jnp.max(jnp.where(lane == i1, LOWEST, cg), axis=-1, keepdims=True)
        group_score.append(m1 + m2)
    allowed = jnp.zeros((tm, E), jnp.bool_)
    for g in range(N_GROUPS):
        rank = jnp.zeros((tm, 1), jnp.int32)
        for o in range(N_GROUPS):
            if o != g:
                ahead = (group_score[o] > group_score[g]) if o > g else (group_score[o] >= group_score[g])
                rank = rank + ahead.astype(jnp.int32)
        allowed = allowed | ((grp == g) & (rank < TOPK_GROUPS))

    cur = jnp.where(allowed, choice, NEG_INF)
    sel = jnp.zeros((tm, E), F32)
    picks = []
    for _ in range(TOP_K):
        _, idx = first_max(cur)
        hit = lane == idx
        picks.append(idx)
        sel = jnp.where(hit, 1.0, sel)
        cur = jnp.where(hit, LOWEST, cur)
    gate = scores * sel
    gate = gate / jnp.sum(gate, axis=-1, keepdims=True) * ROUTED_SCALE

    ri = lax.broadcasted_iota(jnp.int32, (tm, tm), 0)
    ci = lax.broadcasted_iota(jnp.int32, (tm, tm), 1)
    before = jnp.dot((ri > ci).astype(BF16), sel.astype(BF16), preferred_element_type=F32)
    queue_pos = before + carry_ref[...]
    carry_ref[...] = carry_ref[...] + jnp.sum(sel, axis=0, keepdims=True)
    cnt_ref[...] = jnp.broadcast_to(carry_ref[...], cnt_ref.shape)

    out_lane = lax.broadcasted_iota(jnp.int32, (tm, PICK_LANES), 1)
    eidx = jnp.zeros((tm, PICK_LANES), jnp.int32)
    wts = jnp.zeros((tm, PICK_LANES), F32)
    pos = jnp.zeros((tm, PICK_LANES), F32)
    for kk, idx in enumerate(picks):
        hit = lane == idx
        eidx = jnp.where(out_lane == kk, idx, eidx)
        wts = jnp.where(out_lane == kk, jnp.sum(jnp.where(hit, gate, 0.0), axis=-1, keepdims=True), wts)
        pos = jnp.where(out_lane == kk, jnp.sum(jnp.where(hit, queue_pos, 0.0), axis=-1, keepdims=True), pos)
    eidx_ref[...] = eidx
    wts_ref[...] = wts
    pos_ref[...] = pos.astype(jnp.int32)


def _router(xf, router_w, router_bias):
    T, D = xf.shape
    E = router_w.shape[1]
    tm = ROUTER_TILE
    picks = lambda dt: jax.ShapeDtypeStruct((T, PICK_LANES), dt)
    pick_spec = pl.BlockSpec((tm, PICK_LANES), lambda i: (i, 0))
    return pl.pallas_call(
        _router_kernel,
        grid=(T // tm,),
        in_specs=[pl.BlockSpec((tm, D), lambda i: (i, 0)),
                  pl.BlockSpec((D, E), lambda i: (0, 0)),
                  pl.BlockSpec((1, E), lambda i: (0, 0))],
        out_specs=[pick_spec, pick_spec, pick_spec, pl.BlockSpec((8, E), lambda i: (0, 0))],
        out_shape=[picks(jnp.int32), picks(F32), picks(jnp.int32), jax.ShapeDtypeStruct((8, E), F32)],
        scratch_shapes=[pltpu.VMEM((1, E), F32)],
        compiler_params=pltpu.CompilerParams(
            dimension_semantics=("arbitrary",), vmem_limit_bytes=VMEM_LIMIT),
        name="moe_router",
    )(xf, router_w, router_bias.reshape(1, E))


def _dest_kernel(eidx_ref, pos_ref, start_ref, dest_ref):
    tm = eidx_ref.shape[0]
    E = start_ref.shape[1]
    lane = lax.broadcasted_iota(jnp.int32, (tm, E), 1)
    out_lane = lax.broadcasted_iota(jnp.int32, (tm, PICK_LANES), 1)
    eidx = eidx_ref[...]
    base = jnp.zeros((tm, PICK_LANES), jnp.int32)
    for kk in range(TOP_K):
        hit = lane == eidx[:, kk:kk + 1]
        start = jnp.sum(jnp.where(hit, start_ref[...], 0), axis=-1, keepdims=True)
        base = jnp.where(out_lane == kk, start, base)
    dest_ref[...] = base + pos_ref[...]


def _dest_rows(eidx, pos, pad_start):
    T = eidx.shape[0]
    E = pad_start.shape[0]
    tm = ROUTER_TILE
    pick_spec = pl.BlockSpec((tm, PICK_LANES), lambda i: (i, 0))
    return pl.pallas_call(
        _dest_kernel,
        grid=(T // tm,),
        in_specs=[pick_spec, pick_spec, pl.BlockSpec((1, E), lambda i: (0, 0))],
        out_specs=pick_spec,
        out_shape=jax.ShapeDtypeStruct((T, PICK_LANES), jnp.int32),
        compiler_params=pltpu.CompilerParams(
            dimension_semantics=("parallel",), vmem_limit_bytes=VMEM_LIMIT),
        name="moe_dest_rows",
    )(eidx, pos, pad_start.reshape(1, E))


def _sc_mesh():
    return plsc.VectorSubcoreMesh(core_axis_name="c", subcore_axis_name="s")


def _sc_scatter_rows(x, dest_t, n_rows):
    T, D = x.shape
    K = dest_t.shape[0]
    mesh = _sc_mesh()
    nc, nw = mesh.num_cores, mesh.num_cores * mesh.num_subcores
    per_w = T // nw
    chunk = min(SC_TOKEN_CHUNK, per_w)
    n_chunks = per_w // chunk
    idx = dest_t.reshape(K, nw, n_chunks, chunk).transpose(1, 2, 0, 3).reshape(nw, n_chunks * K, chunk)

    @functools.partial(
        pl.kernel, mesh=mesh,
        out_type=jax.ShapeDtypeStruct((n_rows, D), x.dtype),
        scratch_types=[pltpu.VMEM((n_chunks * K, chunk), jnp.int32),
                       pltpu.VMEM((chunk, D), x.dtype),
                       pltpu.SemaphoreType.DMA],
    )
    def scatter(x_hbm, idx_hbm, out_hbm, idx_v, rows_v, sem):
        wid = lax.axis_index("s") * nc + lax.axis_index("c")
        pltpu.sync_copy(idx_hbm.at[wid], idx_v)

        @pl.loop(0, n_chunks)
        def _(j):
            pltpu.sync_copy(x_hbm.at[pl.ds(wid * per_w + j * chunk, chunk)], rows_v)
            copies = [pltpu.async_copy(rows_v, out_hbm.at[idx_v.at[j * K + kk]], sem) for kk in range(K)]
            for c in copies:
                c.wait()

    return scatter(x, idx)


def _sc_gather_rows(src, idx):
    M = idx.shape[0]
    D = src.shape[1]
    mesh = _sc_mesh()
    nc, nw = mesh.num_cores, mesh.num_cores * mesh.num_subcores
    per_w = M // nw
    chunk = min(SC_ROW_CHUNK, per_w)
    n_chunks = per_w // chunk

    @functools.partial(
        pl.kernel, mesh=mesh,
        out_type=jax.ShapeDtypeStruct((M, D), src.dtype),
        scratch_types=[pltpu.VMEM((n_chunks, chunk), jnp.int32),
                       pltpu.VMEM((chunk, D), src.dtype),
                       pltpu.SemaphoreType.DMA],
    )
    def gather(src_hbm, idx_hbm, out_hbm, idx_v, rows_v, sem):
        wid = lax.axis_index("s") * nc + lax.axis_index("c")
        pltpu.sync_copy(idx_hbm.at[wid], idx_v)

        @pl.loop(0, n_chunks)
        def _(j):
            pltpu.async_copy(src_hbm.at[idx_v.at[j]], rows_v, sem).wait()
            pltpu.sync_copy(rows_v, out_hbm.at[pl.ds(wid * per_w + j * chunk, chunk)])

    return gather(src, idx.reshape(nw, n_chunks, chunk))


def _expert_kernel(blk_e_ref, blk_rows_ref, n_used_ref, x_ref, wgu_ref, wd_ref, o_ref):
    i = pl.program_id(0)
    live = i < n_used_ref[0]

    @pl.when(live)
    def _():
        hidden = wd_ref.shape[0]
        half = x_ref.shape[1]
        row = lax.broadcasted_iota(jnp.int32, x_ref.shape, 0)
        left, right = _unpack_bf16_halves(x_ref[...])
        real = row < blk_rows_ref[i]
        left = jnp.where(real, left, 0.0)
        right = jnp.where(real, right, 0.0)
        h = _bdot(left, wgu_ref[:half, :]) + _bdot(right, wgu_ref[half:, :])
        gate, up = h[:, :hidden], h[:, hidden:]
        act = gate * _sigmoid(gate) * up
        o_ref[...] = _bdot(act, wd_ref[...])

    @pl.when(jnp.logical_not(live))
    def _():
        o_ref[...] = jnp.zeros_like(o_ref)


def _expert_ffn(xs, blk_e, blk_rows, n_used, w_gu, w_down):
    n_rows, half = xs.shape
    E, D, two_h = w_gu.shape
    n_blocks = n_rows // EXPERT_ROWS
    live = lambda i, n_used: jnp.minimum(i, n_used[0] - 1)
    grid_spec = pltpu.PrefetchScalarGridSpec(
        num_scalar_prefetch=3,
        grid=(n_blocks,),
        in_specs=[pl.BlockSpec((EXPERT_ROWS, half), lambda i, be, br, nu: (live(i, nu), 0)),
                  pl.BlockSpec((None, D, two_h), lambda i, be, br, nu: (be[live(i, nu)], 0, 0)),
                  pl.BlockSpec((None, two_h // 2, D), lambda i, be, br, nu: (be[live(i, nu)], 0, 0))],
        out_specs=pl.BlockSpec((EXPERT_ROWS, D), lambda i, be, br, nu: (i, 0)),
    )
    return pl.pallas_call(
        _expert_kernel,
        grid_spec=grid_spec,
        out_shape=jax.ShapeDtypeStruct((n_rows, D), F32),
        compiler_params=pltpu.CompilerParams(
            dimension_semantics=("arbitrary",), vmem_limit_bytes=VMEM_LIMIT),
        name="moe_experts",
    )(blk_e, blk_rows, n_used, xs, w_gu, w_down)


def _moe_out_kernel(x_ref, yk_ref, wts_ref, sgu_ref, sd_ref, g_ref, b_ref, o_ref, *, alpha):
    x = x_ref[...]
    hidden = sd_ref.shape[0]
    h = _bdot(x, sgu_ref[...])
    gate, up = h[:, :hidden], h[:, hidden:]
    ffn = _bdot(gate * _sigmoid(gate) * up, sd_ref[...])
    wts = wts_ref[...]
    for kk in range(TOP_K):
        ffn = ffn + wts[:, kk:kk + 1] * yk_ref[kk]
    o_ref[...] = _layer_norm(alpha * x + ffn, g_ref[...], b_ref[...])


def _moe_out(xf, yk, wts, sw_gu, sw_down, ln_g, ln_b, alpha, tm=128):
    T, D = xf.shape
    rows = lambda w: pl.BlockSpec((tm, w), lambda i: (i, 0))
    full = lambda a: pl.BlockSpec(a.shape, lambda i: (0,) * a.ndim)
    ln_g, ln_b = ln_g.reshape(1, D), ln_b.reshape(1, D)
    return pl.pallas_call(
        functools.partial(_moe_out_kernel, alpha=alpha),
        grid=(T // tm,),
        in_specs=[rows(D), pl.BlockSpec((TOP_K, tm, D), lambda i: (0, i, 0)), rows(PICK_LANES),
                  full(sw_gu), full(sw_down), full(ln_g), full(ln_b)],
        out_specs=rows(D),
        out_shape=jax.ShapeDtypeStruct((T, D), F32),
        compiler_params=pltpu.CompilerParams(
            dimension_semantics=("parallel",), vmem_limit_bytes=VMEM_LIMIT),
        name="moe_combine_ln",
    )(xf, yk, wts, sw_gu, sw_down, ln_g, ln_b)


def _moe_ffn_ln(xf, xp, router_w, router_bias, w_gu, w_down, sw_gu, sw_down, ln_g, ln_b, alpha):
    T, D = xf.shape
    E = router_w.shape[1]
    BM = EXPERT_ROWS
    eidx, wts, pos, cnt = _router(xf, router_w, router_bias)
    counts = cnt[0].astype(jnp.int32)
    padded = (counts + BM - 1) // BM * BM
    pad_end = jnp.cumsum(padded)
    pad_start = pad_end - padded
    n_rows = T * TOP_K + E * BM
    n_blocks = n_rows // BM
    blk_row0 = jnp.arange(n_blocks, dtype=jnp.int32) * BM
    blk_e = jnp.minimum(jnp.sum((pad_end[None, :] <= blk_row0[:, None]).astype(jnp.int32), axis=1), E - 1)
    blk_rows = jnp.clip(pad_start[blk_e] + counts[blk_e] - blk_row0, 0, BM).astype(jnp.int32)
    n_used = (pad_end[-1:] // BM).astype(jnp.int32)
    dest = _dest_rows(eidx, pos, pad_start)[:, :TOP_K]
    dest_t = dest.T
    xs = _sc_scatter_rows(xp, dest_t, n_rows)
    ys = _expert_ffn(xs, blk_e, blk_rows, n_used, w_gu, w_down)
    yk = _sc_gather_rows(ys, dest_t.reshape(-1)).reshape(TOP_K, T, D)
    return _moe_out(xf, yk, wts, sw_gu, sw_down, ln_g, ln_b, alpha)


def kernel(x, w_in, tshift_mu, rwkv_w0, rwkv_w2, rwkv_a0, rwkv_a2, rwkv_g2, rwkv_k_k, rwkv_k_a, rwkv_r_k, rwkv_lnx_w, rwkv_lnx_b, cmp_pe_k, cmp_w1_k, cmp_w2_k, cmp_pe_v, cmp_w1_v, cmp_w2_v, w_branch_a, w_branch_b, w_out, ln1_g, ln1_b, router_w, router_bias, exp_w_gu, exp_w_down, shared_w_gu, shared_w_down, ln2_g, ln2_b):
    B, S, D = x.shape
    depth = w_in.shape[0]
    alpha = (2 * depth) ** 0.25
    nsa_w = w_in.shape[2] - RWKV_IN_W - 2 * D
    for l in range(depth):
        xf = x.reshape(B * S, D)
        w_l = w_in[l]
        w_a = w_l[:, :RWKV_IN_W].astype(BF16)
        w_b = _nsa_weight_columns(w_l[:, RWKV_IN_W:RWKV_IN_W + nsa_w]).astype(BF16)
        w_g = w_l[:, RWKV_IN_W + nsa_w:].astype(BF16)
        p_a = _matmul(xf, w_a, 512, RWKV_IN_W // 2).reshape(B, S, -1)
        p_b = _matmul(xf, w_b, 512, w_b.shape[1] // 2).reshape(B, S, -1)
        p_g = _matmul(xf, w_g, 512, D)
        y_a = _rwkv_time_mix(p_a, tshift_mu[l], rwkv_w0[l], rwkv_w2[l], rwkv_a0[l], rwkv_a2[l], rwkv_g2[l],
                             rwkv_k_k[l], rwkv_k_a[l], rwkv_r_k[l].reshape(-1), rwkv_lnx_w[l], rwkv_lnx_b[l])
        y_b = _nsa_branch(p_b, cmp_pe_k[l], cmp_w1_k[l], cmp_w2_k[l], cmp_pe_v[l], cmp_w1_v[l], cmp_w2_v[l])
        x1, x1p = _mixer_out(xf, y_a.reshape(B * S, -1), y_b.reshape(B * S, -1), p_g,
                             w_branch_a[l].astype(BF16), w_branch_b[l].astype(BF16), w_out[l].astype(BF16),
                             ln1_g[l], ln1_b[l], alpha)
        x2 = _moe_ffn_ln(x1, x1p, router_w[l], router_bias[l], exp_w_gu[l], exp_w_down[l],
                         shared_w_gu[l].astype(BF16), shared_w_down[l].astype(BF16), ln2_g[l], ln2_b[l], alpha)
        x = x2.reshape(B, S, D)
    return x
```

```python
import functools

import numpy as np
import jax
import jax.numpy as jnp
from jax import lax
from jax.experimental import pallas as pl
from jax.experimental.pallas import tpu as pltpu
from jax.experimental.pallas import tpu_sc as plsc

F32 = jnp.float32
BF16 = jnp.bfloat16
HIGHEST = lax.Precision.HIGHEST

RWKV_HEADS = 8
HEAD_DIM = 64
RWKV_WIDTH = RWKV_HEADS * HEAD_DIM
W_LORA = 64
A_LORA = 64
G_LORA = 128
GN_EPS = 64e-5
NSA_HEADS = 8
NSA_GROUPS = 2
NSA_HPG = NSA_HEADS // NSA_GROUPS
NSA_WIDTH = NSA_HEADS * HEAD_DIM
NSA_KV_WIDTH = NSA_GROUPS * HEAD_DIM
CMP_BLOCK = 32
CMP_STRIDE = 16
CMP_HIDDEN = 256
SEL_BLOCK = 64
N_SELECT = 16
WINDOW = 512
Q_BLOCK = 128
ROPE_THETA = 10000.0
RWKV_IN_W = 3 * RWKV_WIDTH + W_LORA + A_LORA + G_LORA
N_EXPERTS = 256
TOP_K = 8
N_GROUPS = 8
TOPK_GROUPS = 4
EXPERT_DIM = 256
ROUTED_SCALE = 2.5
LN_EPS = 1e-5
NEG_INF = -1e30
FORCE_BONUS = 1e4

RWKV_CHUNK = 64
RWKV_HEAD_GROUP = 4
VMEM_LIMIT = 56 * 1024 * 1024


def _bdot(a, b):
    return jnp.dot(a.astype(BF16), b.astype(BF16), preferred_element_type=F32)


def _bdot_nt(a, b):
    return lax.dot_general(a.astype(BF16), b.astype(BF16), (((1,), (1,)), ((), ())),
                           preferred_element_type=F32)


def _bdot_tn(a, b):
    return lax.dot_general(a.astype(BF16), b.astype(BF16), (((0,), (0,)), ((), ())),
                           preferred_element_type=F32)


def _hdot(a, b):
    return jnp.dot(a, b, precision=HIGHEST, preferred_element_type=F32)


def _sigmoid(x):
    return 1.0 / (1.0 + jnp.exp(-x))


def _matmul_kernel(x_ref, w_ref, o_ref):
    o_ref[...] = jnp.dot(x_ref[...].astype(BF16), w_ref[...], preferred_element_type=F32)


def _matmul(x, w, tm, tn):
    M, K = x.shape
    N = w.shape[1]
    return pl.pallas_call(
        _matmul_kernel,
        grid=(M // tm, N // tn),
        in_specs=[pl.BlockSpec((tm, K), lambda i, j: (i, 0)),
                  pl.BlockSpec((K, tn), lambda i, j: (0, j))],
        out_specs=pl.BlockSpec((tm, tn), lambda i, j: (i, j)),
        out_shape=jax.ShapeDtypeStruct((M, N), F32),
        compiler_params=pltpu.CompilerParams(
            dimension_semantics=("parallel", "parallel"), vmem_limit_bytes=VMEM_LIMIT),
        name="dense_proj",
    )(x, w)


def _rwkv_kernel(p_ref, mu_ref, w0_ref, w2_ref, a0_ref, a2_ref, g2_ref, kk_ref, ka_ref, rk_ref,
                 lnw_ref, lnb_ref, o_ref, carry_ref, state_ref):
    C, H, N = RWKV_CHUNK, RWKV_HEADS, HEAD_DIM
    W = RWKV_WIDTH
    B = p_ref.shape[0]
    R = B * C

    @pl.when(pl.program_id(0) == 0)
    def _():
        carry_ref[...] = jnp.zeros_like(carry_ref)
        state_ref[...] = jnp.zeros_like(state_ref)

    def per_batch(x):
        return jnp.concatenate([jnp.broadcast_to(x[b].reshape(1, -1), (C, x.shape[-1])) for b in range(B)],
                               axis=0)

    p = p_ref[...].reshape(R, p_ref.shape[-1])
    row = lax.broadcasted_iota(jnp.int32, p.shape, 0)
    prev = jnp.where(row % C == 0, per_batch(carry_ref[...]), pltpu.roll(p, 1, axis=0))
    for b in range(B):
        carry_ref[b] = p[b * C + C - 1:b * C + C, :]
    xs = p + (prev - p) * mu_ref[...]
    r = xs[:, 0:W]
    k = xs[:, W:2 * W]
    v = xs[:, 2 * W:3 * W]
    wl = xs[:, 3 * W:3 * W + W_LORA]
    al = xs[:, 3 * W + W_LORA:3 * W + W_LORA + A_LORA]
    gl = xs[:, 3 * W + W_LORA + A_LORA:]

    z = -(w0_ref[...] + _hdot(jnp.tanh(wl), w2_ref[...]))
    softplus = jnp.maximum(z, 0.0) + jnp.log1p(jnp.exp(-jnp.abs(z)))
    logd = -jnp.exp(-softplus - 0.5)
    a = _sigmoid(a0_ref[...] + _hdot(al, a2_ref[...]))
    g = _hdot(_sigmoid(gl), g2_ref[...])

    kk = k * kk_ref[...]
    knew = k * (1.0 + (a - 1.0) * ka_ref[...])

    def per_head(x, fn):
        return jnp.concatenate(
            [jnp.broadcast_to(fn(x[:, h * N:(h + 1) * N]), (R, N)) for h in range(H)], axis=-1)

    nrm = per_head(kk * kk, lambda t: jnp.sqrt(jnp.sum(t, axis=-1, keepdims=True)))
    kk = kk / jnp.maximum(nrm, 1e-12)
    lr_kk = kk * a

    ti = lax.broadcasted_iota(jnp.int32, (R, R), 0)
    tj = lax.broadcasted_iota(jnp.int32, (R, R), 1)
    same_chunk = (ti >= tj) & (ti // C == tj // C)
    cl = _hdot(same_chunk.astype(F32), logd)
    cl_end = per_batch(jnp.concatenate([cl[b * C + C - 1:b * C + C, :] for b in range(B)], axis=0))
    a_hat = -kk * jnp.exp(cl - logd)
    r_hat = r * jnp.exp(cl)
    inv_gam = jnp.exp(-cl)
    b_til = lr_kk * inv_gam
    k_til = knew * inv_gam
    to_end = jnp.exp(cl_end - cl)
    b_end = lr_kk * to_end
    k_end = knew * to_end
    gam_end = jnp.exp(cl_end)

    HG = RWKV_HEAD_GROUP
    GW = HG * N
    gt = lax.broadcasted_iota(jnp.int32, (C, GW), 0)
    gc = lax.broadcasted_iota(jnp.int32, (C, GW), 1) % N
    strict = gt > gc
    incl = gt >= gc
    eye = (gt == gc).astype(F32)
    bi = lax.broadcasted_iota(jnp.int32, (HG * C, GW), 0) // C
    bj = lax.broadcasted_iota(jnp.int32, (HG * C, GW), 1) // N
    same_head = bi == bj

    def block_diag(y):
        yb = y.astype(BF16)
        return jnp.where(same_head, jnp.concatenate([yb] * HG, axis=0), jnp.zeros((), BF16))

    def bd_dot(x, y):
        return jnp.dot(x.astype(BF16), block_diag(y), preferred_element_type=F32)

    def bd_dot_nt(x, y):
        return lax.dot_general(x.astype(BF16), block_diag(y), (((1,), (1,)), ((), ())),
                               preferred_element_type=F32)

    n_groups = H // HG
    units = [(b, gi) for b in range(B) for gi in range(n_groups)]
    n_units = range(len(units))
    cut = lambda x, b, gi: x[b * C:(b + 1) * C, gi * GW:(gi + 1) * GW]
    v_u = [cut(v, b, gi) for b, gi in units]
    ar = [jnp.concatenate([cut(a_hat, b, gi), cut(r_hat, b, gi)], axis=0) for b, gi in units]
    mb = [bd_dot_nt(ar[i], cut(b_til, *units[i])) for i in n_units]
    mk = [bd_dot_nt(ar[i], cut(k_til, *units[i])) for i in n_units]
    n_ab = [jnp.where(strict, mb[i][:C], 0.0) for i in n_units]
    m_rb = [jnp.where(incl, mb[i][C:], 0.0) for i in n_units]
    l_ak = [jnp.where(strict, mk[i][:C], 0.0) for i in n_units]
    m_rk = [jnp.where(incl, mk[i][C:], 0.0) for i in n_units]

    pw = list(n_ab)
    tinv = [eye + n_ab[i] for i in n_units]
    step = 2
    while step < C:
        pw = [bd_dot(pw[i], pw[i]) for i in n_units]
        tinv = [tinv[i] + bd_dot(tinv[i], pw[i]) for i in n_units]
        step *= 2

    s0 = [state_ref[i] for i in n_units]
    ars = [bd_dot_nt(ar[i], s0[i]) for i in n_units]
    lv = [bd_dot(l_ak[i], v_u[i]) for i in n_units]
    u = [bd_dot(tinv[i], ars[i][:C] + lv[i]) for i in n_units]
    outs = [ars[i][C:] + bd_dot(m_rb[i], u[i]) + bd_dot(m_rk[i], v_u[i]) for i in n_units]
    for i, (b, gi) in enumerate(units):
        uv = jnp.concatenate([u[i], v_u[i]], axis=0)
        bk_end = jnp.concatenate([cut(b_end, b, gi), cut(k_end, b, gi)], axis=0)
        cross = jnp.where(same_head, _bdot_tn(uv, bk_end), 0.0)
        upd = cross[0:N]
        for h in range(1, HG):
            upd = upd + cross[h * N:(h + 1) * N]
        state_ref[i] = s0[i] * gam_end[b * C:b * C + 1, gi * GW:(gi + 1) * GW] + upd

    o = jnp.concatenate([jnp.concatenate(outs[b * n_groups:(b + 1) * n_groups], axis=-1) for b in range(B)],
                        axis=0)
    mean = per_head(o, lambda t: jnp.mean(t, axis=-1, keepdims=True))
    var = per_head(jnp.square(o - mean), lambda t: jnp.mean(t, axis=-1, keepdims=True))
    o = (o - mean) * lax.rsqrt(var + GN_EPS) * lnw_ref[...] + lnb_ref[...]
    bonus = per_head(r * knew * rk_ref[...], lambda t: jnp.sum(t, axis=-1, keepdims=True)) * v
    o_ref[...] = ((o + bonus) * g).reshape(o_ref.shape)


def _hdot_nt(a, b):
    return lax.dot_general(a, b, (((1,), (1,)), ((), ())), precision=HIGHEST,
                           preferred_element_type=F32)


def _hdot_tn(a, b):
    return lax.dot_general(a, b, (((0,), (0,)), ((), ())), precision=HIGHEST,
                           preferred_element_type=F32)


def _rwkv_time_mix(p_a, mu, w0, w2, a0, a2, g2, k_k, k_a, r_k, lnx_w, lnx_b):
    B, S, _ = p_a.shape
    C = RWKV_CHUNK
    row = lambda t: t.reshape(1, -1)
    full = lambda shape: pl.BlockSpec(shape, lambda s: (0,) * len(shape))
    n_units = B * RWKV_HEADS // RWKV_HEAD_GROUP
    return pl.pallas_call(
        _rwkv_kernel,
        grid=(S // C,),
        in_specs=[pl.BlockSpec((B, C, RWKV_IN_W), lambda s: (0, s, 0)),
                  full((1, RWKV_IN_W)), full((1, RWKV_WIDTH)), full((W_LORA, RWKV_WIDTH)),
                  full((1, RWKV_WIDTH)), full((A_LORA, RWKV_WIDTH)), full((G_LORA, RWKV_WIDTH)),
                  full((1, RWKV_WIDTH)), full((1, RWKV_WIDTH)), full((1, RWKV_WIDTH)),
                  full((1, RWKV_WIDTH)), full((1, RWKV_WIDTH))],
        out_specs=pl.BlockSpec((B, C, RWKV_WIDTH), lambda s: (0, s, 0)),
        out_shape=jax.ShapeDtypeStruct((B, S, RWKV_WIDTH), F32),
        scratch_shapes=[pltpu.VMEM((B, 1, RWKV_IN_W), F32),
                        pltpu.VMEM((n_units, HEAD_DIM, RWKV_HEAD_GROUP * HEAD_DIM), F32)],
        compiler_params=pltpu.CompilerParams(
            dimension_semantics=("arbitrary",), vmem_limit_bytes=VMEM_LIMIT),
        name="rwkv7_chunked",
    )(p_a, row(mu), row(w0), w2, row(a0), a2, g2, row(k_k), row(k_a), row(r_k), row(lnx_w), row(lnx_b))


NSA_KV_TILE = 1024
SEL_KEY_TILE = 512
SEL_LANES = 128


def _rope_tables(pos, reps):
    half = HEAD_DIM // 2
    inv = ROPE_THETA ** (-jnp.arange(half, dtype=F32) / half)
    ang = pos.astype(F32)[:, None] * inv
    cos, sin = jnp.cos(ang), jnp.sin(ang)
    cosf = jnp.concatenate([cos, cos], -1)
    sinf = jnp.concatenate([-sin, sin], -1)
    return jnp.tile(cosf, (1, reps)), jnp.tile(sinf, (1, reps))


def _rope(x, cosf, sinf):
    width = x.shape[-1]
    lane = lax.broadcasted_iota(jnp.int32, x.shape, 1)
    first_half = (lane % HEAD_DIM) < HEAD_DIM // 2
    rot = jnp.where(first_half, pltpu.roll(x, width - HEAD_DIM // 2, axis=1),
                    pltpu.roll(x, HEAD_DIM // 2, axis=1))
    return x * cosf + rot * sinf


def _kv_layout_kernel(p_ref, cos_ref, sin_ref, kc_ref, vc_ref, ks_ref, vs_ref, kw_ref, vw_ref):
    outs = (kc_ref, vc_ref, ks_ref, vs_ref, kw_ref, vw_ref)
    roped = (False, False, True, False, True, False)
    for i, (o_ref, use_rope) in enumerate(zip(outs, roped)):
        t = p_ref[:, i * NSA_KV_WIDTH:(i + 1) * NSA_KV_WIDTH]
        if use_rope:
            t = _rope(t, cos_ref[...], sin_ref[...])
        for g in range(NSA_GROUPS):
            o_ref[g] = t[:, g * HEAD_DIM:(g + 1) * HEAD_DIM].astype(o_ref.dtype)


def _kv_layout(p_b, cos2, sin2):
    B, S, _ = p_b.shape
    ts = min(NSA_KV_TILE, S)
    out_spec = pl.BlockSpec((None, NSA_GROUPS, ts, HEAD_DIM), lambda b, s: (b, 0, s, 0))
    shp = lambda dt: jax.ShapeDtypeStruct((B, NSA_GROUPS, S, HEAD_DIM), dt)
    return pl.pallas_call(
        _kv_layout_kernel,
        grid=(B, S // ts),
        in_specs=[pl.BlockSpec((None, ts, 6 * NSA_KV_WIDTH), lambda b, s: (b, s, 0)),
                  pl.BlockSpec((ts, NSA_KV_WIDTH), lambda b, s: (s, 0)),
                  pl.BlockSpec((ts, NSA_KV_WIDTH), lambda b, s: (s, 0))],
        out_specs=[out_spec] * 6,
        out_shape=[shp(F32), shp(F32), shp(BF16), shp(BF16), shp(BF16), shp(BF16)],
        compiler_params=pltpu.CompilerParams(
            dimension_semantics=("parallel", "parallel"), vmem_limit_bytes=VMEM_LIMIT),
        name="nsa_kv_layout",
    )(p_b, cos2, sin2)


def _compress_kernel(subk_ref, subv_ref, pek_ref, w1k_ref, w2k_ref, pev_ref, w1v_ref, w2v_ref,
                     cos_ref, sin_ref, kc_ref, vc_ref):
    n_sub = subk_ref.shape[0]
    half = CMP_STRIDE * HEAD_DIM

    def mlp(sub_ref, pe_ref, w1_ref, w2_ref):
        sub = sub_ref[...]
        top = _bdot(sub, w1_ref[:half, :])
        bot = _bdot(sub, w1_ref[half:, :])
        bias = _bdot(jnp.broadcast_to(pe_ref[...], (8, 2 * half)), w1_ref[...])[0:1, :]
        h = top + pltpu.roll(bot, n_sub - 1, axis=0) + bias
        return _bdot(jax.nn.gelu(h), w2_ref[...])

    kc = mlp(subk_ref, pek_ref, w1k_ref, w2k_ref)
    rot = jnp.concatenate([kc[:, HEAD_DIM // 2:], kc[:, :HEAD_DIM // 2]], axis=-1)
    kc_ref[...] = (kc * cos_ref[...] + rot * sin_ref[...]).astype(kc_ref.dtype)
    vc_ref[...] = mlp(subv_ref, pev_ref, w1v_ref, w2v_ref).astype(vc_ref.dtype)


def _compress(subk, subv, pe_k, w1_k, w2_k, pe_v, w1_v, w2_v, cos_c, sin_c):
    B, G, n_sub, width = subk.shape
    sub_spec = pl.BlockSpec((None, None, n_sub, width), lambda b, g: (b, g, 0, 0))
    full = lambda a: pl.BlockSpec(a.shape, lambda b, g: (0,) * a.ndim)
    out_spec = pl.BlockSpec((None, None, n_sub, HEAD_DIM), lambda b, g: (b, g, 0, 0))
    pe_k, pe_v = pe_k.reshape(1, -1), pe_v.reshape(1, -1)
    args = (pe_k, w1_k, w2_k, pe_v, w1_v, w2_v, cos_c, sin_c)
    return pl.pallas_call(
        _compress_kernel,
        grid=(B, G),
        in_specs=[sub_spec, sub_spec] + [full(a) for a in args],
        out_specs=[out_spec, out_spec],
        out_shape=[jax.ShapeDtypeStruct((B, G, n_sub, HEAD_DIM), BF16)] * 2,
        compiler_params=pltpu.CompilerParams(
            dimension_semantics=("parallel", "parallel"), vmem_limit_bytes=VMEM_LIMIT),
        name="nsa_compress",
    )(subk, subv, *args)


MAX_FLOOR = -1e20


def _masked_exp_rows(s, bias):
    w = s.shape[-1]
    sm = (s.reshape(NSA_HPG, Q_BLOCK, w) + bias[None]).reshape(s.shape)
    m = jnp.maximum(jnp.max(sm, axis=-1, keepdims=True), MAX_FLOOR)
    e = jnp.exp(sm - m)
    den = jnp.sum(e, axis=-1, keepdims=True)
    return e, 1.0 / jnp.where(den > 0.0, den, 1.0)


def _nsa_kernel(q_ref, gate_ref, cos_ref, sin_ref, kc_ref, vc_ref, ks_ref, vs_ref, kw_ref, vw_ref,
                mselt_ref, o_ref, *, n_pick):
    QB, HP, D = Q_BLOCK, NSA_HPG, HEAD_DIM
    qb = pl.program_id(2)
    n_cmp = kc_ref.shape[0]
    scale = D ** -0.5

    q = _rope(q_ref[...], cos_ref[...], sin_ref[...]) * scale
    q4 = jnp.concatenate([q[:, n * D:(n + 1) * D] for n in range(HP)], axis=0).astype(BF16)
    t_col = qb * QB + lax.broadcasted_iota(jnp.int32, (QB, 1), 0)
    t_row = qb * QB + lax.broadcasted_iota(jnp.int32, (1, QB), 1)

    cmp_end = lax.broadcasted_iota(jnp.int32, (1, n_cmp), 1) * CMP_STRIDE + (CMP_BLOCK - 1)
    e_c, inv_c = _masked_exp_rows(_bdot_nt(q4, kc_ref[...]), jnp.where(cmp_end <= t_col, 0.0, NEG_INF))
    p_c = e_c * inv_c
    o_c = _bdot(p_c, vc_ref[...])
    p_sum = p_c[0:QB]
    for n in range(1, HP):
        p_sum = p_sum + p_c[n * QB:(n + 1) * QB]
    p_hi = p_sum.astype(BF16)
    p_lo = (p_sum - p_hi.astype(F32)).astype(BF16)
    imp_t = (lax.dot_general(mselt_ref[...], p_hi, (((1,), (1,)), ((), ())), preferred_element_type=F32)
             + lax.dot_general(mselt_ref[...], p_lo, (((1,), (1,)), ((), ())), preferred_element_type=F32))

    j = lax.broadcasted_iota(jnp.int32, (SEL_LANES, QB), 0)
    cur = t_row // SEL_BLOCK
    valid = j * SEL_BLOCK <= t_row
    forced = (j == 0) | (j == cur) | (j == cur - 1)
    score = jnp.where(valid, imp_t + jnp.where(forced, FORCE_BONUS, 0.0), -1.0)
    sel_t = jnp.zeros((SEL_LANES, QB), F32)
    for _ in range(n_pick):
        m = jnp.max(score, axis=0, keepdims=True)
        idx = jnp.min(jnp.where(score == m, j, SEL_LANES), axis=0, keepdims=True)
        hit = j == idx
        sel_t = jnp.where(hit & (m >= 0.0), 1.0, sel_t)
        score = jnp.where(hit, -2.0, score)
    sel_b = sel_t.T.astype(BF16)

    KT = SEL_KEY_TILE
    blocks_per_tile = KT // SEL_BLOCK
    n_tiles = (qb * QB + QB + KT - 1) // KT

    def sel_step(kt, carry):
        m_i, l_i, acc = carry
        start = pl.multiple_of(kt * KT, KT)
        k_t = ks_ref[pl.ds(start, KT), :]
        v_t = vs_ref[pl.ds(start, KT), :]
        jrow = lax.broadcasted_iota(jnp.int32, (SEL_LANES, KT), 0)
        ccol = lax.broadcasted_iota(jnp.int32, (SEL_LANES, KT), 1)
        expand = (jrow == kt * blocks_per_tile + ccol // SEL_BLOCK).astype(BF16)
        picked = jnp.dot(sel_b, expand, preferred_element_type=F32)
        kpos = start + lax.broadcasted_iota(jnp.int32, (1, KT), 1)
        bias = jnp.where((picked > 0.5) & (kpos <= t_col), 0.0, NEG_INF)
        s = _bdot_nt(q4, k_t)
        sm = (s.reshape(HP, QB, KT) + bias[None]).reshape(s.shape)
        m_new = jnp.maximum(m_i, jnp.max(sm, axis=-1, keepdims=True))
        e = jnp.exp(sm - m_new)
        alpha = jnp.exp(m_i - m_new)
        l_new = alpha * l_i + jnp.sum(e, axis=-1, keepdims=True)
        acc_new = alpha * acc + _bdot(e, v_t)
        return m_new, l_new, acc_new

    init = (jnp.full((HP * QB, 1), MAX_FLOOR, F32), jnp.zeros((HP * QB, 1), F32),
            jnp.zeros((HP * QB, D), F32))
    _, l_s, acc_s = lax.fori_loop(0, n_tiles, sel_step, init)
    o_s = acc_s * (1.0 / jnp.where(l_s > 0.0, l_s, 1.0))

    span = WINDOW + QB
    w_start = pl.multiple_of(jnp.maximum(qb * QB - WINDOW, 0), QB)
    dist = t_col - (w_start + lax.broadcasted_iota(jnp.int32, (1, span), 1))
    e_w, inv_w = _masked_exp_rows(_bdot_nt(q4, kw_ref[pl.ds(w_start, span), :]),
                                  jnp.where((dist >= 0) & (dist < WINDOW), 0.0, NEG_INF))
    o_w = _bdot(e_w, vw_ref[pl.ds(w_start, span), :]) * inv_w

    gates = _sigmoid(gate_ref[...])
    for n in range(HP):
        rows = slice(n * QB, (n + 1) * QB)
        o_ref[:, n * D:(n + 1) * D] = (gates[:, 3 * n:3 * n + 1] * o_c[rows]
                                       + gates[:, 3 * n + 1:3 * n + 2] * o_s[rows]
                                       + gates[:, 3 * n + 2:3 * n + 3] * o_w[rows])


def _cmp_to_sel_matrix(n_cmp_rows, n_sel):
    ratio = SEL_BLOCK // CMP_STRIDE
    ci = np.arange(n_cmp_rows)[:, None]
    sj = np.arange(SEL_LANES)[None, :]
    m = sum(((ci + n) // ratio == sj).astype(np.float32) for n in range(CMP_BLOCK // CMP_STRIDE))
    m = m * (sj < n_sel) * (ci < n_cmp_rows - 1)
    return jnp.asarray(m.T, BF16)


def _nsa_attention(p_b, kc, vc, ks, vs, kw, vw, cos_q, sin_q):
    B, S, _ = p_b.shape
    n_sub = kc.shape[2]
    n_sel = S // SEL_BLOCK
    gw = NSA_HPG * HEAD_DIM
    q_col0 = 6 * NSA_KV_WIDTH // gw
    gate_col0 = (6 * NSA_KV_WIDTH + NSA_WIDTH) // 128
    msel = _cmp_to_sel_matrix(n_sub, n_sel)
    cmp_spec = pl.BlockSpec((None, None, n_sub, HEAD_DIM), lambda b, g, i: (b, g, 0, 0))
    kv_spec = pl.BlockSpec((None, None, S, HEAD_DIM), lambda b, g, i: (b, g, 0, 0))
    return pl.pallas_call(
        functools.partial(_nsa_kernel, n_pick=min(N_SELECT, n_sel)),
        grid=(B, NSA_GROUPS, S // Q_BLOCK),
        in_specs=[pl.BlockSpec((None, Q_BLOCK, gw), lambda b, g, i: (b, i, q_col0 + g)),
                  pl.BlockSpec((None, Q_BLOCK, 128), lambda b, g, i: (b, i, gate_col0 + g)),
                  pl.BlockSpec((Q_BLOCK, gw), lambda b, g, i: (i, 0)),
                  pl.BlockSpec((Q_BLOCK, gw), lambda b, g, i: (i, 0)),
                  cmp_spec, cmp_spec, kv_spec, kv_spec, kv_spec, kv_spec,
                  pl.BlockSpec(msel.shape, lambda b, g, i: (0, 0))],
        out_specs=pl.BlockSpec((None, Q_BLOCK, gw), lambda b, g, i: (b, i, g)),
        out_shape=jax.ShapeDtypeStruct((B, S, NSA_WIDTH), F32),
        compiler_params=pltpu.CompilerParams(
            dimension_semantics=("parallel", "parallel", "arbitrary"), vmem_limit_bytes=VMEM_LIMIT),
        name="nsa_attention",
    )(p_b, p_b, cos_q, sin_q, kc, vc, ks, vs, kw, vw, msel)


def _nsa_branch(p_b, cmp_pe_k, cmp_w1_k, cmp_w2_k, cmp_pe_v, cmp_w1_v, cmp_w2_v):
    B, S, _ = p_b.shape
    pos = jnp.arange(S)
    cos2, sin2 = _rope_tables(pos, NSA_GROUPS)
    kc_raw, vc_raw, ks, vs, kw, vw = _kv_layout(p_b, cos2, sin2)
    n_sub = S // CMP_STRIDE
    sub = lambda t: t.reshape(B, NSA_GROUPS, n_sub, CMP_STRIDE * HEAD_DIM)
    cos_c, sin_c = _rope_tables(jnp.arange(n_sub) * CMP_STRIDE + CMP_BLOCK - 1, 1)
    kc, vc = _compress(sub(kc_raw), sub(vc_raw), cmp_pe_k, cmp_w1_k, cmp_w2_k,
                       cmp_pe_v, cmp_w1_v, cmp_w2_v, cos_c, sin_c)
    cos_q, sin_q = _rope_tables(pos, NSA_HPG)
    return _nsa_attention(p_b, kc, vc, ks, vs, kw, vw, cos_q, sin_q)


def _nsa_weight_columns(w_nsa):
    K = w_nsa.shape[0]
    q = w_nsa[:, :NSA_WIDTH]
    kv = w_nsa[:, NSA_WIDTH:NSA_WIDTH + 6 * NSA_KV_WIDTH]
    gates = w_nsa[:, NSA_WIDTH + 6 * NSA_KV_WIDTH:]
    per_group = NSA_HPG * 3
    gate_blocks = [jnp.pad(gates[:, g * per_group:(g + 1) * per_group], ((0, 0), (0, 128 - per_group)))
                   for g in range(NSA_GROUPS)]
    return jnp.concatenate([kv, q] + gate_blocks, axis=1)


def _layer_norm(h, g, b):
    mu = jnp.mean(h, axis=-1, keepdims=True)
    var = jnp.mean(jnp.square(h - mu), axis=-1, keepdims=True)
    return (h - mu) * lax.rsqrt(var + LN_EPS) * g + b


def _pack_bf16_halves(x):
    n = x.shape[-1] // 2
    bits = lax.bitcast_convert_type(x.astype(BF16).astype(F32), jnp.uint32)
    return (bits[:, n:] & jnp.uint32(0xFFFF0000)) | (bits[:, :n] >> 16)


def _unpack_bf16_halves(u):
    left = lax.bitcast_convert_type(u << 16, F32)
    right = lax.bitcast_convert_type(u & jnp.uint32(0xFFFF0000), F32)
    return left, right


def _mixer_out_kernel(x_ref, ya_ref, yb_ref, pg_ref, wa_ref, wb_ref, wo_ref, g_ref, b_ref, o_ref, op_ref,
                      *, alpha):
    d = x_ref.shape[-1]
    gate_a = _sigmoid(pg_ref[:, :d])
    gate_b = _sigmoid(pg_ref[:, d:])
    mixed = gate_a * _bdot(ya_ref[...], wa_ref[...]) + gate_b * _bdot(yb_ref[...], wb_ref[...])
    h = alpha * x_ref[...] + _bdot(mixed, wo_ref[...])
    out = _layer_norm(h, g_ref[...], b_ref[...])
    o_ref[...] = out
    op_ref[...] = _pack_bf16_halves(out)


def _mixer_out(xf, ya, yb, p_g, wa, wb, wo, ln_g, ln_b, alpha, tm=512):
    T, D = xf.shape
    rows = lambda w: pl.BlockSpec((tm, w), lambda i: (i, 0))
    full = lambda a: pl.BlockSpec(a.shape, lambda i: (0,) * a.ndim)
    ln_g, ln_b = ln_g.reshape(1, D), ln_b.reshape(1, D)
    return pl.pallas_call(
        functools.partial(_mixer_out_kernel, alpha=alpha),
        grid=(T // tm,),
        in_specs=[rows(D), rows(ya.shape[1]), rows(yb.shape[1]), rows(2 * D),
                  full(wa), full(wb), full(wo), full(ln_g), full(ln_b)],
        out_specs=[rows(D), rows(D // 2)],
        out_shape=[jax.ShapeDtypeStruct((T, D), F32), jax.ShapeDtypeStruct((T, D // 2), jnp.uint32)],
        compiler_params=pltpu.CompilerParams(
            dimension_semantics=("parallel",), vmem_limit_bytes=VMEM_LIMIT),
        name="mixer_out_ln",
    )(xf, ya, yb, p_g, wa, wb, wo, ln_g, ln_b)


ROUTER_TILE = 256
EXPERT_ROWS = 256
SC_TOKEN_CHUNK = 64
SC_ROW_CHUNK = 32
PICK_LANES = 128
LOWEST = -3.0e38


def _router_kernel(x_ref, rw_ref, bias_ref, eidx_ref, wts_ref, pos_ref, cnt_ref, carry_ref):
    tm, E = x_ref.shape[0], rw_ref.shape[1]
    per_group = E // N_GROUPS

    @pl.when(pl.program_id(0) == 0)
    def _():
        carry_ref[...] = jnp.zeros_like(carry_ref)

    scores = _sigmoid(_hdot(x_ref[...], rw_ref[...]))
    choice = scores + bias_ref[...]
    lane = lax.broadcasted_iota(jnp.int32, (tm, E), 1)
    grp = lane // per_group

    def first_max(vals):
        m = jnp.max(vals, axis=-1, keepdims=True)
        return m, jnp.min(jnp.where(vals == m, lane, E), axis=-1, keepdims=True)

    group_score = []
    for g in range(N_GROUPS):
        cg = jnp.where(grp == g, choice, LOWEST)
        m1, i1 = first_max(cg)
        m2 = jnp.max(jnp.where(lane == i1, LOWEST, cg), axis=-1, keepdims=True)
        group_score.append(m1 + m2)
    allowed = jnp.zeros((tm, E), jnp.bool_)
    for g in range(N_GROUPS):
        rank = jnp.zeros((tm, 1), jnp.int32)
        for o in range(N_GROUPS):
            if o != g:
                ahead = (group_score[o] > group_score[g]) if o > g else (group_score[o] >= group_score[g])
                rank = rank + ahead.astype(jnp.int32)
        allowed = allowed | ((grp == g) & (rank < TOPK_GROUPS))

    cur = jnp.where(allowed, choice, NEG_INF)
    sel = jnp.zeros((tm, E), F32)
    picks = []
    for _ in range(TOP_K):
        _, idx = first_max(cur)
        hit = lane == idx
        picks.append(idx)
        sel = jnp.where(hit, 1.0, sel)
        cur = jnp.where(hit, LOWEST, cur)
    gate = scores * sel
    gate = gate / jnp.sum(gate, axis=-1, keepdims=True) * ROUTED_SCALE

    ri = lax.broadcasted_iota(jnp.int32, (tm, tm), 0)
    ci = lax.broadcasted_iota(jnp.int32, (tm, tm), 1)
    before = jnp.dot((ri > ci).astype(BF16), sel.astype(BF16), preferred_element_type=F32)
    queue_pos = before + carry_ref[...]
    carry_ref[...] = carry_ref[...] + jnp.sum(sel, axis=0, keepdims=True)
    cnt_ref[...] = jnp.broadcast_to(carry_ref[...], cnt_ref.shape)

    out_lane = lax.broadcasted_iota(jnp.int32, (tm, PICK_LANES), 1)
    eidx = jnp.zeros((tm, PICK_LANES), jnp.int32)
    wts = jnp.zeros((tm, PICK_LANES), F32)
    pos = jnp.zeros((tm, PICK_LANES), F32)
    for kk, idx in enumerate(picks):
        hit = lane == idx
        eidx = jnp.where(out_lane == kk, idx, eidx)
        wts = jnp.where(out_lane == kk, jnp.sum(jnp.where(hit, gate, 0.0), axis=-1, keepdims=True), wts)
        pos = jnp.where(out_lane == kk, jnp.sum(jnp.where(hit, queue_pos, 0.0), axis=-1, keepdims=True), pos)
    eidx_ref[...] = eidx
    wts_ref[...] = wts
    pos_ref[...] = pos.astype(jnp.int32)


def _router(xf, router_w, router_bias):
    T, D = xf.shape
    E = router_w.shape[1]
    tm = ROUTER_TILE
    picks = lambda dt: jax.ShapeDtypeStruct((T, PICK_LANES), dt)
    pick_spec = pl.BlockSpec((tm, PICK_LANES), lambda i: (i, 0))
    return pl.pallas_call(
        _router_kernel,
        grid=(T // tm,),
        in_specs=[pl.BlockSpec((tm, D), lambda i: (i, 0)),
                  pl.BlockSpec((D, E), lambda i: (0, 0)),
                  pl.BlockSpec((1, E), lambda i: (0, 0))],
        out_specs=[pick_spec, pick_spec, pick_spec, pl.BlockSpec((8, E), lambda i: (0, 0))],
        out_shape=[picks(jnp.int32), picks(F32), picks(jnp.int32), jax.ShapeDtypeStruct((8, E), F32)],
        scratch_shapes=[pltpu.VMEM((1, E), F32)],
        compiler_params=pltpu.CompilerParams(
            dimension_semantics=("arbitrary",), vmem_limit_bytes=VMEM_LIMIT),
        name="moe_router",
    )(xf, router_w, router_bias.reshape(1, E))


def _dest_kernel(eidx_ref, pos_ref, start_ref, dest_ref):
    tm = eidx_ref.shape[0]
    E = start_ref.shape[1]
    lane = lax.broadcasted_iota(jnp.int32, (tm, E), 1)
    out_lane = lax.broadcasted_iota(jnp.int32, (tm, PICK_LANES), 1)
    eidx = eidx_ref[...]
    base = jnp.zeros((tm, PICK_LANES), jnp.int32)
    for kk in range(TOP_K):
        hit = lane == eidx[:, kk:kk + 1]
        start = jnp.sum(jnp.where(hit, start_ref[...], 0), axis=-1, keepdims=True)
        base = jnp.where(out_lane == kk, start, base)
    dest_ref[...] = base + pos_ref[...]


def _dest_rows(eidx, pos, pad_start):
    T = eidx.shape[0]
    E = pad_start.shape[0]
    tm = ROUTER_TILE
    pick_spec = pl.BlockSpec((tm, PICK_LANES), lambda i: (i, 0))
    return pl.pallas_call(
        _dest_kernel,
        grid=(T // tm,),
        in_specs=[pick_spec, pick_spec, pl.BlockSpec((1, E), lambda i: (0, 0))],
        out_specs=pick_spec,
        out_shape=jax.ShapeDtypeStruct((T, PICK_LANES), jnp.int32),
        compiler_params=pltpu.CompilerParams(
            dimension_semantics=("parallel",), vmem_limit_bytes=VMEM_LIMIT),
        name="moe_dest_rows",
    )(eidx, pos, pad_start.reshape(1, E))


def _sc_mesh():
    return plsc.VectorSubcoreMesh(core_axis_name="c", subcore_axis_name="s")


def _sc_scatter_rows(x, dest_t, n_rows):
    T, D = x.shape
    K = dest_t.shape[0]
    mesh = _sc_mesh()
    nc, nw = mesh.num_cores, mesh.num_cores * mesh.num_subcores
    per_w = T // nw
    chunk = min(SC_TOKEN_CHUNK, per_w)
    n_chunks = per_w // chunk
    idx = dest_t.reshape(K, nw, n_chunks, chunk).transpose(1, 2, 0, 3).reshape(nw, n_chunks * K, chunk)

    @functools.partial(
        pl.kernel, mesh=mesh,
        out_type=jax.ShapeDtypeStruct((n_rows, D), x.dtype),
        scratch_types=[pltpu.VMEM((n_chunks * K, chunk), jnp.int32),
                       pltpu.VMEM((chunk, D), x.dtype),
                       pltpu.SemaphoreType.DMA],
    )
    def scatter(x_hbm, idx_hbm, out_hbm, idx_v, rows_v, sem):
        wid = lax.axis_index("s") * nc + lax.axis_index("c")
        pltpu.sync_copy(idx_hbm.at[wid], idx_v)

        @pl.loop(0, n_chunks)
        def _(j):
            pltpu.sync_copy(x_hbm.at[pl.ds(wid * per_w + j * chunk, chunk)], rows_v)
            copies = [pltpu.async_copy(rows_v, out_hbm.at[idx_v.at[j * K + kk]], sem) for kk in range(K)]
            for c in copies:
                c.wait()

    return scatter(x, idx)


def _sc_gather_rows(src, idx):
    M = idx.shape[0]
    D = src.shape[1]
    mesh = _sc_mesh()
    nc, nw = mesh.num_cores, mesh.num_cores * mesh.num_subcores
    per_w = M // nw
    chunk = min(SC_ROW_CHUNK, per_w)
    n_chunks = per_w // chunk

    @functools.partial(
        pl.kernel, mesh=mesh,
        out_type=jax.ShapeDtypeStruct((M, D), src.dtype),
        scratch_types=[pltpu.VMEM((n_chunks, chunk), jnp.int32),
                       pltpu.VMEM((chunk, D), src.dtype),
                       pltpu.SemaphoreType.DMA],
    )
    def gather(src_hbm, idx_hbm, out_hbm, idx_v, rows_v, sem):
        wid = lax.axis_index("s") * nc + lax.axis_index("c")
        pltpu.sync_copy(idx_hbm.at[wid], idx_v)

        @pl.loop(0, n_chunks)
        def _(j):
            pltpu.async_copy(src_hbm.at[idx_v.at[j]], rows_v, sem).wait()
            pltpu.sync_copy(rows_v, out_hbm.at[pl.ds(wid * per_w + j * chunk, chunk)])

    return gather(src, idx.reshape(nw, n_chunks, chunk))


def _expert_kernel(blk_e_ref, blk_rows_ref, n_used_ref, x_ref, wgu_ref, wd_ref, o_ref):
    i = pl.program_id(0)
    live = i < n_used_ref[0]

    @pl.when(live)
    def _():
        hidden = wd_ref.shape[0]
        half = x_ref.shape[1]
        row = lax.broadcasted_iota(jnp.int32, x_ref.shape, 0)
        left, right = _unpack_bf16_halves(x_ref[...])
        real = row < blk_rows_ref[i]
        left = jnp.where(real, left, 0.0)
        right = jnp.where(real, right, 0.0)
        h = _bdot(left, wgu_ref[:half, :]) + _bdot(right, wgu_ref[half:, :])
        gate, up = h[:, :hidden], h[:, hidden:]
        act = gate * _sigmoid(gate) * up
        o_ref[...] = _bdot(act, wd_ref[...])

    @pl.when(jnp.logical_not(live))
    def _():
        o_ref[...] = jnp.zeros_like(o_ref)


def _expert_ffn(xs, blk_e, blk_rows, n_used, w_gu, w_down):
    n_rows, half = xs.shape
    E, D, two_h = w_gu.shape
    n_blocks = n_rows // EXPERT_ROWS
    live = lambda i, n_used: jnp.minimum(i, n_used[0] - 1)
    grid_spec = pltpu.PrefetchScalarGridSpec(
        num_scalar_prefetch=3,
        grid=(n_blocks,),
        in_specs=[pl.BlockSpec((EXPERT_ROWS, half), lambda i, be, br, nu: (live(i, nu), 0)),
                  pl.BlockSpec((None, D, two_h), lambda i, be, br, nu: (be[live(i, nu)], 0, 0)),
                  pl.BlockSpec((None, two_h // 2, D), lambda i, be, br, nu: (be[live(i, nu)], 0, 0))],
        out_specs=pl.BlockSpec((EXPERT_ROWS, D), lambda i, be, br, nu: (i, 0)),
    )
    return pl.pallas_call(
        _expert_kernel,
        grid_spec=grid_spec,
        out_shape=jax.ShapeDtypeStruct((n_rows, D), F32),
        compiler_params=pltpu.CompilerParams(
            dimension_semantics=("arbitrary",), vmem_limit_bytes=VMEM_LIMIT),
        name="moe_experts",
    )(blk_e, blk_rows, n_used, xs, w_gu, w_down)


def _moe_out_kernel(x_ref, yk_ref, wts_ref, sgu_ref, sd_ref, g_ref, b_ref, o_ref, *, alpha):
    x = x_ref[...]
    hidden = sd_ref.shape[0]
    h = _bdot(x, sgu_ref[...])
    gate, up = h[:, :hidden], h[:, hidden:]
    ffn = _bdot(gate * _sigmoid(gate) * up, sd_ref[...])
    wts = wts_ref[...]
    for kk in range(TOP_K):
        ffn = ffn + wts[:, kk:kk + 1] * yk_ref[kk]
    o_ref[...] = _layer_norm(alpha * x + ffn, g_ref[...], b_ref[...])


def _moe_out(xf, yk, wts, sw_gu, sw_down, ln_g, ln_b, alpha, tm=128):
    T, D = xf.shape
    rows = lambda w: pl.BlockSpec((tm, w), lambda i: (i, 0))
    full = lambda a: pl.BlockSpec(a.shape, lambda i: (0,) * a.ndim)
    ln_g, ln_b = ln_g.reshape(1, D), ln_b.reshape(1, D)
    return pl.pallas_call(
        functools.partial(_moe_out_kernel, alpha=alpha),
        grid=(T // tm,),
        in_specs=[rows(D), pl.BlockSpec((TOP_K, tm, D), lambda i: (0, i, 0)), rows(PICK_LANES),
                  full(sw_gu), full(sw_down), full(ln_g), full(ln_b)],
        out_specs=rows(D),
        out_shape=jax.ShapeDtypeStruct((T, D), F32),
        compiler_params=pltpu.CompilerParams(
            dimension_semantics=("parallel",), vmem_limit_bytes=VMEM_LIMIT),
        name="moe_combine_ln",
    )(xf, yk, wts, sw_gu, sw_down, ln_g, ln_b)


def _moe_ffn_ln(xf, xp, router_w, router_bias, w_gu, w_down, sw_gu, sw_down, ln_g, ln_b, alpha):
    T, D = xf.shape
    E = router_w.shape[1]
    BM = EXPERT_ROWS
    eidx, wts, pos, cnt = _router(xf, router_w, router_bias)
    counts = cnt[0].astype(jnp.int32)
    padded = (counts + BM - 1) // BM * BM
    pad_end = jnp.cumsum(padded)
    pad_start = pad_end - padded
    n_rows = T * TOP_K + E * BM
    n_blocks = n_rows // BM
    blk_row0 = jnp.arange(n_blocks, dtype=jnp.int32) * BM
    blk_e = jnp.minimum(jnp.sum((pad_end[None, :] <= blk_row0[:, None]).astype(jnp.int32), axis=1), E - 1)
    blk_rows = jnp.clip(pad_start[blk_e] + counts[blk_e] - blk_row0, 0, BM).astype(jnp.int32)
    n_used = (pad_end[-1:] // BM).astype(jnp.int32)
    dest = _dest_rows(eidx, pos, pad_start)[:, :TOP_K]
    dest_t = dest.T
    xs = _sc_scatter_rows(xp, dest_t, n_rows)
    ys = _expert_ffn(xs, blk_e, blk_rows, n_used, w_gu, w_down)
    yk = _sc_gather_rows(ys, dest_t.reshape(-1)).reshape(TOP_K, T, D)
    return _moe_out(xf, yk, wts, sw_gu, sw_down, ln_g, ln_b, alpha)


def kernel(x, w_in, tshift_mu, rwkv_w0, rwkv_w2, rwkv_a0, rwkv_a2, rwkv_g2, rwkv_k_k, rwkv_k_a, rwkv_r_k, rwkv_lnx_w, rwkv_lnx_b, cmp_pe_k, cmp_w1_k, cmp_w2_k, cmp_pe_v, cmp_w1_v, cmp_w2_v, w_branch_a, w_branch_b, w_out, ln1_g, ln1_b, router_w, router_bias, exp_w_gu, exp_w_down, shared_w_gu, shared_w_down, ln2_g, ln2_b):
    B, S, D = x.shape
    depth = w_in.shape[0]
    alpha = (2 * depth) ** 0.25
    nsa_w = w_in.shape[2] - RWKV_IN_W - 2 * D
    for l in range(depth):
        xf = x.reshape(B * S, D)
        w_l = w_in[l]
        w_a = w_l[:, :RWKV_IN_W].astype(BF16)
        w_b = _nsa_weight_columns(w_l[:, RWKV_IN_W:RWKV_IN_W + nsa_w]).astype(BF16)
        w_g = w_l[:, RWKV_IN_W + nsa_w:].astype(BF16)
        p_a = _matmul(xf, w_a, 512, RWKV_IN_W // 2).reshape(B, S, -1)
        p_b = _matmul(xf, w_b, 512, w_b.shape[1] // 2).reshape(B, S, -1)
        p_g = _matmul(xf, w_g, 512, D)
        y_a = _rwkv_time_mix(p_a, tshift_mu[l], rwkv_w0[l], rwkv_w2[l], rwkv_a0[l], rwkv_a2[l], rwkv_g2[l],
                             rwkv_k_k[l], rwkv_k_a[l], rwkv_r_k[l].reshape(-1), rwkv_lnx_w[l], rwkv_lnx_b[l])
        y_b = _nsa_branch(p_b, cmp_pe_k[l], cmp_w1_k[l], cmp_w2_k[l], cmp_pe_v[l], cmp_w1_v[l], cmp_w2_v[l])
        x1, x1p = _mixer_out(xf, y_a.reshape(B * S, -1), y_b.reshape(B * S, -1), p_g,
                             w_branch_a[l].astype(BF16), w_branch_b[l].astype(BF16), w_out[l].astype(BF16),
                             ln1_g[l], ln1_b[l], alpha)
        x2 = _moe_ffn_ln(x1, x1p, router_w[l], router_bias[l], exp_w_gu[l], exp_w_down[l],
                         shared_w_gu[l].astype(BF16), shared_w_down[l].astype(BF16), ln2_g[l], ln2_b[l], alpha)
        x = x2.reshape(B, S, D)
    return x
```

```python
import functools

import numpy as np
import jax
import jax.numpy as jnp
from jax import lax
from jax.experimental import pallas as pl
from jax.experimental.pallas import tpu as pltpu
from jax.experimental.pallas import tpu_sc as plsc

F32 = jnp.float32
BF16 = jnp.bfloat16
HIGHEST = lax.Precision.HIGHEST

RWKV_HEADS = 8
HEAD_DIM = 64
RWKV_WIDTH = RWKV_HEADS * HEAD_DIM
W_LORA = 64
A_LORA = 64
G_LORA = 128
GN_EPS = 64e-5
NSA_HEADS = 8
NSA_GROUPS = 2
NSA_HPG = NSA_HEADS // NSA_GROUPS
NSA_WIDTH = NSA_HEADS * HEAD_DIM
NSA_KV_WIDTH = NSA_GROUPS * HEAD_DIM
CMP_BLOCK = 32
CMP_STRIDE = 16
CMP_HIDDEN = 256
SEL_BLOCK = 64
N_SELECT = 16
WINDOW = 512
Q_BLOCK = 128
ROPE_THETA = 10000.0
RWKV_IN_W = 3 * RWKV_WIDTH + W_LORA + A_LORA + G_LORA
N_EXPERTS = 256
TOP_K = 8
N_GROUPS = 8
TOPK_GROUPS = 4
EXPERT_DIM = 256
ROUTED_SCALE = 2.5
LN_EPS = 1e-5
NEG_INF = -1e30
FORCE_BONUS = 1e4

RWKV_CHUNK = 64
RWKV_HEAD_GROUP = 4
VMEM_LIMIT = 56 * 1024 * 1024


def _bdot(a, b):
    return jnp.dot(a.astype(BF16), b.astype(BF16), preferred_element_type=F32)


def _bdot_nt(a, b):
    return lax.dot_general(a.astype(BF16), b.astype(BF16), (((1,), (1,)), ((), ())),
                           preferred_element_type=F32)


def _bdot_tn(a, b):
    return lax.dot_general(a.astype(BF16), b.astype(BF16), (((0,), (0,)), ((), ())),
                           preferred_element_type=F32)


def _hdot(a, b):
    return jnp.dot(a, b, precision=HIGHEST, preferred_element_type=F32)


def _sigmoid(x):
    return 1.0 / (1.0 + jnp.exp(-x))


def _matmul_kernel(x_ref, w_ref, o_ref):
    o_ref[...] = jnp.dot(x_ref[...].astype(BF16), w_ref[...], preferred_element_type=F32)


def _matmul(x, w, tm, tn):
    M, K = x.shape
    N = w.shape[1]
    return pl.pallas_call(
        _matmul_kernel,
        grid=(M // tm, N // tn),
        in_specs=[pl.BlockSpec((tm, K), lambda i, j: (i, 0)),
                  pl.BlockSpec((K, tn), lambda i, j: (0, j))],
        out_specs=pl.BlockSpec((tm, tn), lambda i, j: (i, j)),
        out_shape=jax.ShapeDtypeStruct((M, N), F32),
        compiler_params=pltpu.CompilerParams(
            dimension_semantics=("parallel", "parallel"), vmem_limit_bytes=VMEM_LIMIT),
        name="dense_proj",
    )(x, w)


def _rwkv_kernel(p_ref, mu_ref, w0_ref, w2_ref, a0_ref, a2_ref, g2_ref, kk_ref, ka_ref, rk_ref,
                 lnw_ref, lnb_ref, o_ref, carry_ref, state_ref):
    C, H, N = RWKV_CHUNK, RWKV_HEADS, HEAD_DIM
    W = RWKV_WIDTH
    B = p_ref.shape[0]
    R = B * C

    @pl.when(pl.program_id(0) == 0)
    def _():
        carry_ref[...] = jnp.zeros_like(carry_ref)
        state_ref[...] = jnp.zeros_like(state_ref)

    def per_batch(x):
        return jnp.concatenate([jnp.broadcast_to(x[b].reshape(1, -1), (C, x.shape[-1])) for b in range(B)],
                               axis=0)

    p = p_ref[...].reshape(R, p_ref.shape[-1])
    row = lax.broadcasted_iota(jnp.int32, p.shape, 0)
    prev = jnp.where(row % C == 0, per_batch(carry_ref[...]), pltpu.roll(p, 1, axis=0))
    for b in range(B):
        carry_ref[b] = p[b * C + C - 1:b * C + C, :]
    xs = p + (prev - p) * mu_ref[...]
    r = xs[:, 0:W]
    k = xs[:, W:2 * W]
    v = xs[:, 2 * W:3 * W]
    wl = xs[:, 3 * W:3 * W + W_LORA]
    al = xs[:, 3 * W + W_LORA:3 * W + W_LORA + A_LORA]
    gl = xs[:, 3 * W + W_LORA + A_LORA:]

    z = -(w0_ref[...] + _hdot(jnp.tanh(wl), w2_ref[...]))
    softplus = jnp.maximum(z, 0.0) + jnp.log1p(jnp.exp(-jnp.abs(z)))
    logd = -jnp.exp(-softplus - 0.5)
    a = _sigmoid(a0_ref[...] + _hdot(al, a2_ref[...]))
    g = _hdot(_sigmoid(gl), g2_ref[...])

    kk = k * kk_ref[...]
    knew = k * (1.0 + (a - 1.0) * ka_ref[...])

    def per_head(x, fn):
        return jnp.concatenate(
            [jnp.broadcast_to(fn(x[:, h * N:(h + 1) * N]), (R, N)) for h in range(H)], axis=-1)

    nrm = per_head(kk * kk, lambda t: jnp.sqrt(jnp.sum(t, axis=-1, keepdims=True)))
    kk = kk / jnp.maximum(nrm, 1e-12)
    lr_kk = kk * a

    ti = lax.broadcasted_iota(jnp.int32, (R, R), 0)
    tj = lax.broadcasted_iota(jnp.int32, (R, R), 1)
    same_chunk = (ti >= tj) & (ti // C == tj // C)
    cl = _hdot(same_chunk.astype(F32), logd)
    cl_end = per_batch(jnp.concatenate([cl[b * C + C - 1:b * C + C, :] for b in range(B)], axis=0))
    a_hat = -kk * jnp.exp(cl - logd)
    r_hat = r * jnp.exp(cl)
    inv_gam = jnp.exp(-cl)
    b_til = lr_kk * inv_gam
    k_til = knew * inv_gam
    to_end = jnp.exp(cl_end - cl)
    b_end = lr_kk * to_end
    k_end = knew * to_end
    gam_end = jnp.exp(cl_end)

    HG = RWKV_HEAD_GROUP
    GW = HG * N
    gt = lax.broadcasted_iota(jnp.int32, (C, GW), 0)
    gc = lax.broadcasted_iota(jnp.int32, (C, GW), 1) % N
    strict = gt > gc
    incl = gt >= gc
    eye = (gt == gc).astype(F32)
    bi = lax.broadcasted_iota(jnp.int32, (HG * C, GW), 0) // C
    bj = lax.broadcasted_iota(jnp.int32, (HG * C, GW), 1) // N
    same_head = bi == bj

    def block_diag(y):
        yb = y.astype(BF16)
        return jnp.where(same_head, jnp.concatenate([yb] * HG, axis=0), jnp.zeros((), BF16))

    def bd_dot(x, y):
        return jnp.dot(x.astype(BF16), block_diag(y), preferred_element_type=F32)

    def bd_dot_nt(x, y):
        return lax.dot_general(x.astype(BF16), block_diag(y), (((1,), (1,)), ((), ())),
                               preferred_element_type=F32)

    n_groups = H // HG
    units = [(b, gi) for b in range(B) for gi in range(n_groups)]
    n_units = range(len(units))
    cut = lambda x, b, gi: x[b * C:(b + 1) * C, gi * GW:(gi + 1) * GW]
    v_u = [cut(v, b, gi) for b, gi in units]
    ar = [jnp.concatenate([cut(a_hat, b, gi), cut(r_hat, b, gi)], axis=0) for b, gi in units]
    mb = [bd_dot_nt(ar[i], cut(b_til, *units[i])) for i in n_units]
    mk = [bd_dot_nt(ar[i], cut(k_til, *units[i])) for i in n_units]
    n_ab = [jnp.where(strict, mb[i][:C], 0.0) for i in n_units]
    m_rb = [jnp.where(incl, mb[i][C:], 0.0) for i in n_units]
    l_ak = [jnp.where(strict, mk[i][:C], 0.0) for i in n_units]
    m_rk = [jnp.where(incl, mk[i][C:], 0.0) for i in n_units]

    pw = list(n_ab)
    tinv = [eye + n_ab[i] for i in n_units]
    step = 2
    while step < C:
        pw = [bd_dot(pw[i], pw[i]) for i in n_units]
        tinv = [tinv[i] + bd_dot(tinv[i], pw[i]) for i in n_units]
        step *= 2

    s0 = [state_ref[i] for i in n_units]
    ars = [bd_dot_nt(ar[i], s0[i]) for i in n_units]
    lv = [bd_dot(l_ak[i], v_u[i]) for i in n_units]
    u = [bd_dot(tinv[i], ars[i][:C] + lv[i]) for i in n_units]
    outs = [ars[i][C:] + bd_dot(m_rb[i], u[i]) + bd_dot(m_rk[i], v_u[i]) for i in n_units]
    for i, (b, gi) in enumerate(units):
        uv = jnp.concatenate([u[i], v_u[i]], axis=0)
        bk_end = jnp.concatenate([cut(b_end, b, gi), cut(k_end, b, gi)], axis=0)
        cross = jnp.where(same_head, _bdot_tn(uv, bk_end), 0.0)
        upd = cross[0:N]
        for h in range(1, HG):
            upd = upd + cross[h * N:(h + 1) * N]
        state_ref[i] = s0[i] * gam_end[b * C:b * C + 1, gi * GW:(gi + 1) * GW] + upd

    o = jnp.concatenate([jnp.concatenate(outs[b * n_groups:(b + 1) * n_groups], axis=-1) for b in range(B)],
                        axis=0)
    mean = per_head(o, lambda t: jnp.mean(t, axis=-1, keepdims=True))
    var = per_head(jnp.square(o - mean), lambda t: jnp.mean(t, axis=-1, keepdims=True))
    o = (o - mean) * lax.rsqrt(var + GN_EPS) * lnw_ref[...] + lnb_ref[...]
    bonus = per_head(r * knew * rk_ref[...], lambda t: jnp.sum(t, axis=-1, keepdims=True)) * v
    o_ref[...] = ((o + bonus) * g).reshape(o_ref.shape)


def _hdot_nt(a, b):
    return lax.dot_general(a, b, (((1,), (1,)), ((), ())), precision=HIGHEST,
                           preferred_element_type=F32)


def _hdot_tn(a, b):
    return lax.dot_general(a, b, (((0,), (0,)), ((), ())), precision=HIGHEST,
                           preferred_element_type=F32)


def _rwkv_time_mix(p_a, mu, w0, w2, a0, a2, g2, k_k, k_a, r_k, lnx_w, lnx_b):
    B, S, _ = p_a.shape
    C = RWKV_CHUNK
    row = lambda t: t.reshape(1, -1)
    full = lambda shape: pl.BlockSpec(shape, lambda s: (0,) * len(shape))
    n_units = B * RWKV_HEADS // RWKV_HEAD_GROUP
    return pl.pallas_call(
        _rwkv_kernel,
        grid=(S // C,),
        in_specs=[pl.BlockSpec((B, C, RWKV_IN_W), lambda s: (0, s, 0)),
                  full((1, RWKV_IN_W)), full((1, RWKV_WIDTH)), full((W_LORA, RWKV_WIDTH)),
                  full((1, RWKV_WIDTH)), full((A_LORA, RWKV_WIDTH)), full((G_LORA, RWKV_WIDTH)),
                  full((1, RWKV_WIDTH)), full((1, RWKV_WIDTH)), full((1, RWKV_WIDTH)),
                  full((1, RWKV_WIDTH)), full((1, RWKV_WIDTH))],
        out_specs=pl.BlockSpec((B, C, RWKV_WIDTH), lambda s: (0, s, 0)),
        out_shape=jax.ShapeDtypeStruct((B, S, RWKV_WIDTH), F32),
        scratch_shapes=[pltpu.VMEM((B, 1, RWKV_IN_W), F32),
                        pltpu.VMEM((n_units, HEAD_DIM, RWKV_HEAD_GROUP * HEAD_DIM), F32)],
        compiler_params=pltpu.CompilerParams(
            dimension_semantics=("arbitrary",), vmem_limit_bytes=VMEM_LIMIT),
        name="rwkv7_chunked",
    )(p_a, row(mu), row(w0), w2, row(a0), a2, g2, row(k_k), row(k_a), row(r_k), row(lnx_w), row(lnx_b))


NSA_KV_TILE = 1024
SEL_KEY_TILE = 512
SEL_LANES = 128


def _rope_tables(pos, reps):
    half = HEAD_DIM // 2
    inv = ROPE_THETA ** (-jnp.arange(half, dtype=F32) / half)
    ang = pos.astype(F32)[:, None] * inv
    cos, sin = jnp.cos(ang), jnp.sin(ang)
    cosf = jnp.concatenate([cos, cos], -1)
    sinf = jnp.concatenate([-sin, sin], -1)
    return jnp.tile(cosf, (1, reps)), jnp.tile(sinf, (1, reps))


def _rope(x, cosf, sinf):
    width = x.shape[-1]
    lane = lax.broadcasted_iota(jnp.int32, x.shape, 1)
    first_half = (lane % HEAD_DIM) < HEAD_DIM // 2
    rot = jnp.where(first_half, pltpu.roll(x, width - HEAD_DIM // 2, axis=1),
                    pltpu.roll(x, HEAD_DIM // 2, axis=1))
    return x * cosf + rot * sinf


def _kv_layout_kernel(p_ref, cos_ref, sin_ref, kc_ref, vc_ref, ks_ref, vs_ref, kw_ref, vw_ref):
    outs = (kc_ref, vc_ref, ks_ref, vs_ref, kw_ref, vw_ref)
    roped = (False, False, True, False, True, False)
    for i, (o_ref, use_rope) in enumerate(zip(outs, roped)):
        t = p_ref[:, i * NSA_KV_WIDTH:(i + 1) * NSA_KV_WIDTH]
        if use_rope:
            t = _rope(t, cos_ref[...], sin_ref[...])
        for g in range(NSA_GROUPS):
            o_ref[g] = t[:, g * HEAD_DIM:(g + 1) * HEAD_DIM].astype(o_ref.dtype)


def _kv_layout(p_b, cos2, sin2):
    B, S, _ = p_b.shape
    ts = min(NSA_KV_TILE, S)
    out_spec = pl.BlockSpec((None, NSA_GROUPS, ts, HEAD_DIM), lambda b, s: (b, 0, s, 0))
    shp = lambda dt: jax.ShapeDtypeStruct((B, NSA_GROUPS, S, HEAD_DIM), dt)
    return pl.pallas_call(
        _kv_layout_kernel,
        grid=(B, S // ts),
        in_specs=[pl.BlockSpec((None, ts, 6 * NSA_KV_WIDTH), lambda b, s: (b, s, 0)),
                  pl.BlockSpec((ts, NSA_KV_WIDTH), lambda b, s: (s, 0)),
                  pl.BlockSpec((ts, NSA_KV_WIDTH), lambda b, s: (s, 0))],
        out_specs=[out_spec] * 6,
        out_shape=[shp(F32), shp(F32), shp(BF16), shp(BF16), shp(BF16), shp(BF16)],
        compiler_params=pltpu.CompilerParams(
            dimension_semantics=("parallel", "parallel"), vmem_limit_bytes=VMEM_LIMIT),
        name="nsa_kv_layout",
    )(p_b, cos2, sin2)


def _compress_kernel(subk_ref, subv_ref, pek_ref, w1k_ref, w2k_ref, pev_ref, w1v_ref, w2v_ref,
                     cos_ref, sin_ref, kc_ref, vc_ref):
    n_sub = subk_ref.shape[0]
    half = CMP_STRIDE * HEAD_DIM

    def mlp(sub_ref, pe_ref, w1_ref, w2_ref):
        sub = sub_ref[...]
        top = _bdot(sub, w1_ref[:half, :])
        bot = _bdot(sub, w1_ref[half:, :])
        bias = _bdot(jnp.broadcast_to(pe_ref[...], (8, 2 * half)), w1_ref[...])[0:1, :]
        h = top + pltpu.roll(bot, n_sub - 1, axis=0) + bias
        return _bdot(jax.nn.gelu(h), w2_ref[...])

    kc = mlp(subk_ref, pek_ref, w1k_ref, w2k_ref)
    rot = jnp.concatenate([kc[:, HEAD_DIM // 2:], kc[:, :HEAD_DIM // 2]], axis=-1)
    kc_ref[...] = (kc * cos_ref[...] + rot * sin_ref[...]).astype(kc_ref.dtype)
    vc_ref[...] = mlp(subv_ref, pev_ref, w1v_ref, w2v_ref).astype(vc_ref.dtype)


def _compress(subk, subv, pe_k, w1_k, w2_k, pe_v, w1_v, w2_v, cos_c, sin_c):
    B, G, n_sub, width = subk.shape
    sub_spec = pl.BlockSpec((None, None, n_sub, width), lambda b, g: (b, g, 0, 0))
    full = lambda a: pl.BlockSpec(a.shape, lambda b, g: (0,) * a.ndim)
    out_spec = pl.BlockSpec((None, None, n_sub, HEAD_DIM), lambda b, g: (b, g, 0, 0))
    pe_k, pe_v = pe_k.reshape(1, -1), pe_v.reshape(1, -1)
    args = (pe_k, w1_k, w2_k, pe_v, w1_v, w2_v, cos_c, sin_c)
    return pl.pallas_call(
        _compress_kernel,
        grid=(B, G),
        in_specs=[sub_spec, sub_spec] + [full(a) for a in args],
        out_specs=[out_spec, out_spec],
        out_shape=[jax.ShapeDtypeStruct((B, G, n_sub, HEAD_DIM), BF16)] * 2,
        compiler_params=pltpu.CompilerParams(
            dimension_semantics=("parallel", "parallel"), vmem_limit_bytes=VMEM_LIMIT),
        name="nsa_compress",
    )(subk, subv, *args)


MAX_FLOOR = -1e20


def _masked_exp_rows(s, bias):
    w = s.shape[-1]
    sm = (s.reshape(NSA_HPG, Q_BLOCK, w) + bias[None]).reshape(s.shape)
    m = jnp.maximum(jnp.max(sm, axis=-1, keepdims=True), MAX_FLOOR)
    e = jnp.exp(sm - m)
    den = jnp.sum(e, axis=-1, keepdims=True)
    return e, 1.0 / jnp.where(den > 0.0, den, 1.0)


def _nsa_kernel(q_ref, gate_ref, cos_ref, sin_ref, kc_ref, vc_ref, ks_ref, vs_ref, kw_ref, vw_ref,
                mselt_ref, o_ref, *, n_pick):
    QB, HP, D = Q_BLOCK, NSA_HPG, HEAD_DIM
    qb = pl.program_id(2)
    n_cmp = kc_ref.shape[0]
    scale = D ** -0.5

    q = _rope(q_ref[...], cos_ref[...], sin_ref[...]) * scale
    q4 = jnp.concatenate([q[:, n * D:(n + 1) * D] for n in range(HP)], axis=0).astype(BF16)
    t_col = qb * QB + lax.broadcasted_iota(jnp.int32, (QB, 1), 0)
    t_row = qb * QB + lax.broadcasted_iota(jnp.int32, (1, QB), 1)

    cmp_end = lax.broadcasted_iota(jnp.int32, (1, n_cmp), 1) * CMP_STRIDE + (CMP_BLOCK - 1)
    e_c, inv_c = _masked_exp_rows(_bdot_nt(q4, kc_ref[...]), jnp.where(cmp_end <= t_col, 0.0, NEG_INF))
    p_c = e_c * inv_c
    o_c = _bdot(p_c, vc_ref[...])
    p_sum = p_c[0:QB]
    for n in range(1, HP):
        p_sum = p_sum + p_c[n * QB:(n + 1) * QB]
    p_hi = p_sum.astype(BF16)
    p_lo = (p_sum - p_hi.astype(F32)).astype(BF16)
    imp_t = (lax.dot_general(mselt_ref[...], p_hi, (((1,), (1,)), ((), ())), preferred_element_type=F32)
             + lax.dot_general(mselt_ref[...], p_lo, (((1,), (1,)), ((), ())), preferred_element_type=F32))

    j = lax.broadcasted_iota(jnp.int32, (SEL_LANES, QB), 0)
    cur = t_row // SEL_BLOCK
    valid = j * SEL_BLOCK <= t_row
    forced = (j == 0) | (j == cur) | (j == cur - 1)
    score = jnp.where(valid, imp_t + jnp.where(forced, FORCE_BONUS, 0.0), -1.0)
    sel_t = jnp.zeros((SEL_LANES, QB), F32)
    for _ in range(n_pick):
        m = jnp.max(score, axis=0, keepdims=True)
        idx = jnp.min(jnp.where(score == m, j, SEL_LANES), axis=0, keepdims=True)
        hit = j == idx
        sel_t = jnp.where(hit & (m >= 0.0), 1.0, sel_t)
        score = jnp.where(hit, -2.0, score)
    sel_b = sel_t.T.astype(BF16)

    KT = SEL_KEY_TILE
    blocks_per_tile = KT // SEL_BLOCK
    n_tiles = (qb * QB + QB + KT - 1) // KT

    def sel_step(kt, carry):
        m_i, l_i, acc = carry
        start = pl.multiple_of(kt * KT, KT)
        k_t = ks_ref[pl.ds(start, KT), :]
        v_t = vs_ref[pl.ds(start, KT), :]
        jrow = lax.broadcasted_iota(jnp.int32, (SEL_LANES, KT), 0)
        ccol = lax.broadcasted_iota(jnp.int32, (SEL_LANES, KT), 1)
        expand = (jrow == kt * blocks_per_tile + ccol // SEL_BLOCK).astype(BF16)
        picked = jnp.dot(sel_b, expand, preferred_element_type=F32)
        kpos = start + lax.broadcasted_iota(jnp.int32, (1, KT), 1)
        bias = jnp.where((picked > 0.5) & (kpos <= t_col), 0.0, NEG_INF)
        s = _bdot_nt(q4, k_t)
        sm = (s.reshape(HP, QB, KT) + bias[None]).reshape(s.shape)
        m_new = jnp.maximum(m_i, jnp.max(sm, axis=-1, keepdims=True))
        e = jnp.exp(sm - m_new)
        alpha = jnp.exp(m_i - m_new)
        l_new = alpha * l_i + jnp.sum(e, axis=-1, keepdims=True)
        acc_new = alpha * acc + _bdot(e, v_t)
        return m_new, l_new, acc_new

    init = (jnp.full((HP * QB, 1), MAX_FLOOR, F32), jnp.zeros((HP * QB, 1), F32),
            jnp.zeros((HP * QB, D), F32))
    _, l_s, acc_s = lax.fori_loop(0, n_tiles, sel_step, init)
    o_s = acc_s * (1.0 / jnp.where(l_s > 0.0, l_s, 1.0))

    span = WINDOW + QB
    w_start = pl.multiple_of(jnp.maximum(qb * QB - WINDOW, 0), QB)
    dist = t_col - (w_start + lax.broadcasted_iota(jnp.int32, (1, span), 1))
    e_w, inv_w = _masked_exp_rows(_bdot_nt(q4, kw_ref[pl.ds(w_start, span), :]),
                                  jnp.where((dist >= 0) & (dist < WINDOW), 0.0, NEG_INF))
    o_w = _bdot(e_w, vw_ref[pl.ds(w_start, span), :]) * inv_w

    gates = _sigmoid(gate_ref[...])
    for n in range(HP):
        rows = slice(n * QB, (n + 1) * QB)
        o_ref[:, n * D:(n + 1) * D] = (gates[:, 3 * n:3 * n + 1] * o_c[rows]
                                       + gates[:, 3 * n + 1:3 * n + 2] * o_s[rows]
                                       + gates[:, 3 * n + 2:3 * n + 3] * o_w[rows])


def _cmp_to_sel_matrix(n_cmp_rows, n_sel):
    ratio = SEL_BLOCK // CMP_STRIDE
    ci = np.arange(n_cmp_rows)[:, None]
    sj = np.arange(SEL_LANES)[None, :]
    m = sum(((ci + n) // ratio == sj).astype(np.float32) for n in range(CMP_BLOCK // CMP_STRIDE))
    m = m * (sj < n_sel) * (ci < n_cmp_rows - 1)
    return jnp.asarray(m.T, BF16)


def _nsa_attention(p_b, kc, vc, ks, vs, kw, vw, cos_q, sin_q):
    B, S, _ = p_b.shape
    n_sub = kc.shape[2]
    n_sel = S // SEL_BLOCK
    gw = NSA_HPG * HEAD_DIM
    q_col0 = 6 * NSA_KV_WIDTH // gw
    gate_col0 = (6 * NSA_KV_WIDTH + NSA_WIDTH) // 128
    msel = _cmp_to_sel_matrix(n_sub, n_sel)
    cmp_spec = pl.BlockSpec((None, None, n_sub, HEAD_DIM), lambda b, g, i: (b, g, 0, 0))
    kv_spec = pl.BlockSpec((None, None, S, HEAD_DIM), lambda b, g, i: (b, g, 0, 0))
    return pl.pallas_call(
        functools.partial(_nsa_kernel, n_pick=min(N_SELECT, n_sel)),
        grid=(B, NSA_GROUPS, S // Q_BLOCK),
        in_specs=[pl.BlockSpec((None, Q_BLOCK, gw), lambda b, g, i: (b, i, q_col0 + g)),
                  pl.BlockSpec((None, Q_BLOCK, 128), lambda b, g, i: (b, i, gate_col0 + g)),
                  pl.BlockSpec((Q_BLOCK, gw), lambda b, g, i: (i, 0)),
                  pl.BlockSpec((Q_BLOCK, gw), lambda b, g, i: (i, 0)),
                  cmp_spec, cmp_spec, kv_spec, kv_spec, kv_spec, kv_spec,
                  pl.BlockSpec(msel.shape, lambda b, g, i: (0, 0))],
        out_specs=pl.BlockSpec((None, Q_BLOCK, gw), lambda b, g, i: (b, i, g)),
        out_shape=jax.ShapeDtypeStruct((B, S, NSA_WIDTH), F32),
        compiler_params=pltpu.CompilerParams(
            dimension_semantics=("parallel", "parallel", "arbitrary"), vmem_limit_bytes=VMEM_LIMIT),
        name="nsa_attention",
    )(p_b, p_b, cos_q, sin_q, kc, vc, ks, vs, kw, vw, msel)


def _nsa_branch(p_b, cmp_pe_k, cmp_w1_k, cmp_w2_k, cmp_pe_v, cmp_w1_v, cmp_w2_v):
    B, S, _ = p_b.shape
    pos = jnp.arange(S)
    cos2, sin2 = _rope_tables(pos, NSA_GROUPS)
    kc_raw, vc_raw, ks, vs, kw, vw = _kv_layout(p_b, cos2, sin2)
    n_sub = S // CMP_STRIDE
    sub = lambda t: t.reshape(B, NSA_GROUPS, n_sub, CMP_STRIDE * HEAD_DIM)
    cos_c, sin_c = _rope_tables(jnp.arange(n_sub) * CMP_STRIDE + CMP_BLOCK - 1, 1)
    kc, vc = _compress(sub(kc_raw), sub(vc_raw), cmp_pe_k, cmp_w1_k, cmp_w2_k,
                       cmp_pe_v, cmp_w1_v, cmp_w2_v, cos_c, sin_c)
    cos_q, sin_q = _rope_tables(pos, NSA_HPG)
    return _nsa_attention(p_b, kc, vc, ks, vs, kw, vw, cos_q, sin_q)


def _nsa_weight_columns(w_nsa):
    K = w_nsa.shape[0]
    q = w_nsa[:, :NSA_WIDTH]
    kv = w_nsa[:, NSA_WIDTH:NSA_WIDTH + 6 * NSA_KV_WIDTH]
    gates = w_nsa[:, NSA_WIDTH + 6 * NSA_KV_WIDTH:]
    per_group = NSA_HPG * 3
    gate_blocks = [jnp.pad(gates[:, g * per_group:(g + 1) * per_group], ((0, 0), (0, 128 - per_group)))
                   for g in range(NSA_GROUPS)]
    return jnp.concatenate([kv, q] + gate_blocks, axis=1)


def _layer_norm(h, g, b):
    mu = jnp.mean(h, axis=-1, keepdims=True)
    var = jnp.mean(jnp.square(h - mu), axis=-1, keepdims=True)
    return (h - mu) * lax.rsqrt(var + LN_EPS) * g + b


def _pack_bf16_halves(x):
    n = x.shape[-1] // 2
    bits = lax.bitcast_convert_type(x.astype(BF16).astype(F32), jnp.uint32)
    return (bits[:, n:] & jnp.uint32(0xFFFF0000)) | (bits[:, :n] >> 16)


def _unpack_bf16_halves(u):
    left = lax.bitcast_convert_type(u << 16, F32)
    right = lax.bitcast_convert_type(u & jnp.uint32(0xFFFF0000), F32)
    return left, right


def _mixer_out_kernel(x_ref, ya_ref, yb_ref, pg_ref, wa_ref, wb_ref, wo_ref, g_ref, b_ref, o_ref, op_ref,
                      *, alpha):
    d = x_ref.shape[-1]
    gate_a = _sigmoid(pg_ref[:, :d])
    gate_b = _sigmoid(pg_ref[:, d:])
    mixed = gate_a * _bdot(ya_ref[...], wa_ref[...]) + gate_b * _bdot(yb_ref[...], wb_ref[...])
    h = alpha * x_ref[...] + _bdot(mixed, wo_ref[...])
    out = _layer_norm(h, g_ref[...], b_ref[...])
    o_ref[...] = out
    op_ref[...] = _pack_bf16_halves(out)


def _mixer_out(xf, ya, yb, p_g, wa, wb, wo, ln_g, ln_b, alpha, tm=512):
    T, D = xf.shape
    rows = lambda w: pl.BlockSpec((tm, w), lambda i: (i, 0))
    full = lambda a: pl.BlockSpec(a.shape, lambda i: (0,) * a.ndim)
    ln_g, ln_b = ln_g.reshape(1, D), ln_b.reshape(1, D)
    return pl.pallas_call(
        functools.partial(_mixer_out_kernel, alpha=alpha),
        grid=(T // tm,),
        in_specs=[rows(D), rows(ya.shape[1]), rows(yb.shape[1]), rows(2 * D),
                  full(wa), full(wb), full(wo), full(ln_g), full(ln_b)],
        out_specs=[rows(D), rows(D // 2)],
        out_shape=[jax.ShapeDtypeStruct((T, D), F32), jax.ShapeDtypeStruct((T, D // 2), jnp.uint32)],
        compiler_params=pltpu.CompilerParams(
            dimension_semantics=("parallel",), vmem_limit_bytes=VMEM_LIMIT),
        name="mixer_out_ln",
    )(xf, ya, yb, p_g, wa, wb, wo, ln_g, ln_b)


ROUTER_TILE = 256
EXPERT_ROWS = 256
SC_TOKEN_CHUNK = 64
SC_ROW_CHUNK = 128
PICK_LANES = 128
LOWEST = -3.0e38


def _router_kernel(x_ref, rw_ref, bias_ref, eidx_ref, wts_ref, pos_ref, cnt_ref, carry_ref):
    tm, E = x_ref.shape[0], rw_ref.shape[1]
    per_group = E // N_GROUPS

    @pl.when(pl.program_id(0) == 0)
    def _():
        carry_ref[...] = jnp.zeros_like(carry_ref)

    scores = _sigmoid(_hdot(x_ref[...], rw_ref[...]))
    choice = scores + bias_ref[...]
    lane = lax.broadcasted_iota(jnp.int32, (tm, E), 1)
    grp = lane // per_group

    def first_max(vals):
        m = jnp.max(vals, axis=-1, keepdims=True)
        return m, jnp.min(jnp.where(vals == m, lane, E), axis=-1, keepdims=True)

    group_score = []
    for g in range(N_GROUPS):
        cg = jnp.where(grp == g, choice, LOWEST)
        m1, i1 = first_max(cg)
        m2 = jnp.max(jnp.where(lane == i1, LOWEST, cg), axis=-1, keepdims=True)
        group_score.append(m1 + m2)
    allowed = jnp.zeros((tm, E), jnp.bool_)
    for g in range(N_GROUPS):
        rank = jnp.zeros((tm, 1), jnp.int32)
        for o in range(N_GROUPS):
            if o != g:
                ahead = (group_score[o] > group_score[g]) if o > g else (group_score[o] >= group_score[g])
                rank = rank + ahead.astype(jnp.int32)
        allowed = allowed | ((grp == g) & (rank < TOPK_GROUPS))

    cur = jnp.where(allowed, choice, NEG_INF)
    sel = jnp.zeros((tm, E), F32)
    picks = []
    for _ in range(TOP_K):
        _, idx = first_max(cur)
        hit = lane == idx
        picks.append(idx)
        sel = jnp.where(hit, 1.0, sel)
        cur = jnp.where(hit, LOWEST, cur)
    gate = scores * sel
    gate = gate / jnp.sum(gate, axis=-1, keepdims=True) * ROUTED_SCALE

    ri = lax.broadcasted_iota(jnp.int32, (tm, tm), 0)
    ci = lax.broadcasted_iota(jnp.int32, (tm, tm), 1)
    before = jnp.dot((ri > ci).astype(BF16), sel.astype(BF16), preferred_element_type=F32)
    queue_pos = before + carry_ref[...]
    carry_ref[...] = carry_ref[...] + jnp.sum(sel, axis=0, keepdims=True)
    cnt_ref[...] = jnp.broadcast_to(carry_ref[...], cnt_ref.shape)

    out_lane = lax.broadcasted_iota(jnp.int32, (tm, PICK_LANES), 1)
    eidx = jnp.zeros((tm, PICK_LANES), jnp.int32)
    wts = jnp.zeros((tm, PICK_LANES), F32)
    pos = jnp.zeros((tm, PICK_LANES), F32)
    for kk, idx in enumerate(picks):
        hit = lane == idx
        eidx = jnp.where(out_lane == kk, idx, eidx)
        wts = jnp.where(out_lane == kk, jnp.sum(jnp.where(hit, gate, 0.0), axis=-1, keepdims=True), wts)
        pos = jnp.where(out_lane == kk, jnp.sum(jnp.where(hit, queue_pos, 0.0), axis=-1, keepdims=True), pos)
    eidx_ref[...] = eidx
    wts_ref[...] = wts
    pos_ref[...] = pos.astype(jnp.int32)


def _router(xf, router_w, router_bias):
    T, D = xf.shape
    E = router_w.shape[1]
    tm = ROUTER_TILE
    picks = lambda dt: jax.ShapeDtypeStruct((T, PICK_LANES), dt)
    pick_spec = pl.BlockSpec((tm, PICK_LANES), lambda i: (i, 0))
    return pl.pallas_call(
        _router_kernel,
        grid=(T // tm,),
        in_specs=[pl.BlockSpec((tm, D), lambda i: (i, 0)),
                  pl.BlockSpec((D, E), lambda i: (0, 0)),
                  pl.BlockSpec((1, E), lambda i: (0, 0))],
        out_specs=[pick_spec, pick_spec, pick_spec, pl.BlockSpec((8, E), lambda i: (0, 0))],
        out_shape=[picks(jnp.int32), picks(F32), picks(jnp.int32), jax.ShapeDtypeStruct((8, E), F32)],
        scratch_shapes=[pltpu.VMEM((1, E), F32)],
        compiler_params=pltpu.CompilerParams(
            dimension_semantics=("arbitrary",), vmem_limit_bytes=VMEM_LIMIT),
        name="moe_router",
    )(xf, router_w, router_bias.reshape(1, E))


def _dest_kernel(eidx_ref, pos_ref, start_ref, dest_ref):
    tm = eidx_ref.shape[0]
    E = start_ref.shape[1]
    lane = lax.broadcasted_iota(jnp.int32, (tm, E), 1)
    out_lane = lax.broadcasted_iota(jnp.int32, (tm, PICK_LANES), 1)
    eidx = eidx_ref[...]
    base = jnp.zeros((tm, PICK_LANES), jnp.int32)
    for kk in range(TOP_K):
        hit = lane == eidx[:, kk:kk + 1]
        start = jnp.sum(jnp.where(hit, start_ref[...], 0), axis=-1, keepdims=True)
        base = jnp.where(out_lane == kk, start, base)
    dest_ref[...] = base + pos_ref[...]


def _dest_rows(eidx, pos, pad_start):
    T = eidx.shape[0]
    E = pad_start.shape[0]
    tm = ROUTER_TILE
    pick_spec = pl.BlockSpec((tm, PICK_LANES), lambda i: (i, 0))
    return pl.pallas_call(
        _dest_kernel,
        grid=(T // tm,),
        in_specs=[pick_spec, pick_spec, pl.BlockSpec((1, E), lambda i: (0, 0))],
        out_specs=pick_spec,
        out_shape=jax.ShapeDtypeStruct((T, PICK_LANES), jnp.int32),
        compiler_params=pltpu.CompilerParams(
            dimension_semantics=("parallel",), vmem_limit_bytes=VMEM_LIMIT),
        name="moe_dest_rows",
    )(eidx, pos, pad_start.reshape(1, E))


def _sc_mesh():
    return plsc.VectorSubcoreMesh(core_axis_name="c", subcore_axis_name="s")


def _sc_scatter_rows(x, dest_t, n_rows):
    T, D = x.shape
    K = dest_t.shape[0]
    mesh = _sc_mesh()
    nc, nw = mesh.num_cores, mesh.num_cores * mesh.num_subcores
    per_w = T // nw
    chunk = min(SC_TOKEN_CHUNK, per_w)
    n_chunks = per_w // chunk
    idx = dest_t.reshape(K, nw, n_chunks, chunk).transpose(1, 2, 0, 3).reshape(nw, n_chunks * K, chunk)

    @functools.partial(
        pl.kernel, mesh=mesh,
        out_type=jax.ShapeDtypeStruct((n_rows, D), x.dtype),
        scratch_types=[pltpu.VMEM((n_chunks * K, chunk), jnp.int32),
                       pltpu.VMEM((chunk, D), x.dtype),
                       pltpu.SemaphoreType.DMA],
    )
    def scatter(x_hbm, idx_hbm, out_hbm, idx_v, rows_v, sem):
        wid = lax.axis_index("s") * nc + lax.axis_index("c")
        pltpu.sync_copy(idx_hbm.at[wid], idx_v)

        @pl.loop(0, n_chunks)
        def _(j):
            pltpu.sync_copy(x_hbm.at[pl.ds(wid * per_w + j * chunk, chunk)], rows_v)
            copies = [pltpu.async_copy(rows_v, out_hbm.at[idx_v.at[j * K + kk]], sem) for kk in range(K)]
            for c in copies:
                c.wait()

    return scatter(x, idx)


def _sc_gather_rows(src, idx):
    M = idx.shape[0]
    D = src.shape[1]
    mesh = _sc_mesh()
    nc, nw = mesh.num_cores, mesh.num_cores * mesh.num_subcores
    per_w = M // nw
    chunk = min(SC_ROW_CHUNK, per_w)
    n_chunks = per_w // chunk

    @functools.partial(
        pl.kernel, mesh=mesh,
        out_type=jax.ShapeDtypeStruct((M, D), src.dtype),
        scratch_types=[pltpu.VMEM((n_chunks, chunk), jnp.int32),
                       pltpu.VMEM((chunk, D), src.dtype),
                       pltpu.SemaphoreType.DMA],
    )
    def gather(src_hbm, idx_hbm, out_hbm, idx_v, rows_v, sem):
        wid = lax.axis_index("s") * nc + lax.axis_index("c")
        pltpu.sync_copy(idx_hbm.at[wid], idx_v)

        @pl.loop(0, n_chunks)
        def _(j):
            pltpu.async_copy(src_hbm.at[idx_v.at[j]], rows_v, sem).wait()
            pltpu.sync_copy(rows_v, out_hbm.at[pl.ds(wid * per_w + j * chunk, chunk)])

    return gather(src, idx.reshape(nw, n_chunks, chunk))


def _expert_kernel(slot_e_ref, blk_slot_ref, blk_new_ref, blk_rows_ref, n_used_ref,
                   x_ref, wgu0_ref, wd0_ref, wgu1_ref, wd1_ref, o_ref, wgu_bf, wd_bf):
    i = pl.program_id(0)
    live = i < n_used_ref[0]
    fresh = live & (blk_new_ref[i] == 1)

    @pl.when(fresh & (blk_slot_ref[i] == 0))
    def _():
        wgu_bf[...] = wgu0_ref[...].astype(BF16)
        wd_bf[...] = wd0_ref[...].astype(BF16)

    @pl.when(fresh & (blk_slot_ref[i] == 1))
    def _():
        wgu_bf[...] = wgu1_ref[...].astype(BF16)
        wd_bf[...] = wd1_ref[...].astype(BF16)

    @pl.when(live)
    def _():
        hidden = wd_bf.shape[0]
        half = x_ref.shape[1]
        row = lax.broadcasted_iota(jnp.int32, x_ref.shape, 0)
        left, right = _unpack_bf16_halves(x_ref[...])
        real = row < blk_rows_ref[i]
        left = jnp.where(real, left, 0.0).astype(BF16)
        right = jnp.where(real, right, 0.0).astype(BF16)
        h = (jnp.dot(left, wgu_bf[:half, :], preferred_element_type=F32)
             + jnp.dot(right, wgu_bf[half:, :], preferred_element_type=F32))
        gate, up = h[:, :hidden], h[:, hidden:]
        act = (gate * _sigmoid(gate) * up).astype(BF16)
        o_ref[...] = _pack_bf16_halves(jnp.dot(act, wd_bf[...], preferred_element_type=F32))

    @pl.when(jnp.logical_not(live))
    def _():
        o_ref[...] = jnp.zeros_like(o_ref)


def _expert_ffn(xs, blk_e, blk_rows, n_used, w_gu, w_down):
    n_rows, half = xs.shape
    E, D, two_h = w_gu.shape
    n_blocks = n_rows // EXPERT_ROWS
    idx = jnp.arange(n_blocks, dtype=jnp.int32)
    is_live = idx < n_used[0]
    blk_new = (is_live & ((idx == 0) | (blk_e != jnp.roll(blk_e, 1)))).astype(jnp.int32)
    ordinal = jnp.cumsum(blk_new) - 1
    n_distinct = ordinal[-1] + 1
    distinct_e = jnp.zeros((n_blocks,), jnp.int32).at[ordinal].max(blk_e * is_live)
    blk_slot = ordinal % 2
    in_slot = lambda s: distinct_e[jnp.minimum(ordinal + (blk_slot != s), n_distinct - 1)]
    slot_e = jnp.stack([in_slot(0), in_slot(1)]).astype(jnp.int32)

    live = lambda i, nu: jnp.minimum(i, nu[0] - 1)
    w_spec = lambda shape, s: pl.BlockSpec((None,) + shape,
                                           lambda i, se, bs, bn, br, nu: (se[s, live(i, nu)], 0, 0))
    grid_spec = pltpu.PrefetchScalarGridSpec(
        num_scalar_prefetch=5,
        grid=(n_blocks,),
        in_specs=[pl.BlockSpec((EXPERT_ROWS, half), lambda i, se, bs, bn, br, nu: (live(i, nu), 0)),
                  w_spec((D, two_h), 0), w_spec((two_h // 2, D), 0),
                  w_spec((D, two_h), 1), w_spec((two_h // 2, D), 1)],
        out_specs=pl.BlockSpec((EXPERT_ROWS, half), lambda i, se, bs, bn, br, nu: (i, 0)),
        scratch_shapes=[pltpu.VMEM((D, two_h), BF16), pltpu.VMEM((two_h // 2, D), BF16)],
    )
    return pl.pallas_call(
        _expert_kernel,
        grid_spec=grid_spec,
        out_shape=jax.ShapeDtypeStruct((n_rows, half), jnp.uint32),
        compiler_params=pltpu.CompilerParams(
            dimension_semantics=("arbitrary",), vmem_limit_bytes=VMEM_LIMIT),
        name="moe_experts",
    )(slot_e, blk_slot, blk_new, blk_rows, n_used, xs, w_gu, w_down, w_gu, w_down)


def _moe_out_kernel(x_ref, yk_ref, wts_ref, sgu_ref, sd_ref, g_ref, b_ref, o_ref, *, alpha):
    x = x_ref[...]
    hidden = sd_ref.shape[0]
    h = _bdot(x, sgu_ref[...])
    gate, up = h[:, :hidden], h[:, hidden:]
    ffn = _bdot(gate * _sigmoid(gate) * up, sd_ref[...])
    wts = wts_ref[...]
    routed_left = routed_right = None
    for kk in range(TOP_K):
        left, right = _unpack_bf16_halves(yk_ref[kk])
        w = wts[:, kk:kk + 1]
        routed_left = w * left if kk == 0 else routed_left + w * left
        routed_right = w * right if kk == 0 else routed_right + w * right
    ffn = ffn + jnp.concatenate([routed_left, routed_right], axis=-1)
    o_ref[...] = _layer_norm(alpha * x + ffn, g_ref[...], b_ref[...])


def _moe_out(xf, yk, wts, sw_gu, sw_down, ln_g, ln_b, alpha, tm=128):
    T, D = xf.shape
    rows = lambda w: pl.BlockSpec((tm, w), lambda i: (i, 0))
    full = lambda a: pl.BlockSpec(a.shape, lambda i: (0,) * a.ndim)
    ln_g, ln_b = ln_g.reshape(1, D), ln_b.reshape(1, D)
    return pl.pallas_call(
        functools.partial(_moe_out_kernel, alpha=alpha),
        grid=(T // tm,),
        in_specs=[rows(D), pl.BlockSpec((TOP_K, tm, D // 2), lambda i: (0, i, 0)), rows(PICK_LANES),
                  full(sw_gu), full(sw_down), full(ln_g), full(ln_b)],
        out_specs=rows(D),
        out_shape=jax.ShapeDtypeStruct((T, D), F32),
        compiler_params=pltpu.CompilerParams(
            dimension_semantics=("parallel",), vmem_limit_bytes=VMEM_LIMIT),
        name="moe_combine_ln",
    )(xf, yk, wts, sw_gu, sw_down, ln_g, ln_b)


def _moe_ffn_ln(xf, xp, router_w, router_bias, w_gu, w_down, sw_gu, sw_down, ln_g, ln_b, alpha):
    T, D = xf.shape
    E = router_w.shape[1]
    BM = EXPERT_ROWS
    eidx, wts, pos, cnt = _router(xf, router_w, router_bias)
    counts = cnt[0].astype(jnp.int32)
    padded = (counts + BM - 1) // BM * BM
    pad_end = jnp.cumsum(padded)
    pad_start = pad_end - padded
    n_rows = T * TOP_K + E * BM
    n_blocks = n_rows // BM
    blk_row0 = jnp.arange(n_blocks, dtype=jnp.int32) * BM
    blk_e = jnp.minimum(jnp.sum((pad_end[None, :] <= blk_row0[:, None]).astype(jnp.int32), axis=1), E - 1)
    blk_rows = jnp.clip(pad_start[blk_e] + counts[blk_e] - blk_row0, 0, BM).astype(jnp.int32)
    n_used = (pad_end[-1:] // BM).astype(jnp.int32)
    dest = _dest_rows(eidx, pos, pad_start)[:, :TOP_K]
    dest_t = dest.T
    xs = _sc_scatter_rows(xp, dest_t, n_rows)
    ys = _expert_ffn(xs, blk_e, blk_rows, n_used, w_gu, w_down)
    yk = _sc_gather_rows(ys, dest_t.reshape(-1)).reshape(TOP_K, T, D // 2)
    return _moe_out(xf, yk, wts, sw_gu, sw_down, ln_g, ln_b, alpha)


def kernel(x, w_in, tshift_mu, rwkv_w0, rwkv_w2, rwkv_a0, rwkv_a2, rwkv_g2, rwkv_k_k, rwkv_k_a, rwkv_r_k, rwkv_lnx_w, rwkv_lnx_b, cmp_pe_k, cmp_w1_k, cmp_w2_k, cmp_pe_v, cmp_w1_v, cmp_w2_v, w_branch_a, w_branch_b, w_out, ln1_g, ln1_b, router_w, router_bias, exp_w_gu, exp_w_down, shared_w_gu, shared_w_down, ln2_g, ln2_b):
    B, S, D = x.shape
    depth = w_in.shape[0]
    alpha = (2 * depth) ** 0.25
    nsa_w = w_in.shape[2] - RWKV_IN_W - 2 * D
    for l in range(depth):
        xf = x.reshape(B * S, D)
        w_l = w_in[l]
        w_a = w_l[:, :RWKV_IN_W].astype(BF16)
        w_b = _nsa_weight_columns(w_l[:, RWKV_IN_W:RWKV_IN_W + nsa_w]).astype(BF16)
        w_g = w_l[:, RWKV_IN_W + nsa_w:].astype(BF16)
        p_a = _matmul(xf, w_a, 512, RWKV_IN_W // 2).reshape(B, S, -1)
        p_b = _matmul(xf, w_b, 512, w_b.shape[1] // 2).reshape(B, S, -1)
        p_g = _matmul(xf, w_g, 512, D)
        y_a = _rwkv_time_mix(p_a, tshift_mu[l], rwkv_w0[l], rwkv_w2[l], rwkv_a0[l], rwkv_a2[l], rwkv_g2[l],
                             rwkv_k_k[l], rwkv_k_a[l], rwkv_r_k[l].reshape(-1), rwkv_lnx_w[l], rwkv_lnx_b[l])
        y_b = _nsa_branch(p_b, cmp_pe_k[l], cmp_w1_k[l], cmp_w2_k[l], cmp_pe_v[l], cmp_w1_v[l], cmp_w2_v[l])
        x1, x1p = _mixer_out(xf, y_a.reshape(B * S, -1), y_b.reshape(B * S, -1), p_g,
                             w_branch_a[l].astype(BF16), w_branch_b[l].astype(BF16), w_out[l].astype(BF16),
                             ln1_g[l], ln1_b[l], alpha)
        x2 = _moe_ffn_ln(x1, x1p, router_w[l], router_bias[l], exp_w_gu[l], exp_w_down[l],
                         shared_w_gu[l].astype(BF16), shared_w_down[l].astype(BF16), ln2_g[l], ln2_b[l], alpha)
        x = x2.reshape(B, S, D)
    return x
```

```python
import functools

import numpy as np
import jax
import jax.numpy as jnp
from jax import lax
from jax.experimental import pallas as pl
from jax.experimental.pallas import tpu as pltpu
from jax.experimental.pallas import tpu_sc as plsc

F32 = jnp.float32
BF16 = jnp.bfloat16
HIGHEST = lax.Precision.HIGHEST

RWKV_HEADS = 8
HEAD_DIM = 64
RWKV_WIDTH = RWKV_HEADS * HEAD_DIM
W_LORA = 64
A_LORA = 64
G_LORA = 128
GN_EPS = 64e-5
NSA_HEADS = 8
NSA_GROUPS = 2
NSA_HPG = NSA_HEADS // NSA_GROUPS
NSA_WIDTH = NSA_HEADS * HEAD_DIM
NSA_KV_WIDTH = NSA_GROUPS * HEAD_DIM
CMP_BLOCK = 32
CMP_STRIDE = 16
CMP_HIDDEN = 256
SEL_BLOCK = 64
N_SELECT = 16
WINDOW = 512
Q_BLOCK = 128
ROPE_THETA = 10000.0
RWKV_IN_W = 3 * RWKV_WIDTH + W_LORA + A_LORA + G_LORA
N_EXPERTS = 256
TOP_K = 8
N_GROUPS = 8
TOPK_GROUPS = 4
EXPERT_DIM = 256
ROUTED_SCALE = 2.5
LN_EPS = 1e-5
NEG_INF = -1e30
FORCE_BONUS = 1e4

RWKV_CHUNK = 64
RWKV_HEAD_GROUP = 4
VMEM_LIMIT = 56 * 1024 * 1024


def _bdot(a, b):
    return jnp.dot(a.astype(BF16), b.astype(BF16), preferred_element_type=F32)


def _bdot_nt(a, b):
    return lax.dot_general(a.astype(BF16), b.astype(BF16), (((1,), (1,)), ((), ())),
                           preferred_element_type=F32)


def _bdot_tn(a, b):
    return lax.dot_general(a.astype(BF16), b.astype(BF16), (((0,), (0,)), ((), ())),
                           preferred_element_type=F32)


def _hdot(a, b):
    return jnp.dot(a, b, precision=HIGHEST, preferred_element_type=F32)


def _sigmoid(x):
    return 1.0 / (1.0 + jnp.exp(-x))


def _matmul_kernel(x_ref, w_ref, o_ref):
    o_ref[...] = jnp.dot(x_ref[...].astype(BF16), w_ref[...], preferred_element_type=F32)


def _matmul(x, w, tm, tn):
    M, K = x.shape
    N = w.shape[1]
    return pl.pallas_call(
        _matmul_kernel,
        grid=(M // tm, N // tn),
        in_specs=[pl.BlockSpec((tm, K), lambda i, j: (i, 0)),
                  pl.BlockSpec((K, tn), lambda i, j: (0, j))],
        out_specs=pl.BlockSpec((tm, tn), lambda i, j: (i, j)),
        out_shape=jax.ShapeDtypeStruct((M, N), F32),
        compiler_params=pltpu.CompilerParams(
            dimension_semantics=("parallel", "parallel"), vmem_limit_bytes=VMEM_LIMIT),
        name="dense_proj",
    )(x, w)


def _rwkv_kernel(p_ref, mu_ref, w0_ref, w2_ref, a0_ref, a2_ref, g2_ref, kk_ref, ka_ref, rk_ref,
                 lnw_ref, lnb_ref, o_ref, carry_ref, state_ref):
    C, H, N = RWKV_CHUNK, RWKV_HEADS, HEAD_DIM
    W = RWKV_WIDTH
    B = p_ref.shape[0]
    R = B * C

    @pl.when(pl.program_id(0) == 0)
    def _():
        carry_ref[...] = jnp.zeros_like(carry_ref)
        state_ref[...] = jnp.zeros_like(state_ref)

    def per_batch(x):
        return jnp.concatenate([jnp.broadcast_to(x[b].reshape(1, -1), (C, x.shape[-1])) for b in range(B)],
                               axis=0)

    p = p_ref[...].reshape(R, p_ref.shape[-1])
    row = lax.broadcasted_iota(jnp.int32, p.shape, 0)
    prev = jnp.where(row % C == 0, per_batch(carry_ref[...]), pltpu.roll(p, 1, axis=0))
    for b in range(B):
        carry_ref[b] = p[b * C + C - 1:b * C + C, :]
    xs = p + (prev - p) * mu_ref[...]
    r = xs[:, 0:W]
    k = xs[:, W:2 * W]
    v = xs[:, 2 * W:3 * W]
    wl = xs[:, 3 * W:3 * W + W_LORA]
    al = xs[:, 3 * W + W_LORA:3 * W + W_LORA + A_LORA]
    gl = xs[:, 3 * W + W_LORA + A_LORA:]

    z = -(w0_ref[...] + _hdot(jnp.tanh(wl), w2_ref[...]))
    softplus = jnp.maximum(z, 0.0) + jnp.log1p(jnp.exp(-jnp.abs(z)))
    logd = -jnp.exp(-softplus - 0.5)
    a = _sigmoid(a0_ref[...] + _hdot(al, a2_ref[...]))
    g = _hdot(_sigmoid(gl), g2_ref[...])

    kk = k * kk_ref[...]
    knew = k * (1.0 + (a - 1.0) * ka_ref[...])

    def per_head(x, fn):
        return jnp.concatenate(
            [jnp.broadcast_to(fn(x[:, h * N:(h + 1) * N]), (R, N)) for h in range(H)], axis=-1)

    nrm = per_head(kk * kk, lambda t: jnp.sqrt(jnp.sum(t, axis=-1, keepdims=True)))
    kk = kk / jnp.maximum(nrm, 1e-12)
    lr_kk = kk * a

    ti = lax.broadcasted_iota(jnp.int32, (R, R), 0)
    tj = lax.broadcasted_iota(jnp.int32, (R, R), 1)
    same_chunk = (ti >= tj) & (ti // C == tj // C)
    cl = _hdot(same_chunk.astype(F32), logd)
    cl_end = per_batch(jnp.concatenate([cl[b * C + C - 1:b * C + C, :] for b in range(B)], axis=0))
    a_hat = -kk * jnp.exp(cl - logd)
    r_hat = r * jnp.exp(cl)
    inv_gam = jnp.exp(-cl)
    b_til = lr_kk * inv_gam
    k_til = knew * inv_gam
    to_end = jnp.exp(cl_end - cl)
    b_end = lr_kk * to_end
    k_end = knew * to_end
    gam_end = jnp.exp(cl_end)

    HG = RWKV_HEAD_GROUP
    GW = HG * N
    gt = lax.broadcasted_iota(jnp.int32, (C, GW), 0)
    gc = lax.broadcasted_iota(jnp.int32, (C, GW), 1) % N
    strict = gt > gc
    incl = gt >= gc
    eye = (gt == gc).astype(F32)
    bi = lax.broadcasted_iota(jnp.int32, (HG * C, GW), 0) // C
    bj = lax.broadcasted_iota(jnp.int32, (HG * C, GW), 1) // N
    same_head = bi == bj

    def block_diag(y):
        yb = y.astype(BF16)
        return jnp.where(same_head, jnp.concatenate([yb] * HG, axis=0), jnp.zeros((), BF16))

    def bd_dot(x, y):
        return jnp.dot(x.astype(BF16), block_diag(y), preferred_element_type=F32)

    def bd_dot_nt(x, y):
        return lax.dot_general(x.astype(BF16), block_diag(y), (((1,), (1,)), ((), ())),
                               preferred_element_type=F32)

    n_groups = H // HG
    units = [(b, gi) for b in range(B) for gi in range(n_groups)]
    n_units = range(len(units))
    cut = lambda x, b, gi: x[b * C:(b + 1) * C, gi * GW:(gi + 1) * GW]
    v_u = [cut(v, b, gi) for b, gi in units]
    ar = [jnp.concatenate([cut(a_hat, b, gi), cut(r_hat, b, gi)], axis=0) for b, gi in units]
    mb = [bd_dot_nt(ar[i], cut(b_til, *units[i])) for i in n_units]
    mk = [bd_dot_nt(ar[i], cut(k_til, *units[i])) for i in n_units]
    n_ab = [jnp.where(strict, mb[i][:C], 0.0) for i in n_units]
    m_rb = [jnp.where(incl, mb[i][C:], 0.0) for i in n_units]
    l_ak = [jnp.where(strict, mk[i][:C], 0.0) for i in n_units]
    m_rk = [jnp.where(incl, mk[i][C:], 0.0) for i in n_units]

    pw = list(n_ab)
    tinv = [eye + n_ab[i] for i in n_units]
    step = 2
    while step < C:
        pw = [bd_dot(pw[i], pw[i]) for i in n_units]
        tinv = [tinv[i] + bd_dot(tinv[i], pw[i]) for i in n_units]
        step *= 2

    s0 = [state_ref[i] for i in n_units]
    ars = [bd_dot_nt(ar[i], s0[i]) for i in n_units]
    lv = [bd_dot(l_ak[i], v_u[i]) for i in n_units]
    u = [bd_dot(tinv[i], ars[i][:C] + lv[i]) for i in n_units]
    outs = [ars[i][C:] + bd_dot(m_rb[i], u[i]) + bd_dot(m_rk[i], v_u[i]) for i in n_units]
    for i, (b, gi) in enumerate(units):
        uv = jnp.concatenate([u[i], v_u[i]], axis=0)
        bk_end = jnp.concatenate([cut(b_end, b, gi), cut(k_end, b, gi)], axis=0)
        cross = jnp.where(same_head, _bdot_tn(uv, bk_end), 0.0)
        upd = cross[0:N]
        for h in range(1, HG):
            upd = upd + cross[h * N:(h + 1) * N]
        state_ref[i] = s0[i] * gam_end[b * C:b * C + 1, gi * GW:(gi + 1) * GW] + upd

    o = jnp.concatenate([jnp.concatenate(outs[b * n_groups:(b + 1) * n_groups], axis=-1) for b in range(B)],
                        axis=0)
    mean = per_head(o, lambda t: jnp.mean(t, axis=-1, keepdims=True))
    var = per_head(jnp.square(o - mean), lambda t: jnp.mean(t, axis=-1, keepdims=True))
    o = (o - mean) * lax.rsqrt(var + GN_EPS) * lnw_ref[...] + lnb_ref[...]
    bonus = per_head(r * knew * rk_ref[...], lambda t: jnp.sum(t, axis=-1, keepdims=True)) * v
    o_ref[...] = ((o + bonus) * g).reshape(o_ref.shape)


def _hdot_nt(a, b):
    return lax.dot_general(a, b, (((1,), (1,)), ((), ())), precision=HIGHEST,
                           preferred_element_type=F32)


def _hdot_tn(a, b):
    return lax.dot_general(a, b, (((0,), (0,)), ((), ())), precision=HIGHEST,
                           preferred_element_type=F32)


def _rwkv_time_mix(p_a, mu, w0, w2, a0, a2, g2, k_k, k_a, r_k, lnx_w, lnx_b):
    B, S, _ = p_a.shape
    C = RWKV_CHUNK
    row = lambda t: t.reshape(1, -1)
    full = lambda shape: pl.BlockSpec(shape, lambda s: (0,) * len(shape))
    n_units = B * RWKV_HEADS // RWKV_HEAD_GROUP
    return pl.pallas_call(
        _rwkv_kernel,
        grid=(S // C,),
        in_specs=[pl.BlockSpec((B, C, RWKV_IN_W), lambda s: (0, s, 0)),
                  full((1, RWKV_IN_W)), full((1, RWKV_WIDTH)), full((W_LORA, RWKV_WIDTH)),
                  full((1, RWKV_WIDTH)), full((A_LORA, RWKV_WIDTH)), full((G_LORA, RWKV_WIDTH)),
                  full((1, RWKV_WIDTH)), full((1, RWKV_WIDTH)), full((1, RWKV_WIDTH)),
                  full((1, RWKV_WIDTH)), full((1, RWKV_WIDTH))],
        out_specs=pl.BlockSpec((B, C, RWKV_WIDTH), lambda s: (0, s, 0)),
        out_shape=jax.ShapeDtypeStruct((B, S, RWKV_WIDTH), F32),
        scratch_shapes=[pltpu.VMEM((B, 1, RWKV_IN_W), F32),
                        pltpu.VMEM((n_units, HEAD_DIM, RWKV_HEAD_GROUP * HEAD_DIM), F32)],
        compiler_params=pltpu.CompilerParams(
            dimension_semantics=("arbitrary",), vmem_limit_bytes=VMEM_LIMIT),
        name="rwkv7_chunked",
    )(p_a, row(mu), row(w0), w2, row(a0), a2, g2, row(k_k), row(k_a), row(r_k), row(lnx_w), row(lnx_b))


NSA_KV_TILE = 1024
SEL_KEY_TILE = 512
SEL_LANES = 128


def _rope_tables(pos, reps):
    half = HEAD_DIM // 2
    inv = ROPE_THETA ** (-jnp.arange(half, dtype=F32) / half)
    ang = pos.astype(F32)[:, None] * inv
    cos, sin = jnp.cos(ang), jnp.sin(ang)
    cosf = jnp.concatenate([cos, cos], -1)
    sinf = jnp.concatenate([-sin, sin], -1)
    return jnp.tile(cosf, (1, reps)), jnp.tile(sinf, (1, reps))


def _rope(x, cosf, sinf):
    width = x.shape[-1]
    lane = lax.broadcasted_iota(jnp.int32, x.shape, 1)
    first_half = (lane % HEAD_DIM) < HEAD_DIM // 2
    rot = jnp.where(first_half, pltpu.roll(x, width - HEAD_DIM // 2, axis=1),
                    pltpu.roll(x, HEAD_DIM // 2, axis=1))
    return x * cosf + rot * sinf


def _kv_layout_kernel(p_ref, cos_ref, sin_ref, kc_ref, vc_ref, ks_ref, vs_ref, kw_ref, vw_ref):
    outs = (kc_ref, vc_ref, ks_ref, vs_ref, kw_ref, vw_ref)
    roped = (False, False, True, False, True, False)
    for i, (o_ref, use_rope) in enumerate(zip(outs, roped)):
        t = p_ref[:, i * NSA_KV_WIDTH:(i + 1) * NSA_KV_WIDTH]
        if use_rope:
            t = _rope(t, cos_ref[...], sin_ref[...])
        for g in range(NSA_GROUPS):
            o_ref[g] = t[:, g * HEAD_DIM:(g + 1) * HEAD_DIM].astype(o_ref.dtype)


def _kv_layout(p_b, cos2, sin2):
    B, S, _ = p_b.shape
    ts = min(NSA_KV_TILE, S)
    out_spec = pl.BlockSpec((None, NSA_GROUPS, ts, HEAD_DIM), lambda b, s: (b, 0, s, 0))
    shp = lambda dt: jax.ShapeDtypeStruct((B, NSA_GROUPS, S, HEAD_DIM), dt)
    return pl.pallas_call(
        _kv_layout_kernel,
        grid=(B, S // ts),
        in_specs=[pl.BlockSpec((None, ts, 6 * NSA_KV_WIDTH), lambda b, s: (b, s, 0)),
                  pl.BlockSpec((ts, NSA_KV_WIDTH), lambda b, s: (s, 0)),
                  pl.BlockSpec((ts, NSA_KV_WIDTH), lambda b, s: (s, 0))],
        out_specs=[out_spec] * 6,
        out_shape=[shp(F32), shp(F32), shp(BF16), shp(BF16), shp(BF16), shp(BF16)],
        compiler_params=pltpu.CompilerParams(
            dimension_semantics=("parallel", "parallel"), vmem_limit_bytes=VMEM_LIMIT),
        name="nsa_kv_layout",
    )(p_b, cos2, sin2)


def _compress_kernel(subk_ref, subv_ref, pek_ref, w1k_ref, w2k_ref, pev_ref, w1v_ref, w2v_ref,
                     cos_ref, sin_ref, kc_ref, vc_ref):
    n_sub = subk_ref.shape[0]
    half = CMP_STRIDE * HEAD_DIM

    def mlp(sub_ref, pe_ref, w1_ref, w2_ref):
        sub = sub_ref[...]
        top = _bdot(sub, w1_ref[:half, :])
        bot = _bdot(sub, w1_ref[half:, :])
        bias = _bdot(jnp.broadcast_to(pe_ref[...], (8, 2 * half)), w1_ref[...])[0:1, :]
        h = top + pltpu.roll(bot, n_sub - 1, axis=0) + bias
        return _bdot(jax.nn.gelu(h), w2_ref[...])

    kc = mlp(subk_ref, pek_ref, w1k_ref, w2k_ref)
    rot = jnp.concatenate([kc[:, HEAD_DIM // 2:], kc[:, :HEAD_DIM // 2]], axis=-1)
    kc_ref[...] = (kc * cos_ref[...] + rot * sin_ref[...]).astype(kc_ref.dtype)
    vc_ref[...] = mlp(subv_ref, pev_ref, w1v_ref, w2v_ref).astype(vc_ref.dtype)


def _compress(subk, subv, pe_k, w1_k, w2_k, pe_v, w1_v, w2_v, cos_c, sin_c):
    B, G, n_sub, width = subk.shape
    sub_spec = pl.BlockSpec((None, None, n_sub, width), lambda b, g: (b, g, 0, 0))
    full = lambda a: pl.BlockSpec(a.shape, lambda b, g: (0,) * a.ndim)
    out_spec = pl.BlockSpec((None, None, n_sub, HEAD_DIM), lambda b, g: (b, g, 0, 0))
    pe_k, pe_v = pe_k.reshape(1, -1), pe_v.reshape(1, -1)
    args = (pe_k, w1_k, w2_k, pe_v, w1_v, w2_v, cos_c, sin_c)
    return pl.pallas_call(
        _compress_kernel,
        grid=(B, G),
        in_specs=[sub_spec, sub_spec] + [full(a) for a in args],
        out_specs=[out_spec, out_spec],
        out_shape=[jax.ShapeDtypeStruct((B, G, n_sub, HEAD_DIM), BF16)] * 2,
        compiler_params=pltpu.CompilerParams(
            dimension_semantics=("parallel", "parallel"), vmem_limit_bytes=VMEM_LIMIT),
        name="nsa_compress",
    )(subk, subv, *args)


MAX_FLOOR = -1e20


def _masked_exp_rows(s, bias):
    w = s.shape[-1]
    sm = (s.reshape(NSA_HPG, Q_BLOCK, w) + bias[None]).reshape(s.shape)
    m = jnp.maximum(jnp.max(sm, axis=-1, keepdims=True), MAX_FLOOR)
    e = jnp.exp(sm - m)
    den = jnp.sum(e, axis=-1, keepdims=True)
    return e, 1.0 / jnp.where(den > 0.0, den, 1.0)


def _nsa_kernel(q_ref, gate_ref, cos_ref, sin_ref, kc_ref, vc_ref, ks_ref, vs_ref, kw_ref, vw_ref,
                mselt_ref, o_ref, *, n_pick):
    QB, HP, D = Q_BLOCK, NSA_HPG, HEAD_DIM
    qb = pl.program_id(2)
    n_cmp = kc_ref.shape[0]
    scale = D ** -0.5

    q = _rope(q_ref[...], cos_ref[...], sin_ref[...]) * scale
    q4 = jnp.concatenate([q[:, n * D:(n + 1) * D] for n in range(HP)], axis=0).astype(BF16)
    t_col = qb * QB + lax.broadcasted_iota(jnp.int32, (QB, 1), 0)
    t_row = qb * QB + lax.broadcasted_iota(jnp.int32, (1, QB), 1)

    cmp_end = lax.broadcasted_iota(jnp.int32, (1, n_cmp), 1) * CMP_STRIDE + (CMP_BLOCK - 1)
    e_c, inv_c = _masked_exp_rows(_bdot_nt(q4, kc_ref[...]), jnp.where(cmp_end <= t_col, 0.0, NEG_INF))
    p_c = e_c * inv_c
    o_c = _bdot(p_c, vc_ref[...])
    p_sum = p_c[0:QB]
    for n in range(1, HP):
        p_sum = p_sum + p_c[n * QB:(n + 1) * QB]
    p_hi = p_sum.astype(BF16)
    p_lo = (p_sum - p_hi.astype(F32)).astype(BF16)
    imp_t = (lax.dot_general(mselt_ref[...], p_hi, (((1,), (1,)), ((), ())), preferred_element_type=F32)
             + lax.dot_general(mselt_ref[...], p_lo, (((1,), (1,)), ((), ())), preferred_element_type=F32))

    j = lax.broadcasted_iota(jnp.int32, (SEL_LANES, QB), 0)
    cur = t_row // SEL_BLOCK
    valid = j * SEL_BLOCK <= t_row
    forced = (j == 0) | (j == cur) | (j == cur - 1)
    score = jnp.where(valid, imp_t + jnp.where(forced, FORCE_BONUS, 0.0), -1.0)
    sel_t = jnp.zeros((SEL_LANES, QB), F32)
    for _ in range(n_pick):
        m = jnp.max(score, axis=0, keepdims=True)
        idx = jnp.min(jnp.where(score == m, j, SEL_LANES), axis=0, keepdims=True)
        hit = j == idx
        sel_t = jnp.where(hit & (m >= 0.0), 1.0, sel_t)
        score = jnp.where(hit, -2.0, score)
    sel_b = sel_t.T.astype(BF16)

    KT = SEL_KEY_TILE
    blocks_per_tile = KT // SEL_BLOCK
    n_tiles = (qb * QB + QB + KT - 1) // KT

    def sel_step(kt, carry):
        m_i, l_i, acc = carry
        start = pl.multiple_of(kt * KT, KT)
        k_t = ks_ref[pl.ds(start, KT), :]
        v_t = vs_ref[pl.ds(start, KT), :]
        jrow = lax.broadcasted_iota(jnp.int32, (SEL_LANES, KT), 0)
        ccol = lax.broadcasted_iota(jnp.int32, (SEL_LANES, KT), 1)
        expand = (jrow == kt * blocks_per_tile + ccol // SEL_BLOCK).astype(BF16)
        picked = jnp.dot(sel_b, expand, preferred_element_type=F32)
        kpos = start + lax.broadcasted_iota(jnp.int32, (1, KT), 1)
        bias = jnp.where((picked > 0.5) & (kpos <= t_col), 0.0, NEG_INF)
        s = _bdot_nt(q4, k_t)
        sm = (s.reshape(HP, QB, KT) + bias[None]).reshape(s.shape)
        m_new = jnp.maximum(m_i, jnp.max(sm, axis=-1, keepdims=True))
        e = jnp.exp(sm - m_new)
        alpha = jnp.exp(m_i - m_new)
        l_new = alpha * l_i + jnp.sum(e, axis=-1, keepdims=True)
        acc_new = alpha * acc + _bdot(e, v_t)
        return m_new, l_new, acc_new

    init = (jnp.full((HP * QB, 1), MAX_FLOOR, F32), jnp.zeros((HP * QB, 1), F32),
            jnp.zeros((HP * QB, D), F32))
    _, l_s, acc_s = lax.fori_loop(0, n_tiles, sel_step, init)
    o_s = acc_s * (1.0 / jnp.where(l_s > 0.0, l_s, 1.0))

    span = WINDOW + QB
    w_start = pl.multiple_of(jnp.maximum(qb * QB - WINDOW, 0), QB)
    dist = t_col - (w_start + lax.broadcasted_iota(jnp.int32, (1, span), 1))
    e_w, inv_w = _masked_exp_rows(_bdot_nt(q4, kw_ref[pl.ds(w_start, span), :]),
                                  jnp.where((dist >= 0) & (dist < WINDOW), 0.0, NEG_INF))
    o_w = _bdot(e_w, vw_ref[pl.ds(w_start, span), :]) * inv_w

    gates = _sigmoid(gate_ref[...])
    for n in range(HP):
        rows = slice(n * QB, (n + 1) * QB)
        o_ref[:, n * D:(n + 1) * D] = (gates[:, 3 * n:3 * n + 1] * o_c[rows]
                                       + gates[:, 3 * n + 1:3 * n + 2] * o_s[rows]
                                       + gates[:, 3 * n + 2:3 * n + 3] * o_w[rows])


def _cmp_to_sel_matrix(n_cmp_rows, n_sel):
    ratio = SEL_BLOCK // CMP_STRIDE
    ci = np.arange(n_cmp_rows)[:, None]
    sj = np.arange(SEL_LANES)[None, :]
    m = sum(((ci + n) // ratio == sj).astype(np.float32) for n in range(CMP_BLOCK // CMP_STRIDE))
    m = m * (sj < n_sel) * (ci < n_cmp_rows - 1)
    return jnp.asarray(m.T, BF16)


def _nsa_attention(p_b, kc, vc, ks, vs, kw, vw, cos_q, sin_q):
    B, S, _ = p_b.shape
    n_sub = kc.shape[2]
    n_sel = S // SEL_BLOCK
    gw = NSA_HPG * HEAD_DIM
    q_col0 = 6 * NSA_KV_WIDTH // gw
    gate_col0 = (6 * NSA_KV_WIDTH + NSA_WIDTH) // 128
    msel = _cmp_to_sel_matrix(n_sub, n_sel)
    cmp_spec = pl.BlockSpec((None, None, n_sub, HEAD_DIM), lambda b, g, i: (b, g, 0, 0))
    kv_spec = pl.BlockSpec((None, None, S, HEAD_DIM), lambda b, g, i: (b, g, 0, 0))
    return pl.pallas_call(
        functools.partial(_nsa_kernel, n_pick=min(N_SELECT, n_sel)),
        grid=(B, NSA_GROUPS, S // Q_BLOCK),
        in_specs=[pl.BlockSpec((None, Q_BLOCK, gw), lambda b, g, i: (b, i, q_col0 + g)),
                  pl.BlockSpec((None, Q_BLOCK, 128), lambda b, g, i: (b, i, gate_col0 + g)),
                  pl.BlockSpec((Q_BLOCK, gw), lambda b, g, i: (i, 0)),
                  pl.BlockSpec((Q_BLOCK, gw), lambda b, g, i: (i, 0)),
                  cmp_spec, cmp_spec, kv_spec, kv_spec, kv_spec, kv_spec,
                  pl.BlockSpec(msel.shape, lambda b, g, i: (0, 0))],
        out_specs=pl.BlockSpec((None, Q_BLOCK, gw), lambda b, g, i: (b, i, g)),
        out_shape=jax.ShapeDtypeStruct((B, S, NSA_WIDTH), F32),
        compiler_params=pltpu.CompilerParams(
            dimension_semantics=("parallel", "parallel", "arbitrary"), vmem_limit_bytes=VMEM_LIMIT),
        name="nsa_attention",
    )(p_b, p_b, cos_q, sin_q, kc, vc, ks, vs, kw, vw, msel)


def _nsa_branch(p_b, cmp_pe_k, cmp_w1_k, cmp_w2_k, cmp_pe_v, cmp_w1_v, cmp_w2_v):
    B, S, _ = p_b.shape
    pos = jnp.arange(S)
    cos2, sin2 = _rope_tables(pos, NSA_GROUPS)
    kc_raw, vc_raw, ks, vs, kw, vw = _kv_layout(p_b, cos2, sin2)
    n_sub = S // CMP_STRIDE
    sub = lambda t: t.reshape(B, NSA_GROUPS, n_sub, CMP_STRIDE * HEAD_DIM)
    cos_c, sin_c = _rope_tables(jnp.arange(n_sub) * CMP_STRIDE + CMP_BLOCK - 1, 1)
    kc, vc = _compress(sub(kc_raw), sub(vc_raw), cmp_pe_k, cmp_w1_k, cmp_w2_k,
                       cmp_pe_v, cmp_w1_v, cmp_w2_v, cos_c, sin_c)
    cos_q, sin_q = _rope_tables(pos, NSA_HPG)
    return _nsa_attention(p_b, kc, vc, ks, vs, kw, vw, cos_q, sin_q)


def _nsa_weight_columns(w_nsa):
    K = w_nsa.shape[0]
    q = w_nsa[:, :NSA_WIDTH]
    kv = w_nsa[:, NSA_WIDTH:NSA_WIDTH + 6 * NSA_KV_WIDTH]
    gates = w_nsa[:, NSA_WIDTH + 6 * NSA_KV_WIDTH:]
    per_group = NSA_HPG * 3
    gate_blocks = [jnp.pad(gates[:, g * per_group:(g + 1) * per_group], ((0, 0), (0, 128 - per_group)))
                   for g in range(NSA_GROUPS)]
    return jnp.concatenate([kv, q] + gate_blocks, axis=1)


def _layer_norm(h, g, b):
    mu = jnp.mean(h, axis=-1, keepdims=True)
    var = jnp.mean(jnp.square(h - mu), axis=-1, keepdims=True)
    return (h - mu) * lax.rsqrt(var + LN_EPS) * g + b


def _pack_bf16_halves(x):
    n = x.shape[-1] // 2
    bits = lax.bitcast_convert_type(x.astype(BF16).astype(F32), jnp.uint32)
    return (bits[:, n:] & jnp.uint32(0xFFFF0000)) | (bits[:, :n] >> 16)


def _unpack_bf16_halves(u):
    left = lax.bitcast_convert_type(u << 16, F32)
    right = lax.bitcast_convert_type(u & jnp.uint32(0xFFFF0000), F32)
    return left, right


def _mixer_out_kernel(x_ref, ya_ref, yb_ref, pg_ref, wa_ref, wb_ref, wo_ref, g_ref, b_ref, o_ref, op_ref,
                      *, alpha):
    d = x_ref.shape[-1]
    gate_a = _sigmoid(pg_ref[:, :d])
    gate_b = _sigmoid(pg_ref[:, d:])
    mixed = gate_a * _bdot(ya_ref[...], wa_ref[...]) + gate_b * _bdot(yb_ref[...], wb_ref[...])
    h = alpha * x_ref[...] + _bdot(mixed, wo_ref[...])
    out = _layer_norm(h, g_ref[...], b_ref[...])
    o_ref[...] = out
    op_ref[...] = _pack_bf16_halves(out)


def _mixer_out(xf, ya, yb, p_g, wa, wb, wo, ln_g, ln_b, alpha, tm=512):
    T, D = xf.shape
    rows = lambda w: pl.BlockSpec((tm, w), lambda i: (i, 0))
    full = lambda a: pl.BlockSpec(a.shape, lambda i: (0,) * a.ndim)
    ln_g, ln_b = ln_g.reshape(1, D), ln_b.reshape(1, D)
    return pl.pallas_call(
        functools.partial(_mixer_out_kernel, alpha=alpha),
        grid=(T // tm,),
        in_specs=[rows(D), rows(ya.shape[1]), rows(yb.shape[1]), rows(2 * D),
                  full(wa), full(wb), full(wo), full(ln_g), full(ln_b)],
        out_specs=[rows(D), rows(D // 2)],
        out_shape=[jax.ShapeDtypeStruct((T, D), F32), jax.ShapeDtypeStruct((T, D // 2), jnp.uint32)],
        compiler_params=pltpu.CompilerParams(
            dimension_semantics=("parallel",), vmem_limit_bytes=VMEM_LIMIT),
        name="mixer_out_ln",
    )(xf, ya, yb, p_g, wa, wb, wo, ln_g, ln_b)


ROUTER_TILE = 256
EXPERT_ROWS = 256
SC_TOKEN_CHUNK = 64
SC_ROW_CHUNK = 128
PICK_LANES = 128
LOWEST = -3.0e38


def _router_kernel(x_ref, rw_ref, bias_ref, eidx_ref, wts_ref, pos_ref, cnt_ref, carry_ref):
    tm, E = x_ref.shape[0], rw_ref.shape[1]
    per_group = E // N_GROUPS

    @pl.when(pl.program_id(0) == 0)
    def _():
        carry_ref[...] = jnp.zeros_like(carry_ref)

    scores = _sigmoid(_hdot(x_ref[...], rw_ref[...]))
    choice = scores + bias_ref[...]
    lane = lax.broadcasted_iota(jnp.int32, (tm, E), 1)
    grp = lane // per_group

    def first_max(vals):
        m = jnp.max(vals, axis=-1, keepdims=True)
        return m, jnp.min(jnp.where(vals == m, lane, E), axis=-1, keepdims=True)

    group_score = []
    for g in range(N_GROUPS):
        cg = jnp.where(grp == g, choice, LOWEST)
        m1, i1 = first_max(cg)
        m2 = jnp.max(jnp.where(lane == i1, LOWEST, cg), axis=-1, keepdims=True)
        group_score.append(m1 + m2)
    allowed = jnp.zeros((tm, E), jnp.bool_)
    for g in range(N_GROUPS):
        rank = jnp.zeros((tm, 1), jnp.int32)
        for o in range(N_GROUPS):
            if o != g:
                ahead = (group_score[o] > group_score[g]) if o > g else (group_score[o] >= group_score[g])
                rank = rank + ahead.astype(jnp.int32)
        allowed = allowed | ((grp == g) & (rank < TOPK_GROUPS))

    cur = jnp.where(allowed, choice, NEG_INF)
    sel = jnp.zeros((tm, E), F32)
    picks = []
    for _ in range(TOP_K):
        _, idx = first_max(cur)
        hit = lane == idx
        picks.append(idx)
        sel = jnp.where(hit, 1.0, sel)
        cur = jnp.where(hit, LOWEST, cur)
    gate = scores * sel
    gate = gate / jnp.sum(gate, axis=-1, keepdims=True) * ROUTED_SCALE

    ri = lax.broadcasted_iota(jnp.int32, (tm, tm), 0)
    ci = lax.broadcasted_iota(jnp.int32, (tm, tm), 1)
    before = jnp.dot((ri > ci).astype(BF16), sel.astype(BF16), preferred_element_type=F32)
    queue_pos = before + carry_ref[...]
    carry_ref[...] = carry_ref[...] + jnp.sum(sel, axis=0, keepdims=True)
    cnt_ref[...] = jnp.broadcast_to(carry_ref[...], cnt_ref.shape)

    out_lane = lax.broadcasted_iota(jnp.int32, (tm, PICK_LANES), 1)
    eidx = jnp.zeros((tm, PICK_LANES), jnp.int32)
    wts = jnp.zeros((tm, PICK_LANES), F32)
    pos = jnp.zeros((tm, PICK_LANES), F32)
    for kk, idx in enumerate(picks):
        hit = lane == idx
        eidx = jnp.where(out_lane == kk, idx, eidx)
        wts = jnp.where(out_lane == kk, jnp.sum(jnp.where(hit, gate, 0.0), axis=-1, keepdims=True), wts)
        pos = jnp.where(out_lane == kk, jnp.sum(jnp.where(hit, queue_pos, 0.0), axis=-1, keepdims=True), pos)
    eidx_ref[...] = eidx
    wts_ref[...] = wts
    pos_ref[...] = pos.astype(jnp.int32)


def _router(xf, router_w, router_bias):
    T, D = xf.shape
    E = router_w.shape[1]
    tm = ROUTER_TILE
    picks = lambda dt: jax.ShapeDtypeStruct((T, PICK_LANES), dt)
    pick_spec = pl.BlockSpec((tm, PICK_LANES), lambda i: (i, 0))
    return pl.pallas_call(
        _router_kernel,
        grid=(T // tm,),
        in_specs=[pl.BlockSpec((tm, D), lambda i: (i, 0)),
                  pl.BlockSpec((D, E), lambda i: (0, 0)),
                  pl.BlockSpec((1, E), lambda i: (0, 0))],
        out_specs=[pick_spec, pick_spec, pick_spec, pl.BlockSpec((8, E), lambda i: (0, 0))],
        out_shape=[picks(jnp.int32), picks(F32), picks(jnp.int32), jax.ShapeDtypeStruct((8, E), F32)],
        scratch_shapes=[pltpu.VMEM((1, E), F32)],
        compiler_params=pltpu.CompilerParams(
            dimension_semantics=("arbitrary",), vmem_limit_bytes=VMEM_LIMIT),
        name="moe_router",
    )(xf, router_w, router_bias.reshape(1, E))


def _dest_kernel(eidx_ref, pos_ref, start_ref, dest_ref):
    tm = eidx_ref.shape[0]
    E = start_ref.shape[1]
    lane = lax.broadcasted_iota(jnp.int32, (tm, E), 1)
    out_lane = lax.broadcasted_iota(jnp.int32, (tm, PICK_LANES), 1)
    eidx = eidx_ref[...]
    base = jnp.zeros((tm, PICK_LANES), jnp.int32)
    for kk in range(TOP_K):
        hit = lane == eidx[:, kk:kk + 1]
        start = jnp.sum(jnp.where(hit, start_ref[...], 0), axis=-1, keepdims=True)
        base = jnp.where(out_lane == kk, start, base)
    dest_ref[...] = base + pos_ref[...]


def _dest_rows(eidx, pos, pad_start):
    T = eidx.shape[0]
    E = pad_start.shape[0]
    tm = ROUTER_TILE
    pick_spec = pl.BlockSpec((tm, PICK_LANES), lambda i: (i, 0))
    return pl.pallas_call(
        _dest_kernel,
        grid=(T // tm,),
        in_specs=[pick_spec, pick_spec, pl.BlockSpec((1, E), lambda i: (0, 0))],
        out_specs=pick_spec,
        out_shape=jax.ShapeDtypeStruct((T, PICK_LANES), jnp.int32),
        compiler_params=pltpu.CompilerParams(
            dimension_semantics=("parallel",), vmem_limit_bytes=VMEM_LIMIT),
        name="moe_dest_rows",
    )(eidx, pos, pad_start.reshape(1, E))


def _sc_mesh():
    return plsc.VectorSubcoreMesh(core_axis_name="c", subcore_axis_name="s")


def _sc_scatter_rows(x, dest_t, n_rows):
    T, D = x.shape
    K = dest_t.shape[0]
    mesh = _sc_mesh()
    nc, nw = mesh.num_cores, mesh.num_cores * mesh.num_subcores
    per_w = T // nw
    chunk = min(SC_TOKEN_CHUNK, per_w)
    n_chunks = per_w // chunk
    idx = dest_t.reshape(K, nw, n_chunks, chunk).transpose(1, 2, 0, 3).reshape(nw, n_chunks * K, chunk)

    @functools.partial(
        pl.kernel, mesh=mesh,
        out_type=jax.ShapeDtypeStruct((n_rows, D), x.dtype),
        scratch_types=[pltpu.VMEM((n_chunks * K, chunk), jnp.int32),
                       pltpu.VMEM((chunk, D), x.dtype),
                       pltpu.SemaphoreType.DMA],
    )
    def scatter(x_hbm, idx_hbm, out_hbm, idx_v, rows_v, sem):
        wid = lax.axis_index("s") * nc + lax.axis_index("c")
        pltpu.sync_copy(idx_hbm.at[wid], idx_v)

        @pl.loop(0, n_chunks)
        def _(j):
            pltpu.sync_copy(x_hbm.at[pl.ds(wid * per_w + j * chunk, chunk)], rows_v)
            copies = [pltpu.async_copy(rows_v, out_hbm.at[idx_v.at[j * K + kk]], sem) for kk in range(K)]
            for c in copies:
                c.wait()

    return scatter(x, idx)


def _sc_gather_rows(src, idx):
    M = idx.shape[0]
    D = src.shape[1]
    mesh = _sc_mesh()
    nc, nw = mesh.num_cores, mesh.num_cores * mesh.num_subcores
    per_w = M // nw
    chunk = min(SC_ROW_CHUNK, per_w)
    n_chunks = per_w // chunk

    @functools.partial(
        pl.kernel, mesh=mesh,
        out_type=jax.ShapeDtypeStruct((M, D), src.dtype),
        scratch_types=[pltpu.VMEM((n_chunks, chunk), jnp.int32),
                       pltpu.VMEM((chunk, D), src.dtype),
                       pltpu.SemaphoreType.DMA],
    )
    def gather(src_hbm, idx_hbm, out_hbm, idx_v, rows_v, sem):
        wid = lax.axis_index("s") * nc + lax.axis_index("c")
        pltpu.sync_copy(idx_hbm.at[wid], idx_v)

        @pl.loop(0, n_chunks)
        def _(j):
            pltpu.async_copy(src_hbm.at[idx_v.at[j]], rows_v, sem).wait()
            pltpu.sync_copy(rows_v, out_hbm.at[pl.ds(wid * per_w + j * chunk, chunk)])

    return gather(src, idx.reshape(nw, n_chunks, chunk))


def _expert_kernel(distinct_e_ref, blk_ord_ref, blk_new_ref, blk_rows_ref, n_used_ref, n_distinct_ref,
                   x_ref, wgu_hbm, wd_hbm, o_ref, wgu_buf, wd_buf, wgu_bf, wd_bf, sem):
    i = pl.program_id(0)
    live = i < n_used_ref[0]
    ordinal = blk_ord_ref[i]
    slot = ordinal % 2

    def weight_copies(k, s):
        e = distinct_e_ref[k]
        return (pltpu.make_async_copy(wgu_hbm.at[e], wgu_buf.at[s], sem.at[0, s]),
                pltpu.make_async_copy(wd_hbm.at[e], wd_buf.at[s], sem.at[1, s]))

    @pl.when(i == 0)
    def _():
        for c in weight_copies(0, 0):
            c.start()

    @pl.when(live & (blk_new_ref[i] == 1))
    def _():
        for c in weight_copies(ordinal, slot):
            c.wait()

        @pl.when(ordinal + 1 < n_distinct_ref[0])
        def _():
            for c in weight_copies(ordinal + 1, 1 - slot):
                c.start()

        wgu_bf[...] = wgu_buf[slot].astype(BF16)
        wd_bf[...] = wd_buf[slot].astype(BF16)

    @pl.when(live)
    def _():
        hidden = wd_bf.shape[0]
        half = x_ref.shape[1]
        row = lax.broadcasted_iota(jnp.int32, x_ref.shape, 0)
        left, right = _unpack_bf16_halves(x_ref[...])
        real = row < blk_rows_ref[i]
        left = jnp.where(real, left, 0.0).astype(BF16)
        right = jnp.where(real, right, 0.0).astype(BF16)
        h = (jnp.dot(left, wgu_bf[:half, :], preferred_element_type=F32)
             + jnp.dot(right, wgu_bf[half:, :], preferred_element_type=F32))
        gate, up = h[:, :hidden], h[:, hidden:]
        act = (gate * _sigmoid(gate) * up).astype(BF16)
        o_ref[...] = _pack_bf16_halves(jnp.dot(act, wd_bf[...], preferred_element_type=F32))

    @pl.when(jnp.logical_not(live))
    def _():
        o_ref[...] = jnp.zeros_like(o_ref)


def _expert_ffn(xs, blk_e, blk_rows, n_used, w_gu, w_down):
    n_rows, half = xs.shape
    E, D, two_h = w_gu.shape
    n_blocks = n_rows // EXPERT_ROWS
    idx = jnp.arange(n_blocks, dtype=jnp.int32)
    is_live = idx < n_used[0]
    blk_new = (is_live & ((idx == 0) | (blk_e != jnp.roll(blk_e, 1)))).astype(jnp.int32)
    blk_ord = (jnp.cumsum(blk_new) - 1).astype(jnp.int32)
    n_distinct = blk_ord[-1:] + 1
    distinct_e = jnp.zeros((n_blocks,), jnp.int32).at[blk_ord].max(blk_e * is_live)

    live = lambda i, nu: jnp.minimum(i, nu[0] - 1)
    grid_spec = pltpu.PrefetchScalarGridSpec(
        num_scalar_prefetch=6,
        grid=(n_blocks,),
        in_specs=[pl.BlockSpec((EXPERT_ROWS, half), lambda i, de, bo, bn, br, nu, nd: (live(i, nu), 0)),
                  pl.BlockSpec(memory_space=pl.ANY), pl.BlockSpec(memory_space=pl.ANY)],
        out_specs=pl.BlockSpec((EXPERT_ROWS, half), lambda i, de, bo, bn, br, nu, nd: (i, 0)),
        scratch_shapes=[pltpu.VMEM((2, D, two_h), F32), pltpu.VMEM((2, two_h // 2, D), F32),
                        pltpu.VMEM((D, two_h), BF16), pltpu.VMEM((two_h // 2, D), BF16),
                        pltpu.SemaphoreType.DMA((2, 2))],
    )
    return pl.pallas_call(
        _expert_kernel,
        grid_spec=grid_spec,
        out_shape=jax.ShapeDtypeStruct((n_rows, half), jnp.uint32),
        compiler_params=pltpu.CompilerParams(
            dimension_semantics=("arbitrary",), vmem_limit_bytes=VMEM_LIMIT),
        name="moe_experts",
    )(distinct_e, blk_ord, blk_new, blk_rows, n_used, n_distinct, xs, w_gu, w_down)


def _moe_out_kernel(x_ref, yk_ref, wts_ref, sgu_ref, sd_ref, g_ref, b_ref, o_ref, *, alpha):
    x = x_ref[...]
    hidden = sd_ref.shape[0]
    h = _bdot(x, sgu_ref[...])
    gate, up = h[:, :hidden], h[:, hidden:]
    ffn = _bdot(gate * _sigmoid(gate) * up, sd_ref[...])
    wts = wts_ref[...]
    routed_left = routed_right = None
    for kk in range(TOP_K):
        left, right = _unpack_bf16_halves(yk_ref[kk])
        w = wts[:, kk:kk + 1]
        routed_left = w * left if kk == 0 else routed_left + w * left
        routed_right = w * right if kk == 0 else routed_right + w * right
    ffn = ffn + jnp.concatenate([routed_left, routed_right], axis=-1)
    o_ref[...] = _layer_norm(alpha * x + ffn, g_ref[...], b_ref[...])


def _moe_out(xf, yk, wts, sw_gu, sw_down, ln_g, ln_b, alpha, tm=128):
    T, D = xf.shape
    rows = lambda w: pl.BlockSpec((tm, w), lambda i: (i, 0))
    full = lambda a: pl.BlockSpec(a.shape, lambda i: (0,) * a.ndim)
    ln_g, ln_b = ln_g.reshape(1, D), ln_b.reshape(1, D)
    return pl.pallas_call(
        functools.partial(_moe_out_kernel, alpha=alpha),
        grid=(T // tm,),
        in_specs=[rows(D), pl.BlockSpec((TOP_K, tm, D // 2), lambda i: (0, i, 0)), rows(PICK_LANES),
                  full(sw_gu), full(sw_down), full(ln_g), full(ln_b)],
        out_specs=rows(D),
        out_shape=jax.ShapeDtypeStruct((T, D), F32),
        compiler_params=pltpu.CompilerParams(
            dimension_semantics=("parallel",), vmem_limit_bytes=VMEM_LIMIT),
        name="moe_combine_ln",
    )(xf, yk, wts, sw_gu, sw_down, ln_g, ln_b)


def _moe_ffn_ln(xf, xp, router_w, router_bias, w_gu, w_down, sw_gu, sw_down, ln_g, ln_b, alpha):
    T, D = xf.shape
    E = router_w.shape[1]
    BM = EXPERT_ROWS
    eidx, wts, pos, cnt = _router(xf, router_w, router_bias)
    counts = cnt[0].astype(jnp.int32)
    padded = (counts + BM - 1) // BM * BM
    pad_end = jnp.cumsum(padded)
    pad_start = pad_end - padded
    n_rows = T * TOP_K + E * BM
    n_blocks = n_rows // BM
    blk_row0 = jnp.arange(n_blocks, dtype=jnp.int32) * BM
    blk_e = jnp.minimum(jnp.sum((pad_end[None, :] <= blk_row0[:, None]).astype(jnp.int32), axis=1), E - 1)
    blk_rows = jnp.clip(pad_start[blk_e] + counts[blk_e] - blk_row0, 0, BM).astype(jnp.int32)
    n_used = (pad_end[-1:] // BM).astype(jnp.int32)
    dest = _dest_rows(eidx, pos, pad_start)[:, :TOP_K]
    dest_t = dest.T
    xs = _sc_scatter_rows(xp, dest_t, n_rows)
    ys = _expert_ffn(xs, blk_e, blk_rows, n_used, w_gu, w_down)
    yk = _sc_gather_rows(ys, dest_t.reshape(-1)).reshape(TOP_K, T, D // 2)
    return _moe_out(xf, yk, wts, sw_gu, sw_down, ln_g, ln_b, alpha)


def kernel(x, w_in, tshift_mu, rwkv_w0, rwkv_w2, rwkv_a0, rwkv_a2, rwkv_g2, rwkv_k_k, rwkv_k_a, rwkv_r_k, rwkv_lnx_w, rwkv_lnx_b, cmp_pe_k, cmp_w1_k, cmp_w2_k, cmp_pe_v, cmp_w1_v, cmp_w2_v, w_branch_a, w_branch_b, w_out, ln1_g, ln1_b, router_w, router_bias, exp_w_gu, exp_w_down, shared_w_gu, shared_w_down, ln2_g, ln2_b):
    B, S, D = x.shape
    depth = w_in.shape[0]
    alpha = (2 * depth) ** 0.25
    nsa_w = w_in.shape[2] - RWKV_IN_W - 2 * D
    for l in range(depth):
        xf = x.reshape(B * S, D)
        w_l = w_in[l]
        w_a = w_l[:, :RWKV_IN_W].astype(BF16)
        w_b = _nsa_weight_columns(w_l[:, RWKV_IN_W:RWKV_IN_W + nsa_w]).astype(BF16)
        w_g = w_l[:, RWKV_IN_W + nsa_w:].astype(BF16)
        p_a = _matmul(xf, w_a, 512, RWKV_IN_W // 2).reshape(B, S, -1)
        p_b = _matmul(xf, w_b, 512, w_b.shape[1] // 2).reshape(B, S, -1)
        p_g = _matmul(xf, w_g, 512, D)
        y_a = _rwkv_time_mix(p_a, tshift_mu[l], rwkv_w0[l], rwkv_w2[l], rwkv_a0[l], rwkv_a2[l], rwkv_g2[l],
                             rwkv_k_k[l], rwkv_k_a[l], rwkv_r_k[l].reshape(-1), rwkv_lnx_w[l], rwkv_lnx_b[l])
        y_b = _nsa_branch(p_b, cmp_pe_k[l], cmp_w1_k[l], cmp_w2_k[l], cmp_pe_v[l], cmp_w1_v[l], cmp_w2_v[l])
        x1, x1p = _mixer_out(xf, y_a.reshape(B * S, -1), y_b.reshape(B * S, -1), p_g,
                             w_branch_a[l].astype(BF16), w_branch_b[l].astype(BF16), w_out[l].astype(BF16),
                             ln1_g[l], ln1_b[l], alpha)
        x2 = _moe_ffn_ln(x1, x1p, router_w[l], router_bias[l], exp_w_gu[l], exp_w_down[l],
                         shared_w_gu[l].astype(BF16), shared_w_down[l].astype(BF16), ln2_g[l], ln2_b[l], alpha)
        x = x2.reshape(B, S, D)
    return x
```

```python
import functools

import numpy as np
import jax
import jax.numpy as jnp
from jax import lax
from jax.experimental import pallas as pl
from jax.experimental.pallas import tpu as pltpu
from jax.experimental.pallas import tpu_sc as plsc

F32 = jnp.float32
BF16 = jnp.bfloat16
HIGHEST = lax.Precision.HIGHEST

RWKV_HEADS = 8
HEAD_DIM = 64
RWKV_WIDTH = RWKV_HEADS * HEAD_DIM
W_LORA = 64
A_LORA = 64
G_LORA = 128
GN_EPS = 64e-5
NSA_HEADS = 8
NSA_GROUPS = 2
NSA_HPG = NSA_HEADS // NSA_GROUPS
NSA_WIDTH = NSA_HEADS * HEAD_DIM
NSA_KV_WIDTH = NSA_GROUPS * HEAD_DIM
CMP_BLOCK = 32
CMP_STRIDE = 16
CMP_HIDDEN = 256
SEL_BLOCK = 64
N_SELECT = 16
WINDOW = 512
Q_BLOCK = 128
ROPE_THETA = 10000.0
RWKV_IN_W = 3 * RWKV_WIDTH + W_LORA + A_LORA + G_LORA
N_EXPERTS = 256
TOP_K = 8
N_GROUPS = 8
TOPK_GROUPS = 4
EXPERT_DIM = 256
ROUTED_SCALE = 2.5
LN_EPS = 1e-5
NEG_INF = -1e30
FORCE_BONUS = 1e4

RWKV_CHUNK = 64
RWKV_HEAD_GROUP = 4
VMEM_LIMIT = 56 * 1024 * 1024


def _bdot(a, b):
    return jnp.dot(a.astype(BF16), b.astype(BF16), preferred_element_type=F32)


def _bdot_nt(a, b):
    return lax.dot_general(a.astype(BF16), b.astype(BF16), (((1,), (1,)), ((), ())),
                           preferred_element_type=F32)


def _bdot_tn(a, b):
    return lax.dot_general(a.astype(BF16), b.astype(BF16), (((0,), (0,)), ((), ())),
                           preferred_element_type=F32)


def _hdot(a, b):
    return jnp.dot(a, b, precision=HIGHEST, preferred_element_type=F32)


def _sigmoid(x):
    return 1.0 / (1.0 + jnp.exp(-x))


def _matmul_kernel(x_ref, w_ref, o_ref):
    o_ref[...] = jnp.dot(x_ref[...].astype(BF16), w_ref[...], preferred_element_type=F32)


def _matmul(x, w, tm, tn):
    M, K = x.shape
    N = w.shape[1]
    return pl.pallas_call(
        _matmul_kernel,
        grid=(M // tm, N // tn),
        in_specs=[pl.BlockSpec((tm, K), lambda i, j: (i, 0)),
                  pl.BlockSpec((K, tn), lambda i, j: (0, j))],
        out_specs=pl.BlockSpec((tm, tn), lambda i, j: (i, j)),
        out_shape=jax.ShapeDtypeStruct((M, N), F32),
        compiler_params=pltpu.CompilerParams(
            dimension_semantics=("parallel", "parallel"), vmem_limit_bytes=VMEM_LIMIT),
        name="dense_proj",
    )(x, w)


def _rwkv_kernel(p_ref, mu_ref, w0_ref, w2_ref, a0_ref, a2_ref, g2_ref, kk_ref, ka_ref, rk_ref,
                 lnw_ref, lnb_ref, o_ref, carry_ref, state_ref):
    C, H, N = RWKV_CHUNK, RWKV_HEADS, HEAD_DIM
    W = RWKV_WIDTH
    B = p_ref.shape[0]
    R = B * C

    @pl.when(pl.program_id(0) == 0)
    def _():
        carry_ref[...] = jnp.zeros_like(carry_ref)
        state_ref[...] = jnp.zeros_like(state_ref)

    def per_batch(x):
        return jnp.concatenate([jnp.broadcast_to(x[b].reshape(1, -1), (C, x.shape[-1])) for b in range(B)],
                               axis=0)

    p = p_ref[...].reshape(R, p_ref.shape[-1])
    row = lax.broadcasted_iota(jnp.int32, p.shape, 0)
    prev = jnp.where(row % C == 0, per_batch(carry_ref[...]), pltpu.roll(p, 1, axis=0))
    for b in range(B):
        carry_ref[b] = p[b * C + C - 1:b * C + C, :]
    xs = p + (prev - p) * mu_ref[...]
    r = xs[:, 0:W]
    k = xs[:, W:2 * W]
    v = xs[:, 2 * W:3 * W]
    wl = xs[:, 3 * W:3 * W + W_LORA]
    al = xs[:, 3 * W + W_LORA:3 * W + W_LORA + A_LORA]
    gl = xs[:, 3 * W + W_LORA + A_LORA:]

    z = -(w0_ref[...] + _hdot(jnp.tanh(wl), w2_ref[...]))
    softplus = jnp.maximum(z, 0.0) + jnp.log1p(jnp.exp(-jnp.abs(z)))
    logd = -jnp.exp(-softplus - 0.5)
    a = _sigmoid(a0_ref[...] + _hdot(al, a2_ref[...]))
    g = _hdot(_sigmoid(gl), g2_ref[...])

    kk = k * kk_ref[...]
    knew = k * (1.0 + (a - 1.0) * ka_ref[...])

    def per_head(x, fn):
        return jnp.concatenate(
            [jnp.broadcast_to(fn(x[:, h * N:(h + 1) * N]), (R, N)) for h in range(H)], axis=-1)

    nrm = per_head(kk * kk, lambda t: jnp.sqrt(jnp.sum(t, axis=-1, keepdims=True)))
    kk = kk / jnp.maximum(nrm, 1e-12)
    lr_kk = kk * a

    ti = lax.broadcasted_iota(jnp.int32, (R, R), 0)
    tj = lax.broadcasted_iota(jnp.int32, (R, R), 1)
    same_chunk = (ti >= tj) & (ti // C == tj // C)
    cl = _hdot(same_chunk.astype(F32), logd)
    cl_end = per_batch(jnp.concatenate([cl[b * C + C - 1:b * C + C, :] for b in range(B)], axis=0))
    a_hat = -kk * jnp.exp(cl - logd)
    r_hat = r * jnp.exp(cl)
    inv_gam = jnp.exp(-cl)
    b_til = lr_kk * inv_gam
    k_til = knew * inv_gam
    to_end = jnp.exp(cl_end - cl)
    b_end = lr_kk * to_end
    k_end = knew * to_end
    gam_end = jnp.exp(cl_end)

    HG = RWKV_HEAD_GROUP
    GW = HG * N
    gt = lax.broadcasted_iota(jnp.int32, (C, GW), 0)
    gc = lax.broadcasted_iota(jnp.int32, (C, GW), 1) % N
    strict = gt > gc
    incl = gt >= gc
    eye = (gt == gc).astype(F32)
    bi = lax.broadcasted_iota(jnp.int32, (HG * C, GW), 0) // C
    bj = lax.broadcasted_iota(jnp.int32, (HG * C, GW), 1) // N
    same_head = bi == bj

    def block_diag(y):
        yb = y.astype(BF16)
        return jnp.where(same_head, jnp.concatenate([yb] * HG, axis=0), jnp.zeros((), BF16))

    def bd_dot(x, y):
        return jnp.dot(x.astype(BF16), block_diag(y), preferred_element_type=F32)

    def bd_dot_nt(x, y):
        return lax.dot_general(x.astype(BF16), block_diag(y), (((1,), (1,)), ((), ())),
                               preferred_element_type=F32)

    n_groups = H // HG
    units = [(b, gi) for b in range(B) for gi in range(n_groups)]
    n_units = range(len(units))
    cut = lambda x, b, gi: x[b * C:(b + 1) * C, gi * GW:(gi + 1) * GW]
    v_u = [cut(v, b, gi) for b, gi in units]
    ar = [jnp.concatenate([cut(a_hat, b, gi), cut(r_hat, b, gi)], axis=0) for b, gi in units]
    mb = [bd_dot_nt(ar[i], cut(b_til, *units[i])) for i in n_units]
    mk = [bd_dot_nt(ar[i], cut(k_til, *units[i])) for i in n_units]
    n_ab = [jnp.where(strict, mb[i][:C], 0.0) for i in n_units]
    m_rb = [jnp.where(incl, mb[i][C:], 0.0) for i in n_units]
    l_ak = [jnp.where(strict, mk[i][:C], 0.0) for i in n_units]
    m_rk = [jnp.where(incl, mk[i][C:], 0.0) for i in n_units]

    pw = list(n_ab)
    tinv = [eye + n_ab[i] for i in n_units]
    step = 2
    while step < C:
        pw = [bd_dot(pw[i], pw[i]) for i in n_units]
        tinv = [tinv[i] + bd_dot(tinv[i], pw[i]) for i in n_units]
        step *= 2

    s0 = [state_ref[i] for i in n_units]
    ars = [bd_dot_nt(ar[i], s0[i]) for i in n_units]
    lv = [bd_dot(l_ak[i], v_u[i]) for i in n_units]
    u = [bd_dot(tinv[i], ars[i][:C] + lv[i]) for i in n_units]
    outs = [ars[i][C:] + bd_dot(m_rb[i], u[i]) + bd_dot(m_rk[i], v_u[i]) for i in n_units]
    for i, (b, gi) in enumerate(units):
        uv = jnp.concatenate([u[i], v_u[i]], axis=0)
        bk_end = jnp.concatenate([cut(b_end, b, gi), cut(k_end, b, gi)], axis=0)
        cross = jnp.where(same_head, _bdot_tn(uv, bk_end), 0.0)
        upd = cross[0:N]
        for h in range(1, HG):
            upd = upd + cross[h * N:(h + 1) * N]
        state_ref[i] = s0[i] * gam_end[b * C:b * C + 1, gi * GW:(gi + 1) * GW] + upd

    o = jnp.concatenate([jnp.concatenate(outs[b * n_groups:(b + 1) * n_groups], axis=-1) for b in range(B)],
                        axis=0)
    mean = per_head(o, lambda t: jnp.mean(t, axis=-1, keepdims=True))
    var = per_head(jnp.square(o - mean), lambda t: jnp.mean(t, axis=-1, keepdims=True))
    o = (o - mean) * lax.rsqrt(var + GN_EPS) * lnw_ref[...] + lnb_ref[...]
    bonus = per_head(r * knew * rk_ref[...], lambda t: jnp.sum(t, axis=-1, keepdims=True)) * v
    o_ref[...] = ((o + bonus) * g).reshape(o_ref.shape)


def _hdot_nt(a, b):
    return lax.dot_general(a, b, (((1,), (1,)), ((), ())), precision=HIGHEST,
                           preferred_element_type=F32)


def _hdot_tn(a, b):
    return lax.dot_general(a, b, (((0,), (0,)), ((), ())), precision=HIGHEST,
                           preferred_element_type=F32)


def _rwkv_time_mix(p_a, mu, w0, w2, a0, a2, g2, k_k, k_a, r_k, lnx_w, lnx_b):
    B, S, _ = p_a.shape
    C = RWKV_CHUNK
    row = lambda t: t.reshape(1, -1)
    full = lambda shape: pl.BlockSpec(shape, lambda s: (0,) * len(shape))
    n_units = B * RWKV_HEADS // RWKV_HEAD_GROUP
    return pl.pallas_call(
        _rwkv_kernel,
        grid=(S // C,),
        in_specs=[pl.BlockSpec((B, C, RWKV_IN_W), lambda s: (0, s, 0)),
                  full((1, RWKV_IN_W)), full((1, RWKV_WIDTH)), full((W_LORA, RWKV_WIDTH)),
                  full((1, RWKV_WIDTH)), full((A_LORA, RWKV_WIDTH)), full((G_LORA, RWKV_WIDTH)),
                  full((1, RWKV_WIDTH)), full((1, RWKV_WIDTH)), full((1, RWKV_WIDTH)),
                  full((1, RWKV_WIDTH)), full((1, RWKV_WIDTH))],
        out_specs=pl.BlockSpec((B, C, RWKV_WIDTH), lambda s: (0, s, 0)),
        out_shape=jax.ShapeDtypeStruct((B, S, RWKV_WIDTH), F32),
        scratch_shapes=[pltpu.VMEM((B, 1, RWKV_IN_W), F32),
                        pltpu.VMEM((n_units, HEAD_DIM, RWKV_HEAD_GROUP * HEAD_DIM), F32)],
        compiler_params=pltpu.CompilerParams(
            dimension_semantics=("arbitrary",), vmem_limit_bytes=VMEM_LIMIT),
        name="rwkv7_chunked",
    )(p_a, row(mu), row(w0), w2, row(a0), a2, g2, row(k_k), row(k_a), row(r_k), row(lnx_w), row(lnx_b))


NSA_KV_TILE = 1024
SEL_KEY_TILE = 1024
SEL_LANES = 128


def _rope_tables(pos, reps):
    half = HEAD_DIM // 2
    inv = ROPE_THETA ** (-jnp.arange(half, dtype=F32) / half)
    ang = pos.astype(F32)[:, None] * inv
    cos, sin = jnp.cos(ang), jnp.sin(ang)
    cosf = jnp.concatenate([cos, cos], -1)
    sinf = jnp.concatenate([-sin, sin], -1)
    return jnp.tile(cosf, (1, reps)), jnp.tile(sinf, (1, reps))


def _rope(x, cosf, sinf):
    width = x.shape[-1]
    lane = lax.broadcasted_iota(jnp.int32, x.shape, 1)
    first_half = (lane % HEAD_DIM) < HEAD_DIM // 2
    rot = jnp.where(first_half, pltpu.roll(x, width - HEAD_DIM // 2, axis=1),
                    pltpu.roll(x, HEAD_DIM // 2, axis=1))
    return x * cosf + rot * sinf


def _kv_layout_kernel(p_ref, cos_ref, sin_ref, kc_ref, vc_ref, ks_ref, vs_ref, kw_ref, vw_ref):
    ts = p_ref.shape[0]
    for i, o_ref in ((0, kc_ref), (1, vc_ref), (2, ks_ref), (4, kw_ref)):
        t = p_ref[:, i * NSA_KV_WIDTH:(i + 1) * NSA_KV_WIDTH]
        if i >= 2:
            t = _rope(t, cos_ref[...], sin_ref[...])
        for g in range(NSA_GROUPS):
            o_ref[g] = t[:, g * HEAD_DIM:(g + 1) * HEAD_DIM].astype(o_ref.dtype)
    pad_row = lax.broadcasted_iota(jnp.int32, (VT_ROWS - HEAD_DIM, ts), 0)
    ones_row = jnp.where(pad_row == 0, 1.0, 0.0)
    for i, o_ref in ((3, vs_ref), (5, vw_ref)):
        t_t = p_ref[:, i * NSA_KV_WIDTH:(i + 1) * NSA_KV_WIDTH].T
        for g in range(NSA_GROUPS):
            o_ref[g] = jnp.concatenate([t_t[g * HEAD_DIM:(g + 1) * HEAD_DIM], ones_row],
                                       axis=0).astype(o_ref.dtype)


def _kv_layout(p_b, cos2, sin2):
    B, S, _ = p_b.shape
    ts = min(NSA_KV_TILE, S)
    out_spec = pl.BlockSpec((None, NSA_GROUPS, ts, HEAD_DIM), lambda b, s: (b, 0, s, 0))
    vt_spec = pl.BlockSpec((None, NSA_GROUPS, VT_ROWS, ts), lambda b, s: (b, 0, 0, s))
    shp = lambda dt: jax.ShapeDtypeStruct((B, NSA_GROUPS, S, HEAD_DIM), dt)
    vt_shp = jax.ShapeDtypeStruct((B, NSA_GROUPS, VT_ROWS, S), BF16)
    return pl.pallas_call(
        _kv_layout_kernel,
        grid=(B, S // ts),
        in_specs=[pl.BlockSpec((None, ts, 6 * NSA_KV_WIDTH), lambda b, s: (b, s, 0)),
                  pl.BlockSpec((ts, NSA_KV_WIDTH), lambda b, s: (s, 0)),
                  pl.BlockSpec((ts, NSA_KV_WIDTH), lambda b, s: (s, 0))],
        out_specs=[out_spec, out_spec, out_spec, vt_spec, out_spec, vt_spec],
        out_shape=[shp(F32), shp(F32), shp(BF16), vt_shp, shp(BF16), vt_shp],
        compiler_params=pltpu.CompilerParams(
            dimension_semantics=("parallel", "parallel"), vmem_limit_bytes=VMEM_LIMIT),
        name="nsa_kv_layout",
    )(p_b, cos2, sin2)


def _compress_kernel(subk_ref, subv_ref, pek_ref, w1k_ref, w2k_ref, pev_ref, w1v_ref, w2v_ref,
                     cos_ref, sin_ref, kc_ref, vc_ref):
    n_sub = subk_ref.shape[0]
    half = CMP_STRIDE * HEAD_DIM

    def mlp(sub_ref, pe_ref, w1_ref, w2_ref):
        sub = sub_ref[...]
        top = _bdot(sub, w1_ref[:half, :])
        bot = _bdot(sub, w1_ref[half:, :])
        bias = _bdot(jnp.broadcast_to(pe_ref[...], (8, 2 * half)), w1_ref[...])[0:1, :]
        h = top + pltpu.roll(bot, n_sub - 1, axis=0) + bias
        return _bdot(jax.nn.gelu(h), w2_ref[...])

    kc = mlp(subk_ref, pek_ref, w1k_ref, w2k_ref)
    rot = jnp.concatenate([kc[:, HEAD_DIM // 2:], kc[:, :HEAD_DIM // 2]], axis=-1)
    kc_ref[...] = (kc * cos_ref[...] + rot * sin_ref[...]).astype(kc_ref.dtype)
    vc_ref[...] = mlp(subv_ref, pev_ref, w1v_ref, w2v_ref).astype(vc_ref.dtype)


def _compress(subk, subv, pe_k, w1_k, w2_k, pe_v, w1_v, w2_v, cos_c, sin_c):
    B, G, n_sub, width = subk.shape
    sub_spec = pl.BlockSpec((None, None, n_sub, width), lambda b, g: (b, g, 0, 0))
    full = lambda a: pl.BlockSpec(a.shape, lambda b, g: (0,) * a.ndim)
    out_spec = pl.BlockSpec((None, None, n_sub, HEAD_DIM), lambda b, g: (b, g, 0, 0))
    pe_k, pe_v = pe_k.reshape(1, -1), pe_v.reshape(1, -1)
    args = (pe_k, w1_k, w2_k, pe_v, w1_v, w2_v, cos_c, sin_c)
    return pl.pallas_call(
        _compress_kernel,
        grid=(B, G),
        in_specs=[sub_spec, sub_spec] + [full(a) for a in args],
        out_specs=[out_spec, out_spec],
        out_shape=[jax.ShapeDtypeStruct((B, G, n_sub, HEAD_DIM), BF16)] * 2,
        compiler_params=pltpu.CompilerParams(
            dimension_semantics=("parallel", "parallel"), vmem_limit_bytes=VMEM_LIMIT),
        name="nsa_compress",
    )(subk, subv, *args)


MAX_FLOOR = -1e20
MASK_BIG = 2.0 ** 100
LOG2_E = 1.4426950408889634
VT_ROWS = 80


def _nsa_kernel(q_ref, gate_ref, cos_ref, sin_ref, kc_ref, vc_ref, ks_ref, vst_ref, kw_ref, vwt_ref,
                mselt_ref, o_ref, blockbias_ref, *, n_pick):
    QB, HP, D = Q_BLOCK, NSA_HPG, HEAD_DIM
    qb = pl.program_id(2)
    n_cmp = kc_ref.shape[0]
    lanes4 = lambda x: jnp.concatenate([x] * HP, axis=1)

    heads = []
    for n in range(HP):
        qh = q_ref[n * D:(n + 1) * D, :]
        rot = jnp.concatenate([qh[D // 2:], qh[:D // 2]], axis=0)
        heads.append(qh * cos_ref[...] + rot * sin_ref[...])
    q4 = (jnp.concatenate(heads, axis=1) * (D ** -0.5 * LOG2_E)).astype(BF16)
    t_row = qb * QB + lax.broadcasted_iota(jnp.int32, (1, QB), 1)

    def softmax_cols(s_t, bias_t):
        sm = s_t + lanes4(bias_t)
        m = jnp.maximum(jnp.max(sm, axis=0, keepdims=True), MAX_FLOOR)
        return jnp.exp2(sm - m)

    cmp_end = lax.broadcasted_iota(jnp.int32, (n_cmp, 1), 0) * CMP_STRIDE + (CMP_BLOCK - 1)
    e_c = softmax_cols(jnp.dot(kc_ref[...], q4, preferred_element_type=F32),
                       jnp.where(cmp_end <= t_row, 0.0, -MASK_BIG))
    den_c = jnp.sum(e_c, axis=0, keepdims=True)
    p_c = e_c * (1.0 / jnp.where(den_c > 0.0, den_c, 1.0))
    o_c = _bdot_tn(vc_ref[...], p_c)
    p_sum = p_c[:, 0:QB]
    for n in range(1, HP):
        p_sum = p_sum + p_c[:, n * QB:(n + 1) * QB]
    p_hi = p_sum.astype(BF16)
    p_lo = (p_sum - p_hi.astype(F32)).astype(BF16)
    imp_t = (jnp.dot(mselt_ref[...], p_hi, preferred_element_type=F32)
             + jnp.dot(mselt_ref[...], p_lo, preferred_element_type=F32))

    j = lax.broadcasted_iota(jnp.int32, (SEL_LANES, QB), 0)
    cur = t_row // SEL_BLOCK
    valid = j * SEL_BLOCK <= t_row
    forced = (j == 0) | (j == cur) | (j == cur - 1)
    score = jnp.where(valid, imp_t + jnp.where(forced, FORCE_BONUS, 0.0), -1.0)
    sel_t = jnp.full((SEL_LANES, QB), -MASK_BIG, F32)
    for _ in range(n_pick):
        m = jnp.max(score, axis=0, keepdims=True)
        idx = jnp.min(jnp.where(score == m, j, SEL_LANES), axis=0, keepdims=True)
        hit = j == idx
        sel_t = jnp.where(hit & (m >= 0.0), 0.0, sel_t)
        score = jnp.where(hit, -2.0, score)
    blockbias_ref[...] = sel_t

    KT = SEL_KEY_TILE
    blocks_per_tile = KT // SEL_BLOCK
    n_tiles = (qb * QB + QB + KT - 1) // KT

    def sel_step(kt, carry, causal):
        m_i, acc = carry
        start = pl.multiple_of(kt * KT, KT)
        s_t = jnp.dot(ks_ref[pl.ds(start, KT), :], q4, preferred_element_type=F32)
        bias = jnp.concatenate(
            [jnp.broadcast_to(blockbias_ref[pl.ds(kt * blocks_per_tile + jb, 1), :], (SEL_BLOCK, QB))
             for jb in range(blocks_per_tile)], axis=0)
        if causal:
            kpos = start + lax.broadcasted_iota(jnp.int32, (KT, 1), 0)
            bias = jnp.where(kpos <= t_row, bias, -MASK_BIG)
        sm = s_t + lanes4(bias)
        m_new = jnp.maximum(m_i, jnp.max(sm, axis=0, keepdims=True))
        e = jnp.exp2(sm - m_new).astype(BF16)
        acc_new = jnp.exp2(m_i - m_new) * acc + jnp.dot(vst_ref[:, pl.ds(start, KT)], e,
                                                        preferred_element_type=F32)
        return m_new, acc_new

    init = (jnp.full((1, HP * QB), MAX_FLOOR, F32), jnp.zeros((VT_ROWS, HP * QB), F32))
    carry = lax.fori_loop(0, n_tiles - 1, lambda kt, c: sel_step(kt, c, False), init)
    _, acc_s = sel_step(n_tiles - 1, carry, True)
    den_s = acc_s[D:D + 1]
    o_s = acc_s[:D] * (1.0 / jnp.where(den_s > 0.0, den_s, 1.0))

    span = WINDOW + QB
    w_start = pl.multiple_of(jnp.maximum(qb * QB - WINDOW, 0), QB)
    dist = t_row - (w_start + lax.broadcasted_iota(jnp.int32, (span, 1), 0))
    e_w = softmax_cols(jnp.dot(kw_ref[pl.ds(w_start, span), :], q4, preferred_element_type=F32),
                       jnp.where((dist >= 0) & (dist < WINDOW), 0.0, -MASK_BIG))
    acc_w = jnp.dot(vwt_ref[:, pl.ds(w_start, span)], e_w.astype(BF16), preferred_element_type=F32)
    den_w = acc_w[D:D + 1]
    o_w = acc_w[:D] * (1.0 / jnp.where(den_w > 0.0, den_w, 1.0))

    gates = _sigmoid(gate_ref[...])
    gate_row = lambda br: jnp.concatenate([gates[3 * n + br:3 * n + br + 1, :] for n in range(HP)], axis=1)
    o_t = gate_row(0) * o_c + gate_row(1) * o_s + gate_row(2) * o_w
    for n in range(HP):
        o_ref[:, n * D:(n + 1) * D] = o_t[:, n * QB:(n + 1) * QB].T


def _cmp_to_sel_matrix(n_cmp_rows, n_sel):
    ratio = SEL_BLOCK // CMP_STRIDE
    ci = np.arange(n_cmp_rows)[:, None]
    sj = np.arange(SEL_LANES)[None, :]
    m = sum(((ci + n) // ratio == sj).astype(np.float32) for n in range(CMP_BLOCK // CMP_STRIDE))
    m = m * (sj < n_sel) * (ci < n_cmp_rows - 1)
    return jnp.asarray(m.T, BF16)


def _nsa_attention(qg_t, kc, vc, ks, vst, kw, vwt, cos_t, sin_t):
    B, _, S = qg_t.shape
    n_sub = kc.shape[2]
    n_sel = S // SEL_BLOCK
    gw = NSA_HPG * HEAD_DIM
    gate_row0 = NSA_WIDTH // 128
    msel_t = _cmp_to_sel_matrix(n_sub, n_sel)
    at_bg = lambda shape: pl.BlockSpec((None, None) + shape, lambda b, g, i: (b, g, 0, 0))
    const = lambda a: pl.BlockSpec(a.shape, lambda b, g, i: (0, 0))
    return pl.pallas_call(
        functools.partial(_nsa_kernel, n_pick=min(N_SELECT, n_sel)),
        grid=(B, NSA_GROUPS, S // Q_BLOCK),
        in_specs=[pl.BlockSpec((None, gw, Q_BLOCK), lambda b, g, i: (b, g, i)),
                  pl.BlockSpec((None, 128, Q_BLOCK), lambda b, g, i: (b, gate_row0 + g, i)),
                  pl.BlockSpec((HEAD_DIM, Q_BLOCK), lambda b, g, i: (0, i)),
                  pl.BlockSpec((HEAD_DIM, Q_BLOCK), lambda b, g, i: (0, i)),
                  at_bg((n_sub, HEAD_DIM)), at_bg((n_sub, HEAD_DIM)),
                  at_bg((S, HEAD_DIM)), at_bg((VT_ROWS, S)), at_bg((S, HEAD_DIM)), at_bg((VT_ROWS, S)),
                  const(msel_t)],
        out_specs=pl.BlockSpec((None, Q_BLOCK, gw), lambda b, g, i: (b, i, g)),
        out_shape=jax.ShapeDtypeStruct((B, S, NSA_WIDTH), F32),
        scratch_shapes=[pltpu.VMEM((SEL_LANES, Q_BLOCK), F32)],
        compiler_params=pltpu.CompilerParams(
            dimension_semantics=("parallel", "parallel", "arbitrary"), vmem_limit_bytes=VMEM_LIMIT),
        name="nsa_attention",
    )(qg_t, qg_t, cos_t, sin_t, kc, vc, ks, vst, kw, vwt, msel_t)


def _nsa_branch(p_b, cmp_pe_k, cmp_w1_k, cmp_w2_k, cmp_pe_v, cmp_w1_v, cmp_w2_v):
    B, S, _ = p_b.shape
    pos = jnp.arange(S)
    cos2, sin2 = _rope_tables(pos, NSA_GROUPS)
    kc_raw, vc_raw, ks, vst, kw, vwt = _kv_layout(p_b, cos2, sin2)
    n_sub = S // CMP_STRIDE
    sub = lambda t: t.reshape(B, NSA_GROUPS, n_sub, CMP_STRIDE * HEAD_DIM)
    cos_c, sin_c = _rope_tables(jnp.arange(n_sub) * CMP_STRIDE + CMP_BLOCK - 1, 1)
    kc, vc = _compress(sub(kc_raw), sub(vc_raw), cmp_pe_k, cmp_w1_k, cmp_w2_k,
                       cmp_pe_v, cmp_w1_v, cmp_w2_v, cos_c, sin_c)
    cos_q, sin_q = _rope_tables(pos, 1)
    qg_t = jnp.swapaxes(p_b[:, :, 6 * NSA_KV_WIDTH:], 1, 2)
    return _nsa_attention(qg_t, kc, vc, ks, vst, kw, vwt, cos_q.T, sin_q.T)


def _nsa_weight_columns(w_nsa):
    K = w_nsa.shape[0]
    q = w_nsa[:, :NSA_WIDTH]
    kv = w_nsa[:, NSA_WIDTH:NSA_WIDTH + 6 * NSA_KV_WIDTH]
    gates = w_nsa[:, NSA_WIDTH + 6 * NSA_KV_WIDTH:]
    per_group = NSA_HPG * 3
    gate_blocks = [jnp.pad(gates[:, g * per_group:(g + 1) * per_group], ((0, 0), (0, 128 - per_group)))
                   for g in range(NSA_GROUPS)]
    return jnp.concatenate([kv, q] + gate_blocks, axis=1)


def _layer_norm(h, g, b):
    mu = jnp.mean(h, axis=-1, keepdims=True)
    var = jnp.mean(jnp.square(h - mu), axis=-1, keepdims=True)
    return (h - mu) * lax.rsqrt(var + LN_EPS) * g + b


def _pack_bf16_halves(x):
    n = x.shape[-1] // 2
    bits = lax.bitcast_convert_type(x.astype(BF16).astype(F32), jnp.uint32)
    return (bits[:, n:] & jnp.uint32(0xFFFF0000)) | (bits[:, :n] >> 16)


def _unpack_bf16_halves(u):
    left = lax.bitcast_convert_type(u << 16, F32)
    right = lax.bitcast_convert_type(u & jnp.uint32(0xFFFF0000), F32)
    return left, right


def _mixer_out_kernel(x_ref, ya_ref, yb_ref, pg_ref, wa_ref, wb_ref, wo_ref, g_ref, b_ref, o_ref, op_ref,
                      *, alpha):
    d = x_ref.shape[-1]
    gate_a = _sigmoid(pg_ref[:, :d])
    gate_b = _sigmoid(pg_ref[:, d:])
    mixed = gate_a * _bdot(ya_ref[...], wa_ref[...]) + gate_b * _bdot(yb_ref[...], wb_ref[...])
    h = alpha * x_ref[...] + _bdot(mixed, wo_ref[...])
    out = _layer_norm(h, g_ref[...], b_ref[...])
    o_ref[...] = out
    op_ref[...] = _pack_bf16_halves(out)


def _mixer_out(xf, ya, yb, p_g, wa, wb, wo, ln_g, ln_b, alpha, tm=512):
    T, D = xf.shape
    rows = lambda w: pl.BlockSpec((tm, w), lambda i: (i, 0))
    full = lambda a: pl.BlockSpec(a.shape, lambda i: (0,) * a.ndim)
    ln_g, ln_b = ln_g.reshape(1, D), ln_b.reshape(1, D)
    return pl.pallas_call(
        functools.partial(_mixer_out_kernel, alpha=alpha),
        grid=(T // tm,),
        in_specs=[rows(D), rows(ya.shape[1]), rows(yb.shape[1]), rows(2 * D),
                  full(wa), full(wb), full(wo), full(ln_g), full(ln_b)],
        out_specs=[rows(D), rows(D // 2)],
        out_shape=[jax.ShapeDtypeStruct((T, D), F32), jax.ShapeDtypeStruct((T, D // 2), jnp.uint32)],
        compiler_params=pltpu.CompilerParams(
            dimension_semantics=("parallel",), vmem_limit_bytes=VMEM_LIMIT),
        name="mixer_out_ln",
    )(xf, ya, yb, p_g, wa, wb, wo, ln_g, ln_b)


ROUTER_TILE = 256
EXPERT_ROWS = 256
SC_TOKEN_CHUNK = 64
SC_ROW_CHUNK = 128
PICK_LANES = 128
LOWEST = -3.0e38


def _router_kernel(x_ref, rw_ref, bias_ref, eidx_ref, wts_ref, pos_ref, cnt_ref, carry_ref):
    tm, E = x_ref.shape[0], rw_ref.shape[1]
    per_group = E // N_GROUPS

    @pl.when(pl.program_id(0) == 0)
    def _():
        carry_ref[...] = jnp.zeros_like(carry_ref)

    scores = _sigmoid(_hdot(x_ref[...], rw_ref[...]))
    choice = scores + bias_ref[...]
    lane = lax.broadcasted_iota(jnp.int32, (tm, E), 1)
    grp = lane // per_group

    def first_max(vals):
        m = jnp.max(vals, axis=-1, keepdims=True)
        return m, jnp.min(jnp.where(vals == m, lane, E), axis=-1, keepdims=True)

    group_score = []
    for g in range(N_GROUPS):
        cg = jnp.where(grp == g, choice, LOWEST)
        m1, i1 = first_max(cg)
        m2 = jnp.max(jnp.where(lane == i1, LOWEST, cg), axis=-1, keepdims=True)
        group_score.append(m1 + m2)
    allowed = jnp.zeros((tm, E), jnp.bool_)
    for g in range(N_GROUPS):
        rank = jnp.zeros((tm, 1), jnp.int32)
        for o in range(N_GROUPS):
            if o != g:
                ahead = (group_score[o] > group_score[g]) if o > g else (group_score[o] >= group_score[g])
                rank = rank + ahead.astype(jnp.int32)
        allowed = allowed | ((grp == g) & (rank < TOPK_GROUPS))

    cur = jnp.where(allowed, choice, NEG_INF)
    sel = jnp.zeros((tm, E), F32)
    picks = []
    for _ in range(TOP_K):
        _, idx = first_max(cur)
        hit = lane == idx
        picks.append(idx)
        sel = jnp.where(hit, 1.0, sel)
        cur = jnp.where(hit, LOWEST, cur)
    gate = scores * sel
    gate = gate / jnp.sum(gate, axis=-1, keepdims=True) * ROUTED_SCALE

    ri = lax.broadcasted_iota(jnp.int32, (tm, tm), 0)
    ci = lax.broadcasted_iota(jnp.int32, (tm, tm), 1)
    before = jnp.dot((ri > ci).astype(BF16), sel.astype(BF16), preferred_element_type=F32)
    queue_pos = before + carry_ref[...]
    carry_ref[...] = carry_ref[...] + jnp.sum(sel, axis=0, keepdims=True)
    cnt_ref[...] = jnp.broadcast_to(carry_ref[...], cnt_ref.shape)

    out_lane = lax.broadcasted_iota(jnp.int32, (tm, PICK_LANES), 1)
    eidx = jnp.zeros((tm, PICK_LANES), jnp.int32)
    wts = jnp.zeros((tm, PICK_LANES), F32)
    pos = jnp.zeros((tm, PICK_LANES), F32)
    for kk, idx in enumerate(picks):
        hit = lane == idx
        eidx = jnp.where(out_lane == kk, idx, eidx)
        wts = jnp.where(out_lane == kk, jnp.sum(jnp.where(hit, gate, 0.0), axis=-1, keepdims=True), wts)
        pos = jnp.where(out_lane == kk, jnp.sum(jnp.where(hit, queue_pos, 0.0), axis=-1, keepdims=True), pos)
    eidx_ref[...] = eidx
    wts_ref[...] = wts
    pos_ref[...] = pos.astype(jnp.int32)


def _router(xf, router_w, router_bias):
    T, D = xf.shape
    E = router_w.shape[1]
    tm = ROUTER_TILE
    picks = lambda dt: jax.ShapeDtypeStruct((T, PICK_LANES), dt)
    pick_spec = pl.BlockSpec((tm, PICK_LANES), lambda i: (i, 0))
    return pl.pallas_call(
        _router_kernel,
        grid=(T // tm,),
        in_specs=[pl.BlockSpec((tm, D), lambda i: (i, 0)),
                  pl.BlockSpec((D, E), lambda i: (0, 0)),
                  pl.BlockSpec((1, E), lambda i: (0, 0))],
        out_specs=[pick_spec, pick_spec, pick_spec, pl.BlockSpec((8, E), lambda i: (0, 0))],
        out_shape=[picks(jnp.int32), picks(F32), picks(jnp.int32), jax.ShapeDtypeStruct((8, E), F32)],
        scratch_shapes=[pltpu.VMEM((1, E), F32)],
        compiler_params=pltpu.CompilerParams(
            dimension_semantics=("arbitrary",), vmem_limit_bytes=VMEM_LIMIT),
        name="moe_router",
    )(xf, router_w, router_bias.reshape(1, E))


def _dest_kernel(eidx_ref, pos_ref, start_ref, dest_ref):
    tm = eidx_ref.shape[0]
    E = start_ref.shape[1]
    lane = lax.broadcasted_iota(jnp.int32, (tm, E), 1)
    out_lane = lax.broadcasted_iota(jnp.int32, (tm, PICK_LANES), 1)
    eidx = eidx_ref[...]
    base = jnp.zeros((tm, PICK_LANES), jnp.int32)
    for kk in range(TOP_K):
        hit = lane == eidx[:, kk:kk + 1]
        start = jnp.sum(jnp.where(hit, start_ref[...], 0), axis=-1, keepdims=True)
        base = jnp.where(out_lane == kk, start, base)
    dest_ref[...] = base + pos_ref[...]


def _dest_rows(eidx, pos, pad_start):
    T = eidx.shape[0]
    E = pad_start.shape[0]
    tm = ROUTER_TILE
    pick_spec = pl.BlockSpec((tm, PICK_LANES), lambda i: (i, 0))
    return pl.pallas_call(
        _dest_kernel,
        grid=(T // tm,),
        in_specs=[pick_spec, pick_spec, pl.BlockSpec((1, E), lambda i: (0, 0))],
        out_specs=pick_spec,
        out_shape=jax.ShapeDtypeStruct((T, PICK_LANES), jnp.int32),
        compiler_params=pltpu.CompilerParams(
            dimension_semantics=("parallel",), vmem_limit_bytes=VMEM_LIMIT),
        name="moe_dest_rows",
    )(eidx, pos, pad_start.reshape(1, E))


def _sc_mesh():
    return plsc.VectorSubcoreMesh(core_axis_name="c", subcore_axis_name="s")


def _sc_scatter_rows(x, dest_t, n_rows):
    T, D = x.shape
    K = dest_t.shape[0]
    mesh = _sc_mesh()
    nc, nw = mesh.num_cores, mesh.num_cores * mesh.num_subcores
    per_w = T // nw
    chunk = min(SC_TOKEN_CHUNK, per_w)
    n_chunks = per_w // chunk
    idx = dest_t.reshape(K, nw, n_chunks, chunk).transpose(1, 2, 0, 3).reshape(nw, n_chunks * K, chunk)

    @functools.partial(
        pl.kernel, mesh=mesh,
        out_type=jax.ShapeDtypeStruct((n_rows, D), x.dtype),
        scratch_types=[pltpu.VMEM((n_chunks * K, chunk), jnp.int32),
                       pltpu.VMEM((chunk, D), x.dtype),
                       pltpu.SemaphoreType.DMA],
    )
    def scatter(x_hbm, idx_hbm, out_hbm, idx_v, rows_v, sem):
        wid = lax.axis_index("s") * nc + lax.axis_index("c")
        pltpu.sync_copy(idx_hbm.at[wid], idx_v)

        @pl.loop(0, n_chunks)
        def _(j):
            pltpu.sync_copy(x_hbm.at[pl.ds(wid * per_w + j * chunk, chunk)], rows_v)
            copies = [pltpu.async_copy(rows_v, out_hbm.at[idx_v.at[j * K + kk]], sem) for kk in range(K)]
            for c in copies:
                c.wait()

    return scatter(x, idx)


def _sc_gather_rows(src, idx):
    M = idx.shape[0]
    D = src.shape[1]
    mesh = _sc_mesh()
    nc, nw = mesh.num_cores, mesh.num_cores * mesh.num_subcores
    per_w = M // nw
    chunk = min(SC_ROW_CHUNK, per_w)
    n_chunks = per_w // chunk

    @functools.partial(
        pl.kernel, mesh=mesh,
        out_type=jax.ShapeDtypeStruct((M, D), src.dtype),
        scratch_types=[pltpu.VMEM((n_chunks, chunk), jnp.int32),
                       pltpu.VMEM((chunk, D), src.dtype),
                       pltpu.SemaphoreType.DMA],
    )
    def gather(src_hbm, idx_hbm, out_hbm, idx_v, rows_v, sem):
        wid = lax.axis_index("s") * nc + lax.axis_index("c")
        pltpu.sync_copy(idx_hbm.at[wid], idx_v)

        @pl.loop(0, n_chunks)
        def _(j):
            pltpu.async_copy(src_hbm.at[idx_v.at[j]], rows_v, sem).wait()
            pltpu.sync_copy(rows_v, out_hbm.at[pl.ds(wid * per_w + j * chunk, chunk)])

    return gather(src, idx.reshape(nw, n_chunks, chunk))


def _expert_kernel(distinct_e_ref, blk_ord_ref, blk_new_ref, blk_rows_ref, n_used_ref, n_distinct_ref,
                   x_ref, wgu_hbm, wd_hbm, o_ref, wgu_buf, wd_buf, wgu_bf, wd_bf, sem):
    i = pl.program_id(0)
    live = i < n_used_ref[0]
    ordinal = blk_ord_ref[i]
    slot = ordinal % 2

    def weight_copies(k, s):
        e = distinct_e_ref[k]
        return (pltpu.make_async_copy(wgu_hbm.at[e], wgu_buf.at[s], sem.at[0, s]),
                pltpu.make_async_copy(wd_hbm.at[e], wd_buf.at[s], sem.at[1, s]))

    @pl.when(i == 0)
    def _():
        for c in weight_copies(0, 0):
            c.start()

    @pl.when(live & (blk_new_ref[i] == 1))
    def _():
        for c in weight_copies(ordinal, slot):
            c.wait()

        @pl.when(ordinal + 1 < n_distinct_ref[0])
        def _():
            for c in weight_copies(ordinal + 1, 1 - slot):
                c.start()

        wgu_bf[...] = wgu_buf[slot].astype(BF16)
        wd_bf[...] = wd_buf[slot].astype(BF16)

    @pl.when(live)
    def _():
        hidden = wd_bf.shape[0]
        half = x_ref.shape[1]
        row = lax.broadcasted_iota(jnp.int32, x_ref.shape, 0)
        left, right = _unpack_bf16_halves(x_ref[...])
        real = row < blk_rows_ref[i]
        left = jnp.where(real, left, 0.0).astype(BF16)
        right = jnp.where(real, right, 0.0).astype(BF16)
        h = (jnp.dot(left, wgu_bf[:half, :], preferred_element_type=F32)
             + jnp.dot(right, wgu_bf[half:, :], preferred_element_type=F32))
        gate, up = h[:, :hidden], h[:, hidden:]
        act = (gate * _sigmoid(gate) * up).astype(BF16)
        o_ref[...] = _pack_bf16_halves(jnp.dot(act, wd_bf[...], preferred_element_type=F32))

    @pl.when(jnp.logical_not(live))
    def _():
        o_ref[...] = jnp.zeros_like(o_ref)


def _expert_ffn(xs, blk_e, blk_rows, n_used, w_gu, w_down):
    n_rows, half = xs.shape
    E, D, two_h = w_gu.shape
    n_blocks = n_rows // EXPERT_ROWS
    idx = jnp.arange(n_blocks, dtype=jnp.int32)
    is_live = idx < n_used[0]
    blk_new = (is_live & ((idx == 0) | (blk_e != jnp.roll(blk_e, 1)))).astype(jnp.int32)
    blk_ord = (jnp.cumsum(blk_new) - 1).astype(jnp.int32)
    n_distinct = blk_ord[-1:] + 1
    distinct_e = jnp.zeros((n_blocks,), jnp.int32).at[blk_ord].max(blk_e * is_live)

    live = lambda i, nu: jnp.minimum(i, nu[0] - 1)
    grid_spec = pltpu.PrefetchScalarGridSpec(
        num_scalar_prefetch=6,
        grid=(n_blocks,),
        in_specs=[pl.BlockSpec((EXPERT_ROWS, half), lambda i, de, bo, bn, br, nu, nd: (live(i, nu), 0)),
                  pl.BlockSpec(memory_space=pl.ANY), pl.BlockSpec(memory_space=pl.ANY)],
        out_specs=pl.BlockSpec((EXPERT_ROWS, half), lambda i, de, bo, bn, br, nu, nd: (i, 0)),
        scratch_shapes=[pltpu.VMEM((2, D, two_h), F32), pltpu.VMEM((2, two_h // 2, D), F32),
                        pltpu.VMEM((D, two_h), BF16), pltpu.VMEM((two_h // 2, D), BF16),
                        pltpu.SemaphoreType.DMA((2, 2))],
    )
    return pl.pallas_call(
        _expert_kernel,
        grid_spec=grid_spec,
        out_shape=jax.ShapeDtypeStruct((n_rows, half), jnp.uint32),
        compiler_params=pltpu.CompilerParams(
            dimension_semantics=("arbitrary",), vmem_limit_bytes=VMEM_LIMIT),
        name="moe_experts",
    )(distinct_e, blk_ord, blk_new, blk_rows, n_used, n_distinct, xs, w_gu, w_down)


def _moe_out_kernel(x_ref, yk_ref, wts_ref, sgu_ref, sd_ref, g_ref, b_ref, o_ref, *, alpha):
    x = x_ref[...]
    hidden = sd_ref.shape[0]
    h = _bdot(x, sgu_ref[...])
    gate, up = h[:, :hidden], h[:, hidden:]
    ffn = _bdot(gate * _sigmoid(gate) * up, sd_ref[...])
    wts = wts_ref[...]
    routed_left = routed_right = None
    for kk in range(TOP_K):
        left, right = _unpack_bf16_halves(yk_ref[kk])
        w = wts[:, kk:kk + 1]
        routed_left = w * left if kk == 0 else routed_left + w * left
        routed_right = w * right if kk == 0 else routed_right + w * right
    ffn = ffn + jnp.concatenate([routed_left, routed_right], axis=-1)
    o_ref[...] = _layer_norm(alpha * x + ffn, g_ref[...], b_ref[...])


def _moe_out(xf, yk, wts, sw_gu, sw_down, ln_g, ln_b, alpha, tm=128):
    T, D = xf.shape
    rows = lambda w: pl.BlockSpec((tm, w), lambda i: (i, 0))
    full = lambda a: pl.BlockSpec(a.shape, lambda i: (0,) * a.ndim)
    ln_g, ln_b = ln_g.reshape(1, D), ln_b.reshape(1, D)
    return pl.pallas_call(
        functools.partial(_moe_out_kernel, alpha=alpha),
        grid=(T // tm,),
        in_specs=[rows(D), pl.BlockSpec((TOP_K, tm, D // 2), lambda i: (0, i, 0)), rows(PICK_LANES),
                  full(sw_gu), full(sw_down), full(ln_g), full(ln_b)],
        out_specs=rows(D),
        out_shape=jax.ShapeDtypeStruct((T, D), F32),
        compiler_params=pltpu.CompilerParams(
            dimension_semantics=("parallel",), vmem_limit_bytes=VMEM_LIMIT),
        name="moe_combine_ln",
    )(xf, yk, wts, sw_gu, sw_down, ln_g, ln_b)


def _moe_ffn_ln(xf, xp, router_w, router_bias, w_gu, w_down, sw_gu, sw_down, ln_g, ln_b, alpha):
    T, D = xf.shape
    E = router_w.shape[1]
    BM = EXPERT_ROWS
    eidx, wts, pos, cnt = _router(xf, router_w, router_bias)
    counts = cnt[0].astype(jnp.int32)
    padded = (counts + BM - 1) // BM * BM
    pad_end = jnp.cumsum(padded)
    pad_start = pad_end - padded
    n_rows = T * TOP_K + E * BM
    n_blocks = n_rows // BM
    blk_row0 = jnp.arange(n_blocks, dtype=jnp.int32) * BM
    blk_e = jnp.minimum(jnp.sum((pad_end[None, :] <= blk_row0[:, None]).astype(jnp.int32), axis=1), E - 1)
    blk_rows = jnp.clip(pad_start[blk_e] + counts[blk_e] - blk_row0, 0, BM).astype(jnp.int32)
    n_used = (pad_end[-1:] // BM).astype(jnp.int32)
    dest = _dest_rows(eidx, pos, pad_start)[:, :TOP_K]
    dest_t = dest.T
    xs = _sc_scatter_rows(xp, dest_t, n_rows)
    ys = _expert_ffn(xs, blk_e, blk_rows, n_used, w_gu, w_down)
    yk = _sc_gather_rows(ys, dest_t.reshape(-1)).reshape(TOP_K, T, D // 2)
    return _moe_out(xf, yk, wts, sw_gu, sw_down, ln_g, ln_b, alpha)


def kernel(x, w_in, tshift_mu, rwkv_w0, rwkv_w2, rwkv_a0, rwkv_a2, rwkv_g2, rwkv_k_k, rwkv_k_a, rwkv_r_k, rwkv_lnx_w, rwkv_lnx_b, cmp_pe_k, cmp_w1_k, cmp_w2_k, cmp_pe_v, cmp_w1_v, cmp_w2_v, w_branch_a, w_branch_b, w_out, ln1_g, ln1_b, router_w, router_bias, exp_w_gu, exp_w_down, shared_w_gu, shared_w_down, ln2_g, ln2_b):
    B, S, D = x.shape
    depth = w_in.shape[0]
    alpha = (2 * depth) ** 0.25
    nsa_w = w_in.shape[2] - RWKV_IN_W - 2 * D
    for l in range(depth):
        xf = x.reshape(B * S, D)
        w_l = w_in[l]
        w_a = w_l[:, :RWKV_IN_W].astype(BF16)
        w_b = _nsa_weight_columns(w_l[:, RWKV_IN_W:RWKV_IN_W + nsa_w]).astype(BF16)
        w_g = w_l[:, RWKV_IN_W + nsa_w:].astype(BF16)
        p_a = _matmul(xf, w_a, 512, RWKV_IN_W // 2).reshape(B, S, -1)
        p_b = _matmul(xf, w_b, 512, w_b.shape[1] // 2).reshape(B, S, -1)
        p_g = _matmul(xf, w_g, 512, D)
        y_a = _rwkv_time_mix(p_a, tshift_mu[l], rwkv_w0[l], rwkv_w2[l], rwkv_a0[l], rwkv_a2[l], rwkv_g2[l],
                             rwkv_k_k[l], rwkv_k_a[l], rwkv_r_k[l].reshape(-1), rwkv_lnx_w[l], rwkv_lnx_b[l])
        y_b = _nsa_branch(p_b, cmp_pe_k[l], cmp_w1_k[l], cmp_w2_k[l], cmp_pe_v[l], cmp_w1_v[l], cmp_w2_v[l])
        x1, x1p = _mixer_out(xf, y_a.reshape(B * S, -1), y_b.reshape(B * S, -1), p_g,
                             w_branch_a[l].astype(BF16), w_branch_b[l].astype(BF16), w_out[l].astype(BF16),
                             ln1_g[l], ln1_b[l], alpha)
        x2 = _moe_ffn_ln(x1, x1p, router_w[l], router_bias[l], exp_w_gu[l], exp_w_down[l],
                         shared_w_gu[l].astype(BF16), shared_w_down[l].astype(BF16), ln2_g[l], ln2_b[l], alpha)
        x = x2.reshape(B, S, D)
    return x
```

```python
import functools

import numpy as np
import jax
import jax.numpy as jnp
from jax import lax
from jax.experimental import pallas as pl
from jax.experimental.pallas import tpu as pltpu
from jax.experimental.pallas import tpu_sc as plsc

F32 = jnp.float32
BF16 = jnp.bfloat16
HIGHEST = lax.Precision.HIGHEST

RWKV_HEADS = 8
HEAD_DIM = 64
RWKV_WIDTH = RWKV_HEADS * HEAD_DIM
W_LORA = 64
A_LORA = 64
G_LORA = 128
GN_EPS = 64e-5
NSA_HEADS = 8
NSA_GROUPS = 2
NSA_HPG = NSA_HEADS // NSA_GROUPS
NSA_WIDTH = NSA_HEADS * HEAD_DIM
NSA_KV_WIDTH = NSA_GROUPS * HEAD_DIM
CMP_BLOCK = 32
CMP_STRIDE = 16
CMP_HIDDEN = 256
SEL_BLOCK = 64
N_SELECT = 16
WINDOW = 512
Q_BLOCK = 128
ROPE_THETA = 10000.0
RWKV_IN_W = 3 * RWKV_WIDTH + W_LORA + A_LORA + G_LORA
N_EXPERTS = 256
TOP_K = 8
N_GROUPS = 8
TOPK_GROUPS = 4
EXPERT_DIM = 256
ROUTED_SCALE = 2.5
LN_EPS = 1e-5
NEG_INF = -1e30
FORCE_BONUS = 1e4

RWKV_CHUNK = 64
RWKV_HEAD_GROUP = 4
VMEM_LIMIT = 56 * 1024 * 1024
PROJ_ROWS = 512


def _bdot(a, b):
    return jnp.dot(a.astype(BF16), b.astype(BF16), preferred_element_type=F32)


def _bdot_nt(a, b):
    return lax.dot_general(a.astype(BF16), b.astype(BF16), (((1,), (1,)), ((), ())),
                           preferred_element_type=F32)


def _bdot_tn(a, b):
    return lax.dot_general(a.astype(BF16), b.astype(BF16), (((0,), (0,)), ((), ())),
                           preferred_element_type=F32)


def _hdot(a, b):
    return jnp.dot(a, b, precision=HIGHEST, preferred_element_type=F32)


def _sigmoid(x):
    return 1.0 / (1.0 + jnp.exp(-x))


def _matmul_kernel(x_ref, w_ref, o_ref):
    o_ref[...] = jnp.dot(x_ref[...].astype(BF16), w_ref[...], preferred_element_type=F32)


def _matmul(x, w, tm, tn):
    M, K = x.shape
    N = w.shape[1]
    return pl.pallas_call(
        _matmul_kernel,
        grid=(M // tm, N // tn),
        in_specs=[pl.BlockSpec((tm, K), lambda i, j: (i, 0)),
                  pl.BlockSpec((K, tn), lambda i, j: (0, j))],
        out_specs=pl.BlockSpec((tm, tn), lambda i, j: (i, j)),
        out_shape=jax.ShapeDtypeStruct((M, N), F32),
        compiler_params=pltpu.CompilerParams(
            dimension_semantics=("parallel", "parallel"), vmem_limit_bytes=VMEM_LIMIT),
        name="dense_proj",
    )(x, w)


def _rwkv_kernel(p_ref, mu_ref, w0_ref, w2_ref, a0_ref, a2_ref, g2_ref, kk_ref, ka_ref, rk_ref,
                 lnw_ref, lnb_ref, o_ref, carry_ref, state_ref):
    C, H, N = RWKV_CHUNK, RWKV_HEADS, HEAD_DIM
    W = RWKV_WIDTH
    B = p_ref.shape[0]
    R = B * C

    @pl.when(pl.program_id(0) == 0)
    def _():
        carry_ref[...] = jnp.zeros_like(carry_ref)
        state_ref[...] = jnp.zeros_like(state_ref)

    def per_batch(x):
        return jnp.concatenate([jnp.broadcast_to(x[b].reshape(1, -1), (C, x.shape[-1])) for b in range(B)],
                               axis=0)

    p = p_ref[...].reshape(R, p_ref.shape[-1])
    row = lax.broadcasted_iota(jnp.int32, p.shape, 0)
    prev = jnp.where(row % C == 0, per_batch(carry_ref[...]), pltpu.roll(p, 1, axis=0))
    for b in range(B):
        carry_ref[b] = p[b * C + C - 1:b * C + C, :]
    xs = p + (prev - p) * mu_ref[...]
    r = xs[:, 0:W]
    k = xs[:, W:2 * W]
    v = xs[:, 2 * W:3 * W]
    wl = xs[:, 3 * W:3 * W + W_LORA]
    al = xs[:, 3 * W + W_LORA:3 * W + W_LORA + A_LORA]
    gl = xs[:, 3 * W + W_LORA + A_LORA:]

    z = -(w0_ref[...] + _hdot(jnp.tanh(wl), w2_ref[...]))
    softplus = jnp.maximum(z, 0.0) + jnp.log1p(jnp.exp(-jnp.abs(z)))
    logd = -jnp.exp(-softplus - 0.5)
    a = _sigmoid(a0_ref[...] + _hdot(al, a2_ref[...]))
    g = _hdot(_sigmoid(gl), g2_ref[...])

    kk = k * kk_ref[...]
    knew = k * (1.0 + (a - 1.0) * ka_ref[...])

    HG = RWKV_HEAD_GROUP
    GW = HG * N
    same_head_lanes = (lax.broadcasted_iota(jnp.int32, (GW, GW), 0) // N
                       == lax.broadcasted_iota(jnp.int32, (GW, GW), 1) // N)
    head_ones = jnp.where(same_head_lanes, 1.0, 0.0).astype(BF16)

    def head_sum(x):
        hi = x.astype(BF16)
        lo = (x - hi.astype(F32)).astype(BF16)
        return jnp.concatenate(
            [jnp.dot(hi[:, s:s + GW], head_ones, preferred_element_type=F32)
             + jnp.dot(lo[:, s:s + GW], head_ones, preferred_element_type=F32) for s in range(0, W, GW)],
            axis=-1)

    kk = kk / jnp.maximum(jnp.sqrt(head_sum(kk * kk)), 1e-12)
    lr_kk = kk * a

    ti = lax.broadcasted_iota(jnp.int32, (R, R), 0)
    tj = lax.broadcasted_iota(jnp.int32, (R, R), 1)
    same_chunk = (ti >= tj) & (ti // C == tj // C)
    cl = _hdot(same_chunk.astype(F32), logd)
    cl_end = per_batch(jnp.concatenate([cl[b * C + C - 1:b * C + C, :] for b in range(B)], axis=0))
    a_hat = -kk * jnp.exp(cl - logd)
    r_hat = r * jnp.exp(cl)
    inv_gam = jnp.exp(-cl)
    b_til = lr_kk * inv_gam
    k_til = knew * inv_gam
    to_end = jnp.exp(cl_end - cl)
    b_end = lr_kk * to_end
    k_end = knew * to_end
    gam_end = jnp.exp(cl_end)

    gt = lax.broadcasted_iota(jnp.int32, (C, GW), 0)
    gc = lax.broadcasted_iota(jnp.int32, (C, GW), 1) % N
    strict = gt > gc
    incl = gt >= gc
    eye = (gt == gc).astype(F32)
    bi = lax.broadcasted_iota(jnp.int32, (HG * C, GW), 0) // C
    bj = lax.broadcasted_iota(jnp.int32, (HG * C, GW), 1) // N
    same_head = bi == bj

    def block_diag(y):
        yb = y.astype(BF16)
        return jnp.where(same_head, jnp.concatenate([yb] * HG, axis=0), jnp.zeros((), BF16))

    def bd_dot(x, y):
        return jnp.dot(x.astype(BF16), block_diag(y), preferred_element_type=F32)

    def bd_dot_nt(x, y):
        return lax.dot_general(x.astype(BF16), block_diag(y), (((1,), (1,)), ((), ())),
                               preferred_element_type=F32)

    n_groups = H // HG
    units = [(b, gi) for b in range(B) for gi in range(n_groups)]
    n_units = range(len(units))
    cut = lambda x, b, gi: x[b * C:(b + 1) * C, gi * GW:(gi + 1) * GW]
    v_u = [cut(v, b, gi) for b, gi in units]
    ar = [jnp.concatenate([cut(a_hat, b, gi), cut(r_hat, b, gi)], axis=0) for b, gi in units]
    mb = [bd_dot_nt(ar[i], cut(b_til, *units[i])) for i in n_units]
    mk = [bd_dot_nt(ar[i], cut(k_til, *units[i])) for i in n_units]
    n_ab = [jnp.where(strict, mb[i][:C], 0.0) for i in n_units]
    m_rb = [jnp.where(incl, mb[i][C:], 0.0) for i in n_units]
    l_ak = [jnp.where(strict, mk[i][:C], 0.0) for i in n_units]
    m_rk = [jnp.where(incl, mk[i][C:], 0.0) for i in n_units]

    pw = list(n_ab)
    tinv = [eye + n_ab[i] for i in n_units]
    step = 2
    while step < C:
        pw = [bd_dot(pw[i], pw[i]) for i in n_units]
        tinv = [tinv[i] + bd_dot(tinv[i], pw[i]) for i in n_units]
        step *= 2

    s0 = [state_ref[i] for i in n_units]
    ars = [bd_dot_nt(ar[i], s0[i]) for i in n_units]
    lv = [bd_dot(l_ak[i], v_u[i]) for i in n_units]
    u = [bd_dot(tinv[i], ars[i][:C] + lv[i]) for i in n_units]
    outs = [ars[i][C:] + bd_dot(m_rb[i], u[i]) + bd_dot(m_rk[i], v_u[i]) for i in n_units]
    for i, (b, gi) in enumerate(units):
        uv = jnp.concatenate([u[i], v_u[i]], axis=0)
        bk_end = jnp.concatenate([cut(b_end, b, gi), cut(k_end, b, gi)], axis=0)
        cross = jnp.where(same_head, _bdot_tn(uv, bk_end), 0.0)
        upd = cross[0:N]
        for h in range(1, HG):
            upd = upd + cross[h * N:(h + 1) * N]
        state_ref[i] = s0[i] * gam_end[b * C:b * C + 1, gi * GW:(gi + 1) * GW] + upd

    o = jnp.concatenate([jnp.concatenate(outs[b * n_groups:(b + 1) * n_groups], axis=-1) for b in range(B)],
                        axis=0)
    mean = head_sum(o) * (1.0 / N)
    var = head_sum(jnp.square(o - mean)) * (1.0 / N)
    o = (o - mean) * lax.rsqrt(var + GN_EPS) * lnw_ref[...] + lnb_ref[...]
    bonus = head_sum(r * knew * rk_ref[...]) * v
    o_ref[...] = ((o + bonus) * g).reshape(o_ref.shape)


def _hdot_nt(a, b):
    return lax.dot_general(a, b, (((1,), (1,)), ((), ())), precision=HIGHEST,
                           preferred_element_type=F32)


def _hdot_tn(a, b):
    return lax.dot_general(a, b, (((0,), (0,)), ((), ())), precision=HIGHEST,
                           preferred_element_type=F32)


def _rwkv_time_mix(p_a, mu, w0, w2, a0, a2, g2, k_k, k_a, r_k, lnx_w, lnx_b):
    B, S, _ = p_a.shape
    C = RWKV_CHUNK
    row = lambda t: t.reshape(1, -1)
    full = lambda shape: pl.BlockSpec(shape, lambda s: (0,) * len(shape))
    n_units = B * RWKV_HEADS // RWKV_HEAD_GROUP
    return pl.pallas_call(
        _rwkv_kernel,
        grid=(S // C,),
        in_specs=[pl.BlockSpec((B, C, RWKV_IN_W), lambda s: (0, s, 0)),
                  full((1, RWKV_IN_W)), full((1, RWKV_WIDTH)), full((W_LORA, RWKV_WIDTH)),
                  full((1, RWKV_WIDTH)), full((A_LORA, RWKV_WIDTH)), full((G_LORA, RWKV_WIDTH)),
                  full((1, RWKV_WIDTH)), full((1, RWKV_WIDTH)), full((1, RWKV_WIDTH)),
                  full((1, RWKV_WIDTH)), full((1, RWKV_WIDTH))],
        out_specs=pl.BlockSpec((B, C, RWKV_WIDTH), lambda s: (0, s, 0)),
        out_shape=jax.ShapeDtypeStruct((B, S, RWKV_WIDTH), F32),
        scratch_shapes=[pltpu.VMEM((B, 1, RWKV_IN_W), F32),
                        pltpu.VMEM((n_units, HEAD_DIM, RWKV_HEAD_GROUP * HEAD_DIM), F32)],
        compiler_params=pltpu.CompilerParams(
            dimension_semantics=("arbitrary",), vmem_limit_bytes=VMEM_LIMIT),
        name="rwkv7_chunked",
    )(p_a, row(mu), row(w0), w2, row(a0), a2, g2, row(k_k), row(k_a), row(r_k), row(lnx_w), row(lnx_b))


NSA_KV_TILE = 1024
SEL_KEY_TILE = 1024
SEL_LANES = 128


def _rope_tables(pos, reps):
    half = HEAD_DIM // 2
    inv = ROPE_THETA ** (-jnp.arange(half, dtype=F32) / half)
    ang = pos.astype(F32)[:, None] * inv
    cos, sin = jnp.cos(ang), jnp.sin(ang)
    cosf = jnp.concatenate([cos, cos], -1)
    sinf = jnp.concatenate([-sin, sin], -1)
    return jnp.tile(cosf, (1, reps)), jnp.tile(sinf, (1, reps))


def _rope(x, cosf, sinf):
    width = x.shape[-1]
    lane = lax.broadcasted_iota(jnp.int32, x.shape, 1)
    first_half = (lane % HEAD_DIM) < HEAD_DIM // 2
    rot = jnp.where(first_half, pltpu.roll(x, width - HEAD_DIM // 2, axis=1),
                    pltpu.roll(x, HEAD_DIM // 2, axis=1))
    return x * cosf + rot * sinf


def _kv_layout_kernel(p_ref, cos_ref, sin_ref, kc_ref, vc_ref, ks_ref, vs_ref, kw_ref, vw_ref):
    ts = p_ref.shape[0]
    for i, o_ref in ((0, kc_ref), (1, vc_ref), (2, ks_ref), (4, kw_ref)):
        t = p_ref[:, i * NSA_KV_WIDTH:(i + 1) * NSA_KV_WIDTH]
        if i >= 2:
            t = _rope(t, cos_ref[...], sin_ref[...])
        for g in range(NSA_GROUPS):
            o_ref[g] = t[:, g * HEAD_DIM:(g + 1) * HEAD_DIM].astype(o_ref.dtype)
    pad_row = lax.broadcasted_iota(jnp.int32, (VT_ROWS - HEAD_DIM, ts), 0)
    ones_row = jnp.where(pad_row == 0, 1.0, 0.0)
    for i, o_ref in ((3, vs_ref), (5, vw_ref)):
        t_t = p_ref[:, i * NSA_KV_WIDTH:(i + 1) * NSA_KV_WIDTH].T
        for g in range(NSA_GROUPS):
            o_ref[g] = jnp.concatenate([t_t[g * HEAD_DIM:(g + 1) * HEAD_DIM], ones_row],
                                       axis=0).astype(o_ref.dtype)


def _kv_layout(p_b, cos2, sin2):
    B, S, _ = p_b.shape
    ts = min(NSA_KV_TILE, S)
    out_spec = pl.BlockSpec((None, NSA_GROUPS, ts, HEAD_DIM), lambda b, s: (b, 0, s, 0))
    vt_spec = pl.BlockSpec((None, NSA_GROUPS, VT_ROWS, ts), lambda b, s: (b, 0, 0, s))
    shp = lambda dt: jax.ShapeDtypeStruct((B, NSA_GROUPS, S, HEAD_DIM), dt)
    vt_shp = jax.ShapeDtypeStruct((B, NSA_GROUPS, VT_ROWS, S), BF16)
    return pl.pallas_call(
        _kv_layout_kernel,
        grid=(B, S // ts),
        in_specs=[pl.BlockSpec((None, ts, 6 * NSA_KV_WIDTH), lambda b, s: (b, s, 0)),
                  pl.BlockSpec((ts, NSA_KV_WIDTH), lambda b, s: (s, 0)),
                  pl.BlockSpec((ts, NSA_KV_WIDTH), lambda b, s: (s, 0))],
        out_specs=[out_spec, out_spec, out_spec, vt_spec, out_spec, vt_spec],
        out_shape=[shp(F32), shp(F32), shp(BF16), vt_shp, shp(BF16), vt_shp],
        compiler_params=pltpu.CompilerParams(
            dimension_semantics=("parallel", "parallel"), vmem_limit_bytes=VMEM_LIMIT),
        name="nsa_kv_layout",
    )(p_b, cos2, sin2)


def _compress_kernel(subk_ref, subv_ref, pek_ref, w1k_ref, w2k_ref, pev_ref, w1v_ref, w2v_ref,
                     cos_ref, sin_ref, kc_ref, vc_ref):
    n_sub = subk_ref.shape[0]
    half = CMP_STRIDE * HEAD_DIM

    def mlp(sub_ref, pe_ref, w1_ref, w2_ref):
        sub = sub_ref[...]
        top = _bdot(sub, w1_ref[:half, :])
        bot = _bdot(sub, w1_ref[half:, :])
        bias = _bdot(jnp.broadcast_to(pe_ref[...], (8, 2 * half)), w1_ref[...])[0:1, :]
        h = top + pltpu.roll(bot, n_sub - 1, axis=0) + bias
        return _bdot(jax.nn.gelu(h), w2_ref[...])

    kc = mlp(subk_ref, pek_ref, w1k_ref, w2k_ref)
    rot = jnp.concatenate([kc[:, HEAD_DIM // 2:], kc[:, :HEAD_DIM // 2]], axis=-1)
    kc_ref[...] = (kc * cos_ref[...] + rot * sin_ref[...]).astype(kc_ref.dtype)
    vc_ref[...] = mlp(subv_ref, pev_ref, w1v_ref, w2v_ref).astype(vc_ref.dtype)


def _compress(subk, subv, pe_k, w1_k, w2_k, pe_v, w1_v, w2_v, cos_c, sin_c):
    B, G, n_sub, width = subk.shape
    sub_spec = pl.BlockSpec((None, None, n_sub, width), lambda b, g: (b, g, 0, 0))
    full = lambda a: pl.BlockSpec(a.shape, lambda b, g: (0,) * a.ndim)
    out_spec = pl.BlockSpec((None, None, n_sub, HEAD_DIM), lambda b, g: (b, g, 0, 0))
    pe_k, pe_v = pe_k.reshape(1, -1), pe_v.reshape(1, -1)
    args = (pe_k, w1_k, w2_k, pe_v, w1_v, w2_v, cos_c, sin_c)
    return pl.pallas_call(
        _compress_kernel,
        grid=(B, G),
        in_specs=[sub_spec, sub_spec] + [full(a) for a in args],
        out_specs=[out_spec, out_spec],
        out_shape=[jax.ShapeDtypeStruct((B, G, n_sub, HEAD_DIM), BF16)] * 2,
        compiler_params=pltpu.CompilerParams(
            dimension_semantics=("parallel", "parallel"), vmem_limit_bytes=VMEM_LIMIT),
        name="nsa_compress",
    )(subk, subv, *args)


MAX_FLOOR = -1e20
MASK_BIG = 2.0 ** 100
LOG2_E = 1.4426950408889634
VT_ROWS = 80


def _nsa_kernel(q_ref, gate_ref, cos_ref, sin_ref, kc_ref, vc_ref, ks_ref, vst_ref, kw_ref, vwt_ref,
                mselt_ref, o_ref, blockbias_ref, *, n_pick):
    QB, HP, D = Q_BLOCK, NSA_HPG, HEAD_DIM
    qb = pl.program_id(2)
    n_cmp = kc_ref.shape[0]
    lanes4 = lambda x: jnp.concatenate([x] * HP, axis=1)

    heads = []
    for n in range(HP):
        qh = q_ref[n * D:(n + 1) * D, :]
        rot = jnp.concatenate([qh[D // 2:], qh[:D // 2]], axis=0)
        heads.append(qh * cos_ref[...] + rot * sin_ref[...])
    q4 = (jnp.concatenate(heads, axis=1) * (D ** -0.5 * LOG2_E)).astype(BF16)
    t_row = qb * QB + lax.broadcasted_iota(jnp.int32, (1, QB), 1)

    def softmax_cols(s_t, bias_t):
        sm = s_t + lanes4(bias_t)
        m = jnp.maximum(jnp.max(sm, axis=0, keepdims=True), MAX_FLOOR)
        return jnp.exp2(sm - m)

    cmp_end = lax.broadcasted_iota(jnp.int32, (n_cmp, 1), 0) * CMP_STRIDE + (CMP_BLOCK - 1)
    e_c = softmax_cols(jnp.dot(kc_ref[...], q4, preferred_element_type=F32),
                       jnp.where(cmp_end <= t_row, 0.0, -MASK_BIG))
    den_c = jnp.sum(e_c, axis=0, keepdims=True)
    p_c = e_c * (1.0 / jnp.where(den_c > 0.0, den_c, 1.0))
    o_c = _bdot_tn(vc_ref[...], p_c)
    p_sum = p_c[:, 0:QB]
    for n in range(1, HP):
        p_sum = p_sum + p_c[:, n * QB:(n + 1) * QB]
    p_hi = p_sum.astype(BF16)
    p_lo = (p_sum - p_hi.astype(F32)).astype(BF16)
    imp_t = (jnp.dot(mselt_ref[...], p_hi, preferred_element_type=F32)
             + jnp.dot(mselt_ref[...], p_lo, preferred_element_type=F32))

    j = lax.broadcasted_iota(jnp.int32, (SEL_LANES, QB), 0)
    cur = t_row // SEL_BLOCK
    valid = j * SEL_BLOCK <= t_row
    forced = (j == 0) | (j == cur) | (j == cur - 1)
    score = jnp.where(valid, imp_t + jnp.where(forced, FORCE_BONUS, 0.0), -1.0)
    sel_t = jnp.full((SEL_LANES, QB), -MASK_BIG, F32)
    for _ in range(n_pick):
        m = jnp.max(score, axis=0, keepdims=True)
        idx = jnp.min(jnp.where(score == m, j, SEL_LANES), axis=0, keepdims=True)
        hit = j == idx
        sel_t = jnp.where(hit & (m >= 0.0), 0.0, sel_t)
        score = jnp.where(hit, -2.0, score)
    blockbias_ref[...] = sel_t

    KT = SEL_KEY_TILE
    blocks_per_tile = KT // SEL_BLOCK
    n_tiles = (qb * QB + QB + KT - 1) // KT

    def sel_step(kt, carry, causal):
        m_i, acc = carry
        start = pl.multiple_of(kt * KT, KT)
        s_t = jnp.dot(ks_ref[pl.ds(start, KT), :], q4, preferred_element_type=F32)
        bias = jnp.concatenate(
            [jnp.broadcast_to(blockbias_ref[pl.ds(kt * blocks_per_tile + jb, 1), :], (SEL_BLOCK, QB))
             for jb in range(blocks_per_tile)], axis=0)
        if causal:
            kpos = start + lax.broadcasted_iota(jnp.int32, (KT, 1), 0)
            bias = jnp.where(kpos <= t_row, bias, -MASK_BIG)
        sm = s_t + lanes4(bias)
        m_new = jnp.maximum(m_i, jnp.max(sm, axis=0, keepdims=True))
        e = jnp.exp2(sm - m_new).astype(BF16)
        acc_new = jnp.exp2(m_i - m_new) * acc + jnp.dot(vst_ref[:, pl.ds(start, KT)], e,
                                                        preferred_element_type=F32)
        return m_new, acc_new

    init = (jnp.full((1, HP * QB), MAX_FLOOR, F32), jnp.zeros((VT_ROWS, HP * QB), F32))
    carry = lax.fori_loop(0, n_tiles - 1, lambda kt, c: sel_step(kt, c, False), init)
    _, acc_s = sel_step(n_tiles - 1, carry, True)
    den_s = acc_s[D:D + 1]
    o_s = acc_s[:D] * (1.0 / jnp.where(den_s > 0.0, den_s, 1.0))

    span = WINDOW + QB
    w_start = pl.multiple_of(jnp.maximum(qb * QB - WINDOW, 0), QB)
    dist = t_row - (w_start + lax.broadcasted_iota(jnp.int32, (span, 1), 0))
    e_w = softmax_cols(jnp.dot(kw_ref[pl.ds(w_start, span), :], q4, preferred_element_type=F32),
                       jnp.where((dist >= 0) & (dist < WINDOW), 0.0, -MASK_BIG))
    acc_w = jnp.dot(vwt_ref[:, pl.ds(w_start, span)], e_w.astype(BF16), preferred_element_type=F32)
    den_w = acc_w[D:D + 1]
    o_w = acc_w[:D] * (1.0 / jnp.where(den_w > 0.0, den_w, 1.0))

    gates = _sigmoid(gate_ref[...])
    gate_row = lambda br: jnp.concatenate([gates[3 * n + br:3 * n + br + 1, :] for n in range(HP)], axis=1)
    o_t = gate_row(0) * o_c + gate_row(1) * o_s + gate_row(2) * o_w
    for n in range(HP):
        o_ref[:, n * D:(n + 1) * D] = o_t[:, n * QB:(n + 1) * QB].T


def _cmp_to_sel_matrix(n_cmp_rows, n_sel):
    ratio = SEL_BLOCK // CMP_STRIDE
    ci = np.arange(n_cmp_rows)[:, None]
    sj = np.arange(SEL_LANES)[None, :]
    m = sum(((ci + n) // ratio == sj).astype(np.float32) for n in range(CMP_BLOCK // CMP_STRIDE))
    m = m * (sj < n_sel) * (ci < n_cmp_rows - 1)
    return jnp.asarray(m.T, BF16)


def _nsa_attention(qg_t, kc, vc, ks, vst, kw, vwt, cos_t, sin_t):
    B, _, S = qg_t.shape
    n_sub = kc.shape[2]
    n_sel = S // SEL_BLOCK
    gw = NSA_HPG * HEAD_DIM
    gate_row0 = NSA_WIDTH // 128
    msel_t = _cmp_to_sel_matrix(n_sub, n_sel)
    at_bg = lambda shape: pl.BlockSpec((None, None) + shape, lambda b, g, i: (b, g, 0, 0))
    const = lambda a: pl.BlockSpec(a.shape, lambda b, g, i: (0, 0))
    return pl.pallas_call(
        functools.partial(_nsa_kernel, n_pick=min(N_SELECT, n_sel)),
        grid=(B, NSA_GROUPS, S // Q_BLOCK),
        in_specs=[pl.BlockSpec((None, gw, Q_BLOCK), lambda b, g, i: (b, g, i)),
                  pl.BlockSpec((None, 128, Q_BLOCK), lambda b, g, i: (b, gate_row0 + g, i)),
                  pl.BlockSpec((HEAD_DIM, Q_BLOCK), lambda b, g, i: (0, i)),
                  pl.BlockSpec((HEAD_DIM, Q_BLOCK), lambda b, g, i: (0, i)),
                  at_bg((n_sub, HEAD_DIM)), at_bg((n_sub, HEAD_DIM)),
                  at_bg((S, HEAD_DIM)), at_bg((VT_ROWS, S)), at_bg((S, HEAD_DIM)), at_bg((VT_ROWS, S)),
                  const(msel_t)],
        out_specs=pl.BlockSpec((None, Q_BLOCK, gw), lambda b, g, i: (b, i, g)),
        out_shape=jax.ShapeDtypeStruct((B, S, NSA_WIDTH), F32),
        scratch_shapes=[pltpu.VMEM((SEL_LANES, Q_BLOCK), F32)],
        compiler_params=pltpu.CompilerParams(
            dimension_semantics=("parallel", "parallel", "arbitrary"), vmem_limit_bytes=VMEM_LIMIT),
        name="nsa_attention",
    )(qg_t, qg_t, cos_t, sin_t, kc, vc, ks, vst, kw, vwt, msel_t)


def _nsa_branch(p_b, cmp_pe_k, cmp_w1_k, cmp_w2_k, cmp_pe_v, cmp_w1_v, cmp_w2_v):
    B, S, _ = p_b.shape
    pos = jnp.arange(S)
    cos2, sin2 = _rope_tables(pos, NSA_GROUPS)
    kc_raw, vc_raw, ks, vst, kw, vwt = _kv_layout(p_b, cos2, sin2)
    n_sub = S // CMP_STRIDE
    sub = lambda t: t.reshape(B, NSA_GROUPS, n_sub, CMP_STRIDE * HEAD_DIM)
    cos_c, sin_c = _rope_tables(jnp.arange(n_sub) * CMP_STRIDE + CMP_BLOCK - 1, 1)
    kc, vc = _compress(sub(kc_raw), sub(vc_raw), cmp_pe_k, cmp_w1_k, cmp_w2_k,
                       cmp_pe_v, cmp_w1_v, cmp_w2_v, cos_c, sin_c)
    cos_q, sin_q = _rope_tables(pos, 1)
    qg_t = jnp.swapaxes(p_b[:, :, 6 * NSA_KV_WIDTH:], 1, 2)
    return _nsa_attention(qg_t, kc, vc, ks, vst, kw, vwt, cos_q.T, sin_q.T)


def _nsa_weight_columns(w_nsa):
    K = w_nsa.shape[0]
    q = w_nsa[:, :NSA_WIDTH]
    kv = w_nsa[:, NSA_WIDTH:NSA_WIDTH + 6 * NSA_KV_WIDTH]
    gates = w_nsa[:, NSA_WIDTH + 6 * NSA_KV_WIDTH:]
    per_group = NSA_HPG * 3
    gate_blocks = [jnp.pad(gates[:, g * per_group:(g + 1) * per_group], ((0, 0), (0, 128 - per_group)))
                   for g in range(NSA_GROUPS)]
    return jnp.concatenate([kv, q] + gate_blocks, axis=1)


def _layer_norm(h, g, b):
    mu = jnp.mean(h, axis=-1, keepdims=True)
    var = jnp.mean(jnp.square(h - mu), axis=-1, keepdims=True)
    return (h - mu) * lax.rsqrt(var + LN_EPS) * g + b


def _pack_bf16_halves(x):
    n = x.shape[-1] // 2
    bits = lax.bitcast_convert_type(x.astype(BF16).astype(F32), jnp.uint32)
    return (bits[:, n:] & jnp.uint32(0xFFFF0000)) | (bits[:, :n] >> 16)


def _unpack_bf16_halves(u):
    left = lax.bitcast_convert_type(u << 16, F32)
    right = lax.bitcast_convert_type(u & jnp.uint32(0xFFFF0000), F32)
    return left, right


def _mixer_out_kernel(x_ref, ya_ref, yb_ref, pg_ref, wa_ref, wb_ref, wo_ref, g_ref, b_ref, o_ref, op_ref,
                      *, alpha):
    d = x_ref.shape[-1]
    gate_a = _sigmoid(pg_ref[:, :d])
    gate_b = _sigmoid(pg_ref[:, d:])
    mixed = gate_a * _bdot(ya_ref[...], wa_ref[...]) + gate_b * _bdot(yb_ref[...], wb_ref[...])
    h = alpha * x_ref[...] + _bdot(mixed, wo_ref[...])
    out = _layer_norm(h, g_ref[...], b_ref[...])
    o_ref[...] = out
    op_ref[...] = _pack_bf16_halves(out)


def _mixer_out(xf, ya, yb, p_g, wa, wb, wo, ln_g, ln_b, alpha, tm=512):
    T, D = xf.shape
    rows = lambda w: pl.BlockSpec((tm, w), lambda i: (i, 0))
    full = lambda a: pl.BlockSpec(a.shape, lambda i: (0,) * a.ndim)
    ln_g, ln_b = ln_g.reshape(1, D), ln_b.reshape(1, D)
    return pl.pallas_call(
        functools.partial(_mixer_out_kernel, alpha=alpha),
        grid=(T // tm,),
        in_specs=[rows(D), rows(ya.shape[1]), rows(yb.shape[1]), rows(2 * D),
                  full(wa), full(wb), full(wo), full(ln_g), full(ln_b)],
        out_specs=[rows(D), rows(D // 2)],
        out_shape=[jax.ShapeDtypeStruct((T, D), F32), jax.ShapeDtypeStruct((T, D // 2), jnp.uint32)],
        compiler_params=pltpu.CompilerParams(
            dimension_semantics=("parallel",), vmem_limit_bytes=VMEM_LIMIT),
        name="mixer_out_ln",
    )(xf, ya, yb, p_g, wa, wb, wo, ln_g, ln_b)


ROUTER_TILE = 256
EXPERT_ROWS = 256
SC_TOKEN_CHUNK = 64
SC_ROW_CHUNK = 128
PICK_LANES = 128
LOWEST = -3.0e38


def _router_kernel(x_ref, rw_ref, bias_ref, eidx_ref, wts_ref, pos_ref, cnt_ref, carry_ref):
    tm, E = x_ref.shape[0], rw_ref.shape[1]
    per_group = E // N_GROUPS

    @pl.when(pl.program_id(0) == 0)
    def _():
        carry_ref[...] = jnp.zeros_like(carry_ref)

    scores = _sigmoid(_hdot(x_ref[...], rw_ref[...]))
    choice = scores + bias_ref[...]
    lane = lax.broadcasted_iota(jnp.int32, (tm, E), 1)
    grp = lane // per_group

    def first_max(vals):
        m = jnp.max(vals, axis=-1, keepdims=True)
        return m, jnp.min(jnp.where(vals == m, lane, E), axis=-1, keepdims=True)

    group_score = []
    for g in range(N_GROUPS):
        cg = jnp.where(grp == g, choice, LOWEST)
        m1, i1 = first_max(cg)
        m2 = jnp.max(jnp.where(lane == i1, LOWEST, cg), axis=-1, keepdims=True)
        group_score.append(m1 + m2)
    allowed = jnp.zeros((tm, E), jnp.bool_)
    for g in range(N_GROUPS):
        rank = jnp.zeros((tm, 1), jnp.int32)
        for o in range(N_GROUPS):
            if o != g:
                ahead = (group_score[o] > group_score[g]) if o > g else (group_score[o] >= group_score[g])
                rank = rank + ahead.astype(jnp.int32)
        allowed = allowed | ((grp == g) & (rank < TOPK_GROUPS))

    cur = jnp.where(allowed, choice, NEG_INF)
    sel = jnp.zeros((tm, E), F32)
    picks = []
    for _ in range(TOP_K):
        _, idx = first_max(cur)
        hit = lane == idx
        picks.append(idx)
        sel = jnp.where(hit, 1.0, sel)
        cur = jnp.where(hit, LOWEST, cur)
    gate = scores * sel
    gate = gate / jnp.sum(gate, axis=-1, keepdims=True) * ROUTED_SCALE

    ri = lax.broadcasted_iota(jnp.int32, (tm, tm), 0)
    ci = lax.broadcasted_iota(jnp.int32, (tm, tm), 1)
    before = jnp.dot((ri > ci).astype(BF16), sel.astype(BF16), preferred_element_type=F32)
    queue_pos = before + carry_ref[...]
    carry_ref[...] = carry_ref[...] + jnp.sum(sel, axis=0, keepdims=True)
    cnt_ref[...] = jnp.broadcast_to(carry_ref[...], cnt_ref.shape)

    out_lane = lax.broadcasted_iota(jnp.int32, (tm, PICK_LANES), 1)
    eidx = jnp.zeros((tm, PICK_LANES), jnp.int32)
    wts = jnp.zeros((tm, PICK_LANES), F32)
    pos = jnp.zeros((tm, PICK_LANES), F32)
    for kk, idx in enumerate(picks):
        hit = lane == idx
        eidx = jnp.where(out_lane == kk, idx, eidx)
        wts = jnp.where(out_lane == kk, jnp.sum(jnp.where(hit, gate, 0.0), axis=-1, keepdims=True), wts)
        pos = jnp.where(out_lane == kk, jnp.sum(jnp.where(hit, queue_pos, 0.0), axis=-1, keepdims=True), pos)
    eidx_ref[...] = eidx
    wts_ref[...] = wts
    pos_ref[...] = pos.astype(jnp.int32)


def _router(xf, router_w, router_bias):
    T, D = xf.shape
    E = router_w.shape[1]
    tm = ROUTER_TILE
    picks = lambda dt: jax.ShapeDtypeStruct((T, PICK_LANES), dt)
    pick_spec = pl.BlockSpec((tm, PICK_LANES), lambda i: (i, 0))
    return pl.pallas_call(
        _router_kernel,
        grid=(T // tm,),
        in_specs=[pl.BlockSpec((tm, D), lambda i: (i, 0)),
                  pl.BlockSpec((D, E), lambda i: (0, 0)),
                  pl.BlockSpec((1, E), lambda i: (0, 0))],
        out_specs=[pick_spec, pick_spec, pick_spec, pl.BlockSpec((8, E), lambda i: (0, 0))],
        out_shape=[picks(jnp.int32), picks(F32), picks(jnp.int32), jax.ShapeDtypeStruct((8, E), F32)],
        scratch_shapes=[pltpu.VMEM((1, E), F32)],
        compiler_params=pltpu.CompilerParams(
            dimension_semantics=("arbitrary",), vmem_limit_bytes=VMEM_LIMIT),
        name="moe_router",
    )(xf, router_w, router_bias.reshape(1, E))


def _dest_kernel(eidx_ref, pos_ref, start_ref, dest_ref):
    tm = eidx_ref.shape[0]
    E = start_ref.shape[1]
    lane = lax.broadcasted_iota(jnp.int32, (tm, E), 1)
    out_lane = lax.broadcasted_iota(jnp.int32, (tm, PICK_LANES), 1)
    eidx = eidx_ref[...]
    base = jnp.zeros((tm, PICK_LANES), jnp.int32)
    for kk in range(TOP_K):
        hit = lane == eidx[:, kk:kk + 1]
        start = jnp.sum(jnp.where(hit, start_ref[...], 0), axis=-1, keepdims=True)
        base = jnp.where(out_lane == kk, start, base)
    dest_ref[...] = base + pos_ref[...]


def _dest_rows(eidx, pos, pad_start):
    T = eidx.shape[0]
    E = pad_start.shape[0]
    tm = ROUTER_TILE
    pick_spec = pl.BlockSpec((tm, PICK_LANES), lambda i: (i, 0))
    return pl.pallas_call(
        _dest_kernel,
        grid=(T // tm,),
        in_specs=[pick_spec, pick_spec, pl.BlockSpec((1, E), lambda i: (0, 0))],
        out_specs=pick_spec,
        out_shape=jax.ShapeDtypeStruct((T, PICK_LANES), jnp.int32),
        compiler_params=pltpu.CompilerParams(
            dimension_semantics=("parallel",), vmem_limit_bytes=VMEM_LIMIT),
        name="moe_dest_rows",
    )(eidx, pos, pad_start.reshape(1, E))


def _sc_mesh():
    return plsc.VectorSubcoreMesh(core_axis_name="c", subcore_axis_name="s")


def _sc_scatter_rows(x, dest_t, n_rows):
    T, D = x.shape
    K = dest_t.shape[0]
    mesh = _sc_mesh()
    nc, nw = mesh.num_cores, mesh.num_cores * mesh.num_subcores
    per_w = T // nw
    chunk = min(SC_TOKEN_CHUNK, per_w)
    n_chunks = per_w // chunk
    idx = dest_t.reshape(K, nw, n_chunks, chunk).transpose(1, 2, 0, 3).reshape(nw, n_chunks * K, chunk)

    @functools.partial(
        pl.kernel, mesh=mesh,
        out_type=jax.ShapeDtypeStruct((n_rows, D), x.dtype),
        scratch_types=[pltpu.VMEM((n_chunks * K, chunk), jnp.int32),
                       pltpu.VMEM((chunk, D), x.dtype),
                       pltpu.SemaphoreType.DMA],
    )
    def scatter(x_hbm, idx_hbm, out_hbm, idx_v, rows_v, sem):
        wid = lax.axis_index("s") * nc + lax.axis_index("c")
        pltpu.sync_copy(idx_hbm.at[wid], idx_v)

        @pl.loop(0, n_chunks)
        def _(j):
            pltpu.sync_copy(x_hbm.at[pl.ds(wid * per_w + j * chunk, chunk)], rows_v)
            copies = [pltpu.async_copy(rows_v, out_hbm.at[idx_v.at[j * K + kk]], sem) for kk in range(K)]
            for c in copies:
                c.wait()

    return scatter(x, idx)


def _sc_gather_rows(src, idx):
    M = idx.shape[0]
    D = src.shape[1]
    mesh = _sc_mesh()
    nc, nw = mesh.num_cores, mesh.num_cores * mesh.num_subcores
    per_w = M // nw
    chunk = min(SC_ROW_CHUNK, per_w)
    n_chunks = per_w // chunk

    @functools.partial(
        pl.kernel, mesh=mesh,
        out_type=jax.ShapeDtypeStruct((M, D), src.dtype),
        scratch_types=[pltpu.VMEM((n_chunks, chunk), jnp.int32),
                       pltpu.VMEM((chunk, D), src.dtype),
                       pltpu.SemaphoreType.DMA],
    )
    def gather(src_hbm, idx_hbm, out_hbm, idx_v, rows_v, sem):
        wid = lax.axis_index("s") * nc + lax.axis_index("c")
        pltpu.sync_copy(idx_hbm.at[wid], idx_v)

        @pl.loop(0, n_chunks)
        def _(j):
            pltpu.async_copy(src_hbm.at[idx_v.at[j]], rows_v, sem).wait()
            pltpu.sync_copy(rows_v, out_hbm.at[pl.ds(wid * per_w + j * chunk, chunk)])

    return gather(src, idx.reshape(nw, n_chunks, chunk))


def _expert_kernel(distinct_e_ref, blk_ord_ref, blk_new_ref, blk_rows_ref, n_used_ref, n_distinct_ref,
                   x_ref, wgu_hbm, wd_hbm, o_ref, wgu_buf, wd_buf, wgu_bf, wd_bf, sem):
    i = pl.program_id(0)
    live = i < n_used_ref[0]
    ordinal = blk_ord_ref[i]
    slot = ordinal % 2

    def weight_copies(k, s):
        e = distinct_e_ref[k]
        return (pltpu.make_async_copy(wgu_hbm.at[e], wgu_buf.at[s], sem.at[0, s]),
                pltpu.make_async_copy(wd_hbm.at[e], wd_buf.at[s], sem.at[1, s]))

    @pl.when(i == 0)
    def _():
        for c in weight_copies(0, 0):
            c.start()

    @pl.when(live & (blk_new_ref[i] == 1))
    def _():
        for c in weight_copies(ordinal, slot):
            c.wait()

        @pl.when(ordinal + 1 < n_distinct_ref[0])
        def _():
            for c in weight_copies(ordinal + 1, 1 - slot):
                c.start()

        wgu_bf[...] = wgu_buf[slot].astype(BF16)
        wd_bf[...] = wd_buf[slot].astype(BF16)

    @pl.when(live)
    def _():
        hidden = wd_bf.shape[0]
        half = x_ref.shape[1]
        row = lax.broadcasted_iota(jnp.int32, x_ref.shape, 0)
        left, right = _unpack_bf16_halves(x_ref[...])
        real = row < blk_rows_ref[i]
        left = jnp.where(real, left, 0.0).astype(BF16)
        right = jnp.where(real, right, 0.0).astype(BF16)
        h = (jnp.dot(left, wgu_bf[:half, :], preferred_element_type=F32)
             + jnp.dot(right, wgu_bf[half:, :], preferred_element_type=F32))
        gate, up = h[:, :hidden], h[:, hidden:]
        act = (gate * _sigmoid(gate) * up).astype(BF16)
        o_ref[...] = _pack_bf16_halves(jnp.dot(act, wd_bf[...], preferred_element_type=F32))

    @pl.when(jnp.logical_not(live))
    def _():
        o_ref[...] = jnp.zeros_like(o_ref)


def _expert_ffn(xs, blk_e, blk_rows, n_used, w_gu, w_down):
    n_rows, half = xs.shape
    E, D, two_h = w_gu.shape
    n_blocks = n_rows // EXPERT_ROWS
    idx = jnp.arange(n_blocks, dtype=jnp.int32)
    is_live = idx < n_used[0]
    blk_new = (is_live & ((idx == 0) | (blk_e != jnp.roll(blk_e, 1)))).astype(jnp.int32)
    blk_ord = (jnp.cumsum(blk_new) - 1).astype(jnp.int32)
    n_distinct = blk_ord[-1:] + 1
    distinct_e = jnp.zeros((n_blocks,), jnp.int32).at[blk_ord].max(blk_e * is_live)

    live = lambda i, nu: jnp.minimum(i, nu[0] - 1)
    grid_spec = pltpu.PrefetchScalarGridSpec(
        num_scalar_prefetch=6,
        grid=(n_blocks,),
        in_specs=[pl.BlockSpec((EXPERT_ROWS, half), lambda i, de, bo, bn, br, nu, nd: (live(i, nu), 0)),
                  pl.BlockSpec(memory_space=pl.ANY), pl.BlockSpec(memory_space=pl.ANY)],
        out_specs=pl.BlockSpec((EXPERT_ROWS, half), lambda i, de, bo, bn, br, nu, nd: (i, 0)),
        scratch_shapes=[pltpu.VMEM((2, D, two_h), F32), pltpu.VMEM((2, two_h // 2, D), F32),
                        pltpu.VMEM((D, two_h), BF16), pltpu.VMEM((two_h // 2, D), BF16),
                        pltpu.SemaphoreType.DMA((2, 2))],
    )
    return pl.pallas_call(
        _expert_kernel,
        grid_spec=grid_spec,
        out_shape=jax.ShapeDtypeStruct((n_rows, half), jnp.uint32),
        compiler_params=pltpu.CompilerParams(
            dimension_semantics=("arbitrary",), vmem_limit_bytes=VMEM_LIMIT),
        name="moe_experts",
    )(distinct_e, blk_ord, blk_new, blk_rows, n_used, n_distinct, xs, w_gu, w_down)


def _moe_out_kernel(x_ref, yk_ref, wts_ref, sgu_ref, sd_ref, g_ref, b_ref, o_ref, *, alpha):
    x = x_ref[...]
    hidden = sd_ref.shape[0]
    h = _bdot(x, sgu_ref[...])
    gate, up = h[:, :hidden], h[:, hidden:]
    ffn = _bdot(gate * _sigmoid(gate) * up, sd_ref[...])
    wts = wts_ref[...]
    routed_left = routed_right = None
    for kk in range(TOP_K):
        left, right = _unpack_bf16_halves(yk_ref[kk])
        w = wts[:, kk:kk + 1]
        routed_left = w * left if kk == 0 else routed_left + w * left
        routed_right = w * right if kk == 0 else routed_right + w * right
    ffn = ffn + jnp.concatenate([routed_left, routed_right], axis=-1)
    o_ref[...] = _layer_norm(alpha * x + ffn, g_ref[...], b_ref[...])


def _moe_out(xf, yk, wts, sw_gu, sw_down, ln_g, ln_b, alpha, tm=128):
    T, D = xf.shape
    rows = lambda w: pl.BlockSpec((tm, w), lambda i: (i, 0))
    full = lambda a: pl.BlockSpec(a.shape, lambda i: (0,) * a.ndim)
    ln_g, ln_b = ln_g.reshape(1, D), ln_b.reshape(1, D)
    return pl.pallas_call(
        functools.partial(_moe_out_kernel, alpha=alpha),
        grid=(T // tm,),
        in_specs=[rows(D), pl.BlockSpec((TOP_K, tm, D // 2), lambda i: (0, i, 0)), rows(PICK_LANES),
                  full(sw_gu), full(sw_down), full(ln_g), full(ln_b)],
        out_specs=rows(D),
        out_shape=jax.ShapeDtypeStruct((T, D), F32),
        compiler_params=pltpu.CompilerParams(
            dimension_semantics=("parallel",), vmem_limit_bytes=VMEM_LIMIT),
        name="moe_combine_ln",
    )(xf, yk, wts, sw_gu, sw_down, ln_g, ln_b)


def _moe_ffn_ln(xf, xp, router_w, router_bias, w_gu, w_down, sw_gu, sw_down, ln_g, ln_b, alpha):
    T, D = xf.shape
    E = router_w.shape[1]
    BM = EXPERT_ROWS
    eidx, wts, pos, cnt = _router(xf, router_w, router_bias)
    counts = cnt[0].astype(jnp.int32)
    padded = (counts + BM - 1) // BM * BM
    pad_end = jnp.cumsum(padded)
    pad_start = pad_end - padded
    n_rows = T * TOP_K + E * BM
    n_blocks = n_rows // BM
    blk_row0 = jnp.arange(n_blocks, dtype=jnp.int32) * BM
    blk_e = jnp.minimum(jnp.sum((pad_end[None, :] <= blk_row0[:, None]).astype(jnp.int32), axis=1), E - 1)
    blk_rows = jnp.clip(pad_start[blk_e] + counts[blk_e] - blk_row0, 0, BM).astype(jnp.int32)
    n_used = (pad_end[-1:] // BM).astype(jnp.int32)
    dest = _dest_rows(eidx, pos, pad_start)[:, :TOP_K]
    dest_t = dest.T
    xs = _sc_scatter_rows(xp, dest_t, n_rows)
    ys = _expert_ffn(xs, blk_e, blk_rows, n_used, w_gu, w_down)
    yk = _sc_gather_rows(ys, dest_t.reshape(-1)).reshape(TOP_K, T, D // 2)
    return _moe_out(xf, yk, wts, sw_gu, sw_down, ln_g, ln_b, alpha)


def kernel(x, w_in, tshift_mu, rwkv_w0, rwkv_w2, rwkv_a0, rwkv_a2, rwkv_g2, rwkv_k_k, rwkv_k_a, rwkv_r_k, rwkv_lnx_w, rwkv_lnx_b, cmp_pe_k, cmp_w1_k, cmp_w2_k, cmp_pe_v, cmp_w1_v, cmp_w2_v, w_branch_a, w_branch_b, w_out, ln1_g, ln1_b, router_w, router_bias, exp_w_gu, exp_w_down, shared_w_gu, shared_w_down, ln2_g, ln2_b):
    B, S, D = x.shape
    depth = w_in.shape[0]
    alpha = (2 * depth) ** 0.25
    nsa_w = w_in.shape[2] - RWKV_IN_W - 2 * D
    for l in range(depth):
        xf = x.reshape(B * S, D)
        w_l = w_in[l]
        w_a = w_l[:, :RWKV_IN_W].astype(BF16)
        w_b = _nsa_weight_columns(w_l[:, RWKV_IN_W:RWKV_IN_W + nsa_w]).astype(BF16)
        w_g = w_l[:, RWKV_IN_W + nsa_w:].astype(BF16)
        p_a = _matmul(xf, w_a, PROJ_ROWS, w_a.shape[1]).reshape(B, S, -1)
        p_b = _matmul(xf, w_b, PROJ_ROWS, w_b.shape[1]).reshape(B, S, -1)
        p_g = _matmul(xf, w_g, PROJ_ROWS, w_g.shape[1])
        y_a = _rwkv_time_mix(p_a, tshift_mu[l], rwkv_w0[l], rwkv_w2[l], rwkv_a0[l], rwkv_a2[l], rwkv_g2[l],
                             rwkv_k_k[l], rwkv_k_a[l], rwkv_r_k[l].reshape(-1), rwkv_lnx_w[l], rwkv_lnx_b[l])
        y_b = _nsa_branch(p_b, cmp_pe_k[l], cmp_w1_k[l], cmp_w2_k[l], cmp_pe_v[l], cmp_w1_v[l], cmp_w2_v[l])
        x1, x1p = _mixer_out(xf, y_a.reshape(B * S, -1), y_b.reshape(B * S, -1), p_g,
                             w_branch_a[l].astype(BF16), w_branch_b[l].astype(BF16), w_out[l].astype(BF16),
                             ln1_g[l], ln1_b[l], alpha)
        x2 = _moe_ffn_ln(x1, x1p, router_w[l], router_bias[l], exp_w_gu[l], exp_w_down[l],
                         shared_w_gu[l].astype(BF16), shared_w_down[l].astype(BF16), ln2_g[l], ln2_b[l], alpha)
        x = x2.reshape(B, S, D)
    return x
```

```python
import functools

import numpy as np
import jax
import jax.numpy as jnp
from jax import lax
from jax.experimental import pallas as pl
from jax.experimental.pallas import tpu as pltpu
from jax.experimental.pallas import tpu_sc as plsc

F32 = jnp.float32
BF16 = jnp.bfloat16
HIGHEST = lax.Precision.HIGHEST

RWKV_HEADS = 8
HEAD_DIM = 64
RWKV_WIDTH = RWKV_HEADS * HEAD_DIM
W_LORA = 64
A_LORA = 64
G_LORA = 128
GN_EPS = 64e-5
NSA_HEADS = 8
NSA_GROUPS = 2
NSA_HPG = NSA_HEADS // NSA_GROUPS
NSA_WIDTH = NSA_HEADS * HEAD_DIM
NSA_KV_WIDTH = NSA_GROUPS * HEAD_DIM
CMP_BLOCK = 32
CMP_STRIDE = 16
CMP_HIDDEN = 256
SEL_BLOCK = 64
N_SELECT = 16
WINDOW = 512
Q_BLOCK = 128
ROPE_THETA = 10000.0
RWKV_IN_W = 3 * RWKV_WIDTH + W_LORA + A_LORA + G_LORA
N_EXPERTS = 256
TOP_K = 8
N_GROUPS = 8
TOPK_GROUPS = 4
EXPERT_DIM = 256
ROUTED_SCALE = 2.5
LN_EPS = 1e-5
NEG_INF = -1e30
FORCE_BONUS = 1e4

RWKV_CHUNK = 64
RWKV_HEAD_GROUP = 4
VMEM_LIMIT = 56 * 1024 * 1024
PROJ_ROWS = 512


def _bdot(a, b):
    return jnp.dot(a.astype(BF16), b.astype(BF16), preferred_element_type=F32)


def _bdot_nt(a, b):
    return lax.dot_general(a.astype(BF16), b.astype(BF16), (((1,), (1,)), ((), ())),
                           preferred_element_type=F32)


def _bdot_tn(a, b):
    return lax.dot_general(a.astype(BF16), b.astype(BF16), (((0,), (0,)), ((), ())),
                           preferred_element_type=F32)


def _hdot(a, b):
    return jnp.dot(a, b, precision=HIGHEST, preferred_element_type=F32)


def _sigmoid(x):
    return 1.0 / (1.0 + jnp.exp(-x))


def _matmul_kernel(x_ref, w_ref, o_ref):
    o_ref[...] = jnp.dot(x_ref[...].astype(BF16), w_ref[...], preferred_element_type=F32)


def _matmul(x, w, tm, tn):
    M, K = x.shape
    N = w.shape[1]
    return pl.pallas_call(
        _matmul_kernel,
        grid=(M // tm, N // tn),
        in_specs=[pl.BlockSpec((tm, K), lambda i, j: (i, 0)),
                  pl.BlockSpec((K, tn), lambda i, j: (0, j))],
        out_specs=pl.BlockSpec((tm, tn), lambda i, j: (i, j)),
        out_shape=jax.ShapeDtypeStruct((M, N), F32),
        compiler_params=pltpu.CompilerParams(
            dimension_semantics=("parallel", "parallel"), vmem_limit_bytes=VMEM_LIMIT),
        name="dense_proj",
    )(x, w)


def _matmul_t_kernel(x_ref, wt_ref, o_ref):
    o_ref[...] = lax.dot_general(wt_ref[...], x_ref[...].astype(BF16), (((1,), (1,)), ((), ())),
                                 preferred_element_type=F32)


def _matmul_t(x, w_t, tm):
    B, S, K = x.shape
    N = w_t.shape[0]
    return pl.pallas_call(
        _matmul_t_kernel,
        grid=(B, S // tm),
        in_specs=[pl.BlockSpec((None, tm, K), lambda b, s: (b, s, 0)),
                  pl.BlockSpec((N, K), lambda b, s: (0, 0))],
        out_specs=pl.BlockSpec((None, N, tm), lambda b, s: (b, 0, s)),
        out_shape=jax.ShapeDtypeStruct((B, N, S), F32),
        compiler_params=pltpu.CompilerParams(
            dimension_semantics=("parallel", "parallel"), vmem_limit_bytes=VMEM_LIMIT),
        name="dense_proj_t",
    )(x, w_t)


def _rwkv_kernel(p_ref, mu_ref, w0_ref, w2_ref, a0_ref, a2_ref, g2_ref, kk_ref, ka_ref, rk_ref,
                 lnw_ref, lnb_ref, o_ref, carry_ref, state_ref):
    C, H, N = RWKV_CHUNK, RWKV_HEADS, HEAD_DIM
    W = RWKV_WIDTH
    B = p_ref.shape[0]
    R = B * C

    @pl.when(pl.program_id(0) == 0)
    def _():
        carry_ref[...] = jnp.zeros_like(carry_ref)
        state_ref[...] = jnp.zeros_like(state_ref)

    def per_batch(x):
        return jnp.concatenate([jnp.broadcast_to(x[b].reshape(1, -1), (C, x.shape[-1])) for b in range(B)],
                               axis=0)

    p = p_ref[...].reshape(R, p_ref.shape[-1])
    row = lax.broadcasted_iota(jnp.int32, p.shape, 0)
    prev = jnp.where(row % C == 0, per_batch(carry_ref[...]), pltpu.roll(p, 1, axis=0))
    for b in range(B):
        carry_ref[b] = p[b * C + C - 1:b * C + C, :]
    xs = p + (prev - p) * mu_ref[...]
    r = xs[:, 0:W]
    k = xs[:, W:2 * W]
    v = xs[:, 2 * W:3 * W]
    wl = xs[:, 3 * W:3 * W + W_LORA]
    al = xs[:, 3 * W + W_LORA:3 * W + W_LORA + A_LORA]
    gl = xs[:, 3 * W + W_LORA + A_LORA:]

    z = -(w0_ref[...] + _hdot(jnp.tanh(wl), w2_ref[...]))
    softplus = jnp.maximum(z, 0.0) + jnp.log1p(jnp.exp(-jnp.abs(z)))
    logd = -jnp.exp(-softplus - 0.5)
    a = _sigmoid(a0_ref[...] + _hdot(al, a2_ref[...]))
    g = _hdot(_sigmoid(gl), g2_ref[...])

    kk = k * kk_ref[...]
    knew = k * (1.0 + (a - 1.0) * ka_ref[...])

    HG = RWKV_HEAD_GROUP
    GW = HG * N
    same_head_lanes = (lax.broadcasted_iota(jnp.int32, (GW, GW), 0) // N
                       == lax.broadcasted_iota(jnp.int32, (GW, GW), 1) // N)
    head_ones = jnp.where(same_head_lanes, 1.0, 0.0).astype(BF16)

    def head_sum(x):
        hi = x.astype(BF16)
        lo = (x - hi.astype(F32)).astype(BF16)
        return jnp.concatenate(
            [jnp.dot(hi[:, s:s + GW], head_ones, preferred_element_type=F32)
             + jnp.dot(lo[:, s:s + GW], head_ones, preferred_element_type=F32) for s in range(0, W, GW)],
            axis=-1)

    kk = kk / jnp.maximum(jnp.sqrt(head_sum(kk * kk)), 1e-12)
    lr_kk = kk * a

    ti = lax.broadcasted_iota(jnp.int32, (R, R), 0)
    tj = lax.broadcasted_iota(jnp.int32, (R, R), 1)
    same_chunk = (ti >= tj) & (ti // C == tj // C)
    cl = _hdot(same_chunk.astype(F32), logd)
    cl_end = per_batch(jnp.concatenate([cl[b * C + C - 1:b * C + C, :] for b in range(B)], axis=0))
    a_hat = -kk * jnp.exp(cl - logd)
    r_hat = r * jnp.exp(cl)
    inv_gam = jnp.exp(-cl)
    b_til = lr_kk * inv_gam
    k_til = knew * inv_gam
    to_end = jnp.exp(cl_end - cl)
    b_end = lr_kk * to_end
    k_end = knew * to_end
    gam_end = jnp.exp(cl_end)

    gt = lax.broadcasted_iota(jnp.int32, (C, GW), 0)
    gc = lax.broadcasted_iota(jnp.int32, (C, GW), 1) % N
    strict = gt > gc
    incl = gt >= gc
    eye = (gt == gc).astype(F32)
    bi = lax.broadcasted_iota(jnp.int32, (HG * C, GW), 0) // C
    bj = lax.broadcasted_iota(jnp.int32, (HG * C, GW), 1) // N
    same_head = bi == bj

    def block_diag(y):
        yb = y.astype(BF16)
        return jnp.where(same_head, jnp.concatenate([yb] * HG, axis=0), jnp.zeros((), BF16))

    def bd_dot(x, y):
        return jnp.dot(x.astype(BF16), block_diag(y), preferred_element_type=F32)

    def bd_dot_nt(x, y):
        return lax.dot_general(x.astype(BF16), block_diag(y), (((1,), (1,)), ((), ())),
                               preferred_element_type=F32)

    n_groups = H // HG
    units = [(b, gi) for b in range(B) for gi in range(n_groups)]
    n_units = range(len(units))
    cut = lambda x, b, gi: x[b * C:(b + 1) * C, gi * GW:(gi + 1) * GW]
    v_u = [cut(v, b, gi) for b, gi in units]
    ar = [jnp.concatenate([cut(a_hat, b, gi), cut(r_hat, b, gi)], axis=0) for b, gi in units]
    mb = [bd_dot_nt(ar[i], cut(b_til, *units[i])) for i in n_units]
    mk = [bd_dot_nt(ar[i], cut(k_til, *units[i])) for i in n_units]
    n_ab = [jnp.where(strict, mb[i][:C], 0.0) for i in n_units]
    m_rb = [jnp.where(incl, mb[i][C:], 0.0) for i in n_units]
    l_ak = [jnp.where(strict, mk[i][:C], 0.0) for i in n_units]
    m_rk = [jnp.where(incl, mk[i][C:], 0.0) for i in n_units]

    pw = list(n_ab)
    tinv = [eye + n_ab[i] for i in n_units]
    step = 2
    while step < C:
        pw = [bd_dot(pw[i], pw[i]) for i in n_units]
        tinv = [tinv[i] + bd_dot(tinv[i], pw[i]) for i in n_units]
        step *= 2

    s0 = [state_ref[i] for i in n_units]
    ars = [bd_dot_nt(ar[i], s0[i]) for i in n_units]
    lv = [bd_dot(l_ak[i], v_u[i]) for i in n_units]
    u = [bd_dot(tinv[i], ars[i][:C] + lv[i]) for i in n_units]
    outs = [ars[i][C:] + bd_dot(m_rb[i], u[i]) + bd_dot(m_rk[i], v_u[i]) for i in n_units]
    for i, (b, gi) in enumerate(units):
        uv = jnp.concatenate([u[i], v_u[i]], axis=0)
        bk_end = jnp.concatenate([cut(b_end, b, gi), cut(k_end, b, gi)], axis=0)
        cross = jnp.where(same_head, _bdot_tn(uv, bk_end), 0.0)
        upd = cross[0:N]
        for h in range(1, HG):
            upd = upd + cross[h * N:(h + 1) * N]
        state_ref[i] = s0[i] * gam_end[b * C:b * C + 1, gi * GW:(gi + 1) * GW] + upd

    o = jnp.concatenate([jnp.concatenate(outs[b * n_groups:(b + 1) * n_groups], axis=-1) for b in range(B)],
                        axis=0)
    mean = head_sum(o) * (1.0 / N)
    var = head_sum(jnp.square(o - mean)) * (1.0 / N)
    o = (o - mean) * lax.rsqrt(var + GN_EPS) * lnw_ref[...] + lnb_ref[...]
    bonus = head_sum(r * knew * rk_ref[...]) * v
    o_ref[...] = ((o + bonus) * g).reshape(o_ref.shape)


def _hdot_nt(a, b):
    return lax.dot_general(a, b, (((1,), (1,)), ((), ())), precision=HIGHEST,
                           preferred_element_type=F32)


def _hdot_tn(a, b):
    return lax.dot_general(a, b, (((0,), (0,)), ((), ())), precision=HIGHEST,
                           preferred_element_type=F32)


def _rwkv_time_mix(p_a, mu, w0, w2, a0, a2, g2, k_k, k_a, r_k, lnx_w, lnx_b):
    B, S, _ = p_a.shape
    C = RWKV_CHUNK
    row = lambda t: t.reshape(1, -1)
    full = lambda shape: pl.BlockSpec(shape, lambda s: (0,) * len(shape))
    n_units = B * RWKV_HEADS // RWKV_HEAD_GROUP
    return pl.pallas_call(
        _rwkv_kernel,
        grid=(S // C,),
        in_specs=[pl.BlockSpec((B, C, RWKV_IN_W), lambda s: (0, s, 0)),
                  full((1, RWKV_IN_W)), full((1, RWKV_WIDTH)), full((W_LORA, RWKV_WIDTH)),
                  full((1, RWKV_WIDTH)), full((A_LORA, RWKV_WIDTH)), full((G_LORA, RWKV_WIDTH)),
                  full((1, RWKV_WIDTH)), full((1, RWKV_WIDTH)), full((1, RWKV_WIDTH)),
                  full((1, RWKV_WIDTH)), full((1, RWKV_WIDTH))],
        out_specs=pl.BlockSpec((B, C, RWKV_WIDTH), lambda s: (0, s, 0)),
        out_shape=jax.ShapeDtypeStruct((B, S, RWKV_WIDTH), F32),
        scratch_shapes=[pltpu.VMEM((B, 1, RWKV_IN_W), F32),
                        pltpu.VMEM((n_units, HEAD_DIM, RWKV_HEAD_GROUP * HEAD_DIM), F32)],
        compiler_params=pltpu.CompilerParams(
            dimension_semantics=("arbitrary",), vmem_limit_bytes=VMEM_LIMIT),
        name="rwkv7_chunked",
    )(p_a, row(mu), row(w0), w2, row(a0), a2, g2, row(k_k), row(k_a), row(r_k), row(lnx_w), row(lnx_b))


NSA_KV_TILE = 1024
SEL_KEY_TILE = 1024
SEL_LANES = 128


def _rope_tables(pos, reps):
    half = HEAD_DIM // 2
    inv = ROPE_THETA ** (-jnp.arange(half, dtype=F32) / half)
    ang = pos.astype(F32)[:, None] * inv
    cos, sin = jnp.cos(ang), jnp.sin(ang)
    cosf = jnp.concatenate([cos, cos], -1)
    sinf = jnp.concatenate([-sin, sin], -1)
    return jnp.tile(cosf, (1, reps)), jnp.tile(sinf, (1, reps))


def _rope(x, cosf, sinf):
    width = x.shape[-1]
    lane = lax.broadcasted_iota(jnp.int32, x.shape, 1)
    first_half = (lane % HEAD_DIM) < HEAD_DIM // 2
    rot = jnp.where(first_half, pltpu.roll(x, width - HEAD_DIM // 2, axis=1),
                    pltpu.roll(x, HEAD_DIM // 2, axis=1))
    return x * cosf + rot * sinf


def _kv_layout_kernel(p_ref, cos_ref, sin_ref, kc_ref, vc_ref, ks_ref, vs_ref, kw_ref, vw_ref):
    ts = p_ref.shape[0]
    for i, o_ref in ((0, kc_ref), (1, vc_ref), (2, ks_ref), (4, kw_ref)):
        t = p_ref[:, i * NSA_KV_WIDTH:(i + 1) * NSA_KV_WIDTH]
        if i >= 2:
            t = _rope(t, cos_ref[...], sin_ref[...])
        for g in range(NSA_GROUPS):
            o_ref[g] = t[:, g * HEAD_DIM:(g + 1) * HEAD_DIM].astype(o_ref.dtype)
    pad_row = lax.broadcasted_iota(jnp.int32, (VT_ROWS - HEAD_DIM, ts), 0)
    ones_row = jnp.where(pad_row == 0, 1.0, 0.0)
    for i, o_ref in ((3, vs_ref), (5, vw_ref)):
        t_t = p_ref[:, i * NSA_KV_WIDTH:(i + 1) * NSA_KV_WIDTH].T
        for g in range(NSA_GROUPS):
            o_ref[g] = jnp.concatenate([t_t[g * HEAD_DIM:(g + 1) * HEAD_DIM], ones_row],
                                       axis=0).astype(o_ref.dtype)


def _kv_layout(p_b, cos2, sin2):
    B, S, _ = p_b.shape
    ts = min(NSA_KV_TILE, S)
    out_spec = pl.BlockSpec((None, NSA_GROUPS, ts, HEAD_DIM), lambda b, s: (b, 0, s, 0))
    vt_spec = pl.BlockSpec((None, NSA_GROUPS, VT_ROWS, ts), lambda b, s: (b, 0, 0, s))
    shp = lambda dt: jax.ShapeDtypeStruct((B, NSA_GROUPS, S, HEAD_DIM), dt)
    vt_shp = jax.ShapeDtypeStruct((B, NSA_GROUPS, VT_ROWS, S), BF16)
    return pl.pallas_call(
        _kv_layout_kernel,
        grid=(B, S // ts),
        in_specs=[pl.BlockSpec((None, ts, 6 * NSA_KV_WIDTH), lambda b, s: (b, s, 0)),
                  pl.BlockSpec((ts, NSA_KV_WIDTH), lambda b, s: (s, 0)),
                  pl.BlockSpec((ts, NSA_KV_WIDTH), lambda b, s: (s, 0))],
        out_specs=[out_spec, out_spec, out_spec, vt_spec, out_spec, vt_spec],
        out_shape=[shp(F32), shp(F32), shp(BF16), vt_shp, shp(BF16), vt_shp],
        compiler_params=pltpu.CompilerParams(
            dimension_semantics=("parallel", "parallel"), vmem_limit_bytes=VMEM_LIMIT),
        name="nsa_kv_layout",
    )(p_b, cos2, sin2)


def _compress_kernel(subk_ref, subv_ref, pek_ref, w1k_ref, w2k_ref, pev_ref, w1v_ref, w2v_ref,
                     cos_ref, sin_ref, kc_ref, vc_ref):
    n_sub = subk_ref.shape[0]
    half = CMP_STRIDE * HEAD_DIM

    def mlp(sub_ref, pe_ref, w1_ref, w2_ref):
        sub = sub_ref[...]
        top = _bdot(sub, w1_ref[:half, :])
        bot = _bdot(sub, w1_ref[half:, :])
        bias = _bdot(jnp.broadcast_to(pe_ref[...], (8, 2 * half)), w1_ref[...])[0:1, :]
        h = top + pltpu.roll(bot, n_sub - 1, axis=0) + bias
        return _bdot(jax.nn.gelu(h), w2_ref[...])

    kc = mlp(subk_ref, pek_ref, w1k_ref, w2k_ref)
    rot = jnp.concatenate([kc[:, HEAD_DIM // 2:], kc[:, :HEAD_DIM // 2]], axis=-1)
    kc_ref[...] = (kc * cos_ref[...] + rot * sin_ref[...]).astype(kc_ref.dtype)
    vc_ref[...] = mlp(subv_ref, pev_ref, w1v_ref, w2v_ref).astype(vc_ref.dtype)


def _compress(subk, subv, pe_k, w1_k, w2_k, pe_v, w1_v, w2_v, cos_c, sin_c):
    B, G, n_sub, width = subk.shape
    sub_spec = pl.BlockSpec((None, None, n_sub, width), lambda b, g: (b, g, 0, 0))
    full = lambda a: pl.BlockSpec(a.shape, lambda b, g: (0,) * a.ndim)
    out_spec = pl.BlockSpec((None, None, n_sub, HEAD_DIM), lambda b, g: (b, g, 0, 0))
    pe_k, pe_v = pe_k.reshape(1, -1), pe_v.reshape(1, -1)
    args = (pe_k, w1_k, w2_k, pe_v, w1_v, w2_v, cos_c, sin_c)
    return pl.pallas_call(
        _compress_kernel,
        grid=(B, G),
        in_specs=[sub_spec, sub_spec] + [full(a) for a in args],
        out_specs=[out_spec, out_spec],
        out_shape=[jax.ShapeDtypeStruct((B, G, n_sub, HEAD_DIM), BF16)] * 2,
        compiler_params=pltpu.CompilerParams(
            dimension_semantics=("parallel", "parallel"), vmem_limit_bytes=VMEM_LIMIT),
        name="nsa_compress",
    )(subk, subv, *args)


MAX_FLOOR = -1e20
MASK_BIG = 2.0 ** 100
LOG2_E = 1.4426950408889634
VT_ROWS = 80


def _nsa_kernel(q_ref, gate_ref, cos_ref, sin_ref, kc_ref, vc_ref, ks_ref, vst_ref, kw_ref, vwt_ref,
                mselt_ref, o_ref, blockbias_ref, *, n_pick):
    QB, HP, D = Q_BLOCK, NSA_HPG, HEAD_DIM
    qb = pl.program_id(2)
    n_cmp = kc_ref.shape[0]
    lanes4 = lambda x: jnp.concatenate([x] * HP, axis=1)

    heads = []
    for n in range(HP):
        qh = q_ref[n * D:(n + 1) * D, :]
        rot = jnp.concatenate([qh[D // 2:], qh[:D // 2]], axis=0)
        heads.append(qh * cos_ref[...] + rot * sin_ref[...])
    q4 = (jnp.concatenate(heads, axis=1) * (D ** -0.5 * LOG2_E)).astype(BF16)
    t_row = qb * QB + lax.broadcasted_iota(jnp.int32, (1, QB), 1)

    def softmax_cols(s_t, bias_t):
        sm = s_t + lanes4(bias_t)
        m = jnp.maximum(jnp.max(sm, axis=0, keepdims=True), MAX_FLOOR)
        return jnp.exp2(sm - m)

    cmp_end = lax.broadcasted_iota(jnp.int32, (n_cmp, 1), 0) * CMP_STRIDE + (CMP_BLOCK - 1)
    e_c = softmax_cols(jnp.dot(kc_ref[...], q4, preferred_element_type=F32),
                       jnp.where(cmp_end <= t_row, 0.0, -MASK_BIG))
    den_c = jnp.sum(e_c, axis=0, keepdims=True)
    p_c = e_c * (1.0 / jnp.where(den_c > 0.0, den_c, 1.0))
    o_c = _bdot_tn(vc_ref[...], p_c)
    p_sum = p_c[:, 0:QB]
    for n in range(1, HP):
        p_sum = p_sum + p_c[:, n * QB:(n + 1) * QB]
    p_hi = p_sum.astype(BF16)
    p_lo = (p_sum - p_hi.astype(F32)).astype(BF16)
    imp_t = (jnp.dot(mselt_ref[...], p_hi, preferred_element_type=F32)
             + jnp.dot(mselt_ref[...], p_lo, preferred_element_type=F32))

    j = lax.broadcasted_iota(jnp.int32, (SEL_LANES, QB), 0)
    cur = t_row // SEL_BLOCK
    valid = j * SEL_BLOCK <= t_row
    forced = (j == 0) | (j == cur) | (j == cur - 1)
    score = jnp.where(valid, imp_t + jnp.where(forced, FORCE_BONUS, 0.0), -1.0)
    for _ in range(n_pick):
        m = jnp.max(score, axis=0, keepdims=True)
        idx = jnp.min(jnp.where(score == m, j, SEL_LANES), axis=0, keepdims=True)
        score = jnp.where(j == idx, -2.0, score)
    blockbias_ref[...] = jnp.where((score == -2.0) & valid, 0.0, -MASK_BIG)

    KT = SEL_KEY_TILE
    blocks_per_tile = KT // SEL_BLOCK
    n_tiles = (qb * QB + QB + KT - 1) // KT

    def sel_step(kt, carry, causal):
        m_i, acc = carry
        start = pl.multiple_of(kt * KT, KT)
        s_t = jnp.dot(ks_ref[pl.ds(start, KT), :], q4, preferred_element_type=F32)
        bias = jnp.concatenate(
            [jnp.broadcast_to(blockbias_ref[pl.ds(kt * blocks_per_tile + jb, 1), :], (SEL_BLOCK, QB))
             for jb in range(blocks_per_tile)], axis=0)
        if causal:
            kpos = start + lax.broadcasted_iota(jnp.int32, (KT, 1), 0)
            bias = jnp.where(kpos <= t_row, bias, -MASK_BIG)
        sm = s_t + lanes4(bias)
        m_new = jnp.maximum(m_i, jnp.max(sm, axis=0, keepdims=True))
        e = jnp.exp2(sm - m_new).astype(BF16)
        acc_new = jnp.exp2(m_i - m_new) * acc + jnp.dot(vst_ref[:, pl.ds(start, KT)], e,
                                                        preferred_element_type=F32)
        return m_new, acc_new

    init = (jnp.full((1, HP * QB), MAX_FLOOR, F32), jnp.zeros((VT_ROWS, HP * QB), F32))
    carry = lax.fori_loop(0, n_tiles - 1, lambda kt, c: sel_step(kt, c, False), init)
    _, acc_s = sel_step(n_tiles - 1, carry, True)
    den_s = acc_s[D:D + 1]
    o_s = acc_s[:D] * (1.0 / jnp.where(den_s > 0.0, den_s, 1.0))

    span = WINDOW + QB
    w_start = pl.multiple_of(jnp.maximum(qb * QB - WINDOW, 0), QB)
    dist = t_row - (w_start + lax.broadcasted_iota(jnp.int32, (span, 1), 0))
    e_w = softmax_cols(jnp.dot(kw_ref[pl.ds(w_start, span), :], q4, preferred_element_type=F32),
                       jnp.where((dist >= 0) & (dist < WINDOW), 0.0, -MASK_BIG))
    acc_w = jnp.dot(vwt_ref[:, pl.ds(w_start, span)], e_w.astype(BF16), preferred_element_type=F32)
    den_w = acc_w[D:D + 1]
    o_w = acc_w[:D] * (1.0 / jnp.where(den_w > 0.0, den_w, 1.0))

    gates = _sigmoid(gate_ref[...])
    gate_row = lambda br: jnp.concatenate([gates[3 * n + br:3 * n + br + 1, :] for n in range(HP)], axis=1)
    o_t = gate_row(0) * o_c + gate_row(1) * o_s + gate_row(2) * o_w
    for n in range(HP):
        o_ref[:, n * D:(n + 1) * D] = o_t[:, n * QB:(n + 1) * QB].T


def _cmp_to_sel_matrix(n_cmp_rows, n_sel):
    ratio = SEL_BLOCK // CMP_STRIDE
    ci = np.arange(n_cmp_rows)[:, None]
    sj = np.arange(SEL_LANES)[None, :]
    m = sum(((ci + n) // ratio == sj).astype(np.float32) for n in range(CMP_BLOCK // CMP_STRIDE))
    m = m * (sj < n_sel) * (ci < n_cmp_rows - 1)
    return jnp.asarray(m.T, BF16)


def _nsa_attention(qg_t, kc, vc, ks, vst, kw, vwt, cos_t, sin_t):
    B, _, S = qg_t.shape
    n_sub = kc.shape[2]
    n_sel = S // SEL_BLOCK
    gw = NSA_HPG * HEAD_DIM
    gate_row0 = NSA_WIDTH // 128
    msel_t = _cmp_to_sel_matrix(n_sub, n_sel)
    at_bg = lambda shape: pl.BlockSpec((None, None) + shape, lambda b, g, i: (b, g, 0, 0))
    const = lambda a: pl.BlockSpec(a.shape, lambda b, g, i: (0, 0))
    return pl.pallas_call(
        functools.partial(_nsa_kernel, n_pick=min(N_SELECT, n_sel)),
        grid=(B, NSA_GROUPS, S // Q_BLOCK),
        in_specs=[pl.BlockSpec((None, gw, Q_BLOCK), lambda b, g, i: (b, g, i)),
                  pl.BlockSpec((None, 128, Q_BLOCK), lambda b, g, i: (b, gate_row0 + g, i)),
                  pl.BlockSpec((HEAD_DIM, Q_BLOCK), lambda b, g, i: (0, i)),
                  pl.BlockSpec((HEAD_DIM, Q_BLOCK), lambda b, g, i: (0, i)),
                  at_bg((n_sub, HEAD_DIM)), at_bg((n_sub, HEAD_DIM)),
                  at_bg((S, HEAD_DIM)), at_bg((VT_ROWS, S)), at_bg((S, HEAD_DIM)), at_bg((VT_ROWS, S)),
                  const(msel_t)],
        out_specs=pl.BlockSpec((None, Q_BLOCK, gw), lambda b, g, i: (b, i, g)),
        out_shape=jax.ShapeDtypeStruct((B, S, NSA_WIDTH), F32),
        scratch_shapes=[pltpu.VMEM((SEL_LANES, Q_BLOCK), F32)],
        compiler_params=pltpu.CompilerParams(
            dimension_semantics=("parallel", "parallel", "arbitrary"), vmem_limit_bytes=VMEM_LIMIT),
        name="nsa_attention",
    )(qg_t, qg_t, cos_t, sin_t, kc, vc, ks, vst, kw, vwt, msel_t)


def _nsa_branch(p_kv, qg_t, cmp_pe_k, cmp_w1_k, cmp_w2_k, cmp_pe_v, cmp_w1_v, cmp_w2_v):
    B, S, _ = p_kv.shape
    pos = jnp.arange(S)
    cos2, sin2 = _rope_tables(pos, NSA_GROUPS)
    kc_raw, vc_raw, ks, vst, kw, vwt = _kv_layout(p_kv, cos2, sin2)
    n_sub = S // CMP_STRIDE
    sub = lambda t: t.reshape(B, NSA_GROUPS, n_sub, CMP_STRIDE * HEAD_DIM)
    cos_c, sin_c = _rope_tables(jnp.arange(n_sub) * CMP_STRIDE + CMP_BLOCK - 1, 1)
    kc, vc = _compress(sub(kc_raw), sub(vc_raw), cmp_pe_k, cmp_w1_k, cmp_w2_k,
                       cmp_pe_v, cmp_w1_v, cmp_w2_v, cos_c, sin_c)
    cos_q, sin_q = _rope_tables(pos, 1)
    return _nsa_attention(qg_t, kc, vc, ks, vst, kw, vwt, cos_q.T, sin_q.T)


def _nsa_weight_columns(w_nsa):
    K = w_nsa.shape[0]
    q = w_nsa[:, :NSA_WIDTH]
    kv = w_nsa[:, NSA_WIDTH:NSA_WIDTH + 6 * NSA_KV_WIDTH]
    gates = w_nsa[:, NSA_WIDTH + 6 * NSA_KV_WIDTH:]
    per_group = NSA_HPG * 3
    gate_blocks = [jnp.pad(gates[:, g * per_group:(g + 1) * per_group], ((0, 0), (0, 128 - per_group)))
                   for g in range(NSA_GROUPS)]
    return jnp.concatenate([kv, q] + gate_blocks, axis=1)


def _layer_norm(h, g, b):
    mu = jnp.mean(h, axis=-1, keepdims=True)
    var = jnp.mean(jnp.square(h - mu), axis=-1, keepdims=True)
    return (h - mu) * lax.rsqrt(var + LN_EPS) * g + b


def _pack_bf16_halves(x):
    n = x.shape[-1] // 2
    bits = lax.bitcast_convert_type(x.astype(BF16).astype(F32), jnp.uint32)
    return (bits[:, n:] & jnp.uint32(0xFFFF0000)) | (bits[:, :n] >> 16)


def _unpack_bf16_halves(u):
    left = lax.bitcast_convert_type(u << 16, F32)
    right = lax.bitcast_convert_type(u & jnp.uint32(0xFFFF0000), F32)
    return left, right


def _mixer_out_kernel(x_ref, ya_ref, yb_ref, pg_ref, wa_ref, wb_ref, wo_ref, g_ref, b_ref, o_ref, op_ref,
                      *, alpha):
    d = x_ref.shape[-1]
    gate_a = _sigmoid(pg_ref[:, :d])
    gate_b = _sigmoid(pg_ref[:, d:])
    mixed = gate_a * _bdot(ya_ref[...], wa_ref[...]) + gate_b * _bdot(yb_ref[...], wb_ref[...])
    h = alpha * x_ref[...] + _bdot(mixed, wo_ref[...])
    out = _layer_norm(h, g_ref[...], b_ref[...])
    o_ref[...] = out
    op_ref[...] = _pack_bf16_halves(out)


def _mixer_out(xf, ya, yb, p_g, wa, wb, wo, ln_g, ln_b, alpha, tm=512):
    T, D = xf.shape
    rows = lambda w: pl.BlockSpec((tm, w), lambda i: (i, 0))
    full = lambda a: pl.BlockSpec(a.shape, lambda i: (0,) * a.ndim)
    ln_g, ln_b = ln_g.reshape(1, D), ln_b.reshape(1, D)
    return pl.pallas_call(
        functools.partial(_mixer_out_kernel, alpha=alpha),
        grid=(T // tm,),
        in_specs=[rows(D), rows(ya.shape[1]), rows(yb.shape[1]), rows(2 * D),
                  full(wa), full(wb), full(wo), full(ln_g), full(ln_b)],
        out_specs=[rows(D), rows(D // 2)],
        out_shape=[jax.ShapeDtypeStruct((T, D), F32), jax.ShapeDtypeStruct((T, D // 2), jnp.uint32)],
        compiler_params=pltpu.CompilerParams(
            dimension_semantics=("parallel",), vmem_limit_bytes=VMEM_LIMIT),
        name="mixer_out_ln",
    )(xf, ya, yb, p_g, wa, wb, wo, ln_g, ln_b)


ROUTER_TILE = 256
EXPERT_ROWS = 256
SC_TOKEN_CHUNK = 64
SC_ROW_CHUNK = 128
PICK_LANES = 128
LOWEST = -3.0e38


def _router_kernel(x_ref, rw_ref, bias_ref, eidx_ref, wts_ref, pos_ref, cnt_ref, carry_ref):
    tm, E = x_ref.shape[0], rw_ref.shape[1]
    per_group = E // N_GROUPS

    @pl.when(pl.program_id(0) == 0)
    def _():
        carry_ref[...] = jnp.zeros_like(carry_ref)

    scores = _sigmoid(_hdot(x_ref[...], rw_ref[...]))
    choice = scores + bias_ref[...]
    lane = lax.broadcasted_iota(jnp.int32, (tm, E), 1)
    grp = lane // per_group

    def first_max(vals):
        m = jnp.max(vals, axis=-1, keepdims=True)
        return m, jnp.min(jnp.where(vals == m, lane, E), axis=-1, keepdims=True)

    group_score = []
    for g in range(N_GROUPS):
        cg = jnp.where(grp == g, choice, LOWEST)
        m1, i1 = first_max(cg)
        m2 = jnp.max(jnp.where(lane == i1, LOWEST, cg), axis=-1, keepdims=True)
        group_score.append(m1 + m2)
    allowed = jnp.zeros((tm, E), jnp.bool_)
    for g in range(N_GROUPS):
        rank = jnp.zeros((tm, 1), jnp.int32)
        for o in range(N_GROUPS):
            if o != g:
                ahead = (group_score[o] > group_score[g]) if o > g else (group_score[o] >= group_score[g])
                rank = rank + ahead.astype(jnp.int32)
        allowed = allowed | ((grp == g) & (rank < TOPK_GROUPS))

    cur = jnp.where(allowed, choice, NEG_INF)
    picks = []
    for _ in range(TOP_K):
        idx = jnp.argmax(cur, axis=-1, keepdims=True).astype(jnp.int32)
        hit = lane == idx
        picks.append(idx)
        cur = jnp.where(hit, LOWEST, cur)
    sel = jnp.where(cur == LOWEST, 1.0, 0.0)
    gate = scores * sel
    gate = gate / jnp.sum(gate, axis=-1, keepdims=True) * ROUTED_SCALE

    ri = lax.broadcasted_iota(jnp.int32, (tm, tm), 0)
    ci = lax.broadcasted_iota(jnp.int32, (tm, tm), 1)
    before = jnp.dot((ri > ci).astype(BF16), sel.astype(BF16), preferred_element_type=F32)
    queue_pos = before + carry_ref[...]
    carry_ref[...] = carry_ref[...] + jnp.sum(sel, axis=0, keepdims=True)
    cnt_ref[...] = jnp.broadcast_to(carry_ref[...], cnt_ref.shape)

    out_lane = lax.broadcasted_iota(jnp.int32, (tm, PICK_LANES), 1)
    eidx = jnp.zeros((tm, PICK_LANES), jnp.int32)
    wts = jnp.zeros((tm, PICK_LANES), F32)
    pos = jnp.zeros((tm, PICK_LANES), F32)
    for kk, idx in enumerate(picks):
        hit = lane == idx
        eidx = jnp.where(out_lane == kk, idx, eidx)
        wts = jnp.where(out_lane == kk, jnp.sum(jnp.where(hit, gate, 0.0), axis=-1, keepdims=True), wts)
        pos = jnp.where(out_lane == kk, jnp.sum(jnp.where(hit, queue_pos, 0.0), axis=-1, keepdims=True), pos)
    eidx_ref[...] = eidx
    wts_ref[...] = wts
    pos_ref[...] = pos.astype(jnp.int32)


def _router(xf, router_w, router_bias):
    T, D = xf.shape
    E = router_w.shape[1]
    tm = ROUTER_TILE
    picks = lambda dt: jax.ShapeDtypeStruct((T, PICK_LANES), dt)
    pick_spec = pl.BlockSpec((tm, PICK_LANES), lambda i: (i, 0))
    return pl.pallas_call(
        _router_kernel,
        grid=(T // tm,),
        in_specs=[pl.BlockSpec((tm, D), lambda i: (i, 0)),
                  pl.BlockSpec((D, E), lambda i: (0, 0)),
                  pl.BlockSpec((1, E), lambda i: (0, 0))],
        out_specs=[pick_spec, pick_spec, pick_spec, pl.BlockSpec((8, E), lambda i: (0, 0))],
        out_shape=[picks(jnp.int32), picks(F32), picks(jnp.int32), jax.ShapeDtypeStruct((8, E), F32)],
        scratch_shapes=[pltpu.VMEM((1, E), F32)],
        compiler_params=pltpu.CompilerParams(
            dimension_semantics=("arbitrary",), vmem_limit_bytes=VMEM_LIMIT),
        name="moe_router",
    )(xf, router_w, router_bias.reshape(1, E))


def _dest_kernel(eidx_ref, pos_ref, start_ref, dest_ref):
    tm = eidx_ref.shape[0]
    E = start_ref.shape[1]
    lane = lax.broadcasted_iota(jnp.int32, (tm, E), 1)
    out_lane = lax.broadcasted_iota(jnp.int32, (tm, PICK_LANES), 1)
    eidx = eidx_ref[...]
    base = jnp.zeros((tm, PICK_LANES), jnp.int32)
    for kk in range(TOP_K):
        hit = lane == eidx[:, kk:kk + 1]
        start = jnp.sum(jnp.where(hit, start_ref[...], 0), axis=-1, keepdims=True)
        base = jnp.where(out_lane == kk, start, base)
    dest_ref[...] = base + pos_ref[...]


def _dest_rows(eidx, pos, pad_start):
    T = eidx.shape[0]
    E = pad_start.shape[0]
    tm = ROUTER_TILE
    pick_spec = pl.BlockSpec((tm, PICK_LANES), lambda i: (i, 0))
    return pl.pallas_call(
        _dest_kernel,
        grid=(T // tm,),
        in_specs=[pick_spec, pick_spec, pl.BlockSpec((1, E), lambda i: (0, 0))],
        out_specs=pick_spec,
        out_shape=jax.ShapeDtypeStruct((T, PICK_LANES), jnp.int32),
        compiler_params=pltpu.CompilerParams(
            dimension_semantics=("parallel",), vmem_limit_bytes=VMEM_LIMIT),
        name="moe_dest_rows",
    )(eidx, pos, pad_start.reshape(1, E))


def _sc_mesh():
    return plsc.VectorSubcoreMesh(core_axis_name="c", subcore_axis_name="s")


def _sc_scatter_rows(x, dest_t, n_rows):
    T, D = x.shape
    K = dest_t.shape[0]
    mesh = _sc_mesh()
    nc, nw = mesh.num_cores, mesh.num_cores * mesh.num_subcores
    per_w = T // nw
    chunk = min(SC_TOKEN_CHUNK, per_w)
    n_chunks = per_w // chunk
    idx = dest_t.reshape(K, nw, n_chunks, chunk).transpose(1, 2, 0, 3).reshape(nw, n_chunks * K, chunk)

    @functools.partial(
        pl.kernel, mesh=mesh,
        out_type=jax.ShapeDtypeStruct((n_rows, D), x.dtype),
        scratch_types=[pltpu.VMEM((n_chunks * K, chunk), jnp.int32),
                       pltpu.VMEM((chunk, D), x.dtype),
                       pltpu.SemaphoreType.DMA],
    )
    def scatter(x_hbm, idx_hbm, out_hbm, idx_v, rows_v, sem):
        wid = lax.axis_index("s") * nc + lax.axis_index("c")
        pltpu.sync_copy(idx_hbm.at[wid], idx_v)

        @pl.loop(0, n_chunks)
        def _(j):
            pltpu.sync_copy(x_hbm.at[pl.ds(wid * per_w + j * chunk, chunk)], rows_v)
            copies = [pltpu.async_copy(rows_v, out_hbm.at[idx_v.at[j * K + kk]], sem) for kk in range(K)]
            for c in copies:
                c.wait()

    return scatter(x, idx)


def _sc_gather_rows(src, idx):
    M = idx.shape[0]
    D = src.shape[1]
    mesh = _sc_mesh()
    nc, nw = mesh.num_cores, mesh.num_cores * mesh.num_subcores
    per_w = M // nw
    chunk = min(SC_ROW_CHUNK, per_w)
    n_chunks = per_w // chunk

    @functools.partial(
        pl.kernel, mesh=mesh,
        out_type=jax.ShapeDtypeStruct((M, D), src.dtype),
        scratch_types=[pltpu.VMEM((n_chunks, chunk), jnp.int32),
                       pltpu.VMEM((chunk, D), src.dtype),
                       pltpu.SemaphoreType.DMA],
    )
    def gather(src_hbm, idx_hbm, out_hbm, idx_v, rows_v, sem):
        wid = lax.axis_index("s") * nc + lax.axis_index("c")
        pltpu.sync_copy(idx_hbm.at[wid], idx_v)

        @pl.loop(0, n_chunks)
        def _(j):
            pltpu.async_copy(src_hbm.at[idx_v.at[j]], rows_v, sem).wait()
            pltpu.sync_copy(rows_v, out_hbm.at[pl.ds(wid * per_w + j * chunk, chunk)])

    return gather(src, idx.reshape(nw, n_chunks, chunk))


def _expert_kernel(distinct_e_ref, blk_ord_ref, blk_new_ref, blk_rows_ref, n_used_ref, n_distinct_ref,
                   x_ref, wgu_hbm, wd_hbm, o_ref, wgu_buf, wd_buf, wgu_bf, wd_bf, sem):
    i = pl.program_id(0)
    live = i < n_used_ref[0]
    ordinal = blk_ord_ref[i]
    slot = ordinal % 2

    def weight_copies(k, s):
        e = distinct_e_ref[k]
        return (pltpu.make_async_copy(wgu_hbm.at[e], wgu_buf.at[s], sem.at[0, s]),
                pltpu.make_async_copy(wd_hbm.at[e], wd_buf.at[s], sem.at[1, s]))

    @pl.when(i == 0)
    def _():
        for c in weight_copies(0, 0):
            c.start()

    @pl.when(live & (blk_new_ref[i] == 1))
    def _():
        for c in weight_copies(ordinal, slot):
            c.wait()

        @pl.when(ordinal + 1 < n_distinct_ref[0])
        def _():
            for c in weight_copies(ordinal + 1, 1 - slot):
                c.start()

        wgu_bf[...] = wgu_buf[slot].astype(BF16)
        wd_bf[...] = wd_buf[slot].astype(BF16)

    @pl.when(live)
    def _():
        hidden = wd_bf.shape[0]
        half = x_ref.shape[1]
        row = lax.broadcasted_iota(jnp.int32, x_ref.shape, 0)
        left, right = _unpack_bf16_halves(x_ref[...])
        real = row < blk_rows_ref[i]
        left = jnp.where(real, left, 0.0).astype(BF16)
        right = jnp.where(real, right, 0.0).astype(BF16)
        h = (jnp.dot(left, wgu_bf[:half, :], preferred_element_type=F32)
             + jnp.dot(right, wgu_bf[half:, :], preferred_element_type=F32))
        gate, up = h[:, :hidden], h[:, hidden:]
        act = (gate * _sigmoid(gate) * up).astype(BF16)
        o_ref[...] = _pack_bf16_halves(jnp.dot(act, wd_bf[...], preferred_element_type=F32))

    @pl.when(jnp.logical_not(live))
    def _():
        o_ref[...] = jnp.zeros_like(o_ref)


def _expert_ffn(xs, blk_e, blk_rows, n_used, w_gu, w_down):
    n_rows, half = xs.shape
    E, D, two_h = w_gu.shape
    n_blocks = n_rows // EXPERT_ROWS
    idx = jnp.arange(n_blocks, dtype=jnp.int32)
    is_live = idx < n_used[0]
    blk_new = (is_live & ((idx == 0) | (blk_e != jnp.roll(blk_e, 1)))).astype(jnp.int32)
    blk_ord = (jnp.cumsum(blk_new) - 1).astype(jnp.int32)
    n_distinct = blk_ord[-1:] + 1
    distinct_e = jnp.zeros((n_blocks,), jnp.int32).at[blk_ord].max(blk_e * is_live)

    live = lambda i, nu: jnp.minimum(i, nu[0] - 1)
    grid_spec = pltpu.PrefetchScalarGridSpec(
        num_scalar_prefetch=6,
        grid=(n_blocks,),
        in_specs=[pl.BlockSpec((EXPERT_ROWS, half), lambda i, de, bo, bn, br, nu, nd: (live(i, nu), 0)),
                  pl.BlockSpec(memory_space=pl.ANY), pl.BlockSpec(memory_space=pl.ANY)],
        out_specs=pl.BlockSpec((EXPERT_ROWS, half), lambda i, de, bo, bn, br, nu, nd: (i, 0)),
        scratch_shapes=[pltpu.VMEM((2, D, two_h), F32), pltpu.VMEM((2, two_h // 2, D), F32),
                        pltpu.VMEM((D, two_h), BF16), pltpu.VMEM((two_h // 2, D), BF16),
                        pltpu.SemaphoreType.DMA((2, 2))],
    )
    return pl.pallas_call(
        _expert_kernel,
        grid_spec=grid_spec,
        out_shape=jax.ShapeDtypeStruct((n_rows, half), jnp.uint32),
        compiler_params=pltpu.CompilerParams(
            dimension_semantics=("arbitrary",), vmem_limit_bytes=VMEM_LIMIT),
        name="moe_experts",
    )(distinct_e, blk_ord, blk_new, blk_rows, n_used, n_distinct, xs, w_gu, w_down)


def _moe_out_kernel(x_ref, yk_ref, wts_ref, sgu_ref, sd_ref, g_ref, b_ref, o_ref, *, alpha):
    x = x_ref[...]
    hidden = sd_ref.shape[0]
    h = _bdot(x, sgu_ref[...])
    gate, up = h[:, :hidden], h[:, hidden:]
    ffn = _bdot(gate * _sigmoid(gate) * up, sd_ref[...])
    wts = wts_ref[...]
    routed_left = routed_right = None
    for kk in range(TOP_K):
        left, right = _unpack_bf16_halves(yk_ref[kk])
        w = wts[:, kk:kk + 1]
        routed_left = w * left if kk == 0 else routed_left + w * left
        routed_right = w * right if kk == 0 else routed_right + w * right
    ffn = ffn + jnp.concatenate([routed_left, routed_right], axis=-1)
    o_ref[...] = _layer_norm(alpha * x + ffn, g_ref[...], b_ref[...])


def _moe_out(xf, yk, wts, sw_gu, sw_down, ln_g, ln_b, alpha, tm=128):
    T, D = xf.shape
    rows = lambda w: pl.BlockSpec((tm, w), lambda i: (i, 0))
    full = lambda a: pl.BlockSpec(a.shape, lambda i: (0,) * a.ndim)
    ln_g, ln_b = ln_g.reshape(1, D), ln_b.reshape(1, D)
    return pl.pallas_call(
        functools.partial(_moe_out_kernel, alpha=alpha),
        grid=(T // tm,),
        in_specs=[rows(D), pl.BlockSpec((TOP_K, tm, D // 2), lambda i: (0, i, 0)), rows(PICK_LANES),
                  full(sw_gu), full(sw_down), full(ln_g), full(ln_b)],
        out_specs=rows(D),
        out_shape=jax.ShapeDtypeStruct((T, D), F32),
        compiler_params=pltpu.CompilerParams(
            dimension_semantics=("parallel",), vmem_limit_bytes=VMEM_LIMIT),
        name="moe_combine_ln",
    )(xf, yk, wts, sw_gu, sw_down, ln_g, ln_b)


def _moe_ffn_ln(xf, xp, router_w, router_bias, w_gu, w_down, sw_gu, sw_down, ln_g, ln_b, alpha):
    T, D = xf.shape
    E = router_w.shape[1]
    BM = EXPERT_ROWS
    eidx, wts, pos, cnt = _router(xf, router_w, router_bias)
    counts = cnt[0].astype(jnp.int32)
    padded = (counts + BM - 1) // BM * BM
    pad_end = jnp.cumsum(padded)
    pad_start = pad_end - padded
    n_rows = T * TOP_K + E * BM
    n_blocks = n_rows // BM
    blk_row0 = jnp.arange(n_blocks, dtype=jnp.int32) * BM
    blk_e = jnp.minimum(jnp.sum((pad_end[None, :] <= blk_row0[:, None]).astype(jnp.int32), axis=1), E - 1)
    blk_rows = jnp.clip(pad_start[blk_e] + counts[blk_e] - blk_row0, 0, BM).astype(jnp.int32)
    n_used = (pad_end[-1:] // BM).astype(jnp.int32)
    dest = _dest_rows(eidx, pos, pad_start)[:, :TOP_K]
    dest_t = dest.T
    xs = _sc_scatter_rows(xp, dest_t, n_rows)
    ys = _expert_ffn(xs, blk_e, blk_rows, n_used, w_gu, w_down)
    yk = _sc_gather_rows(ys, dest_t.reshape(-1)).reshape(TOP_K, T, D // 2)
    return _moe_out(xf, yk, wts, sw_gu, sw_down, ln_g, ln_b, alpha)


def kernel(x, w_in, tshift_mu, rwkv_w0, rwkv_w2, rwkv_a0, rwkv_a2, rwkv_g2, rwkv_k_k, rwkv_k_a, rwkv_r_k, rwkv_lnx_w, rwkv_lnx_b, cmp_pe_k, cmp_w1_k, cmp_w2_k, cmp_pe_v, cmp_w1_v, cmp_w2_v, w_branch_a, w_branch_b, w_out, ln1_g, ln1_b, router_w, router_bias, exp_w_gu, exp_w_down, shared_w_gu, shared_w_down, ln2_g, ln2_b):
    B, S, D = x.shape
    depth = w_in.shape[0]
    alpha = (2 * depth) ** 0.25
    nsa_w = w_in.shape[2] - RWKV_IN_W - 2 * D
    for l in range(depth):
        xf = x.reshape(B * S, D)
        w_l = w_in[l]
        w_a = w_l[:, :RWKV_IN_W].astype(BF16)
        w_b = _nsa_weight_columns(w_l[:, RWKV_IN_W:RWKV_IN_W + nsa_w]).astype(BF16)
        w_g = w_l[:, RWKV_IN_W + nsa_w:].astype(BF16)
        kv_w = 6 * NSA_KV_WIDTH
        p_a = _matmul(xf, w_a, PROJ_ROWS, w_a.shape[1]).reshape(B, S, -1)
        p_kv = _matmul(xf, w_b[:, :kv_w], PROJ_ROWS, kv_w).reshape(B, S, -1)
        qg_t = _matmul_t(x, w_b[:, kv_w:].T, PROJ_ROWS)
        p_g = _matmul(xf, w_g, PROJ_ROWS, w_g.shape[1])
        y_a = _rwkv_time_mix(p_a, tshift_mu[l], rwkv_w0[l], rwkv_w2[l], rwkv_a0[l], rwkv_a2[l], rwkv_g2[l],
                             rwkv_k_k[l], rwkv_k_a[l], rwkv_r_k[l].reshape(-1), rwkv_lnx_w[l], rwkv_lnx_b[l])
        y_b = _nsa_branch(p_kv, qg_t, cmp_pe_k[l], cmp_w1_k[l], cmp_w2_k[l],
                          cmp_pe_v[l], cmp_w1_v[l], cmp_w2_v[l])
        x1, x1p = _mixer_out(xf, y_a.reshape(B * S, -1), y_b.reshape(B * S, -1), p_g,
                             w_branch_a[l].astype(BF16), w_branch_b[l].astype(BF16), w_out[l].astype(BF16),
                             ln1_g[l], ln1_b[l], alpha)
        x2 = _moe_ffn_ln(x1, x1p, router_w[l], router_bias[l], exp_w_gu[l], exp_w_down[l],
                         shared_w_gu[l].astype(BF16), shared_w_down[l].astype(BF16), ln2_g[l], ln2_b[l], alpha)
        x = x2.reshape(B, S, D)
    return x
```

```python
import functools

import numpy as np
import jax
import jax.numpy as jnp
from jax import lax
from jax.experimental import pallas as pl
from jax.experimental.pallas import tpu as pltpu
from jax.experimental.pallas import tpu_sc as plsc

F32 = jnp.float32
BF16 = jnp.bfloat16
HIGHEST = lax.Precision.HIGHEST

RWKV_HEADS = 8
HEAD_DIM = 64
RWKV_WIDTH = RWKV_HEADS * HEAD_DIM
W_LORA = 64
A_LORA = 64
G_LORA = 128
GN_EPS = 64e-5
NSA_HEADS = 8
NSA_GROUPS = 2
NSA_HPG = NSA_HEADS // NSA_GROUPS
NSA_WIDTH = NSA_HEADS * HEAD_DIM
NSA_KV_WIDTH = NSA_GROUPS * HEAD_DIM
CMP_BLOCK = 32
CMP_STRIDE = 16
CMP_HIDDEN = 256
SEL_BLOCK = 64
N_SELECT = 16
WINDOW = 512
Q_BLOCK = 128
ROPE_THETA = 10000.0
RWKV_IN_W = 3 * RWKV_WIDTH + W_LORA + A_LORA + G_LORA
N_EXPERTS = 256
TOP_K = 8
N_GROUPS = 8
TOPK_GROUPS = 4
EXPERT_DIM = 256
ROUTED_SCALE = 2.5
LN_EPS = 1e-5
NEG_INF = -1e30
FORCE_BONUS = 1e4

RWKV_CHUNK = 64
RWKV_HEAD_GROUP = 4
RWKV_STEP_CHUNKS = 2
VMEM_LIMIT = 56 * 1024 * 1024
PROJ_ROWS = 512


def _bdot(a, b):
    return jnp.dot(a.astype(BF16), b.astype(BF16), preferred_element_type=F32)


def _bdot_nt(a, b):
    return lax.dot_general(a.astype(BF16), b.astype(BF16), (((1,), (1,)), ((), ())),
                           preferred_element_type=F32)


def _bdot_tn(a, b):
    return lax.dot_general(a.astype(BF16), b.astype(BF16), (((0,), (0,)), ((), ())),
                           preferred_element_type=F32)


def _hdot(a, b):
    return jnp.dot(a, b, precision=HIGHEST, preferred_element_type=F32)


def _sigmoid(x):
    return 1.0 / (1.0 + jnp.exp(-x))


def _matmul_kernel(x_ref, w_ref, o_ref):
    o_ref[...] = jnp.dot(x_ref[...].astype(BF16), w_ref[...], preferred_element_type=F32)


def _matmul(x, w, tm, tn):
    M, K = x.shape
    N = w.shape[1]
    return pl.pallas_call(
        _matmul_kernel,
        grid=(M // tm, N // tn),
        in_specs=[pl.BlockSpec((tm, K), lambda i, j: (i, 0)),
                  pl.BlockSpec((K, tn), lambda i, j: (0, j))],
        out_specs=pl.BlockSpec((tm, tn), lambda i, j: (i, j)),
        out_shape=jax.ShapeDtypeStruct((M, N), F32),
        compiler_params=pltpu.CompilerParams(
            dimension_semantics=("parallel", "parallel"), vmem_limit_bytes=VMEM_LIMIT),
        name="dense_proj",
    )(x, w)


def _matmul_t_kernel(x_ref, wt_ref, o_ref):
    o_ref[...] = lax.dot_general(wt_ref[...], x_ref[...].astype(BF16), (((1,), (1,)), ((), ())),
                                 preferred_element_type=F32)


def _matmul_t(x, w_t, tm):
    B, S, K = x.shape
    N = w_t.shape[0]
    return pl.pallas_call(
        _matmul_t_kernel,
        grid=(B, S // tm),
        in_specs=[pl.BlockSpec((None, tm, K), lambda b, s: (b, s, 0)),
                  pl.BlockSpec((N, K), lambda b, s: (0, 0))],
        out_specs=pl.BlockSpec((None, N, tm), lambda b, s: (b, 0, s)),
        out_shape=jax.ShapeDtypeStruct((B, N, S), F32),
        compiler_params=pltpu.CompilerParams(
            dimension_semantics=("parallel", "parallel"), vmem_limit_bytes=VMEM_LIMIT),
        name="dense_proj_t",
    )(x, w_t)


def _rwkv_kernel(p_ref, mu_ref, w0_ref, w2_ref, a0_ref, a2_ref, g2_ref, kk_ref, ka_ref, rk_ref,
                 lnw_ref, lnb_ref, o_ref, carry_ref, state_ref):
    C, H, N = RWKV_CHUNK, RWKV_HEADS, HEAD_DIM
    W = RWKV_WIDTH
    B = p_ref.shape[0]
    NC = p_ref.shape[1] // C
    L = NC * C
    R = B * L

    @pl.when(pl.program_id(0) == 0)
    def _():
        carry_ref[...] = jnp.zeros_like(carry_ref)
        state_ref[...] = jnp.zeros_like(state_ref)

    def per_block(x, rows):
        return jnp.concatenate(
            [jnp.broadcast_to(x[i].reshape(1, -1), (rows, x.shape[-1])) for i in range(x.shape[0])], axis=0)

    p = p_ref[...].reshape(R, p_ref.shape[-1])
    row = lax.broadcasted_iota(jnp.int32, p.shape, 0)
    prev = jnp.where(row % L == 0, per_block(carry_ref[...], L), pltpu.roll(p, 1, axis=0))
    for b in range(B):
        carry_ref[b] = p[b * L + L - 1:b * L + L, :]
    xs = p + (prev - p) * mu_ref[...]
    r = xs[:, 0:W]
    k = xs[:, W:2 * W]
    v = xs[:, 2 * W:3 * W]
    wl = xs[:, 3 * W:3 * W + W_LORA]
    al = xs[:, 3 * W + W_LORA:3 * W + W_LORA + A_LORA]
    gl = xs[:, 3 * W + W_LORA + A_LORA:]

    z = -(w0_ref[...] + _hdot(jnp.tanh(wl), w2_ref[...]))
    softplus = jnp.maximum(z, 0.0) + jnp.log1p(jnp.exp(-jnp.abs(z)))
    logd = -jnp.exp(-softplus - 0.5)
    a = _sigmoid(a0_ref[...] + _hdot(al, a2_ref[...]))
    g = _hdot(_sigmoid(gl), g2_ref[...])

    kk = k * kk_ref[...]
    knew = k * (1.0 + (a - 1.0) * ka_ref[...])

    HG = RWKV_HEAD_GROUP
    GW = HG * N
    same_head_lanes = (lax.broadcasted_iota(jnp.int32, (GW, GW), 0) // N
                       == lax.broadcasted_iota(jnp.int32, (GW, GW), 1) // N)
    head_ones = jnp.where(same_head_lanes, 1.0, 0.0).astype(BF16)

    def head_sum(x):
        hi = x.astype(BF16)
        lo = (x - hi.astype(F32)).astype(BF16)
        return jnp.concatenate(
            [jnp.dot(hi[:, s:s + GW], head_ones, preferred_element_type=F32)
             + jnp.dot(lo[:, s:s + GW], head_ones, preferred_element_type=F32) for s in range(0, W, GW)],
            axis=-1)

    kk = kk / jnp.maximum(jnp.sqrt(head_sum(kk * kk)), 1e-12)
    lr_kk = kk * a

    ti = lax.broadcasted_iota(jnp.int32, (R, R), 0)
    tj = lax.broadcasted_iota(jnp.int32, (R, R), 1)
    same_chunk = (ti >= tj) & (ti // C == tj // C)
    cl = _hdot(same_chunk.astype(F32), logd)
    cl_end = per_block(jnp.concatenate([cl[i * C + C - 1:i * C + C, :] for i in range(B * NC)], axis=0), C)
    a_hat = -kk * jnp.exp(cl - logd)
    r_hat = r * jnp.exp(cl)
    inv_gam = jnp.exp(-cl)
    b_til = lr_kk * inv_gam
    k_til = knew * inv_gam
    to_end = jnp.exp(cl_end - cl)
    b_end = lr_kk * to_end
    k_end = knew * to_end
    gam_end = jnp.exp(cl_end)

    gt = lax.broadcasted_iota(jnp.int32, (C, GW), 0)
    gc = lax.broadcasted_iota(jnp.int32, (C, GW), 1) % N
    strict = gt > gc
    incl = gt >= gc
    eye = (gt == gc).astype(F32)
    bi = lax.broadcasted_iota(jnp.int32, (HG * C, GW), 0) // C
    bj = lax.broadcasted_iota(jnp.int32, (HG * C, GW), 1) // N
    same_head = bi == bj

    def block_diag(y):
        yb = y.astype(BF16)
        return jnp.where(same_head, jnp.concatenate([yb] * HG, axis=0), jnp.zeros((), BF16))

    def bd_dot(x, y):
        return jnp.dot(x.astype(BF16), block_diag(y), preferred_element_type=F32)

    def bd_dot_nt(x, y):
        return lax.dot_general(x.astype(BF16), block_diag(y), (((1,), (1,)), ((), ())),
                               preferred_element_type=F32)

    n_groups = H // HG
    units = [(b, c, gi) for b in range(B) for c in range(NC) for gi in range(n_groups)]
    n_units = range(len(units))
    cut = lambda x, b, c, gi: x[(b * NC + c) * C:(b * NC + c + 1) * C, gi * GW:(gi + 1) * GW]
    v_u = [cut(v, *un) for un in units]
    ar = [jnp.concatenate([cut(a_hat, *un), cut(r_hat, *un)], axis=0) for un in units]
    mb = [bd_dot_nt(ar[i], cut(b_til, *units[i])) for i in n_units]
    mk = [bd_dot_nt(ar[i], cut(k_til, *units[i])) for i in n_units]
    n_ab = [jnp.where(strict, mb[i][:C], 0.0) for i in n_units]
    m_rb = [jnp.where(incl, mb[i][C:], 0.0) for i in n_units]
    l_ak = [jnp.where(strict, mk[i][:C], 0.0) for i in n_units]
    m_rk = [jnp.where(incl, mk[i][C:], 0.0) for i in n_units]

    pw = list(n_ab)
    tinv = [eye + n_ab[i] for i in n_units]
    step = 2
    while step < C:
        pw = [bd_dot(pw[i], pw[i]) for i in n_units]
        tinv = [tinv[i] + bd_dot(tinv[i], pw[i]) for i in n_units]
        step *= 2
    lv = [bd_dot(l_ak[i], v_u[i]) for i in n_units]

    state = {(b, gi): state_ref[b * n_groups + gi] for b in range(B) for gi in range(n_groups)}
    outs = {}
    for c in range(NC):
        live = [i for i in n_units if units[i][1] == c]
        s0 = {i: state[(units[i][0], units[i][2])] for i in live}
        ars = {i: bd_dot_nt(ar[i], s0[i]) for i in live}
        u = {i: bd_dot(tinv[i], ars[i][:C] + lv[i]) for i in live}
        for i in live:
            outs[units[i]] = ars[i][C:] + bd_dot(m_rb[i], u[i]) + bd_dot(m_rk[i], v_u[i])
        for i in live:
            b, _, gi = units[i]
            uv = jnp.concatenate([u[i], v_u[i]], axis=0)
            bk_end = jnp.concatenate([cut(b_end, *units[i]), cut(k_end, *units[i])], axis=0)
            cross = jnp.where(same_head, _bdot_tn(uv, bk_end), 0.0)
            upd = cross[0:N]
            for h in range(1, HG):
                upd = upd + cross[h * N:(h + 1) * N]
            state[(b, gi)] = s0[i] * cut(gam_end, *units[i])[0:1] + upd
    for (b, gi), s_new in state.items():
        state_ref[b * n_groups + gi] = s_new

    o = jnp.concatenate([jnp.concatenate([outs[(b, c, gi)] for gi in range(n_groups)], axis=-1)
                         for b in range(B) for c in range(NC)], axis=0)
    mean = head_sum(o) * (1.0 / N)
    var = head_sum(jnp.square(o - mean)) * (1.0 / N)
    o = (o - mean) * lax.rsqrt(var + GN_EPS) * lnw_ref[...] + lnb_ref[...]
    bonus = head_sum(r * knew * rk_ref[...]) * v
    o_ref[...] = ((o + bonus) * g).reshape(o_ref.shape)


def _hdot_nt(a, b):
    return lax.dot_general(a, b, (((1,), (1,)), ((), ())), precision=HIGHEST,
                           preferred_element_type=F32)


def _hdot_tn(a, b):
    return lax.dot_general(a, b, (((0,), (0,)), ((), ())), precision=HIGHEST,
                           preferred_element_type=F32)


def _rwkv_time_mix(p_a, mu, w0, w2, a0, a2, g2, k_k, k_a, r_k, lnx_w, lnx_b):
    B, S, _ = p_a.shape
    L = RWKV_CHUNK * RWKV_STEP_CHUNKS
    row = lambda t: t.reshape(1, -1)
    full = lambda shape: pl.BlockSpec(shape, lambda s: (0,) * len(shape))
    n_units = B * RWKV_HEADS // RWKV_HEAD_GROUP
    return pl.pallas_call(
        _rwkv_kernel,
        grid=(S // L,),
        in_specs=[pl.BlockSpec((B, L, RWKV_IN_W), lambda s: (0, s, 0)),
                  full((1, RWKV_IN_W)), full((1, RWKV_WIDTH)), full((W_LORA, RWKV_WIDTH)),
                  full((1, RWKV_WIDTH)), full((A_LORA, RWKV_WIDTH)), full((G_LORA, RWKV_WIDTH)),
                  full((1, RWKV_WIDTH)), full((1, RWKV_WIDTH)), full((1, RWKV_WIDTH)),
                  full((1, RWKV_WIDTH)), full((1, RWKV_WIDTH))],
        out_specs=pl.BlockSpec((B, L, RWKV_WIDTH), lambda s: (0, s, 0)),
        out_shape=jax.ShapeDtypeStruct((B, S, RWKV_WIDTH), F32),
        scratch_shapes=[pltpu.VMEM((B, 1, RWKV_IN_W), F32),
                        pltpu.VMEM((n_units, HEAD_DIM, RWKV_HEAD_GROUP * HEAD_DIM), F32)],
        compiler_params=pltpu.CompilerParams(
            dimension_semantics=("arbitrary",), vmem_limit_bytes=VMEM_LIMIT),
        name="rwkv7_chunked",
    )(p_a, row(mu), row(w0), w2, row(a0), a2, g2, row(k_k), row(k_a), row(r_k), row(lnx_w), row(lnx_b))


NSA_KV_TILE = 1024
SEL_KEY_TILE = 1024
SEL_LANES = 128


def _rope_tables(pos, reps):
    half = HEAD_DIM // 2
    inv = ROPE_THETA ** (-jnp.arange(half, dtype=F32) / half)
    ang = pos.astype(F32)[:, None] * inv
    cos, sin = jnp.cos(ang), jnp.sin(ang)
    cosf = jnp.concatenate([cos, cos], -1)
    sinf = jnp.concatenate([-sin, sin], -1)
    return jnp.tile(cosf, (1, reps)), jnp.tile(sinf, (1, reps))


def _rope(x, cosf, sinf):
    width = x.shape[-1]
    lane = lax.broadcasted_iota(jnp.int32, x.shape, 1)
    first_half = (lane % HEAD_DIM) < HEAD_DIM // 2
    rot = jnp.where(first_half, pltpu.roll(x, width - HEAD_DIM // 2, axis=1),
                    pltpu.roll(x, HEAD_DIM // 2, axis=1))
    return x * cosf + rot * sinf


def _kv_layout_kernel(p_ref, cos_ref, sin_ref, kc_ref, vc_ref, ks_ref, vs_ref, kw_ref, vw_ref):
    ts = p_ref.shape[0]
    for i, o_ref in ((0, kc_ref), (1, vc_ref), (2, ks_ref), (4, kw_ref)):
        t = p_ref[:, i * NSA_KV_WIDTH:(i + 1) * NSA_KV_WIDTH]
        if i >= 2:
            t = _rope(t, cos_ref[...], sin_ref[...])
        for g in range(NSA_GROUPS):
            o_ref[g] = t[:, g * HEAD_DIM:(g + 1) * HEAD_DIM].astype(o_ref.dtype)
    pad_row = lax.broadcasted_iota(jnp.int32, (VT_ROWS - HEAD_DIM, ts), 0)
    ones_row = jnp.where(pad_row == 0, 1.0, 0.0)
    for i, o_ref in ((3, vs_ref), (5, vw_ref)):
        t_t = p_ref[:, i * NSA_KV_WIDTH:(i + 1) * NSA_KV_WIDTH].T
        for g in range(NSA_GROUPS):
            o_ref[g] = jnp.concatenate([t_t[g * HEAD_DIM:(g + 1) * HEAD_DIM], ones_row],
                                       axis=0).astype(o_ref.dtype)


def _kv_layout(p_b, cos2, sin2):
    B, S, _ = p_b.shape
    ts = min(NSA_KV_TILE, S)
    out_spec = pl.BlockSpec((None, NSA_GROUPS, ts, HEAD_DIM), lambda b, s: (b, 0, s, 0))
    vt_spec = pl.BlockSpec((None, NSA_GROUPS, VT_ROWS, ts), lambda b, s: (b, 0, 0, s))
    shp = lambda dt: jax.ShapeDtypeStruct((B, NSA_GROUPS, S, HEAD_DIM), dt)
    vt_shp = jax.ShapeDtypeStruct((B, NSA_GROUPS, VT_ROWS, S), BF16)
    return pl.pallas_call(
        _kv_layout_kernel,
        grid=(B, S // ts),
        in_specs=[pl.BlockSpec((None, ts, 6 * NSA_KV_WIDTH), lambda b, s: (b, s, 0)),
                  pl.BlockSpec((ts, NSA_KV_WIDTH), lambda b, s: (s, 0)),
                  pl.BlockSpec((ts, NSA_KV_WIDTH), lambda b, s: (s, 0))],
        out_specs=[out_spec, out_spec, out_spec, vt_spec, out_spec, vt_spec],
        out_shape=[shp(F32), shp(F32), shp(BF16), vt_shp, shp(BF16), vt_shp],
        compiler_params=pltpu.CompilerParams(
            dimension_semantics=("parallel", "parallel"), vmem_limit_bytes=VMEM_LIMIT),
        name="nsa_kv_layout",
    )(p_b, cos2, sin2)


def _compress_kernel(subk_ref, subv_ref, pek_ref, w1k_ref, w2k_ref, pev_ref, w1v_ref, w2v_ref,
                     cos_ref, sin_ref, kc_ref, vc_ref):
    n_sub = subk_ref.shape[0]
    half = CMP_STRIDE * HEAD_DIM

    def mlp(sub_ref, pe_ref, w1_ref, w2_ref):
        sub = sub_ref[...]
        top = _bdot(sub, w1_ref[:half, :])
        bot = _bdot(sub, w1_ref[half:, :])
        bias = _bdot(jnp.broadcast_to(pe_ref[...], (8, 2 * half)), w1_ref[...])[0:1, :]
        h = top + pltpu.roll(bot, n_sub - 1, axis=0) + bias
        return _bdot(jax.nn.gelu(h), w2_ref[...])

    kc = mlp(subk_ref, pek_ref, w1k_ref, w2k_ref)
    rot = jnp.concatenate([kc[:, HEAD_DIM // 2:], kc[:, :HEAD_DIM // 2]], axis=-1)
    kc_ref[...] = (kc * cos_ref[...] + rot * sin_ref[...]).astype(kc_ref.dtype)
    vc_ref[...] = mlp(subv_ref, pev_ref, w1v_ref, w2v_ref).astype(vc_ref.dtype)


def _compress(subk, subv, pe_k, w1_k, w2_k, pe_v, w1_v, w2_v, cos_c, sin_c):
    B, G, n_sub, width = subk.shape
    sub_spec = pl.BlockSpec((None, None, n_sub, width), lambda b, g: (b, g, 0, 0))
    full = lambda a: pl.BlockSpec(a.shape, lambda b, g: (0,) * a.ndim)
    out_spec = pl.BlockSpec((None, None, n_sub, HEAD_DIM), lambda b, g: (b, g, 0, 0))
    pe_k, pe_v = pe_k.reshape(1, -1), pe_v.reshape(1, -1)
    args = (pe_k, w1_k, w2_k, pe_v, w1_v, w2_v, cos_c, sin_c)
    return pl.pallas_call(
        _compress_kernel,
        grid=(B, G),
        in_specs=[sub_spec, sub_spec] + [full(a) for a in args],
        out_specs=[out_spec, out_spec],
        out_shape=[jax.ShapeDtypeStruct((B, G, n_sub, HEAD_DIM), BF16)] * 2,
        compiler_params=pltpu.CompilerParams(
            dimension_semantics=("parallel", "parallel"), vmem_limit_bytes=VMEM_LIMIT),
        name="nsa_compress",
    )(subk, subv, *args)


MAX_FLOOR = -1e20
MASK_BIG = 2.0 ** 100
LOG2_E = 1.4426950408889634
VT_ROWS = 80


def _nsa_kernel(q_ref, gate_ref, cos_ref, sin_ref, kc_ref, vc_ref, ks_ref, vst_ref, kw_ref, vwt_ref,
                mselt_ref, o_ref, blockbias_ref, *, n_pick):
    QB, HP, D = Q_BLOCK, NSA_HPG, HEAD_DIM
    qb = pl.program_id(2)
    n_cmp = kc_ref.shape[0]
    lanes4 = lambda x: jnp.concatenate([x] * HP, axis=1)

    heads = []
    for n in range(HP):
        qh = q_ref[n * D:(n + 1) * D, :]
        rot = jnp.concatenate([qh[D // 2:], qh[:D // 2]], axis=0)
        heads.append(qh * cos_ref[...] + rot * sin_ref[...])
    q4 = (jnp.concatenate(heads, axis=1) * (D ** -0.5 * LOG2_E)).astype(BF16)
    t_row = qb * QB + lax.broadcasted_iota(jnp.int32, (1, QB), 1)

    def softmax_cols(s_t, bias_t):
        sm = s_t + lanes4(bias_t)
        m = jnp.maximum(jnp.max(sm, axis=0, keepdims=True), MAX_FLOOR)
        return jnp.exp2(sm - m)

    cmp_end = lax.broadcasted_iota(jnp.int32, (n_cmp, 1), 0) * CMP_STRIDE + (CMP_BLOCK - 1)
    e_c = softmax_cols(jnp.dot(kc_ref[...], q4, preferred_element_type=F32),
                       jnp.where(cmp_end <= t_row, 0.0, -MASK_BIG))
    den_c = jnp.sum(e_c, axis=0, keepdims=True)
    p_c = e_c * (1.0 / jnp.where(den_c > 0.0, den_c, 1.0))
    o_c = _bdot_tn(vc_ref[...], p_c)
    p_sum = p_c[:, 0:QB]
    for n in range(1, HP):
        p_sum = p_sum + p_c[:, n * QB:(n + 1) * QB]
    p_hi = p_sum.astype(BF16)
    p_lo = (p_sum - p_hi.astype(F32)).astype(BF16)
    imp_t = (jnp.dot(mselt_ref[...], p_hi, preferred_element_type=F32)
             + jnp.dot(mselt_ref[...], p_lo, preferred_element_type=F32))

    j = lax.broadcasted_iota(jnp.int32, (SEL_LANES, QB), 0)
    cur = t_row // SEL_BLOCK
    valid = j * SEL_BLOCK <= t_row
    forced = (j == 0) | (j == cur) | (j == cur - 1)
    score = jnp.where(valid, imp_t + jnp.where(forced, FORCE_BONUS, 0.0), -1.0)
    for _ in range(n_pick):
        m = jnp.max(score, axis=0, keepdims=True)
        idx = jnp.min(jnp.where(score == m, j, SEL_LANES), axis=0, keepdims=True)
        score = jnp.where(j == idx, -2.0, score)
    blockbias_ref[...] = jnp.where((score == -2.0) & valid, 0.0, -MASK_BIG)

    KT = SEL_KEY_TILE
    blocks_per_tile = KT // SEL_BLOCK
    n_tiles = (qb * QB + QB + KT - 1) // KT

    def sel_step(kt, carry, causal):
        m_i, acc = carry
        start = pl.multiple_of(kt * KT, KT)
        s_t = jnp.dot(ks_ref[pl.ds(start, KT), :], q4, preferred_element_type=F32)
        bias = jnp.concatenate(
            [jnp.broadcast_to(blockbias_ref[pl.ds(kt * blocks_per_tile + jb, 1), :], (SEL_BLOCK, QB))
             for jb in range(blocks_per_tile)], axis=0)
        if causal:
            kpos = start + lax.broadcasted_iota(jnp.int32, (KT, 1), 0)
            bias = jnp.where(kpos <= t_row, bias, -MASK_BIG)
        sm = s_t + lanes4(bias)
        m_new = jnp.maximum(m_i, jnp.max(sm, axis=0, keepdims=True))
        e = jnp.exp2(sm - m_new).astype(BF16)
        acc_new = jnp.exp2(m_i - m_new) * acc + jnp.dot(vst_ref[:, pl.ds(start, KT)], e,
                                                        preferred_element_type=F32)
        return m_new, acc_new

    init = (jnp.full((1, HP * QB), MAX_FLOOR, F32), jnp.zeros((VT_ROWS, HP * QB), F32))
    carry = lax.fori_loop(0, n_tiles - 1, lambda kt, c: sel_step(kt, c, False), init)
    _, acc_s = sel_step(n_tiles - 1, carry, True)
    den_s = acc_s[D:D + 1]
    o_s = acc_s[:D] * (1.0 / jnp.where(den_s > 0.0, den_s, 1.0))

    span = WINDOW + QB
    w_start = pl.multiple_of(jnp.maximum(qb * QB - WINDOW, 0), QB)
    dist = t_row - (w_start + lax.broadcasted_iota(jnp.int32, (span, 1), 0))
    e_w = softmax_cols(jnp.dot(kw_ref[pl.ds(w_start, span), :], q4, preferred_element_type=F32),
                       jnp.where((dist >= 0) & (dist < WINDOW), 0.0, -MASK_BIG))
    acc_w = jnp.dot(vwt_ref[:, pl.ds(w_start, span)], e_w.astype(BF16), preferred_element_type=F32)
    den_w = acc_w[D:D + 1]
    o_w = acc_w[:D] * (1.0 / jnp.where(den_w > 0.0, den_w, 1.0))

    gates = _sigmoid(gate_ref[...])
    gate_row = lambda br: jnp.concatenate([gates[3 * n + br:3 * n + br + 1, :] for n in range(HP)], axis=1)
    o_t = gate_row(0) * o_c + gate_row(1) * o_s + gate_row(2) * o_w
    for n in range(HP):
        o_ref[:, n * D:(n + 1) * D] = o_t[:, n * QB:(n + 1) * QB].T


def _cmp_to_sel_matrix(n_cmp_rows, n_sel):
    ratio = SEL_BLOCK // CMP_STRIDE
    ci = np.arange(n_cmp_rows)[:, None]
    sj = np.arange(SEL_LANES)[None, :]
    m = sum(((ci + n) // ratio == sj).astype(np.float32) for n in range(CMP_BLOCK // CMP_STRIDE))
    m = m * (sj < n_sel) * (ci < n_cmp_rows - 1)
    return jnp.asarray(m.T, BF16)


def _nsa_attention(qg_t, kc, vc, ks, vst, kw, vwt, cos_t, sin_t):
    B, _, S = qg_t.shape
    n_sub = kc.shape[2]
    n_sel = S // SEL_BLOCK
    gw = NSA_HPG * HEAD_DIM
    gate_row0 = NSA_WIDTH // 128
    msel_t = _cmp_to_sel_matrix(n_sub, n_sel)
    at_bg = lambda shape: pl.BlockSpec((None, None) + shape, lambda b, g, i: (b, g, 0, 0))
    const = lambda a: pl.BlockSpec(a.shape, lambda b, g, i: (0, 0))
    return pl.pallas_call(
        functools.partial(_nsa_kernel, n_pick=min(N_SELECT, n_sel)),
        grid=(B, NSA_GROUPS, S // Q_BLOCK),
        in_specs=[pl.BlockSpec((None, gw, Q_BLOCK), lambda b, g, i: (b, g, i)),
                  pl.BlockSpec((None, 128, Q_BLOCK), lambda b, g, i: (b, gate_row0 + g, i)),
                  pl.BlockSpec((HEAD_DIM, Q_BLOCK), lambda b, g, i: (0, i)),
                  pl.BlockSpec((HEAD_DIM, Q_BLOCK), lambda b, g, i: (0, i)),
                  at_bg((n_sub, HEAD_DIM)), at_bg((n_sub, HEAD_DIM)),
                  at_bg((S, HEAD_DIM)), at_bg((VT_ROWS, S)), at_bg((S, HEAD_DIM)), at_bg((VT_ROWS, S)),
                  const(msel_t)],
        out_specs=pl.BlockSpec((None, Q_BLOCK, gw), lambda b, g, i: (b, i, g)),
        out_shape=jax.ShapeDtypeStruct((B, S, NSA_WIDTH), F32),
        scratch_shapes=[pltpu.VMEM((SEL_LANES, Q_BLOCK), F32)],
        compiler_params=pltpu.CompilerParams(
            dimension_semantics=("parallel", "parallel", "arbitrary"), vmem_limit_bytes=VMEM_LIMIT),
        name="nsa_attention",
    )(qg_t, qg_t, cos_t, sin_t, kc, vc, ks, vst, kw, vwt, msel_t)


def _nsa_branch(p_kv, qg_t, cmp_pe_k, cmp_w1_k, cmp_w2_k, cmp_pe_v, cmp_w1_v, cmp_w2_v):
    B, S, _ = p_kv.shape
    pos = jnp.arange(S)
    cos2, sin2 = _rope_tables(pos, NSA_GROUPS)
    kc_raw, vc_raw, ks, vst, kw, vwt = _kv_layout(p_kv, cos2, sin2)
    n_sub = S // CMP_STRIDE
    sub = lambda t: t.reshape(B, NSA_GROUPS, n_sub, CMP_STRIDE * HEAD_DIM)
    cos_c, sin_c = _rope_tables(jnp.arange(n_sub) * CMP_STRIDE + CMP_BLOCK - 1, 1)
    kc, vc = _compress(sub(kc_raw), sub(vc_raw), cmp_pe_k, cmp_w1_k, cmp_w2_k,
                       cmp_pe_v, cmp_w1_v, cmp_w2_v, cos_c, sin_c)
    cos_q, sin_q = _rope_tables(pos, 1)
    return _nsa_attention(qg_t, kc, vc, ks, vst, kw, vwt, cos_q.T, sin_q.T)


def _nsa_weight_columns(w_nsa):
    K = w_nsa.shape[0]
    q = w_nsa[:, :NSA_WIDTH]
    kv = w_nsa[:, NSA_WIDTH:NSA_WIDTH + 6 * NSA_KV_WIDTH]
    gates = w_nsa[:, NSA_WIDTH + 6 * NSA_KV_WIDTH:]
    per_group = NSA_HPG * 3
    gate_blocks = [jnp.pad(gates[:, g * per_group:(g + 1) * per_group], ((0, 0), (0, 128 - per_group)))
                   for g in range(NSA_GROUPS)]
    return jnp.concatenate([kv, q] + gate_blocks, axis=1)


def _layer_norm(h, g, b):
    mu = jnp.mean(h, axis=-1, keepdims=True)
    var = jnp.mean(jnp.square(h - mu), axis=-1, keepdims=True)
    return (h - mu) * lax.rsqrt(var + LN_EPS) * g + b


def _pack_bf16_halves(x):
    n = x.shape[-1] // 2
    bits = lax.bitcast_convert_type(x.astype(BF16).astype(F32), jnp.uint32)
    return (bits[:, n:] & jnp.uint32(0xFFFF0000)) | (bits[:, :n] >> 16)


def _unpack_bf16_halves(u):
    left = lax.bitcast_convert_type(u << 16, F32)
    right = lax.bitcast_convert_type(u & jnp.uint32(0xFFFF0000), F32)
    return left, right


def _mixer_out_kernel(x_ref, ya_ref, yb_ref, pg_ref, wa_ref, wb_ref, wo_ref, g_ref, b_ref, o_ref, op_ref,
                      *, alpha):
    d = x_ref.shape[-1]
    gate_a = _sigmoid(pg_ref[:, :d])
    gate_b = _sigmoid(pg_ref[:, d:])
    mixed = gate_a * _bdot(ya_ref[...], wa_ref[...]) + gate_b * _bdot(yb_ref[...], wb_ref[...])
    h = alpha * x_ref[...] + _bdot(mixed, wo_ref[...])
    out = _layer_norm(h, g_ref[...], b_ref[...])
    o_ref[...] = out
    op_ref[...] = _pack_bf16_halves(out)


def _mixer_out(xf, ya, yb, p_g, wa, wb, wo, ln_g, ln_b, alpha, tm=512):
    T, D = xf.shape
    rows = lambda w: pl.BlockSpec((tm, w), lambda i: (i, 0))
    full = lambda a: pl.BlockSpec(a.shape, lambda i: (0,) * a.ndim)
    ln_g, ln_b = ln_g.reshape(1, D), ln_b.reshape(1, D)
    return pl.pallas_call(
        functools.partial(_mixer_out_kernel, alpha=alpha),
        grid=(T // tm,),
        in_specs=[rows(D), rows(ya.shape[1]), rows(yb.shape[1]), rows(2 * D),
                  full(wa), full(wb), full(wo), full(ln_g), full(ln_b)],
        out_specs=[rows(D), rows(D // 2)],
        out_shape=[jax.ShapeDtypeStruct((T, D), F32), jax.ShapeDtypeStruct((T, D // 2), jnp.uint32)],
        compiler_params=pltpu.CompilerParams(
            dimension_semantics=("parallel",), vmem_limit_bytes=VMEM_LIMIT),
        name="mixer_out_ln",
    )(xf, ya, yb, p_g, wa, wb, wo, ln_g, ln_b)


ROUTER_TILE = 256
EXPERT_ROWS = 256
SC_TOKEN_CHUNK = 64
SC_ROW_CHUNK = 64
PICK_LANES = 128
LOWEST = -3.0e38


def _router_kernel(x_ref, rw_ref, bias_ref, eidx_ref, wts_ref, pos_ref, cnt_ref, carry_ref):
    tm, E = x_ref.shape[0], rw_ref.shape[1]
    per_group = E // N_GROUPS

    @pl.when(pl.program_id(0) == 0)
    def _():
        carry_ref[...] = jnp.zeros_like(carry_ref)

    scores = _sigmoid(_hdot(x_ref[...], rw_ref[...]))
    choice = scores + bias_ref[...]
    lane = lax.broadcasted_iota(jnp.int32, (tm, E), 1)
    grp = lane // per_group

    def first_max(vals):
        m = jnp.max(vals, axis=-1, keepdims=True)
        return m, jnp.min(jnp.where(vals == m, lane, E), axis=-1, keepdims=True)

    group_score = []
    for g in range(N_GROUPS):
        cg = jnp.where(grp == g, choice, LOWEST)
        m1, i1 = first_max(cg)
        m2 = jnp.max(jnp.where(lane == i1, LOWEST, cg), axis=-1, keepdims=True)
        group_score.append(m1 + m2)
    allowed = jnp.zeros((tm, E), jnp.bool_)
    for g in range(N_GROUPS):
        rank = jnp.zeros((tm, 1), jnp.int32)
        for o in range(N_GROUPS):
            if o != g:
                ahead = (group_score[o] > group_score[g]) if o > g else (group_score[o] >= group_score[g])
                rank = rank + ahead.astype(jnp.int32)
        allowed = allowed | ((grp == g) & (rank < TOPK_GROUPS))

    cur = jnp.where(allowed, choice, NEG_INF)
    picks = []
    for _ in range(TOP_K):
        idx = jnp.argmax(cur, axis=-1, keepdims=True).astype(jnp.int32)
        hit = lane == idx
        picks.append(idx)
        cur = jnp.where(hit, LOWEST, cur)
    sel = jnp.where(cur == LOWEST, 1.0, 0.0)
    gate = scores * sel
    gate = gate / jnp.sum(gate, axis=-1, keepdims=True) * ROUTED_SCALE

    ri = lax.broadcasted_iota(jnp.int32, (tm, tm), 0)
    ci = lax.broadcasted_iota(jnp.int32, (tm, tm), 1)
    before = jnp.dot((ri > ci).astype(BF16), sel.astype(BF16), preferred_element_type=F32)
    queue_pos = before + carry_ref[...]
    carry_ref[...] = carry_ref[...] + jnp.sum(sel, axis=0, keepdims=True)
    cnt_ref[...] = jnp.broadcast_to(carry_ref[...], cnt_ref.shape)

    out_lane = lax.broadcasted_iota(jnp.int32, (tm, PICK_LANES), 1)
    eidx = jnp.zeros((tm, PICK_LANES), jnp.int32)
    wts = jnp.zeros((tm, PICK_LANES), F32)
    pos = jnp.zeros((tm, PICK_LANES), F32)
    for kk, idx in enumerate(picks):
        hit = lane == idx
        eidx = jnp.where(out_lane == kk, idx, eidx)
        wts = jnp.where(out_lane == kk, jnp.sum(jnp.where(hit, gate, 0.0), axis=-1, keepdims=True), wts)
        pos = jnp.where(out_lane == kk, jnp.sum(jnp.where(hit, queue_pos, 0.0), axis=-1, keepdims=True), pos)
    eidx_ref[...] = eidx
    wts_ref[...] = wts
    pos_ref[...] = pos.astype(jnp.int32)


def _router(xf, router_w, router_bias):
    T, D = xf.shape
    E = router_w.shape[1]
    tm = ROUTER_TILE
    picks = lambda dt: jax.ShapeDtypeStruct((T, PICK_LANES), dt)
    pick_spec = pl.BlockSpec((tm, PICK_LANES), lambda i: (i, 0))
    return pl.pallas_call(
        _router_kernel,
        grid=(T // tm,),
        in_specs=[pl.BlockSpec((tm, D), lambda i: (i, 0)),
                  pl.BlockSpec((D, E), lambda i: (0, 0)),
                  pl.BlockSpec((1, E), lambda i: (0, 0))],
        out_specs=[pick_spec, pick_spec, pick_spec, pl.BlockSpec((8, E), lambda i: (0, 0))],
        out_shape=[picks(jnp.int32), picks(F32), picks(jnp.int32), jax.ShapeDtypeStruct((8, E), F32)],
        scratch_shapes=[pltpu.VMEM((1, E), F32)],
        compiler_params=pltpu.CompilerParams(
            dimension_semantics=("arbitrary",), vmem_limit_bytes=VMEM_LIMIT),
        name="moe_router",
    )(xf, router_w, router_bias.reshape(1, E))


def _dest_kernel(eidx_ref, pos_ref, start_ref, dest_ref):
    tm = eidx_ref.shape[0]
    E = start_ref.shape[1]
    lane = lax.broadcasted_iota(jnp.int32, (tm, E), 1)
    out_lane = lax.broadcasted_iota(jnp.int32, (tm, PICK_LANES), 1)
    eidx = eidx_ref[...]
    base = jnp.zeros((tm, PICK_LANES), jnp.int32)
    for kk in range(TOP_K):
        hit = lane == eidx[:, kk:kk + 1]
        start = jnp.sum(jnp.where(hit, start_ref[...], 0), axis=-1, keepdims=True)
        base = jnp.where(out_lane == kk, start, base)
    dest_ref[...] = base + pos_ref[...]


def _dest_rows(eidx, pos, pad_start):
    T = eidx.shape[0]
    E = pad_start.shape[0]
    tm = ROUTER_TILE
    pick_spec = pl.BlockSpec((tm, PICK_LANES), lambda i: (i, 0))
    return pl.pallas_call(
        _dest_kernel,
        grid=(T // tm,),
        in_specs=[pick_spec, pick_spec, pl.BlockSpec((1, E), lambda i: (0, 0))],
        out_specs=pick_spec,
        out_shape=jax.ShapeDtypeStruct((T, PICK_LANES), jnp.int32),
        compiler_params=pltpu.CompilerParams(
            dimension_semantics=("parallel",), vmem_limit_bytes=VMEM_LIMIT),
        name="moe_dest_rows",
    )(eidx, pos, pad_start.reshape(1, E))


def _sc_mesh():
    return plsc.VectorSubcoreMesh(core_axis_name="c", subcore_axis_name="s")


def _sc_scatter_rows(x, dest_t, n_rows):
    T, D = x.shape
    K = dest_t.shape[0]
    mesh = _sc_mesh()
    nc, nw = mesh.num_cores, mesh.num_cores * mesh.num_subcores
    per_w = T // nw
    chunk = min(SC_TOKEN_CHUNK, per_w)
    n_chunks = per_w // chunk
    idx = dest_t.reshape(K, nw, n_chunks, chunk).transpose(1, 2, 0, 3).reshape(nw, n_chunks * K, chunk)

    assert n_chunks % 2 == 0

    @functools.partial(
        pl.kernel, mesh=mesh,
        out_type=jax.ShapeDtypeStruct((n_rows, D), x.dtype),
        scratch_types=[pltpu.VMEM((n_chunks * K, chunk), jnp.int32),
                       pltpu.VMEM((2, chunk, D), x.dtype),
                       pltpu.SemaphoreType.DMA((2,)), pltpu.SemaphoreType.DMA((2,))],
    )
    def scatter(x_hbm, idx_hbm, out_hbm, idx_v, rows_v, load_sem, send_sem):
        wid = lax.axis_index("s") * nc + lax.axis_index("c")
        pltpu.sync_copy(idx_hbm.at[wid], idx_v)

        def load(j, b):
            return pltpu.make_async_copy(x_hbm.at[pl.ds(wid * per_w + j * chunk, chunk)], rows_v.at[b],
                                         load_sem.at[b])

        def sends(j, b):
            return [pltpu.make_async_copy(rows_v.at[b], out_hbm.at[idx_v.at[j * K + kk]], send_sem.at[b])
                    for kk in range(K)]

        load(0, 0).start()

        @pl.loop(0, n_chunks, step=2)
        def _(j0):
            for b in range(2):
                j = j0 + b
                load(j, b).wait()

                @pl.when(j >= 1)
                def _():
                    for c in sends(j - 1, 1 - b):
                        c.wait()

                @pl.when(j + 1 < n_chunks)
                def _():
                    load(j + 1, 1 - b).start()

                for c in sends(j, b):
                    c.start()

        for c in sends(n_chunks - 1, (n_chunks - 1) % 2):
            c.wait()

    return scatter(x, idx)


def _sc_gather_rows(src, idx):
    M = idx.shape[0]
    D = src.shape[1]
    mesh = _sc_mesh()
    nc, nw = mesh.num_cores, mesh.num_cores * mesh.num_subcores
    per_w = M // nw
    chunk = min(SC_ROW_CHUNK, per_w)
    n_chunks = per_w // chunk

    assert n_chunks % 2 == 0

    @functools.partial(
        pl.kernel, mesh=mesh,
        out_type=jax.ShapeDtypeStruct((M, D), src.dtype),
        scratch_types=[pltpu.VMEM((n_chunks, chunk), jnp.int32),
                       pltpu.VMEM((2, chunk, D), src.dtype),
                       pltpu.SemaphoreType.DMA((2,)), pltpu.SemaphoreType.DMA((2,))],
    )
    def gather(src_hbm, idx_hbm, out_hbm, idx_v, rows_v, fetch_sem, store_sem):
        wid = lax.axis_index("s") * nc + lax.axis_index("c")
        pltpu.sync_copy(idx_hbm.at[wid], idx_v)

        def fetch(j, b):
            return pltpu.make_async_copy(src_hbm.at[idx_v.at[j]], rows_v.at[b], fetch_sem.at[b])

        def store(j, b):
            return pltpu.make_async_copy(rows_v.at[b], out_hbm.at[pl.ds(wid * per_w + j * chunk, chunk)],
                                         store_sem.at[b])

        fetch(0, 0).start()

        @pl.loop(0, n_chunks, step=2)
        def _(j0):
            for b in range(2):
                j = j0 + b
                fetch(j, b).wait()

                @pl.when(j >= 1)
                def _():
                    store(j - 1, 1 - b).wait()

                @pl.when(j + 1 < n_chunks)
                def _():
                    fetch(j + 1, 1 - b).start()

                store(j, b).start()

        store(n_chunks - 1, (n_chunks - 1) % 2).wait()

    return gather(src, idx.reshape(nw, n_chunks, chunk))


def _expert_kernel(distinct_e_ref, blk_ord_ref, blk_new_ref, blk_rows_ref, n_used_ref, n_distinct_ref,
                   x_ref, wgu_hbm, wd_hbm, o_ref, wgu_buf, wd_buf, wgu_bf, wd_bf, sem):
    i = pl.program_id(0)
    live = i < n_used_ref[0]
    ordinal = blk_ord_ref[i]
    slot = ordinal % 2

    def weight_copies(k, s):
        e = distinct_e_ref[k]
        return (pltpu.make_async_copy(wgu_hbm.at[e], wgu_buf.at[s], sem.at[0, s]),
                pltpu.make_async_copy(wd_hbm.at[e], wd_buf.at[s], sem.at[1, s]))

    @pl.when(i == 0)
    def _():
        for c in weight_copies(0, 0):
            c.start()

    @pl.when(live & (blk_new_ref[i] == 1))
    def _():
        for c in weight_copies(ordinal, slot):
            c.wait()

        @pl.when(ordinal + 1 < n_distinct_ref[0])
        def _():
            for c in weight_copies(ordinal + 1, 1 - slot):
                c.start()

        wgu_bf[...] = wgu_buf[slot].astype(BF16)
        wd_bf[...] = wd_buf[slot].astype(BF16)

    @pl.when(live)
    def _():
        hidden = wd_bf.shape[0]
        half = x_ref.shape[1]
        row = lax.broadcasted_iota(jnp.int32, x_ref.shape, 0)
        left, right = _unpack_bf16_halves(x_ref[...])
        real = row < blk_rows_ref[i]
        left = jnp.where(real, left, 0.0).astype(BF16)
        right = jnp.where(real, right, 0.0).astype(BF16)
        h = (jnp.dot(left, wgu_bf[:half, :], preferred_element_type=F32)
             + jnp.dot(right, wgu_bf[half:, :], preferred_element_type=F32))
        gate, up = h[:, :hidden], h[:, hidden:]
        act = (gate * _sigmoid(gate) * up).astype(BF16)
        o_ref[...] = _pack_bf16_halves(jnp.dot(act, wd_bf[...], preferred_element_type=F32))

    @pl.when(jnp.logical_not(live))
    def _():
        o_ref[...] = jnp.zeros_like(o_ref)


def _expert_ffn(xs, blk_e, blk_rows, n_used, w_gu, w_down):
    n_rows, half = xs.shape
    E, D, two_h = w_gu.shape
    n_blocks = n_rows // EXPERT_ROWS
    idx = jnp.arange(n_blocks, dtype=jnp.int32)
    is_live = idx < n_used[0]
    blk_new = (is_live & ((idx == 0) | (blk_e != jnp.roll(blk_e, 1)))).astype(jnp.int32)
    blk_ord = (jnp.cumsum(blk_new) - 1).astype(jnp.int32)
    n_distinct = blk_ord[-1:] + 1
    distinct_e = jnp.zeros((n_blocks,), jnp.int32).at[blk_ord].max(blk_e * is_live)

    live = lambda i, nu: jnp.minimum(i, nu[0] - 1)
    grid_spec = pltpu.PrefetchScalarGridSpec(
        num_scalar_prefetch=6,
        grid=(n_blocks,),
        in_specs=[pl.BlockSpec((EXPERT_ROWS, half), lambda i, de, bo, bn, br, nu, nd: (live(i, nu), 0)),
                  pl.BlockSpec(memory_space=pl.ANY), pl.BlockSpec(memory_space=pl.ANY)],
        out_specs=pl.BlockSpec((EXPERT_ROWS, half), lambda i, de, bo, bn, br, nu, nd: (i, 0)),
        scratch_shapes=[pltpu.VMEM((2, D, two_h), F32), pltpu.VMEM((2, two_h // 2, D), F32),
                        pltpu.VMEM((D, two_h), BF16), pltpu.VMEM((two_h // 2, D), BF16),
                        pltpu.SemaphoreType.DMA((2, 2))],
    )
    return pl.pallas_call(
        _expert_kernel,
        grid_spec=grid_spec,
        out_shape=jax.ShapeDtypeStruct((n_rows, half), jnp.uint32),
        compiler_params=pltpu.CompilerParams(
            dimension_semantics=("arbitrary",), vmem_limit_bytes=VMEM_LIMIT),
        name="moe_experts",
    )(distinct_e, blk_ord, blk_new, blk_rows, n_used, n_distinct, xs, w_gu, w_down)


def _moe_out_kernel(x_ref, yk_ref, wts_ref, sgu_ref, sd_ref, g_ref, b_ref, o_ref, *, alpha):
    x = x_ref[...]
    hidden = sd_ref.shape[0]
    h = _bdot(x, sgu_ref[...])
    gate, up = h[:, :hidden], h[:, hidden:]
    ffn = _bdot(gate * _sigmoid(gate) * up, sd_ref[...])
    wts = wts_ref[...]
    routed_left = routed_right = None
    for kk in range(TOP_K):
        left, right = _unpack_bf16_halves(yk_ref[kk])
        w = wts[:, kk:kk + 1]
        routed_left = w * left if kk == 0 else routed_left + w * left
        routed_right = w * right if kk == 0 else routed_right + w * right
    ffn = ffn + jnp.concatenate([routed_left, routed_right], axis=-1)
    o_ref[...] = _layer_norm(alpha * x + ffn, g_ref[...], b_ref[...])


def _moe_out(xf, yk, wts, sw_gu, sw_down, ln_g, ln_b, alpha, tm=128):
    T, D = xf.shape
    rows = lambda w: pl.BlockSpec((tm, w), lambda i: (i, 0))
    full = lambda a: pl.BlockSpec(a.shape, lambda i: (0,) * a.ndim)
    ln_g, ln_b = ln_g.reshape(1, D), ln_b.reshape(1, D)
    return pl.pallas_call(
        functools.partial(_moe_out_kernel, alpha=alpha),
        grid=(T // tm,),
        in_specs=[rows(D), pl.BlockSpec((TOP_K, tm, D // 2), lambda i: (0, i, 0)), rows(PICK_LANES),
                  full(sw_gu), full(sw_down), full(ln_g), full(ln_b)],
        out_specs=rows(D),
        out_shape=jax.ShapeDtypeStruct((T, D), F32),
        compiler_params=pltpu.CompilerParams(
            dimension_semantics=("parallel",), vmem_limit_bytes=VMEM_LIMIT),
        name="moe_combine_ln",
    )(xf, yk, wts, sw_gu, sw_down, ln_g, ln_b)


def _moe_ffn_ln(xf, xp, router_w, router_bias, w_gu, w_down, sw_gu, sw_down, ln_g, ln_b, alpha):
    T, D = xf.shape
    E = router_w.shape[1]
    BM = EXPERT_ROWS
    eidx, wts, pos, cnt = _router(xf, router_w, router_bias)
    counts = cnt[0].astype(jnp.int32)
    padded = (counts + BM - 1) // BM * BM
    pad_end = jnp.cumsum(padded)
    pad_start = pad_end - padded
    n_rows = T * TOP_K + E * BM
    n_blocks = n_rows // BM
    blk_row0 = jnp.arange(n_blocks, dtype=jnp.int32) * BM
    blk_e = jnp.minimum(jnp.sum((pad_end[None, :] <= blk_row0[:, None]).astype(jnp.int32), axis=1), E - 1)
    blk_rows = jnp.clip(pad_start[blk_e] + counts[blk_e] - blk_row0, 0, BM).astype(jnp.int32)
    n_used = (pad_end[-1:] // BM).astype(jnp.int32)
    dest = _dest_rows(eidx, pos, pad_start)[:, :TOP_K]
    dest_t = dest.T
    xs = _sc_scatter_rows(xp, dest_t, n_rows)
    ys = _expert_ffn(xs, blk_e, blk_rows, n_used, w_gu, w_down)
    yk = _sc_gather_rows(ys, dest_t.reshape(-1)).reshape(TOP_K, T, D // 2)
    return _moe_out(xf, yk, wts, sw_gu, sw_down, ln_g, ln_b, alpha)


def kernel(x, w_in, tshift_mu, rwkv_w0, rwkv_w2, rwkv_a0, rwkv_a2, rwkv_g2, rwkv_k_k, rwkv_k_a, rwkv_r_k, rwkv_lnx_w, rwkv_lnx_b, cmp_pe_k, cmp_w1_k, cmp_w2_k, cmp_pe_v, cmp_w1_v, cmp_w2_v, w_branch_a, w_branch_b, w_out, ln1_g, ln1_b, router_w, router_bias, exp_w_gu, exp_w_down, shared_w_gu, shared_w_down, ln2_g, ln2_b):
    B, S, D = x.shape
    depth = w_in.shape[0]
    alpha = (2 * depth) ** 0.25
    nsa_w = w_in.shape[2] - RWKV_IN_W - 2 * D
    for l in range(depth):
        xf = x.reshape(B * S, D)
        w_l = w_in[l]
        w_a = w_l[:, :RWKV_IN_W].astype(BF16)
        w_b = _nsa_weight_columns(w_l[:, RWKV_IN_W:RWKV_IN_W + nsa_w]).astype(BF16)
        w_g = w_l[:, RWKV_IN_W + nsa_w:].astype(BF16)
        kv_w = 6 * NSA_KV_WIDTH
        p_a = _matmul(xf, w_a, PROJ_ROWS, w_a.shape[1]).reshape(B, S, -1)
        p_kv = _matmul(xf, w_b[:, :kv_w], PROJ_ROWS, kv_w).reshape(B, S, -1)
        qg_t = _matmul_t(x, w_b[:, kv_w:].T, PROJ_ROWS)
        p_g = _matmul(xf, w_g, PROJ_ROWS, w_g.shape[1])
        y_a = _rwkv_time_mix(p_a, tshift_mu[l], rwkv_w0[l], rwkv_w2[l], rwkv_a0[l], rwkv_a2[l], rwkv_g2[l],
                             rwkv_k_k[l], rwkv_k_a[l], rwkv_r_k[l].reshape(-1), rwkv_lnx_w[l], rwkv_lnx_b[l])
        y_b = _nsa_branch(p_kv, qg_t, cmp_pe_k[l], cmp_w1_k[l], cmp_w2_k[l],
                          cmp_pe_v[l], cmp_w1_v[l], cmp_w2_v[l])
        x1, x1p = _mixer_out(xf, y_a.reshape(B * S, -1), y_b.reshape(B * S, -1), p_g,
                             w_branch_a[l].astype(BF16), w_branch_b[l].astype(BF16), w_out[l].astype(BF16),
                             ln1_g[l], ln1_b[l], alpha)
        x2 = _moe_ffn_ln(x1, x1p, router_w[l], router_bias[l], exp_w_gu[l], exp_w_down[l],
                         shared_w_gu[l].astype(BF16), shared_w_down[l].astype(BF16), ln2_g[l], ln2_b[l], alpha)
        x = x2.reshape(B, S, D)
    return x
```

```python
import functools

import numpy as np
import jax
import jax.numpy as jnp
from jax import lax
from jax.experimental import pallas as pl
from jax.experimental.pallas import tpu as pltpu
from jax.experimental.pallas import tpu_sc as plsc

F32 = jnp.float32
BF16 = jnp.bfloat16
HIGHEST = lax.Precision.HIGHEST

RWKV_HEADS = 8
HEAD_DIM = 64
RWKV_WIDTH = RWKV_HEADS * HEAD_DIM
W_LORA = 64
A_LORA = 64
G_LORA = 128
GN_EPS = 64e-5
NSA_HEADS = 8
NSA_GROUPS = 2
NSA_HPG = NSA_HEADS // NSA_GROUPS
NSA_WIDTH = NSA_HEADS * HEAD_DIM
NSA_KV_WIDTH = NSA_GROUPS * HEAD_DIM
CMP_BLOCK = 32
CMP_STRIDE = 16
CMP_HIDDEN = 256
SEL_BLOCK = 64
N_SELECT = 16
WINDOW = 512
Q_BLOCK = 128
ROPE_THETA = 10000.0
RWKV_IN_W = 3 * RWKV_WIDTH + W_LORA + A_LORA + G_LORA
N_EXPERTS = 256
TOP_K = 8
N_GROUPS = 8
TOPK_GROUPS = 4
EXPERT_DIM = 256
ROUTED_SCALE = 2.5
LN_EPS = 1e-5
NEG_INF = -1e30
FORCE_BONUS = 1e4

RWKV_CHUNK = 64
RWKV_HEAD_GROUP = 4
RWKV_STEP_CHUNKS = 2
VMEM_LIMIT = 56 * 1024 * 1024
PROJ_ROWS = 512


def _bdot(a, b):
    return jnp.dot(a.astype(BF16), b.astype(BF16), preferred_element_type=F32)


def _bdot_nt(a, b):
    return lax.dot_general(a.astype(BF16), b.astype(BF16), (((1,), (1,)), ((), ())),
                           preferred_element_type=F32)


def _bdot_tn(a, b):
    return lax.dot_general(a.astype(BF16), b.astype(BF16), (((0,), (0,)), ((), ())),
                           preferred_element_type=F32)


def _hdot(a, b):
    return jnp.dot(a, b, precision=HIGHEST, preferred_element_type=F32)


def _sigmoid(x):
    return 1.0 / (1.0 + jnp.exp(-x))


def _matmul_kernel(x_ref, w_ref, o_ref):
    o_ref[...] = jnp.dot(x_ref[...].astype(BF16), w_ref[...], preferred_element_type=F32)


def _matmul(x, w, tm, tn):
    M, K = x.shape
    N = w.shape[1]
    return pl.pallas_call(
        _matmul_kernel,
        grid=(M // tm, N // tn),
        in_specs=[pl.BlockSpec((tm, K), lambda i, j: (i, 0)),
                  pl.BlockSpec((K, tn), lambda i, j: (0, j))],
        out_specs=pl.BlockSpec((tm, tn), lambda i, j: (i, j)),
        out_shape=jax.ShapeDtypeStruct((M, N), F32),
        compiler_params=pltpu.CompilerParams(
            dimension_semantics=("parallel", "parallel"), vmem_limit_bytes=VMEM_LIMIT),
        name="dense_proj",
    )(x, w)


def _matmul_t_kernel(x_ref, wt_ref, o_ref):
    o_ref[...] = lax.dot_general(wt_ref[...], x_ref[...].astype(BF16), (((1,), (1,)), ((), ())),
                                 preferred_element_type=F32)


def _matmul_t(x, w_t, tm):
    B, S, K = x.shape
    N = w_t.shape[0]
    return pl.pallas_call(
        _matmul_t_kernel,
        grid=(B, S // tm),
        in_specs=[pl.BlockSpec((None, tm, K), lambda b, s: (b, s, 0)),
                  pl.BlockSpec((N, K), lambda b, s: (0, 0))],
        out_specs=pl.BlockSpec((None, N, tm), lambda b, s: (b, 0, s)),
        out_shape=jax.ShapeDtypeStruct((B, N, S), F32),
        compiler_params=pltpu.CompilerParams(
            dimension_semantics=("parallel", "parallel"), vmem_limit_bytes=VMEM_LIMIT),
        name="dense_proj_t",
    )(x, w_t)


def _rwkv_kernel(p_ref, mu_ref, w0_ref, w2_ref, a0_ref, a2_ref, g2_ref, kk_ref, ka_ref, rk_ref,
                 lnw_ref, lnb_ref, o_ref, carry_ref, state_ref):
    C, H, N = RWKV_CHUNK, RWKV_HEADS, HEAD_DIM
    W = RWKV_WIDTH
    B = p_ref.shape[0]
    NC = p_ref.shape[1] // C
    L = NC * C
    R = B * L

    @pl.when(pl.program_id(0) == 0)
    def _():
        carry_ref[...] = jnp.zeros_like(carry_ref)
        state_ref[...] = jnp.zeros_like(state_ref)

    def per_block(x, rows):
        return jnp.concatenate(
            [jnp.broadcast_to(x[i].reshape(1, -1), (rows, x.shape[-1])) for i in range(x.shape[0])], axis=0)

    p = p_ref[...].reshape(R, p_ref.shape[-1])
    row = lax.broadcasted_iota(jnp.int32, p.shape, 0)
    prev = jnp.where(row % L == 0, per_block(carry_ref[...], L), pltpu.roll(p, 1, axis=0))
    for b in range(B):
        carry_ref[b] = p[b * L + L - 1:b * L + L, :]
    xs = p + (prev - p) * mu_ref[...]
    r = xs[:, 0:W]
    k = xs[:, W:2 * W]
    v = xs[:, 2 * W:3 * W]
    wl = xs[:, 3 * W:3 * W + W_LORA]
    al = xs[:, 3 * W + W_LORA:3 * W + W_LORA + A_LORA]
    gl = xs[:, 3 * W + W_LORA + A_LORA:]

    z = -(w0_ref[...] + _hdot(jnp.tanh(wl), w2_ref[...]))
    softplus = jnp.maximum(z, 0.0) + jnp.log1p(jnp.exp(-jnp.abs(z)))
    logd = -jnp.exp(-softplus - 0.5)
    a = _sigmoid(a0_ref[...] + _hdot(al, a2_ref[...]))
    g = _hdot(_sigmoid(gl), g2_ref[...])

    kk = k * kk_ref[...]
    knew = k * (1.0 + (a - 1.0) * ka_ref[...])

    HG = RWKV_HEAD_GROUP
    GW = HG * N
    same_head_lanes = (lax.broadcasted_iota(jnp.int32, (GW, GW), 0) // N
                       == lax.broadcasted_iota(jnp.int32, (GW, GW), 1) // N)
    head_ones = jnp.where(same_head_lanes, 1.0, 0.0).astype(BF16)

    def head_sum(x):
        hi = x.astype(BF16)
        lo = (x - hi.astype(F32)).astype(BF16)
        return jnp.concatenate(
            [jnp.dot(hi[:, s:s + GW], head_ones, preferred_element_type=F32)
             + jnp.dot(lo[:, s:s + GW], head_ones, preferred_element_type=F32) for s in range(0, W, GW)],
            axis=-1)

    kk = kk / jnp.maximum(jnp.sqrt(head_sum(kk * kk)), 1e-12)
    lr_kk = kk * a

    ti = lax.broadcasted_iota(jnp.int32, (R, R), 0)
    tj = lax.broadcasted_iota(jnp.int32, (R, R), 1)
    same_chunk = (ti >= tj) & (ti // C == tj // C)
    cl = _hdot(same_chunk.astype(F32), logd)
    cl_end = per_block(jnp.concatenate([cl[i * C + C - 1:i * C + C, :] for i in range(B * NC)], axis=0), C)
    a_hat = -kk * jnp.exp(cl - logd)
    r_hat = r * jnp.exp(cl)
    inv_gam = jnp.exp(-cl)
    b_til = lr_kk * inv_gam
    k_til = knew * inv_gam
    to_end = jnp.exp(cl_end - cl)
    b_end = lr_kk * to_end
    k_end = knew * to_end
    gam_end = jnp.exp(cl_end)

    gt = lax.broadcasted_iota(jnp.int32, (C, GW), 0)
    gc = lax.broadcasted_iota(jnp.int32, (C, GW), 1) % N
    strict = gt > gc
    incl = gt >= gc
    eye = (gt == gc).astype(F32)
    bi = lax.broadcasted_iota(jnp.int32, (HG * C, GW), 0) // C
    bj = lax.broadcasted_iota(jnp.int32, (HG * C, GW), 1) // N
    same_head = bi == bj

    def block_diag(y):
        yb = y.astype(BF16)
        return jnp.where(same_head, jnp.concatenate([yb] * HG, axis=0), jnp.zeros((), BF16))

    def bd_dot(x, y):
        return jnp.dot(x.astype(BF16), block_diag(y), preferred_element_type=F32)

    def bd_dot_nt(x, y):
        return lax.dot_general(x.astype(BF16), block_diag(y), (((1,), (1,)), ((), ())),
                               preferred_element_type=F32)

    n_groups = H // HG
    units = [(b, c, gi) for b in range(B) for c in range(NC) for gi in range(n_groups)]
    n_units = range(len(units))
    cut = lambda x, b, c, gi: x[(b * NC + c) * C:(b * NC + c + 1) * C, gi * GW:(gi + 1) * GW]
    v_u = [cut(v, *un) for un in units]
    ar = [jnp.concatenate([cut(a_hat, *un), cut(r_hat, *un)], axis=0) for un in units]
    mb = [bd_dot_nt(ar[i], cut(b_til, *units[i])) for i in n_units]
    mk = [bd_dot_nt(ar[i], cut(k_til, *units[i])) for i in n_units]
    n_ab = [jnp.where(strict, mb[i][:C], 0.0) for i in n_units]
    m_rb = [jnp.where(incl, mb[i][C:], 0.0) for i in n_units]
    l_ak = [jnp.where(strict, mk[i][:C], 0.0) for i in n_units]
    m_rk = [jnp.where(incl, mk[i][C:], 0.0) for i in n_units]

    pw = list(n_ab)
    tinv = [eye + n_ab[i] for i in n_units]
    step = 2
    while step < C:
        pw = [bd_dot(pw[i], pw[i]) for i in n_units]
        tinv = [tinv[i] + bd_dot(tinv[i], pw[i]) for i in n_units]
        step *= 2
    lv = [bd_dot(l_ak[i], v_u[i]) for i in n_units]

    state = {(b, gi): state_ref[b * n_groups + gi] for b in range(B) for gi in range(n_groups)}
    outs = {}
    for c in range(NC):
        live = [i for i in n_units if units[i][1] == c]
        s0 = {i: state[(units[i][0], units[i][2])] for i in live}
        ars = {i: bd_dot_nt(ar[i], s0[i]) for i in live}
        u = {i: bd_dot(tinv[i], ars[i][:C] + lv[i]) for i in live}
        for i in live:
            outs[units[i]] = ars[i][C:] + bd_dot(m_rb[i], u[i]) + bd_dot(m_rk[i], v_u[i])
        for i in live:
            b, _, gi = units[i]
            uv = jnp.concatenate([u[i], v_u[i]], axis=0)
            bk_end = jnp.concatenate([cut(b_end, *units[i]), cut(k_end, *units[i])], axis=0)
            cross = jnp.where(same_head, _bdot_tn(uv, bk_end), 0.0)
            upd = cross[0:N]
            for h in range(1, HG):
                upd = upd + cross[h * N:(h + 1) * N]
            state[(b, gi)] = s0[i] * cut(gam_end, *units[i])[0:1] + upd
    for (b, gi), s_new in state.items():
        state_ref[b * n_groups + gi] = s_new

    o = jnp.concatenate([jnp.concatenate([outs[(b, c, gi)] for gi in range(n_groups)], axis=-1)
                         for b in range(B) for c in range(NC)], axis=0)
    mean = head_sum(o) * (1.0 / N)
    var = head_sum(jnp.square(o - mean)) * (1.0 / N)
    o = (o - mean) * lax.rsqrt(var + GN_EPS) * lnw_ref[...] + lnb_ref[...]
    bonus = head_sum(r * knew * rk_ref[...]) * v
    o_ref[...] = ((o + bonus) * g).reshape(o_ref.shape)


def _hdot_nt(a, b):
    return lax.dot_general(a, b, (((1,), (1,)), ((), ())), precision=HIGHEST,
                           preferred_element_type=F32)


def _hdot_tn(a, b):
    return lax.dot_general(a, b, (((0,), (0,)), ((), ())), precision=HIGHEST,
                           preferred_element_type=F32)


def _rwkv_time_mix(p_a, mu, w0, w2, a0, a2, g2, k_k, k_a, r_k, lnx_w, lnx_b):
    B, S, _ = p_a.shape
    L = RWKV_CHUNK * RWKV_STEP_CHUNKS
    row = lambda t: t.reshape(1, -1)
    full = lambda shape: pl.BlockSpec(shape, lambda s: (0,) * len(shape))
    n_units = B * RWKV_HEADS // RWKV_HEAD_GROUP
    return pl.pallas_call(
        _rwkv_kernel,
        grid=(S // L,),
        in_specs=[pl.BlockSpec((B, L, RWKV_IN_W), lambda s: (0, s, 0)),
                  full((1, RWKV_IN_W)), full((1, RWKV_WIDTH)), full((W_LORA, RWKV_WIDTH)),
                  full((1, RWKV_WIDTH)), full((A_LORA, RWKV_WIDTH)), full((G_LORA, RWKV_WIDTH)),
                  full((1, RWKV_WIDTH)), full((1, RWKV_WIDTH)), full((1, RWKV_WIDTH)),
                  full((1, RWKV_WIDTH)), full((1, RWKV_WIDTH))],
        out_specs=pl.BlockSpec((B, L, RWKV_WIDTH), lambda s: (0, s, 0)),
        out_shape=jax.ShapeDtypeStruct((B, S, RWKV_WIDTH), F32),
        scratch_shapes=[pltpu.VMEM((B, 1, RWKV_IN_W), F32),
                        pltpu.VMEM((n_units, HEAD_DIM, RWKV_HEAD_GROUP * HEAD_DIM), F32)],
        compiler_params=pltpu.CompilerParams(
            dimension_semantics=("arbitrary",), vmem_limit_bytes=VMEM_LIMIT),
        name="rwkv7_chunked",
    )(p_a, row(mu), row(w0), w2, row(a0), a2, g2, row(k_k), row(k_a), row(r_k), row(lnx_w), row(lnx_b))


NSA_KV_TILE = 1024
SEL_KEY_TILE = 1024
SEL_LANES = 128


def _rope_tables(pos, reps):
    half = HEAD_DIM // 2
    inv = ROPE_THETA ** (-jnp.arange(half, dtype=F32) / half)
    ang = pos.astype(F32)[:, None] * inv
    cos, sin = jnp.cos(ang), jnp.sin(ang)
    cosf = jnp.concatenate([cos, cos], -1)
    sinf = jnp.concatenate([-sin, sin], -1)
    return jnp.tile(cosf, (1, reps)), jnp.tile(sinf, (1, reps))


def _rope(x, cosf, sinf):
    width = x.shape[-1]
    lane = lax.broadcasted_iota(jnp.int32, x.shape, 1)
    first_half = (lane % HEAD_DIM) < HEAD_DIM // 2
    rot = jnp.where(first_half, pltpu.roll(x, width - HEAD_DIM // 2, axis=1),
                    pltpu.roll(x, HEAD_DIM // 2, axis=1))
    return x * cosf + rot * sinf


def _kv_layout_kernel(p_ref, cos_ref, sin_ref, kc_ref, vc_ref, ks_ref, vs_ref, kw_ref, vw_ref):
    ts = p_ref.shape[0]
    for i, o_ref in ((0, kc_ref), (1, vc_ref), (2, ks_ref), (4, kw_ref)):
        t = p_ref[:, i * NSA_KV_WIDTH:(i + 1) * NSA_KV_WIDTH]
        if i >= 2:
            t = _rope(t, cos_ref[...], sin_ref[...])
        for g in range(NSA_GROUPS):
            o_ref[g] = t[:, g * HEAD_DIM:(g + 1) * HEAD_DIM].astype(o_ref.dtype)
    pad_row = lax.broadcasted_iota(jnp.int32, (VT_ROWS - HEAD_DIM, ts), 0)
    ones_row = jnp.where(pad_row == 0, 1.0, 0.0)
    for i, o_ref in ((3, vs_ref), (5, vw_ref)):
        t_t = p_ref[:, i * NSA_KV_WIDTH:(i + 1) * NSA_KV_WIDTH].T
        for g in range(NSA_GROUPS):
            o_ref[g] = jnp.concatenate([t_t[g * HEAD_DIM:(g + 1) * HEAD_DIM], ones_row],
                                       axis=0).astype(o_ref.dtype)


def _kv_layout(p_b, cos2, sin2):
    B, S, _ = p_b.shape
    ts = min(NSA_KV_TILE, S)
    out_spec = pl.BlockSpec((None, NSA_GROUPS, ts, HEAD_DIM), lambda b, s: (b, 0, s, 0))
    vt_spec = pl.BlockSpec((None, NSA_GROUPS, VT_ROWS, ts), lambda b, s: (b, 0, 0, s))
    shp = lambda dt: jax.ShapeDtypeStruct((B, NSA_GROUPS, S, HEAD_DIM), dt)
    vt_shp = jax.ShapeDtypeStruct((B, NSA_GROUPS, VT_ROWS, S), BF16)
    return pl.pallas_call(
        _kv_layout_kernel,
        grid=(B, S // ts),
        in_specs=[pl.BlockSpec((None, ts, 6 * NSA_KV_WIDTH), lambda b, s: (b, s, 0)),
                  pl.BlockSpec((ts, NSA_KV_WIDTH), lambda b, s: (s, 0)),
                  pl.BlockSpec((ts, NSA_KV_WIDTH), lambda b, s: (s, 0))],
        out_specs=[out_spec, out_spec, out_spec, vt_spec, out_spec, vt_spec],
        out_shape=[shp(F32), shp(F32), shp(BF16), vt_shp, shp(BF16), vt_shp],
        compiler_params=pltpu.CompilerParams(
            dimension_semantics=("parallel", "parallel"), vmem_limit_bytes=VMEM_LIMIT),
        name="nsa_kv_layout",
    )(p_b, cos2, sin2)


def _compress_kernel(subk_ref, subv_ref, pek_ref, w1k_ref, w2k_ref, pev_ref, w1v_ref, w2v_ref,
                     cos_ref, sin_ref, kc_ref, vc_ref):
    n_sub = subk_ref.shape[0]
    half = CMP_STRIDE * HEAD_DIM

    def mlp(sub_ref, pe_ref, w1_ref, w2_ref):
        sub = sub_ref[...]
        top = _bdot(sub, w1_ref[:half, :])
        bot = _bdot(sub, w1_ref[half:, :])
        bias = _bdot(jnp.broadcast_to(pe_ref[...], (8, 2 * half)), w1_ref[...])[0:1, :]
        h = top + pltpu.roll(bot, n_sub - 1, axis=0) + bias
        return _bdot(jax.nn.gelu(h), w2_ref[...])

    kc = mlp(subk_ref, pek_ref, w1k_ref, w2k_ref)
    rot = jnp.concatenate([kc[:, HEAD_DIM // 2:], kc[:, :HEAD_DIM // 2]], axis=-1)
    kc_ref[...] = (kc * cos_ref[...] + rot * sin_ref[...]).astype(kc_ref.dtype)
    vc_ref[...] = mlp(subv_ref, pev_ref, w1v_ref, w2v_ref).astype(vc_ref.dtype)


def _compress(subk, subv, pe_k, w1_k, w2_k, pe_v, w1_v, w2_v, cos_c, sin_c):
    B, G, n_sub, width = subk.shape
    sub_spec = pl.BlockSpec((None, None, n_sub, width), lambda b, g: (b, g, 0, 0))
    full = lambda a: pl.BlockSpec(a.shape, lambda b, g: (0,) * a.ndim)
    out_spec = pl.BlockSpec((None, None, n_sub, HEAD_DIM), lambda b, g: (b, g, 0, 0))
    pe_k, pe_v = pe_k.reshape(1, -1), pe_v.reshape(1, -1)
    args = (pe_k, w1_k, w2_k, pe_v, w1_v, w2_v, cos_c, sin_c)
    return pl.pallas_call(
        _compress_kernel,
        grid=(B, G),
        in_specs=[sub_spec, sub_spec] + [full(a) for a in args],
        out_specs=[out_spec, out_spec],
        out_shape=[jax.ShapeDtypeStruct((B, G, n_sub, HEAD_DIM), BF16)] * 2,
        compiler_params=pltpu.CompilerParams(
            dimension_semantics=("parallel", "parallel"), vmem_limit_bytes=VMEM_LIMIT),
        name="nsa_compress",
    )(subk, subv, *args)


MAX_FLOOR = -1e20
MASK_BIG = 2.0 ** 100
LOG2_E = 1.4426950408889634
VT_ROWS = 80


def _nsa_kernel(q_ref, gate_ref, cos_ref, sin_ref, kc_ref, vc_ref, ks_ref, vst_ref, kw_ref, vwt_ref,
                mselt_ref, o_ref, blockbias_ref, *, n_pick):
    QB, HP, D = Q_BLOCK, NSA_HPG, HEAD_DIM
    qb = pl.program_id(2)
    n_cmp = kc_ref.shape[0]
    lanes4 = lambda x: jnp.concatenate([x] * HP, axis=1)

    heads = []
    for n in range(HP):
        qh = q_ref[n * D:(n + 1) * D, :]
        rot = jnp.concatenate([qh[D // 2:], qh[:D // 2]], axis=0)
        heads.append(qh * cos_ref[...] + rot * sin_ref[...])
    q4 = (jnp.concatenate(heads, axis=1) * (D ** -0.5 * LOG2_E)).astype(BF16)
    t_row = qb * QB + lax.broadcasted_iota(jnp.int32, (1, QB), 1)

    def softmax_cols(s_t, bias_t):
        sm = s_t + lanes4(bias_t)
        m = jnp.maximum(jnp.max(sm, axis=0, keepdims=True), MAX_FLOOR)
        return jnp.exp2(sm - m)

    cmp_end = lax.broadcasted_iota(jnp.int32, (n_cmp, 1), 0) * CMP_STRIDE + (CMP_BLOCK - 1)
    e_c = softmax_cols(jnp.dot(kc_ref[...], q4, preferred_element_type=F32),
                       jnp.where(cmp_end <= t_row, 0.0, -MASK_BIG))
    den_c = jnp.sum(e_c, axis=0, keepdims=True)
    p_c = e_c * (1.0 / jnp.where(den_c > 0.0, den_c, 1.0))
    o_c = _bdot_tn(vc_ref[...], p_c)
    p_sum = p_c[:, 0:QB]
    for n in range(1, HP):
        p_sum = p_sum + p_c[:, n * QB:(n + 1) * QB]
    p_hi = p_sum.astype(BF16)
    p_lo = (p_sum - p_hi.astype(F32)).astype(BF16)
    imp_t = (jnp.dot(mselt_ref[...], p_hi, preferred_element_type=F32)
             + jnp.dot(mselt_ref[...], p_lo, preferred_element_type=F32))

    j = lax.broadcasted_iota(jnp.int32, (SEL_LANES, QB), 0)
    cur = t_row // SEL_BLOCK
    valid = j * SEL_BLOCK <= t_row
    forced = (j == 0) | (j == cur) | (j == cur - 1)
    score = jnp.where(valid, imp_t + jnp.where(forced, FORCE_BONUS, 0.0), -1.0)
    for _ in range(n_pick):
        m = jnp.max(score, axis=0, keepdims=True)
        idx = jnp.min(jnp.where(score == m, j, SEL_LANES), axis=0, keepdims=True)
        score = jnp.where(j == idx, -2.0, score)
    blockbias_ref[...] = jnp.where((score == -2.0) & valid, 0.0, -MASK_BIG)

    KT = SEL_KEY_TILE
    blocks_per_tile = KT // SEL_BLOCK
    n_tiles = (qb * QB + QB + KT - 1) // KT

    def sel_step(kt, carry, causal):
        m_i, acc = carry
        start = pl.multiple_of(kt * KT, KT)
        s_t = jnp.dot(ks_ref[pl.ds(start, KT), :], q4, preferred_element_type=F32)
        bias = jnp.concatenate(
            [jnp.broadcast_to(blockbias_ref[pl.ds(kt * blocks_per_tile + jb, 1), :], (SEL_BLOCK, QB))
             for jb in range(blocks_per_tile)], axis=0)
        if causal:
            kpos = start + lax.broadcasted_iota(jnp.int32, (KT, 1), 0)
            bias = jnp.where(kpos <= t_row, bias, -MASK_BIG)
        sm = s_t + lanes4(bias)
        m_new = jnp.maximum(m_i, jnp.max(sm, axis=0, keepdims=True))
        e = jnp.exp2(sm - m_new).astype(BF16)
        acc_new = jnp.exp2(m_i - m_new) * acc + jnp.dot(vst_ref[:, pl.ds(start, KT)], e,
                                                        preferred_element_type=F32)
        return m_new, acc_new

    init = (jnp.full((1, HP * QB), MAX_FLOOR, F32), jnp.zeros((VT_ROWS, HP * QB), F32))
    carry = lax.fori_loop(0, n_tiles - 1, lambda kt, c: sel_step(kt, c, False), init)
    _, acc_s = sel_step(n_tiles - 1, carry, True)
    den_s = acc_s[D:D + 1]
    o_s = acc_s[:D] * (1.0 / jnp.where(den_s > 0.0, den_s, 1.0))

    span = WINDOW + QB
    w_start = pl.multiple_of(jnp.maximum(qb * QB - WINDOW, 0), QB)
    dist = t_row - (w_start + lax.broadcasted_iota(jnp.int32, (span, 1), 0))
    e_w = softmax_cols(jnp.dot(kw_ref[pl.ds(w_start, span), :], q4, preferred_element_type=F32),
                       jnp.where((dist >= 0) & (dist < WINDOW), 0.0, -MASK_BIG))
    acc_w = jnp.dot(vwt_ref[:, pl.ds(w_start, span)], e_w.astype(BF16), preferred_element_type=F32)
    den_w = acc_w[D:D + 1]
    o_w = acc_w[:D] * (1.0 / jnp.where(den_w > 0.0, den_w, 1.0))

    gates = _sigmoid(gate_ref[...])
    gate_row = lambda br: jnp.concatenate([gates[3 * n + br:3 * n + br + 1, :] for n in range(HP)], axis=1)
    o_t = gate_row(0) * o_c + gate_row(1) * o_s + gate_row(2) * o_w
    for n in range(HP):
        o_ref[:, n * D:(n + 1) * D] = o_t[:, n * QB:(n + 1) * QB].T


def _cmp_to_sel_matrix(n_cmp_rows, n_sel):
    ratio = SEL_BLOCK // CMP_STRIDE
    ci = np.arange(n_cmp_rows)[:, None]
    sj = np.arange(SEL_LANES)[None, :]
    m = sum(((ci + n) // ratio == sj).astype(np.float32) for n in range(CMP_BLOCK // CMP_STRIDE))
    m = m * (sj < n_sel) * (ci < n_cmp_rows - 1)
    return jnp.asarray(m.T, BF16)


def _nsa_attention(qg_t, kc, vc, ks, vst, kw, vwt, cos_t, sin_t):
    B, _, S = qg_t.shape
    n_sub = kc.shape[2]
    n_sel = S // SEL_BLOCK
    gw = NSA_HPG * HEAD_DIM
    gate_row0 = NSA_WIDTH // 128
    msel_t = _cmp_to_sel_matrix(n_sub, n_sel)
    at_bg = lambda shape: pl.BlockSpec((None, None) + shape, lambda b, g, i: (b, g, 0, 0))
    const = lambda a: pl.BlockSpec(a.shape, lambda b, g, i: (0, 0))
    return pl.pallas_call(
        functools.partial(_nsa_kernel, n_pick=min(N_SELECT, n_sel)),
        grid=(B, NSA_GROUPS, S // Q_BLOCK),
        in_specs=[pl.BlockSpec((None, gw, Q_BLOCK), lambda b, g, i: (b, g, i)),
                  pl.BlockSpec((None, 128, Q_BLOCK), lambda b, g, i: (b, gate_row0 + g, i)),
                  pl.BlockSpec((HEAD_DIM, Q_BLOCK), lambda b, g, i: (0, i)),
                  pl.BlockSpec((HEAD_DIM, Q_BLOCK), lambda b, g, i: (0, i)),
                  at_bg((n_sub, HEAD_DIM)), at_bg((n_sub, HEAD_DIM)),
                  at_bg((S, HEAD_DIM)), at_bg((VT_ROWS, S)), at_bg((S, HEAD_DIM)), at_bg((VT_ROWS, S)),
                  const(msel_t)],
        out_specs=pl.BlockSpec((None, Q_BLOCK, gw), lambda b, g, i: (b, i, g)),
        out_shape=jax.ShapeDtypeStruct((B, S, NSA_WIDTH), F32),
        scratch_shapes=[pltpu.VMEM((SEL_LANES, Q_BLOCK), F32)],
        compiler_params=pltpu.CompilerParams(
            dimension_semantics=("parallel", "parallel", "arbitrary"), vmem_limit_bytes=VMEM_LIMIT),
        name="nsa_attention",
    )(qg_t, qg_t, cos_t, sin_t, kc, vc, ks, vst, kw, vwt, msel_t)


def _nsa_branch(p_kv, qg_t, cmp_pe_k, cmp_w1_k, cmp_w2_k, cmp_pe_v, cmp_w1_v, cmp_w2_v):
    B, S, _ = p_kv.shape
    pos = jnp.arange(S)
    cos2, sin2 = _rope_tables(pos, NSA_GROUPS)
    kc_raw, vc_raw, ks, vst, kw, vwt = _kv_layout(p_kv, cos2, sin2)
    n_sub = S // CMP_STRIDE
    sub = lambda t: t.reshape(B, NSA_GROUPS, n_sub, CMP_STRIDE * HEAD_DIM)
    cos_c, sin_c = _rope_tables(jnp.arange(n_sub) * CMP_STRIDE + CMP_BLOCK - 1, 1)
    kc, vc = _compress(sub(kc_raw), sub(vc_raw), cmp_pe_k, cmp_w1_k, cmp_w2_k,
                       cmp_pe_v, cmp_w1_v, cmp_w2_v, cos_c, sin_c)
    cos_q, sin_q = _rope_tables(pos, 1)
    return _nsa_attention(qg_t, kc, vc, ks, vst, kw, vwt, cos_q.T, sin_q.T)


def _nsa_weight_columns(w_nsa):
    K = w_nsa.shape[0]
    q = w_nsa[:, :NSA_WIDTH]
    kv = w_nsa[:, NSA_WIDTH:NSA_WIDTH + 6 * NSA_KV_WIDTH]
    gates = w_nsa[:, NSA_WIDTH + 6 * NSA_KV_WIDTH:]
    per_group = NSA_HPG * 3
    gate_blocks = [jnp.pad(gates[:, g * per_group:(g + 1) * per_group], ((0, 0), (0, 128 - per_group)))
                   for g in range(NSA_GROUPS)]
    return jnp.concatenate([kv, q] + gate_blocks, axis=1)


def _layer_norm(h, g, b):
    mu = jnp.mean(h, axis=-1, keepdims=True)
    var = jnp.mean(jnp.square(h - mu), axis=-1, keepdims=True)
    return (h - mu) * lax.rsqrt(var + LN_EPS) * g + b


def _pack_bf16_halves(x):
    n = x.shape[-1] // 2
    bits = lax.bitcast_convert_type(x.astype(BF16).astype(F32), jnp.uint32)
    return (bits[:, n:] & jnp.uint32(0xFFFF0000)) | (bits[:, :n] >> 16)


def _unpack_bf16_halves(u):
    left = lax.bitcast_convert_type(u << 16, F32)
    right = lax.bitcast_convert_type(u & jnp.uint32(0xFFFF0000), F32)
    return left, right


def _mixer_out_kernel(x_ref, ya_ref, yb_ref, pg_ref, wa_ref, wb_ref, wo_ref, g_ref, b_ref, o_ref, op_ref,
                      *, alpha):
    d = x_ref.shape[-1]
    gate_a = _sigmoid(pg_ref[:, :d])
    gate_b = _sigmoid(pg_ref[:, d:])
    mixed = gate_a * _bdot(ya_ref[...], wa_ref[...]) + gate_b * _bdot(yb_ref[...], wb_ref[...])
    h = alpha * x_ref[...] + _bdot(mixed, wo_ref[...])
    out = _layer_norm(h, g_ref[...], b_ref[...])
    o_ref[...] = out
    op_ref[...] = _pack_bf16_halves(out)


def _mixer_out(xf, ya, yb, p_g, wa, wb, wo, ln_g, ln_b, alpha, tm=512):
    T, D = xf.shape
    rows = lambda w: pl.BlockSpec((tm, w), lambda i: (i, 0))
    full = lambda a: pl.BlockSpec(a.shape, lambda i: (0,) * a.ndim)
    ln_g, ln_b = ln_g.reshape(1, D), ln_b.reshape(1, D)
    return pl.pallas_call(
        functools.partial(_mixer_out_kernel, alpha=alpha),
        grid=(T // tm,),
        in_specs=[rows(D), rows(ya.shape[1]), rows(yb.shape[1]), rows(2 * D),
                  full(wa), full(wb), full(wo), full(ln_g), full(ln_b)],
        out_specs=[rows(D), rows(D // 2)],
        out_shape=[jax.ShapeDtypeStruct((T, D), F32), jax.ShapeDtypeStruct((T, D // 2), jnp.uint32)],
        compiler_params=pltpu.CompilerParams(
            dimension_semantics=("parallel",), vmem_limit_bytes=VMEM_LIMIT),
        name="mixer_out_ln",
    )(xf, ya, yb, p_g, wa, wb, wo, ln_g, ln_b)


ROUTER_TILE = 256
EXPERT_ROWS = 256
SC_TOKEN_CHUNK = 64
SC_ROW_CHUNK = 64
PICK_LANES = 128
LOWEST = -3.0e38


def _router_kernel(x_ref, rwt_ref, bias_ref, eidx_ref, wts_ref, pos_ref, cnt_ref, carry_ref):
    tm, E = x_ref.shape[0], rwt_ref.shape[0]
    per_group = E // N_GROUPS
    reps = tm // PICK_LANES

    @pl.when(pl.program_id(0) == 0)
    def _():
        carry_ref[...] = jnp.zeros_like(carry_ref)

    scores = _sigmoid(_hdot_nt(rwt_ref[...], x_ref[...]))
    choice = scores + jnp.concatenate([bias_ref[...]] * reps, axis=1)
    row = lax.broadcasted_iota(jnp.int32, (E, tm), 0)

    def first_max(vals, rows):
        m = jnp.max(vals, axis=0, keepdims=True)
        return m, jnp.min(jnp.where(vals == m, rows, E), axis=0, keepdims=True)

    group_score = []
    for g in range(N_GROUPS):
        rows = slice(g * per_group, (g + 1) * per_group)
        group_row = g * per_group + lax.broadcasted_iota(jnp.int32, (per_group, tm), 0)
        m1, i1 = first_max(choice[rows], group_row)
        m2 = jnp.max(jnp.where(group_row == i1, LOWEST, choice[rows]), axis=0, keepdims=True)
        group_score.append(m1 + m2)
    masked = []
    for g in range(N_GROUPS):
        rank = jnp.zeros((1, tm), jnp.int32)
        for o in range(N_GROUPS):
            if o != g:
                ahead = (group_score[o] > group_score[g]) if o > g else (group_score[o] >= group_score[g])
                rank = rank + ahead.astype(jnp.int32)
        masked.append(jnp.where(rank < TOPK_GROUPS, choice[g * per_group:(g + 1) * per_group], NEG_INF))

    cur = jnp.concatenate(masked, axis=0)
    picks = []
    for _ in range(TOP_K):
        _, idx = first_max(cur, row)
        picks.append(idx)
        cur = jnp.where(row == idx, LOWEST, cur)
    sel = jnp.where(cur == LOWEST, 1.0, 0.0)
    gate = scores * sel
    gate = gate * (ROUTED_SCALE / jnp.sum(gate, axis=0, keepdims=True))

    ti = lax.broadcasted_iota(jnp.int32, (tm, tm), 0)
    tj = lax.broadcasted_iota(jnp.int32, (tm, tm), 1)
    sel_b = sel.astype(BF16)
    before = jnp.dot(sel_b, (ti < tj).astype(BF16), preferred_element_type=F32)
    queue_pos = before + jnp.concatenate([carry_ref[...]] * reps, axis=1)
    carry_ref[...] = carry_ref[...] + jnp.dot(sel_b, jnp.ones((tm, PICK_LANES), BF16),
                                              preferred_element_type=F32)
    cnt_ref[...] = carry_ref[...]

    at_pick = lambda vals, idx: jnp.sum(jnp.where(row == idx, vals, 0.0), axis=0, keepdims=True)
    eidx_ref[...] = jnp.concatenate(picks, axis=0)
    wts_ref[...] = jnp.concatenate([at_pick(gate, idx) for idx in picks], axis=0)
    pos_ref[...] = jnp.concatenate([at_pick(queue_pos, idx) for idx in picks], axis=0).astype(jnp.int32)


def _router(xf, router_w, router_bias):
    T, D = xf.shape
    E = router_w.shape[1]
    tm = ROUTER_TILE
    picks = lambda dt: jax.ShapeDtypeStruct((TOP_K, T), dt)
    pick_spec = pl.BlockSpec((TOP_K, tm), lambda i: (0, i))
    lanes = lambda v: jnp.broadcast_to(v.reshape(E, 1), (E, PICK_LANES))
    return pl.pallas_call(
        _router_kernel,
        grid=(T // tm,),
        in_specs=[pl.BlockSpec((tm, D), lambda i: (i, 0)),
                  pl.BlockSpec((E, D), lambda i: (0, 0)),
                  pl.BlockSpec((E, PICK_LANES), lambda i: (0, 0))],
        out_specs=[pick_spec, pick_spec, pick_spec, pl.BlockSpec((E, PICK_LANES), lambda i: (0, 0))],
        out_shape=[picks(jnp.int32), picks(F32), picks(jnp.int32),
                   jax.ShapeDtypeStruct((E, PICK_LANES), F32)],
        scratch_shapes=[pltpu.VMEM((E, PICK_LANES), F32)],
        compiler_params=pltpu.CompilerParams(
            dimension_semantics=("arbitrary",), vmem_limit_bytes=VMEM_LIMIT),
        name="moe_router",
    )(xf, router_w.T, lanes(router_bias))


def _dest_kernel(eidx_ref, pos_ref, start_ref, dest_ref):
    E = start_ref.shape[0]
    tm = eidx_ref.shape[1]
    row = lax.broadcasted_iota(jnp.int32, (E, tm), 0)
    start = jnp.concatenate([start_ref[...]] * (tm // PICK_LANES), axis=1)
    eidx = eidx_ref[...]
    base = [jnp.sum(jnp.where(row == eidx[kk:kk + 1, :], start, 0), axis=0, keepdims=True)
            for kk in range(TOP_K)]
    dest_ref[...] = jnp.concatenate(base, axis=0) + pos_ref[...]


def _dest_rows(eidx_t, pos_t, pad_start):
    T = eidx_t.shape[1]
    E = pad_start.shape[0]
    tm = ROUTER_TILE
    pick_spec = pl.BlockSpec((TOP_K, tm), lambda i: (0, i))
    return pl.pallas_call(
        _dest_kernel,
        grid=(T // tm,),
        in_specs=[pick_spec, pick_spec, pl.BlockSpec((E, PICK_LANES), lambda i: (0, 0))],
        out_specs=pick_spec,
        out_shape=jax.ShapeDtypeStruct((TOP_K, T), jnp.int32),
        compiler_params=pltpu.CompilerParams(
            dimension_semantics=("parallel",), vmem_limit_bytes=VMEM_LIMIT),
        name="moe_dest_rows",
    )(eidx_t, pos_t, jnp.broadcast_to(pad_start.reshape(E, 1), (E, PICK_LANES)))


def _sc_mesh():
    return plsc.VectorSubcoreMesh(core_axis_name="c", subcore_axis_name="s")


def _sc_scatter_rows(x, dest_t, n_rows):
    T, D = x.shape
    K = dest_t.shape[0]
    mesh = _sc_mesh()
    nc, nw = mesh.num_cores, mesh.num_cores * mesh.num_subcores
    per_w = T // nw
    chunk = min(SC_TOKEN_CHUNK, per_w)
    n_chunks = per_w // chunk
    idx = dest_t.reshape(K, nw, n_chunks, chunk).transpose(1, 2, 0, 3).reshape(nw, n_chunks * K, chunk)

    assert n_chunks % 2 == 0

    @functools.partial(
        pl.kernel, mesh=mesh,
        out_type=jax.ShapeDtypeStruct((n_rows, D), x.dtype),
        scratch_types=[pltpu.VMEM((n_chunks * K, chunk), jnp.int32),
                       pltpu.VMEM((2, chunk, D), x.dtype),
                       pltpu.SemaphoreType.DMA((2,)), pltpu.SemaphoreType.DMA((2,))],
    )
    def scatter(x_hbm, idx_hbm, out_hbm, idx_v, rows_v, load_sem, send_sem):
        wid = lax.axis_index("s") * nc + lax.axis_index("c")
        pltpu.sync_copy(idx_hbm.at[wid], idx_v)

        def load(j, b):
            return pltpu.make_async_copy(x_hbm.at[pl.ds(wid * per_w + j * chunk, chunk)], rows_v.at[b],
                                         load_sem.at[b])

        def sends(j, b):
            return [pltpu.make_async_copy(rows_v.at[b], out_hbm.at[idx_v.at[j * K + kk]], send_sem.at[b])
                    for kk in range(K)]

        load(0, 0).start()

        @pl.loop(0, n_chunks, step=2)
        def _(j0):
            for b in range(2):
                j = j0 + b
                load(j, b).wait()

                @pl.when(j >= 1)
                def _():
                    for c in sends(j - 1, 1 - b):
                        c.wait()

                @pl.when(j + 1 < n_chunks)
                def _():
                    load(j + 1, 1 - b).start()

                for c in sends(j, b):
                    c.start()

        for c in sends(n_chunks - 1, (n_chunks - 1) % 2):
            c.wait()

    return scatter(x, idx)


def _sc_gather_rows(src, idx):
    M = idx.shape[0]
    D = src.shape[1]
    mesh = _sc_mesh()
    nc, nw = mesh.num_cores, mesh.num_cores * mesh.num_subcores
    per_w = M // nw
    chunk = min(SC_ROW_CHUNK, per_w)
    n_chunks = per_w // chunk

    assert n_chunks % 2 == 0

    @functools.partial(
        pl.kernel, mesh=mesh,
        out_type=jax.ShapeDtypeStruct((M, D), src.dtype),
        scratch_types=[pltpu.VMEM((n_chunks, chunk), jnp.int32),
                       pltpu.VMEM((2, chunk, D), src.dtype),
                       pltpu.SemaphoreType.DMA((2,)), pltpu.SemaphoreType.DMA((2,))],
    )
    def gather(src_hbm, idx_hbm, out_hbm, idx_v, rows_v, fetch_sem, store_sem):
        wid = lax.axis_index("s") * nc + lax.axis_index("c")
        pltpu.sync_copy(idx_hbm.at[wid], idx_v)

        def fetch(j, b):
            return pltpu.make_async_copy(src_hbm.at[idx_v.at[j]], rows_v.at[b], fetch_sem.at[b])

        def store(j, b):
            return pltpu.make_async_copy(rows_v.at[b], out_hbm.at[pl.ds(wid * per_w + j * chunk, chunk)],
                                         store_sem.at[b])

        fetch(0, 0).start()

        @pl.loop(0, n_chunks, step=2)
        def _(j0):
            for b in range(2):
                j = j0 + b
                fetch(j, b).wait()

                @pl.when(j >= 1)
                def _():
                    store(j - 1, 1 - b).wait()

                @pl.when(j + 1 < n_chunks)
                def _():
                    fetch(j + 1, 1 - b).start()

                store(j, b).start()

        store(n_chunks - 1, (n_chunks - 1) % 2).wait()

    return gather(src, idx.reshape(nw, n_chunks, chunk))


def _expert_kernel(distinct_e_ref, blk_ord_ref, blk_new_ref, blk_rows_ref, n_used_ref, n_distinct_ref,
                   x_ref, wgu_hbm, wd_hbm, o_ref, wgu_buf, wd_buf, wgu_bf, wd_bf, sem):
    i = pl.program_id(0)
    live = i < n_used_ref[0]
    ordinal = blk_ord_ref[i]
    slot = ordinal % 2

    def weight_copies(k, s):
        e = distinct_e_ref[k]
        return (pltpu.make_async_copy(wgu_hbm.at[e], wgu_buf.at[s], sem.at[0, s]),
                pltpu.make_async_copy(wd_hbm.at[e], wd_buf.at[s], sem.at[1, s]))

    @pl.when(i == 0)
    def _():
        for c in weight_copies(0, 0):
            c.start()

    @pl.when(live & (blk_new_ref[i] == 1))
    def _():
        for c in weight_copies(ordinal, slot):
            c.wait()

        @pl.when(ordinal + 1 < n_distinct_ref[0])
        def _():
            for c in weight_copies(ordinal + 1, 1 - slot):
                c.start()

        wgu_bf[...] = wgu_buf[slot].astype(BF16)
        wd_bf[...] = wd_buf[slot].astype(BF16)

    @pl.when(live)
    def _():
        hidden = wd_bf.shape[0]
        half = x_ref.shape[1]
        row = lax.broadcasted_iota(jnp.int32, x_ref.shape, 0)
        left, right = _unpack_bf16_halves(x_ref[...])
        real = row < blk_rows_ref[i]
        left = jnp.where(real, left, 0.0).astype(BF16)
        right = jnp.where(real, right, 0.0).astype(BF16)
        h = (jnp.dot(left, wgu_bf[:half, :], preferred_element_type=F32)
             + jnp.dot(right, wgu_bf[half:, :], preferred_element_type=F32))
        gate, up = h[:, :hidden], h[:, hidden:]
        act = (gate * _sigmoid(gate) * up).astype(BF16)
        o_ref[...] = _pack_bf16_halves(jnp.dot(act, wd_bf[...], preferred_element_type=F32))

    @pl.when(jnp.logical_not(live))
    def _():
        o_ref[...] = jnp.zeros_like(o_ref)


def _expert_ffn(xs, blk_e, blk_rows, n_used, w_gu, w_down):
    n_rows, half = xs.shape
    E, D, two_h = w_gu.shape
    n_blocks = n_rows // EXPERT_ROWS
    idx = jnp.arange(n_blocks, dtype=jnp.int32)
    is_live = idx < n_used[0]
    blk_new = (is_live & ((idx == 0) | (blk_e != jnp.roll(blk_e, 1)))).astype(jnp.int32)
    blk_ord = (jnp.cumsum(blk_new) - 1).astype(jnp.int32)
    n_distinct = blk_ord[-1:] + 1
    distinct_e = jnp.zeros((n_blocks,), jnp.int32).at[blk_ord].max(blk_e * is_live)

    live = lambda i, nu: jnp.minimum(i, nu[0] - 1)
    grid_spec = pltpu.PrefetchScalarGridSpec(
        num_scalar_prefetch=6,
        grid=(n_blocks,),
        in_specs=[pl.BlockSpec((EXPERT_ROWS, half), lambda i, de, bo, bn, br, nu, nd: (live(i, nu), 0)),
                  pl.BlockSpec(memory_space=pl.ANY), pl.BlockSpec(memory_space=pl.ANY)],
        out_specs=pl.BlockSpec((EXPERT_ROWS, half), lambda i, de, bo, bn, br, nu, nd: (i, 0)),
        scratch_shapes=[pltpu.VMEM((2, D, two_h), F32), pltpu.VMEM((2, two_h // 2, D), F32),
                        pltpu.VMEM((D, two_h), BF16), pltpu.VMEM((two_h // 2, D), BF16),
                        pltpu.SemaphoreType.DMA((2, 2))],
    )
    return pl.pallas_call(
        _expert_kernel,
        grid_spec=grid_spec,
        out_shape=jax.ShapeDtypeStruct((n_rows, half), jnp.uint32),
        compiler_params=pltpu.CompilerParams(
            dimension_semantics=("arbitrary",), vmem_limit_bytes=VMEM_LIMIT),
        name="moe_experts",
    )(distinct_e, blk_ord, blk_new, blk_rows, n_used, n_distinct, xs, w_gu, w_down)


def _moe_out_kernel(x_ref, yk_ref, wts_ref, sgu_ref, sd_ref, g_ref, b_ref, o_ref, *, alpha):
    x = x_ref[...]
    hidden = sd_ref.shape[0]
    h = _bdot(x, sgu_ref[...])
    gate, up = h[:, :hidden], h[:, hidden:]
    ffn = _bdot(gate * _sigmoid(gate) * up, sd_ref[...])
    wts = wts_ref[...]
    routed_left = routed_right = None
    for kk in range(TOP_K):
        left, right = _unpack_bf16_halves(yk_ref[kk])
        w = wts[:, kk:kk + 1]
        routed_left = w * left if kk == 0 else routed_left + w * left
        routed_right = w * right if kk == 0 else routed_right + w * right
    ffn = ffn + jnp.concatenate([routed_left, routed_right], axis=-1)
    o_ref[...] = _layer_norm(alpha * x + ffn, g_ref[...], b_ref[...])


def _moe_out(xf, yk, wts, sw_gu, sw_down, ln_g, ln_b, alpha, tm=128):
    T, D = xf.shape
    rows = lambda w: pl.BlockSpec((tm, w), lambda i: (i, 0))
    full = lambda a: pl.BlockSpec(a.shape, lambda i: (0,) * a.ndim)
    ln_g, ln_b = ln_g.reshape(1, D), ln_b.reshape(1, D)
    return pl.pallas_call(
        functools.partial(_moe_out_kernel, alpha=alpha),
        grid=(T // tm,),
        in_specs=[rows(D), pl.BlockSpec((TOP_K, tm, D // 2), lambda i: (0, i, 0)), rows(PICK_LANES),
                  full(sw_gu), full(sw_down), full(ln_g), full(ln_b)],
        out_specs=rows(D),
        out_shape=jax.ShapeDtypeStruct((T, D), F32),
        compiler_params=pltpu.CompilerParams(
            dimension_semantics=("parallel",), vmem_limit_bytes=VMEM_LIMIT),
        name="moe_combine_ln",
    )(xf, yk, wts, sw_gu, sw_down, ln_g, ln_b)


def _moe_ffn_ln(xf, xp, router_w, router_bias, w_gu, w_down, sw_gu, sw_down, ln_g, ln_b, alpha):
    T, D = xf.shape
    E = router_w.shape[1]
    BM = EXPERT_ROWS
    eidx_t, wts_t, pos_t, cnt = _router(xf, router_w, router_bias)
    counts = cnt[:, 0].astype(jnp.int32)
    padded = (counts + BM - 1) // BM * BM
    pad_end = jnp.cumsum(padded)
    pad_start = pad_end - padded
    n_rows = T * TOP_K + E * BM
    n_blocks = n_rows // BM
    blk_row0 = jnp.arange(n_blocks, dtype=jnp.int32) * BM
    blk_e = jnp.minimum(jnp.sum((pad_end[None, :] <= blk_row0[:, None]).astype(jnp.int32), axis=1), E - 1)
    blk_rows = jnp.clip(pad_start[blk_e] + counts[blk_e] - blk_row0, 0, BM).astype(jnp.int32)
    n_used = (pad_end[-1:] // BM).astype(jnp.int32)
    dest_t = _dest_rows(eidx_t, pos_t, pad_start)
    wts = jnp.pad(wts_t.T, ((0, 0), (0, PICK_LANES - TOP_K)))
    xs = _sc_scatter_rows(xp, dest_t, n_rows)
    ys = _expert_ffn(xs, blk_e, blk_rows, n_used, w_gu, w_down)
    yk = _sc_gather_rows(ys, dest_t.reshape(-1)).reshape(TOP_K, T, D // 2)
    return _moe_out(xf, yk, wts, sw_gu, sw_down, ln_g, ln_b, alpha)


def kernel(x, w_in, tshift_mu, rwkv_w0, rwkv_w2, rwkv_a0, rwkv_a2, rwkv_g2, rwkv_k_k, rwkv_k_a, rwkv_r_k, rwkv_lnx_w, rwkv_lnx_b, cmp_pe_k, cmp_w1_k, cmp_w2_k, cmp_pe_v, cmp_w1_v, cmp_w2_v, w_branch_a, w_branch_b, w_out, ln1_g, ln1_b, router_w, router_bias, exp_w_gu, exp_w_down, shared_w_gu, shared_w_down, ln2_g, ln2_b):
    B, S, D = x.shape
    depth = w_in.shape[0]
    alpha = (2 * depth) ** 0.25
    nsa_w = w_in.shape[2] - RWKV_IN_W - 2 * D
    for l in range(depth):
        xf = x.reshape(B * S, D)
        w_l = w_in[l]
        w_a = w_l[:, :RWKV_IN_W].astype(BF16)
        w_b = _nsa_weight_columns(w_l[:, RWKV_IN_W:RWKV_IN_W + nsa_w]).astype(BF16)
        w_g = w_l[:, RWKV_IN_W + nsa_w:].astype(BF16)
        kv_w = 6 * NSA_KV_WIDTH
        p_a = _matmul(xf, w_a, PROJ_ROWS, w_a.shape[1]).reshape(B, S, -1)
        p_kv = _matmul(xf, w_b[:, :kv_w], PROJ_ROWS, kv_w).reshape(B, S, -1)
        qg_t = _matmul_t(x, w_b[:, kv_w:].T, PROJ_ROWS)
        p_g = _matmul(xf, w_g, PROJ_ROWS, w_g.shape[1])
        y_a = _rwkv_time_mix(p_a, tshift_mu[l], rwkv_w0[l], rwkv_w2[l], rwkv_a0[l], rwkv_a2[l], rwkv_g2[l],
                             rwkv_k_k[l], rwkv_k_a[l], rwkv_r_k[l].reshape(-1), rwkv_lnx_w[l], rwkv_lnx_b[l])
        y_b = _nsa_branch(p_kv, qg_t, cmp_pe_k[l], cmp_w1_k[l], cmp_w2_k[l],
                          cmp_pe_v[l], cmp_w1_v[l], cmp_w2_v[l])
        x1, x1p = _mixer_out(xf, y_a.reshape(B * S, -1), y_b.reshape(B * S, -1), p_g,
                             w_branch_a[l].astype(BF16), w_branch_b[l].astype(BF16), w_out[l].astype(BF16),
                             ln1_g[l], ln1_b[l], alpha)
        x2 = _moe_ffn_ln(x1, x1p, router_w[l], router_bias[l], exp_w_gu[l], exp_w_down[l],
                         shared_w_gu[l].astype(BF16), shared_w_down[l].astype(BF16), ln2_g[l], ln2_b[l], alpha)
        x = x2.reshape(B, S, D)
    return x
```

```python
import functools

import numpy as np
import jax
import jax.numpy as jnp
from jax import lax
from jax.experimental import pallas as pl
from jax.experimental.pallas import tpu as pltpu
from jax.experimental.pallas import tpu_sc as plsc

F32 = jnp.float32
BF16 = jnp.bfloat16

RWKV_HEADS = 8
HEAD_DIM = 64
RWKV_WIDTH = RWKV_HEADS * HEAD_DIM
W_LORA = 64
A_LORA = 64
G_LORA = 128
GN_EPS = 64e-5
NSA_HEADS = 8
NSA_GROUPS = 2
NSA_HPG = NSA_HEADS // NSA_GROUPS
NSA_WIDTH = NSA_HEADS * HEAD_DIM
NSA_KV_WIDTH = NSA_GROUPS * HEAD_DIM
CMP_BLOCK = 32
CMP_STRIDE = 16
CMP_HIDDEN = 256
SEL_BLOCK = 64
N_SELECT = 16
WINDOW = 512
Q_BLOCK = 128
ROPE_THETA = 10000.0
RWKV_IN_W = 3 * RWKV_WIDTH + W_LORA + A_LORA + G_LORA
N_EXPERTS = 256
TOP_K = 8
N_GROUPS = 8
TOPK_GROUPS = 4
EXPERT_DIM = 256
ROUTED_SCALE = 2.5
LN_EPS = 1e-5
NEG_INF = -1e30
FORCE_BONUS = 1e4

RWKV_CHUNK = 64
RWKV_HEAD_GROUP = 4
RWKV_STEP_CHUNKS = 2
VMEM_LIMIT = 56 * 1024 * 1024
PROJ_ROWS = 512


def _bdot(a, b):
    return jnp.dot(a.astype(BF16), b.astype(BF16), preferred_element_type=F32)


def _bdot_nt(a, b):
    return lax.dot_general(a.astype(BF16), b.astype(BF16), (((1,), (1,)), ((), ())),
                           preferred_element_type=F32)


def _bdot_tn(a, b):
    return lax.dot_general(a.astype(BF16), b.astype(BF16), (((0,), (0,)), ((), ())),
                           preferred_element_type=F32)


def _bf16_pieces(x, n):
    pieces = []
    for _ in range(n):
        p = x.astype(BF16)
        pieces.append(p)
        x = x - p.astype(F32)
    return pieces


def _dot3(a, b, dims=(((1,), (0,)), ((), ()))):
    (a_hi, a_lo), (b_hi, b_lo) = _bf16_pieces(a, 2), _bf16_pieces(b, 2)
    dot = lambda p, q: lax.dot_general(p, q, dims, preferred_element_type=F32)
    return dot(a_hi, b_hi) + (dot(a_hi, b_lo) + dot(a_lo, b_hi))


def _sigmoid(x):
    return 1.0 / (1.0 + jnp.exp(-x))


def _matmul_kernel(x_ref, w_ref, o_ref):
    o_ref[...] = jnp.dot(x_ref[...].astype(BF16), w_ref[...], preferred_element_type=F32)


def _matmul(x, w, tm, tn):
    M, K = x.shape
    N = w.shape[1]
    return pl.pallas_call(
        _matmul_kernel,
        grid=(M // tm, N // tn),
        in_specs=[pl.BlockSpec((tm, K), lambda i, j: (i, 0)),
                  pl.BlockSpec((K, tn), lambda i, j: (0, j))],
        out_specs=pl.BlockSpec((tm, tn), lambda i, j: (i, j)),
        out_shape=jax.ShapeDtypeStruct((M, N), F32),
        compiler_params=pltpu.CompilerParams(
            dimension_semantics=("parallel", "parallel"), vmem_limit_bytes=VMEM_LIMIT),
        name="dense_proj",
    )(x, w)


def _matmul_t_kernel(x_ref, wt_ref, o_ref):
    o_ref[...] = lax.dot_general(wt_ref[...], x_ref[...].astype(BF16), (((1,), (1,)), ((), ())),
                                 preferred_element_type=F32)


def _matmul_t(x, w_t, tm):
    B, S, K = x.shape
    N = w_t.shape[0]
    return pl.pallas_call(
        _matmul_t_kernel,
        grid=(B, S // tm),
        in_specs=[pl.BlockSpec((None, tm, K), lambda b, s: (b, s, 0)),
                  pl.BlockSpec((N, K), lambda b, s: (0, 0))],
        out_specs=pl.BlockSpec((None, N, tm), lambda b, s: (b, 0, s)),
        out_shape=jax.ShapeDtypeStruct((B, N, S), F32),
        compiler_params=pltpu.CompilerParams(
            dimension_semantics=("parallel", "parallel"), vmem_limit_bytes=VMEM_LIMIT),
        name="dense_proj_t",
    )(x, w_t)


def _rwkv_kernel(p_ref, mu_ref, w0_ref, w2_ref, a0_ref, a2_ref, g2_ref, kk_ref, ka_ref, rk_ref,
                 lnw_ref, lnb_ref, o_ref, carry_ref, state_ref):
    C, H, N = RWKV_CHUNK, RWKV_HEADS, HEAD_DIM
    W = RWKV_WIDTH
    B = p_ref.shape[0]
    NC = p_ref.shape[1] // C
    L = NC * C
    R = B * L

    @pl.when(pl.program_id(0) == 0)
    def _():
        carry_ref[...] = jnp.zeros_like(carry_ref)
        state_ref[...] = jnp.zeros_like(state_ref)

    def per_block(x, rows):
        return jnp.concatenate(
            [jnp.broadcast_to(x[i].reshape(1, -1), (rows, x.shape[-1])) for i in range(x.shape[0])], axis=0)

    p = p_ref[...].reshape(R, p_ref.shape[-1])
    row = lax.broadcasted_iota(jnp.int32, p.shape, 0)
    prev = jnp.where(row % L == 0, per_block(carry_ref[...], L), pltpu.roll(p, 1, axis=0))
    for b in range(B):
        carry_ref[b] = p[b * L + L - 1:b * L + L, :]
    xs = p + (prev - p) * mu_ref[...]
    r = xs[:, 0:W]
    k = xs[:, W:2 * W]
    v = xs[:, 2 * W:3 * W]
    wl = xs[:, 3 * W:3 * W + W_LORA]
    al = xs[:, 3 * W + W_LORA:3 * W + W_LORA + A_LORA]
    gl = xs[:, 3 * W + W_LORA + A_LORA:]

    z = -(w0_ref[...] + _dot3(jnp.tanh(wl), w2_ref[...]))
    softplus = jnp.maximum(z, 0.0) + jnp.log1p(jnp.exp(-jnp.abs(z)))
    logd = -jnp.exp(-softplus - 0.5)
    a = _sigmoid(a0_ref[...] + _dot3(al, a2_ref[...]))
    g = _dot3(_sigmoid(gl), g2_ref[...])

    kk = k * kk_ref[...]
    knew = k * (1.0 + (a - 1.0) * ka_ref[...])

    HG = RWKV_HEAD_GROUP
    GW = HG * N
    same_head_lanes = (lax.broadcasted_iota(jnp.int32, (GW, GW), 0) // N
                       == lax.broadcasted_iota(jnp.int32, (GW, GW), 1) // N)
    head_ones = jnp.where(same_head_lanes, 1.0, 0.0).astype(BF16)

    def head_sum(x):
        hi = x.astype(BF16)
        lo = (x - hi.astype(F32)).astype(BF16)
        return jnp.concatenate(
            [jnp.dot(hi[:, s:s + GW], head_ones, preferred_element_type=F32)
             + jnp.dot(lo[:, s:s + GW], head_ones, preferred_element_type=F32) for s in range(0, W, GW)],
            axis=-1)

    kk = kk / jnp.maximum(jnp.sqrt(head_sum(kk * kk)), 1e-12)
    lr_kk = kk * a

    ti = lax.broadcasted_iota(jnp.int32, (R, R), 0)
    tj = lax.broadcasted_iota(jnp.int32, (R, R), 1)
    same_chunk = (ti >= tj) & (ti // C == tj // C)
    tri = same_chunk.astype(BF16)
    cl = sum(jnp.dot(tri, piece, preferred_element_type=F32) for piece in reversed(_bf16_pieces(logd, 3)))
    cl_end = per_block(jnp.concatenate([cl[i * C + C - 1:i * C + C, :] for i in range(B * NC)], axis=0), C)
    a_hat = -kk * jnp.exp(cl - logd)
    r_hat = r * jnp.exp(cl)
    inv_gam = jnp.exp(-cl)
    b_til = lr_kk * inv_gam
    k_til = knew * inv_gam
    to_end = jnp.exp(cl_end - cl)
    b_end = lr_kk * to_end
    k_end = knew * to_end
    gam_end = jnp.exp(cl_end)

    gt = lax.broadcasted_iota(jnp.int32, (C, GW), 0)
    gc = lax.broadcasted_iota(jnp.int32, (C, GW), 1) % N
    strict = gt > gc
    incl = gt >= gc
    eye = (gt == gc).astype(F32)
    bi = lax.broadcasted_iota(jnp.int32, (HG * C, GW), 0) // C
    bj = lax.broadcasted_iota(jnp.int32, (HG * C, GW), 1) // N
    same_head = bi == bj

    def block_diag(y):
        yb = y.astype(BF16)
        return jnp.where(same_head, jnp.concatenate([yb] * HG, axis=0), jnp.zeros((), BF16))

    def bd_dot(x, y_bd):
        return jnp.dot(x.astype(BF16), y_bd, preferred_element_type=F32)

    def bd_dot_nt(x, y_bd):
        return lax.dot_general(x.astype(BF16), y_bd, (((1,), (1,)), ((), ())), preferred_element_type=F32)

    n_groups = H // HG
    units = [(b, c, gi) for b in range(B) for c in range(NC) for gi in range(n_groups)]
    n_units = range(len(units))
    cut = lambda x, b, c, gi: x[(b * NC + c) * C:(b * NC + c + 1) * C, gi * GW:(gi + 1) * GW]
    v_u = [cut(v, *un) for un in units]
    v_bd = [block_diag(v_u[i]) for i in n_units]
    ar = [jnp.concatenate([cut(a_hat, *un), cut(r_hat, *un)], axis=0) for un in units]
    mb = [bd_dot_nt(ar[i], block_diag(cut(b_til, *units[i]))) for i in n_units]
    mk = [bd_dot_nt(ar[i], block_diag(cut(k_til, *units[i]))) for i in n_units]
    n_ab = [jnp.where(strict, mb[i][:C], 0.0) for i in n_units]
    m_rb = [jnp.where(incl, mb[i][C:], 0.0) for i in n_units]
    l_ak = [jnp.where(strict, mk[i][:C], 0.0) for i in n_units]
    m_rk = [jnp.where(incl, mk[i][C:], 0.0) for i in n_units]

    pw = list(n_ab)
    pw_bd = [block_diag(pw[i]) for i in n_units]
    tinv = [eye + n_ab[i] for i in n_units]
    step = 2
    while step < C:
        pw = [bd_dot(pw[i], pw_bd[i]) for i in n_units]
        pw_bd = [block_diag(pw[i]) for i in n_units]
        tinv = [tinv[i] + bd_dot(tinv[i], pw_bd[i]) for i in n_units]
        step *= 2
    lv = [bd_dot(l_ak[i], v_bd[i]) for i in n_units]

    state = {(b, gi): state_ref[b * n_groups + gi] for b in range(B) for gi in range(n_groups)}
    outs = {}
    for c in range(NC):
        live = [i for i in n_units if units[i][1] == c]
        s0 = {i: state[(units[i][0], units[i][2])] for i in live}
        ars = {i: bd_dot_nt(ar[i], block_diag(s0[i])) for i in live}
        u = {i: bd_dot(tinv[i], block_diag(ars[i][:C] + lv[i])) for i in live}
        for i in live:
            outs[units[i]] = ars[i][C:] + bd_dot(m_rb[i], block_diag(u[i])) + bd_dot(m_rk[i], v_bd[i])
        for i in live:
            b, _, gi = units[i]
            uv = jnp.concatenate([u[i], v_u[i]], axis=0)
            bk_end = jnp.concatenate([cut(b_end, *units[i]), cut(k_end, *units[i])], axis=0)
            cross = jnp.where(same_head, _bdot_tn(uv, bk_end), 0.0)
            upd = cross[0:N]
            for h in range(1, HG):
                upd = upd + cross[h * N:(h + 1) * N]
            state[(b, gi)] = s0[i] * cut(gam_end, *units[i])[0:1] + upd
    for (b, gi), s_new in state.items():
        state_ref[b * n_groups + gi] = s_new

    o = jnp.concatenate([jnp.concatenate([outs[(b, c, gi)] for gi in range(n_groups)], axis=-1)
                         for b in range(B) for c in range(NC)], axis=0)
    mean = head_sum(o) * (1.0 / N)
    var = head_sum(jnp.square(o - mean)) * (1.0 / N)
    o = (o - mean) * lax.rsqrt(var + GN_EPS) * lnw_ref[...] + lnb_ref[...]
    bonus = head_sum(r * knew * rk_ref[...]) * v
    o_ref[...] = ((o + bonus) * g).reshape(o_ref.shape)


def _rwkv_time_mix(p_a, mu, w0, w2, a0, a2, g2, k_k, k_a, r_k, lnx_w, lnx_b):
    B, S, _ = p_a.shape
    L = RWKV_CHUNK * RWKV_STEP_CHUNKS
    row = lambda t: t.reshape(1, -1)
    full = lambda shape: pl.BlockSpec(shape, lambda s: (0,) * len(shape))
    n_units = B * RWKV_HEADS // RWKV_HEAD_GROUP
    return pl.pallas_call(
        _rwkv_kernel,
        grid=(S // L,),
        in_specs=[pl.BlockSpec((B, L, RWKV_IN_W), lambda s: (0, s, 0)),
                  full((1, RWKV_IN_W)), full((1, RWKV_WIDTH)), full((W_LORA, RWKV_WIDTH)),
                  full((1, RWKV_WIDTH)), full((A_LORA, RWKV_WIDTH)), full((G_LORA, RWKV_WIDTH)),
                  full((1, RWKV_WIDTH)), full((1, RWKV_WIDTH)), full((1, RWKV_WIDTH)),
                  full((1, RWKV_WIDTH)), full((1, RWKV_WIDTH))],
        out_specs=pl.BlockSpec((B, L, RWKV_WIDTH), lambda s: (0, s, 0)),
        out_shape=jax.ShapeDtypeStruct((B, S, RWKV_WIDTH), F32),
        scratch_shapes=[pltpu.VMEM((B, 1, RWKV_IN_W), F32),
                        pltpu.VMEM((n_units, HEAD_DIM, RWKV_HEAD_GROUP * HEAD_DIM), F32)],
        compiler_params=pltpu.CompilerParams(
            dimension_semantics=("arbitrary",), vmem_limit_bytes=VMEM_LIMIT),
        name="rwkv7_chunked",
    )(p_a, row(mu), row(w0), w2, row(a0), a2, g2, row(k_k), row(k_a), row(r_k), row(lnx_w), row(lnx_b))


NSA_KV_TILE = 1024
SEL_KEY_TILE = 1024
SEL_LANES = 128


def _rope_tables(pos, reps):
    half = HEAD_DIM // 2
    inv = ROPE_THETA ** (-jnp.arange(half, dtype=F32) / half)
    ang = pos.astype(F32)[:, None] * inv
    cos, sin = jnp.cos(ang), jnp.sin(ang)
    cosf = jnp.concatenate([cos, cos], -1)
    sinf = jnp.concatenate([-sin, sin], -1)
    return jnp.tile(cosf, (1, reps)), jnp.tile(sinf, (1, reps))


def _rope(x, cosf, sinf):
    width = x.shape[-1]
    lane = lax.broadcasted_iota(jnp.int32, x.shape, 1)
    first_half = (lane % HEAD_DIM) < HEAD_DIM // 2
    rot = jnp.where(first_half, pltpu.roll(x, width - HEAD_DIM // 2, axis=1),
                    pltpu.roll(x, HEAD_DIM // 2, axis=1))
    return x * cosf + rot * sinf


def _kv_layout_kernel(p_ref, cos_ref, sin_ref, kc_ref, vc_ref, ks_ref, vs_ref, kw_ref, vw_ref):
    ts = p_ref.shape[0]
    for i, o_ref in ((0, kc_ref), (1, vc_ref), (2, ks_ref), (4, kw_ref)):
        t = p_ref[:, i * NSA_KV_WIDTH:(i + 1) * NSA_KV_WIDTH]
        if i >= 2:
            t = _rope(t, cos_ref[...], sin_ref[...])
        for g in range(NSA_GROUPS):
            o_ref[g] = t[:, g * HEAD_DIM:(g + 1) * HEAD_DIM].astype(o_ref.dtype)
    pad_row = lax.broadcasted_iota(jnp.int32, (VT_ROWS - HEAD_DIM, ts), 0)
    ones_row = jnp.where(pad_row == 0, 1.0, 0.0)
    for i, o_ref in ((3, vs_ref), (5, vw_ref)):
        t_t = p_ref[:, i * NSA_KV_WIDTH:(i + 1) * NSA_KV_WIDTH].T
        for g in range(NSA_GROUPS):
            o_ref[g] = jnp.concatenate([t_t[g * HEAD_DIM:(g + 1) * HEAD_DIM], ones_row],
                                       axis=0).astype(o_ref.dtype)


def _kv_layout(p_b, cos2, sin2):
    B, S, _ = p_b.shape
    ts = min(NSA_KV_TILE, S)
    out_spec = pl.BlockSpec((None, NSA_GROUPS, ts, HEAD_DIM), lambda b, s: (b, 0, s, 0))
    vt_spec = pl.BlockSpec((None, NSA_GROUPS, VT_ROWS, ts), lambda b, s: (b, 0, 0, s))
    shp = lambda dt: jax.ShapeDtypeStruct((B, NSA_GROUPS, S, HEAD_DIM), dt)
    vt_shp = jax.ShapeDtypeStruct((B, NSA_GROUPS, VT_ROWS, S), BF16)
    return pl.pallas_call(
        _kv_layout_kernel,
        grid=(B, S // ts),
        in_specs=[pl.BlockSpec((None, ts, 6 * NSA_KV_WIDTH), lambda b, s: (b, s, 0)),
                  pl.BlockSpec((ts, NSA_KV_WIDTH), lambda b, s: (s, 0)),
                  pl.BlockSpec((ts, NSA_KV_WIDTH), lambda b, s: (s, 0))],
        out_specs=[out_spec, out_spec, out_spec, vt_spec, out_spec, vt_spec],
        out_shape=[shp(F32), shp(F32), shp(BF16), vt_shp, shp(BF16), vt_shp],
        compiler_params=pltpu.CompilerParams(
            dimension_semantics=("parallel", "parallel"), vmem_limit_bytes=VMEM_LIMIT),
        name="nsa_kv_layout",
    )(p_b, cos2, sin2)


def _compress_kernel(subk_ref, subv_ref, pek_ref, w1k_ref, w2k_ref, pev_ref, w1v_ref, w2v_ref,
                     cos_ref, sin_ref, kc_ref, vc_ref):
    n_sub = subk_ref.shape[0]
    half = CMP_STRIDE * HEAD_DIM

    def mlp(sub_ref, pe_ref, w1_ref, w2_ref):
        sub = sub_ref[...]
        top = _bdot(sub, w1_ref[:half, :])
        bot = _bdot(sub, w1_ref[half:, :])
        bias = _bdot(jnp.broadcast_to(pe_ref[...], (8, 2 * half)), w1_ref[...])[0:1, :]
        h = top + pltpu.roll(bot, n_sub - 1, axis=0) + bias
        return _bdot(jax.nn.gelu(h), w2_ref[...])

    kc = mlp(subk_ref, pek_ref, w1k_ref, w2k_ref)
    rot = jnp.concatenate([kc[:, HEAD_DIM // 2:], kc[:, :HEAD_DIM // 2]], axis=-1)
    kc_ref[...] = (kc * cos_ref[...] + rot * sin_ref[...]).astype(kc_ref.dtype)
    vc_ref[...] = mlp(subv_ref, pev_ref, w1v_ref, w2v_ref).astype(vc_ref.dtype)


def _compress(subk, subv, pe_k, w1_k, w2_k, pe_v, w1_v, w2_v, cos_c, sin_c):
    B, G, n_sub, width = subk.shape
    sub_spec = pl.BlockSpec((None, None, n_sub, width), lambda b, g: (b, g, 0, 0))
    full = lambda a: pl.BlockSpec(a.shape, lambda b, g: (0,) * a.ndim)
    out_spec = pl.BlockSpec((None, None, n_sub, HEAD_DIM), lambda b, g: (b, g, 0, 0))
    pe_k, pe_v = pe_k.reshape(1, -1), pe_v.reshape(1, -1)
    args = (pe_k, w1_k, w2_k, pe_v, w1_v, w2_v, cos_c, sin_c)
    return pl.pallas_call(
        _compress_kernel,
        grid=(B, G),
        in_specs=[sub_spec, sub_spec] + [full(a) for a in args],
        out_specs=[out_spec, out_spec],
        out_shape=[jax.ShapeDtypeStruct((B, G, n_sub, HEAD_DIM), BF16)] * 2,
        compiler_params=pltpu.CompilerParams(
            dimension_semantics=("parallel", "parallel"), vmem_limit_bytes=VMEM_LIMIT),
        name="nsa_compress",
    )(subk, subv, *args)


MAX_FLOOR = -1e20
MASK_BIG = 2.0 ** 100
LOG2_E = 1.4426950408889634
VT_ROWS = 80


def _nsa_kernel(q_ref, gate_ref, cos_ref, sin_ref, kc_ref, vc_ref, ks_ref, vst_ref, kw_ref, vwt_ref,
                mselt_ref, o_ref, blockbias_ref, *, n_pick):
    QB, HP, D = Q_BLOCK, NSA_HPG, HEAD_DIM
    qb = pl.program_id(2)
    n_cmp = kc_ref.shape[0]
    lanes4 = lambda x: jnp.concatenate([x] * HP, axis=1)

    heads = []
    for n in range(HP):
        qh = q_ref[n * D:(n + 1) * D, :]
        rot = jnp.concatenate([qh[D // 2:], qh[:D // 2]], axis=0)
        heads.append(qh * cos_ref[...] + rot * sin_ref[...])
    q4 = (jnp.concatenate(heads, axis=1) * (D ** -0.5 * LOG2_E)).astype(BF16)
    t_row = qb * QB + lax.broadcasted_iota(jnp.int32, (1, QB), 1)

    def softmax_cols(s_t, bias_t):
        sm = s_t + lanes4(bias_t)
        m = jnp.maximum(jnp.max(sm, axis=0, keepdims=True), MAX_FLOOR)
        return jnp.exp2(sm - m)

    cmp_end = lax.broadcasted_iota(jnp.int32, (n_cmp, 1), 0) * CMP_STRIDE + (CMP_BLOCK - 1)
    e_c = softmax_cols(jnp.dot(kc_ref[...], q4, preferred_element_type=F32),
                       jnp.where(cmp_end <= t_row, 0.0, -MASK_BIG))
    den_c = jnp.sum(e_c, axis=0, keepdims=True)
    p_c = e_c * (1.0 / jnp.where(den_c > 0.0, den_c, 1.0))
    o_c = _bdot_tn(vc_ref[...], p_c)
    p_sum = p_c[:, 0:QB]
    for n in range(1, HP):
        p_sum = p_sum + p_c[:, n * QB:(n + 1) * QB]
    p_hi = p_sum.astype(BF16)
    p_lo = (p_sum - p_hi.astype(F32)).astype(BF16)
    imp_t = (jnp.dot(mselt_ref[...], p_hi, preferred_element_type=F32)
             + jnp.dot(mselt_ref[...], p_lo, preferred_element_type=F32))

    j = lax.broadcasted_iota(jnp.int32, (SEL_LANES, QB), 0)
    cur = t_row // SEL_BLOCK
    valid = j * SEL_BLOCK <= t_row
    forced = (j == 0) | (j == cur) | (j == cur - 1)
    score = jnp.where(valid, imp_t + jnp.where(forced, FORCE_BONUS, 0.0), -1.0)
    for _ in range(n_pick):
        m = jnp.max(score, axis=0, keepdims=True)
        idx = jnp.min(jnp.where(score == m, j, SEL_LANES), axis=0, keepdims=True)
        score = jnp.where(j == idx, -2.0, score)
    blockbias_ref[...] = jnp.where((score == -2.0) & valid, 0.0, -MASK_BIG)

    KT = SEL_KEY_TILE
    blocks_per_tile = KT // SEL_BLOCK
    n_tiles = (qb * QB + QB + KT - 1) // KT

    def sel_step(kt, carry, causal):
        m_i, acc = carry
        start = pl.multiple_of(kt * KT, KT)
        s_t = jnp.dot(ks_ref[pl.ds(start, KT), :], q4, preferred_element_type=F32)
        bias = jnp.concatenate(
            [jnp.broadcast_to(blockbias_ref[pl.ds(kt * blocks_per_tile + jb, 1), :], (SEL_BLOCK, QB))
             for jb in range(blocks_per_tile)], axis=0)
        if causal:
            kpos = start + lax.broadcasted_iota(jnp.int32, (KT, 1), 0)
            bias = jnp.where(kpos <= t_row, bias, -MASK_BIG)
        sm = s_t + lanes4(bias)
        m_new = jnp.maximum(m_i, jnp.max(sm, axis=0, keepdims=True))
        e = jnp.exp2(sm - m_new).astype(BF16)
        acc_new = jnp.exp2(m_i - m_new) * acc + jnp.dot(vst_ref[:, pl.ds(start, KT)], e,
                                                        preferred_element_type=F32)
        return m_new, acc_new

    init = (jnp.full((1, HP * QB), MAX_FLOOR, F32), jnp.zeros((VT_ROWS, HP * QB), F32))
    carry = lax.fori_loop(0, n_tiles - 1, lambda kt, c: sel_step(kt, c, False), init)
    _, acc_s = sel_step(n_tiles - 1, carry, True)
    den_s = acc_s[D:D + 1]
    o_s = acc_s[:D] * (1.0 / jnp.where(den_s > 0.0, den_s, 1.0))

    span = WINDOW + QB
    w_start = pl.multiple_of(jnp.maximum(qb * QB - WINDOW, 0), QB)
    dist = t_row - (w_start + lax.broadcasted_iota(jnp.int32, (span, 1), 0))
    e_w = softmax_cols(jnp.dot(kw_ref[pl.ds(w_start, span), :], q4, preferred_element_type=F32),
                       jnp.where((dist >= 0) & (dist < WINDOW), 0.0, -MASK_BIG))
    acc_w = jnp.dot(vwt_ref[:, pl.ds(w_start, span)], e_w.astype(BF16), preferred_element_type=F32)
    den_w = acc_w[D:D + 1]
    o_w = acc_w[:D] * (1.0 / jnp.where(den_w > 0.0, den_w, 1.0))

    gates = _sigmoid(gate_ref[...])
    gate_row = lambda br: jnp.concatenate([gates[3 * n + br:3 * n + br + 1, :] for n in range(HP)], axis=1)
    o_t = gate_row(0) * o_c + gate_row(1) * o_s + gate_row(2) * o_w
    for n in range(HP):
        o_ref[:, n * D:(n + 1) * D] = o_t[:, n * QB:(n + 1) * QB].T


def _cmp_to_sel_matrix(n_cmp_rows, n_sel):
    ratio = SEL_BLOCK // CMP_STRIDE
    ci = np.arange(n_cmp_rows)[:, None]
    sj = np.arange(SEL_LANES)[None, :]
    m = sum(((ci + n) // ratio == sj).astype(np.float32) for n in range(CMP_BLOCK // CMP_STRIDE))
    m = m * (sj < n_sel) * (ci < n_cmp_rows - 1)
    return jnp.asarray(m.T, BF16)


def _nsa_attention(qg_t, kc, vc, ks, vst, kw, vwt, cos_t, sin_t):
    B, _, S = qg_t.shape
    n_sub = kc.shape[2]
    n_sel = S // SEL_BLOCK
    gw = NSA_HPG * HEAD_DIM
    gate_row0 = NSA_WIDTH // 128
    msel_t = _cmp_to_sel_matrix(n_sub, n_sel)
    at_bg = lambda shape: pl.BlockSpec((None, None) + shape, lambda b, g, i: (b, g, 0, 0))
    const = lambda a: pl.BlockSpec(a.shape, lambda b, g, i: (0, 0))
    return pl.pallas_call(
        functools.partial(_nsa_kernel, n_pick=min(N_SELECT, n_sel)),
        grid=(B, NSA_GROUPS, S // Q_BLOCK),
        in_specs=[pl.BlockSpec((None, gw, Q_BLOCK), lambda b, g, i: (b, g, i)),
                  pl.BlockSpec((None, 128, Q_BLOCK), lambda b, g, i: (b, gate_row0 + g, i)),
                  pl.BlockSpec((HEAD_DIM, Q_BLOCK), lambda b, g, i: (0, i)),
                  pl.BlockSpec((HEAD_DIM, Q_BLOCK), lambda b, g, i: (0, i)),
                  at_bg((n_sub, HEAD_DIM)), at_bg((n_sub, HEAD_DIM)),
                  at_bg((S, HEAD_DIM)), at_bg((VT_ROWS, S)), at_bg((S, HEAD_DIM)), at_bg((VT_ROWS, S)),
                  const(msel_t)],
        out_specs=pl.BlockSpec((None, Q_BLOCK, gw), lambda b, g, i: (b, i, g)),
        out_shape=jax.ShapeDtypeStruct((B, S, NSA_WIDTH), F32),
        scratch_shapes=[pltpu.VMEM((SEL_LANES, Q_BLOCK), F32)],
        compiler_params=pltpu.CompilerParams(
            dimension_semantics=("parallel", "parallel", "arbitrary"), vmem_limit_bytes=VMEM_LIMIT),
        name="nsa_attention",
    )(qg_t, qg_t, cos_t, sin_t, kc, vc, ks, vst, kw, vwt, msel_t)


def _nsa_branch(p_kv, qg_t, cmp_pe_k, cmp_w1_k, cmp_w2_k, cmp_pe_v, cmp_w1_v, cmp_w2_v):
    B, S, _ = p_kv.shape
    pos = jnp.arange(S)
    cos2, sin2 = _rope_tables(pos, NSA_GROUPS)
    kc_raw, vc_raw, ks, vst, kw, vwt = _kv_layout(p_kv, cos2, sin2)
    n_sub = S // CMP_STRIDE
    sub = lambda t: t.reshape(B, NSA_GROUPS, n_sub, CMP_STRIDE * HEAD_DIM)
    cos_c, sin_c = _rope_tables(jnp.arange(n_sub) * CMP_STRIDE + CMP_BLOCK - 1, 1)
    kc, vc = _compress(sub(kc_raw), sub(vc_raw), cmp_pe_k, cmp_w1_k, cmp_w2_k,
                       cmp_pe_v, cmp_w1_v, cmp_w2_v, cos_c, sin_c)
    cos_q, sin_q = _rope_tables(pos, 1)
    return _nsa_attention(qg_t, kc, vc, ks, vst, kw, vwt, cos_q.T, sin_q.T)


def _nsa_weight_columns(w_nsa):
    K = w_nsa.shape[0]
    q = w_nsa[:, :NSA_WIDTH]
    kv = w_nsa[:, NSA_WIDTH:NSA_WIDTH + 6 * NSA_KV_WIDTH]
    gates = w_nsa[:, NSA_WIDTH + 6 * NSA_KV_WIDTH:]
    per_group = NSA_HPG * 3
    gate_blocks = [jnp.pad(gates[:, g * per_group:(g + 1) * per_group], ((0, 0), (0, 128 - per_group)))
                   for g in range(NSA_GROUPS)]
    return jnp.concatenate([kv, q] + gate_blocks, axis=1)


def _layer_norm(h, g, b):
    mu = jnp.mean(h, axis=-1, keepdims=True)
    var = jnp.mean(jnp.square(h - mu), axis=-1, keepdims=True)
    return (h - mu) * lax.rsqrt(var + LN_EPS) * g + b


def _pack_bf16_halves(x):
    n = x.shape[-1] // 2
    bits = lax.bitcast_convert_type(x.astype(BF16).astype(F32), jnp.uint32)
    return (bits[:, n:] & jnp.uint32(0xFFFF0000)) | (bits[:, :n] >> 16)


def _unpack_bf16_halves(u):
    left = lax.bitcast_convert_type(u << 16, F32)
    right = lax.bitcast_convert_type(u & jnp.uint32(0xFFFF0000), F32)
    return left, right


def _mixer_out_kernel(x_ref, ya_ref, yb_ref, pg_ref, wa_ref, wb_ref, wo_ref, g_ref, b_ref, o_ref, op_ref,
                      *, alpha):
    d = x_ref.shape[-1]
    gate_a = _sigmoid(pg_ref[:, :d])
    gate_b = _sigmoid(pg_ref[:, d:])
    mixed = gate_a * _bdot(ya_ref[...], wa_ref[...]) + gate_b * _bdot(yb_ref[...], wb_ref[...])
    h = alpha * x_ref[...] + _bdot(mixed, wo_ref[...])
    out = _layer_norm(h, g_ref[...], b_ref[...])
    o_ref[...] = out
    op_ref[...] = _pack_bf16_halves(out)


def _mixer_out(xf, ya, yb, p_g, wa, wb, wo, ln_g, ln_b, alpha, tm=512):
    T, D = xf.shape
    rows = lambda w: pl.BlockSpec((tm, w), lambda i: (i, 0))
    full = lambda a: pl.BlockSpec(a.shape, lambda i: (0,) * a.ndim)
    ln_g, ln_b = ln_g.reshape(1, D), ln_b.reshape(1, D)
    return pl.pallas_call(
        functools.partial(_mixer_out_kernel, alpha=alpha),
        grid=(T // tm,),
        in_specs=[rows(D), rows(ya.shape[1]), rows(yb.shape[1]), rows(2 * D),
                  full(wa), full(wb), full(wo), full(ln_g), full(ln_b)],
        out_specs=[rows(D), rows(D // 2)],
        out_shape=[jax.ShapeDtypeStruct((T, D), F32), jax.ShapeDtypeStruct((T, D // 2), jnp.uint32)],
        compiler_params=pltpu.CompilerParams(
            dimension_semantics=("parallel",), vmem_limit_bytes=VMEM_LIMIT),
        name="mixer_out_ln",
    )(xf, ya, yb, p_g, wa, wb, wo, ln_g, ln_b)


ROUTER_TILE = 256
EXPERT_ROWS = 256
SC_TOKEN_CHUNK = 64
SC_ROW_CHUNK = 64
PICK_LANES = 128
LOWEST = -3.0e38


def _router_kernel(x_ref, rwt_ref, bias_ref, eidx_ref, wts_ref, pos_ref, cnt_ref, carry_ref):
    tm, E = x_ref.shape[0], rwt_ref.shape[0]
    per_group = E // N_GROUPS
    reps = tm // PICK_LANES

    @pl.when(pl.program_id(0) == 0)
    def _():
        carry_ref[...] = jnp.zeros_like(carry_ref)

    scores = _sigmoid(_dot3(rwt_ref[...], x_ref[...], (((1,), (1,)), ((), ()))))
    choice = scores + jnp.concatenate([bias_ref[...]] * reps, axis=1)
    row = lax.broadcasted_iota(jnp.int32, (E, tm), 0)

    def first_max(vals, rows):
        m = jnp.max(vals, axis=0, keepdims=True)
        return m, jnp.min(jnp.where(vals == m, rows, E), axis=0, keepdims=True)

    group_score = []
    for g in range(N_GROUPS):
        rows = slice(g * per_group, (g + 1) * per_group)
        group_row = g * per_group + lax.broadcasted_iota(jnp.int32, (per_group, tm), 0)
        m1, i1 = first_max(choice[rows], group_row)
        m2 = jnp.max(jnp.where(group_row == i1, LOWEST, choice[rows]), axis=0, keepdims=True)
        group_score.append(m1 + m2)
    masked = []
    for g in range(N_GROUPS):
        rank = jnp.zeros((1, tm), jnp.int32)
        for o in range(N_GROUPS):
            if o != g:
                ahead = (group_score[o] > group_score[g]) if o > g else (group_score[o] >= group_score[g])
                rank = rank + ahead.astype(jnp.int32)
        masked.append(jnp.where(rank < TOPK_GROUPS, choice[g * per_group:(g + 1) * per_group], NEG_INF))

    cur = jnp.concatenate(masked, axis=0)
    picks = []
    for _ in range(TOP_K):
        _, idx = first_max(cur, row)
        picks.append(idx)
        cur = jnp.where(row == idx, LOWEST, cur)
    sel = jnp.where(cur == LOWEST, 1.0, 0.0)
    gate = scores * sel
    gate = gate * (ROUTED_SCALE / jnp.sum(gate, axis=0, keepdims=True))

    ti = lax.broadcasted_iota(jnp.int32, (tm, tm), 0)
    tj = lax.broadcasted_iota(jnp.int32, (tm, tm), 1)
    sel_b = sel.astype(BF16)
    before = jnp.dot(sel_b, (ti < tj).astype(BF16), preferred_element_type=F32)
    queue_pos = before + jnp.concatenate([carry_ref[...]] * reps, axis=1)
    carry_ref[...] = carry_ref[...] + jnp.dot(sel_b, jnp.ones((tm, PICK_LANES), BF16),
                                              preferred_element_type=F32)
    cnt_ref[...] = carry_ref[...]

    at_pick = lambda vals, idx: jnp.sum(jnp.where(row == idx, vals, 0.0), axis=0, keepdims=True)
    eidx_ref[...] = jnp.concatenate(picks, axis=0)
    wts_ref[...] = jnp.concatenate([at_pick(gate, idx) for idx in picks], axis=0)
    pos_ref[...] = jnp.concatenate([at_pick(queue_pos, idx) for idx in picks], axis=0).astype(jnp.int32)


def _router(xf, router_w, router_bias):
    T, D = xf.shape
    E = router_w.shape[1]
    tm = ROUTER_TILE
    picks = lambda dt: jax.ShapeDtypeStruct((TOP_K, T), dt)
    pick_spec = pl.BlockSpec((TOP_K, tm), lambda i: (0, i))
    lanes = lambda v: jnp.broadcast_to(v.reshape(E, 1), (E, PICK_LANES))
    return pl.pallas_call(
        _router_kernel,
        grid=(T // tm,),
        in_specs=[pl.BlockSpec((tm, D), lambda i: (i, 0)),
                  pl.BlockSpec((E, D), lambda i: (0, 0)),
                  pl.BlockSpec((E, PICK_LANES), lambda i: (0, 0))],
        out_specs=[pick_spec, pick_spec, pick_spec, pl.BlockSpec((E, PICK_LANES), lambda i: (0, 0))],
        out_shape=[picks(jnp.int32), picks(F32), picks(jnp.int32),
                   jax.ShapeDtypeStruct((E, PICK_LANES), F32)],
        scratch_shapes=[pltpu.VMEM((E, PICK_LANES), F32)],
        compiler_params=pltpu.CompilerParams(
            dimension_semantics=("arbitrary",), vmem_limit_bytes=VMEM_LIMIT),
        name="moe_router",
    )(xf, router_w.T, lanes(router_bias))


def _dest_kernel(eidx_ref, pos_ref, start_ref, dest_ref):
    E = start_ref.shape[0]
    tm = eidx_ref.shape[1]
    row = lax.broadcasted_iota(jnp.int32, (E, tm), 0)
    start = jnp.concatenate([start_ref[...]] * (tm // PICK_LANES), axis=1)
    eidx = eidx_ref[...]
    base = [jnp.sum(jnp.where(row == eidx[kk:kk + 1, :], start, 0), axis=0, keepdims=True)
            for kk in range(TOP_K)]
    dest_ref[...] = jnp.concatenate(base, axis=0) + pos_ref[...]


def _dest_rows(eidx_t, pos_t, pad_start):
    T = eidx_t.shape[1]
    E = pad_start.shape[0]
    tm = ROUTER_TILE
    pick_spec = pl.BlockSpec((TOP_K, tm), lambda i: (0, i))
    return pl.pallas_call(
        _dest_kernel,
        grid=(T // tm,),
        in_specs=[pick_spec, pick_spec, pl.BlockSpec((E, PICK_LANES), lambda i: (0, 0))],
        out_specs=pick_spec,
        out_shape=jax.ShapeDtypeStruct((TOP_K, T), jnp.int32),
        compiler_params=pltpu.CompilerParams(
            dimension_semantics=("parallel",), vmem_limit_bytes=VMEM_LIMIT),
        name="moe_dest_rows",
    )(eidx_t, pos_t, jnp.broadcast_to(pad_start.reshape(E, 1), (E, PICK_LANES)))


def _sc_mesh():
    return plsc.VectorSubcoreMesh(core_axis_name="c", subcore_axis_name="s")


def _sc_scatter_rows(x, dest_t, n_rows):
    T, D = x.shape
    K = dest_t.shape[0]
    mesh = _sc_mesh()
    nc, nw = mesh.num_cores, mesh.num_cores * mesh.num_subcores
    per_w = T // nw
    chunk = min(SC_TOKEN_CHUNK, per_w)
    n_chunks = per_w // chunk
    idx = dest_t.reshape(K, nw, n_chunks, chunk).transpose(1, 2, 0, 3).reshape(nw, n_chunks * K, chunk)

    assert n_chunks % 2 == 0

    @functools.partial(
        pl.kernel, mesh=mesh,
        out_type=jax.ShapeDtypeStruct((n_rows, D), x.dtype),
        scratch_types=[pltpu.VMEM((n_chunks * K, chunk), jnp.int32),
                       pltpu.VMEM((2, chunk, D), x.dtype),
                       pltpu.SemaphoreType.DMA((2,)), pltpu.SemaphoreType.DMA((2,))],
    )
    def scatter(x_hbm, idx_hbm, out_hbm, idx_v, rows_v, load_sem, send_sem):
        wid = lax.axis_index("s") * nc + lax.axis_index("c")
        pltpu.sync_copy(idx_hbm.at[wid], idx_v)

        def load(j, b):
            return pltpu.make_async_copy(x_hbm.at[pl.ds(wid * per_w + j * chunk, chunk)], rows_v.at[b],
                                         load_sem.at[b])

        def sends(j, b):
            return [pltpu.make_async_copy(rows_v.at[b], out_hbm.at[idx_v.at[j * K + kk]], send_sem.at[b])
                    for kk in range(K)]

        load(0, 0).start()

        @pl.loop(0, n_chunks, step=2)
        def _(j0):
            for b in range(2):
                j = j0 + b
                load(j, b).wait()

                @pl.when(j >= 1)
                def _():
                    for c in sends(j - 1, 1 - b):
                        c.wait()

                @pl.when(j + 1 < n_chunks)
                def _():
                    load(j + 1, 1 - b).start()

                for c in sends(j, b):
                    c.start()

        for c in sends(n_chunks - 1, (n_chunks - 1) % 2):
            c.wait()

    return scatter(x, idx)


def _sc_gather_rows(src, idx):
    M = idx.shape[0]
    D = src.shape[1]
    mesh = _sc_mesh()
    nc, nw = mesh.num_cores, mesh.num_cores * mesh.num_subcores
    per_w = M // nw
    chunk = min(SC_ROW_CHUNK, per_w)
    n_chunks = per_w // chunk

    assert n_chunks % 2 == 0

    @functools.partial(
        pl.kernel, mesh=mesh,
        out_type=jax.ShapeDtypeStruct((M, D), src.dtype),
        scratch_types=[pltpu.VMEM((n_chunks, chunk), jnp.int32),
                       pltpu.VMEM((2, chunk, D), src.dtype),
                       pltpu.SemaphoreType.DMA((2,)), pltpu.SemaphoreType.DMA((2,))],
    )
    def gather(src_hbm, idx_hbm, out_hbm, idx_v, rows_v, fetch_sem, store_sem):
        wid = lax.axis_index("s") * nc + lax.axis_index("c")
        pltpu.sync_copy(idx_hbm.at[wid], idx_v)

        def fetch(j, b):
            return pltpu.make_async_copy(src_hbm.at[idx_v.at[j]], rows_v.at[b], fetch_sem.at[b])

        def store(j, b):
            return pltpu.make_async_copy(rows_v.at[b], out_hbm.at[pl.ds(wid * per_w + j * chunk, chunk)],
                                         store_sem.at[b])

        fetch(0, 0).start()

        @pl.loop(0, n_chunks, step=2)
        def _(j0):
            for b in range(2):
                j = j0 + b
                fetch(j, b).wait()

                @pl.when(j >= 1)
                def _():
                    store(j - 1, 1 - b).wait()

                @pl.when(j + 1 < n_chunks)
                def _():
                    fetch(j + 1, 1 - b).start()

                store(j, b).start()

        store(n_chunks - 1, (n_chunks - 1) % 2).wait()

    return gather(src, idx.reshape(nw, n_chunks, chunk))


def _expert_kernel(distinct_e_ref, blk_ord_ref, blk_new_ref, blk_rows_ref, n_used_ref, n_distinct_ref,
                   x_ref, wgu_hbm, wd_hbm, o_ref, wgu_buf, wd_buf, wgu_bf, wd_bf, sem):
    i = pl.program_id(0)
    live = i < n_used_ref[0]
    ordinal = blk_ord_ref[i]
    slot = ordinal % 2

    def weight_copies(k, s):
        e = distinct_e_ref[k]
        return (pltpu.make_async_copy(wgu_hbm.at[e], wgu_buf.at[s], sem.at[0, s]),
                pltpu.make_async_copy(wd_hbm.at[e], wd_buf.at[s], sem.at[1, s]))

    @pl.when(i == 0)
    def _():
        for c in weight_copies(0, 0):
            c.start()

    @pl.when(live & (blk_new_ref[i] == 1))
    def _():
        for c in weight_copies(ordinal, slot):
            c.wait()

        @pl.when(ordinal + 1 < n_distinct_ref[0])
        def _():
            for c in weight_copies(ordinal + 1, 1 - slot):
                c.start()

        wgu_bf[...] = wgu_buf[slot].astype(BF16)
        wd_bf[...] = wd_buf[slot].astype(BF16)

    @pl.when(live)
    def _():
        hidden = wd_bf.shape[0]
        half = x_ref.shape[1]
        row = lax.broadcasted_iota(jnp.int32, x_ref.shape, 0)
        left, right = _unpack_bf16_halves(x_ref[...])
        real = row < blk_rows_ref[i]
        left = jnp.where(real, left, 0.0).astype(BF16)
        right = jnp.where(real, right, 0.0).astype(BF16)
        h = (jnp.dot(left, wgu_bf[:half, :], preferred_element_type=F32)
             + jnp.dot(right, wgu_bf[half:, :], preferred_element_type=F32))
        gate, up = h[:, :hidden], h[:, hidden:]
        act = (gate * _sigmoid(gate) * up).astype(BF16)
        o_ref[...] = _pack_bf16_halves(jnp.dot(act, wd_bf[...], preferred_element_type=F32))

    @pl.when(jnp.logical_not(live))
    def _():
        o_ref[...] = jnp.zeros_like(o_ref)


def _expert_ffn(xs, blk_e, blk_rows, n_used, w_gu, w_down):
    n_rows, half = xs.shape
    E, D, two_h = w_gu.shape
    n_blocks = n_rows // EXPERT_ROWS
    idx = jnp.arange(n_blocks, dtype=jnp.int32)
    is_live = idx < n_used[0]
    blk_new = (is_live & ((idx == 0) | (blk_e != jnp.roll(blk_e, 1)))).astype(jnp.int32)
    blk_ord = (jnp.cumsum(blk_new) - 1).astype(jnp.int32)
    n_distinct = blk_ord[-1:] + 1
    distinct_e = jnp.zeros((n_blocks,), jnp.int32).at[blk_ord].max(blk_e * is_live)

    live = lambda i, nu: jnp.minimum(i, nu[0] - 1)
    grid_spec = pltpu.PrefetchScalarGridSpec(
        num_scalar_prefetch=6,
        grid=(n_blocks,),
        in_specs=[pl.BlockSpec((EXPERT_ROWS, half), lambda i, de, bo, bn, br, nu, nd: (live(i, nu), 0)),
                  pl.BlockSpec(memory_space=pl.ANY), pl.BlockSpec(memory_space=pl.ANY)],
        out_specs=pl.BlockSpec((EXPERT_ROWS, half), lambda i, de, bo, bn, br, nu, nd: (i, 0)),
        scratch_shapes=[pltpu.VMEM((2, D, two_h), F32), pltpu.VMEM((2, two_h // 2, D), F32),
                        pltpu.VMEM((D, two_h), BF16), pltpu.VMEM((two_h // 2, D), BF16),
                        pltpu.SemaphoreType.DMA((2, 2))],
    )
    return pl.pallas_call(
        _expert_kernel,
        grid_spec=grid_spec,
        out_shape=jax.ShapeDtypeStruct((n_rows, half), jnp.uint32),
        compiler_params=pltpu.CompilerParams(
            dimension_semantics=("arbitrary",), vmem_limit_bytes=VMEM_LIMIT),
        name="moe_experts",
    )(distinct_e, blk_ord, blk_new, blk_rows, n_used, n_distinct, xs, w_gu, w_down)


def _moe_out_kernel(x_ref, yk_ref, wts_ref, sgu_ref, sd_ref, g_ref, b_ref, o_ref, *, alpha):
    x = x_ref[...]
    hidden = sd_ref.shape[0]
    h = _bdot(x, sgu_ref[...])
    gate, up = h[:, :hidden], h[:, hidden:]
    ffn = _bdot(gate * _sigmoid(gate) * up, sd_ref[...])
    wts = wts_ref[...]
    routed_left = routed_right = None
    for kk in range(TOP_K):
        left, right = _unpack_bf16_halves(yk_ref[kk])
        w = wts[:, kk:kk + 1]
        routed_left = w * left if kk == 0 else routed_left + w * left
        routed_right = w * right if kk == 0 else routed_right + w * right
    ffn = ffn + jnp.concatenate([routed_left, routed_right], axis=-1)
    o_ref[...] = _layer_norm(alpha * x + ffn, g_ref[...], b_ref[...])


def _moe_out(xf, yk, wts, sw_gu, sw_down, ln_g, ln_b, alpha, tm=128):
    T, D = xf.shape
    rows = lambda w: pl.BlockSpec((tm, w), lambda i: (i, 0))
    full = lambda a: pl.BlockSpec(a.shape, lambda i: (0,) * a.ndim)
    ln_g, ln_b = ln_g.reshape(1, D), ln_b.reshape(1, D)
    return pl.pallas_call(
        functools.partial(_moe_out_kernel, alpha=alpha),
        grid=(T // tm,),
        in_specs=[rows(D), pl.BlockSpec((TOP_K, tm, D // 2), lambda i: (0, i, 0)), rows(PICK_LANES),
                  full(sw_gu), full(sw_down), full(ln_g), full(ln_b)],
        out_specs=rows(D),
        out_shape=jax.ShapeDtypeStruct((T, D), F32),
        compiler_params=pltpu.CompilerParams(
            dimension_semantics=("parallel",), vmem_limit_bytes=VMEM_LIMIT),
        name="moe_combine_ln",
    )(xf, yk, wts, sw_gu, sw_down, ln_g, ln_b)


def _moe_ffn_ln(xf, xp, router_w, router_bias, w_gu, w_down, sw_gu, sw_down, ln_g, ln_b, alpha):
    T, D = xf.shape
    E = router_w.shape[1]
    BM = EXPERT_ROWS
    eidx_t, wts_t, pos_t, cnt = _router(xf, router_w, router_bias)
    counts = cnt[:, 0].astype(jnp.int32)
    padded = (counts + BM - 1) // BM * BM
    pad_end = jnp.cumsum(padded)
    pad_start = pad_end - padded
    n_rows = T * TOP_K + E * BM
    n_blocks = n_rows // BM
    blk_row0 = jnp.arange(n_blocks, dtype=jnp.int32) * BM
    blk_e = jnp.minimum(jnp.sum((pad_end[None, :] <= blk_row0[:, None]).astype(jnp.int32), axis=1), E - 1)
    blk_rows = jnp.clip(pad_start[blk_e] + counts[blk_e] - blk_row0, 0, BM).astype(jnp.int32)
    n_used = (pad_end[-1:] // BM).astype(jnp.int32)
    dest_t = _dest_rows(eidx_t, pos_t, pad_start)
    wts = jnp.pad(wts_t.T, ((0, 0), (0, PICK_LANES - TOP_K)))
    xs = _sc_scatter_rows(xp, dest_t, n_rows)
    ys = _expert_ffn(xs, blk_e, blk_rows, n_used, w_gu, w_down)
    yk = _sc_gather_rows(ys, dest_t.reshape(-1)).reshape(TOP_K, T, D // 2)
    return _moe_out(xf, yk, wts, sw_gu, sw_down, ln_g, ln_b, alpha)


def kernel(x, w_in, tshift_mu, rwkv_w0, rwkv_w2, rwkv_a0, rwkv_a2, rwkv_g2, rwkv_k_k, rwkv_k_a, rwkv_r_k, rwkv_lnx_w, rwkv_lnx_b, cmp_pe_k, cmp_w1_k, cmp_w2_k, cmp_pe_v, cmp_w1_v, cmp_w2_v, w_branch_a, w_branch_b, w_out, ln1_g, ln1_b, router_w, router_bias, exp_w_gu, exp_w_down, shared_w_gu, shared_w_down, ln2_g, ln2_b):
    B, S, D = x.shape
    depth = w_in.shape[0]
    alpha = (2 * depth) ** 0.25
    nsa_w = w_in.shape[2] - RWKV_IN_W - 2 * D
    for l in range(depth):
        xf = x.reshape(B * S, D)
        w_l = w_in[l]
        w_a = w_l[:, :RWKV_IN_W].astype(BF16)
        w_b = _nsa_weight_columns(w_l[:, RWKV_IN_W:RWKV_IN_W + nsa_w]).astype(BF16)
        w_g = w_l[:, RWKV_IN_W + nsa_w:].astype(BF16)
        kv_w = 6 * NSA_KV_WIDTH
        p_a = _matmul(xf, w_a, PROJ_ROWS, w_a.shape[1]).reshape(B, S, -1)
        p_kv = _matmul(xf, w_b[:, :kv_w], PROJ_ROWS, kv_w).reshape(B, S, -1)
        qg_t = _matmul_t(x, w_b[:, kv_w:].T, PROJ_ROWS)
        p_g = _matmul(xf, w_g, PROJ_ROWS, w_g.shape[1])
        y_a = _rwkv_time_mix(p_a, tshift_mu[l], rwkv_w0[l], rwkv_w2[l], rwkv_a0[l], rwkv_a2[l], rwkv_g2[l],
                             rwkv_k_k[l], rwkv_k_a[l], rwkv_r_k[l].reshape(-1), rwkv_lnx_w[l], rwkv_lnx_b[l])
        y_b = _nsa_branch(p_kv, qg_t, cmp_pe_k[l], cmp_w1_k[l], cmp_w2_k[l],
                          cmp_pe_v[l], cmp_w1_v[l], cmp_w2_v[l])
        x1, x1p = _mixer_out(xf, y_a.reshape(B * S, -1), y_b.reshape(B * S, -1), p_g,
                             w_branch_a[l].astype(BF16), w_branch_b[l].astype(BF16), w_out[l].astype(BF16),
                             ln1_g[l], ln1_b[l], alpha)
        x2 = _moe_ffn_ln(x1, x1p, router_w[l], router_bias[l], exp_w_gu[l], exp_w_down[l],
                         shared_w_gu[l].astype(BF16), shared_w_down[l].astype(BF16), ln2_g[l], ln2_b[l], alpha)
        x = x2.reshape(B, S, D)
    return x
```

```python
import functools

import numpy as np
import jax
import jax.numpy as jnp
from jax import lax
from jax.experimental import pallas as pl
from jax.experimental.pallas import tpu as pltpu
from jax.experimental.pallas import tpu_sc as plsc

F32 = jnp.float32
BF16 = jnp.bfloat16

RWKV_HEADS = 8
HEAD_DIM = 64
RWKV_WIDTH = RWKV_HEADS * HEAD_DIM
W_LORA = 64
A_LORA = 64
G_LORA = 128
GN_EPS = 64e-5
NSA_HEADS = 8
NSA_GROUPS = 2
NSA_HPG = NSA_HEADS // NSA_GROUPS
NSA_WIDTH = NSA_HEADS * HEAD_DIM
NSA_KV_WIDTH = NSA_GROUPS * HEAD_DIM
CMP_BLOCK = 32
CMP_STRIDE = 16
CMP_HIDDEN = 256
SEL_BLOCK = 64
N_SELECT = 16
WINDOW = 512
ROPE_THETA = 10000.0
RWKV_IN_W = 3 * RWKV_WIDTH + W_LORA + A_LORA + G_LORA
N_EXPERTS = 256
TOP_K = 8
N_GROUPS = 8
TOPK_GROUPS = 4
EXPERT_DIM = 256
ROUTED_SCALE = 2.5
LN_EPS = 1e-5
NEG_INF = -1e30
FORCE_BONUS = 1e4

RWKV_CHUNK = 64
RWKV_HEAD_GROUP = 4
RWKV_STEP_CHUNKS = 2
VMEM_LIMIT = 56 * 1024 * 1024
PROJ_ROWS = 512


def _bdot(a, b):
    return jnp.dot(a.astype(BF16), b.astype(BF16), preferred_element_type=F32)


def _bdot_nt(a, b):
    return lax.dot_general(a.astype(BF16), b.astype(BF16), (((1,), (1,)), ((), ())),
                           preferred_element_type=F32)


def _bdot_tn(a, b):
    return lax.dot_general(a.astype(BF16), b.astype(BF16), (((0,), (0,)), ((), ())),
                           preferred_element_type=F32)


def _bf16_pieces(x, n):
    pieces = []
    for _ in range(n):
        p = x.astype(BF16)
        pieces.append(p)
        x = x - p.astype(F32)
    return pieces


def _dot3(a, b, dims=(((1,), (0,)), ((), ()))):
    (a_hi, a_lo), (b_hi, b_lo) = _bf16_pieces(a, 2), _bf16_pieces(b, 2)
    dot = lambda p, q: lax.dot_general(p, q, dims, preferred_element_type=F32)
    return dot(a_hi, b_hi) + (dot(a_hi, b_lo) + dot(a_lo, b_hi))


def _sigmoid(x):
    return 1.0 / (1.0 + jnp.exp(-x))


def _matmul_kernel(x_ref, w_ref, o_ref):
    o_ref[...] = jnp.dot(x_ref[...].astype(BF16), w_ref[...], preferred_element_type=F32)


def _matmul(x, w, tm, tn):
    M, K = x.shape
    N = w.shape[1]
    return pl.pallas_call(
        _matmul_kernel,
        grid=(M // tm, N // tn),
        in_specs=[pl.BlockSpec((tm, K), lambda i, j: (i, 0)),
                  pl.BlockSpec((K, tn), lambda i, j: (0, j))],
        out_specs=pl.BlockSpec((tm, tn), lambda i, j: (i, j)),
        out_shape=jax.ShapeDtypeStruct((M, N), F32),
        compiler_params=pltpu.CompilerParams(
            dimension_semantics=("parallel", "parallel"), vmem_limit_bytes=VMEM_LIMIT),
        name="dense_proj",
    )(x, w)


def _matmul_t_kernel(x_ref, wt_ref, o_ref):
    o_ref[...] = lax.dot_general(wt_ref[...], x_ref[...].astype(BF16), (((1,), (1,)), ((), ())),
                                 preferred_element_type=F32)


def _matmul_t(x, w_t, tm):
    B, S, K = x.shape
    N = w_t.shape[0]
    return pl.pallas_call(
        _matmul_t_kernel,
        grid=(B, S // tm),
        in_specs=[pl.BlockSpec((None, tm, K), lambda b, s: (b, s, 0)),
                  pl.BlockSpec((N, K), lambda b, s: (0, 0))],
        out_specs=pl.BlockSpec((None, N, tm), lambda b, s: (b, 0, s)),
        out_shape=jax.ShapeDtypeStruct((B, N, S), F32),
        compiler_params=pltpu.CompilerParams(
            dimension_semantics=("parallel", "parallel"), vmem_limit_bytes=VMEM_LIMIT),
        name="dense_proj_t",
    )(x, w_t)


def _rwkv_kernel(p_ref, mu_ref, w0_ref, w2_ref, a0_ref, a2_ref, g2_ref, kk_ref, ka_ref, rk_ref,
                 lnw_ref, lnb_ref, o_ref, carry_ref, state_ref):
    C, H, N = RWKV_CHUNK, RWKV_HEADS, HEAD_DIM
    W = RWKV_WIDTH
    B = p_ref.shape[0]
    NC = p_ref.shape[1] // C
    L = NC * C
    R = B * L

    @pl.when(pl.program_id(0) == 0)
    def _():
        carry_ref[...] = jnp.zeros_like(carry_ref)
        state_ref[...] = jnp.zeros_like(state_ref)

    def per_block(x, rows):
        return jnp.concatenate(
            [jnp.broadcast_to(x[i].reshape(1, -1), (rows, x.shape[-1])) for i in range(x.shape[0])], axis=0)

    p = p_ref[...].reshape(R, p_ref.shape[-1])
    row = lax.broadcasted_iota(jnp.int32, p.shape, 0)
    prev = jnp.where(row % L == 0, per_block(carry_ref[...], L), pltpu.roll(p, 1, axis=0))
    for b in range(B):
        carry_ref[b] = p[b * L + L - 1:b * L + L, :]
    xs = p + (prev - p) * mu_ref[...]
    r = xs[:, 0:W]
    k = xs[:, W:2 * W]
    v = xs[:, 2 * W:3 * W]
    wl = xs[:, 3 * W:3 * W + W_LORA]
    al = xs[:, 3 * W + W_LORA:3 * W + W_LORA + A_LORA]
    gl = xs[:, 3 * W + W_LORA + A_LORA:]

    z = -(w0_ref[...] + _dot3(jnp.tanh(wl), w2_ref[...]))
    softplus = jnp.maximum(z, 0.0) + jnp.log1p(jnp.exp(-jnp.abs(z)))
    logd = -jnp.exp(-softplus - 0.5)
    a = _sigmoid(a0_ref[...] + _dot3(al, a2_ref[...]))
    g = _dot3(_sigmoid(gl), g2_ref[...])

    kk = k * kk_ref[...]
    knew = k * (1.0 + (a - 1.0) * ka_ref[...])

    HG = RWKV_HEAD_GROUP
    GW = HG * N
    same_head_lanes = (lax.broadcasted_iota(jnp.int32, (GW, GW), 0) // N
                       == lax.broadcasted_iota(jnp.int32, (GW, GW), 1) // N)
    head_ones = jnp.where(same_head_lanes, 1.0, 0.0).astype(BF16)

    def head_sum(x):
        hi = x.astype(BF16)
        lo = (x - hi.astype(F32)).astype(BF16)
        return jnp.concatenate(
            [jnp.dot(hi[:, s:s + GW], head_ones, preferred_element_type=F32)
             + jnp.dot(lo[:, s:s + GW], head_ones, preferred_element_type=F32) for s in range(0, W, GW)],
            axis=-1)

    kk = kk / jnp.maximum(jnp.sqrt(head_sum(kk * kk)), 1e-12)
    lr_kk = kk * a

    ti = lax.broadcasted_iota(jnp.int32, (R, R), 0)
    tj = lax.broadcasted_iota(jnp.int32, (R, R), 1)
    same_chunk = (ti >= tj) & (ti // C == tj // C)
    tri = same_chunk.astype(BF16)
    cl = sum(jnp.dot(tri, piece, preferred_element_type=F32) for piece in reversed(_bf16_pieces(logd, 3)))
    cl_end = per_block(jnp.concatenate([cl[i * C + C - 1:i * C + C, :] for i in range(B * NC)], axis=0), C)
    a_hat = -kk * jnp.exp(cl - logd)
    r_hat = r * jnp.exp(cl)
    inv_gam = jnp.exp(-cl)
    b_til = lr_kk * inv_gam
    k_til = knew * inv_gam
    to_end = jnp.exp(cl_end - cl)
    b_end = lr_kk * to_end
    k_end = knew * to_end
    gam_end = jnp.exp(cl_end)

    gt = lax.broadcasted_iota(jnp.int32, (C, GW), 0)
    gc = lax.broadcasted_iota(jnp.int32, (C, GW), 1) % N
    strict = gt > gc
    incl = gt >= gc
    eye = (gt == gc).astype(F32)
    bi = lax.broadcasted_iota(jnp.int32, (HG * C, GW), 0) // C
    bj = lax.broadcasted_iota(jnp.int32, (HG * C, GW), 1) // N
    same_head = bi == bj

    def block_diag(y):
        yb = y.astype(BF16)
        return jnp.where(same_head, jnp.concatenate([yb] * HG, axis=0), jnp.zeros((), BF16))

    def bd_dot(x, y_bd):
        return jnp.dot(x.astype(BF16), y_bd, preferred_element_type=F32)

    def bd_dot_nt(x, y_bd):
        return lax.dot_general(x.astype(BF16), y_bd, (((1,), (1,)), ((), ())), preferred_element_type=F32)

    n_groups = H // HG
    units = [(b, c, gi) for b in range(B) for c in range(NC) for gi in range(n_groups)]
    n_units = range(len(units))
    cut = lambda x, b, c, gi: x[(b * NC + c) * C:(b * NC + c + 1) * C, gi * GW:(gi + 1) * GW]
    v_u = [cut(v, *un) for un in units]
    v_bd = [block_diag(v_u[i]) for i in n_units]
    ar = [jnp.concatenate([cut(a_hat, *un), cut(r_hat, *un)], axis=0) for un in units]
    mb = [bd_dot_nt(ar[i], block_diag(cut(b_til, *units[i]))) for i in n_units]
    mk = [bd_dot_nt(ar[i], block_diag(cut(k_til, *units[i]))) for i in n_units]
    n_ab = [jnp.where(strict, mb[i][:C], 0.0) for i in n_units]
    m_rb = [jnp.where(incl, mb[i][C:], 0.0) for i in n_units]
    l_ak = [jnp.where(strict, mk[i][:C], 0.0) for i in n_units]
    m_rk = [jnp.where(incl, mk[i][C:], 0.0) for i in n_units]

    pw = list(n_ab)
    pw_bd = [block_diag(pw[i]) for i in n_units]
    tinv = [eye + n_ab[i] for i in n_units]
    step = 2
    while step < C:
        pw = [bd_dot(pw[i], pw_bd[i]) for i in n_units]
        pw_bd = [block_diag(pw[i]) for i in n_units]
        tinv = [tinv[i] + bd_dot(tinv[i], pw_bd[i]) for i in n_units]
        step *= 2
    lv = [bd_dot(l_ak[i], v_bd[i]) for i in n_units]

    state = {(b, gi): state_ref[b * n_groups + gi] for b in range(B) for gi in range(n_groups)}
    outs = {}
    for c in range(NC):
        live = [i for i in n_units if units[i][1] == c]
        s0 = {i: state[(units[i][0], units[i][2])] for i in live}
        ars = {i: bd_dot_nt(ar[i], block_diag(s0[i])) for i in live}
        u = {i: bd_dot(tinv[i], block_diag(ars[i][:C] + lv[i])) for i in live}
        for i in live:
            outs[units[i]] = ars[i][C:] + bd_dot(m_rb[i], block_diag(u[i])) + bd_dot(m_rk[i], v_bd[i])
        for i in live:
            b, _, gi = units[i]
            uv = jnp.concatenate([u[i], v_u[i]], axis=0)
            bk_end = jnp.concatenate([cut(b_end, *units[i]), cut(k_end, *units[i])], axis=0)
            cross = jnp.where(same_head, _bdot_tn(uv, bk_end), 0.0)
            upd = cross[0:N]
            for h in range(1, HG):
                upd = upd + cross[h * N:(h + 1) * N]
            state[(b, gi)] = s0[i] * cut(gam_end, *units[i])[0:1] + upd
    for (b, gi), s_new in state.items():
        state_ref[b * n_groups + gi] = s_new

    o = jnp.concatenate([jnp.concatenate([outs[(b, c, gi)] for gi in range(n_groups)], axis=-1)
                         for b in range(B) for c in range(NC)], axis=0)
    mean = head_sum(o) * (1.0 / N)
    var = head_sum(jnp.square(o - mean)) * (1.0 / N)
    o = (o - mean) * lax.rsqrt(var + GN_EPS) * lnw_ref[...] + lnb_ref[...]
    bonus = head_sum(r * knew * rk_ref[...]) * v
    o_ref[...] = ((o + bonus) * g).reshape(o_ref.shape)


def _rwkv_time_mix(p_a, mu, w0, w2, a0, a2, g2, k_k, k_a, r_k, lnx_w, lnx_b):
    B, S, _ = p_a.shape
    L = RWKV_CHUNK * RWKV_STEP_CHUNKS
    row = lambda t: t.reshape(1, -1)
    full = lambda shape: pl.BlockSpec(shape, lambda s: (0,) * len(shape))
    n_units = B * RWKV_HEADS // RWKV_HEAD_GROUP
    return pl.pallas_call(
        _rwkv_kernel,
        grid=(S // L,),
        in_specs=[pl.BlockSpec((B, L, RWKV_IN_W), lambda s: (0, s, 0)),
                  full((1, RWKV_IN_W)), full((1, RWKV_WIDTH)), full((W_LORA, RWKV_WIDTH)),
                  full((1, RWKV_WIDTH)), full((A_LORA, RWKV_WIDTH)), full((G_LORA, RWKV_WIDTH)),
                  full((1, RWKV_WIDTH)), full((1, RWKV_WIDTH)), full((1, RWKV_WIDTH)),
                  full((1, RWKV_WIDTH)), full((1, RWKV_WIDTH))],
        out_specs=pl.BlockSpec((B, L, RWKV_WIDTH), lambda s: (0, s, 0)),
        out_shape=jax.ShapeDtypeStruct((B, S, RWKV_WIDTH), F32),
        scratch_shapes=[pltpu.VMEM((B, 1, RWKV_IN_W), F32),
                        pltpu.VMEM((n_units, HEAD_DIM, RWKV_HEAD_GROUP * HEAD_DIM), F32)],
        compiler_params=pltpu.CompilerParams(
            dimension_semantics=("arbitrary",), vmem_limit_bytes=VMEM_LIMIT),
        name="rwkv7_chunked",
    )(p_a, row(mu), row(w0), w2, row(a0), a2, g2, row(k_k), row(k_a), row(r_k), row(lnx_w), row(lnx_b))


NSA_KV_TILE = 1024
SEL_KEY_TILE = 1024
NSA_QUERY_TILE = 256
SEL_LANES = 128


def _rope_tables(pos, reps):
    half = HEAD_DIM // 2
    inv = ROPE_THETA ** (-jnp.arange(half, dtype=F32) / half)
    ang = pos.astype(F32)[:, None] * inv
    cos, sin = jnp.cos(ang), jnp.sin(ang)
    cosf = jnp.concatenate([cos, cos], -1)
    sinf = jnp.concatenate([-sin, sin], -1)
    return jnp.tile(cosf, (1, reps)), jnp.tile(sinf, (1, reps))


def _rope(x, cosf, sinf):
    width = x.shape[-1]
    lane = lax.broadcasted_iota(jnp.int32, x.shape, 1)
    first_half = (lane % HEAD_DIM) < HEAD_DIM // 2
    rot = jnp.where(first_half, pltpu.roll(x, width - HEAD_DIM // 2, axis=1),
                    pltpu.roll(x, HEAD_DIM // 2, axis=1))
    return x * cosf + rot * sinf


def _kv_layout_kernel(p_ref, cos_ref, sin_ref, kc_ref, vc_ref, ks_ref, vs_ref, kw_ref, vw_ref):
    ts = p_ref.shape[0]
    for i, o_ref in ((0, kc_ref), (1, vc_ref), (2, ks_ref), (4, kw_ref)):
        t = p_ref[:, i * NSA_KV_WIDTH:(i + 1) * NSA_KV_WIDTH]
        if i >= 2:
            t = _rope(t, cos_ref[...], sin_ref[...])
        for g in range(NSA_GROUPS):
            o_ref[g] = t[:, g * HEAD_DIM:(g + 1) * HEAD_DIM].astype(o_ref.dtype)
    pad_row = lax.broadcasted_iota(jnp.int32, (VT_ROWS - HEAD_DIM, ts), 0)
    ones_row = jnp.where(pad_row == 0, 1.0, 0.0)
    for i, o_ref in ((3, vs_ref), (5, vw_ref)):
        t_t = p_ref[:, i * NSA_KV_WIDTH:(i + 1) * NSA_KV_WIDTH].T
        for g in range(NSA_GROUPS):
            o_ref[g] = jnp.concatenate([t_t[g * HEAD_DIM:(g + 1) * HEAD_DIM], ones_row],
                                       axis=0).astype(o_ref.dtype)


def _kv_layout(p_b, cos2, sin2):
    B, S, _ = p_b.shape
    ts = min(NSA_KV_TILE, S)
    out_spec = pl.BlockSpec((None, NSA_GROUPS, ts, HEAD_DIM), lambda b, s: (b, 0, s, 0))
    vt_spec = pl.BlockSpec((None, NSA_GROUPS, VT_ROWS, ts), lambda b, s: (b, 0, 0, s))
    shp = lambda dt: jax.ShapeDtypeStruct((B, NSA_GROUPS, S, HEAD_DIM), dt)
    vt_shp = jax.ShapeDtypeStruct((B, NSA_GROUPS, VT_ROWS, S), BF16)
    return pl.pallas_call(
        _kv_layout_kernel,
        grid=(B, S // ts),
        in_specs=[pl.BlockSpec((None, ts, 6 * NSA_KV_WIDTH), lambda b, s: (b, s, 0)),
                  pl.BlockSpec((ts, NSA_KV_WIDTH), lambda b, s: (s, 0)),
                  pl.BlockSpec((ts, NSA_KV_WIDTH), lambda b, s: (s, 0))],
        out_specs=[out_spec, out_spec, out_spec, vt_spec, out_spec, vt_spec],
        out_shape=[shp(F32), shp(F32), shp(BF16), vt_shp, shp(BF16), vt_shp],
        compiler_params=pltpu.CompilerParams(
            dimension_semantics=("parallel", "parallel"), vmem_limit_bytes=VMEM_LIMIT),
        name="nsa_kv_layout",
    )(p_b, cos2, sin2)


def _compress_kernel(subk_ref, subv_ref, pek_ref, w1k_ref, w2k_ref, pev_ref, w1v_ref, w2v_ref,
                     cos_ref, sin_ref, kc_ref, vc_ref):
    n_sub = subk_ref.shape[0]
    half = CMP_STRIDE * HEAD_DIM

    def mlp(sub_ref, pe_ref, w1_ref, w2_ref):
        sub = sub_ref[...]
        top = _bdot(sub, w1_ref[:half, :])
        bot = _bdot(sub, w1_ref[half:, :])
        bias = _bdot(jnp.broadcast_to(pe_ref[...], (8, 2 * half)), w1_ref[...])[0:1, :]
        h = top + pltpu.roll(bot, n_sub - 1, axis=0) + bias
        return _bdot(jax.nn.gelu(h), w2_ref[...])

    kc = mlp(subk_ref, pek_ref, w1k_ref, w2k_ref)
    rot = jnp.concatenate([kc[:, HEAD_DIM // 2:], kc[:, :HEAD_DIM // 2]], axis=-1)
    kc_ref[...] = (kc * cos_ref[...] + rot * sin_ref[...]).astype(kc_ref.dtype)
    vc_ref[...] = mlp(subv_ref, pev_ref, w1v_ref, w2v_ref).astype(vc_ref.dtype)


def _compress(subk, subv, pe_k, w1_k, w2_k, pe_v, w1_v, w2_v, cos_c, sin_c):
    B, G, n_sub, width = subk.shape
    sub_spec = pl.BlockSpec((None, None, n_sub, width), lambda b, g: (b, g, 0, 0))
    full = lambda a: pl.BlockSpec(a.shape, lambda b, g: (0,) * a.ndim)
    out_spec = pl.BlockSpec((None, None, n_sub, HEAD_DIM), lambda b, g: (b, g, 0, 0))
    pe_k, pe_v = pe_k.reshape(1, -1), pe_v.reshape(1, -1)
    args = (pe_k, w1_k, w2_k, pe_v, w1_v, w2_v, cos_c, sin_c)
    return pl.pallas_call(
        _compress_kernel,
        grid=(B, G),
        in_specs=[sub_spec, sub_spec] + [full(a) for a in args],
        out_specs=[out_spec, out_spec],
        out_shape=[jax.ShapeDtypeStruct((B, G, n_sub, HEAD_DIM), BF16)] * 2,
        compiler_params=pltpu.CompilerParams(
            dimension_semantics=("parallel", "parallel"), vmem_limit_bytes=VMEM_LIMIT),
        name="nsa_compress",
    )(subk, subv, *args)


MAX_FLOOR = -1e20
MASK_BIG = 2.0 ** 100
LOG2_E = 1.4426950408889634
VT_ROWS = 80


def _nsa_kernel(q_ref, gate_ref, cos_ref, sin_ref, kc_ref, vc_ref, ks_ref, vst_ref, kw_ref, vwt_ref,
                mselt_ref, o_ref, blockbias_ref, *, n_pick):
    QB, HP, D = NSA_QUERY_TILE, NSA_HPG, HEAD_DIM
    qb = pl.program_id(2)
    n_cmp = kc_ref.shape[0]
    lanes4 = lambda x: jnp.concatenate([x] * HP, axis=1)

    heads = []
    for n in range(HP):
        qh = q_ref[n * D:(n + 1) * D, :]
        rot = jnp.concatenate([qh[D // 2:], qh[:D // 2]], axis=0)
        heads.append(qh * cos_ref[...] + rot * sin_ref[...])
    q4 = (jnp.concatenate(heads, axis=1) * (D ** -0.5 * LOG2_E)).astype(BF16)
    t_row = qb * QB + lax.broadcasted_iota(jnp.int32, (1, QB), 1)

    def softmax_cols(s_t, bias_t):
        sm = s_t + lanes4(bias_t)
        m = jnp.maximum(jnp.max(sm, axis=0, keepdims=True), MAX_FLOOR)
        return jnp.exp2(sm - m)

    cmp_end = lax.broadcasted_iota(jnp.int32, (n_cmp, 1), 0) * CMP_STRIDE + (CMP_BLOCK - 1)
    e_c = softmax_cols(jnp.dot(kc_ref[...], q4, preferred_element_type=F32),
                       jnp.where(cmp_end <= t_row, 0.0, -MASK_BIG))
    den_c = jnp.sum(e_c, axis=0, keepdims=True)
    p_c = e_c * (1.0 / jnp.where(den_c > 0.0, den_c, 1.0))
    o_c = _bdot_tn(vc_ref[...], p_c)
    p_sum = p_c[:, 0:QB]
    for n in range(1, HP):
        p_sum = p_sum + p_c[:, n * QB:(n + 1) * QB]
    p_hi = p_sum.astype(BF16)
    p_lo = (p_sum - p_hi.astype(F32)).astype(BF16)
    imp_t = (jnp.dot(mselt_ref[...], p_hi, preferred_element_type=F32)
             + jnp.dot(mselt_ref[...], p_lo, preferred_element_type=F32))

    j = lax.broadcasted_iota(jnp.int32, (SEL_LANES, QB), 0)
    cur = t_row // SEL_BLOCK
    valid = j * SEL_BLOCK <= t_row
    forced = (j == 0) | (j == cur) | (j == cur - 1)
    score = jnp.where(valid, imp_t + jnp.where(forced, FORCE_BONUS, 0.0), -1.0)
    for _ in range(n_pick):
        m = jnp.max(score, axis=0, keepdims=True)
        idx = jnp.min(jnp.where(score == m, j, SEL_LANES), axis=0, keepdims=True)
        score = jnp.where(j == idx, -2.0, score)
    blockbias_ref[...] = jnp.where((score == -2.0) & valid, 0.0, -MASK_BIG)

    KT = SEL_KEY_TILE
    blocks_per_tile = KT // SEL_BLOCK
    n_tiles = (qb * QB + QB + KT - 1) // KT

    def sel_step(kt, carry, causal):
        m_i, acc = carry
        start = pl.multiple_of(kt * KT, KT)
        s_t = jnp.dot(ks_ref[pl.ds(start, KT), :], q4, preferred_element_type=F32)
        bias = jnp.concatenate(
            [jnp.broadcast_to(blockbias_ref[pl.ds(kt * blocks_per_tile + jb, 1), :], (SEL_BLOCK, QB))
             for jb in range(blocks_per_tile)], axis=0)
        if causal:
            kpos = start + lax.broadcasted_iota(jnp.int32, (KT, 1), 0)
            bias = jnp.where(kpos <= t_row, bias, -MASK_BIG)
        sm = s_t + lanes4(bias)
        m_new = jnp.maximum(m_i, jnp.max(sm, axis=0, keepdims=True))
        e = jnp.exp2(sm - m_new).astype(BF16)
        acc_new = jnp.exp2(m_i - m_new) * acc + jnp.dot(vst_ref[:, pl.ds(start, KT)], e,
                                                        preferred_element_type=F32)
        return m_new, acc_new

    init = (jnp.full((1, HP * QB), MAX_FLOOR, F32), jnp.zeros((VT_ROWS, HP * QB), F32))
    carry = lax.fori_loop(0, n_tiles - 1, lambda kt, c: sel_step(kt, c, False), init)
    _, acc_s = sel_step(n_tiles - 1, carry, True)
    den_s = acc_s[D:D + 1]
    o_s = acc_s[:D] * (1.0 / jnp.where(den_s > 0.0, den_s, 1.0))

    span = WINDOW + QB
    w_start = pl.multiple_of(jnp.maximum(qb * QB - WINDOW, 0), QB)
    dist = t_row - (w_start + lax.broadcasted_iota(jnp.int32, (span, 1), 0))
    e_w = softmax_cols(jnp.dot(kw_ref[pl.ds(w_start, span), :], q4, preferred_element_type=F32),
                       jnp.where((dist >= 0) & (dist < WINDOW), 0.0, -MASK_BIG))
    acc_w = jnp.dot(vwt_ref[:, pl.ds(w_start, span)], e_w.astype(BF16), preferred_element_type=F32)
    den_w = acc_w[D:D + 1]
    o_w = acc_w[:D] * (1.0 / jnp.where(den_w > 0.0, den_w, 1.0))

    gates = _sigmoid(gate_ref[...])
    gate_row = lambda br: jnp.concatenate([gates[3 * n + br:3 * n + br + 1, :] for n in range(HP)], axis=1)
    o_t = gate_row(0) * o_c + gate_row(1) * o_s + gate_row(2) * o_w
    for n in range(HP):
        o_ref[:, n * D:(n + 1) * D] = o_t[:, n * QB:(n + 1) * QB].T


def _cmp_to_sel_matrix(n_cmp_rows, n_sel):
    ratio = SEL_BLOCK // CMP_STRIDE
    ci = np.arange(n_cmp_rows)[:, None]
    sj = np.arange(SEL_LANES)[None, :]
    m = sum(((ci + n) // ratio == sj).astype(np.float32) for n in range(CMP_BLOCK // CMP_STRIDE))
    m = m * (sj < n_sel) * (ci < n_cmp_rows - 1)
    return jnp.asarray(m.T, BF16)


def _nsa_attention(qg_t, kc, vc, ks, vst, kw, vwt, cos_t, sin_t):
    B, _, S = qg_t.shape
    n_sub = kc.shape[2]
    n_sel = S // SEL_BLOCK
    gw = NSA_HPG * HEAD_DIM
    gate_row0 = NSA_WIDTH // 128
    msel_t = _cmp_to_sel_matrix(n_sub, n_sel)
    at_bg = lambda shape: pl.BlockSpec((None, None) + shape, lambda b, g, i: (b, g, 0, 0))
    const = lambda a: pl.BlockSpec(a.shape, lambda b, g, i: (0, 0))
    return pl.pallas_call(
        functools.partial(_nsa_kernel, n_pick=min(N_SELECT, n_sel)),
        grid=(B, NSA_GROUPS, S // NSA_QUERY_TILE),
        in_specs=[pl.BlockSpec((None, gw, NSA_QUERY_TILE), lambda b, g, i: (b, g, i)),
                  pl.BlockSpec((None, 128, NSA_QUERY_TILE), lambda b, g, i: (b, gate_row0 + g, i)),
                  pl.BlockSpec((HEAD_DIM, NSA_QUERY_TILE), lambda b, g, i: (0, i)),
                  pl.BlockSpec((HEAD_DIM, NSA_QUERY_TILE), lambda b, g, i: (0, i)),
                  at_bg((n_sub, HEAD_DIM)), at_bg((n_sub, HEAD_DIM)),
                  at_bg((S, HEAD_DIM)), at_bg((VT_ROWS, S)), at_bg((S, HEAD_DIM)), at_bg((VT_ROWS, S)),
                  const(msel_t)],
        out_specs=pl.BlockSpec((None, NSA_QUERY_TILE, gw), lambda b, g, i: (b, i, g)),
        out_shape=jax.ShapeDtypeStruct((B, S, NSA_WIDTH), F32),
        scratch_shapes=[pltpu.VMEM((SEL_LANES, NSA_QUERY_TILE), F32)],
        compiler_params=pltpu.CompilerParams(
            dimension_semantics=("parallel", "parallel", "arbitrary"), vmem_limit_bytes=VMEM_LIMIT),
        name="nsa_attention",
    )(qg_t, qg_t, cos_t, sin_t, kc, vc, ks, vst, kw, vwt, msel_t)


def _nsa_branch(p_kv, qg_t, cmp_pe_k, cmp_w1_k, cmp_w2_k, cmp_pe_v, cmp_w1_v, cmp_w2_v):
    B, S, _ = p_kv.shape
    pos = jnp.arange(S)
    cos2, sin2 = _rope_tables(pos, NSA_GROUPS)
    kc_raw, vc_raw, ks, vst, kw, vwt = _kv_layout(p_kv, cos2, sin2)
    n_sub = S // CMP_STRIDE
    sub = lambda t: t.reshape(B, NSA_GROUPS, n_sub, CMP_STRIDE * HEAD_DIM)
    cos_c, sin_c = _rope_tables(jnp.arange(n_sub) * CMP_STRIDE + CMP_BLOCK - 1, 1)
    kc, vc = _compress(sub(kc_raw), sub(vc_raw), cmp_pe_k, cmp_w1_k, cmp_w2_k,
                       cmp_pe_v, cmp_w1_v, cmp_w2_v, cos_c, sin_c)
    cos_q, sin_q = _rope_tables(pos, 1)
    return _nsa_attention(qg_t, kc, vc, ks, vst, kw, vwt, cos_q.T, sin_q.T)


def _nsa_weight_columns(w_nsa):
    K = w_nsa.shape[0]
    q = w_nsa[:, :NSA_WIDTH]
    kv = w_nsa[:, NSA_WIDTH:NSA_WIDTH + 6 * NSA_KV_WIDTH]
    gates = w_nsa[:, NSA_WIDTH + 6 * NSA_KV_WIDTH:]
    per_group = NSA_HPG * 3
    gate_blocks = [jnp.pad(gates[:, g * per_group:(g + 1) * per_group], ((0, 0), (0, 128 - per_group)))
                   for g in range(NSA_GROUPS)]
    return jnp.concatenate([kv, q] + gate_blocks, axis=1)


def _layer_norm(h, g, b):
    mu = jnp.mean(h, axis=-1, keepdims=True)
    var = jnp.mean(jnp.square(h - mu), axis=-1, keepdims=True)
    return (h - mu) * lax.rsqrt(var + LN_EPS) * g + b


def _pack_bf16_halves(x):
    n = x.shape[-1] // 2
    bits = lax.bitcast_convert_type(x.astype(BF16).astype(F32), jnp.uint32)
    return (bits[:, n:] & jnp.uint32(0xFFFF0000)) | (bits[:, :n] >> 16)


def _unpack_bf16_halves(u):
    left = lax.bitcast_convert_type(u << 16, F32)
    right = lax.bitcast_convert_type(u & jnp.uint32(0xFFFF0000), F32)
    return left, right


def _mixer_out_kernel(x_ref, ya_ref, yb_ref, pg_ref, wa_ref, wb_ref, wo_ref, g_ref, b_ref, o_ref, op_ref,
                      *, alpha):
    d = x_ref.shape[-1]
    gate_a = _sigmoid(pg_ref[:, :d])
    gate_b = _sigmoid(pg_ref[:, d:])
    mixed = gate_a * _bdot(ya_ref[...], wa_ref[...]) + gate_b * _bdot(yb_ref[...], wb_ref[...])
    h = alpha * x_ref[...] + _bdot(mixed, wo_ref[...])
    out = _layer_norm(h, g_ref[...], b_ref[...])
    o_ref[...] = out
    op_ref[...] = _pack_bf16_halves(out)


def _mixer_out(xf, ya, yb, p_g, wa, wb, wo, ln_g, ln_b, alpha, tm=512):
    T, D = xf.shape
    rows = lambda w: pl.BlockSpec((tm, w), lambda i: (i, 0))
    full = lambda a: pl.BlockSpec(a.shape, lambda i: (0,) * a.ndim)
    ln_g, ln_b = ln_g.reshape(1, D), ln_b.reshape(1, D)
    return pl.pallas_call(
        functools.partial(_mixer_out_kernel, alpha=alpha),
        grid=(T // tm,),
        in_specs=[rows(D), rows(ya.shape[1]), rows(yb.shape[1]), rows(2 * D),
                  full(wa), full(wb), full(wo), full(ln_g), full(ln_b)],
        out_specs=[rows(D), rows(D // 2)],
        out_shape=[jax.ShapeDtypeStruct((T, D), F32), jax.ShapeDtypeStruct((T, D // 2), jnp.uint32)],
        compiler_params=pltpu.CompilerParams(
            dimension_semantics=("parallel",), vmem_limit_bytes=VMEM_LIMIT),
        name="mixer_out_ln",
    )(xf, ya, yb, p_g, wa, wb, wo, ln_g, ln_b)


ROUTER_TILE = 256
EXPERT_ROWS = 256
SC_TOKEN_CHUNK = 64
SC_ROW_CHUNK = 64
PICK_LANES = 128
LOWEST = -3.0e38


def _router_kernel(x_ref, rwt_ref, bias_ref, eidx_ref, wts_ref, pos_ref, cnt_ref, carry_ref):
    tm, E = x_ref.shape[0], rwt_ref.shape[0]
    per_group = E // N_GROUPS
    reps = tm // PICK_LANES

    @pl.when(pl.program_id(0) == 0)
    def _():
        carry_ref[...] = jnp.zeros_like(carry_ref)

    scores = _sigmoid(_dot3(rwt_ref[...], x_ref[...], (((1,), (1,)), ((), ()))))
    choice = scores + jnp.concatenate([bias_ref[...]] * reps, axis=1)
    row = lax.broadcasted_iota(jnp.int32, (E, tm), 0)

    def first_max(vals, rows):
        m = jnp.max(vals, axis=0, keepdims=True)
        return m, jnp.min(jnp.where(vals == m, rows, E), axis=0, keepdims=True)

    group_score = []
    for g in range(N_GROUPS):
        rows = slice(g * per_group, (g + 1) * per_group)
        group_row = g * per_group + lax.broadcasted_iota(jnp.int32, (per_group, tm), 0)
        m1, i1 = first_max(choice[rows], group_row)
        m2 = jnp.max(jnp.where(group_row == i1, LOWEST, choice[rows]), axis=0, keepdims=True)
        group_score.append(m1 + m2)
    masked = []
    for g in range(N_GROUPS):
        rank = jnp.zeros((1, tm), jnp.int32)
        for o in range(N_GROUPS):
            if o != g:
                ahead = (group_score[o] > group_score[g]) if o > g else (group_score[o] >= group_score[g])
                rank = rank + ahead.astype(jnp.int32)
        masked.append(jnp.where(rank < TOPK_GROUPS, choice[g * per_group:(g + 1) * per_group], NEG_INF))

    cur = jnp.concatenate(masked, axis=0)
    picks = []
    for _ in range(TOP_K):
        _, idx = first_max(cur, row)
        picks.append(idx)
        cur = jnp.where(row == idx, LOWEST, cur)
    sel = jnp.where(cur == LOWEST, 1.0, 0.0)
    gate = scores * sel
    gate = gate * (ROUTED_SCALE / jnp.sum(gate, axis=0, keepdims=True))

    ti = lax.broadcasted_iota(jnp.int32, (tm, tm), 0)
    tj = lax.broadcasted_iota(jnp.int32, (tm, tm), 1)
    sel_b = sel.astype(BF16)
    before = jnp.dot(sel_b, (ti < tj).astype(BF16), preferred_element_type=F32)
    queue_pos = before + jnp.concatenate([carry_ref[...]] * reps, axis=1)
    carry_ref[...] = carry_ref[...] + jnp.dot(sel_b, jnp.ones((tm, PICK_LANES), BF16),
                                              preferred_element_type=F32)
    cnt_ref[...] = carry_ref[...]

    at_pick = lambda vals, idx: jnp.sum(jnp.where(row == idx, vals, 0.0), axis=0, keepdims=True)
    eidx_ref[...] = jnp.concatenate(picks, axis=0)
    wts_ref[...] = jnp.concatenate([at_pick(gate, idx) for idx in picks], axis=0)
    pos_ref[...] = jnp.concatenate([at_pick(queue_pos, idx) for idx in picks], axis=0).astype(jnp.int32)


def _router(xf, router_w, router_bias):
    T, D = xf.shape
    E = router_w.shape[1]
    tm = ROUTER_TILE
    picks = lambda dt: jax.ShapeDtypeStruct((TOP_K, T), dt)
    pick_spec = pl.BlockSpec((TOP_K, tm), lambda i: (0, i))
    lanes = lambda v: jnp.broadcast_to(v.reshape(E, 1), (E, PICK_LANES))
    return pl.pallas_call(
        _router_kernel,
        grid=(T // tm,),
        in_specs=[pl.BlockSpec((tm, D), lambda i: (i, 0)),
                  pl.BlockSpec((E, D), lambda i: (0, 0)),
                  pl.BlockSpec((E, PICK_LANES), lambda i: (0, 0))],
        out_specs=[pick_spec, pick_spec, pick_spec, pl.BlockSpec((E, PICK_LANES), lambda i: (0, 0))],
        out_shape=[picks(jnp.int32), picks(F32), picks(jnp.int32),
                   jax.ShapeDtypeStruct((E, PICK_LANES), F32)],
        scratch_shapes=[pltpu.VMEM((E, PICK_LANES), F32)],
        compiler_params=pltpu.CompilerParams(
            dimension_semantics=("arbitrary",), vmem_limit_bytes=VMEM_LIMIT),
        name="moe_router",
    )(xf, router_w.T, lanes(router_bias))


def _dest_kernel(eidx_ref, pos_ref, start_ref, dest_ref):
    E = start_ref.shape[0]
    tm = eidx_ref.shape[1]
    row = lax.broadcasted_iota(jnp.int32, (E, tm), 0)
    start = jnp.concatenate([start_ref[...]] * (tm // PICK_LANES), axis=1)
    eidx = eidx_ref[...]
    base = [jnp.sum(jnp.where(row == eidx[kk:kk + 1, :], start, 0), axis=0, keepdims=True)
            for kk in range(TOP_K)]
    dest_ref[...] = jnp.concatenate(base, axis=0) + pos_ref[...]


def _dest_rows(eidx_t, pos_t, pad_start):
    T = eidx_t.shape[1]
    E = pad_start.shape[0]
    tm = ROUTER_TILE
    pick_spec = pl.BlockSpec((TOP_K, tm), lambda i: (0, i))
    return pl.pallas_call(
        _dest_kernel,
        grid=(T // tm,),
        in_specs=[pick_spec, pick_spec, pl.BlockSpec((E, PICK_LANES), lambda i: (0, 0))],
        out_specs=pick_spec,
        out_shape=jax.ShapeDtypeStruct((TOP_K, T), jnp.int32),
        compiler_params=pltpu.CompilerParams(
            dimension_semantics=("parallel",), vmem_limit_bytes=VMEM_LIMIT),
        name="moe_dest_rows",
    )(eidx_t, pos_t, jnp.broadcast_to(pad_start.reshape(E, 1), (E, PICK_LANES)))


def _sc_mesh():
    return plsc.VectorSubcoreMesh(core_axis_name="c", subcore_axis_name="s")


def _sc_scatter_rows(x, dest_t, n_rows):
    T, D = x.shape
    K = dest_t.shape[0]
    mesh = _sc_mesh()
    nc, nw = mesh.num_cores, mesh.num_cores * mesh.num_subcores
    per_w = T // nw
    chunk = min(SC_TOKEN_CHUNK, per_w)
    n_chunks = per_w // chunk
    idx = dest_t.reshape(K, nw, n_chunks, chunk).transpose(1, 2, 0, 3).reshape(nw, n_chunks * K, chunk)

    assert n_chunks % 2 == 0

    @functools.partial(
        pl.kernel, mesh=mesh,
        out_type=jax.ShapeDtypeStruct((n_rows, D), x.dtype),
        scratch_types=[pltpu.VMEM((n_chunks * K, chunk), jnp.int32),
                       pltpu.VMEM((2, chunk, D), x.dtype),
                       pltpu.SemaphoreType.DMA((2,)), pltpu.SemaphoreType.DMA((2,))],
    )
    def scatter(x_hbm, idx_hbm, out_hbm, idx_v, rows_v, load_sem, send_sem):
        wid = lax.axis_index("s") * nc + lax.axis_index("c")
        pltpu.sync_copy(idx_hbm.at[wid], idx_v)

        def load(j, b):
            return pltpu.make_async_copy(x_hbm.at[pl.ds(wid * per_w + j * chunk, chunk)], rows_v.at[b],
                                         load_sem.at[b])

        def sends(j, b):
            return [pltpu.make_async_copy(rows_v.at[b], out_hbm.at[idx_v.at[j * K + kk]], send_sem.at[b])
                    for kk in range(K)]

        load(0, 0).start()

        @pl.loop(0, n_chunks, step=2)
        def _(j0):
            for b in range(2):
                j = j0 + b
                load(j, b).wait()

                @pl.when(j >= 1)
                def _():
                    for c in sends(j - 1, 1 - b):
                        c.wait()

                @pl.when(j + 1 < n_chunks)
                def _():
                    load(j + 1, 1 - b).start()

                for c in sends(j, b):
                    c.start()

        for c in sends(n_chunks - 1, (n_chunks - 1) % 2):
            c.wait()

    return scatter(x, idx)


def _sc_gather_rows(src, idx):
    M = idx.shape[0]
    D = src.shape[1]
    mesh = _sc_mesh()
    nc, nw = mesh.num_cores, mesh.num_cores * mesh.num_subcores
    per_w = M // nw
    chunk = min(SC_ROW_CHUNK, per_w)
    n_chunks = per_w // chunk

    assert n_chunks % 2 == 0

    @functools.partial(
        pl.kernel, mesh=mesh,
        out_type=jax.ShapeDtypeStruct((M, D), src.dtype),
        scratch_types=[pltpu.VMEM((n_chunks, chunk), jnp.int32),
                       pltpu.VMEM((2, chunk, D), src.dtype),
                       pltpu.SemaphoreType.DMA((2,)), pltpu.SemaphoreType.DMA((2,))],
    )
    def gather(src_hbm, idx_hbm, out_hbm, idx_v, rows_v, fetch_sem, store_sem):
        wid = lax.axis_index("s") * nc + lax.axis_index("c")
        pltpu.sync_copy(idx_hbm.at[wid], idx_v)

        def fetch(j, b):
            return pltpu.make_async_copy(src_hbm.at[idx_v.at[j]], rows_v.at[b], fetch_sem.at[b])

        def store(j, b):
            return pltpu.make_async_copy(rows_v.at[b], out_hbm.at[pl.ds(wid * per_w + j * chunk, chunk)],
                                         store_sem.at[b])

        fetch(0, 0).start()

        @pl.loop(0, n_chunks, step=2)
        def _(j0):
            for b in range(2):
                j = j0 + b
                fetch(j, b).wait()

                @pl.when(j >= 1)
                def _():
                    store(j - 1, 1 - b).wait()

                @pl.when(j + 1 < n_chunks)
                def _():
                    fetch(j + 1, 1 - b).start()

                store(j, b).start()

        store(n_chunks - 1, (n_chunks - 1) % 2).wait()

    return gather(src, idx.reshape(nw, n_chunks, chunk))


def _expert_kernel(distinct_e_ref, blk_ord_ref, blk_new_ref, blk_rows_ref, n_used_ref, n_distinct_ref,
                   x_ref, wgu_hbm, wd_hbm, o_ref, wgu_buf, wd_buf, wgu_bf, wd_bf, sem):
    i = pl.program_id(0)
    live = i < n_used_ref[0]
    ordinal = blk_ord_ref[i]
    slot = ordinal % 2

    def weight_copies(k, s):
        e = distinct_e_ref[k]
        return (pltpu.make_async_copy(wgu_hbm.at[e], wgu_buf.at[s], sem.at[0, s]),
                pltpu.make_async_copy(wd_hbm.at[e], wd_buf.at[s], sem.at[1, s]))

    @pl.when(i == 0)
    def _():
        for c in weight_copies(0, 0):
            c.start()

    @pl.when(live & (blk_new_ref[i] == 1))
    def _():
        for c in weight_copies(ordinal, slot):
            c.wait()

        @pl.when(ordinal + 1 < n_distinct_ref[0])
        def _():
            for c in weight_copies(ordinal + 1, 1 - slot):
                c.start()

        wgu_bf[...] = wgu_buf[slot].astype(BF16)
        wd_bf[...] = wd_buf[slot].astype(BF16)

    @pl.when(live)
    def _():
        hidden = wd_bf.shape[0]
        half = x_ref.shape[1]
        row = lax.broadcasted_iota(jnp.int32, x_ref.shape, 0)
        left, right = _unpack_bf16_halves(x_ref[...])
        real = row < blk_rows_ref[i]
        left = jnp.where(real, left, 0.0).astype(BF16)
        right = jnp.where(real, right, 0.0).astype(BF16)
        h = (jnp.dot(left, wgu_bf[:half, :], preferred_element_type=F32)
             + jnp.dot(right, wgu_bf[half:, :], preferred_element_type=F32))
        gate, up = h[:, :hidden], h[:, hidden:]
        act = (gate * _sigmoid(gate) * up).astype(BF16)
        o_ref[...] = _pack_bf16_halves(jnp.dot(act, wd_bf[...], preferred_element_type=F32))

    @pl.when(jnp.logical_not(live))
    def _():
        o_ref[...] = jnp.zeros_like(o_ref)


def _expert_ffn(xs, blk_e, blk_rows, n_used, w_gu, w_down):
    n_rows, half = xs.shape
    E, D, two_h = w_gu.shape
    n_blocks = n_rows // EXPERT_ROWS
    idx = jnp.arange(n_blocks, dtype=jnp.int32)
    is_live = idx < n_used[0]
    blk_new = (is_live & ((idx == 0) | (blk_e != jnp.roll(blk_e, 1)))).astype(jnp.int32)
    blk_ord = (jnp.cumsum(blk_new) - 1).astype(jnp.int32)
    n_distinct = blk_ord[-1:] + 1
    distinct_e = jnp.zeros((n_blocks,), jnp.int32).at[blk_ord].max(blk_e * is_live)

    live = lambda i, nu: jnp.minimum(i, nu[0] - 1)
    grid_spec = pltpu.PrefetchScalarGridSpec(
        num_scalar_prefetch=6,
        grid=(n_blocks,),
        in_specs=[pl.BlockSpec((EXPERT_ROWS, half), lambda i, de, bo, bn, br, nu, nd: (live(i, nu), 0)),
                  pl.BlockSpec(memory_space=pl.ANY), pl.BlockSpec(memory_space=pl.ANY)],
        out_specs=pl.BlockSpec((EXPERT_ROWS, half), lambda i, de, bo, bn, br, nu, nd: (i, 0)),
        scratch_shapes=[pltpu.VMEM((2, D, two_h), F32), pltpu.VMEM((2, two_h // 2, D), F32),
                        pltpu.VMEM((D, two_h), BF16), pltpu.VMEM((two_h // 2, D), BF16),
                        pltpu.SemaphoreType.DMA((2, 2))],
    )
    return pl.pallas_call(
        _expert_kernel,
        grid_spec=grid_spec,
        out_shape=jax.ShapeDtypeStruct((n_rows, half), jnp.uint32),
        compiler_params=pltpu.CompilerParams(
            dimension_semantics=("arbitrary",), vmem_limit_bytes=VMEM_LIMIT),
        name="moe_experts",
    )(distinct_e, blk_ord, blk_new, blk_rows, n_used, n_distinct, xs, w_gu, w_down)


def _moe_out_kernel(x_ref, yk_ref, wts_ref, sgu_ref, sd_ref, g_ref, b_ref, o_ref, *, alpha):
    x = x_ref[...]
    hidden = sd_ref.shape[0]
    h = _bdot(x, sgu_ref[...])
    gate, up = h[:, :hidden], h[:, hidden:]
    ffn = _bdot(gate * _sigmoid(gate) * up, sd_ref[...])
    wts = wts_ref[...]
    routed_left = routed_right = None
    for kk in range(TOP_K):
        left, right = _unpack_bf16_halves(yk_ref[kk])
        w = wts[:, kk:kk + 1]
        routed_left = w * left if kk == 0 else routed_left + w * left
        routed_right = w * right if kk == 0 else routed_right + w * right
    ffn = ffn + jnp.concatenate([routed_left, routed_right], axis=-1)
    o_ref[...] = _layer_norm(alpha * x + ffn, g_ref[...], b_ref[...])


def _moe_out(xf, yk, wts, sw_gu, sw_down, ln_g, ln_b, alpha, tm=128):
    T, D = xf.shape
    rows = lambda w: pl.BlockSpec((tm, w), lambda i: (i, 0))
    full = lambda a: pl.BlockSpec(a.shape, lambda i: (0,) * a.ndim)
    ln_g, ln_b = ln_g.reshape(1, D), ln_b.reshape(1, D)
    return pl.pallas_call(
        functools.partial(_moe_out_kernel, alpha=alpha),
        grid=(T // tm,),
        in_specs=[rows(D), pl.BlockSpec((TOP_K, tm, D // 2), lambda i: (0, i, 0)), rows(PICK_LANES),
                  full(sw_gu), full(sw_down), full(ln_g), full(ln_b)],
        out_specs=rows(D),
        out_shape=jax.ShapeDtypeStruct((T, D), F32),
        compiler_params=pltpu.CompilerParams(
            dimension_semantics=("parallel",), vmem_limit_bytes=VMEM_LIMIT),
        name="moe_combine_ln",
    )(xf, yk, wts, sw_gu, sw_down, ln_g, ln_b)


def _moe_ffn_ln(xf, xp, router_w, router_bias, w_gu, w_down, sw_gu, sw_down, ln_g, ln_b, alpha):
    T, D = xf.shape
    E = router_w.shape[1]
    BM = EXPERT_ROWS
    eidx_t, wts_t, pos_t, cnt = _router(xf, router_w, router_bias)
    counts = cnt[:, 0].astype(jnp.int32)
    padded = (counts + BM - 1) // BM * BM
    pad_end = jnp.cumsum(padded)
    pad_start = pad_end - padded
    n_rows = T * TOP_K + E * BM
    n_blocks = n_rows // BM
    blk_row0 = jnp.arange(n_blocks, dtype=jnp.int32) * BM
    blk_e = jnp.minimum(jnp.sum((pad_end[None, :] <= blk_row0[:, None]).astype(jnp.int32), axis=1), E - 1)
    blk_rows = jnp.clip(pad_start[blk_e] + counts[blk_e] - blk_row0, 0, BM).astype(jnp.int32)
    n_used = (pad_end[-1:] // BM).astype(jnp.int32)
    dest_t = _dest_rows(eidx_t, pos_t, pad_start)
    wts = jnp.pad(wts_t.T, ((0, 0), (0, PICK_LANES - TOP_K)))
    xs = _sc_scatter_rows(xp, dest_t, n_rows)
    ys = _expert_ffn(xs, blk_e, blk_rows, n_used, w_gu, w_down)
    yk = _sc_gather_rows(ys, dest_t.reshape(-1)).reshape(TOP_K, T, D // 2)
    return _moe_out(xf, yk, wts, sw_gu, sw_down, ln_g, ln_b, alpha)


def kernel(x, w_in, tshift_mu, rwkv_w0, rwkv_w2, rwkv_a0, rwkv_a2, rwkv_g2, rwkv_k_k, rwkv_k_a, rwkv_r_k, rwkv_lnx_w, rwkv_lnx_b, cmp_pe_k, cmp_w1_k, cmp_w2_k, cmp_pe_v, cmp_w1_v, cmp_w2_v, w_branch_a, w_branch_b, w_out, ln1_g, ln1_b, router_w, router_bias, exp_w_gu, exp_w_down, shared_w_gu, shared_w_down, ln2_g, ln2_b):
    B, S, D = x.shape
    depth = w_in.shape[0]
    alpha = (2 * depth) ** 0.25
    nsa_w = w_in.shape[2] - RWKV_IN_W - 2 * D
    for l in range(depth):
        xf = x.reshape(B * S, D)
        w_l = w_in[l]
        w_a = w_l[:, :RWKV_IN_W].astype(BF16)
        w_b = _nsa_weight_columns(w_l[:, RWKV_IN_W:RWKV_IN_W + nsa_w]).astype(BF16)
        w_g = w_l[:, RWKV_IN_W + nsa_w:].astype(BF16)
        kv_w = 6 * NSA_KV_WIDTH
        p_a = _matmul(xf, w_a, PROJ_ROWS, w_a.shape[1]).reshape(B, S, -1)
        p_kv = _matmul(xf, w_b[:, :kv_w], PROJ_ROWS, kv_w).reshape(B, S, -1)
        qg_t = _matmul_t(x, w_b[:, kv_w:].T, PROJ_ROWS)
        p_g = _matmul(xf, w_g, PROJ_ROWS, w_g.shape[1])
        y_a = _rwkv_time_mix(p_a, tshift_mu[l], rwkv_w0[l], rwkv_w2[l], rwkv_a0[l], rwkv_a2[l], rwkv_g2[l],
                             rwkv_k_k[l], rwkv_k_a[l], rwkv_r_k[l].reshape(-1), rwkv_lnx_w[l], rwkv_lnx_b[l])
        y_b = _nsa_branch(p_kv, qg_t, cmp_pe_k[l], cmp_w1_k[l], cmp_w2_k[l],
                          cmp_pe_v[l], cmp_w1_v[l], cmp_w2_v[l])
        x1, x1p = _mixer_out(xf, y_a.reshape(B * S, -1), y_b.reshape(B * S, -1), p_g,
                             w_branch_a[l].astype(BF16), w_branch_b[l].astype(BF16), w_out[l].astype(BF16),
                             ln1_g[l], ln1_b[l], alpha)
        x2 = _moe_ffn_ln(x1, x1p, router_w[l], router_bias[l], exp_w_gu[l], exp_w_down[l],
                         shared_w_gu[l].astype(BF16), shared_w_down[l].astype(BF16), ln2_g[l], ln2_b[l], alpha)
        x = x2.reshape(B, S, D)
    return x
```

```python
import functools

import numpy as np
import jax
import jax.numpy as jnp
from jax import lax
from jax.experimental import pallas as pl
from jax.experimental.pallas import tpu as pltpu
from jax.experimental.pallas import tpu_sc as plsc

F32 = jnp.float32
BF16 = jnp.bfloat16

RWKV_HEADS = 8
HEAD_DIM = 64
RWKV_WIDTH = RWKV_HEADS * HEAD_DIM
W_LORA = 64
A_LORA = 64
G_LORA = 128
GN_EPS = 64e-5
NSA_HEADS = 8
NSA_GROUPS = 2
NSA_HPG = NSA_HEADS // NSA_GROUPS
NSA_WIDTH = NSA_HEADS * HEAD_DIM
NSA_KV_WIDTH = NSA_GROUPS * HEAD_DIM
CMP_BLOCK = 32
CMP_STRIDE = 16
CMP_HIDDEN = 256
SEL_BLOCK = 64
N_SELECT = 16
WINDOW = 512
ROPE_THETA = 10000.0
RWKV_IN_W = 3 * RWKV_WIDTH + W_LORA + A_LORA + G_LORA
N_EXPERTS = 256
TOP_K = 8
N_GROUPS = 8
TOPK_GROUPS = 4
EXPERT_DIM = 256
ROUTED_SCALE = 2.5
LN_EPS = 1e-5
NEG_INF = -1e30
FORCE_BONUS = 1e4

RWKV_CHUNK = 64
RWKV_HEAD_GROUP = 4
RWKV_STEP_CHUNKS = 2
VMEM_LIMIT = 56 * 1024 * 1024
PROJ_ROWS = 1024


def _bdot(a, b):
    return jnp.dot(a.astype(BF16), b.astype(BF16), preferred_element_type=F32)


def _bdot_nt(a, b):
    return lax.dot_general(a.astype(BF16), b.astype(BF16), (((1,), (1,)), ((), ())),
                           preferred_element_type=F32)


def _bdot_tn(a, b):
    return lax.dot_general(a.astype(BF16), b.astype(BF16), (((0,), (0,)), ((), ())),
                           preferred_element_type=F32)


def _bf16_pieces(x, n):
    pieces = []
    for _ in range(n):
        p = x.astype(BF16)
        pieces.append(p)
        x = x - p.astype(F32)
    return pieces


def _dot3(a, b, dims=(((1,), (0,)), ((), ()))):
    (a_hi, a_lo), (b_hi, b_lo) = _bf16_pieces(a, 2), _bf16_pieces(b, 2)
    dot = lambda p, q: lax.dot_general(p, q, dims, preferred_element_type=F32)
    return dot(a_hi, b_hi) + (dot(a_hi, b_lo) + dot(a_lo, b_hi))


def _sigmoid(x):
    return 1.0 / (1.0 + jnp.exp(-x))


def _matmul_kernel(x_ref, w_ref, o_ref):
    o_ref[...] = jnp.dot(x_ref[...].astype(BF16), w_ref[...], preferred_element_type=F32).astype(o_ref.dtype)


def _matmul(x, w, tm, tn, out_dtype=F32):
    M, K = x.shape
    N = w.shape[1]
    return pl.pallas_call(
        _matmul_kernel,
        grid=(M // tm, N // tn),
        in_specs=[pl.BlockSpec((tm, K), lambda i, j: (i, 0)),
                  pl.BlockSpec((K, tn), lambda i, j: (0, j))],
        out_specs=pl.BlockSpec((tm, tn), lambda i, j: (i, j)),
        out_shape=jax.ShapeDtypeStruct((M, N), out_dtype),
        compiler_params=pltpu.CompilerParams(
            dimension_semantics=("parallel", "parallel"), vmem_limit_bytes=VMEM_LIMIT),
        name="dense_proj",
    )(x, w)


def _matmul_t_kernel(x_ref, wt_ref, o_ref):
    o_ref[...] = lax.dot_general(wt_ref[...], x_ref[...].astype(BF16), (((1,), (1,)), ((), ())),
                                 preferred_element_type=F32)


def _matmul_t(x, w_t, tm):
    B, S, K = x.shape
    N = w_t.shape[0]
    return pl.pallas_call(
        _matmul_t_kernel,
        grid=(B, S // tm),
        in_specs=[pl.BlockSpec((None, tm, K), lambda b, s: (b, s, 0)),
                  pl.BlockSpec((N, K), lambda b, s: (0, 0))],
        out_specs=pl.BlockSpec((None, N, tm), lambda b, s: (b, 0, s)),
        out_shape=jax.ShapeDtypeStruct((B, N, S), F32),
        compiler_params=pltpu.CompilerParams(
            dimension_semantics=("parallel", "parallel"), vmem_limit_bytes=VMEM_LIMIT),
        name="dense_proj_t",
    )(x, w_t)


def _rwkv_kernel(p_ref, mu_ref, w0_ref, w2_ref, a0_ref, a2_ref, g2_ref, kk_ref, ka_ref, rk_ref,
                 lnw_ref, lnb_ref, o_ref, carry_ref, state_ref):
    C, H, N = RWKV_CHUNK, RWKV_HEADS, HEAD_DIM
    W = RWKV_WIDTH
    B = p_ref.shape[0]
    NC = p_ref.shape[1] // C
    L = NC * C
    R = B * L

    @pl.when(pl.program_id(0) == 0)
    def _():
        carry_ref[...] = jnp.zeros_like(carry_ref)
        state_ref[...] = jnp.zeros_like(state_ref)

    def per_block(x, rows):
        return jnp.concatenate(
            [jnp.broadcast_to(x[i].reshape(1, -1), (rows, x.shape[-1])) for i in range(x.shape[0])], axis=0)

    p = p_ref[...].reshape(R, p_ref.shape[-1])
    row = lax.broadcasted_iota(jnp.int32, p.shape, 0)
    prev = jnp.where(row % L == 0, per_block(carry_ref[...], L), pltpu.roll(p, 1, axis=0))
    for b in range(B):
        carry_ref[b] = p[b * L + L - 1:b * L + L, :]
    xs = p + (prev - p) * mu_ref[...]
    r = xs[:, 0:W]
    k = xs[:, W:2 * W]
    v = xs[:, 2 * W:3 * W]
    wl = xs[:, 3 * W:3 * W + W_LORA]
    al = xs[:, 3 * W + W_LORA:3 * W + W_LORA + A_LORA]
    gl = xs[:, 3 * W + W_LORA + A_LORA:]

    z = -(w0_ref[...] + _dot3(jnp.tanh(wl), w2_ref[...]))
    softplus = jnp.maximum(z, 0.0) + jnp.log1p(jnp.exp(-jnp.abs(z)))
    logd = -jnp.exp(-softplus - 0.5)
    a = _sigmoid(a0_ref[...] + _dot3(al, a2_ref[...]))
    g = _dot3(_sigmoid(gl), g2_ref[...])

    kk = k * kk_ref[...]
    knew = k * (1.0 + (a - 1.0) * ka_ref[...])

    HG = RWKV_HEAD_GROUP
    GW = HG * N
    same_head_lanes = (lax.broadcasted_iota(jnp.int32, (GW, GW), 0) // N
                       == lax.broadcasted_iota(jnp.int32, (GW, GW), 1) // N)
    head_ones = jnp.where(same_head_lanes, 1.0, 0.0).astype(BF16)

    def head_sum(x):
        hi = x.astype(BF16)
        lo = (x - hi.astype(F32)).astype(BF16)
        return jnp.concatenate(
            [jnp.dot(hi[:, s:s + GW], head_ones, preferred_element_type=F32)
             + jnp.dot(lo[:, s:s + GW], head_ones, preferred_element_type=F32) for s in range(0, W, GW)],
            axis=-1)

    kk = kk / jnp.maximum(jnp.sqrt(head_sum(kk * kk)), 1e-12)
    lr_kk = kk * a

    ti = lax.broadcasted_iota(jnp.int32, (R, R), 0)
    tj = lax.broadcasted_iota(jnp.int32, (R, R), 1)
    same_chunk = (ti >= tj) & (ti // C == tj // C)
    tri = same_chunk.astype(BF16)
    cl = sum(jnp.dot(tri, piece, preferred_element_type=F32) for piece in reversed(_bf16_pieces(logd, 3)))
    cl_end = per_block(jnp.concatenate([cl[i * C + C - 1:i * C + C, :] for i in range(B * NC)], axis=0), C)
    a_hat = -kk * jnp.exp(cl - logd)
    r_hat = r * jnp.exp(cl)
    inv_gam = jnp.exp(-cl)
    b_til = lr_kk * inv_gam
    k_til = knew * inv_gam
    to_end = jnp.exp(cl_end - cl)
    b_end = lr_kk * to_end
    k_end = knew * to_end
    gam_end = jnp.exp(cl_end)

    gt = lax.broadcasted_iota(jnp.int32, (C, GW), 0)
    gc = lax.broadcasted_iota(jnp.int32, (C, GW), 1) % N
    strict = gt > gc
    incl = gt >= gc
    eye = (gt == gc).astype(F32)
    bi = lax.broadcasted_iota(jnp.int32, (HG * C, GW), 0) // C
    bj = lax.broadcasted_iota(jnp.int32, (HG * C, GW), 1) // N
    same_head = bi == bj

    def block_diag(y):
        yb = y.astype(BF16)
        return jnp.where(same_head, jnp.concatenate([yb] * HG, axis=0), jnp.zeros((), BF16))

    def bd_dot(x, y_bd):
        return jnp.dot(x.astype(BF16), y_bd, preferred_element_type=F32)

    def bd_dot_nt(x, y_bd):
        return lax.dot_general(x.astype(BF16), y_bd, (((1,), (1,)), ((), ())), preferred_element_type=F32)

    n_groups = H // HG
    units = [(b, c, gi) for b in range(B) for c in range(NC) for gi in range(n_groups)]
    n_units = range(len(units))
    cut = lambda x, b, c, gi: x[(b * NC + c) * C:(b * NC + c + 1) * C, gi * GW:(gi + 1) * GW]
    v_u = [cut(v, *un) for un in units]
    v_bd = [block_diag(v_u[i]) for i in n_units]
    ar = [jnp.concatenate([cut(a_hat, *un), cut(r_hat, *un)], axis=0) for un in units]
    mb = [bd_dot_nt(ar[i], block_diag(cut(b_til, *units[i]))) for i in n_units]
    mk = [bd_dot_nt(ar[i], block_diag(cut(k_til, *units[i]))) for i in n_units]
    n_ab = [jnp.where(strict, mb[i][:C], 0.0) for i in n_units]
    m_rb = [jnp.where(incl, mb[i][C:], 0.0) for i in n_units]
    l_ak = [jnp.where(strict, mk[i][:C], 0.0) for i in n_units]
    m_rk = [jnp.where(incl, mk[i][C:], 0.0) for i in n_units]

    pw = list(n_ab)
    pw_bd = [block_diag(pw[i]) for i in n_units]
    tinv = [eye + n_ab[i] for i in n_units]
    step = 2
    while step < C:
        pw = [bd_dot(pw[i], pw_bd[i]) for i in n_units]
        pw_bd = [block_diag(pw[i]) for i in n_units]
        tinv = [tinv[i] + bd_dot(tinv[i], pw_bd[i]) for i in n_units]
        step *= 2
    lv = [bd_dot(l_ak[i], v_bd[i]) for i in n_units]

    state = {(b, gi): state_ref[b * n_groups + gi] for b in range(B) for gi in range(n_groups)}
    outs = {}
    for c in range(NC):
        live = [i for i in n_units if units[i][1] == c]
        s0 = {i: state[(units[i][0], units[i][2])] for i in live}
        ars = {i: bd_dot_nt(ar[i], block_diag(s0[i])) for i in live}
        u = {i: bd_dot(tinv[i], block_diag(ars[i][:C] + lv[i])) for i in live}
        for i in live:
            outs[units[i]] = ars[i][C:] + bd_dot(m_rb[i], block_diag(u[i])) + bd_dot(m_rk[i], v_bd[i])
        for i in live:
            b, _, gi = units[i]
            uv = jnp.concatenate([u[i], v_u[i]], axis=0)
            bk_end = jnp.concatenate([cut(b_end, *units[i]), cut(k_end, *units[i])], axis=0)
            cross = jnp.where(same_head, _bdot_tn(uv, bk_end), 0.0)
            upd = cross[0:N]
            for h in range(1, HG):
                upd = upd + cross[h * N:(h + 1) * N]
            state[(b, gi)] = s0[i] * cut(gam_end, *units[i])[0:1] + upd
    for (b, gi), s_new in state.items():
        state_ref[b * n_groups + gi] = s_new

    o = jnp.concatenate([jnp.concatenate([outs[(b, c, gi)] for gi in range(n_groups)], axis=-1)
                         for b in range(B) for c in range(NC)], axis=0)
    mean = head_sum(o) * (1.0 / N)
    var = head_sum(jnp.square(o - mean)) * (1.0 / N)
    o = (o - mean) * lax.rsqrt(var + GN_EPS) * lnw_ref[...] + lnb_ref[...]
    bonus = head_sum(r * knew * rk_ref[...]) * v
    o_ref[...] = ((o + bonus) * g).reshape(o_ref.shape)


def _rwkv_time_mix(p_a, mu, w0, w2, a0, a2, g2, k_k, k_a, r_k, lnx_w, lnx_b):
    B, S, _ = p_a.shape
    L = RWKV_CHUNK * RWKV_STEP_CHUNKS
    row = lambda t: t.reshape(1, -1)
    full = lambda shape: pl.BlockSpec(shape, lambda s: (0,) * len(shape))
    n_units = B * RWKV_HEADS // RWKV_HEAD_GROUP
    return pl.pallas_call(
        _rwkv_kernel,
        grid=(S // L,),
        in_specs=[pl.BlockSpec((B, L, RWKV_IN_W), lambda s: (0, s, 0)),
                  full((1, RWKV_IN_W)), full((1, RWKV_WIDTH)), full((W_LORA, RWKV_WIDTH)),
                  full((1, RWKV_WIDTH)), full((A_LORA, RWKV_WIDTH)), full((G_LORA, RWKV_WIDTH)),
                  full((1, RWKV_WIDTH)), full((1, RWKV_WIDTH)), full((1, RWKV_WIDTH)),
                  full((1, RWKV_WIDTH)), full((1, RWKV_WIDTH))],
        out_specs=pl.BlockSpec((B, L, RWKV_WIDTH), lambda s: (0, s, 0)),
        out_shape=jax.ShapeDtypeStruct((B, S, RWKV_WIDTH), F32),
        scratch_shapes=[pltpu.VMEM((B, 1, RWKV_IN_W), F32),
                        pltpu.VMEM((n_units, HEAD_DIM, RWKV_HEAD_GROUP * HEAD_DIM), F32)],
        compiler_params=pltpu.CompilerParams(
            dimension_semantics=("arbitrary",), vmem_limit_bytes=VMEM_LIMIT),
        name="rwkv7_chunked",
    )(p_a, row(mu), row(w0), w2, row(a0), a2, g2, row(k_k), row(k_a), row(r_k), row(lnx_w), row(lnx_b))


NSA_KV_TILE = 1024
SEL_KEY_TILE = 1024
NSA_QUERY_TILE = 256
SEL_LANES = 128


def _rope_tables(pos, reps):
    half = HEAD_DIM // 2
    inv = ROPE_THETA ** (-jnp.arange(half, dtype=F32) / half)
    ang = pos.astype(F32)[:, None] * inv
    cos, sin = jnp.cos(ang), jnp.sin(ang)
    cosf = jnp.concatenate([cos, cos], -1)
    sinf = jnp.concatenate([-sin, sin], -1)
    return jnp.tile(cosf, (1, reps)), jnp.tile(sinf, (1, reps))


def _rope(x, cosf, sinf):
    width = x.shape[-1]
    lane = lax.broadcasted_iota(jnp.int32, x.shape, 1)
    first_half = (lane % HEAD_DIM) < HEAD_DIM // 2
    rot = jnp.where(first_half, pltpu.roll(x, width - HEAD_DIM // 2, axis=1),
                    pltpu.roll(x, HEAD_DIM // 2, axis=1))
    return x * cosf + rot * sinf


def _kv_layout_kernel(p_ref, cos_ref, sin_ref, kc_ref, vc_ref, ks_ref, vs_ref, kw_ref, vw_ref):
    ts = p_ref.shape[0]
    for i, o_ref in ((0, kc_ref), (1, vc_ref), (2, ks_ref), (4, kw_ref)):
        t = p_ref[:, i * NSA_KV_WIDTH:(i + 1) * NSA_KV_WIDTH]
        if i >= 2:
            t = _rope(t, cos_ref[...], sin_ref[...])
        for g in range(NSA_GROUPS):
            o_ref[g] = t[:, g * HEAD_DIM:(g + 1) * HEAD_DIM].astype(o_ref.dtype)
    pad_row = lax.broadcasted_iota(jnp.int32, (VT_ROWS - HEAD_DIM, ts), 0)
    ones_row = jnp.where(pad_row == 0, 1.0, 0.0)
    for i, o_ref in ((3, vs_ref), (5, vw_ref)):
        t_t = p_ref[:, i * NSA_KV_WIDTH:(i + 1) * NSA_KV_WIDTH].T
        for g in range(NSA_GROUPS):
            o_ref[g] = jnp.concatenate([t_t[g * HEAD_DIM:(g + 1) * HEAD_DIM], ones_row],
                                       axis=0).astype(o_ref.dtype)


def _kv_layout(p_b, cos2, sin2):
    B, S, _ = p_b.shape
    ts = min(NSA_KV_TILE, S)
    out_spec = pl.BlockSpec((None, NSA_GROUPS, ts, HEAD_DIM), lambda b, s: (b, 0, s, 0))
    vt_spec = pl.BlockSpec((None, NSA_GROUPS, VT_ROWS, ts), lambda b, s: (b, 0, 0, s))
    shp = lambda dt: jax.ShapeDtypeStruct((B, NSA_GROUPS, S, HEAD_DIM), dt)
    vt_shp = jax.ShapeDtypeStruct((B, NSA_GROUPS, VT_ROWS, S), BF16)
    return pl.pallas_call(
        _kv_layout_kernel,
        grid=(B, S // ts),
        in_specs=[pl.BlockSpec((None, ts, 6 * NSA_KV_WIDTH), lambda b, s: (b, s, 0)),
                  pl.BlockSpec((ts, NSA_KV_WIDTH), lambda b, s: (s, 0)),
                  pl.BlockSpec((ts, NSA_KV_WIDTH), lambda b, s: (s, 0))],
        out_specs=[out_spec, out_spec, out_spec, vt_spec, out_spec, vt_spec],
        out_shape=[shp(F32), shp(F32), shp(BF16), vt_shp, shp(BF16), vt_shp],
        compiler_params=pltpu.CompilerParams(
            dimension_semantics=("parallel", "parallel"), vmem_limit_bytes=VMEM_LIMIT),
        name="nsa_kv_layout",
    )(p_b, cos2, sin2)


def _compress_kernel(subk_ref, subv_ref, pek_ref, w1k_ref, w2k_ref, pev_ref, w1v_ref, w2v_ref,
                     cos_ref, sin_ref, kc_ref, vc_ref):
    n_sub = subk_ref.shape[0]
    half = CMP_STRIDE * HEAD_DIM

    def mlp(sub_ref, pe_ref, w1_ref, w2_ref):
        sub = sub_ref[...]
        top = _bdot(sub, w1_ref[:half, :])
        bot = _bdot(sub, w1_ref[half:, :])
        bias = _bdot(jnp.broadcast_to(pe_ref[...], (8, 2 * half)), w1_ref[...])[0:1, :]
        h = top + pltpu.roll(bot, n_sub - 1, axis=0) + bias
        return _bdot(jax.nn.gelu(h), w2_ref[...])

    kc = mlp(subk_ref, pek_ref, w1k_ref, w2k_ref)
    rot = jnp.concatenate([kc[:, HEAD_DIM // 2:], kc[:, :HEAD_DIM // 2]], axis=-1)
    kc_ref[...] = (kc * cos_ref[...] + rot * sin_ref[...]).astype(kc_ref.dtype)
    vc_ref[...] = mlp(subv_ref, pev_ref, w1v_ref, w2v_ref).astype(vc_ref.dtype)


def _compress(subk, subv, pe_k, w1_k, w2_k, pe_v, w1_v, w2_v, cos_c, sin_c):
    B, G, n_sub, width = subk.shape
    sub_spec = pl.BlockSpec((None, None, n_sub, width), lambda b, g: (b, g, 0, 0))
    full = lambda a: pl.BlockSpec(a.shape, lambda b, g: (0,) * a.ndim)
    out_spec = pl.BlockSpec((None, None, n_sub, HEAD_DIM), lambda b, g: (b, g, 0, 0))
    pe_k, pe_v = pe_k.reshape(1, -1), pe_v.reshape(1, -1)
    args = (pe_k, w1_k, w2_k, pe_v, w1_v, w2_v, cos_c, sin_c)
    return pl.pallas_call(
        _compress_kernel,
        grid=(B, G),
        in_specs=[sub_spec, sub_spec] + [full(a) for a in args],
        out_specs=[out_spec, out_spec],
        out_shape=[jax.ShapeDtypeStruct((B, G, n_sub, HEAD_DIM), BF16)] * 2,
        compiler_params=pltpu.CompilerParams(
            dimension_semantics=("parallel", "parallel"), vmem_limit_bytes=VMEM_LIMIT),
        name="nsa_compress",
    )(subk, subv, *args)


MAX_FLOOR = -1e20
MASK_BIG = 2.0 ** 100
LOG2_E = 1.4426950408889634
VT_ROWS = 80


def _nsa_kernel(q_ref, gate_ref, cos_ref, sin_ref, kc_ref, vc_ref, ks_ref, vst_ref, kw_ref, vwt_ref,
                mselt_ref, o_ref, blockbias_ref, *, n_pick):
    QB, HP, D = NSA_QUERY_TILE, NSA_HPG, HEAD_DIM
    qb = pl.program_id(2)
    n_cmp = kc_ref.shape[0]
    lanes4 = lambda x: jnp.concatenate([x] * HP, axis=1)

    heads = []
    for n in range(HP):
        qh = q_ref[n * D:(n + 1) * D, :]
        rot = jnp.concatenate([qh[D // 2:], qh[:D // 2]], axis=0)
        heads.append(qh * cos_ref[...] + rot * sin_ref[...])
    q4 = (jnp.concatenate(heads, axis=1) * (D ** -0.5 * LOG2_E)).astype(BF16)
    t_row = qb * QB + lax.broadcasted_iota(jnp.int32, (1, QB), 1)

    def softmax_cols(s_t, bias_t):
        sm = s_t + lanes4(bias_t)
        m = jnp.maximum(jnp.max(sm, axis=0, keepdims=True), MAX_FLOOR)
        return jnp.exp2(sm - m)

    cmp_end = lax.broadcasted_iota(jnp.int32, (n_cmp, 1), 0) * CMP_STRIDE + (CMP_BLOCK - 1)
    e_c = softmax_cols(jnp.dot(kc_ref[...], q4, preferred_element_type=F32),
                       jnp.where(cmp_end <= t_row, 0.0, -MASK_BIG))
    den_c = jnp.sum(e_c, axis=0, keepdims=True)
    p_c = e_c * (1.0 / jnp.where(den_c > 0.0, den_c, 1.0))
    o_c = _bdot_tn(vc_ref[...], p_c)
    p_sum = p_c[:, 0:QB]
    for n in range(1, HP):
        p_sum = p_sum + p_c[:, n * QB:(n + 1) * QB]
    p_hi = p_sum.astype(BF16)
    p_lo = (p_sum - p_hi.astype(F32)).astype(BF16)
    imp_t = (jnp.dot(mselt_ref[...], p_hi, preferred_element_type=F32)
             + jnp.dot(mselt_ref[...], p_lo, preferred_element_type=F32))

    j = lax.broadcasted_iota(jnp.int32, (SEL_LANES, QB), 0)
    cur = t_row // SEL_BLOCK
    valid = j * SEL_BLOCK <= t_row
    forced = (j == 0) | (j == cur) | (j == cur - 1)
    score = jnp.where(valid, imp_t + jnp.where(forced, FORCE_BONUS, 0.0), -1.0)
    for _ in range(n_pick):
        m = jnp.max(score, axis=0, keepdims=True)
        idx = jnp.min(jnp.where(score == m, j, SEL_LANES), axis=0, keepdims=True)
        score = jnp.where(j == idx, -2.0, score)
    blockbias_ref[...] = jnp.where((score == -2.0) & valid, 0.0, -MASK_BIG)

    KT = SEL_KEY_TILE
    blocks_per_tile = KT // SEL_BLOCK
    n_tiles = (qb * QB + QB + KT - 1) // KT

    def sel_step(kt, carry, causal):
        m_i, acc = carry
        start = pl.multiple_of(kt * KT, KT)
        s_t = jnp.dot(ks_ref[pl.ds(start, KT), :], q4, preferred_element_type=F32)
        bias = jnp.concatenate(
            [jnp.broadcast_to(blockbias_ref[pl.ds(kt * blocks_per_tile + jb, 1), :], (SEL_BLOCK, QB))
             for jb in range(blocks_per_tile)], axis=0)
        if causal:
            kpos = start + lax.broadcasted_iota(jnp.int32, (KT, 1), 0)
            bias = jnp.where(kpos <= t_row, bias, -MASK_BIG)
        sm = s_t + lanes4(bias)
        m_new = jnp.maximum(m_i, jnp.max(sm, axis=0, keepdims=True))
        e = jnp.exp2(sm - m_new).astype(BF16)
        acc_new = jnp.exp2(m_i - m_new) * acc + jnp.dot(vst_ref[:, pl.ds(start, KT)], e,
                                                        preferred_element_type=F32)
        return m_new, acc_new

    init = (jnp.full((1, HP * QB), MAX_FLOOR, F32), jnp.zeros((VT_ROWS, HP * QB), F32))
    carry = lax.fori_loop(0, n_tiles - 1, lambda kt, c: sel_step(kt, c, False), init)
    _, acc_s = sel_step(n_tiles - 1, carry, True)
    den_s = acc_s[D:D + 1]
    o_s = acc_s[:D] * (1.0 / jnp.where(den_s > 0.0, den_s, 1.0))

    span = WINDOW + QB
    w_start = pl.multiple_of(jnp.maximum(qb * QB - WINDOW, 0), QB)
    dist = t_row - (w_start + lax.broadcasted_iota(jnp.int32, (span, 1), 0))
    e_w = softmax_cols(jnp.dot(kw_ref[pl.ds(w_start, span), :], q4, preferred_element_type=F32),
                       jnp.where((dist >= 0) & (dist < WINDOW), 0.0, -MASK_BIG))
    acc_w = jnp.dot(vwt_ref[:, pl.ds(w_start, span)], e_w.astype(BF16), preferred_element_type=F32)
    den_w = acc_w[D:D + 1]
    o_w = acc_w[:D] * (1.0 / jnp.where(den_w > 0.0, den_w, 1.0))

    gates = _sigmoid(gate_ref[...])
    gate_row = lambda br: jnp.concatenate([gates[3 * n + br:3 * n + br + 1, :] for n in range(HP)], axis=1)
    o_t = gate_row(0) * o_c + gate_row(1) * o_s + gate_row(2) * o_w
    for n in range(HP):
        o_ref[:, n * D:(n + 1) * D] = o_t[:, n * QB:(n + 1) * QB].T


def _cmp_to_sel_matrix(n_cmp_rows, n_sel):
    ratio = SEL_BLOCK // CMP_STRIDE
    ci = np.arange(n_cmp_rows)[:, None]
    sj = np.arange(SEL_LANES)[None, :]
    m = sum(((ci + n) // ratio == sj).astype(np.float32) for n in range(CMP_BLOCK // CMP_STRIDE))
    m = m * (sj < n_sel) * (ci < n_cmp_rows - 1)
    return jnp.asarray(m.T, BF16)


def _nsa_attention(qg_t, kc, vc, ks, vst, kw, vwt, cos_t, sin_t):
    B, _, S = qg_t.shape
    n_sub = kc.shape[2]
    n_sel = S // SEL_BLOCK
    gw = NSA_HPG * HEAD_DIM
    gate_row0 = NSA_WIDTH // 128
    msel_t = _cmp_to_sel_matrix(n_sub, n_sel)
    at_bg = lambda shape: pl.BlockSpec((None, None) + shape, lambda b, g, i: (b, g, 0, 0))
    const = lambda a: pl.BlockSpec(a.shape, lambda b, g, i: (0, 0))
    return pl.pallas_call(
        functools.partial(_nsa_kernel, n_pick=min(N_SELECT, n_sel)),
        grid=(B, NSA_GROUPS, S // NSA_QUERY_TILE),
        in_specs=[pl.BlockSpec((None, gw, NSA_QUERY_TILE), lambda b, g, i: (b, g, i)),
                  pl.BlockSpec((None, 128, NSA_QUERY_TILE), lambda b, g, i: (b, gate_row0 + g, i)),
                  pl.BlockSpec((HEAD_DIM, NSA_QUERY_TILE), lambda b, g, i: (0, i)),
                  pl.BlockSpec((HEAD_DIM, NSA_QUERY_TILE), lambda b, g, i: (0, i)),
                  at_bg((n_sub, HEAD_DIM)), at_bg((n_sub, HEAD_DIM)),
                  at_bg((S, HEAD_DIM)), at_bg((VT_ROWS, S)), at_bg((S, HEAD_DIM)), at_bg((VT_ROWS, S)),
                  const(msel_t)],
        out_specs=pl.BlockSpec((None, NSA_QUERY_TILE, gw), lambda b, g, i: (b, i, g)),
        out_shape=jax.ShapeDtypeStruct((B, S, NSA_WIDTH), F32),
        scratch_shapes=[pltpu.VMEM((SEL_LANES, NSA_QUERY_TILE), F32)],
        compiler_params=pltpu.CompilerParams(
            dimension_semantics=("parallel", "parallel", "arbitrary"), vmem_limit_bytes=VMEM_LIMIT),
        name="nsa_attention",
    )(qg_t, qg_t, cos_t, sin_t, kc, vc, ks, vst, kw, vwt, msel_t)


def _nsa_branch(p_kv, qg_t, cmp_pe_k, cmp_w1_k, cmp_w2_k, cmp_pe_v, cmp_w1_v, cmp_w2_v):
    B, S, _ = p_kv.shape
    pos = jnp.arange(S)
    cos2, sin2 = _rope_tables(pos, NSA_GROUPS)
    kc_raw, vc_raw, ks, vst, kw, vwt = _kv_layout(p_kv, cos2, sin2)
    n_sub = S // CMP_STRIDE
    sub = lambda t: t.reshape(B, NSA_GROUPS, n_sub, CMP_STRIDE * HEAD_DIM)
    cos_c, sin_c = _rope_tables(jnp.arange(n_sub) * CMP_STRIDE + CMP_BLOCK - 1, 1)
    kc, vc = _compress(sub(kc_raw), sub(vc_raw), cmp_pe_k, cmp_w1_k, cmp_w2_k,
                       cmp_pe_v, cmp_w1_v, cmp_w2_v, cos_c, sin_c)
    cos_q, sin_q = _rope_tables(pos, 1)
    return _nsa_attention(qg_t, kc, vc, ks, vst, kw, vwt, cos_q.T, sin_q.T)


def _nsa_weight_columns(w_nsa):
    K = w_nsa.shape[0]
    q = w_nsa[:, :NSA_WIDTH]
    kv = w_nsa[:, NSA_WIDTH:NSA_WIDTH + 6 * NSA_KV_WIDTH]
    gates = w_nsa[:, NSA_WIDTH + 6 * NSA_KV_WIDTH:]
    per_group = NSA_HPG * 3
    gate_blocks = [jnp.pad(gates[:, g * per_group:(g + 1) * per_group], ((0, 0), (0, 128 - per_group)))
                   for g in range(NSA_GROUPS)]
    return jnp.concatenate([kv, q] + gate_blocks, axis=1)


def _layer_norm(h, g, b):
    mu = jnp.mean(h, axis=-1, keepdims=True)
    var = jnp.mean(jnp.square(h - mu), axis=-1, keepdims=True)
    return (h - mu) * lax.rsqrt(var + LN_EPS) * g + b


def _pack_bf16_halves(x):
    n = x.shape[-1] // 2
    bits = lax.bitcast_convert_type(x.astype(BF16).astype(F32), jnp.uint32)
    return (bits[:, n:] & jnp.uint32(0xFFFF0000)) | (bits[:, :n] >> 16)


def _unpack_bf16_halves(u):
    left = lax.bitcast_convert_type(u << 16, F32)
    right = lax.bitcast_convert_type(u & jnp.uint32(0xFFFF0000), F32)
    return left, right


def _mixer_out_kernel(x_ref, ya_ref, yb_ref, pg_ref, wa_ref, wb_ref, wo_ref, g_ref, b_ref, o_ref, op_ref,
                      *, alpha):
    d = x_ref.shape[-1]
    gate_a = _sigmoid(pg_ref[:, :d].astype(F32))
    gate_b = _sigmoid(pg_ref[:, d:].astype(F32))
    mixed = gate_a * _bdot(ya_ref[...], wa_ref[...]) + gate_b * _bdot(yb_ref[...], wb_ref[...])
    h = alpha * x_ref[...] + _bdot(mixed, wo_ref[...])
    out = _layer_norm(h, g_ref[...], b_ref[...])
    o_ref[...] = out
    op_ref[...] = _pack_bf16_halves(out)


def _mixer_out(xf, ya, yb, p_g, wa, wb, wo, ln_g, ln_b, alpha, tm=512):
    T, D = xf.shape
    rows = lambda w: pl.BlockSpec((tm, w), lambda i: (i, 0))
    full = lambda a: pl.BlockSpec(a.shape, lambda i: (0,) * a.ndim)
    ln_g, ln_b = ln_g.reshape(1, D), ln_b.reshape(1, D)
    return pl.pallas_call(
        functools.partial(_mixer_out_kernel, alpha=alpha),
        grid=(T // tm,),
        in_specs=[rows(D), rows(ya.shape[1]), rows(yb.shape[1]), rows(2 * D),
                  full(wa), full(wb), full(wo), full(ln_g), full(ln_b)],
        out_specs=[rows(D), rows(D // 2)],
        out_shape=[jax.ShapeDtypeStruct((T, D), F32), jax.ShapeDtypeStruct((T, D // 2), jnp.uint32)],
        compiler_params=pltpu.CompilerParams(
            dimension_semantics=("parallel",), vmem_limit_bytes=VMEM_LIMIT),
        name="mixer_out_ln",
    )(xf, ya, yb, p_g, wa, wb, wo, ln_g, ln_b)


ROUTER_TILE = 256
EXPERT_ROWS = 256
SC_TOKEN_CHUNK = 64
SC_ROW_CHUNK = 64
PICK_LANES = 128
LOWEST = -3.0e38


def _router_kernel(x_ref, rwt_ref, bias_ref, eidx_ref, wts_ref, pos_ref, cnt_ref, carry_ref):
    tm, E = x_ref.shape[0], rwt_ref.shape[0]
    per_group = E // N_GROUPS
    reps = tm // PICK_LANES

    @pl.when(pl.program_id(0) == 0)
    def _():
        carry_ref[...] = jnp.zeros_like(carry_ref)

    scores = _sigmoid(_dot3(rwt_ref[...], x_ref[...], (((1,), (1,)), ((), ()))))
    choice = scores + jnp.concatenate([bias_ref[...]] * reps, axis=1)
    row = lax.broadcasted_iota(jnp.int32, (E, tm), 0)

    def first_max(vals, rows):
        m = jnp.max(vals, axis=0, keepdims=True)
        return m, jnp.min(jnp.where(vals == m, rows, E), axis=0, keepdims=True)

    group_score = []
    for g in range(N_GROUPS):
        rows = slice(g * per_group, (g + 1) * per_group)
        group_row = g * per_group + lax.broadcasted_iota(jnp.int32, (per_group, tm), 0)
        m1, i1 = first_max(choice[rows], group_row)
        m2 = jnp.max(jnp.where(group_row == i1, LOWEST, choice[rows]), axis=0, keepdims=True)
        group_score.append(m1 + m2)
    masked = []
    for g in range(N_GROUPS):
        rank = jnp.zeros((1, tm), jnp.int32)
        for o in range(N_GROUPS):
            if o != g:
                ahead = (group_score[o] > group_score[g]) if o > g else (group_score[o] >= group_score[g])
                rank = rank + ahead.astype(jnp.int32)
        masked.append(jnp.where(rank < TOPK_GROUPS, choice[g * per_group:(g + 1) * per_group], NEG_INF))

    cur = jnp.concatenate(masked, axis=0)
    picks = []
    for _ in range(TOP_K):
        _, idx = first_max(cur, row)
        picks.append(idx)
        cur = jnp.where(row == idx, LOWEST, cur)
    sel = jnp.where(cur == LOWEST, 1.0, 0.0)
    gate = scores * sel
    gate = gate * (ROUTED_SCALE / jnp.sum(gate, axis=0, keepdims=True))

    ti = lax.broadcasted_iota(jnp.int32, (tm, tm), 0)
    tj = lax.broadcasted_iota(jnp.int32, (tm, tm), 1)
    sel_b = sel.astype(BF16)
    before = jnp.dot(sel_b, (ti < tj).astype(BF16), preferred_element_type=F32)
    queue_pos = before + jnp.concatenate([carry_ref[...]] * reps, axis=1)
    carry_ref[...] = carry_ref[...] + jnp.dot(sel_b, jnp.ones((tm, PICK_LANES), BF16),
                                              preferred_element_type=F32)
    cnt_ref[...] = carry_ref[...]

    at_pick = lambda vals, idx: jnp.sum(jnp.where(row == idx, vals, 0.0), axis=0, keepdims=True)
    eidx_ref[...] = jnp.concatenate(picks, axis=0)
    wts_ref[...] = jnp.concatenate([at_pick(gate, idx) for idx in picks], axis=0)
    pos_ref[...] = jnp.concatenate([at_pick(queue_pos, idx) for idx in picks], axis=0).astype(jnp.int32)


def _router(xf, router_w, router_bias):
    T, D = xf.shape
    E = router_w.shape[1]
    tm = ROUTER_TILE
    picks = lambda dt: jax.ShapeDtypeStruct((TOP_K, T), dt)
    pick_spec = pl.BlockSpec((TOP_K, tm), lambda i: (0, i))
    lanes = lambda v: jnp.broadcast_to(v.reshape(E, 1), (E, PICK_LANES))
    return pl.pallas_call(
        _router_kernel,
        grid=(T // tm,),
        in_specs=[pl.BlockSpec((tm, D), lambda i: (i, 0)),
                  pl.BlockSpec((E, D), lambda i: (0, 0)),
                  pl.BlockSpec((E, PICK_LANES), lambda i: (0, 0))],
        out_specs=[pick_spec, pick_spec, pick_spec, pl.BlockSpec((E, PICK_LANES), lambda i: (0, 0))],
        out_shape=[picks(jnp.int32), picks(F32), picks(jnp.int32),
                   jax.ShapeDtypeStruct((E, PICK_LANES), F32)],
        scratch_shapes=[pltpu.VMEM((E, PICK_LANES), F32)],
        compiler_params=pltpu.CompilerParams(
            dimension_semantics=("arbitrary",), vmem_limit_bytes=VMEM_LIMIT),
        name="moe_router",
    )(xf, router_w.T, lanes(router_bias))


def _dest_kernel(eidx_ref, pos_ref, start_ref, dest_ref):
    E = start_ref.shape[0]
    tm = eidx_ref.shape[1]
    row = lax.broadcasted_iota(jnp.int32, (E, tm), 0)
    start = jnp.concatenate([start_ref[...]] * (tm // PICK_LANES), axis=1)
    eidx = eidx_ref[...]
    base = [jnp.sum(jnp.where(row == eidx[kk:kk + 1, :], start, 0), axis=0, keepdims=True)
            for kk in range(TOP_K)]
    dest_ref[...] = jnp.concatenate(base, axis=0) + pos_ref[...]


def _dest_rows(eidx_t, pos_t, pad_start):
    T = eidx_t.shape[1]
    E = pad_start.shape[0]
    tm = ROUTER_TILE
    pick_spec = pl.BlockSpec((TOP_K, tm), lambda i: (0, i))
    return pl.pallas_call(
        _dest_kernel,
        grid=(T // tm,),
        in_specs=[pick_spec, pick_spec, pl.BlockSpec((E, PICK_LANES), lambda i: (0, 0))],
        out_specs=pick_spec,
        out_shape=jax.ShapeDtypeStruct((TOP_K, T), jnp.int32),
        compiler_params=pltpu.CompilerParams(
            dimension_semantics=("parallel",), vmem_limit_bytes=VMEM_LIMIT),
        name="moe_dest_rows",
    )(eidx_t, pos_t, jnp.broadcast_to(pad_start.reshape(E, 1), (E, PICK_LANES)))


def _sc_mesh():
    return plsc.VectorSubcoreMesh(core_axis_name="c", subcore_axis_name="s")


def _sc_scatter_rows(x, dest_t, n_rows):
    T, D = x.shape
    K = dest_t.shape[0]
    mesh = _sc_mesh()
    nc, nw = mesh.num_cores, mesh.num_cores * mesh.num_subcores
    per_w = T // nw
    chunk = min(SC_TOKEN_CHUNK, per_w)
    n_chunks = per_w // chunk
    idx = dest_t.reshape(K, nw, n_chunks, chunk).transpose(1, 2, 0, 3).reshape(nw, n_chunks * K, chunk)

    assert n_chunks % 2 == 0

    @functools.partial(
        pl.kernel, mesh=mesh,
        out_type=jax.ShapeDtypeStruct((n_rows, D), x.dtype),
        scratch_types=[pltpu.VMEM((n_chunks * K, chunk), jnp.int32),
                       pltpu.VMEM((2, chunk, D), x.dtype),
                       pltpu.SemaphoreType.DMA((2,)), pltpu.SemaphoreType.DMA((2,))],
    )
    def scatter(x_hbm, idx_hbm, out_hbm, idx_v, rows_v, load_sem, send_sem):
        wid = lax.axis_index("s") * nc + lax.axis_index("c")
        pltpu.sync_copy(idx_hbm.at[wid], idx_v)

        def load(j, b):
            return pltpu.make_async_copy(x_hbm.at[pl.ds(wid * per_w + j * chunk, chunk)], rows_v.at[b],
                                         load_sem.at[b])

        def sends(j, b):
            return [pltpu.make_async_copy(rows_v.at[b], out_hbm.at[idx_v.at[j * K + kk]], send_sem.at[b])
                    for kk in range(K)]

        load(0, 0).start()

        @pl.loop(0, n_chunks, step=2)
        def _(j0):
            for b in range(2):
                j = j0 + b
                load(j, b).wait()

                @pl.when(j >= 1)
                def _():
                    for c in sends(j - 1, 1 - b):
                        c.wait()

                @pl.when(j + 1 < n_chunks)
                def _():
                    load(j + 1, 1 - b).start()

                for c in sends(j, b):
                    c.start()

        for c in sends(n_chunks - 1, (n_chunks - 1) % 2):
            c.wait()

    return scatter(x, idx)


def _sc_gather_rows(src, idx):
    M = idx.shape[0]
    D = src.shape[1]
    mesh = _sc_mesh()
    nc, nw = mesh.num_cores, mesh.num_cores * mesh.num_subcores
    per_w = M // nw
    chunk = min(SC_ROW_CHUNK, per_w)
    n_chunks = per_w // chunk

    assert n_chunks % 2 == 0

    @functools.partial(
        pl.kernel, mesh=mesh,
        out_type=jax.ShapeDtypeStruct((M, D), src.dtype),
        scratch_types=[pltpu.VMEM((n_chunks, chunk), jnp.int32),
                       pltpu.VMEM((2, chunk, D), src.dtype),
                       pltpu.SemaphoreType.DMA((2,)), pltpu.SemaphoreType.DMA((2,))],
    )
    def gather(src_hbm, idx_hbm, out_hbm, idx_v, rows_v, fetch_sem, store_sem):
        wid = lax.axis_index("s") * nc + lax.axis_index("c")
        pltpu.sync_copy(idx_hbm.at[wid], idx_v)

        def fetch(j, b):
            return pltpu.make_async_copy(src_hbm.at[idx_v.at[j]], rows_v.at[b], fetch_sem.at[b])

        def store(j, b):
            return pltpu.make_async_copy(rows_v.at[b], out_hbm.at[pl.ds(wid * per_w + j * chunk, chunk)],
                                         store_sem.at[b])

        fetch(0, 0).start()

        @pl.loop(0, n_chunks, step=2)
        def _(j0):
            for b in range(2):
                j = j0 + b
                fetch(j, b).wait()

                @pl.when(j >= 1)
                def _():
                    store(j - 1, 1 - b).wait()

                @pl.when(j + 1 < n_chunks)
                def _():
                    fetch(j + 1, 1 - b).start()

                store(j, b).start()

        store(n_chunks - 1, (n_chunks - 1) % 2).wait()

    return gather(src, idx.reshape(nw, n_chunks, chunk))


def _expert_kernel(distinct_e_ref, blk_ord_ref, blk_new_ref, blk_rows_ref, n_used_ref, n_distinct_ref,
                   x_ref, wgu_hbm, wd_hbm, o_ref, wgu_buf, wd_buf, wgu_bf, wd_bf, sem):
    i = pl.program_id(0)
    live = i < n_used_ref[0]
    ordinal = blk_ord_ref[i]
    slot = ordinal % 2

    def weight_copies(k, s):
        e = distinct_e_ref[k]
        return (pltpu.make_async_copy(wgu_hbm.at[e], wgu_buf.at[s], sem.at[0, s]),
                pltpu.make_async_copy(wd_hbm.at[e], wd_buf.at[s], sem.at[1, s]))

    @pl.when(i == 0)
    def _():
        for c in weight_copies(0, 0):
            c.start()

    @pl.when(live & (blk_new_ref[i] == 1))
    def _():
        for c in weight_copies(ordinal, slot):
            c.wait()

        @pl.when(ordinal + 1 < n_distinct_ref[0])
        def _():
            for c in weight_copies(ordinal + 1, 1 - slot):
                c.start()

        wgu_bf[...] = wgu_buf[slot].astype(BF16)
        wd_bf[...] = wd_buf[slot].astype(BF16)

    @pl.when(live)
    def _():
        hidden = wd_bf.shape[0]
        half = x_ref.shape[1]
        row = lax.broadcasted_iota(jnp.int32, x_ref.shape, 0)
        left, right = _unpack_bf16_halves(x_ref[...])
        real = row < blk_rows_ref[i]
        left = jnp.where(real, left, 0.0).astype(BF16)
        right = jnp.where(real, right, 0.0).astype(BF16)
        h = (jnp.dot(left, wgu_bf[:half, :], preferred_element_type=F32)
             + jnp.dot(right, wgu_bf[half:, :], preferred_element_type=F32))
        gate, up = h[:, :hidden], h[:, hidden:]
        act = (gate * _sigmoid(gate) * up).astype(BF16)
        o_ref[...] = _pack_bf16_halves(jnp.dot(act, wd_bf[...], preferred_element_type=F32))

    @pl.when(jnp.logical_not(live))
    def _():
        o_ref[...] = jnp.zeros_like(o_ref)


def _expert_ffn(xs, blk_e, blk_rows, n_used, w_gu, w_down):
    n_rows, half = xs.shape
    E, D, two_h = w_gu.shape
    n_blocks = n_rows // EXPERT_ROWS
    idx = jnp.arange(n_blocks, dtype=jnp.int32)
    is_live = idx < n_used[0]
    blk_new = (is_live & ((idx == 0) | (blk_e != jnp.roll(blk_e, 1)))).astype(jnp.int32)
    blk_ord = (jnp.cumsum(blk_new) - 1).astype(jnp.int32)
    n_distinct = blk_ord[-1:] + 1
    first_of = (blk_new[None, :] == 1) & (blk_ord[None, :] == idx[:, None])
    distinct_e = jnp.sum(jnp.where(first_of, blk_e[None, :], 0), axis=1).astype(jnp.int32)

    live = lambda i, nu: jnp.minimum(i, nu[0] - 1)
    grid_spec = pltpu.PrefetchScalarGridSpec(
        num_scalar_prefetch=6,
        grid=(n_blocks,),
        in_specs=[pl.BlockSpec((EXPERT_ROWS, half), lambda i, de, bo, bn, br, nu, nd: (live(i, nu), 0)),
                  pl.BlockSpec(memory_space=pl.ANY), pl.BlockSpec(memory_space=pl.ANY)],
        out_specs=pl.BlockSpec((EXPERT_ROWS, half), lambda i, de, bo, bn, br, nu, nd: (i, 0)),
        scratch_shapes=[pltpu.VMEM((2, D, two_h), F32), pltpu.VMEM((2, two_h // 2, D), F32),
                        pltpu.VMEM((D, two_h), BF16), pltpu.VMEM((two_h // 2, D), BF16),
                        pltpu.SemaphoreType.DMA((2, 2))],
    )
    return pl.pallas_call(
        _expert_kernel,
        grid_spec=grid_spec,
        out_shape=jax.ShapeDtypeStruct((n_rows, half), jnp.uint32),
        compiler_params=pltpu.CompilerParams(
            dimension_semantics=("arbitrary",), vmem_limit_bytes=VMEM_LIMIT),
        name="moe_experts",
    )(distinct_e, blk_ord, blk_new, blk_rows, n_used, n_distinct, xs, w_gu, w_down)


def _moe_out_kernel(x_ref, yk_ref, wts_ref, sgu_ref, sd_ref, g_ref, b_ref, o_ref, *, alpha):
    x = x_ref[...]
    hidden = sd_ref.shape[0]
    h = _bdot(x, sgu_ref[...])
    gate, up = h[:, :hidden], h[:, hidden:]
    ffn = _bdot(gate * _sigmoid(gate) * up, sd_ref[...])
    wts = wts_ref[...]
    routed_left = routed_right = None
    for kk in range(TOP_K):
        left, right = _unpack_bf16_halves(yk_ref[kk])
        w = wts[:, kk:kk + 1]
        routed_left = w * left if kk == 0 else routed_left + w * left
        routed_right = w * right if kk == 0 else routed_right + w * right
    ffn = ffn + jnp.concatenate([routed_left, routed_right], axis=-1)
    o_ref[...] = _layer_norm(alpha * x + ffn, g_ref[...], b_ref[...])


def _moe_out(xf, yk, wts, sw_gu, sw_down, ln_g, ln_b, alpha, tm=128):
    T, D = xf.shape
    rows = lambda w: pl.BlockSpec((tm, w), lambda i: (i, 0))
    full = lambda a: pl.BlockSpec(a.shape, lambda i: (0,) * a.ndim)
    ln_g, ln_b = ln_g.reshape(1, D), ln_b.reshape(1, D)
    return pl.pallas_call(
        functools.partial(_moe_out_kernel, alpha=alpha),
        grid=(T // tm,),
        in_specs=[rows(D), pl.BlockSpec((TOP_K, tm, D // 2), lambda i: (0, i, 0)), rows(PICK_LANES),
                  full(sw_gu), full(sw_down), full(ln_g), full(ln_b)],
        out_specs=rows(D),
        out_shape=jax.ShapeDtypeStruct((T, D), F32),
        compiler_params=pltpu.CompilerParams(
            dimension_semantics=("parallel",), vmem_limit_bytes=VMEM_LIMIT),
        name="moe_combine_ln",
    )(xf, yk, wts, sw_gu, sw_down, ln_g, ln_b)


def _moe_ffn_ln(xf, xp, router_w, router_bias, w_gu, w_down, sw_gu, sw_down, ln_g, ln_b, alpha):
    T, D = xf.shape
    E = router_w.shape[1]
    BM = EXPERT_ROWS
    eidx_t, wts_t, pos_t, cnt = _router(xf, router_w, router_bias)
    counts = cnt[:, 0].astype(jnp.int32)
    padded = (counts + BM - 1) // BM * BM
    pad_end = jnp.cumsum(padded)
    pad_start = pad_end - padded
    n_rows = T * TOP_K + E * BM
    n_blocks = n_rows // BM
    blk_row0 = jnp.arange(n_blocks, dtype=jnp.int32) * BM
    blk_e = jnp.minimum(jnp.sum((pad_end[None, :] <= blk_row0[:, None]).astype(jnp.int32), axis=1), E - 1)
    blk_rows = jnp.clip(pad_start[blk_e] + counts[blk_e] - blk_row0, 0, BM).astype(jnp.int32)
    n_used = (pad_end[-1:] // BM).astype(jnp.int32)
    dest_t = _dest_rows(eidx_t, pos_t, pad_start)
    wts = jnp.pad(wts_t.T, ((0, 0), (0, PICK_LANES - TOP_K)))
    xs = _sc_scatter_rows(xp, dest_t, n_rows)
    ys = _expert_ffn(xs, blk_e, blk_rows, n_used, w_gu, w_down)
    yk = _sc_gather_rows(ys, dest_t.reshape(-1)).reshape(TOP_K, T, D // 2)
    return _moe_out(xf, yk, wts, sw_gu, sw_down, ln_g, ln_b, alpha)


def kernel(x, w_in, tshift_mu, rwkv_w0, rwkv_w2, rwkv_a0, rwkv_a2, rwkv_g2, rwkv_k_k, rwkv_k_a, rwkv_r_k, rwkv_lnx_w, rwkv_lnx_b, cmp_pe_k, cmp_w1_k, cmp_w2_k, cmp_pe_v, cmp_w1_v, cmp_w2_v, w_branch_a, w_branch_b, w_out, ln1_g, ln1_b, router_w, router_bias, exp_w_gu, exp_w_down, shared_w_gu, shared_w_down, ln2_g, ln2_b):
    B, S, D = x.shape
    depth = w_in.shape[0]
    alpha = (2 * depth) ** 0.25
    nsa_w = w_in.shape[2] - RWKV_IN_W - 2 * D
    for l in range(depth):
        xf = x.reshape(B * S, D)
        w_l = w_in[l]
        w_a = w_l[:, :RWKV_IN_W].astype(BF16)
        w_b = _nsa_weight_columns(w_l[:, RWKV_IN_W:RWKV_IN_W + nsa_w]).astype(BF16)
        w_g = w_l[:, RWKV_IN_W + nsa_w:].astype(BF16)
        kv_w = 6 * NSA_KV_WIDTH
        p_a = _matmul(xf, w_a, PROJ_ROWS, w_a.shape[1]).reshape(B, S, -1)
        p_kv = _matmul(xf, w_b[:, :kv_w], PROJ_ROWS, kv_w).reshape(B, S, -1)
        qg_t = _matmul_t(x, w_b[:, kv_w:].T, PROJ_ROWS)
        p_g = _matmul(xf, w_g, PROJ_ROWS, w_g.shape[1], BF16)
        y_a = _rwkv_time_mix(p_a, tshift_mu[l], rwkv_w0[l], rwkv_w2[l], rwkv_a0[l], rwkv_a2[l], rwkv_g2[l],
                             rwkv_k_k[l], rwkv_k_a[l], rwkv_r_k[l].reshape(-1), rwkv_lnx_w[l], rwkv_lnx_b[l])
        y_b = _nsa_branch(p_kv, qg_t, cmp_pe_k[l], cmp_w1_k[l], cmp_w2_k[l],
                          cmp_pe_v[l], cmp_w1_v[l], cmp_w2_v[l])
        x1, x1p = _mixer_out(xf, y_a.reshape(B * S, -1), y_b.reshape(B * S, -1), p_g,
                             w_branch_a[l].astype(BF16), w_branch_b[l].astype(BF16), w_out[l].astype(BF16),
                             ln1_g[l], ln1_b[l], alpha)
        x2 = _moe_ffn_ln(x1, x1p, router_w[l], router_bias[l], exp_w_gu[l], exp_w_down[l],
                         shared_w_gu[l].astype(BF16), shared_w_down[l].astype(BF16), ln2_g[l], ln2_b[l], alpha)
        x = x2.reshape(B, S, D)
    return x
```

```python
import functools

import numpy as np
import jax
import jax.numpy as jnp
from jax import lax
from jax.experimental import pallas as pl
from jax.experimental.pallas import tpu as pltpu
from jax.experimental.pallas import tpu_sc as plsc

F32 = jnp.float32
BF16 = jnp.bfloat16

RWKV_HEADS = 8
HEAD_DIM = 64
RWKV_WIDTH = RWKV_HEADS * HEAD_DIM
W_LORA = 64
A_LORA = 64
G_LORA = 128
GN_EPS = 64e-5
NSA_HEADS = 8
NSA_GROUPS = 2
NSA_HPG = NSA_HEADS // NSA_GROUPS
NSA_WIDTH = NSA_HEADS * HEAD_DIM
NSA_KV_WIDTH = NSA_GROUPS * HEAD_DIM
CMP_BLOCK = 32
CMP_STRIDE = 16
CMP_HIDDEN = 256
SEL_BLOCK = 64
N_SELECT = 16
WINDOW = 512
ROPE_THETA = 10000.0
RWKV_IN_W = 3 * RWKV_WIDTH + W_LORA + A_LORA + G_LORA
N_EXPERTS = 256
TOP_K = 8
N_GROUPS = 8
TOPK_GROUPS = 4
EXPERT_DIM = 256
ROUTED_SCALE = 2.5
LN_EPS = 1e-5
NEG_INF = -1e30
FORCE_BONUS = 1e4

RWKV_CHUNK = 64
RWKV_HEAD_GROUP = 4
RWKV_STEP_CHUNKS = 2
VMEM_LIMIT = 56 * 1024 * 1024
PROJ_ROWS = 1024


def _bdot(a, b):
    return jnp.dot(a.astype(BF16), b.astype(BF16), preferred_element_type=F32)


def _bdot_nt(a, b):
    return lax.dot_general(a.astype(BF16), b.astype(BF16), (((1,), (1,)), ((), ())),
                           preferred_element_type=F32)


def _bdot_tn(a, b):
    return lax.dot_general(a.astype(BF16), b.astype(BF16), (((0,), (0,)), ((), ())),
                           preferred_element_type=F32)


def _bf16_pieces(x, n):
    pieces = []
    for _ in range(n):
        p = x.astype(BF16)
        pieces.append(p)
        x = x - p.astype(F32)
    return pieces


def _dot3(a, b, dims=(((1,), (0,)), ((), ()))):
    (a_hi, a_lo), (b_hi, b_lo) = _bf16_pieces(a, 2), _bf16_pieces(b, 2)
    dot = lambda p, q: lax.dot_general(p, q, dims, preferred_element_type=F32)
    return dot(a_hi, b_hi) + (dot(a_hi, b_lo) + dot(a_lo, b_hi))


def _sigmoid(x):
    return 1.0 / (1.0 + jnp.exp(-x))


def _matmul_kernel(x_ref, w_ref, o_ref):
    o_ref[...] = jnp.dot(x_ref[...].astype(BF16), w_ref[...], preferred_element_type=F32).astype(o_ref.dtype)


def _matmul(x, w, tm, tn, out_dtype=F32):
    M, K = x.shape
    N = w.shape[1]
    return pl.pallas_call(
        _matmul_kernel,
        grid=(M // tm, N // tn),
        in_specs=[pl.BlockSpec((tm, K), lambda i, j: (i, 0)),
                  pl.BlockSpec((K, tn), lambda i, j: (0, j))],
        out_specs=pl.BlockSpec((tm, tn), lambda i, j: (i, j)),
        out_shape=jax.ShapeDtypeStruct((M, N), out_dtype),
        compiler_params=pltpu.CompilerParams(
            dimension_semantics=("parallel", "parallel"), vmem_limit_bytes=VMEM_LIMIT),
        name="dense_proj",
    )(x, w)


def _matmul_t_kernel(x_ref, wt_ref, o_ref):
    o_ref[...] = lax.dot_general(wt_ref[...], x_ref[...].astype(BF16), (((1,), (1,)), ((), ())),
                                 preferred_element_type=F32)


def _matmul_t(x, w_t, tm):
    B, S, K = x.shape
    N = w_t.shape[0]
    return pl.pallas_call(
        _matmul_t_kernel,
        grid=(B, S // tm),
        in_specs=[pl.BlockSpec((None, tm, K), lambda b, s: (b, s, 0)),
                  pl.BlockSpec((N, K), lambda b, s: (0, 0))],
        out_specs=pl.BlockSpec((None, N, tm), lambda b, s: (b, 0, s)),
        out_shape=jax.ShapeDtypeStruct((B, N, S), F32),
        compiler_params=pltpu.CompilerParams(
            dimension_semantics=("parallel", "parallel"), vmem_limit_bytes=VMEM_LIMIT),
        name="dense_proj_t",
    )(x, w_t)


def _rwkv_kernel(p_ref, mu_ref, w0_ref, w2_ref, a0_ref, a2_ref, g2_ref, kk_ref, ka_ref, rk_ref,
                 lnw_ref, lnb_ref, o_ref, carry_ref, state_ref):
    C, H, N = RWKV_CHUNK, RWKV_HEADS, HEAD_DIM
    W = RWKV_WIDTH
    B = p_ref.shape[0]
    NC = p_ref.shape[1] // C
    L = NC * C
    R = B * L

    @pl.when(pl.program_id(0) == 0)
    def _():
        carry_ref[...] = jnp.zeros_like(carry_ref)
        state_ref[...] = jnp.zeros_like(state_ref)

    def per_block(x, rows):
        return jnp.concatenate(
            [jnp.broadcast_to(x[i].reshape(1, -1), (rows, x.shape[-1])) for i in range(x.shape[0])], axis=0)

    p = p_ref[...].reshape(R, p_ref.shape[-1])
    row = lax.broadcasted_iota(jnp.int32, p.shape, 0)
    prev = jnp.where(row % L == 0, per_block(carry_ref[...], L), pltpu.roll(p, 1, axis=0))
    for b in range(B):
        carry_ref[b] = p[b * L + L - 1:b * L + L, :]
    xs = p + (prev - p) * mu_ref[...]
    r = xs[:, 0:W]
    k = xs[:, W:2 * W]
    v = xs[:, 2 * W:3 * W]
    wl = xs[:, 3 * W:3 * W + W_LORA]
    al = xs[:, 3 * W + W_LORA:3 * W + W_LORA + A_LORA]
    gl = xs[:, 3 * W + W_LORA + A_LORA:]

    z = -(w0_ref[...] + _dot3(jnp.tanh(wl), w2_ref[...]))
    softplus = jnp.maximum(z, 0.0) + jnp.log1p(jnp.exp(-jnp.abs(z)))
    logd = -jnp.exp(-softplus - 0.5)
    a = _sigmoid(a0_ref[...] + _dot3(al, a2_ref[...]))
    g = _dot3(_sigmoid(gl), g2_ref[...])

    kk = k * kk_ref[...]
    knew = k * (1.0 + (a - 1.0) * ka_ref[...])

    HG = RWKV_HEAD_GROUP
    GW = HG * N
    same_head_lanes = (lax.broadcasted_iota(jnp.int32, (GW, GW), 0) // N
                       == lax.broadcasted_iota(jnp.int32, (GW, GW), 1) // N)
    head_ones = jnp.where(same_head_lanes, 1.0, 0.0).astype(BF16)

    def head_sum(x):
        hi = x.astype(BF16)
        lo = (x - hi.astype(F32)).astype(BF16)
        return jnp.concatenate(
            [jnp.dot(hi[:, s:s + GW], head_ones, preferred_element_type=F32)
             + jnp.dot(lo[:, s:s + GW], head_ones, preferred_element_type=F32) for s in range(0, W, GW)],
            axis=-1)

    kk = kk / jnp.maximum(jnp.sqrt(head_sum(kk * kk)), 1e-12)
    lr_kk = kk * a

    ti = lax.broadcasted_iota(jnp.int32, (R, R), 0)
    tj = lax.broadcasted_iota(jnp.int32, (R, R), 1)
    same_chunk = (ti >= tj) & (ti // C == tj // C)
    tri = same_chunk.astype(BF16)
    cl = sum(jnp.dot(tri, piece, preferred_element_type=F32) for piece in reversed(_bf16_pieces(logd, 3)))
    cl_end = per_block(jnp.concatenate([cl[i * C + C - 1:i * C + C, :] for i in range(B * NC)], axis=0), C)
    a_hat = -kk * jnp.exp(cl - logd)
    r_hat = r * jnp.exp(cl)
    inv_gam = jnp.exp(-cl)
    b_til = lr_kk * inv_gam
    k_til = knew * inv_gam
    to_end = jnp.exp(cl_end - cl)
    b_end = lr_kk * to_end
    k_end = knew * to_end
    gam_end = jnp.exp(cl_end)

    gt = lax.broadcasted_iota(jnp.int32, (C, GW), 0)
    gc = lax.broadcasted_iota(jnp.int32, (C, GW), 1) % N
    strict = gt > gc
    incl = gt >= gc
    eye = (gt == gc).astype(F32)
    bi = lax.broadcasted_iota(jnp.int32, (HG * C, GW), 0) // C
    bj = lax.broadcasted_iota(jnp.int32, (HG * C, GW), 1) // N
    same_head = bi == bj

    def block_diag(y):
        yb = y.astype(BF16)
        return jnp.where(same_head, jnp.concatenate([yb] * HG, axis=0), jnp.zeros((), BF16))

    def bd_dot(x, y_bd):
        return jnp.dot(x.astype(BF16), y_bd, preferred_element_type=F32)

    def bd_dot_nt(x, y_bd):
        return lax.dot_general(x.astype(BF16), y_bd, (((1,), (1,)), ((), ())), preferred_element_type=F32)

    n_groups = H // HG
    units = [(b, c, gi) for b in range(B) for c in range(NC) for gi in range(n_groups)]
    n_units = range(len(units))
    cut = lambda x, b, c, gi: x[(b * NC + c) * C:(b * NC + c + 1) * C, gi * GW:(gi + 1) * GW]
    v_u = [cut(v, *un) for un in units]
    v_bd = [block_diag(v_u[i]) for i in n_units]
    ar = [jnp.concatenate([cut(a_hat, *un), cut(r_hat, *un)], axis=0) for un in units]
    mb = [bd_dot_nt(ar[i], block_diag(cut(b_til, *units[i]))) for i in n_units]
    mk = [bd_dot_nt(ar[i], block_diag(cut(k_til, *units[i]))) for i in n_units]
    n_ab = [jnp.where(strict, mb[i][:C], 0.0) for i in n_units]
    m_rb = [jnp.where(incl, mb[i][C:], 0.0) for i in n_units]
    l_ak = [jnp.where(strict, mk[i][:C], 0.0) for i in n_units]
    m_rk = [jnp.where(incl, mk[i][C:], 0.0) for i in n_units]

    pw = list(n_ab)
    pw_bd = [block_diag(pw[i]) for i in n_units]
    tinv = [eye + n_ab[i] for i in n_units]
    step = 2
    while step < C:
        pw = [bd_dot(pw[i], pw_bd[i]) for i in n_units]
        pw_bd = [block_diag(pw[i]) for i in n_units]
        tinv = [tinv[i] + bd_dot(tinv[i], pw_bd[i]) for i in n_units]
        step *= 2
    lv = [bd_dot(l_ak[i], v_bd[i]) for i in n_units]

    state = {(b, gi): state_ref[b * n_groups + gi] for b in range(B) for gi in range(n_groups)}
    outs = {}
    for c in range(NC):
        live = [i for i in n_units if units[i][1] == c]
        s0 = {i: state[(units[i][0], units[i][2])] for i in live}
        ars = {i: bd_dot_nt(ar[i], block_diag(s0[i])) for i in live}
        u = {i: bd_dot(tinv[i], block_diag(ars[i][:C] + lv[i])) for i in live}
        for i in live:
            outs[units[i]] = ars[i][C:] + bd_dot(m_rb[i], block_diag(u[i])) + bd_dot(m_rk[i], v_bd[i])
        for i in live:
            b, _, gi = units[i]
            uv = jnp.concatenate([u[i], v_u[i]], axis=0)
            bk_end = jnp.concatenate([cut(b_end, *units[i]), cut(k_end, *units[i])], axis=0)
            cross = jnp.where(same_head, _bdot_tn(uv, bk_end), 0.0)
            upd = cross[0:N]
            for h in range(1, HG):
                upd = upd + cross[h * N:(h + 1) * N]
            state[(b, gi)] = s0[i] * cut(gam_end, *units[i])[0:1] + upd
    for (b, gi), s_new in state.items():
        state_ref[b * n_groups + gi] = s_new

    o = jnp.concatenate([jnp.concatenate([outs[(b, c, gi)] for gi in range(n_groups)], axis=-1)
                         for b in range(B) for c in range(NC)], axis=0)
    mean = head_sum(o) * (1.0 / N)
    var = head_sum(jnp.square(o - mean)) * (1.0 / N)
    o = (o - mean) * lax.rsqrt(var + GN_EPS) * lnw_ref[...] + lnb_ref[...]
    bonus = head_sum(r * knew * rk_ref[...]) * v
    o_ref[...] = ((o + bonus) * g).reshape(o_ref.shape)


def _rwkv_time_mix(p_a, mu, w0, w2, a0, a2, g2, k_k, k_a, r_k, lnx_w, lnx_b):
    B, S, _ = p_a.shape
    L = RWKV_CHUNK * RWKV_STEP_CHUNKS
    row = lambda t: t.reshape(1, -1)
    full = lambda shape: pl.BlockSpec(shape, lambda s: (0,) * len(shape))
    n_units = B * RWKV_HEADS // RWKV_HEAD_GROUP
    return pl.pallas_call(
        _rwkv_kernel,
        grid=(S // L,),
        in_specs=[pl.BlockSpec((B, L, RWKV_IN_W), lambda s: (0, s, 0)),
                  full((1, RWKV_IN_W)), full((1, RWKV_WIDTH)), full((W_LORA, RWKV_WIDTH)),
                  full((1, RWKV_WIDTH)), full((A_LORA, RWKV_WIDTH)), full((G_LORA, RWKV_WIDTH)),
                  full((1, RWKV_WIDTH)), full((1, RWKV_WIDTH)), full((1, RWKV_WIDTH)),
                  full((1, RWKV_WIDTH)), full((1, RWKV_WIDTH))],
        out_specs=pl.BlockSpec((B, L, RWKV_WIDTH), lambda s: (0, s, 0)),
        out_shape=jax.ShapeDtypeStruct((B, S, RWKV_WIDTH), F32),
        scratch_shapes=[pltpu.VMEM((B, 1, RWKV_IN_W), F32),
                        pltpu.VMEM((n_units, HEAD_DIM, RWKV_HEAD_GROUP * HEAD_DIM), F32)],
        compiler_params=pltpu.CompilerParams(
            dimension_semantics=("arbitrary",), vmem_limit_bytes=VMEM_LIMIT),
        name="rwkv7_chunked",
    )(p_a, row(mu), row(w0), w2, row(a0), a2, g2, row(k_k), row(k_a), row(r_k), row(lnx_w), row(lnx_b))


NSA_KV_TILE = 1024
SEL_KEY_TILE = 1024
NSA_QUERY_TILE = 256
SEL_LANES = 128


def _rope_tables(pos, reps):
    half = HEAD_DIM // 2
    inv = ROPE_THETA ** (-jnp.arange(half, dtype=F32) / half)
    ang = pos.astype(F32)[:, None] * inv
    cos, sin = jnp.cos(ang), jnp.sin(ang)
    cosf = jnp.concatenate([cos, cos], -1)
    sinf = jnp.concatenate([-sin, sin], -1)
    return jnp.tile(cosf, (1, reps)), jnp.tile(sinf, (1, reps))


def _rope(x, cosf, sinf):
    width = x.shape[-1]
    lane = lax.broadcasted_iota(jnp.int32, x.shape, 1)
    first_half = (lane % HEAD_DIM) < HEAD_DIM // 2
    rot = jnp.where(first_half, pltpu.roll(x, width - HEAD_DIM // 2, axis=1),
                    pltpu.roll(x, HEAD_DIM // 2, axis=1))
    return x * cosf + rot * sinf


def _kv_layout_kernel(p_ref, cos_ref, sin_ref, kc_ref, vc_ref, ks_ref, vs_ref, kw_ref, vw_ref):
    ts = p_ref.shape[0]
    for i, o_ref in ((0, kc_ref), (1, vc_ref), (2, ks_ref), (4, kw_ref)):
        t = p_ref[:, i * NSA_KV_WIDTH:(i + 1) * NSA_KV_WIDTH]
        if i >= 2:
            t = _rope(t, cos_ref[...], sin_ref[...])
        for g in range(NSA_GROUPS):
            o_ref[g] = t[:, g * HEAD_DIM:(g + 1) * HEAD_DIM].astype(o_ref.dtype)
    pad_row = lax.broadcasted_iota(jnp.int32, (VT_ROWS - HEAD_DIM, ts), 0)
    ones_row = jnp.where(pad_row == 0, 1.0, 0.0)
    for i, o_ref in ((3, vs_ref), (5, vw_ref)):
        t_t = p_ref[:, i * NSA_KV_WIDTH:(i + 1) * NSA_KV_WIDTH].T
        for g in range(NSA_GROUPS):
            o_ref[g] = jnp.concatenate([t_t[g * HEAD_DIM:(g + 1) * HEAD_DIM], ones_row],
                                       axis=0).astype(o_ref.dtype)


def _kv_layout(p_b, cos2, sin2):
    B, S, _ = p_b.shape
    ts = min(NSA_KV_TILE, S)
    out_spec = pl.BlockSpec((None, NSA_GROUPS, ts, HEAD_DIM), lambda b, s: (b, 0, s, 0))
    vt_spec = pl.BlockSpec((None, NSA_GROUPS, VT_ROWS, ts), lambda b, s: (b, 0, 0, s))
    shp = lambda dt: jax.ShapeDtypeStruct((B, NSA_GROUPS, S, HEAD_DIM), dt)
    vt_shp = jax.ShapeDtypeStruct((B, NSA_GROUPS, VT_ROWS, S), BF16)
    return pl.pallas_call(
        _kv_layout_kernel,
        grid=(B, S // ts),
        in_specs=[pl.BlockSpec((None, ts, 6 * NSA_KV_WIDTH), lambda b, s: (b, s, 0)),
                  pl.BlockSpec((ts, NSA_KV_WIDTH), lambda b, s: (s, 0)),
                  pl.BlockSpec((ts, NSA_KV_WIDTH), lambda b, s: (s, 0))],
        out_specs=[out_spec, out_spec, out_spec, vt_spec, out_spec, vt_spec],
        out_shape=[shp(F32), shp(F32), shp(BF16), vt_shp, shp(BF16), vt_shp],
        compiler_params=pltpu.CompilerParams(
            dimension_semantics=("parallel", "parallel"), vmem_limit_bytes=VMEM_LIMIT),
        name="nsa_kv_layout",
    )(p_b, cos2, sin2)


def _compress_kernel(subk_ref, subv_ref, pek_ref, w1k_ref, w2k_ref, pev_ref, w1v_ref, w2v_ref,
                     cos_ref, sin_ref, kc_ref, vc_ref):
    n_sub = subk_ref.shape[0]
    half = CMP_STRIDE * HEAD_DIM

    def mlp(sub_ref, pe_ref, w1_ref, w2_ref):
        sub = sub_ref[...]
        top = _bdot(sub, w1_ref[:half, :])
        bot = _bdot(sub, w1_ref[half:, :])
        bias = _bdot(jnp.broadcast_to(pe_ref[...], (8, 2 * half)), w1_ref[...])[0:1, :]
        h = top + pltpu.roll(bot, n_sub - 1, axis=0) + bias
        return _bdot(jax.nn.gelu(h), w2_ref[...])

    kc = mlp(subk_ref, pek_ref, w1k_ref, w2k_ref)
    rot = jnp.concatenate([kc[:, HEAD_DIM // 2:], kc[:, :HEAD_DIM // 2]], axis=-1)
    kc_ref[...] = (kc * cos_ref[...] + rot * sin_ref[...]).astype(kc_ref.dtype)
    vc_ref[...] = mlp(subv_ref, pev_ref, w1v_ref, w2v_ref).astype(vc_ref.dtype)


def _compress(subk, subv, pe_k, w1_k, w2_k, pe_v, w1_v, w2_v, cos_c, sin_c):
    B, G, n_sub, width = subk.shape
    sub_spec = pl.BlockSpec((None, None, n_sub, width), lambda b, g: (b, g, 0, 0))
    full = lambda a: pl.BlockSpec(a.shape, lambda b, g: (0,) * a.ndim)
    out_spec = pl.BlockSpec((None, None, n_sub, HEAD_DIM), lambda b, g: (b, g, 0, 0))
    pe_k, pe_v = pe_k.reshape(1, -1), pe_v.reshape(1, -1)
    args = (pe_k, w1_k, w2_k, pe_v, w1_v, w2_v, cos_c, sin_c)
    return pl.pallas_call(
        _compress_kernel,
        grid=(B, G),
        in_specs=[sub_spec, sub_spec] + [full(a) for a in args],
        out_specs=[out_spec, out_spec],
        out_shape=[jax.ShapeDtypeStruct((B, G, n_sub, HEAD_DIM), BF16)] * 2,
        compiler_params=pltpu.CompilerParams(
            dimension_semantics=("parallel", "parallel"), vmem_limit_bytes=VMEM_LIMIT),
        name="nsa_compress",
    )(subk, subv, *args)


MAX_FLOOR = -1e20
MASK_BIG = 2.0 ** 100
LOG2_E = 1.4426950408889634
VT_ROWS = 80


def _nsa_kernel(q_ref, gate_ref, cos_ref, sin_ref, kc_ref, vc_ref, ks_ref, vst_ref, kw_ref, vwt_ref,
                mselt_ref, o_ref, blockbias_ref, *, n_pick):
    QB, HP, D = NSA_QUERY_TILE, NSA_HPG, HEAD_DIM
    qb = pl.program_id(2)
    n_cmp = kc_ref.shape[0]
    lanes4 = lambda x: jnp.concatenate([x] * HP, axis=1)

    heads = []
    for n in range(HP):
        qh = q_ref[n * D:(n + 1) * D, :]
        rot = jnp.concatenate([qh[D // 2:], qh[:D // 2]], axis=0)
        heads.append(qh * cos_ref[...] + rot * sin_ref[...])
    q4 = (jnp.concatenate(heads, axis=1) * (D ** -0.5 * LOG2_E)).astype(BF16)
    t_row = qb * QB + lax.broadcasted_iota(jnp.int32, (1, QB), 1)

    def softmax_cols(s_t, bias_t):
        sm = s_t + lanes4(bias_t)
        m = jnp.maximum(jnp.max(sm, axis=0, keepdims=True), MAX_FLOOR)
        return jnp.exp2(sm - m)

    cmp_end = lax.broadcasted_iota(jnp.int32, (n_cmp, 1), 0) * CMP_STRIDE + (CMP_BLOCK - 1)
    e_c = softmax_cols(jnp.dot(kc_ref[...], q4, preferred_element_type=F32),
                       jnp.where(cmp_end <= t_row, 0.0, -MASK_BIG))
    den_c = jnp.sum(e_c, axis=0, keepdims=True)
    p_c = e_c * (1.0 / jnp.where(den_c > 0.0, den_c, 1.0))
    o_c = _bdot_tn(vc_ref[...], p_c)
    p_sum = p_c[:, 0:QB]
    for n in range(1, HP):
        p_sum = p_sum + p_c[:, n * QB:(n + 1) * QB]
    p_hi = p_sum.astype(BF16)
    p_lo = (p_sum - p_hi.astype(F32)).astype(BF16)
    imp_t = (jnp.dot(mselt_ref[...], p_hi, preferred_element_type=F32)
             + jnp.dot(mselt_ref[...], p_lo, preferred_element_type=F32))

    j = lax.broadcasted_iota(jnp.int32, (SEL_LANES, QB), 0)
    cur = t_row // SEL_BLOCK
    valid = j * SEL_BLOCK <= t_row
    forced = (j == 0) | (j == cur) | (j == cur - 1)
    score = jnp.where(valid, imp_t + jnp.where(forced, FORCE_BONUS, 0.0), -1.0)
    for _ in range(n_pick):
        m = jnp.max(score, axis=0, keepdims=True)
        idx = jnp.min(jnp.where(score == m, j, SEL_LANES), axis=0, keepdims=True)
        score = jnp.where(j == idx, -2.0, score)
    blockbias_ref[...] = jnp.where((score == -2.0) & valid, 0.0, -MASK_BIG)

    KT = SEL_KEY_TILE
    blocks_per_tile = KT // SEL_BLOCK
    n_tiles = (qb * QB + QB + KT - 1) // KT

    def sel_step(kt, carry, causal):
        m_i, acc = carry
        start = pl.multiple_of(kt * KT, KT)
        s_t = jnp.dot(ks_ref[pl.ds(start, KT), :], q4, preferred_element_type=F32)
        bias = jnp.concatenate(
            [jnp.broadcast_to(blockbias_ref[pl.ds(kt * blocks_per_tile + jb, 1), :], (SEL_BLOCK, QB))
             for jb in range(blocks_per_tile)], axis=0)
        if causal:
            kpos = start + lax.broadcasted_iota(jnp.int32, (KT, 1), 0)
            bias = jnp.where(kpos <= t_row, bias, -MASK_BIG)
        sm = s_t + lanes4(bias)
        m_new = jnp.maximum(m_i, jnp.max(sm, axis=0, keepdims=True))
        e = jnp.exp2(sm - m_new).astype(BF16)
        acc_new = jnp.exp2(m_i - m_new) * acc + jnp.dot(vst_ref[:, pl.ds(start, KT)], e,
                                                        preferred_element_type=F32)
        return m_new, acc_new

    init = (jnp.full((1, HP * QB), MAX_FLOOR, F32), jnp.zeros((VT_ROWS, HP * QB), F32))
    carry = lax.fori_loop(0, n_tiles - 1, lambda kt, c: sel_step(kt, c, False), init)
    _, acc_s = sel_step(n_tiles - 1, carry, True)
    den_s = acc_s[D:D + 1]
    o_s = acc_s[:D] * (1.0 / jnp.where(den_s > 0.0, den_s, 1.0))

    span = WINDOW + QB
    w_start = pl.multiple_of(jnp.maximum(qb * QB - WINDOW, 0), QB)
    dist = t_row - (w_start + lax.broadcasted_iota(jnp.int32, (span, 1), 0))
    e_w = softmax_cols(jnp.dot(kw_ref[pl.ds(w_start, span), :], q4, preferred_element_type=F32),
                       jnp.where((dist >= 0) & (dist < WINDOW), 0.0, -MASK_BIG))
    acc_w = jnp.dot(vwt_ref[:, pl.ds(w_start, span)], e_w.astype(BF16), preferred_element_type=F32)
    den_w = acc_w[D:D + 1]
    o_w = acc_w[:D] * (1.0 / jnp.where(den_w > 0.0, den_w, 1.0))

    gates = _sigmoid(gate_ref[...])
    gate_row = lambda br: jnp.concatenate([gates[3 * n + br:3 * n + br + 1, :] for n in range(HP)], axis=1)
    o_t = gate_row(0) * o_c + gate_row(1) * o_s + gate_row(2) * o_w
    for n in range(HP):
        o_ref[:, n * D:(n + 1) * D] = o_t[:, n * QB:(n + 1) * QB].T


def _cmp_to_sel_matrix(n_cmp_rows, n_sel):
    ratio = SEL_BLOCK // CMP_STRIDE
    ci = np.arange(n_cmp_rows)[:, None]
    sj = np.arange(SEL_LANES)[None, :]
    m = sum(((ci + n) // ratio == sj).astype(np.float32) for n in range(CMP_BLOCK // CMP_STRIDE))
    m = m * (sj < n_sel) * (ci < n_cmp_rows - 1)
    return jnp.asarray(m.T, BF16)


def _nsa_attention(qg_t, kc, vc, ks, vst, kw, vwt, cos_t, sin_t):
    B, _, S = qg_t.shape
    n_sub = kc.shape[2]
    n_sel = S // SEL_BLOCK
    gw = NSA_HPG * HEAD_DIM
    gate_row0 = NSA_WIDTH // 128
    msel_t = _cmp_to_sel_matrix(n_sub, n_sel)
    at_bg = lambda shape: pl.BlockSpec((None, None) + shape, lambda b, g, i: (b, g, 0, 0))
    const = lambda a: pl.BlockSpec(a.shape, lambda b, g, i: (0, 0))
    return pl.pallas_call(
        functools.partial(_nsa_kernel, n_pick=min(N_SELECT, n_sel)),
        grid=(B, NSA_GROUPS, S // NSA_QUERY_TILE),
        in_specs=[pl.BlockSpec((None, gw, NSA_QUERY_TILE), lambda b, g, i: (b, g, i)),
                  pl.BlockSpec((None, 128, NSA_QUERY_TILE), lambda b, g, i: (b, gate_row0 + g, i)),
                  pl.BlockSpec((HEAD_DIM, NSA_QUERY_TILE), lambda b, g, i: (0, i)),
                  pl.BlockSpec((HEAD_DIM, NSA_QUERY_TILE), lambda b, g, i: (0, i)),
                  at_bg((n_sub, HEAD_DIM)), at_bg((n_sub, HEAD_DIM)),
                  at_bg((S, HEAD_DIM)), at_bg((VT_ROWS, S)), at_bg((S, HEAD_DIM)), at_bg((VT_ROWS, S)),
                  const(msel_t)],
        out_specs=pl.BlockSpec((None, NSA_QUERY_TILE, gw), lambda b, g, i: (b, i, g)),
        out_shape=jax.ShapeDtypeStruct((B, S, NSA_WIDTH), F32),
        scratch_shapes=[pltpu.VMEM((SEL_LANES, NSA_QUERY_TILE), F32)],
        compiler_params=pltpu.CompilerParams(
            dimension_semantics=("parallel", "parallel", "arbitrary"), vmem_limit_bytes=VMEM_LIMIT),
        name="nsa_attention",
    )(qg_t, qg_t, cos_t, sin_t, kc, vc, ks, vst, kw, vwt, msel_t)


def _nsa_branch(p_kv, qg_t, cmp_pe_k, cmp_w1_k, cmp_w2_k, cmp_pe_v, cmp_w1_v, cmp_w2_v):
    B, S, _ = p_kv.shape
    pos = jnp.arange(S)
    cos2, sin2 = _rope_tables(pos, NSA_GROUPS)
    kc_raw, vc_raw, ks, vst, kw, vwt = _kv_layout(p_kv, cos2, sin2)
    n_sub = S // CMP_STRIDE
    sub = lambda t: t.reshape(B, NSA_GROUPS, n_sub, CMP_STRIDE * HEAD_DIM)
    cos_c, sin_c = _rope_tables(jnp.arange(n_sub) * CMP_STRIDE + CMP_BLOCK - 1, 1)
    kc, vc = _compress(sub(kc_raw), sub(vc_raw), cmp_pe_k, cmp_w1_k, cmp_w2_k,
                       cmp_pe_v, cmp_w1_v, cmp_w2_v, cos_c, sin_c)
    cos_q, sin_q = _rope_tables(pos, 1)
    return _nsa_attention(qg_t, kc, vc, ks, vst, kw, vwt, cos_q.T, sin_q.T)


def _nsa_weight_columns(w_nsa):
    K = w_nsa.shape[0]
    q = w_nsa[:, :NSA_WIDTH]
    kv = w_nsa[:, NSA_WIDTH:NSA_WIDTH + 6 * NSA_KV_WIDTH]
    gates = w_nsa[:, NSA_WIDTH + 6 * NSA_KV_WIDTH:]
    per_group = NSA_HPG * 3
    gate_blocks = [jnp.pad(gates[:, g * per_group:(g + 1) * per_group], ((0, 0), (0, 128 - per_group)))
                   for g in range(NSA_GROUPS)]
    return jnp.concatenate([kv, q] + gate_blocks, axis=1)


def _layer_norm(h, g, b):
    mu = jnp.mean(h, axis=-1, keepdims=True)
    var = jnp.mean(jnp.square(h - mu), axis=-1, keepdims=True)
    return (h - mu) * lax.rsqrt(var + LN_EPS) * g + b


def _pack_bf16_halves(x):
    n = x.shape[-1] // 2
    bits = lax.bitcast_convert_type(x.astype(BF16).astype(F32), jnp.uint32)
    return (bits[:, n:] & jnp.uint32(0xFFFF0000)) | (bits[:, :n] >> 16)


def _unpack_bf16_halves(u):
    left = lax.bitcast_convert_type(u << 16, F32)
    right = lax.bitcast_convert_type(u & jnp.uint32(0xFFFF0000), F32)
    return left, right


def _mixer_out_kernel(x_ref, ya_ref, yb_ref, pg_ref, wa_ref, wb_ref, wo_ref, g_ref, b_ref, o_ref, op_ref,
                      *, alpha):
    d = x_ref.shape[-1]
    gate_a = _sigmoid(pg_ref[:, :d].astype(F32))
    gate_b = _sigmoid(pg_ref[:, d:].astype(F32))
    mixed = gate_a * _bdot(ya_ref[...], wa_ref[...]) + gate_b * _bdot(yb_ref[...], wb_ref[...])
    h = alpha * x_ref[...] + _bdot(mixed, wo_ref[...])
    out = _layer_norm(h, g_ref[...], b_ref[...])
    o_ref[...] = out
    op_ref[...] = _pack_bf16_halves(out)


def _mixer_out(xf, ya, yb, p_g, wa, wb, wo, ln_g, ln_b, alpha, tm=512):
    T, D = xf.shape
    rows = lambda w: pl.BlockSpec((tm, w), lambda i: (i, 0))
    full = lambda a: pl.BlockSpec(a.shape, lambda i: (0,) * a.ndim)
    ln_g, ln_b = ln_g.reshape(1, D), ln_b.reshape(1, D)
    return pl.pallas_call(
        functools.partial(_mixer_out_kernel, alpha=alpha),
        grid=(T // tm,),
        in_specs=[rows(D), rows(ya.shape[1]), rows(yb.shape[1]), rows(2 * D),
                  full(wa), full(wb), full(wo), full(ln_g), full(ln_b)],
        out_specs=[rows(D), rows(D // 2)],
        out_shape=[jax.ShapeDtypeStruct((T, D), F32), jax.ShapeDtypeStruct((T, D // 2), jnp.uint32)],
        compiler_params=pltpu.CompilerParams(
            dimension_semantics=("parallel",), vmem_limit_bytes=VMEM_LIMIT),
        name="mixer_out_ln",
    )(xf, ya, yb, p_g, wa, wb, wo, ln_g, ln_b)


ROUTER_TILE = 256
EXPERT_ROWS = 256
EXPERT_DMA_PIECES = 4
SC_TOKEN_CHUNK = 64
SC_ROW_CHUNK = 64
PICK_LANES = 128
LOWEST = -3.0e38


def _router_kernel(x_ref, rwt_ref, bias_ref, eidx_ref, wts_ref, pos_ref, cnt_ref, carry_ref):
    tm, E = x_ref.shape[0], rwt_ref.shape[0]
    per_group = E // N_GROUPS
    reps = tm // PICK_LANES

    @pl.when(pl.program_id(0) == 0)
    def _():
        carry_ref[...] = jnp.zeros_like(carry_ref)

    scores = _sigmoid(_dot3(rwt_ref[...], x_ref[...], (((1,), (1,)), ((), ()))))
    choice = scores + jnp.concatenate([bias_ref[...]] * reps, axis=1)
    row = lax.broadcasted_iota(jnp.int32, (E, tm), 0)

    def first_max(vals, rows):
        m = jnp.max(vals, axis=0, keepdims=True)
        return m, jnp.min(jnp.where(vals == m, rows, E), axis=0, keepdims=True)

    group_score = []
    for g in range(N_GROUPS):
        rows = slice(g * per_group, (g + 1) * per_group)
        group_row = g * per_group + lax.broadcasted_iota(jnp.int32, (per_group, tm), 0)
        m1, i1 = first_max(choice[rows], group_row)
        m2 = jnp.max(jnp.where(group_row == i1, LOWEST, choice[rows]), axis=0, keepdims=True)
        group_score.append(m1 + m2)
    masked = []
    for g in range(N_GROUPS):
        rank = jnp.zeros((1, tm), jnp.int32)
        for o in range(N_GROUPS):
            if o != g:
                ahead = (group_score[o] > group_score[g]) if o > g else (group_score[o] >= group_score[g])
                rank = rank + ahead.astype(jnp.int32)
        masked.append(jnp.where(rank < TOPK_GROUPS, choice[g * per_group:(g + 1) * per_group], NEG_INF))

    cur = jnp.concatenate(masked, axis=0)
    picks = []
    for _ in range(TOP_K):
        _, idx = first_max(cur, row)
        picks.append(idx)
        cur = jnp.where(row == idx, LOWEST, cur)
    sel = jnp.where(cur == LOWEST, 1.0, 0.0)
    gate = scores * sel
    gate = gate * (ROUTED_SCALE / jnp.sum(gate, axis=0, keepdims=True))

    ti = lax.broadcasted_iota(jnp.int32, (tm, tm), 0)
    tj = lax.broadcasted_iota(jnp.int32, (tm, tm), 1)
    sel_b = sel.astype(BF16)
    before = jnp.dot(sel_b, (ti < tj).astype(BF16), preferred_element_type=F32)
    queue_pos = before + jnp.concatenate([carry_ref[...]] * reps, axis=1)
    carry_ref[...] = carry_ref[...] + jnp.dot(sel_b, jnp.ones((tm, PICK_LANES), BF16),
                                              preferred_element_type=F32)
    cnt_ref[...] = carry_ref[...]

    at_pick = lambda vals, idx: jnp.sum(jnp.where(row == idx, vals, 0.0), axis=0, keepdims=True)
    eidx_ref[...] = jnp.concatenate(picks, axis=0)
    wts_ref[...] = jnp.concatenate([at_pick(gate, idx) for idx in picks], axis=0)
    pos_ref[...] = jnp.concatenate([at_pick(queue_pos, idx) for idx in picks], axis=0).astype(jnp.int32)


def _router(xf, router_w, router_bias):
    T, D = xf.shape
    E = router_w.shape[1]
    tm = ROUTER_TILE
    picks = lambda dt: jax.ShapeDtypeStruct((TOP_K, T), dt)
    pick_spec = pl.BlockSpec((TOP_K, tm), lambda i: (0, i))
    lanes = lambda v: jnp.broadcast_to(v.reshape(E, 1), (E, PICK_LANES))
    return pl.pallas_call(
        _router_kernel,
        grid=(T // tm,),
        in_specs=[pl.BlockSpec((tm, D), lambda i: (i, 0)),
                  pl.BlockSpec((E, D), lambda i: (0, 0)),
                  pl.BlockSpec((E, PICK_LANES), lambda i: (0, 0))],
        out_specs=[pick_spec, pick_spec, pick_spec, pl.BlockSpec((E, PICK_LANES), lambda i: (0, 0))],
        out_shape=[picks(jnp.int32), picks(F32), picks(jnp.int32),
                   jax.ShapeDtypeStruct((E, PICK_LANES), F32)],
        scratch_shapes=[pltpu.VMEM((E, PICK_LANES), F32)],
        compiler_params=pltpu.CompilerParams(
            dimension_semantics=("arbitrary",), vmem_limit_bytes=VMEM_LIMIT),
        name="moe_router",
    )(xf, router_w.T, lanes(router_bias))


def _dest_kernel(eidx_ref, pos_ref, start_ref, dest_ref):
    E = start_ref.shape[0]
    tm = eidx_ref.shape[1]
    row = lax.broadcasted_iota(jnp.int32, (E, tm), 0)
    start = jnp.concatenate([start_ref[...]] * (tm // PICK_LANES), axis=1)
    eidx = eidx_ref[...]
    base = [jnp.sum(jnp.where(row == eidx[kk:kk + 1, :], start, 0), axis=0, keepdims=True)
            for kk in range(TOP_K)]
    dest_ref[...] = jnp.concatenate(base, axis=0) + pos_ref[...]


def _dest_rows(eidx_t, pos_t, pad_start):
    T = eidx_t.shape[1]
    E = pad_start.shape[0]
    tm = ROUTER_TILE
    pick_spec = pl.BlockSpec((TOP_K, tm), lambda i: (0, i))
    return pl.pallas_call(
        _dest_kernel,
        grid=(T // tm,),
        in_specs=[pick_spec, pick_spec, pl.BlockSpec((E, PICK_LANES), lambda i: (0, 0))],
        out_specs=pick_spec,
        out_shape=jax.ShapeDtypeStruct((TOP_K, T), jnp.int32),
        compiler_params=pltpu.CompilerParams(
            dimension_semantics=("parallel",), vmem_limit_bytes=VMEM_LIMIT),
        name="moe_dest_rows",
    )(eidx_t, pos_t, jnp.broadcast_to(pad_start.reshape(E, 1), (E, PICK_LANES)))


def _sc_mesh():
    return plsc.VectorSubcoreMesh(core_axis_name="c", subcore_axis_name="s")


def _sc_scatter_rows(x, dest_t, n_rows):
    T, D = x.shape
    K = dest_t.shape[0]
    mesh = _sc_mesh()
    nc, nw = mesh.num_cores, mesh.num_cores * mesh.num_subcores
    per_w = T // nw
    chunk = min(SC_TOKEN_CHUNK, per_w)
    n_chunks = per_w // chunk
    idx = dest_t.reshape(K, nw, n_chunks, chunk).transpose(1, 2, 0, 3).reshape(nw, n_chunks * K, chunk)

    assert n_chunks % 2 == 0

    @functools.partial(
        pl.kernel, mesh=mesh,
        out_type=jax.ShapeDtypeStruct((n_rows, D), x.dtype),
        scratch_types=[pltpu.VMEM((n_chunks * K, chunk), jnp.int32),
                       pltpu.VMEM((2, chunk, D), x.dtype),
                       pltpu.SemaphoreType.DMA((2,)), pltpu.SemaphoreType.DMA((2,))],
    )
    def scatter(x_hbm, idx_hbm, out_hbm, idx_v, rows_v, load_sem, send_sem):
        wid = lax.axis_index("s") * nc + lax.axis_index("c")
        pltpu.sync_copy(idx_hbm.at[wid], idx_v)

        def load(j, b):
            return pltpu.make_async_copy(x_hbm.at[pl.ds(wid * per_w + j * chunk, chunk)], rows_v.at[b],
                                         load_sem.at[b])

        def sends(j, b):
            return [pltpu.make_async_copy(rows_v.at[b], out_hbm.at[idx_v.at[j * K + kk]], send_sem.at[b])
                    for kk in range(K)]

        load(0, 0).start()

        @pl.loop(0, n_chunks, step=2)
        def _(j0):
            for b in range(2):
                j = j0 + b
                load(j, b).wait()

                @pl.when(j >= 1)
                def _():
                    for c in sends(j - 1, 1 - b):
                        c.wait()

                @pl.when(j + 1 < n_chunks)
                def _():
                    load(j + 1, 1 - b).start()

                for c in sends(j, b):
                    c.start()

        for c in sends(n_chunks - 1, (n_chunks - 1) % 2):
            c.wait()

    return scatter(x, idx)


def _sc_gather_rows(src, idx):
    M = idx.shape[0]
    D = src.shape[1]
    mesh = _sc_mesh()
    nc, nw = mesh.num_cores, mesh.num_cores * mesh.num_subcores
    per_w = M // nw
    chunk = min(SC_ROW_CHUNK, per_w)
    n_chunks = per_w // chunk

    assert n_chunks % 2 == 0

    @functools.partial(
        pl.kernel, mesh=mesh,
        out_type=jax.ShapeDtypeStruct((M, D), src.dtype),
        scratch_types=[pltpu.VMEM((n_chunks, chunk), jnp.int32),
                       pltpu.VMEM((2, chunk, D), src.dtype),
                       pltpu.SemaphoreType.DMA((2,)), pltpu.SemaphoreType.DMA((2,))],
    )
    def gather(src_hbm, idx_hbm, out_hbm, idx_v, rows_v, fetch_sem, store_sem):
        wid = lax.axis_index("s") * nc + lax.axis_index("c")
        pltpu.sync_copy(idx_hbm.at[wid], idx_v)

        def fetch(j, b):
            return pltpu.make_async_copy(src_hbm.at[idx_v.at[j]], rows_v.at[b], fetch_sem.at[b])

        def store(j, b):
            return pltpu.make_async_copy(rows_v.at[b], out_hbm.at[pl.ds(wid * per_w + j * chunk, chunk)],
                                         store_sem.at[b])

        fetch(0, 0).start()

        @pl.loop(0, n_chunks, step=2)
        def _(j0):
            for b in range(2):
                j = j0 + b
                fetch(j, b).wait()

                @pl.when(j >= 1)
                def _():
                    store(j - 1, 1 - b).wait()

                @pl.when(j + 1 < n_chunks)
                def _():
                    fetch(j + 1, 1 - b).start()

                store(j, b).start()

        store(n_chunks - 1, (n_chunks - 1) % 2).wait()

    return gather(src, idx.reshape(nw, n_chunks, chunk))


def _expert_kernel(distinct_e_ref, blk_ord_ref, blk_new_ref, blk_rows_ref, n_used_ref, n_distinct_ref,
                   x_ref, wgu_hbm, wd_hbm, o_ref, wgu_buf, wd_buf, wgu_bf, wd_bf, sem):
    i = pl.program_id(0)
    live = i < n_used_ref[0]
    ordinal = blk_ord_ref[i]
    slot = ordinal % 2

    def weight_copies(k, s):
        e = distinct_e_ref[k]
        copies = []
        for m, (hbm, buf) in enumerate(((wgu_hbm, wgu_buf), (wd_hbm, wd_buf))):
            rows = buf.shape[1] // EXPERT_DMA_PIECES
            for piece in range(EXPERT_DMA_PIECES):
                cut = pl.ds(piece * rows, rows)
                copies.append(pltpu.make_async_copy(hbm.at[e, cut], buf.at[s, cut],
                                                    sem.at[m * EXPERT_DMA_PIECES + piece, s]))
        return copies

    @pl.when(i == 0)
    def _():
        for c in weight_copies(0, 0):
            c.start()

    @pl.when(live & (blk_new_ref[i] == 1))
    def _():
        for c in weight_copies(ordinal, slot):
            c.wait()

        @pl.when(ordinal + 1 < n_distinct_ref[0])
        def _():
            for c in weight_copies(ordinal + 1, 1 - slot):
                c.start()

        wgu_bf[...] = wgu_buf[slot].astype(BF16)
        wd_bf[...] = wd_buf[slot].astype(BF16)

    @pl.when(live)
    def _():
        hidden = wd_bf.shape[0]
        half = x_ref.shape[1]
        row = lax.broadcasted_iota(jnp.int32, x_ref.shape, 0)
        left, right = _unpack_bf16_halves(x_ref[...])
        real = row < blk_rows_ref[i]
        left = jnp.where(real, left, 0.0).astype(BF16)
        right = jnp.where(real, right, 0.0).astype(BF16)
        h = (jnp.dot(left, wgu_bf[:half, :], preferred_element_type=F32)
             + jnp.dot(right, wgu_bf[half:, :], preferred_element_type=F32))
        gate, up = h[:, :hidden], h[:, hidden:]
        act = (gate * _sigmoid(gate) * up).astype(BF16)
        o_ref[...] = _pack_bf16_halves(jnp.dot(act, wd_bf[...], preferred_element_type=F32))

    @pl.when(jnp.logical_not(live))
    def _():
        o_ref[...] = jnp.zeros_like(o_ref)


def _expert_ffn(xs, blk_e, blk_rows, n_used, w_gu, w_down):
    n_rows, half = xs.shape
    E, D, two_h = w_gu.shape
    n_blocks = n_rows // EXPERT_ROWS
    idx = jnp.arange(n_blocks, dtype=jnp.int32)
    is_live = idx < n_used[0]
    blk_new = (is_live & ((idx == 0) | (blk_e != jnp.roll(blk_e, 1)))).astype(jnp.int32)
    blk_ord = (jnp.cumsum(blk_new) - 1).astype(jnp.int32)
    n_distinct = blk_ord[-1:] + 1
    first_of = (blk_new[None, :] == 1) & (blk_ord[None, :] == idx[:, None])
    distinct_e = jnp.sum(jnp.where(first_of, blk_e[None, :], 0), axis=1).astype(jnp.int32)

    live = lambda i, nu: jnp.minimum(i, nu[0] - 1)
    grid_spec = pltpu.PrefetchScalarGridSpec(
        num_scalar_prefetch=6,
        grid=(n_blocks,),
        in_specs=[pl.BlockSpec((EXPERT_ROWS, half), lambda i, de, bo, bn, br, nu, nd: (live(i, nu), 0)),
                  pl.BlockSpec(memory_space=pl.ANY), pl.BlockSpec(memory_space=pl.ANY)],
        out_specs=pl.BlockSpec((EXPERT_ROWS, half), lambda i, de, bo, bn, br, nu, nd: (i, 0)),
        scratch_shapes=[pltpu.VMEM((2, D, two_h), F32), pltpu.VMEM((2, two_h // 2, D), F32),
                        pltpu.VMEM((D, two_h), BF16), pltpu.VMEM((two_h // 2, D), BF16),
                        pltpu.SemaphoreType.DMA((2 * EXPERT_DMA_PIECES, 2))],
    )
    return pl.pallas_call(
        _expert_kernel,
        grid_spec=grid_spec,
        out_shape=jax.ShapeDtypeStruct((n_rows, half), jnp.uint32),
        compiler_params=pltpu.CompilerParams(
            dimension_semantics=("arbitrary",), vmem_limit_bytes=VMEM_LIMIT),
        name="moe_experts",
    )(distinct_e, blk_ord, blk_new, blk_rows, n_used, n_distinct, xs, w_gu, w_down)


def _moe_out_kernel(x_ref, yk_ref, wts_ref, sgu_ref, sd_ref, g_ref, b_ref, o_ref, *, alpha):
    x = x_ref[...]
    hidden = sd_ref.shape[0]
    h = _bdot(x, sgu_ref[...])
    gate, up = h[:, :hidden], h[:, hidden:]
    ffn = _bdot(gate * _sigmoid(gate) * up, sd_ref[...])
    wts = wts_ref[...]
    routed_left = routed_right = None
    for kk in range(TOP_K):
        left, right = _unpack_bf16_halves(yk_ref[kk])
        w = wts[:, kk:kk + 1]
        routed_left = w * left if kk == 0 else routed_left + w * left
        routed_right = w * right if kk == 0 else routed_right + w * right
    ffn = ffn + jnp.concatenate([routed_left, routed_right], axis=-1)
    o_ref[...] = _layer_norm(alpha * x + ffn, g_ref[...], b_ref[...])


def _moe_out(xf, yk, wts, sw_gu, sw_down, ln_g, ln_b, alpha, tm=128):
    T, D = xf.shape
    rows = lambda w: pl.BlockSpec((tm, w), lambda i: (i, 0))
    full = lambda a: pl.BlockSpec(a.shape, lambda i: (0,) * a.ndim)
    ln_g, ln_b = ln_g.reshape(1, D), ln_b.reshape(1, D)
    return pl.pallas_call(
        functools.partial(_moe_out_kernel, alpha=alpha),
        grid=(T // tm,),
        in_specs=[rows(D), pl.BlockSpec((TOP_K, tm, D // 2), lambda i: (0, i, 0)), rows(PICK_LANES),
                  full(sw_gu), full(sw_down), full(ln_g), full(ln_b)],
        out_specs=rows(D),
        out_shape=jax.ShapeDtypeStruct((T, D), F32),
        compiler_params=pltpu.CompilerParams(
            dimension_semantics=("parallel",), vmem_limit_bytes=VMEM_LIMIT),
        name="moe_combine_ln",
    )(xf, yk, wts, sw_gu, sw_down, ln_g, ln_b)


def _moe_ffn_ln(xf, xp, router_w, router_bias, w_gu, w_down, sw_gu, sw_down, ln_g, ln_b, alpha):
    T, D = xf.shape
    E = router_w.shape[1]
    BM = EXPERT_ROWS
    eidx_t, wts_t, pos_t, cnt = _router(xf, router_w, router_bias)
    counts = cnt[:, 0].astype(jnp.int32)
    padded = (counts + BM - 1) // BM * BM
    pad_end = jnp.cumsum(padded)
    pad_start = pad_end - padded
    n_rows = T * TOP_K + E * BM
    n_blocks = n_rows // BM
    blk_row0 = jnp.arange(n_blocks, dtype=jnp.int32) * BM
    blk_e = jnp.minimum(jnp.sum((pad_end[None, :] <= blk_row0[:, None]).astype(jnp.int32), axis=1), E - 1)
    blk_rows = jnp.clip(pad_start[blk_e] + counts[blk_e] - blk_row0, 0, BM).astype(jnp.int32)
    n_used = (pad_end[-1:] // BM).astype(jnp.int32)
    dest_t = _dest_rows(eidx_t, pos_t, pad_start)
    wts = jnp.pad(wts_t.T, ((0, 0), (0, PICK_LANES - TOP_K)))
    xs = _sc_scatter_rows(xp, dest_t, n_rows)
    ys = _expert_ffn(xs, blk_e, blk_rows, n_used, w_gu, w_down)
    yk = _sc_gather_rows(ys, dest_t.reshape(-1)).reshape(TOP_K, T, D // 2)
    return _moe_out(xf, yk, wts, sw_gu, sw_down, ln_g, ln_b, alpha)


def kernel(x, w_in, tshift_mu, rwkv_w0, rwkv_w2, rwkv_a0, rwkv_a2, rwkv_g2, rwkv_k_k, rwkv_k_a, rwkv_r_k, rwkv_lnx_w, rwkv_lnx_b, cmp_pe_k, cmp_w1_k, cmp_w2_k, cmp_pe_v, cmp_w1_v, cmp_w2_v, w_branch_a, w_branch_b, w_out, ln1_g, ln1_b, router_w, router_bias, exp_w_gu, exp_w_down, shared_w_gu, shared_w_down, ln2_g, ln2_b):
    B, S, D = x.shape
    depth = w_in.shape[0]
    alpha = (2 * depth) ** 0.25
    nsa_w = w_in.shape[2] - RWKV_IN_W - 2 * D
    for l in range(depth):
        xf = x.reshape(B * S, D)
        w_l = w_in[l]
        w_a = w_l[:, :RWKV_IN_W].astype(BF16)
        w_b = _nsa_weight_columns(w_l[:, RWKV_IN_W:RWKV_IN_W + nsa_w]).astype(BF16)
        w_g = w_l[:, RWKV_IN_W + nsa_w:].astype(BF16)
        kv_w = 6 * NSA_KV_WIDTH
        p_a = _matmul(xf, w_a, PROJ_ROWS, w_a.shape[1]).reshape(B, S, -1)
        p_kv = _matmul(xf, w_b[:, :kv_w], PROJ_ROWS, kv_w).reshape(B, S, -1)
        qg_t = _matmul_t(x, w_b[:, kv_w:].T, PROJ_ROWS)
        p_g = _matmul(xf, w_g, PROJ_ROWS, w_g.shape[1], BF16)
        y_a = _rwkv_time_mix(p_a, tshift_mu[l], rwkv_w0[l], rwkv_w2[l], rwkv_a0[l], rwkv_a2[l], rwkv_g2[l],
                             rwkv_k_k[l], rwkv_k_a[l], rwkv_r_k[l].reshape(-1), rwkv_lnx_w[l], rwkv_lnx_b[l])
        y_b = _nsa_branch(p_kv, qg_t, cmp_pe_k[l], cmp_w1_k[l], cmp_w2_k[l],
                          cmp_pe_v[l], cmp_w1_v[l], cmp_w2_v[l])
        x1, x1p = _mixer_out(xf, y_a.reshape(B * S, -1), y_b.reshape(B * S, -1), p_g,
                             w_branch_a[l].astype(BF16), w_branch_b[l].astype(BF16), w_out[l].astype(BF16),
                             ln1_g[l], ln1_b[l], alpha)
        x2 = _moe_ffn_ln(x1, x1p, router_w[l], router_bias[l], exp_w_gu[l], exp_w_down[l],
                         shared_w_gu[l].astype(BF16), shared_w_down[l].astype(BF16), ln2_g[l], ln2_b[l], alpha)
        x = x2.reshape(B, S, D)
    return x
```

```python
import functools

import numpy as np
import jax
import jax.numpy as jnp
from jax import lax
from jax.experimental import pallas as pl
from jax.experimental.pallas import tpu as pltpu
from jax.experimental.pallas import tpu_sc as plsc

F32 = jnp.float32
BF16 = jnp.bfloat16

RWKV_HEADS = 8
HEAD_DIM = 64
RWKV_WIDTH = RWKV_HEADS * HEAD_DIM
W_LORA = 64
A_LORA = 64
G_LORA = 128
GN_EPS = 64e-5
NSA_HEADS = 8
NSA_GROUPS = 2
NSA_HPG = NSA_HEADS // NSA_GROUPS
NSA_WIDTH = NSA_HEADS * HEAD_DIM
NSA_KV_WIDTH = NSA_GROUPS * HEAD_DIM
CMP_BLOCK = 32
CMP_STRIDE = 16
CMP_HIDDEN = 256
SEL_BLOCK = 64
N_SELECT = 16
WINDOW = 512
ROPE_THETA = 10000.0
RWKV_IN_W = 3 * RWKV_WIDTH + W_LORA + A_LORA + G_LORA
N_EXPERTS = 256
TOP_K = 8
N_GROUPS = 8
TOPK_GROUPS = 4
EXPERT_DIM = 256
ROUTED_SCALE = 2.5
LN_EPS = 1e-5
NEG_INF = -1e30
FORCE_BONUS = 1e4

RWKV_CHUNK = 64
RWKV_HEAD_GROUP = 4
RWKV_STEP_CHUNKS = 2
VMEM_LIMIT = 56 * 1024 * 1024
PROJ_ROWS = 1024


def _bdot(a, b):
    return jnp.dot(a.astype(BF16), b.astype(BF16), preferred_element_type=F32)


def _bdot_nt(a, b):
    return lax.dot_general(a.astype(BF16), b.astype(BF16), (((1,), (1,)), ((), ())),
                           preferred_element_type=F32)


def _bdot_tn(a, b):
    return lax.dot_general(a.astype(BF16), b.astype(BF16), (((0,), (0,)), ((), ())),
                           preferred_element_type=F32)


def _bf16_pieces(x, n):
    pieces = []
    for _ in range(n):
        p = x.astype(BF16)
        pieces.append(p)
        x = x - p.astype(F32)
    return pieces


def _dot3(a, b, dims=(((1,), (0,)), ((), ()))):
    (a_hi, a_lo), (b_hi, b_lo) = _bf16_pieces(a, 2), _bf16_pieces(b, 2)
    dot = lambda p, q: lax.dot_general(p, q, dims, preferred_element_type=F32)
    return dot(a_hi, b_hi) + (dot(a_hi, b_lo) + dot(a_lo, b_hi))


def _sigmoid(x):
    return 1.0 / (1.0 + jnp.exp(-x))


def _matmul_kernel(x_ref, w_ref, o_ref):
    o_ref[...] = jnp.dot(x_ref[...].astype(BF16), w_ref[...], preferred_element_type=F32).astype(o_ref.dtype)


def _matmul(x, w, tm, tn, out_dtype=F32):
    M, K = x.shape
    N = w.shape[1]
    return pl.pallas_call(
        _matmul_kernel,
        grid=(M // tm, N // tn),
        in_specs=[pl.BlockSpec((tm, K), lambda i, j: (i, 0)),
                  pl.BlockSpec((K, tn), lambda i, j: (0, j))],
        out_specs=pl.BlockSpec((tm, tn), lambda i, j: (i, j)),
        out_shape=jax.ShapeDtypeStruct((M, N), out_dtype),
        compiler_params=pltpu.CompilerParams(
            dimension_semantics=("parallel", "parallel"), vmem_limit_bytes=VMEM_LIMIT),
        name="dense_proj",
    )(x, w)


def _matmul_t_kernel(x_ref, wt_ref, o_ref):
    o_ref[...] = lax.dot_general(wt_ref[...], x_ref[...].astype(BF16), (((1,), (1,)), ((), ())),
                                 preferred_element_type=F32)


def _matmul_t(x, w_t, tm):
    B, S, K = x.shape
    N = w_t.shape[0]
    return pl.pallas_call(
        _matmul_t_kernel,
        grid=(B, S // tm),
        in_specs=[pl.BlockSpec((None, tm, K), lambda b, s: (b, s, 0)),
                  pl.BlockSpec((N, K), lambda b, s: (0, 0))],
        out_specs=pl.BlockSpec((None, N, tm), lambda b, s: (b, 0, s)),
        out_shape=jax.ShapeDtypeStruct((B, N, S), F32),
        compiler_params=pltpu.CompilerParams(
            dimension_semantics=("parallel", "parallel"), vmem_limit_bytes=VMEM_LIMIT),
        name="dense_proj_t",
    )(x, w_t)


def _rwkv_kernel(p_ref, mu_ref, w0_ref, w2_ref, a0_ref, a2_ref, g2_ref, kk_ref, ka_ref, rk_ref,
                 lnw_ref, lnb_ref, o_ref, carry_ref, state_ref):
    C, H, N = RWKV_CHUNK, RWKV_HEADS, HEAD_DIM
    W = RWKV_WIDTH
    B = p_ref.shape[0]
    NC = p_ref.shape[1] // C
    L = NC * C
    R = B * L

    @pl.when(pl.program_id(0) == 0)
    def _():
        carry_ref[...] = jnp.zeros_like(carry_ref)
        state_ref[...] = jnp.zeros_like(state_ref)

    def per_block(x, rows):
        return jnp.concatenate(
            [jnp.broadcast_to(x[i].reshape(1, -1), (rows, x.shape[-1])) for i in range(x.shape[0])], axis=0)

    p = p_ref[...].reshape(R, p_ref.shape[-1])
    row = lax.broadcasted_iota(jnp.int32, p.shape, 0)
    prev = jnp.where(row % L == 0, per_block(carry_ref[...], L), pltpu.roll(p, 1, axis=0))
    for b in range(B):
        carry_ref[b] = p[b * L + L - 1:b * L + L, :]
    xs = p + (prev - p) * mu_ref[...]
    r = xs[:, 0:W]
    k = xs[:, W:2 * W]
    v = xs[:, 2 * W:3 * W]
    wl = xs[:, 3 * W:3 * W + W_LORA]
    al = xs[:, 3 * W + W_LORA:3 * W + W_LORA + A_LORA]
    gl = xs[:, 3 * W + W_LORA + A_LORA:]

    z = -(w0_ref[...] + _dot3(jnp.tanh(wl), w2_ref[...]))
    softplus = jnp.maximum(z, 0.0) + jnp.log1p(jnp.exp(-jnp.abs(z)))
    logd = -jnp.exp(-softplus - 0.5)
    a = _sigmoid(a0_ref[...] + _dot3(al, a2_ref[...]))
    g = _dot3(_sigmoid(gl), g2_ref[...])

    kk = k * kk_ref[...]
    knew = k * (1.0 + (a - 1.0) * ka_ref[...])

    HG = RWKV_HEAD_GROUP
    GW = HG * N
    same_head_lanes = (lax.broadcasted_iota(jnp.int32, (GW, GW), 0) // N
                       == lax.broadcasted_iota(jnp.int32, (GW, GW), 1) // N)
    head_ones = jnp.where(same_head_lanes, 1.0, 0.0).astype(BF16)

    def head_sum(x):
        hi = x.astype(BF16)
        lo = (x - hi.astype(F32)).astype(BF16)
        return jnp.concatenate(
            [jnp.dot(hi[:, s:s + GW], head_ones, preferred_element_type=F32)
             + jnp.dot(lo[:, s:s + GW], head_ones, preferred_element_type=F32) for s in range(0, W, GW)],
            axis=-1)

    kk = kk / jnp.maximum(jnp.sqrt(head_sum(kk * kk)), 1e-12)
    lr_kk = kk * a

    ti = lax.broadcasted_iota(jnp.int32, (R, R), 0)
    tj = lax.broadcasted_iota(jnp.int32, (R, R), 1)
    same_chunk = (ti >= tj) & (ti // C == tj // C)
    tri = same_chunk.astype(BF16)
    cl = sum(jnp.dot(tri, piece, preferred_element_type=F32) for piece in reversed(_bf16_pieces(logd, 3)))
    cl_end = per_block(jnp.concatenate([cl[i * C + C - 1:i * C + C, :] for i in range(B * NC)], axis=0), C)
    a_hat = -kk * jnp.exp(cl - logd)
    r_hat = r * jnp.exp(cl)
    inv_gam = jnp.exp(-cl)
    b_til = lr_kk * inv_gam
    k_til = knew * inv_gam
    to_end = jnp.exp(cl_end - cl)
    b_end = lr_kk * to_end
    k_end = knew * to_end
    gam_end = jnp.exp(cl_end)

    gt = lax.broadcasted_iota(jnp.int32, (C, GW), 0)
    gc = lax.broadcasted_iota(jnp.int32, (C, GW), 1) % N
    strict = gt > gc
    incl = gt >= gc
    eye = (gt == gc).astype(F32)
    bi = lax.broadcasted_iota(jnp.int32, (HG * C, GW), 0) // C
    bj = lax.broadcasted_iota(jnp.int32, (HG * C, GW), 1) // N
    same_head = bi == bj

    def block_diag(y):
        yb = y.astype(BF16)
        return jnp.where(same_head, jnp.concatenate([yb] * HG, axis=0), jnp.zeros((), BF16))

    def bd_dot(x, y_bd):
        return jnp.dot(x.astype(BF16), y_bd, preferred_element_type=F32)

    def bd_dot_nt(x, y_bd):
        return lax.dot_general(x.astype(BF16), y_bd, (((1,), (1,)), ((), ())), preferred_element_type=F32)

    n_groups = H // HG
    units = [(b, c, gi) for b in range(B) for c in range(NC) for gi in range(n_groups)]
    n_units = range(len(units))
    cut = lambda x, b, c, gi: x[(b * NC + c) * C:(b * NC + c + 1) * C, gi * GW:(gi + 1) * GW]
    v_u = [cut(v, *un) for un in units]
    v_bd = [block_diag(v_u[i]) for i in n_units]
    ar = [jnp.concatenate([cut(a_hat, *un), cut(r_hat, *un)], axis=0) for un in units]
    mb = [bd_dot_nt(ar[i], block_diag(cut(b_til, *units[i]))) for i in n_units]
    mk = [bd_dot_nt(ar[i], block_diag(cut(k_til, *units[i]))) for i in n_units]
    n_ab = [jnp.where(strict, mb[i][:C], 0.0) for i in n_units]
    m_rb = [jnp.where(incl, mb[i][C:], 0.0) for i in n_units]
    l_ak = [jnp.where(strict, mk[i][:C], 0.0) for i in n_units]
    m_rk = [jnp.where(incl, mk[i][C:], 0.0) for i in n_units]

    pw = list(n_ab)
    pw_bd = [block_diag(pw[i]) for i in n_units]
    tinv = [eye + n_ab[i] for i in n_units]
    step = 2
    while step < C:
        pw = [bd_dot(pw[i], pw_bd[i]) for i in n_units]
        pw_bd = [block_diag(pw[i]) for i in n_units]
        tinv = [tinv[i] + bd_dot(tinv[i], pw_bd[i]) for i in n_units]
        step *= 2
    lv = [bd_dot(l_ak[i], v_bd[i]) for i in n_units]

    state = {(b, gi): state_ref[b * n_groups + gi] for b in range(B) for gi in range(n_groups)}
    outs = {}
    for c in range(NC):
        live = [i for i in n_units if units[i][1] == c]
        s0 = {i: state[(units[i][0], units[i][2])] for i in live}
        ars = {i: bd_dot_nt(ar[i], block_diag(s0[i])) for i in live}
        u = {i: bd_dot(tinv[i], block_diag(ars[i][:C] + lv[i])) for i in live}
        for i in live:
            outs[units[i]] = ars[i][C:] + bd_dot(m_rb[i], block_diag(u[i])) + bd_dot(m_rk[i], v_bd[i])
        for i in live:
            b, _, gi = units[i]
            uv = jnp.concatenate([u[i], v_u[i]], axis=0)
            bk_end = jnp.concatenate([cut(b_end, *units[i]), cut(k_end, *units[i])], axis=0)
            cross = jnp.where(same_head, _bdot_tn(uv, bk_end), 0.0)
            upd = cross[0:N]
            for h in range(1, HG):
                upd = upd + cross[h * N:(h + 1) * N]
            state[(b, gi)] = s0[i] * cut(gam_end, *units[i])[0:1] + upd
    for (b, gi), s_new in state.items():
        state_ref[b * n_groups + gi] = s_new

    o = jnp.concatenate([jnp.concatenate([outs[(b, c, gi)] for gi in range(n_groups)], axis=-1)
                         for b in range(B) for c in range(NC)], axis=0)
    mean = head_sum(o) * (1.0 / N)
    var = head_sum(jnp.square(o - mean)) * (1.0 / N)
    o = (o - mean) * lax.rsqrt(var + GN_EPS) * lnw_ref[...] + lnb_ref[...]
    bonus = head_sum(r * knew * rk_ref[...]) * v
    o_ref[...] = ((o + bonus) * g).reshape(o_ref.shape)


def _rwkv_time_mix(p_a, mu, w0, w2, a0, a2, g2, k_k, k_a, r_k, lnx_w, lnx_b):
    B, S, _ = p_a.shape
    L = RWKV_CHUNK * RWKV_STEP_CHUNKS
    row = lambda t: t.reshape(1, -1)
    full = lambda shape: pl.BlockSpec(shape, lambda s: (0,) * len(shape))
    n_units = B * RWKV_HEADS // RWKV_HEAD_GROUP
    return pl.pallas_call(
        _rwkv_kernel,
        grid=(S // L,),
        in_specs=[pl.BlockSpec((B, L, RWKV_IN_W), lambda s: (0, s, 0)),
                  full((1, RWKV_IN_W)), full((1, RWKV_WIDTH)), full((W_LORA, RWKV_WIDTH)),
                  full((1, RWKV_WIDTH)), full((A_LORA, RWKV_WIDTH)), full((G_LORA, RWKV_WIDTH)),
                  full((1, RWKV_WIDTH)), full((1, RWKV_WIDTH)), full((1, RWKV_WIDTH)),
                  full((1, RWKV_WIDTH)), full((1, RWKV_WIDTH))],
        out_specs=pl.BlockSpec((B, L, RWKV_WIDTH), lambda s: (0, s, 0)),
        out_shape=jax.ShapeDtypeStruct((B, S, RWKV_WIDTH), F32),
        scratch_shapes=[pltpu.VMEM((B, 1, RWKV_IN_W), F32),
                        pltpu.VMEM((n_units, HEAD_DIM, RWKV_HEAD_GROUP * HEAD_DIM), F32)],
        compiler_params=pltpu.CompilerParams(
            dimension_semantics=("arbitrary",), vmem_limit_bytes=VMEM_LIMIT),
        name="rwkv7_chunked",
    )(p_a, row(mu), row(w0), w2, row(a0), a2, g2, row(k_k), row(k_a), row(r_k), row(lnx_w), row(lnx_b))


NSA_KV_TILE = 1024
SEL_KEY_TILE = 1024
NSA_QUERY_TILE = 256
SEL_LANES = 128


def _rope_tables(pos, reps):
    half = HEAD_DIM // 2
    inv = ROPE_THETA ** (-jnp.arange(half, dtype=F32) / half)
    ang = pos.astype(F32)[:, None] * inv
    cos, sin = jnp.cos(ang), jnp.sin(ang)
    cosf = jnp.concatenate([cos, cos], -1)
    sinf = jnp.concatenate([-sin, sin], -1)
    return jnp.tile(cosf, (1, reps)), jnp.tile(sinf, (1, reps))


def _rope(x, cosf, sinf):
    width = x.shape[-1]
    lane = lax.broadcasted_iota(jnp.int32, x.shape, 1)
    first_half = (lane % HEAD_DIM) < HEAD_DIM // 2
    rot = jnp.where(first_half, pltpu.roll(x, width - HEAD_DIM // 2, axis=1),
                    pltpu.roll(x, HEAD_DIM // 2, axis=1))
    return x * cosf + rot * sinf


def _kv_layout_kernel(p_ref, cos_ref, sin_ref, kc_ref, vc_ref, ks_ref, vs_ref, kw_ref, vw_ref):
    ts = p_ref.shape[0]
    for i, o_ref in ((0, kc_ref), (1, vc_ref), (2, ks_ref), (4, kw_ref)):
        t = p_ref[:, i * NSA_KV_WIDTH:(i + 1) * NSA_KV_WIDTH]
        if i >= 2:
            t = _rope(t, cos_ref[...], sin_ref[...])
        for g in range(NSA_GROUPS):
            o_ref[g] = t[:, g * HEAD_DIM:(g + 1) * HEAD_DIM].astype(o_ref.dtype)
    pad_row = lax.broadcasted_iota(jnp.int32, (VT_ROWS - HEAD_DIM, ts), 0)
    ones_row = jnp.where(pad_row == 0, 1.0, 0.0)
    for i, o_ref in ((3, vs_ref), (5, vw_ref)):
        t_t = p_ref[:, i * NSA_KV_WIDTH:(i + 1) * NSA_KV_WIDTH].T
        for g in range(NSA_GROUPS):
            o_ref[g] = jnp.concatenate([t_t[g * HEAD_DIM:(g + 1) * HEAD_DIM], ones_row],
                                       axis=0).astype(o_ref.dtype)


def _kv_layout(p_b, cos2, sin2):
    B, S, _ = p_b.shape
    ts = min(NSA_KV_TILE, S)
    out_spec = pl.BlockSpec((None, NSA_GROUPS, ts, HEAD_DIM), lambda b, s: (b, 0, s, 0))
    vt_spec = pl.BlockSpec((None, NSA_GROUPS, VT_ROWS, ts), lambda b, s: (b, 0, 0, s))
    shp = lambda dt: jax.ShapeDtypeStruct((B, NSA_GROUPS, S, HEAD_DIM), dt)
    vt_shp = jax.ShapeDtypeStruct((B, NSA_GROUPS, VT_ROWS, S), BF16)
    return pl.pallas_call(
        _kv_layout_kernel,
        grid=(B, S // ts),
        in_specs=[pl.BlockSpec((None, ts, 6 * NSA_KV_WIDTH), lambda b, s: (b, s, 0)),
                  pl.BlockSpec((ts, NSA_KV_WIDTH), lambda b, s: (s, 0)),
                  pl.BlockSpec((ts, NSA_KV_WIDTH), lambda b, s: (s, 0))],
        out_specs=[out_spec, out_spec, out_spec, vt_spec, out_spec, vt_spec],
        out_shape=[shp(F32), shp(F32), shp(BF16), vt_shp, shp(BF16), vt_shp],
        compiler_params=pltpu.CompilerParams(
            dimension_semantics=("parallel", "parallel"), vmem_limit_bytes=VMEM_LIMIT),
        name="nsa_kv_layout",
    )(p_b, cos2, sin2)


def _compress_kernel(subk_ref, subv_ref, pek_ref, w1k_ref, w2k_ref, pev_ref, w1v_ref, w2v_ref,
                     cos_ref, sin_ref, kc_ref, vc_ref):
    n_sub = subk_ref.shape[0]
    half = CMP_STRIDE * HEAD_DIM

    def mlp(sub_ref, pe_ref, w1_ref, w2_ref):
        sub = sub_ref[...]
        top = _bdot(sub, w1_ref[:half, :])
        bot = _bdot(sub, w1_ref[half:, :])
        bias = _bdot(jnp.broadcast_to(pe_ref[...], (8, 2 * half)), w1_ref[...])[0:1, :]
        h = top + pltpu.roll(bot, n_sub - 1, axis=0) + bias
        return _bdot(jax.nn.gelu(h), w2_ref[...])

    kc = mlp(subk_ref, pek_ref, w1k_ref, w2k_ref)
    rot = jnp.concatenate([kc[:, HEAD_DIM // 2:], kc[:, :HEAD_DIM // 2]], axis=-1)
    kc_ref[...] = (kc * cos_ref[...] + rot * sin_ref[...]).astype(kc_ref.dtype)
    vc_ref[...] = mlp(subv_ref, pev_ref, w1v_ref, w2v_ref).astype(vc_ref.dtype)


def _compress(subk, subv, pe_k, w1_k, w2_k, pe_v, w1_v, w2_v, cos_c, sin_c):
    B, G, n_sub, width = subk.shape
    sub_spec = pl.BlockSpec((None, None, n_sub, width), lambda b, g: (b, g, 0, 0))
    full = lambda a: pl.BlockSpec(a.shape, lambda b, g: (0,) * a.ndim)
    out_spec = pl.BlockSpec((None, None, n_sub, HEAD_DIM), lambda b, g: (b, g, 0, 0))
    pe_k, pe_v = pe_k.reshape(1, -1), pe_v.reshape(1, -1)
    args = (pe_k, w1_k, w2_k, pe_v, w1_v, w2_v, cos_c, sin_c)
    return pl.pallas_call(
        _compress_kernel,
        grid=(B, G),
        in_specs=[sub_spec, sub_spec] + [full(a) for a in args],
        out_specs=[out_spec, out_spec],
        out_shape=[jax.ShapeDtypeStruct((B, G, n_sub, HEAD_DIM), BF16)] * 2,
        compiler_params=pltpu.CompilerParams(
            dimension_semantics=("parallel", "parallel"), vmem_limit_bytes=VMEM_LIMIT),
        name="nsa_compress",
    )(subk, subv, *args)


MAX_FLOOR = -1e20
MASK_BIG = 2.0 ** 100
LOG2_E = 1.4426950408889634
VT_ROWS = 80


def _nsa_kernel(q_ref, gate_ref, cos_ref, sin_ref, kc_ref, vc_ref, ks_ref, vst_ref, kw_ref, vwt_ref,
                mselt_ref, o_ref, blockbias_ref, *, n_pick):
    QB, HP, D = NSA_QUERY_TILE, NSA_HPG, HEAD_DIM
    qb = pl.program_id(2)
    n_cmp = kc_ref.shape[0]
    lanes4 = lambda x: jnp.concatenate([x] * HP, axis=1)

    heads = []
    for n in range(HP):
        qh = q_ref[n * D:(n + 1) * D, :]
        rot = jnp.concatenate([qh[D // 2:], qh[:D // 2]], axis=0)
        heads.append(qh * cos_ref[...] + rot * sin_ref[...])
    q4 = (jnp.concatenate(heads, axis=1) * (D ** -0.5 * LOG2_E)).astype(BF16)
    t_row = qb * QB + lax.broadcasted_iota(jnp.int32, (1, QB), 1)

    def softmax_cols(s_t, bias_t):
        sm = s_t + lanes4(bias_t)
        m = jnp.maximum(jnp.max(sm, axis=0, keepdims=True), MAX_FLOOR)
        return jnp.exp2(sm - m)

    cmp_end = lax.broadcasted_iota(jnp.int32, (n_cmp, 1), 0) * CMP_STRIDE + (CMP_BLOCK - 1)
    e_c = softmax_cols(jnp.dot(kc_ref[...], q4, preferred_element_type=F32),
                       jnp.where(cmp_end <= t_row, 0.0, -MASK_BIG))
    den_c = jnp.sum(e_c, axis=0, keepdims=True)
    p_c = e_c * (1.0 / jnp.where(den_c > 0.0, den_c, 1.0))
    o_c = _bdot_tn(vc_ref[...], p_c)
    p_sum = p_c[:, 0:QB]
    for n in range(1, HP):
        p_sum = p_sum + p_c[:, n * QB:(n + 1) * QB]
    p_hi = p_sum.astype(BF16)
    p_lo = (p_sum - p_hi.astype(F32)).astype(BF16)
    imp_t = (jnp.dot(mselt_ref[...], p_hi, preferred_element_type=F32)
             + jnp.dot(mselt_ref[...], p_lo, preferred_element_type=F32))

    j = lax.broadcasted_iota(jnp.int32, (SEL_LANES, QB), 0)
    cur = t_row // SEL_BLOCK
    valid = j * SEL_BLOCK <= t_row
    forced = (j == 0) | (j == cur) | (j == cur - 1)
    score = jnp.where(valid, imp_t + jnp.where(forced, FORCE_BONUS, 0.0), -1.0)
    for _ in range(n_pick):
        m = jnp.max(score, axis=0, keepdims=True)
        idx = jnp.min(jnp.where(score == m, j, SEL_LANES), axis=0, keepdims=True)
        score = jnp.where(j == idx, -2.0, score)
    blockbias_ref[...] = jnp.where((score == -2.0) & valid, 0.0, -MASK_BIG)

    KT = SEL_KEY_TILE
    blocks_per_tile = KT // SEL_BLOCK
    n_tiles = (qb * QB + QB + KT - 1) // KT

    def sel_step(kt, carry, causal):
        m_i, acc = carry
        start = pl.multiple_of(kt * KT, KT)
        s_t = jnp.dot(ks_ref[pl.ds(start, KT), :], q4, preferred_element_type=F32)
        bias = jnp.concatenate(
            [jnp.broadcast_to(blockbias_ref[pl.ds(kt * blocks_per_tile + jb, 1), :], (SEL_BLOCK, QB))
             for jb in range(blocks_per_tile)], axis=0)
        if causal:
            kpos = start + lax.broadcasted_iota(jnp.int32, (KT, 1), 0)
            bias = jnp.where(kpos <= t_row, bias, -MASK_BIG)
        sm = s_t + lanes4(bias)
        m_new = jnp.maximum(m_i, jnp.max(sm, axis=0, keepdims=True))
        e = jnp.exp2(sm - m_new).astype(BF16)
        acc_new = jnp.exp2(m_i - m_new) * acc + jnp.dot(vst_ref[:, pl.ds(start, KT)], e,
                                                        preferred_element_type=F32)
        return m_new, acc_new

    init = (jnp.full((1, HP * QB), MAX_FLOOR, F32), jnp.zeros((VT_ROWS, HP * QB), F32))
    carry = lax.fori_loop(0, n_tiles - 1, lambda kt, c: sel_step(kt, c, False), init)
    _, acc_s = sel_step(n_tiles - 1, carry, True)
    den_s = acc_s[D:D + 1]
    o_s = acc_s[:D] * (1.0 / jnp.where(den_s > 0.0, den_s, 1.0))

    span = WINDOW + QB
    w_start = pl.multiple_of(jnp.maximum(qb * QB - WINDOW, 0), QB)
    dist = t_row - (w_start + lax.broadcasted_iota(jnp.int32, (span, 1), 0))
    e_w = softmax_cols(jnp.dot(kw_ref[pl.ds(w_start, span), :], q4, preferred_element_type=F32),
                       jnp.where((dist >= 0) & (dist < WINDOW), 0.0, -MASK_BIG))
    acc_w = jnp.dot(vwt_ref[:, pl.ds(w_start, span)], e_w.astype(BF16), preferred_element_type=F32)
    den_w = acc_w[D:D + 1]
    o_w = acc_w[:D] * (1.0 / jnp.where(den_w > 0.0, den_w, 1.0))

    gates = _sigmoid(gate_ref[...])
    gate_row = lambda br: jnp.concatenate([gates[3 * n + br:3 * n + br + 1, :] for n in range(HP)], axis=1)
    o_t = gate_row(0) * o_c + gate_row(1) * o_s + gate_row(2) * o_w
    for n in range(HP):
        o_ref[:, n * D:(n + 1) * D] = o_t[:, n * QB:(n + 1) * QB].T


def _cmp_to_sel_matrix(n_cmp_rows, n_sel):
    ratio = SEL_BLOCK // CMP_STRIDE
    ci = np.arange(n_cmp_rows)[:, None]
    sj = np.arange(SEL_LANES)[None, :]
    m = sum(((ci + n) // ratio == sj).astype(np.float32) for n in range(CMP_BLOCK // CMP_STRIDE))
    m = m * (sj < n_sel) * (ci < n_cmp_rows - 1)
    return jnp.asarray(m.T, BF16)


def _nsa_attention(qg_t, kc, vc, ks, vst, kw, vwt, cos_t, sin_t):
    B, _, S = qg_t.shape
    n_sub = kc.shape[2]
    n_sel = S // SEL_BLOCK
    gw = NSA_HPG * HEAD_DIM
    gate_row0 = NSA_WIDTH // 128
    msel_t = _cmp_to_sel_matrix(n_sub, n_sel)
    at_bg = lambda shape: pl.BlockSpec((None, None) + shape, lambda b, g, i: (b, g, 0, 0))
    const = lambda a: pl.BlockSpec(a.shape, lambda b, g, i: (0, 0))
    return pl.pallas_call(
        functools.partial(_nsa_kernel, n_pick=min(N_SELECT, n_sel)),
        grid=(B, NSA_GROUPS, S // NSA_QUERY_TILE),
        in_specs=[pl.BlockSpec((None, gw, NSA_QUERY_TILE), lambda b, g, i: (b, g, i)),
                  pl.BlockSpec((None, 128, NSA_QUERY_TILE), lambda b, g, i: (b, gate_row0 + g, i)),
                  pl.BlockSpec((HEAD_DIM, NSA_QUERY_TILE), lambda b, g, i: (0, i)),
                  pl.BlockSpec((HEAD_DIM, NSA_QUERY_TILE), lambda b, g, i: (0, i)),
                  at_bg((n_sub, HEAD_DIM)), at_bg((n_sub, HEAD_DIM)),
                  at_bg((S, HEAD_DIM)), at_bg((VT_ROWS, S)), at_bg((S, HEAD_DIM)), at_bg((VT_ROWS, S)),
                  const(msel_t)],
        out_specs=pl.BlockSpec((None, NSA_QUERY_TILE, gw), lambda b, g, i: (b, i, g)),
        out_shape=jax.ShapeDtypeStruct((B, S, NSA_WIDTH), F32),
        scratch_shapes=[pltpu.VMEM((SEL_LANES, NSA_QUERY_TILE), F32)],
        compiler_params=pltpu.CompilerParams(
            dimension_semantics=("parallel", "parallel", "arbitrary"), vmem_limit_bytes=VMEM_LIMIT),
        name="nsa_attention",
    )(qg_t, qg_t, cos_t, sin_t, kc, vc, ks, vst, kw, vwt, msel_t)


def _nsa_branch(p_kv, qg_t, cmp_pe_k, cmp_w1_k, cmp_w2_k, cmp_pe_v, cmp_w1_v, cmp_w2_v):
    B, S, _ = p_kv.shape
    pos = jnp.arange(S)
    cos2, sin2 = _rope_tables(pos, NSA_GROUPS)
    kc_raw, vc_raw, ks, vst, kw, vwt = _kv_layout(p_kv, cos2, sin2)
    n_sub = S // CMP_STRIDE
    sub = lambda t: t.reshape(B, NSA_GROUPS, n_sub, CMP_STRIDE * HEAD_DIM)
    cos_c, sin_c = _rope_tables(jnp.arange(n_sub) * CMP_STRIDE + CMP_BLOCK - 1, 1)
    kc, vc = _compress(sub(kc_raw), sub(vc_raw), cmp_pe_k, cmp_w1_k, cmp_w2_k,
                       cmp_pe_v, cmp_w1_v, cmp_w2_v, cos_c, sin_c)
    cos_q, sin_q = _rope_tables(pos, 1)
    return _nsa_attention(qg_t, kc, vc, ks, vst, kw, vwt, cos_q.T, sin_q.T)


def _nsa_weight_columns(w_nsa):
    K = w_nsa.shape[0]
    q = w_nsa[:, :NSA_WIDTH]
    kv = w_nsa[:, NSA_WIDTH:NSA_WIDTH + 6 * NSA_KV_WIDTH]
    gates = w_nsa[:, NSA_WIDTH + 6 * NSA_KV_WIDTH:]
    per_group = NSA_HPG * 3
    gate_blocks = [jnp.pad(gates[:, g * per_group:(g + 1) * per_group], ((0, 0), (0, 128 - per_group)))
                   for g in range(NSA_GROUPS)]
    return jnp.concatenate([kv, q] + gate_blocks, axis=1)


def _layer_norm(h, g, b):
    mu = jnp.mean(h, axis=-1, keepdims=True)
    var = jnp.mean(jnp.square(h - mu), axis=-1, keepdims=True)
    return (h - mu) * lax.rsqrt(var + LN_EPS) * g + b


def _pack_bf16_halves(x):
    n = x.shape[-1] // 2
    bits = lax.bitcast_convert_type(x.astype(BF16).astype(F32), jnp.uint32)
    return (bits[:, n:] & jnp.uint32(0xFFFF0000)) | (bits[:, :n] >> 16)


def _unpack_bf16_halves(u):
    left = lax.bitcast_convert_type(u << 16, F32)
    right = lax.bitcast_convert_type(u & jnp.uint32(0xFFFF0000), F32)
    return left, right


def _mixer_out_kernel(x_ref, ya_ref, yb_ref, pg_ref, wa_ref, wb_ref, wo_ref, g_ref, b_ref, o_ref, op_ref,
                      *, alpha):
    d = x_ref.shape[-1]
    gate_a = _sigmoid(pg_ref[:, :d].astype(F32))
    gate_b = _sigmoid(pg_ref[:, d:].astype(F32))
    mixed = gate_a * _bdot(ya_ref[...], wa_ref[...]) + gate_b * _bdot(yb_ref[...], wb_ref[...])
    h = alpha * x_ref[...] + _bdot(mixed, wo_ref[...])
    out = _layer_norm(h, g_ref[...], b_ref[...])
    o_ref[...] = out
    op_ref[...] = _pack_bf16_halves(out)


def _mixer_out(xf, ya, yb, p_g, wa, wb, wo, ln_g, ln_b, alpha, tm=512):
    T, D = xf.shape
    rows = lambda w: pl.BlockSpec((tm, w), lambda i: (i, 0))
    full = lambda a: pl.BlockSpec(a.shape, lambda i: (0,) * a.ndim)
    ln_g, ln_b = ln_g.reshape(1, D), ln_b.reshape(1, D)
    return pl.pallas_call(
        functools.partial(_mixer_out_kernel, alpha=alpha),
        grid=(T // tm,),
        in_specs=[rows(D), rows(ya.shape[1]), rows(yb.shape[1]), rows(2 * D),
                  full(wa), full(wb), full(wo), full(ln_g), full(ln_b)],
        out_specs=[rows(D), rows(D // 2)],
        out_shape=[jax.ShapeDtypeStruct((T, D), F32), jax.ShapeDtypeStruct((T, D // 2), jnp.uint32)],
        compiler_params=pltpu.CompilerParams(
            dimension_semantics=("parallel",), vmem_limit_bytes=VMEM_LIMIT),
        name="mixer_out_ln",
    )(xf, ya, yb, p_g, wa, wb, wo, ln_g, ln_b)


ROUTER_TILE = 256
EXPERT_ROWS = 256
EXPERT_DMA_PIECES = 4
MOE_COMBINE_PARTS = 2
SC_TOKEN_CHUNK = 64
SC_ROW_CHUNK = 64
PICK_LANES = 128
LOWEST = -3.0e38


def _router_kernel(x_ref, rwt_ref, bias_ref, eidx_ref, wts_ref, pos_ref, cnt_ref, carry_ref):
    tm, E = x_ref.shape[0], rwt_ref.shape[0]
    per_group = E // N_GROUPS
    reps = tm // PICK_LANES

    @pl.when(pl.program_id(0) == 0)
    def _():
        carry_ref[...] = jnp.zeros_like(carry_ref)

    scores = _sigmoid(_dot3(rwt_ref[...], x_ref[...], (((1,), (1,)), ((), ()))))
    choice = scores + jnp.concatenate([bias_ref[...]] * reps, axis=1)
    row = lax.broadcasted_iota(jnp.int32, (E, tm), 0)

    def first_max(vals, rows):
        m = jnp.max(vals, axis=0, keepdims=True)
        return m, jnp.min(jnp.where(vals == m, rows, E), axis=0, keepdims=True)

    group_score = []
    for g in range(N_GROUPS):
        rows = slice(g * per_group, (g + 1) * per_group)
        group_row = g * per_group + lax.broadcasted_iota(jnp.int32, (per_group, tm), 0)
        m1, i1 = first_max(choice[rows], group_row)
        m2 = jnp.max(jnp.where(group_row == i1, LOWEST, choice[rows]), axis=0, keepdims=True)
        group_score.append(m1 + m2)
    masked = []
    for g in range(N_GROUPS):
        rank = jnp.zeros((1, tm), jnp.int32)
        for o in range(N_GROUPS):
            if o != g:
                ahead = (group_score[o] > group_score[g]) if o > g else (group_score[o] >= group_score[g])
                rank = rank + ahead.astype(jnp.int32)
        masked.append(jnp.where(rank < TOPK_GROUPS, choice[g * per_group:(g + 1) * per_group], NEG_INF))

    cur = jnp.concatenate(masked, axis=0)
    picks = []
    for _ in range(TOP_K):
        _, idx = first_max(cur, row)
        picks.append(idx)
        cur = jnp.where(row == idx, LOWEST, cur)
    sel = jnp.where(cur == LOWEST, 1.0, 0.0)
    gate = scores * sel
    gate = gate * (ROUTED_SCALE / jnp.sum(gate, axis=0, keepdims=True))

    ti = lax.broadcasted_iota(jnp.int32, (tm, tm), 0)
    tj = lax.broadcasted_iota(jnp.int32, (tm, tm), 1)
    sel_b = sel.astype(BF16)
    before = jnp.dot(sel_b, (ti < tj).astype(BF16), preferred_element_type=F32)
    queue_pos = before + jnp.concatenate([carry_ref[...]] * reps, axis=1)
    carry_ref[...] = carry_ref[...] + jnp.dot(sel_b, jnp.ones((tm, PICK_LANES), BF16),
                                              preferred_element_type=F32)
    cnt_ref[...] = carry_ref[...]

    at_pick = lambda vals, idx: jnp.sum(jnp.where(row == idx, vals, 0.0), axis=0, keepdims=True)
    eidx_ref[...] = jnp.concatenate(picks, axis=0)
    wts_ref[...] = jnp.concatenate([at_pick(gate, idx) for idx in picks], axis=0)
    pos_ref[...] = jnp.concatenate([at_pick(queue_pos, idx) for idx in picks], axis=0).astype(jnp.int32)


def _router(xf, router_w, router_bias):
    T, D = xf.shape
    E = router_w.shape[1]
    tm = ROUTER_TILE
    picks = lambda dt: jax.ShapeDtypeStruct((TOP_K, T), dt)
    pick_spec = pl.BlockSpec((TOP_K, tm), lambda i: (0, i))
    lanes = lambda v: jnp.broadcast_to(v.reshape(E, 1), (E, PICK_LANES))
    return pl.pallas_call(
        _router_kernel,
        grid=(T // tm,),
        in_specs=[pl.BlockSpec((tm, D), lambda i: (i, 0)),
                  pl.BlockSpec((E, D), lambda i: (0, 0)),
                  pl.BlockSpec((E, PICK_LANES), lambda i: (0, 0))],
        out_specs=[pick_spec, pick_spec, pick_spec, pl.BlockSpec((E, PICK_LANES), lambda i: (0, 0))],
        out_shape=[picks(jnp.int32), picks(F32), picks(jnp.int32),
                   jax.ShapeDtypeStruct((E, PICK_LANES), F32)],
        scratch_shapes=[pltpu.VMEM((E, PICK_LANES), F32)],
        compiler_params=pltpu.CompilerParams(
            dimension_semantics=("arbitrary",), vmem_limit_bytes=VMEM_LIMIT),
        name="moe_router",
    )(xf, router_w.T, lanes(router_bias))


def _dest_kernel(eidx_ref, pos_ref, start_ref, dest_ref):
    E = start_ref.shape[0]
    tm = eidx_ref.shape[1]
    row = lax.broadcasted_iota(jnp.int32, (E, tm), 0)
    start = jnp.concatenate([start_ref[...]] * (tm // PICK_LANES), axis=1)
    eidx = eidx_ref[...]
    base = [jnp.sum(jnp.where(row == eidx[kk:kk + 1, :], start, 0), axis=0, keepdims=True)
            for kk in range(TOP_K)]
    dest_ref[...] = jnp.concatenate(base, axis=0) + pos_ref[...]


def _dest_rows(eidx_t, pos_t, pad_start):
    T = eidx_t.shape[1]
    E = pad_start.shape[0]
    tm = ROUTER_TILE
    pick_spec = pl.BlockSpec((TOP_K, tm), lambda i: (0, i))
    return pl.pallas_call(
        _dest_kernel,
        grid=(T // tm,),
        in_specs=[pick_spec, pick_spec, pl.BlockSpec((E, PICK_LANES), lambda i: (0, 0))],
        out_specs=pick_spec,
        out_shape=jax.ShapeDtypeStruct((TOP_K, T), jnp.int32),
        compiler_params=pltpu.CompilerParams(
            dimension_semantics=("parallel",), vmem_limit_bytes=VMEM_LIMIT),
        name="moe_dest_rows",
    )(eidx_t, pos_t, jnp.broadcast_to(pad_start.reshape(E, 1), (E, PICK_LANES)))


def _sc_mesh():
    return plsc.VectorSubcoreMesh(core_axis_name="c", subcore_axis_name="s")


def _sc_scatter_rows(x, dest_t, n_rows):
    T, D = x.shape
    K = dest_t.shape[0]
    mesh = _sc_mesh()
    nc, nw = mesh.num_cores, mesh.num_cores * mesh.num_subcores
    per_w = T // nw
    chunk = min(SC_TOKEN_CHUNK, per_w)
    n_chunks = per_w // chunk
    idx = dest_t.reshape(K, nw, n_chunks, chunk).transpose(1, 2, 0, 3).reshape(nw, n_chunks * K, chunk)

    assert n_chunks % 2 == 0

    @functools.partial(
        pl.kernel, mesh=mesh,
        out_type=jax.ShapeDtypeStruct((n_rows, D), x.dtype),
        scratch_types=[pltpu.VMEM((n_chunks * K, chunk), jnp.int32),
                       pltpu.VMEM((2, chunk, D), x.dtype),
                       pltpu.SemaphoreType.DMA((2,)), pltpu.SemaphoreType.DMA((2,))],
    )
    def scatter(x_hbm, idx_hbm, out_hbm, idx_v, rows_v, load_sem, send_sem):
        wid = lax.axis_index("s") * nc + lax.axis_index("c")
        pltpu.sync_copy(idx_hbm.at[wid], idx_v)

        def load(j, b):
            return pltpu.make_async_copy(x_hbm.at[pl.ds(wid * per_w + j * chunk, chunk)], rows_v.at[b],
                                         load_sem.at[b])

        def sends(j, b):
            return [pltpu.make_async_copy(rows_v.at[b], out_hbm.at[idx_v.at[j * K + kk]], send_sem.at[b])
                    for kk in range(K)]

        load(0, 0).start()

        @pl.loop(0, n_chunks, step=2)
        def _(j0):
            for b in range(2):
                j = j0 + b
                load(j, b).wait()

                @pl.when(j >= 1)
                def _():
                    for c in sends(j - 1, 1 - b):
                        c.wait()

                @pl.when(j + 1 < n_chunks)
                def _():
                    load(j + 1, 1 - b).start()

                for c in sends(j, b):
                    c.start()

        for c in sends(n_chunks - 1, (n_chunks - 1) % 2):
            c.wait()

    return scatter(x, idx)


def _sc_gather_rows(src, idx):
    M = idx.shape[0]
    D = src.shape[1]
    mesh = _sc_mesh()
    nc, nw = mesh.num_cores, mesh.num_cores * mesh.num_subcores
    per_w = M // nw
    chunk = min(SC_ROW_CHUNK, per_w)
    n_chunks = per_w // chunk

    assert n_chunks % 2 == 0

    @functools.partial(
        pl.kernel, mesh=mesh,
        out_type=jax.ShapeDtypeStruct((M, D), src.dtype),
        scratch_types=[pltpu.VMEM((n_chunks, chunk), jnp.int32),
                       pltpu.VMEM((2, chunk, D), src.dtype),
                       pltpu.SemaphoreType.DMA((2,)), pltpu.SemaphoreType.DMA((2,))],
    )
    def gather(src_hbm, idx_hbm, out_hbm, idx_v, rows_v, fetch_sem, store_sem):
        wid = lax.axis_index("s") * nc + lax.axis_index("c")
        pltpu.sync_copy(idx_hbm.at[wid], idx_v)

        def fetch(j, b):
            return pltpu.make_async_copy(src_hbm.at[idx_v.at[j]], rows_v.at[b], fetch_sem.at[b])

        def store(j, b):
            return pltpu.make_async_copy(rows_v.at[b], out_hbm.at[pl.ds(wid * per_w + j * chunk, chunk)],
                                         store_sem.at[b])

        fetch(0, 0).start()

        @pl.loop(0, n_chunks, step=2)
        def _(j0):
            for b in range(2):
                j = j0 + b
                fetch(j, b).wait()

                @pl.when(j >= 1)
                def _():
                    store(j - 1, 1 - b).wait()

                @pl.when(j + 1 < n_chunks)
                def _():
                    fetch(j + 1, 1 - b).start()

                store(j, b).start()

        store(n_chunks - 1, (n_chunks - 1) % 2).wait()

    return gather(src, idx.reshape(nw, n_chunks, chunk))


def _expert_kernel(distinct_e_ref, blk_ord_ref, blk_new_ref, blk_rows_ref, n_used_ref, n_distinct_ref,
                   x_ref, wgu_hbm, wd_hbm, o_ref, wgu_buf, wd_buf, wgu_bf, wd_bf, sem):
    i = pl.program_id(0)
    live = i < n_used_ref[0]
    ordinal = blk_ord_ref[i]
    slot = ordinal % 2

    def weight_copies(k, s):
        e = distinct_e_ref[k]
        copies = []
        for m, (hbm, buf) in enumerate(((wgu_hbm, wgu_buf), (wd_hbm, wd_buf))):
            rows = buf.shape[1] // EXPERT_DMA_PIECES
            for piece in range(EXPERT_DMA_PIECES):
                cut = pl.ds(piece * rows, rows)
                copies.append(pltpu.make_async_copy(hbm.at[e, cut], buf.at[s, cut],
                                                    sem.at[m * EXPERT_DMA_PIECES + piece, s]))
        return copies

    @pl.when(i == 0)
    def _():
        for c in weight_copies(0, 0):
            c.start()

    @pl.when(live & (blk_new_ref[i] == 1))
    def _():
        for c in weight_copies(ordinal, slot):
            c.wait()

        @pl.when(ordinal + 1 < n_distinct_ref[0])
        def _():
            for c in weight_copies(ordinal + 1, 1 - slot):
                c.start()

        wgu_bf[...] = wgu_buf[slot].astype(BF16)
        wd_bf[...] = wd_buf[slot].astype(BF16)

    @pl.when(live)
    def _():
        hidden = wd_bf.shape[0]
        half = x_ref.shape[1]
        row = lax.broadcasted_iota(jnp.int32, x_ref.shape, 0)
        left, right = _unpack_bf16_halves(x_ref[...])
        real = row < blk_rows_ref[i]
        left = jnp.where(real, left, 0.0).astype(BF16)
        right = jnp.where(real, right, 0.0).astype(BF16)
        h = (jnp.dot(left, wgu_bf[:half, :], preferred_element_type=F32)
             + jnp.dot(right, wgu_bf[half:, :], preferred_element_type=F32))
        gate, up = h[:, :hidden], h[:, hidden:]
        act = (gate * _sigmoid(gate) * up).astype(BF16)
        o_ref[...] = _pack_bf16_halves(jnp.dot(act, wd_bf[...], preferred_element_type=F32))

    @pl.when(jnp.logical_not(live))
    def _():
        o_ref[...] = jnp.zeros_like(o_ref)


def _expert_ffn(xs, blk_e, blk_rows, n_used, w_gu, w_down):
    n_rows, half = xs.shape
    E, D, two_h = w_gu.shape
    n_blocks = n_rows // EXPERT_ROWS
    idx = jnp.arange(n_blocks, dtype=jnp.int32)
    is_live = idx < n_used[0]
    blk_new = (is_live & ((idx == 0) | (blk_e != jnp.roll(blk_e, 1)))).astype(jnp.int32)
    blk_ord = (jnp.cumsum(blk_new) - 1).astype(jnp.int32)
    n_distinct = blk_ord[-1:] + 1
    first_of = (blk_new[None, :] == 1) & (blk_ord[None, :] == idx[:, None])
    distinct_e = jnp.sum(jnp.where(first_of, blk_e[None, :], 0), axis=1).astype(jnp.int32)

    live = lambda i, nu: jnp.minimum(i, nu[0] - 1)
    grid_spec = pltpu.PrefetchScalarGridSpec(
        num_scalar_prefetch=6,
        grid=(n_blocks,),
        in_specs=[pl.BlockSpec((EXPERT_ROWS, half), lambda i, de, bo, bn, br, nu, nd: (live(i, nu), 0)),
                  pl.BlockSpec(memory_space=pl.ANY), pl.BlockSpec(memory_space=pl.ANY)],
        out_specs=pl.BlockSpec((EXPERT_ROWS, half), lambda i, de, bo, bn, br, nu, nd: (i, 0)),
        scratch_shapes=[pltpu.VMEM((2, D, two_h), F32), pltpu.VMEM((2, two_h // 2, D), F32),
                        pltpu.VMEM((D, two_h), BF16), pltpu.VMEM((two_h // 2, D), BF16),
                        pltpu.SemaphoreType.DMA((2 * EXPERT_DMA_PIECES, 2))],
    )
    return pl.pallas_call(
        _expert_kernel,
        grid_spec=grid_spec,
        out_shape=jax.ShapeDtypeStruct((n_rows, half), jnp.uint32),
        compiler_params=pltpu.CompilerParams(
            dimension_semantics=("arbitrary",), vmem_limit_bytes=VMEM_LIMIT),
        name="moe_experts",
    )(distinct_e, blk_ord, blk_new, blk_rows, n_used, n_distinct, xs, w_gu, w_down)


def _moe_out_kernel(x_ref, yk_ref, wts_ref, sgu_ref, sd_ref, g_ref, b_ref, *rest, alpha, has_prev):
    o_ref = rest[1] if has_prev else rest[0]
    x = x_ref[...]
    hidden = sd_ref.shape[0]
    h = _bdot(x, sgu_ref[...])
    gate, up = h[:, :hidden], h[:, hidden:]
    ffn = _bdot(gate * _sigmoid(gate) * up, sd_ref[...])
    wts = wts_ref[...]
    routed_left = routed_right = None
    for kk in range(TOP_K):
        left, right = _unpack_bf16_halves(yk_ref[kk])
        w = wts[:, kk:kk + 1]
        routed_left = w * left if kk == 0 else routed_left + w * left
        routed_right = w * right if kk == 0 else routed_right + w * right
    ffn = ffn + jnp.concatenate([routed_left, routed_right], axis=-1)
    o_ref[...] = _layer_norm(alpha * x + ffn, g_ref[...], b_ref[...])


def _moe_out(xf, yk_parts, wts, sw_gu, sw_down, ln_g, ln_b, alpha, tm=128):
    T, D = xf.shape
    steps = T // len(yk_parts) // tm
    full = lambda a: pl.BlockSpec(a.shape, lambda i: (0,) * a.ndim)
    ln_g, ln_b = ln_g.reshape(1, D), ln_b.reshape(1, D)
    out = None
    for p, yk in enumerate(yk_parts):
        rows = lambda w, off=p * steps: pl.BlockSpec((tm, w), lambda i: (i + off, 0))
        args = [xf, yk, wts, sw_gu, sw_down, ln_g, ln_b]
        in_specs = [rows(D), pl.BlockSpec((TOP_K, tm, D // 2), lambda i: (0, i, 0)), rows(PICK_LANES),
                    full(sw_gu), full(sw_down), full(ln_g), full(ln_b)]
        if out is not None:
            args.append(out)
            in_specs.append(pl.BlockSpec(memory_space=pl.ANY))
        out = pl.pallas_call(
            functools.partial(_moe_out_kernel, alpha=alpha, has_prev=out is not None),
            grid=(steps,),
            in_specs=in_specs,
            out_specs=rows(D),
            out_shape=jax.ShapeDtypeStruct((T, D), F32),
            input_output_aliases={len(args) - 1: 0} if out is not None else {},
            compiler_params=pltpu.CompilerParams(
                dimension_semantics=("parallel",), vmem_limit_bytes=VMEM_LIMIT),
            name="moe_combine_ln",
        )(*args)
    return out


def _moe_ffn_ln(xf, xp, router_w, router_bias, w_gu, w_down, sw_gu, sw_down, ln_g, ln_b, alpha):
    T, D = xf.shape
    E = router_w.shape[1]
    BM = EXPERT_ROWS
    eidx_t, wts_t, pos_t, cnt = _router(xf, router_w, router_bias)
    counts = cnt[:, 0].astype(jnp.int32)
    padded = (counts + BM - 1) // BM * BM
    pad_end = jnp.cumsum(padded)
    pad_start = pad_end - padded
    n_rows = T * TOP_K + E * BM
    n_blocks = n_rows // BM
    blk_row0 = jnp.arange(n_blocks, dtype=jnp.int32) * BM
    blk_e = jnp.minimum(jnp.sum((pad_end[None, :] <= blk_row0[:, None]).astype(jnp.int32), axis=1), E - 1)
    blk_rows = jnp.clip(pad_start[blk_e] + counts[blk_e] - blk_row0, 0, BM).astype(jnp.int32)
    n_used = (pad_end[-1:] // BM).astype(jnp.int32)
    dest_t = _dest_rows(eidx_t, pos_t, pad_start)
    wts = jnp.pad(wts_t.T, ((0, 0), (0, PICK_LANES - TOP_K)))
    xs = _sc_scatter_rows(xp, dest_t, n_rows)
    ys = _expert_ffn(xs, blk_e, blk_rows, n_used, w_gu, w_down)
    part = T // MOE_COMBINE_PARTS
    yk_parts = [_sc_gather_rows(ys, dest_t[:, p * part:(p + 1) * part].reshape(-1)).reshape(TOP_K, part, D // 2)
                for p in range(MOE_COMBINE_PARTS)]
    return _moe_out(xf, yk_parts, wts, sw_gu, sw_down, ln_g, ln_b, alpha)


def kernel(x, w_in, tshift_mu, rwkv_w0, rwkv_w2, rwkv_a0, rwkv_a2, rwkv_g2, rwkv_k_k, rwkv_k_a, rwkv_r_k, rwkv_lnx_w, rwkv_lnx_b, cmp_pe_k, cmp_w1_k, cmp_w2_k, cmp_pe_v, cmp_w1_v, cmp_w2_v, w_branch_a, w_branch_b, w_out, ln1_g, ln1_b, router_w, router_bias, exp_w_gu, exp_w_down, shared_w_gu, shared_w_down, ln2_g, ln2_b):
    B, S, D = x.shape
    depth = w_in.shape[0]
    alpha = (2 * depth) ** 0.25
    nsa_w = w_in.shape[2] - RWKV_IN_W - 2 * D
    for l in range(depth):
        xf = x.reshape(B * S, D)
        w_l = w_in[l]
        w_a = w_l[:, :RWKV_IN_W].astype(BF16)
        w_b = _nsa_weight_columns(w_l[:, RWKV_IN_W:RWKV_IN_W + nsa_w]).astype(BF16)
        w_g = w_l[:, RWKV_IN_W + nsa_w:].astype(BF16)
        kv_w = 6 * NSA_KV_WIDTH
        p_a = _matmul(xf, w_a, PROJ_ROWS, w_a.shape[1]).reshape(B, S, -1)
        p_kv = _matmul(xf, w_b[:, :kv_w], PROJ_ROWS, kv_w).reshape(B, S, -1)
        qg_t = _matmul_t(x, w_b[:, kv_w:].T, PROJ_ROWS)
        p_g = _matmul(xf, w_g, PROJ_ROWS, w_g.shape[1], BF16)
        y_a = _rwkv_time_mix(p_a, tshift_mu[l], rwkv_w0[l], rwkv_w2[l], rwkv_a0[l], rwkv_a2[l], rwkv_g2[l],
                             rwkv_k_k[l], rwkv_k_a[l], rwkv_r_k[l].reshape(-1), rwkv_lnx_w[l], rwkv_lnx_b[l])
        y_b = _nsa_branch(p_kv, qg_t, cmp_pe_k[l], cmp_w1_k[l], cmp_w2_k[l],
                          cmp_pe_v[l], cmp_w1_v[l], cmp_w2_v[l])
        x1, x1p = _mixer_out(xf, y_a.reshape(B * S, -1), y_b.reshape(B * S, -1), p_g,
                             w_branch_a[l].astype(BF16), w_branch_b[l].astype(BF16), w_out[l].astype(BF16),
                             ln1_g[l], ln1_b[l], alpha)
        x2 = _moe_ffn_ln(x1, x1p, router_w[l], router_bias[l], exp_w_gu[l], exp_w_down[l],
                         shared_w_gu[l].astype(BF16), shared_w_down[l].astype(BF16), ln2_g[l], ln2_b[l], alpha)
        x = x2.reshape(B, S, D)
    return x
```

```python
import functools

import numpy as np
import jax
import jax.numpy as jnp
from jax import lax
from jax.experimental import pallas as pl
from jax.experimental.pallas import tpu as pltpu
from jax.experimental.pallas import tpu_sc as plsc

F32 = jnp.float32
BF16 = jnp.bfloat16

RWKV_HEADS = 8
HEAD_DIM = 64
RWKV_WIDTH = RWKV_HEADS * HEAD_DIM
W_LORA = 64
A_LORA = 64
G_LORA = 128
GN_EPS = 64e-5
NSA_HEADS = 8
NSA_GROUPS = 2
NSA_HPG = NSA_HEADS // NSA_GROUPS
NSA_WIDTH = NSA_HEADS * HEAD_DIM
NSA_KV_WIDTH = NSA_GROUPS * HEAD_DIM
CMP_BLOCK = 32
CMP_STRIDE = 16
SEL_BLOCK = 64
N_SELECT = 16
WINDOW = 512
ROPE_THETA = 10000.0
RWKV_IN_W = 3 * RWKV_WIDTH + W_LORA + A_LORA + G_LORA
TOP_K = 8
N_GROUPS = 8
TOPK_GROUPS = 4
ROUTED_SCALE = 2.5
LN_EPS = 1e-5
NEG_INF = -1e30
FORCE_BONUS = 1e4

RWKV_CHUNK = 64
RWKV_HEAD_GROUP = 4
RWKV_STEP_CHUNKS = 2
VMEM_LIMIT = 56 * 1024 * 1024
PROJ_ROWS = 1024


def _bdot(a, b):
    return jnp.dot(a.astype(BF16), b.astype(BF16), preferred_element_type=F32)


def _bdot_tn(a, b):
    return lax.dot_general(a.astype(BF16), b.astype(BF16), (((0,), (0,)), ((), ())),
                           preferred_element_type=F32)


def _bf16_pieces(x, n):
    pieces = []
    for _ in range(n):
        p = x.astype(BF16)
        pieces.append(p)
        x = x - p.astype(F32)
    return pieces


def _dot3(a, b, dims=(((1,), (0,)), ((), ()))):
    (a_hi, a_lo), (b_hi, b_lo) = _bf16_pieces(a, 2), _bf16_pieces(b, 2)
    dot = lambda p, q: lax.dot_general(p, q, dims, preferred_element_type=F32)
    return dot(a_hi, b_hi) + (dot(a_hi, b_lo) + dot(a_lo, b_hi))


def _sigmoid(x):
    return 1.0 / (1.0 + jnp.exp(-x))


def _matmul_kernel(x_ref, w_ref, o_ref):
    o_ref[...] = jnp.dot(x_ref[...].astype(BF16), w_ref[...], preferred_element_type=F32).astype(o_ref.dtype)


def _matmul(x, w, tm, tn, out_dtype=F32):
    M, K = x.shape
    N = w.shape[1]
    return pl.pallas_call(
        _matmul_kernel,
        grid=(M // tm, N // tn),
        in_specs=[pl.BlockSpec((tm, K), lambda i, j: (i, 0)),
                  pl.BlockSpec((K, tn), lambda i, j: (0, j))],
        out_specs=pl.BlockSpec((tm, tn), lambda i, j: (i, j)),
        out_shape=jax.ShapeDtypeStruct((M, N), out_dtype),
        compiler_params=pltpu.CompilerParams(
            dimension_semantics=("parallel", "parallel"), vmem_limit_bytes=VMEM_LIMIT),
        name="dense_proj",
    )(x, w)


def _matmul_t_kernel(x_ref, wt_ref, o_ref):
    o_ref[...] = lax.dot_general(wt_ref[...], x_ref[...].astype(BF16), (((1,), (1,)), ((), ())),
                                 preferred_element_type=F32)


def _matmul_t(x, w_t, tm):
    B, S, K = x.shape
    N = w_t.shape[0]
    return pl.pallas_call(
        _matmul_t_kernel,
        grid=(B, S // tm),
        in_specs=[pl.BlockSpec((None, tm, K), lambda b, s: (b, s, 0)),
                  pl.BlockSpec((N, K), lambda b, s: (0, 0))],
        out_specs=pl.BlockSpec((None, N, tm), lambda b, s: (b, 0, s)),
        out_shape=jax.ShapeDtypeStruct((B, N, S), F32),
        compiler_params=pltpu.CompilerParams(
            dimension_semantics=("parallel", "parallel"), vmem_limit_bytes=VMEM_LIMIT),
        name="dense_proj_t",
    )(x, w_t)


def _rwkv_kernel(p_ref, mu_ref, w0_ref, w2_ref, a0_ref, a2_ref, g2_ref, kk_ref, ka_ref, rk_ref,
                 lnw_ref, lnb_ref, o_ref, carry_ref, state_ref):
    C, H, N = RWKV_CHUNK, RWKV_HEADS, HEAD_DIM
    W = RWKV_WIDTH
    B = p_ref.shape[0]
    NC = p_ref.shape[1] // C
    L = NC * C
    R = B * L

    @pl.when(pl.program_id(0) == 0)
    def _():
        carry_ref[...] = jnp.zeros_like(carry_ref)
        state_ref[...] = jnp.zeros_like(state_ref)

    def per_block(x, rows):
        return jnp.concatenate(
            [jnp.broadcast_to(x[i].reshape(1, -1), (rows, x.shape[-1])) for i in range(x.shape[0])], axis=0)

    p = p_ref[...].reshape(R, p_ref.shape[-1])
    row = lax.broadcasted_iota(jnp.int32, p.shape, 0)
    prev = jnp.where(row % L == 0, per_block(carry_ref[...], L), pltpu.roll(p, 1, axis=0))
    for b in range(B):
        carry_ref[b] = p[b * L + L - 1:b * L + L, :]
    xs = p + (prev - p) * mu_ref[...]
    r = xs[:, 0:W]
    k = xs[:, W:2 * W]
    v = xs[:, 2 * W:3 * W]
    wl = xs[:, 3 * W:3 * W + W_LORA]
    al = xs[:, 3 * W + W_LORA:3 * W + W_LORA + A_LORA]
    gl = xs[:, 3 * W + W_LORA + A_LORA:]

    z = -(w0_ref[...] + _dot3(jnp.tanh(wl), w2_ref[...]))
    softplus = jnp.maximum(z, 0.0) + jnp.log1p(jnp.exp(-jnp.abs(z)))
    logd = -jnp.exp(-softplus - 0.5)
    a = _sigmoid(a0_ref[...] + _dot3(al, a2_ref[...]))
    g = _dot3(_sigmoid(gl), g2_ref[...])

    kk = k * kk_ref[...]
    knew = k * (1.0 + (a - 1.0) * ka_ref[...])

    HG = RWKV_HEAD_GROUP
    GW = HG * N
    same_head_lanes = (lax.broadcasted_iota(jnp.int32, (GW, GW), 0) // N
                       == lax.broadcasted_iota(jnp.int32, (GW, GW), 1) // N)
    head_ones = jnp.where(same_head_lanes, 1.0, 0.0).astype(BF16)

    def head_sum(x):
        hi = x.astype(BF16)
        lo = (x - hi.astype(F32)).astype(BF16)
        return jnp.concatenate(
            [jnp.dot(hi[:, s:s + GW], head_ones, preferred_element_type=F32)
             + jnp.dot(lo[:, s:s + GW], head_ones, preferred_element_type=F32) for s in range(0, W, GW)],
            axis=-1)

    kk = kk / jnp.maximum(jnp.sqrt(head_sum(kk * kk)), 1e-12)
    lr_kk = kk * a

    ti = lax.broadcasted_iota(jnp.int32, (R, R), 0)
    tj = lax.broadcasted_iota(jnp.int32, (R, R), 1)
    same_chunk = (ti >= tj) & (ti // C == tj // C)
    tri = same_chunk.astype(BF16)
    cl = sum(jnp.dot(tri, piece, preferred_element_type=F32) for piece in reversed(_bf16_pieces(logd, 3)))
    cl_end = per_block(jnp.concatenate([cl[i * C + C - 1:i * C + C, :] for i in range(B * NC)], axis=0), C)
    a_hat = -kk * jnp.exp(cl - logd)
    r_hat = r * jnp.exp(cl)
    inv_gam = jnp.exp(-cl)
    b_til = lr_kk * inv_gam
    k_til = knew * inv_gam
    to_end = jnp.exp(cl_end - cl)
    b_end = lr_kk * to_end
    k_end = knew * to_end
    gam_end = jnp.exp(cl_end)

    gt = lax.broadcasted_iota(jnp.int32, (C, GW), 0)
    gc = lax.broadcasted_iota(jnp.int32, (C, GW), 1) % N
    strict = gt > gc
    incl = gt >= gc
    eye = (gt == gc).astype(F32)
    bi = lax.broadcasted_iota(jnp.int32, (HG * C, GW), 0) // C
    bj = lax.broadcasted_iota(jnp.int32, (HG * C, GW), 1) // N
    same_head = bi == bj

    def block_diag(y):
        yb = y.astype(BF16)
        return jnp.where(same_head, jnp.concatenate([yb] * HG, axis=0), jnp.zeros((), BF16))

    def bd_dot(x, y_bd):
        return jnp.dot(x.astype(BF16), y_bd, preferred_element_type=F32)

    def bd_dot_nt(x, y_bd):
        return lax.dot_general(x.astype(BF16), y_bd, (((1,), (1,)), ((), ())), preferred_element_type=F32)

    n_groups = H // HG
    units = [(b, c, gi) for b in range(B) for c in range(NC) for gi in range(n_groups)]
    n_units = range(len(units))
    cut = lambda x, b, c, gi: x[(b * NC + c) * C:(b * NC + c + 1) * C, gi * GW:(gi + 1) * GW]
    v_u = [cut(v, *un) for un in units]
    v_bd = [block_diag(v_u[i]) for i in n_units]
    ar = [jnp.concatenate([cut(a_hat, *un), cut(r_hat, *un)], axis=0) for un in units]
    mb = [bd_dot_nt(ar[i], block_diag(cut(b_til, *units[i]))) for i in n_units]
    mk = [bd_dot_nt(ar[i], block_diag(cut(k_til, *units[i]))) for i in n_units]
    n_ab = [jnp.where(strict, mb[i][:C], 0.0) for i in n_units]
    m_rb = [jnp.where(incl, mb[i][C:], 0.0) for i in n_units]
    l_ak = [jnp.where(strict, mk[i][:C], 0.0) for i in n_units]
    m_rk = [jnp.where(incl, mk[i][C:], 0.0) for i in n_units]

    pw = list(n_ab)
    pw_bd = [block_diag(pw[i]) for i in n_units]
    tinv = [eye + n_ab[i] for i in n_units]
    step = 2
    while step < C:
        pw = [bd_dot(pw[i], pw_bd[i]) for i in n_units]
        pw_bd = [block_diag(pw[i]) for i in n_units]
        tinv = [tinv[i] + bd_dot(tinv[i], pw_bd[i]) for i in n_units]
        step *= 2
    lv = [bd_dot(l_ak[i], v_bd[i]) for i in n_units]

    state = {(b, gi): state_ref[b * n_groups + gi] for b in range(B) for gi in range(n_groups)}
    outs = {}
    for c in range(NC):
        live = [i for i in n_units if units[i][1] == c]
        s0 = {i: state[(units[i][0], units[i][2])] for i in live}
        ars = {i: bd_dot_nt(ar[i], block_diag(s0[i])) for i in live}
        u = {i: bd_dot(tinv[i], block_diag(ars[i][:C] + lv[i])) for i in live}
        for i in live:
            outs[units[i]] = ars[i][C:] + bd_dot(m_rb[i], block_diag(u[i])) + bd_dot(m_rk[i], v_bd[i])
        for i in live:
            b, _, gi = units[i]
            uv = jnp.concatenate([u[i], v_u[i]], axis=0)
            bk_end = jnp.concatenate([cut(b_end, *units[i]), cut(k_end, *units[i])], axis=0)
            cross = jnp.where(same_head, _bdot_tn(uv, bk_end), 0.0)
            upd = cross[0:N]
            for h in range(1, HG):
                upd = upd + cross[h * N:(h + 1) * N]
            state[(b, gi)] = s0[i] * cut(gam_end, *units[i])[0:1] + upd
    for (b, gi), s_new in state.items():
        state_ref[b * n_groups + gi] = s_new

    o = jnp.concatenate([jnp.concatenate([outs[(b, c, gi)] for gi in range(n_groups)], axis=-1)
                         for b in range(B) for c in range(NC)], axis=0)
    mean = head_sum(o) * (1.0 / N)
    var = head_sum(jnp.square(o - mean)) * (1.0 / N)
    o = (o - mean) * lax.rsqrt(var + GN_EPS) * lnw_ref[...] + lnb_ref[...]
    bonus = head_sum(r * knew * rk_ref[...]) * v
    o_ref[...] = ((o + bonus) * g).reshape(o_ref.shape)


def _rwkv_time_mix(p_a, mu, w0, w2, a0, a2, g2, k_k, k_a, r_k, lnx_w, lnx_b):
    B, S, _ = p_a.shape
    L = RWKV_CHUNK * RWKV_STEP_CHUNKS
    row = lambda t: t.reshape(1, -1)
    full = lambda shape: pl.BlockSpec(shape, lambda s: (0,) * len(shape))
    n_units = B * RWKV_HEADS // RWKV_HEAD_GROUP
    return pl.pallas_call(
        _rwkv_kernel,
        grid=(S // L,),
        in_specs=[pl.BlockSpec((B, L, RWKV_IN_W), lambda s: (0, s, 0)),
                  full((1, RWKV_IN_W)), full((1, RWKV_WIDTH)), full((W_LORA, RWKV_WIDTH)),
                  full((1, RWKV_WIDTH)), full((A_LORA, RWKV_WIDTH)), full((G_LORA, RWKV_WIDTH)),
                  full((1, RWKV_WIDTH)), full((1, RWKV_WIDTH)), full((1, RWKV_WIDTH)),
                  full((1, RWKV_WIDTH)), full((1, RWKV_WIDTH))],
        out_specs=pl.BlockSpec((B, L, RWKV_WIDTH), lambda s: (0, s, 0)),
        out_shape=jax.ShapeDtypeStruct((B, S, RWKV_WIDTH), F32),
        scratch_shapes=[pltpu.VMEM((B, 1, RWKV_IN_W), F32),
                        pltpu.VMEM((n_units, HEAD_DIM, RWKV_HEAD_GROUP * HEAD_DIM), F32)],
        compiler_params=pltpu.CompilerParams(
            dimension_semantics=("arbitrary",), vmem_limit_bytes=VMEM_LIMIT),
        name="rwkv7_chunked",
    )(p_a, row(mu), row(w0), w2, row(a0), a2, g2, row(k_k), row(k_a), row(r_k), row(lnx_w), row(lnx_b))


NSA_KV_TILE = 1024
SEL_KEY_TILE = 1024
NSA_QUERY_TILE = 256
SEL_LANES = 128


def _rope_tables(pos, reps):
    half = HEAD_DIM // 2
    inv = ROPE_THETA ** (-jnp.arange(half, dtype=F32) / half)
    ang = pos.astype(F32)[:, None] * inv
    cos, sin = jnp.cos(ang), jnp.sin(ang)
    cosf = jnp.concatenate([cos, cos], -1)
    sinf = jnp.concatenate([-sin, sin], -1)
    return jnp.tile(cosf, (1, reps)), jnp.tile(sinf, (1, reps))


def _rope(x, cosf, sinf):
    width = x.shape[-1]
    lane = lax.broadcasted_iota(jnp.int32, x.shape, 1)
    first_half = (lane % HEAD_DIM) < HEAD_DIM // 2
    rot = jnp.where(first_half, pltpu.roll(x, width - HEAD_DIM // 2, axis=1),
                    pltpu.roll(x, HEAD_DIM // 2, axis=1))
    return x * cosf + rot * sinf


def _kv_layout_kernel(p_ref, cos_ref, sin_ref, kc_ref, vc_ref, ks_ref, vs_ref, kw_ref, vw_ref):
    ts = p_ref.shape[0]
    for i, o_ref in ((0, kc_ref), (1, vc_ref), (2, ks_ref), (4, kw_ref)):
        t = p_ref[:, i * NSA_KV_WIDTH:(i + 1) * NSA_KV_WIDTH]
        if i >= 2:
            t = _rope(t, cos_ref[...], sin_ref[...])
        for g in range(NSA_GROUPS):
            o_ref[g] = t[:, g * HEAD_DIM:(g + 1) * HEAD_DIM].astype(o_ref.dtype)
    pad_row = lax.broadcasted_iota(jnp.int32, (VT_ROWS - HEAD_DIM, ts), 0)
    ones_row = jnp.where(pad_row == 0, 1.0, 0.0)
    for i, o_ref in ((3, vs_ref), (5, vw_ref)):
        t_t = p_ref[:, i * NSA_KV_WIDTH:(i + 1) * NSA_KV_WIDTH].T
        for g in range(NSA_GROUPS):
            o_ref[g] = jnp.concatenate([t_t[g * HEAD_DIM:(g + 1) * HEAD_DIM], ones_row],
                                       axis=0).astype(o_ref.dtype)


def _kv_layout(p_b, cos2, sin2):
    B, S, _ = p_b.shape
    ts = min(NSA_KV_TILE, S)
    out_spec = pl.BlockSpec((None, NSA_GROUPS, ts, HEAD_DIM), lambda b, s: (b, 0, s, 0))
    vt_spec = pl.BlockSpec((None, NSA_GROUPS, VT_ROWS, ts), lambda b, s: (b, 0, 0, s))
    shp = lambda dt: jax.ShapeDtypeStruct((B, NSA_GROUPS, S, HEAD_DIM), dt)
    vt_shp = jax.ShapeDtypeStruct((B, NSA_GROUPS, VT_ROWS, S), BF16)
    return pl.pallas_call(
        _kv_layout_kernel,
        grid=(B, S // ts),
        in_specs=[pl.BlockSpec((None, ts, 6 * NSA_KV_WIDTH), lambda b, s: (b, s, 0)),
                  pl.BlockSpec((ts, NSA_KV_WIDTH), lambda b, s: (s, 0)),
                  pl.BlockSpec((ts, NSA_KV_WIDTH), lambda b, s: (s, 0))],
        out_specs=[out_spec, out_spec, out_spec, vt_spec, out_spec, vt_spec],
        out_shape=[shp(F32), shp(F32), shp(BF16), vt_shp, shp(BF16), vt_shp],
        compiler_params=pltpu.CompilerParams(
            dimension_semantics=("parallel", "parallel"), vmem_limit_bytes=VMEM_LIMIT),
        name="nsa_kv_layout",
    )(p_b, cos2, sin2)


def _compress_kernel(subk_ref, subv_ref, pek_ref, w1k_ref, w2k_ref, pev_ref, w1v_ref, w2v_ref,
                     cos_ref, sin_ref, kc_ref, vc_ref):
    n_sub = subk_ref.shape[0]
    half = CMP_STRIDE * HEAD_DIM

    def mlp(sub_ref, pe_ref, w1_ref, w2_ref):
        sub = sub_ref[...]
        top = _bdot(sub, w1_ref[:half, :])
        bot = _bdot(sub, w1_ref[half:, :])
        bias = _bdot(jnp.broadcast_to(pe_ref[...], (8, 2 * half)), w1_ref[...])[0:1, :]
        h = top + pltpu.roll(bot, n_sub - 1, axis=0) + bias
        return _bdot(jax.nn.gelu(h), w2_ref[...])

    kc = mlp(subk_ref, pek_ref, w1k_ref, w2k_ref)
    rot = jnp.concatenate([kc[:, HEAD_DIM // 2:], kc[:, :HEAD_DIM // 2]], axis=-1)
    kc_ref[...] = (kc * cos_ref[...] + rot * sin_ref[...]).astype(kc_ref.dtype)
    vc_ref[...] = mlp(subv_ref, pev_ref, w1v_ref, w2v_ref).astype(vc_ref.dtype)


def _compress(subk, subv, pe_k, w1_k, w2_k, pe_v, w1_v, w2_v, cos_c, sin_c):
    B, G, n_sub, width = subk.shape
    sub_spec = pl.BlockSpec((None, None, n_sub, width), lambda b, g: (b, g, 0, 0))
    full = lambda a: pl.BlockSpec(a.shape, lambda b, g: (0,) * a.ndim)
    out_spec = pl.BlockSpec((None, None, n_sub, HEAD_DIM), lambda b, g: (b, g, 0, 0))
    pe_k, pe_v = pe_k.reshape(1, -1), pe_v.reshape(1, -1)
    args = (pe_k, w1_k, w2_k, pe_v, w1_v, w2_v, cos_c, sin_c)
    return pl.pallas_call(
        _compress_kernel,
        grid=(B, G),
        in_specs=[sub_spec, sub_spec] + [full(a) for a in args],
        out_specs=[out_spec, out_spec],
        out_shape=[jax.ShapeDtypeStruct((B, G, n_sub, HEAD_DIM), BF16)] * 2,
        compiler_params=pltpu.CompilerParams(
            dimension_semantics=("parallel", "parallel"), vmem_limit_bytes=VMEM_LIMIT),
        name="nsa_compress",
    )(subk, subv, *args)


MAX_FLOOR = -1e20
MASK_BIG = 2.0 ** 100
LOG2_E = 1.4426950408889634
VT_ROWS = 80


def _nsa_kernel(q_ref, gate_ref, cos_ref, sin_ref, kc_ref, vc_ref, ks_ref, vst_ref, kw_ref, vwt_ref,
                mselt_ref, o_ref, blockbias_ref, *, n_pick):
    QB, HP, D = NSA_QUERY_TILE, NSA_HPG, HEAD_DIM
    qb = pl.program_id(2)
    n_cmp = kc_ref.shape[0]
    lanes4 = lambda x: jnp.concatenate([x] * HP, axis=1)

    heads = []
    for n in range(HP):
        qh = q_ref[n * D:(n + 1) * D, :]
        rot = jnp.concatenate([qh[D // 2:], qh[:D // 2]], axis=0)
        heads.append(qh * cos_ref[...] + rot * sin_ref[...])
    q4 = (jnp.concatenate(heads, axis=1) * (D ** -0.5 * LOG2_E)).astype(BF16)
    t_row = qb * QB + lax.broadcasted_iota(jnp.int32, (1, QB), 1)

    def softmax_cols(s_t, bias_t):
        sm = s_t + lanes4(bias_t)
        m = jnp.maximum(jnp.max(sm, axis=0, keepdims=True), MAX_FLOOR)
        return jnp.exp2(sm - m)

    cmp_end = lax.broadcasted_iota(jnp.int32, (n_cmp, 1), 0) * CMP_STRIDE + (CMP_BLOCK - 1)
    e_c = softmax_cols(jnp.dot(kc_ref[...], q4, preferred_element_type=F32),
                       jnp.where(cmp_end <= t_row, 0.0, -MASK_BIG))
    den_c = jnp.sum(e_c, axis=0, keepdims=True)
    p_c = e_c * (1.0 / jnp.where(den_c > 0.0, den_c, 1.0))
    o_c = _bdot_tn(vc_ref[...], p_c)
    p_sum = p_c[:, 0:QB]
    for n in range(1, HP):
        p_sum = p_sum + p_c[:, n * QB:(n + 1) * QB]
    p_hi = p_sum.astype(BF16)
    p_lo = (p_sum - p_hi.astype(F32)).astype(BF16)
    imp_t = (jnp.dot(mselt_ref[...], p_hi, preferred_element_type=F32)
             + jnp.dot(mselt_ref[...], p_lo, preferred_element_type=F32))

    j = lax.broadcasted_iota(jnp.int32, (SEL_LANES, QB), 0)
    cur = t_row // SEL_BLOCK
    valid = j * SEL_BLOCK <= t_row
    forced = (j == 0) | (j == cur) | (j == cur - 1)
    score = jnp.where(valid, imp_t + jnp.where(forced, FORCE_BONUS, 0.0), -1.0)
    for _ in range(n_pick):
        m = jnp.max(score, axis=0, keepdims=True)
        idx = jnp.min(jnp.where(score == m, j, SEL_LANES), axis=0, keepdims=True)
        score = jnp.where(j == idx, -2.0, score)
    blockbias_ref[...] = jnp.where((score == -2.0) & valid, 0.0, -MASK_BIG)

    KT = SEL_KEY_TILE
    blocks_per_tile = KT // SEL_BLOCK
    n_tiles = (qb * QB + QB + KT - 1) // KT

    def sel_step(kt, carry, causal):
        m_i, acc = carry
        start = pl.multiple_of(kt * KT, KT)
        s_t = jnp.dot(ks_ref[pl.ds(start, KT), :], q4, preferred_element_type=F32)
        bias = jnp.concatenate(
            [jnp.broadcast_to(blockbias_ref[pl.ds(kt * blocks_per_tile + jb, 1), :], (SEL_BLOCK, QB))
             for jb in range(blocks_per_tile)], axis=0)
        if causal:
            kpos = start + lax.broadcasted_iota(jnp.int32, (KT, 1), 0)
            bias = jnp.where(kpos <= t_row, bias, -MASK_BIG)
        sm = s_t + lanes4(bias)
        m_new = jnp.maximum(m_i, jnp.max(sm, axis=0, keepdims=True))
        e = jnp.exp2(sm - m_new).astype(BF16)
        acc_new = jnp.exp2(m_i - m_new) * acc + jnp.dot(vst_ref[:, pl.ds(start, KT)], e,
                                                        preferred_element_type=F32)
        return m_new, acc_new

    init = (jnp.full((1, HP * QB), MAX_FLOOR, F32), jnp.zeros((VT_ROWS, HP * QB), F32))
    carry = lax.fori_loop(0, n_tiles - 1, lambda kt, c: sel_step(kt, c, False), init)
    _, acc_s = sel_step(n_tiles - 1, carry, True)
    den_s = acc_s[D:D + 1]
    o_s = acc_s[:D] * (1.0 / jnp.where(den_s > 0.0, den_s, 1.0))

    span = WINDOW + QB
    w_start = pl.multiple_of(jnp.maximum(qb * QB - WINDOW, 0), QB)
    dist = t_row - (w_start + lax.broadcasted_iota(jnp.int32, (span, 1), 0))
    e_w = softmax_cols(jnp.dot(kw_ref[pl.ds(w_start, span), :], q4, preferred_element_type=F32),
                       jnp.where((dist >= 0) & (dist < WINDOW), 0.0, -MASK_BIG))
    acc_w = jnp.dot(vwt_ref[:, pl.ds(w_start, span)], e_w.astype(BF16), preferred_element_type=F32)
    den_w = acc_w[D:D + 1]
    o_w = acc_w[:D] * (1.0 / jnp.where(den_w > 0.0, den_w, 1.0))

    gates = _sigmoid(gate_ref[...])
    gate_row = lambda br: jnp.concatenate([gates[3 * n + br:3 * n + br + 1, :] for n in range(HP)], axis=1)
    o_t = gate_row(0) * o_c + gate_row(1) * o_s + gate_row(2) * o_w
    for n in range(HP):
        o_ref[:, n * D:(n + 1) * D] = o_t[:, n * QB:(n + 1) * QB].T


def _cmp_to_sel_matrix(n_cmp_rows, n_sel):
    ratio = SEL_BLOCK // CMP_STRIDE
    ci = np.arange(n_cmp_rows)[:, None]
    sj = np.arange(SEL_LANES)[None, :]
    m = sum(((ci + n) // ratio == sj).astype(np.float32) for n in range(CMP_BLOCK // CMP_STRIDE))
    m = m * (sj < n_sel) * (ci < n_cmp_rows - 1)
    return jnp.asarray(m.T, BF16)


def _nsa_attention(qg_t, kc, vc, ks, vst, kw, vwt, cos_t, sin_t):
    B, _, S = qg_t.shape
    n_sub = kc.shape[2]
    n_sel = S // SEL_BLOCK
    gw = NSA_HPG * HEAD_DIM
    gate_row0 = NSA_WIDTH // 128
    msel_t = _cmp_to_sel_matrix(n_sub, n_sel)
    at_bg = lambda shape: pl.BlockSpec((None, None) + shape, lambda b, g, i: (b, g, 0, 0))
    const = lambda a: pl.BlockSpec(a.shape, lambda b, g, i: (0, 0))
    return pl.pallas_call(
        functools.partial(_nsa_kernel, n_pick=min(N_SELECT, n_sel)),
        grid=(B, NSA_GROUPS, S // NSA_QUERY_TILE),
        in_specs=[pl.BlockSpec((None, gw, NSA_QUERY_TILE), lambda b, g, i: (b, g, i)),
                  pl.BlockSpec((None, 128, NSA_QUERY_TILE), lambda b, g, i: (b, gate_row0 + g, i)),
                  pl.BlockSpec((HEAD_DIM, NSA_QUERY_TILE), lambda b, g, i: (0, i)),
                  pl.BlockSpec((HEAD_DIM, NSA_QUERY_TILE), lambda b, g, i: (0, i)),
                  at_bg((n_sub, HEAD_DIM)), at_bg((n_sub, HEAD_DIM)),
                  at_bg((S, HEAD_DIM)), at_bg((VT_ROWS, S)), at_bg((S, HEAD_DIM)), at_bg((VT_ROWS, S)),
                  const(msel_t)],
        out_specs=pl.BlockSpec((None, NSA_QUERY_TILE, gw), lambda b, g, i: (b, i, g)),
        out_shape=jax.ShapeDtypeStruct((B, S, NSA_WIDTH), F32),
        scratch_shapes=[pltpu.VMEM((SEL_LANES, NSA_QUERY_TILE), F32)],
        compiler_params=pltpu.CompilerParams(
            dimension_semantics=("parallel", "parallel", "arbitrary"), vmem_limit_bytes=VMEM_LIMIT),
        name="nsa_attention",
    )(qg_t, qg_t, cos_t, sin_t, kc, vc, ks, vst, kw, vwt, msel_t)


def _nsa_branch(p_kv, qg_t, cmp_pe_k, cmp_w1_k, cmp_w2_k, cmp_pe_v, cmp_w1_v, cmp_w2_v):
    B, S, _ = p_kv.shape
    pos = jnp.arange(S)
    cos2, sin2 = _rope_tables(pos, NSA_GROUPS)
    kc_raw, vc_raw, ks, vst, kw, vwt = _kv_layout(p_kv, cos2, sin2)
    n_sub = S // CMP_STRIDE
    sub = lambda t: t.reshape(B, NSA_GROUPS, n_sub, CMP_STRIDE * HEAD_DIM)
    cos_c, sin_c = _rope_tables(jnp.arange(n_sub) * CMP_STRIDE + CMP_BLOCK - 1, 1)
    kc, vc = _compress(sub(kc_raw), sub(vc_raw), cmp_pe_k, cmp_w1_k, cmp_w2_k,
                       cmp_pe_v, cmp_w1_v, cmp_w2_v, cos_c, sin_c)
    cos_q, sin_q = _rope_tables(pos, 1)
    return _nsa_attention(qg_t, kc, vc, ks, vst, kw, vwt, cos_q.T, sin_q.T)


def _nsa_weight_columns(w_nsa):
    K = w_nsa.shape[0]
    q = w_nsa[:, :NSA_WIDTH]
    kv = w_nsa[:, NSA_WIDTH:NSA_WIDTH + 6 * NSA_KV_WIDTH]
    gates = w_nsa[:, NSA_WIDTH + 6 * NSA_KV_WIDTH:]
    per_group = NSA_HPG * 3
    gate_blocks = [jnp.pad(gates[:, g * per_group:(g + 1) * per_group], ((0, 0), (0, 128 - per_group)))
                   for g in range(NSA_GROUPS)]
    return jnp.concatenate([kv, q] + gate_blocks, axis=1)


def _layer_norm(h, g, b):
    mu = jnp.mean(h, axis=-1, keepdims=True)
    var = jnp.mean(jnp.square(h - mu), axis=-1, keepdims=True)
    return (h - mu) * lax.rsqrt(var + LN_EPS) * g + b


def _pack_bf16_halves(x):
    n = x.shape[-1] // 2
    bits = lax.bitcast_convert_type(x.astype(BF16).astype(F32), jnp.uint32)
    return (bits[:, n:] & jnp.uint32(0xFFFF0000)) | (bits[:, :n] >> 16)


def _unpack_bf16_halves(u):
    left = lax.bitcast_convert_type(u << 16, F32)
    right = lax.bitcast_convert_type(u & jnp.uint32(0xFFFF0000), F32)
    return left, right


def _mixer_out_kernel(x_ref, ya_ref, yb_ref, pg_ref, wa_ref, wb_ref, wo_ref, g_ref, b_ref, o_ref, op_ref,
                      *, alpha):
    d = x_ref.shape[-1]
    gate_a = _sigmoid(pg_ref[:, :d].astype(F32))
    gate_b = _sigmoid(pg_ref[:, d:].astype(F32))
    mixed = gate_a * _bdot(ya_ref[...], wa_ref[...]) + gate_b * _bdot(yb_ref[...], wb_ref[...])
    h = alpha * x_ref[...] + _bdot(mixed, wo_ref[...])
    out = _layer_norm(h, g_ref[...], b_ref[...])
    o_ref[...] = out
    op_ref[...] = _pack_bf16_halves(out)


def _mixer_out(xf, ya, yb, p_g, wa, wb, wo, ln_g, ln_b, alpha, tm=512):
    T, D = xf.shape
    rows = lambda w: pl.BlockSpec((tm, w), lambda i: (i, 0))
    full = lambda a: pl.BlockSpec(a.shape, lambda i: (0,) * a.ndim)
    ln_g, ln_b = ln_g.reshape(1, D), ln_b.reshape(1, D)
    return pl.pallas_call(
        functools.partial(_mixer_out_kernel, alpha=alpha),
        grid=(T // tm,),
        in_specs=[rows(D), rows(ya.shape[1]), rows(yb.shape[1]), rows(2 * D),
                  full(wa), full(wb), full(wo), full(ln_g), full(ln_b)],
        out_specs=[rows(D), rows(D // 2)],
        out_shape=[jax.ShapeDtypeStruct((T, D), F32), jax.ShapeDtypeStruct((T, D // 2), jnp.uint32)],
        compiler_params=pltpu.CompilerParams(
            dimension_semantics=("parallel",), vmem_limit_bytes=VMEM_LIMIT),
        name="mixer_out_ln",
    )(xf, ya, yb, p_g, wa, wb, wo, ln_g, ln_b)


ROUTER_TILE = 256
EXPERT_ROWS = 256
MOE_COMBINE_PARTS = 2
SC_TOKEN_CHUNK = 64
SC_ROW_CHUNK = 64
PICK_LANES = 128
LOWEST = -3.0e38


def _router_kernel(x_ref, rwt_ref, bias_ref, eidx_ref, wts_ref, pos_ref, cnt_ref, carry_ref):
    tm, E = x_ref.shape[0], rwt_ref.shape[0]
    per_group = E // N_GROUPS
    reps = tm // PICK_LANES

    @pl.when(pl.program_id(0) == 0)
    def _():
        carry_ref[...] = jnp.zeros_like(carry_ref)

    scores = _sigmoid(_dot3(rwt_ref[...], x_ref[...], (((1,), (1,)), ((), ()))))
    choice = scores + jnp.concatenate([bias_ref[...]] * reps, axis=1)
    row = lax.broadcasted_iota(jnp.int32, (E, tm), 0)

    def first_max(vals, rows):
        m = jnp.max(vals, axis=0, keepdims=True)
        return m, jnp.min(jnp.where(vals == m, rows, E), axis=0, keepdims=True)

    group_score = []
    for g in range(N_GROUPS):
        rows = slice(g * per_group, (g + 1) * per_group)
        group_row = g * per_group + lax.broadcasted_iota(jnp.int32, (per_group, tm), 0)
        m1, i1 = first_max(choice[rows], group_row)
        m2 = jnp.max(jnp.where(group_row == i1, LOWEST, choice[rows]), axis=0, keepdims=True)
        group_score.append(m1 + m2)
    masked = []
    for g in range(N_GROUPS):
        rank = jnp.zeros((1, tm), jnp.int32)
        for o in range(N_GROUPS):
            if o != g:
                ahead = (group_score[o] > group_score[g]) if o > g else (group_score[o] >= group_score[g])
                rank = rank + ahead.astype(jnp.int32)
        masked.append(jnp.where(rank < TOPK_GROUPS, choice[g * per_group:(g + 1) * per_group], NEG_INF))

    cur = jnp.concatenate(masked, axis=0)
    picks = []
    for _ in range(TOP_K):
        _, idx = first_max(cur, row)
        picks.append(idx)
        cur = jnp.where(row == idx, LOWEST, cur)
    sel = jnp.where(cur == LOWEST, 1.0, 0.0)
    gate = scores * sel
    gate = gate * (ROUTED_SCALE / jnp.sum(gate, axis=0, keepdims=True))

    ti = lax.broadcasted_iota(jnp.int32, (tm, tm), 0)
    tj = lax.broadcasted_iota(jnp.int32, (tm, tm), 1)
    sel_b = sel.astype(BF16)
    before = jnp.dot(sel_b, (ti < tj).astype(BF16), preferred_element_type=F32)
    queue_pos = before + jnp.concatenate([carry_ref[...]] * reps, axis=1)
    carry_ref[...] = carry_ref[...] + jnp.dot(sel_b, jnp.ones((tm, PICK_LANES), BF16),
                                              preferred_element_type=F32)
    cnt_ref[...] = carry_ref[...]

    at_pick = lambda vals, idx: jnp.sum(jnp.where(row == idx, vals, 0.0), axis=0, keepdims=True)
    eidx_ref[...] = jnp.concatenate(picks, axis=0)
    wts_ref[...] = jnp.concatenate([at_pick(gate, idx) for idx in picks], axis=0)
    pos_ref[...] = jnp.concatenate([at_pick(queue_pos, idx) for idx in picks], axis=0).astype(jnp.int32)


def _router(xf, router_w, router_bias):
    T, D = xf.shape
    E = router_w.shape[1]
    tm = ROUTER_TILE
    picks = lambda dt: jax.ShapeDtypeStruct((TOP_K, T), dt)
    pick_spec = pl.BlockSpec((TOP_K, tm), lambda i: (0, i))
    lanes = lambda v: jnp.broadcast_to(v.reshape(E, 1), (E, PICK_LANES))
    return pl.pallas_call(
        _router_kernel,
        grid=(T // tm,),
        in_specs=[pl.BlockSpec((tm, D), lambda i: (i, 0)),
                  pl.BlockSpec((E, D), lambda i: (0, 0)),
                  pl.BlockSpec((E, PICK_LANES), lambda i: (0, 0))],
        out_specs=[pick_spec, pick_spec, pick_spec, pl.BlockSpec((E, PICK_LANES), lambda i: (0, 0))],
        out_shape=[picks(jnp.int32), picks(F32), picks(jnp.int32),
                   jax.ShapeDtypeStruct((E, PICK_LANES), F32)],
        scratch_shapes=[pltpu.VMEM((E, PICK_LANES), F32)],
        compiler_params=pltpu.CompilerParams(
            dimension_semantics=("arbitrary",), vmem_limit_bytes=VMEM_LIMIT),
        name="moe_router",
    )(xf, router_w.T, lanes(router_bias))


def _dest_kernel(eidx_ref, pos_ref, start_ref, dest_ref):
    E = start_ref.shape[0]
    tm = eidx_ref.shape[1]
    row = lax.broadcasted_iota(jnp.int32, (E, tm), 0)
    start = jnp.concatenate([start_ref[...]] * (tm // PICK_LANES), axis=1)
    eidx = eidx_ref[...]
    base = [jnp.sum(jnp.where(row == eidx[kk:kk + 1, :], start, 0), axis=0, keepdims=True)
            for kk in range(TOP_K)]
    dest_ref[...] = jnp.concatenate(base, axis=0) + pos_ref[...]


def _dest_rows(eidx_t, pos_t, pad_start):
    T = eidx_t.shape[1]
    E = pad_start.shape[0]
    tm = ROUTER_TILE
    pick_spec = pl.BlockSpec((TOP_K, tm), lambda i: (0, i))
    return pl.pallas_call(
        _dest_kernel,
        grid=(T // tm,),
        in_specs=[pick_spec, pick_spec, pl.BlockSpec((E, PICK_LANES), lambda i: (0, 0))],
        out_specs=pick_spec,
        out_shape=jax.ShapeDtypeStruct((TOP_K, T), jnp.int32),
        compiler_params=pltpu.CompilerParams(
            dimension_semantics=("parallel",), vmem_limit_bytes=VMEM_LIMIT),
        name="moe_dest_rows",
    )(eidx_t, pos_t, jnp.broadcast_to(pad_start.reshape(E, 1), (E, PICK_LANES)))


def _sc_mesh():
    return plsc.VectorSubcoreMesh(core_axis_name="c", subcore_axis_name="s")


def _sc_scatter_rows(x, dest_t, n_rows):
    T, D = x.shape
    K = dest_t.shape[0]
    mesh = _sc_mesh()
    nc, nw = mesh.num_cores, mesh.num_cores * mesh.num_subcores
    per_w = T // nw
    chunk = min(SC_TOKEN_CHUNK, per_w)
    n_chunks = per_w // chunk
    idx = dest_t.reshape(K, nw, n_chunks, chunk).transpose(1, 2, 0, 3).reshape(nw, n_chunks * K, chunk)

    assert n_chunks % 2 == 0

    @functools.partial(
        pl.kernel, mesh=mesh,
        out_type=jax.ShapeDtypeStruct((n_rows, D), x.dtype),
        scratch_types=[pltpu.VMEM((n_chunks * K, chunk), jnp.int32),
                       pltpu.VMEM((2, chunk, D), x.dtype),
                       pltpu.SemaphoreType.DMA((2,)), pltpu.SemaphoreType.DMA((2,))],
    )
    def scatter(x_hbm, idx_hbm, out_hbm, idx_v, rows_v, load_sem, send_sem):
        wid = lax.axis_index("s") * nc + lax.axis_index("c")
        pltpu.sync_copy(idx_hbm.at[wid], idx_v)

        def load(j, b):
            return pltpu.make_async_copy(x_hbm.at[pl.ds(wid * per_w + j * chunk, chunk)], rows_v.at[b],
                                         load_sem.at[b])

        def sends(j, b):
            return [pltpu.make_async_copy(rows_v.at[b], out_hbm.at[idx_v.at[j * K + kk]], send_sem.at[b])
                    for kk in range(K)]

        load(0, 0).start()

        @pl.loop(0, n_chunks, step=2)
        def _(j0):
            for b in range(2):
                j = j0 + b
                load(j, b).wait()

                @pl.when(j >= 1)
                def _():
                    for c in sends(j - 1, 1 - b):
                        c.wait()

                @pl.when(j + 1 < n_chunks)
                def _():
                    load(j + 1, 1 - b).start()

                for c in sends(j, b):
                    c.start()

        for c in sends(n_chunks - 1, (n_chunks - 1) % 2):
            c.wait()

    return scatter(x, idx)


def _sc_gather_rows(src, idx):
    M = idx.shape[0]
    D = src.shape[1]
    mesh = _sc_mesh()
    nc, nw = mesh.num_cores, mesh.num_cores * mesh.num_subcores
    per_w = M // nw
    chunk = min(SC_ROW_CHUNK, per_w)
    n_chunks = per_w // chunk

    assert n_chunks % 2 == 0

    @functools.partial(
        pl.kernel, mesh=mesh,
        out_type=jax.ShapeDtypeStruct((M, D), src.dtype),
        scratch_types=[pltpu.VMEM((n_chunks, chunk), jnp.int32),
                       pltpu.VMEM((2, chunk, D), src.dtype),
                       pltpu.SemaphoreType.DMA((2,)), pltpu.SemaphoreType.DMA((2,))],
    )
    def gather(src_hbm, idx_hbm, out_hbm, idx_v, rows_v, fetch_sem, store_sem):
        wid = lax.axis_index("s") * nc + lax.axis_index("c")
        pltpu.sync_copy(idx_hbm.at[wid], idx_v)

        def fetch(j, b):
            return pltpu.make_async_copy(src_hbm.at[idx_v.at[j]], rows_v.at[b], fetch_sem.at[b])

        def store(j, b):
            return pltpu.make_async_copy(rows_v.at[b], out_hbm.at[pl.ds(wid * per_w + j * chunk, chunk)],
                                         store_sem.at[b])

        fetch(0, 0).start()

        @pl.loop(0, n_chunks, step=2)
        def _(j0):
            for b in range(2):
                j = j0 + b
                fetch(j, b).wait()

                @pl.when(j >= 1)
                def _():
                    store(j - 1, 1 - b).wait()

                @pl.when(j + 1 < n_chunks)
                def _():
                    fetch(j + 1, 1 - b).start()

                store(j, b).start()

        store(n_chunks - 1, (n_chunks - 1) % 2).wait()

    return gather(src, idx.reshape(nw, n_chunks, chunk))


def _expert_kernel(distinct_e_ref, blk_ord_ref, blk_new_ref, blk_rows_ref, n_used_ref, n_distinct_ref,
                   x_ref, wgu_hbm, wd_hbm, o_ref, wgu_buf, wd_buf, wgu_bf, wd_bf, sem):
    i = pl.program_id(0)
    live = i < n_used_ref[0]
    ordinal = blk_ord_ref[i]
    slot = ordinal % 2

    def weight_copies(k, s):
        e = distinct_e_ref[k]
        return (pltpu.make_async_copy(wgu_hbm.at[e], wgu_buf.at[s], sem.at[0, s]),
                pltpu.make_async_copy(wd_hbm.at[e], wd_buf.at[s], sem.at[1, s]))

    @pl.when(i == 0)
    def _():
        for c in weight_copies(0, 0):
            c.start()

    @pl.when(live & (blk_new_ref[i] == 1))
    def _():
        for c in weight_copies(ordinal, slot):
            c.wait()

        @pl.when(ordinal + 1 < n_distinct_ref[0])
        def _():
            for c in weight_copies(ordinal + 1, 1 - slot):
                c.start()

        wgu_bf[...] = wgu_buf[slot].astype(BF16)
        wd_bf[...] = wd_buf[slot].astype(BF16)

    @pl.when(live)
    def _():
        hidden = wd_bf.shape[0]
        half = x_ref.shape[1]
        row = lax.broadcasted_iota(jnp.int32, x_ref.shape, 0)
        left, right = _unpack_bf16_halves(x_ref[...])
        real = row < blk_rows_ref[i]
        left = jnp.where(real, left, 0.0).astype(BF16)
        right = jnp.where(real, right, 0.0).astype(BF16)
        h = (jnp.dot(left, wgu_bf[:half, :], preferred_element_type=F32)
             + jnp.dot(right, wgu_bf[half:, :], preferred_element_type=F32))
        gate, up = h[:, :hidden], h[:, hidden:]
        act = (gate * _sigmoid(gate) * up).astype(BF16)
        o_ref[...] = _pack_bf16_halves(jnp.dot(act, wd_bf[...], preferred_element_type=F32))

    @pl.when(jnp.logical_not(live))
    def _():
        o_ref[...] = jnp.zeros_like(o_ref)


def _expert_ffn(xs, blk_e, blk_rows, n_used, w_gu, w_down):
    n_rows, half = xs.shape
    E, D, two_h = w_gu.shape
    n_blocks = n_rows // EXPERT_ROWS
    idx = jnp.arange(n_blocks, dtype=jnp.int32)
    is_live = idx < n_used[0]
    blk_new = (is_live & ((idx == 0) | (blk_e != jnp.roll(blk_e, 1)))).astype(jnp.int32)
    blk_ord = (jnp.cumsum(blk_new) - 1).astype(jnp.int32)
    n_distinct = blk_ord[-1:] + 1
    first_of = (blk_new[None, :] == 1) & (blk_ord[None, :] == idx[:, None])
    distinct_e = jnp.sum(jnp.where(first_of, blk_e[None, :], 0), axis=1).astype(jnp.int32)

    live = lambda i, nu: jnp.minimum(i, nu[0] - 1)
    grid_spec = pltpu.PrefetchScalarGridSpec(
        num_scalar_prefetch=6,
        grid=(n_blocks,),
        in_specs=[pl.BlockSpec((EXPERT_ROWS, half), lambda i, de, bo, bn, br, nu, nd: (live(i, nu), 0)),
                  pl.BlockSpec(memory_space=pl.ANY), pl.BlockSpec(memory_space=pl.ANY)],
        out_specs=pl.BlockSpec((EXPERT_ROWS, half), lambda i, de, bo, bn, br, nu, nd: (i, 0)),
        scratch_shapes=[pltpu.VMEM((2, D, two_h), F32), pltpu.VMEM((2, two_h // 2, D), F32),
                        pltpu.VMEM((D, two_h), BF16), pltpu.VMEM((two_h // 2, D), BF16),
                        pltpu.SemaphoreType.DMA((2, 2))],
    )
    return pl.pallas_call(
        _expert_kernel,
        grid_spec=grid_spec,
        out_shape=jax.ShapeDtypeStruct((n_rows, half), jnp.uint32),
        compiler_params=pltpu.CompilerParams(
            dimension_semantics=("arbitrary",), vmem_limit_bytes=VMEM_LIMIT),
        name="moe_experts",
    )(distinct_e, blk_ord, blk_new, blk_rows, n_used, n_distinct, xs, w_gu, w_down)


def _moe_out_kernel(x_ref, yk_ref, wts_ref, sgu_ref, sd_ref, g_ref, b_ref, *rest, alpha, has_prev):
    o_ref = rest[1] if has_prev else rest[0]
    x = x_ref[...]
    hidden = sd_ref.shape[0]
    h = _bdot(x, sgu_ref[...])
    gate, up = h[:, :hidden], h[:, hidden:]
    ffn = _bdot(gate * _sigmoid(gate) * up, sd_ref[...])
    wts = wts_ref[...]
    routed_left = routed_right = None
    for kk in range(TOP_K):
        left, right = _unpack_bf16_halves(yk_ref[kk])
        w = wts[:, kk:kk + 1]
        routed_left = w * left if kk == 0 else routed_left + w * left
        routed_right = w * right if kk == 0 else routed_right + w * right
    ffn = ffn + jnp.concatenate([routed_left, routed_right], axis=-1)
    o_ref[...] = _layer_norm(alpha * x + ffn, g_ref[...], b_ref[...])


def _moe_out(xf, yk_parts, wts, sw_gu, sw_down, ln_g, ln_b, alpha, tm=128):
    T, D = xf.shape
    steps = T // len(yk_parts) // tm
    full = lambda a: pl.BlockSpec(a.shape, lambda i: (0,) * a.ndim)
    ln_g, ln_b = ln_g.reshape(1, D), ln_b.reshape(1, D)
    out = None
    for p, yk in enumerate(yk_parts):
        rows = lambda w, off=p * steps: pl.BlockSpec((tm, w), lambda i: (i + off, 0))
        args = [xf, yk, wts, sw_gu, sw_down, ln_g, ln_b]
        in_specs = [rows(D), pl.BlockSpec((TOP_K, tm, D // 2), lambda i: (0, i, 0)), rows(PICK_LANES),
                    full(sw_gu), full(sw_down), full(ln_g), full(ln_b)]
        if out is not None:
            args.append(out)
            in_specs.append(pl.BlockSpec(memory_space=pl.ANY))
        out = pl.pallas_call(
            functools.partial(_moe_out_kernel, alpha=alpha, has_prev=out is not None),
            grid=(steps,),
            in_specs=in_specs,
            out_specs=rows(D),
            out_shape=jax.ShapeDtypeStruct((T, D), F32),
            input_output_aliases={len(args) - 1: 0} if out is not None else {},
            compiler_params=pltpu.CompilerParams(
                dimension_semantics=("parallel",), vmem_limit_bytes=VMEM_LIMIT),
            name="moe_combine_ln",
        )(*args)
    return out


def _moe_ffn_ln(xf, xp, router_w, router_bias, w_gu, w_down, sw_gu, sw_down, ln_g, ln_b, alpha):
    T, D = xf.shape
    E = router_w.shape[1]
    BM = EXPERT_ROWS
    eidx_t, wts_t, pos_t, cnt = _router(xf, router_w, router_bias)
    counts = cnt[:, 0].astype(jnp.int32)
    padded = (counts + BM - 1) // BM * BM
    pad_end = jnp.cumsum(padded)
    pad_start = pad_end - padded
    n_rows = T * TOP_K + E * BM
    n_blocks = n_rows // BM
    blk_row0 = jnp.arange(n_blocks, dtype=jnp.int32) * BM
    blk_e = jnp.minimum(jnp.sum((pad_end[None, :] <= blk_row0[:, None]).astype(jnp.int32), axis=1), E - 1)
    blk_rows = jnp.clip(pad_start[blk_e] + counts[blk_e] - blk_row0, 0, BM).astype(jnp.int32)
    n_used = (pad_end[-1:] // BM).astype(jnp.int32)
    dest_t = _dest_rows(eidx_t, pos_t, pad_start)
    wts = jnp.pad(wts_t.T, ((0, 0), (0, PICK_LANES - TOP_K)))
    xs = _sc_scatter_rows(xp, dest_t, n_rows)
    ys = _expert_ffn(xs, blk_e, blk_rows, n_used, w_gu, w_down)
    part = T // MOE_COMBINE_PARTS
    yk_parts = [_sc_gather_rows(ys, dest_t[:, p * part:(p + 1) * part].reshape(-1)).reshape(TOP_K, part, D // 2)
                for p in range(MOE_COMBINE_PARTS)]
    return _moe_out(xf, yk_parts, wts, sw_gu, sw_down, ln_g, ln_b, alpha)


def kernel(x, w_in, tshift_mu, rwkv_w0, rwkv_w2, rwkv_a0, rwkv_a2, rwkv_g2, rwkv_k_k, rwkv_k_a, rwkv_r_k, rwkv_lnx_w, rwkv_lnx_b, cmp_pe_k, cmp_w1_k, cmp_w2_k, cmp_pe_v, cmp_w1_v, cmp_w2_v, w_branch_a, w_branch_b, w_out, ln1_g, ln1_b, router_w, router_bias, exp_w_gu, exp_w_down, shared_w_gu, shared_w_down, ln2_g, ln2_b):
    B, S, D = x.shape
    depth = w_in.shape[0]
    alpha = (2 * depth) ** 0.25
    nsa_w = w_in.shape[2] - RWKV_IN_W - 2 * D
    for l in range(depth):
        xf = x.reshape(B * S, D)
        w_l = w_in[l]
        w_a = w_l[:, :RWKV_IN_W].astype(BF16)
        w_b = _nsa_weight_columns(w_l[:, RWKV_IN_W:RWKV_IN_W + nsa_w]).astype(BF16)
        w_g = w_l[:, RWKV_IN_W + nsa_w:].astype(BF16)
        kv_w = 6 * NSA_KV_WIDTH
        p_a = _matmul(xf, w_a, PROJ_ROWS, w_a.shape[1]).reshape(B, S, -1)
        p_kv = _matmul(xf, w_b[:, :kv_w], PROJ_ROWS, kv_w).reshape(B, S, -1)
        qg_t = _matmul_t(x, w_b[:, kv_w:].T, PROJ_ROWS)
        p_g = _matmul(xf, w_g, PROJ_ROWS, w_g.shape[1], BF16)
        y_a = _rwkv_time_mix(p_a, tshift_mu[l], rwkv_w0[l], rwkv_w2[l], rwkv_a0[l], rwkv_a2[l], rwkv_g2[l],
                             rwkv_k_k[l], rwkv_k_a[l], rwkv_r_k[l].reshape(-1), rwkv_lnx_w[l], rwkv_lnx_b[l])
        y_b = _nsa_branch(p_kv, qg_t, cmp_pe_k[l], cmp_w1_k[l], cmp_w2_k[l],
                          cmp_pe_v[l], cmp_w1_v[l], cmp_w2_v[l])
        x1, x1p = _mixer_out(xf, y_a.reshape(B * S, -1), y_b.reshape(B * S, -1), p_g,
                             w_branch_a[l].astype(BF16), w_branch_b[l].astype(BF16), w_out[l].astype(BF16),
                             ln1_g[l], ln1_b[l], alpha)
        x2 = _moe_ffn_ln(x1, x1p, router_w[l], router_bias[l], exp_w_gu[l], exp_w_down[l],
                         shared_w_gu[l].astype(BF16), shared_w_down[l].astype(BF16), ln2_g[l], ln2_b[l], alpha)
        x = x2.reshape(B, S, D)
    return x
```

```python
import functools

import numpy as np
import jax
import jax.numpy as jnp
from jax import lax
from jax.experimental import pallas as pl
from jax.experimental.pallas import tpu as pltpu
from jax.experimental.pallas import tpu_sc as plsc

F32 = jnp.float32
BF16 = jnp.bfloat16

RWKV_HEADS = 8
HEAD_DIM = 64
RWKV_WIDTH = RWKV_HEADS * HEAD_DIM
W_LORA = 64
A_LORA = 64
G_LORA = 128
GN_EPS = 64e-5
NSA_HEADS = 8
NSA_GROUPS = 2
NSA_HPG = NSA_HEADS // NSA_GROUPS
NSA_WIDTH = NSA_HEADS * HEAD_DIM
NSA_KV_WIDTH = NSA_GROUPS * HEAD_DIM
CMP_BLOCK = 32
CMP_STRIDE = 16
SEL_BLOCK = 64
N_SELECT = 16
WINDOW = 512
ROPE_THETA = 10000.0
RWKV_IN_W = 3 * RWKV_WIDTH + W_LORA + A_LORA + G_LORA
TOP_K = 8
N_GROUPS = 8
TOPK_GROUPS = 4
ROUTED_SCALE = 2.5
LN_EPS = 1e-5
NEG_INF = -1e30
FORCE_BONUS = 1e4

RWKV_CHUNK = 64
RWKV_HEAD_GROUP = 4
RWKV_STEP_CHUNKS = 2
VMEM_LIMIT = 56 * 1024 * 1024
PROJ_ROWS = 1024


def _bdot(a, b):
    return jnp.dot(a.astype(BF16), b.astype(BF16), preferred_element_type=F32)


def _bdot_tn(a, b):
    return lax.dot_general(a.astype(BF16), b.astype(BF16), (((0,), (0,)), ((), ())),
                           preferred_element_type=F32)


def _bf16_pieces(x, n):
    pieces = []
    for _ in range(n):
        p = x.astype(BF16)
        pieces.append(p)
        x = x - p.astype(F32)
    return pieces


def _dot3(a, b, dims=(((1,), (0,)), ((), ()))):
    (a_hi, a_lo), (b_hi, b_lo) = _bf16_pieces(a, 2), _bf16_pieces(b, 2)
    dot = lambda p, q: lax.dot_general(p, q, dims, preferred_element_type=F32)
    return dot(a_hi, b_hi) + (dot(a_hi, b_lo) + dot(a_lo, b_hi))


def _sigmoid(x):
    return 1.0 / (1.0 + jnp.exp(-x))


def _matmul_kernel(x_ref, w_ref, o_ref):
    o_ref[...] = jnp.dot(x_ref[...].astype(BF16), w_ref[...], preferred_element_type=F32).astype(o_ref.dtype)


def _matmul(x, w, tm, tn, out_dtype=F32):
    M, K = x.shape
    N = w.shape[1]
    return pl.pallas_call(
        _matmul_kernel,
        grid=(M // tm, N // tn),
        in_specs=[pl.BlockSpec((tm, K), lambda i, j: (i, 0)),
                  pl.BlockSpec((K, tn), lambda i, j: (0, j))],
        out_specs=pl.BlockSpec((tm, tn), lambda i, j: (i, j)),
        out_shape=jax.ShapeDtypeStruct((M, N), out_dtype),
        compiler_params=pltpu.CompilerParams(
            dimension_semantics=("parallel", "parallel"), vmem_limit_bytes=VMEM_LIMIT),
        name="dense_proj",
    )(x, w)


def _matmul_t_kernel(x_ref, wt_ref, o_ref):
    o_ref[...] = lax.dot_general(wt_ref[...], x_ref[...].astype(BF16), (((1,), (1,)), ((), ())),
                                 preferred_element_type=F32)


def _matmul_t(x, w_t, tm):
    B, S, K = x.shape
    N = w_t.shape[0]
    return pl.pallas_call(
        _matmul_t_kernel,
        grid=(B, S // tm),
        in_specs=[pl.BlockSpec((None, tm, K), lambda b, s: (b, s, 0)),
                  pl.BlockSpec((N, K), lambda b, s: (0, 0))],
        out_specs=pl.BlockSpec((None, N, tm), lambda b, s: (b, 0, s)),
        out_shape=jax.ShapeDtypeStruct((B, N, S), F32),
        compiler_params=pltpu.CompilerParams(
            dimension_semantics=("parallel", "parallel"), vmem_limit_bytes=VMEM_LIMIT),
        name="dense_proj_t",
    )(x, w_t)


def _rwkv_kernel(p_ref, mu_ref, w0_ref, w2_ref, a0_ref, a2_ref, g2_ref, kk_ref, ka_ref, rk_ref,
                 lnw_ref, lnb_ref, o_ref, carry_ref, state_ref):
    C, H, N = RWKV_CHUNK, RWKV_HEADS, HEAD_DIM
    W = RWKV_WIDTH
    B = p_ref.shape[0]
    NC = p_ref.shape[1] // C
    L = NC * C
    R = B * L

    @pl.when(pl.program_id(0) == 0)
    def _():
        carry_ref[...] = jnp.zeros_like(carry_ref)
        state_ref[...] = jnp.zeros_like(state_ref)

    def per_block(x, rows):
        return jnp.concatenate(
            [jnp.broadcast_to(x[i].reshape(1, -1), (rows, x.shape[-1])) for i in range(x.shape[0])], axis=0)

    p = p_ref[...].reshape(R, p_ref.shape[-1])
    row = lax.broadcasted_iota(jnp.int32, p.shape, 0)
    prev = jnp.where(row % L == 0, per_block(carry_ref[...], L), pltpu.roll(p, 1, axis=0))
    for b in range(B):
        carry_ref[b] = p[b * L + L - 1:b * L + L, :]
    xs = p + (prev - p) * mu_ref[...]
    r = xs[:, 0:W]
    k = xs[:, W:2 * W]
    v = xs[:, 2 * W:3 * W]
    wl = xs[:, 3 * W:3 * W + W_LORA]
    al = xs[:, 3 * W + W_LORA:3 * W + W_LORA + A_LORA]
    gl = xs[:, 3 * W + W_LORA + A_LORA:]

    z = -(w0_ref[...] + _dot3(jnp.tanh(wl), w2_ref[...]))
    softplus = jnp.maximum(z, 0.0) + jnp.log1p(jnp.exp(-jnp.abs(z)))
    logd = -jnp.exp(-softplus - 0.5)
    a = _sigmoid(a0_ref[...] + _dot3(al, a2_ref[...]))
    g = _dot3(_sigmoid(gl), g2_ref[...])

    kk = k * kk_ref[...]
    knew = k * (1.0 + (a - 1.0) * ka_ref[...])

    HG = RWKV_HEAD_GROUP
    GW = HG * N
    same_head_lanes = (lax.broadcasted_iota(jnp.int32, (GW, GW), 0) // N
                       == lax.broadcasted_iota(jnp.int32, (GW, GW), 1) // N)
    head_ones = jnp.where(same_head_lanes, 1.0, 0.0).astype(BF16)

    def head_sum(x):
        hi = x.astype(BF16)
        lo = (x - hi.astype(F32)).astype(BF16)
        return jnp.concatenate(
            [jnp.dot(hi[:, s:s + GW], head_ones, preferred_element_type=F32)
             + jnp.dot(lo[:, s:s + GW], head_ones, preferred_element_type=F32) for s in range(0, W, GW)],
            axis=-1)

    kk = kk / jnp.maximum(jnp.sqrt(head_sum(kk * kk)), 1e-12)
    lr_kk = kk * a

    ti = lax.broadcasted_iota(jnp.int32, (R, R), 0)
    tj = lax.broadcasted_iota(jnp.int32, (R, R), 1)
    same_chunk = (ti >= tj) & (ti // C == tj // C)
    tri = same_chunk.astype(BF16)
    cl = sum(jnp.dot(tri, piece, preferred_element_type=F32) for piece in reversed(_bf16_pieces(logd, 3)))
    cl_end = per_block(jnp.concatenate([cl[i * C + C - 1:i * C + C, :] for i in range(B * NC)], axis=0), C)
    a_hat = -kk * jnp.exp(cl - logd)
    r_hat = r * jnp.exp(cl)
    inv_gam = jnp.exp(-cl)
    b_til = lr_kk * inv_gam
    k_til = knew * inv_gam
    to_end = jnp.exp(cl_end - cl)
    b_end = lr_kk * to_end
    k_end = knew * to_end
    gam_end = jnp.exp(cl_end)

    gt = lax.broadcasted_iota(jnp.int32, (C, GW), 0)
    gc = lax.broadcasted_iota(jnp.int32, (C, GW), 1) % N
    strict = gt > gc
    incl = gt >= gc
    eye = (gt == gc).astype(F32)
    bi = lax.broadcasted_iota(jnp.int32, (HG * C, GW), 0) // C
    bj = lax.broadcasted_iota(jnp.int32, (HG * C, GW), 1) // N
    same_head = bi == bj

    def block_diag(y):
        yb = y.astype(BF16)
        return jnp.where(same_head, jnp.concatenate([yb] * HG, axis=0), jnp.zeros((), BF16))

    def bd_dot(x, y_bd):
        return jnp.dot(x.astype(BF16), y_bd, preferred_element_type=F32)

    def bd_dot_nt(x, y_bd):
        return lax.dot_general(x.astype(BF16), y_bd, (((1,), (1,)), ((), ())), preferred_element_type=F32)

    n_groups = H // HG
    units = [(b, c, gi) for b in range(B) for c in range(NC) for gi in range(n_groups)]
    n_units = range(len(units))
    cut = lambda x, b, c, gi: x[(b * NC + c) * C:(b * NC + c + 1) * C, gi * GW:(gi + 1) * GW]
    v_u = [cut(v, *un) for un in units]
    v_bd = [block_diag(v_u[i]) for i in n_units]
    ar = [jnp.concatenate([cut(a_hat, *un), cut(r_hat, *un)], axis=0) for un in units]
    mb = [bd_dot_nt(ar[i], block_diag(cut(b_til, *units[i]))) for i in n_units]
    mk = [bd_dot_nt(ar[i], block_diag(cut(k_til, *units[i]))) for i in n_units]
    n_ab = [jnp.where(strict, mb[i][:C], 0.0) for i in n_units]
    m_rb = [jnp.where(incl, mb[i][C:], 0.0) for i in n_units]
    l_ak = [jnp.where(strict, mk[i][:C], 0.0) for i in n_units]
    m_rk = [jnp.where(incl, mk[i][C:], 0.0) for i in n_units]

    pw = list(n_ab)
    pw_bd = [block_diag(pw[i]) for i in n_units]
    tinv = [eye + n_ab[i] for i in n_units]
    step = 2
    while step < C:
        pw = [bd_dot(pw[i], pw_bd[i]) for i in n_units]
        pw_bd = [block_diag(pw[i]) for i in n_units]
        tinv = [tinv[i] + bd_dot(tinv[i], pw_bd[i]) for i in n_units]
        step *= 2
    lv = [bd_dot(l_ak[i], v_bd[i]) for i in n_units]

    state = {(b, gi): state_ref[b * n_groups + gi] for b in range(B) for gi in range(n_groups)}
    outs = {}
    for c in range(NC):
        live = [i for i in n_units if units[i][1] == c]
        s0 = {i: state[(units[i][0], units[i][2])] for i in live}
        ars = {i: bd_dot_nt(ar[i], block_diag(s0[i])) for i in live}
        u = {i: bd_dot(tinv[i], block_diag(ars[i][:C] + lv[i])) for i in live}
        for i in live:
            outs[units[i]] = ars[i][C:] + bd_dot(m_rb[i], block_diag(u[i])) + bd_dot(m_rk[i], v_bd[i])
        for i in live:
            b, _, gi = units[i]
            uv = jnp.concatenate([u[i], v_u[i]], axis=0)
            bk_end = jnp.concatenate([cut(b_end, *units[i]), cut(k_end, *units[i])], axis=0)
            cross = jnp.where(same_head, _bdot_tn(uv, bk_end), 0.0)
            upd = cross[0:N]
            for h in range(1, HG):
                upd = upd + cross[h * N:(h + 1) * N]
            state[(b, gi)] = s0[i] * cut(gam_end, *units[i])[0:1] + upd
    for (b, gi), s_new in state.items():
        state_ref[b * n_groups + gi] = s_new

    o = jnp.concatenate([jnp.concatenate([outs[(b, c, gi)] for gi in range(n_groups)], axis=-1)
                         for b in range(B) for c in range(NC)], axis=0)
    mean = head_sum(o) * (1.0 / N)
    var = head_sum(jnp.square(o - mean)) * (1.0 / N)
    o = (o - mean) * lax.rsqrt(var + GN_EPS) * lnw_ref[...] + lnb_ref[...]
    bonus = head_sum(r * knew * rk_ref[...]) * v
    o_ref[...] = ((o + bonus) * g).reshape(o_ref.shape)


def _rwkv_time_mix(p_a, mu, w0, w2, a0, a2, g2, k_k, k_a, r_k, lnx_w, lnx_b):
    B, S, _ = p_a.shape
    L = RWKV_CHUNK * RWKV_STEP_CHUNKS
    row = lambda t: t.reshape(1, -1)
    full = lambda shape: pl.BlockSpec(shape, lambda s: (0,) * len(shape))
    n_units = B * RWKV_HEADS // RWKV_HEAD_GROUP
    return pl.pallas_call(
        _rwkv_kernel,
        grid=(S // L,),
        in_specs=[pl.BlockSpec((B, L, RWKV_IN_W), lambda s: (0, s, 0)),
                  full((1, RWKV_IN_W)), full((1, RWKV_WIDTH)), full((W_LORA, RWKV_WIDTH)),
                  full((1, RWKV_WIDTH)), full((A_LORA, RWKV_WIDTH)), full((G_LORA, RWKV_WIDTH)),
                  full((1, RWKV_WIDTH)), full((1, RWKV_WIDTH)), full((1, RWKV_WIDTH)),
                  full((1, RWKV_WIDTH)), full((1, RWKV_WIDTH))],
        out_specs=pl.BlockSpec((B, L, RWKV_WIDTH), lambda s: (0, s, 0)),
        out_shape=jax.ShapeDtypeStruct((B, S, RWKV_WIDTH), F32),
        scratch_shapes=[pltpu.VMEM((B, 1, RWKV_IN_W), F32),
                        pltpu.VMEM((n_units, HEAD_DIM, RWKV_HEAD_GROUP * HEAD_DIM), F32)],
        compiler_params=pltpu.CompilerParams(
            dimension_semantics=("arbitrary",), vmem_limit_bytes=VMEM_LIMIT),
        name="rwkv7_chunked",
    )(p_a, row(mu), row(w0), w2, row(a0), a2, g2, row(k_k), row(k_a), row(r_k), row(lnx_w), row(lnx_b))


NSA_KV_TILE = 1024
SEL_KEY_TILE = 1024
NSA_QUERY_TILE = 256
SEL_LANES = 128


def _rope_tables(pos, reps):
    half = HEAD_DIM // 2
    inv = ROPE_THETA ** (-jnp.arange(half, dtype=F32) / half)
    ang = pos.astype(F32)[:, None] * inv
    cos, sin = jnp.cos(ang), jnp.sin(ang)
    cosf = jnp.concatenate([cos, cos], -1)
    sinf = jnp.concatenate([-sin, sin], -1)
    return jnp.tile(cosf, (1, reps)), jnp.tile(sinf, (1, reps))


def _rope(x, cosf, sinf):
    width = x.shape[-1]
    lane = lax.broadcasted_iota(jnp.int32, x.shape, 1)
    first_half = (lane % HEAD_DIM) < HEAD_DIM // 2
    rot = jnp.where(first_half, pltpu.roll(x, width - HEAD_DIM // 2, axis=1),
                    pltpu.roll(x, HEAD_DIM // 2, axis=1))
    return x * cosf + rot * sinf


def _kv_layout_kernel(p_ref, cos_ref, sin_ref, kc_ref, vc_ref, ks_ref, vs_ref, kw_ref, vw_ref):
    ts = p_ref.shape[0]
    for i, o_ref in ((0, kc_ref), (1, vc_ref), (2, ks_ref), (4, kw_ref)):
        t = p_ref[:, i * NSA_KV_WIDTH:(i + 1) * NSA_KV_WIDTH]
        if i >= 2:
            t = _rope(t, cos_ref[...], sin_ref[...])
        for g in range(NSA_GROUPS):
            o_ref[g] = t[:, g * HEAD_DIM:(g + 1) * HEAD_DIM].astype(o_ref.dtype)
    pad_row = lax.broadcasted_iota(jnp.int32, (VT_ROWS - HEAD_DIM, ts), 0)
    ones_row = jnp.where(pad_row == 0, 1.0, 0.0)
    for i, o_ref in ((3, vs_ref), (5, vw_ref)):
        t_t = p_ref[:, i * NSA_KV_WIDTH:(i + 1) * NSA_KV_WIDTH].T
        for g in range(NSA_GROUPS):
            o_ref[g] = jnp.concatenate([t_t[g * HEAD_DIM:(g + 1) * HEAD_DIM], ones_row],
                                       axis=0).astype(o_ref.dtype)


def _kv_layout(p_b, cos2, sin2):
    B, S, _ = p_b.shape
    ts = min(NSA_KV_TILE, S)
    out_spec = pl.BlockSpec((None, NSA_GROUPS, ts, HEAD_DIM), lambda b, s: (b, 0, s, 0))
    vt_spec = pl.BlockSpec((None, NSA_GROUPS, VT_ROWS, ts), lambda b, s: (b, 0, 0, s))
    shp = lambda dt: jax.ShapeDtypeStruct((B, NSA_GROUPS, S, HEAD_DIM), dt)
    vt_shp = jax.ShapeDtypeStruct((B, NSA_GROUPS, VT_ROWS, S), BF16)
    return pl.pallas_call(
        _kv_layout_kernel,
        grid=(B, S // ts),
        in_specs=[pl.BlockSpec((None, ts, 6 * NSA_KV_WIDTH), lambda b, s: (b, s, 0)),
                  pl.BlockSpec((ts, NSA_KV_WIDTH), lambda b, s: (s, 0)),
                  pl.BlockSpec((ts, NSA_KV_WIDTH), lambda b, s: (s, 0))],
        out_specs=[out_spec, out_spec, out_spec, vt_spec, out_spec, vt_spec],
        out_shape=[shp(F32), shp(F32), shp(BF16), vt_shp, shp(BF16), vt_shp],
        compiler_params=pltpu.CompilerParams(
            dimension_semantics=("parallel", "parallel"), vmem_limit_bytes=VMEM_LIMIT),
        name="nsa_kv_layout",
    )(p_b, cos2, sin2)


def _compress_kernel(subk_ref, subv_ref, pek_ref, w1k_ref, w2k_ref, pev_ref, w1v_ref, w2v_ref,
                     cos_ref, sin_ref, kc_ref, vc_ref):
    n_sub = subk_ref.shape[0]
    half = CMP_STRIDE * HEAD_DIM

    def mlp(sub_ref, pe_ref, w1_ref, w2_ref):
        sub = sub_ref[...]
        top = _bdot(sub, w1_ref[:half, :])
        bot = _bdot(sub, w1_ref[half:, :])
        bias = _bdot(jnp.broadcast_to(pe_ref[...], (8, 2 * half)), w1_ref[...])[0:1, :]
        h = top + pltpu.roll(bot, n_sub - 1, axis=0) + bias
        return _bdot(jax.nn.gelu(h), w2_ref[...])

    kc = mlp(subk_ref, pek_ref, w1k_ref, w2k_ref)
    rot = jnp.concatenate([kc[:, HEAD_DIM // 2:], kc[:, :HEAD_DIM // 2]], axis=-1)
    kc_ref[...] = (kc * cos_ref[...] + rot * sin_ref[...]).astype(kc_ref.dtype)
    vc_ref[...] = mlp(subv_ref, pev_ref, w1v_ref, w2v_ref).astype(vc_ref.dtype)


def _compress(subk, subv, pe_k, w1_k, w2_k, pe_v, w1_v, w2_v, cos_c, sin_c):
    B, G, n_sub, width = subk.shape
    sub_spec = pl.BlockSpec((None, None, n_sub, width), lambda b, g: (b, g, 0, 0))
    full = lambda a: pl.BlockSpec(a.shape, lambda b, g: (0,) * a.ndim)
    out_spec = pl.BlockSpec((None, None, n_sub, HEAD_DIM), lambda b, g: (b, g, 0, 0))
    pe_k, pe_v = pe_k.reshape(1, -1), pe_v.reshape(1, -1)
    args = (pe_k, w1_k, w2_k, pe_v, w1_v, w2_v, cos_c, sin_c)
    return pl.pallas_call(
        _compress_kernel,
        grid=(B, G),
        in_specs=[sub_spec, sub_spec] + [full(a) for a in args],
        out_specs=[out_spec, out_spec],
        out_shape=[jax.ShapeDtypeStruct((B, G, n_sub, HEAD_DIM), BF16)] * 2,
        compiler_params=pltpu.CompilerParams(
            dimension_semantics=("parallel", "parallel"), vmem_limit_bytes=VMEM_LIMIT),
        name="nsa_compress",
    )(subk, subv, *args)


MAX_FLOOR = -1e20
MASK_BIG = 2.0 ** 100
LOG2_E = 1.4426950408889634
VT_ROWS = 80


def _nsa_kernel(q_ref, gate_ref, cos_ref, sin_ref, kc_ref, vc_ref, ks_ref, vst_ref, kw_ref, vwt_ref,
                mselt_ref, o_ref, blockbias_ref, *, n_pick):
    QB, HP, D = NSA_QUERY_TILE, NSA_HPG, HEAD_DIM
    qb = pl.program_id(2)
    n_cmp = kc_ref.shape[0]
    lanes4 = lambda x: jnp.concatenate([x] * HP, axis=1)

    heads = []
    for n in range(HP):
        qh = q_ref[n * D:(n + 1) * D, :]
        rot = jnp.concatenate([qh[D // 2:], qh[:D // 2]], axis=0)
        heads.append(qh * cos_ref[...] + rot * sin_ref[...])
    q4 = (jnp.concatenate(heads, axis=1) * (D ** -0.5 * LOG2_E)).astype(BF16)
    t_row = qb * QB + lax.broadcasted_iota(jnp.int32, (1, QB), 1)

    def softmax_cols(s_t, bias_t):
        sm = s_t + lanes4(bias_t)
        m = jnp.maximum(jnp.max(sm, axis=0, keepdims=True), MAX_FLOOR)
        return jnp.exp2(sm - m)

    cmp_end = lax.broadcasted_iota(jnp.int32, (n_cmp, 1), 0) * CMP_STRIDE + (CMP_BLOCK - 1)
    e_c = softmax_cols(jnp.dot(kc_ref[...], q4, preferred_element_type=F32),
                       jnp.where(cmp_end <= t_row, 0.0, -MASK_BIG))
    den_c = jnp.sum(e_c, axis=0, keepdims=True)
    p_c = e_c * (1.0 / jnp.where(den_c > 0.0, den_c, 1.0))
    o_c = _bdot_tn(vc_ref[...], p_c)
    p_sum = p_c[:, 0:QB]
    for n in range(1, HP):
        p_sum = p_sum + p_c[:, n * QB:(n + 1) * QB]
    p_hi = p_sum.astype(BF16)
    p_lo = (p_sum - p_hi.astype(F32)).astype(BF16)
    imp_t = (jnp.dot(mselt_ref[...], p_hi, preferred_element_type=F32)
             + jnp.dot(mselt_ref[...], p_lo, preferred_element_type=F32))

    j = lax.broadcasted_iota(jnp.int32, (SEL_LANES, QB), 0)
    cur = t_row // SEL_BLOCK
    valid = j * SEL_BLOCK <= t_row
    forced = (j == 0) | (j == cur) | (j == cur - 1)
    score = jnp.where(valid, imp_t + jnp.where(forced, FORCE_BONUS, 0.0), -1.0)
    for _ in range(n_pick):
        m = jnp.max(score, axis=0, keepdims=True)
        idx = jnp.min(jnp.where(score == m, j, SEL_LANES), axis=0, keepdims=True)
        score = jnp.where(j == idx, -2.0, score)
    blockbias_ref[...] = jnp.where((score == -2.0) & valid, 0.0, -MASK_BIG)

    KT = SEL_KEY_TILE
    blocks_per_tile = KT // SEL_BLOCK
    n_tiles = (qb * QB + QB + KT - 1) // KT

    def sel_step(kt, carry, causal):
        m_i, acc = carry
        start = pl.multiple_of(kt * KT, KT)
        s_t = jnp.dot(ks_ref[pl.ds(start, KT), :], q4, preferred_element_type=F32)
        bias = jnp.concatenate(
            [jnp.broadcast_to(blockbias_ref[pl.ds(kt * blocks_per_tile + jb, 1), :], (SEL_BLOCK, QB))
             for jb in range(blocks_per_tile)], axis=0)
        if causal:
            kpos = start + lax.broadcasted_iota(jnp.int32, (KT, 1), 0)
            bias = jnp.where(kpos <= t_row, bias, -MASK_BIG)
        sm = s_t + lanes4(bias)
        m_new = jnp.maximum(m_i, jnp.max(sm, axis=0, keepdims=True))
        e = jnp.exp2(sm - m_new).astype(BF16)
        acc_new = jnp.exp2(m_i - m_new) * acc + jnp.dot(vst_ref[:, pl.ds(start, KT)], e,
                                                        preferred_element_type=F32)
        return m_new, acc_new

    init = (jnp.full((1, HP * QB), MAX_FLOOR, F32), jnp.zeros((VT_ROWS, HP * QB), F32))
    carry = lax.fori_loop(0, n_tiles - 1, lambda kt, c: sel_step(kt, c, False), init)
    _, acc_s = sel_step(n_tiles - 1, carry, True)
    den_s = acc_s[D:D + 1]
    o_s = acc_s[:D] * (1.0 / jnp.where(den_s > 0.0, den_s, 1.0))

    span = WINDOW + QB
    w_start = pl.multiple_of(jnp.maximum(qb * QB - WINDOW, 0), QB)
    dist = t_row - (w_start + lax.broadcasted_iota(jnp.int32, (span, 1), 0))
    e_w = softmax_cols(jnp.dot(kw_ref[pl.ds(w_start, span), :], q4, preferred_element_type=F32),
                       jnp.where((dist >= 0) & (dist < WINDOW), 0.0, -MASK_BIG))
    acc_w = jnp.dot(vwt_ref[:, pl.ds(w_start, span)], e_w.astype(BF16), preferred_element_type=F32)
    den_w = acc_w[D:D + 1]
    o_w = acc_w[:D] * (1.0 / jnp.where(den_w > 0.0, den_w, 1.0))

    gates = _sigmoid(gate_ref[...])
    gate_row = lambda br: jnp.concatenate([gates[3 * n + br:3 * n + br + 1, :] for n in range(HP)], axis=1)
    o_t = gate_row(0) * o_c + gate_row(1) * o_s + gate_row(2) * o_w
    for n in range(HP):
        o_ref[:, n * D:(n + 1) * D] = o_t[:, n * QB:(n + 1) * QB].T


def _cmp_to_sel_matrix(n_cmp_rows, n_sel):
    ratio = SEL_BLOCK // CMP_STRIDE
    ci = np.arange(n_cmp_rows)[:, None]
    sj = np.arange(SEL_LANES)[None, :]
    m = sum(((ci + n) // ratio == sj).astype(np.float32) for n in range(CMP_BLOCK // CMP_STRIDE))
    m = m * (sj < n_sel) * (ci < n_cmp_rows - 1)
    return jnp.asarray(m.T, BF16)


def _nsa_attention(qg_t, kc, vc, ks, vst, kw, vwt, cos_t, sin_t):
    B, _, S = qg_t.shape
    n_sub = kc.shape[2]
    n_sel = S // SEL_BLOCK
    gw = NSA_HPG * HEAD_DIM
    gate_row0 = NSA_WIDTH // 128
    msel_t = _cmp_to_sel_matrix(n_sub, n_sel)
    at_bg = lambda shape: pl.BlockSpec((None, None) + shape, lambda b, g, i: (b, g, 0, 0))
    const = lambda a: pl.BlockSpec(a.shape, lambda b, g, i: (0, 0))
    return pl.pallas_call(
        functools.partial(_nsa_kernel, n_pick=min(N_SELECT, n_sel)),
        grid=(B, NSA_GROUPS, S // NSA_QUERY_TILE),
        in_specs=[pl.BlockSpec((None, gw, NSA_QUERY_TILE), lambda b, g, i: (b, g, i)),
                  pl.BlockSpec((None, 128, NSA_QUERY_TILE), lambda b, g, i: (b, gate_row0 + g, i)),
                  pl.BlockSpec((HEAD_DIM, NSA_QUERY_TILE), lambda b, g, i: (0, i)),
                  pl.BlockSpec((HEAD_DIM, NSA_QUERY_TILE), lambda b, g, i: (0, i)),
                  at_bg((n_sub, HEAD_DIM)), at_bg((n_sub, HEAD_DIM)),
                  at_bg((S, HEAD_DIM)), at_bg((VT_ROWS, S)), at_bg((S, HEAD_DIM)), at_bg((VT_ROWS, S)),
                  const(msel_t)],
        out_specs=pl.BlockSpec((None, NSA_QUERY_TILE, gw), lambda b, g, i: (b, i, g)),
        out_shape=jax.ShapeDtypeStruct((B, S, NSA_WIDTH), F32),
        scratch_shapes=[pltpu.VMEM((SEL_LANES, NSA_QUERY_TILE), F32)],
        compiler_params=pltpu.CompilerParams(
            dimension_semantics=("parallel", "parallel", "arbitrary"), vmem_limit_bytes=VMEM_LIMIT),
        name="nsa_attention",
    )(qg_t, qg_t, cos_t, sin_t, kc, vc, ks, vst, kw, vwt, msel_t)


def _nsa_branch(p_kv, qg_t, cmp_pe_k, cmp_w1_k, cmp_w2_k, cmp_pe_v, cmp_w1_v, cmp_w2_v):
    B, S, _ = p_kv.shape
    pos = jnp.arange(S)
    cos2, sin2 = _rope_tables(pos, NSA_GROUPS)
    kc_raw, vc_raw, ks, vst, kw, vwt = _kv_layout(p_kv, cos2, sin2)
    n_sub = S // CMP_STRIDE
    sub = lambda t: t.reshape(B, NSA_GROUPS, n_sub, CMP_STRIDE * HEAD_DIM)
    cos_c, sin_c = _rope_tables(jnp.arange(n_sub) * CMP_STRIDE + CMP_BLOCK - 1, 1)
    kc, vc = _compress(sub(kc_raw), sub(vc_raw), cmp_pe_k, cmp_w1_k, cmp_w2_k,
                       cmp_pe_v, cmp_w1_v, cmp_w2_v, cos_c, sin_c)
    cos_q, sin_q = _rope_tables(pos, 1)
    return _nsa_attention(qg_t, kc, vc, ks, vst, kw, vwt, cos_q.T, sin_q.T)


def _nsa_weight_columns(w_nsa):
    K = w_nsa.shape[0]
    q = w_nsa[:, :NSA_WIDTH]
    kv = w_nsa[:, NSA_WIDTH:NSA_WIDTH + 6 * NSA_KV_WIDTH]
    gates = w_nsa[:, NSA_WIDTH + 6 * NSA_KV_WIDTH:]
    per_group = NSA_HPG * 3
    gate_blocks = [jnp.pad(gates[:, g * per_group:(g + 1) * per_group], ((0, 0), (0, 128 - per_group)))
                   for g in range(NSA_GROUPS)]
    return jnp.concatenate([kv, q] + gate_blocks, axis=1)


def _layer_norm(h, g, b):
    mu = jnp.mean(h, axis=-1, keepdims=True)
    var = jnp.mean(jnp.square(h - mu), axis=-1, keepdims=True)
    return (h - mu) * lax.rsqrt(var + LN_EPS) * g + b


def _pack_bf16_halves(x):
    n = x.shape[-1] // 2
    bits = lax.bitcast_convert_type(x.astype(BF16).astype(F32), jnp.uint32)
    return (bits[:, n:] & jnp.uint32(0xFFFF0000)) | (bits[:, :n] >> 16)


def _unpack_bf16_halves(u):
    left = lax.bitcast_convert_type(u << 16, F32)
    right = lax.bitcast_convert_type(u & jnp.uint32(0xFFFF0000), F32)
    return left, right


def _mixer_out_kernel(x_ref, ya_ref, yb_ref, pg_ref, wa_ref, wb_ref, wo_ref, g_ref, b_ref, o_ref, op_ref,
                      *, alpha):
    d = x_ref.shape[-1]
    gate_a = _sigmoid(pg_ref[:, :d].astype(F32))
    gate_b = _sigmoid(pg_ref[:, d:].astype(F32))
    mixed = gate_a * _bdot(ya_ref[...], wa_ref[...]) + gate_b * _bdot(yb_ref[...], wb_ref[...])
    h = alpha * x_ref[...] + _bdot(mixed, wo_ref[...])
    out = _layer_norm(h, g_ref[...], b_ref[...])
    o_ref[...] = out
    op_ref[...] = _pack_bf16_halves(out)


def _mixer_out(xf, ya, yb, p_g, wa, wb, wo, ln_g, ln_b, alpha, tm=512):
    T, D = xf.shape
    rows = lambda w: pl.BlockSpec((tm, w), lambda i: (i, 0))
    full = lambda a: pl.BlockSpec(a.shape, lambda i: (0,) * a.ndim)
    ln_g, ln_b = ln_g.reshape(1, D), ln_b.reshape(1, D)
    return pl.pallas_call(
        functools.partial(_mixer_out_kernel, alpha=alpha),
        grid=(T // tm,),
        in_specs=[rows(D), rows(ya.shape[1]), rows(yb.shape[1]), rows(2 * D),
                  full(wa), full(wb), full(wo), full(ln_g), full(ln_b)],
        out_specs=[rows(D), rows(D // 2)],
        out_shape=[jax.ShapeDtypeStruct((T, D), F32), jax.ShapeDtypeStruct((T, D // 2), jnp.uint32)],
        compiler_params=pltpu.CompilerParams(
            dimension_semantics=("parallel",), vmem_limit_bytes=VMEM_LIMIT),
        name="mixer_out_ln",
    )(xf, ya, yb, p_g, wa, wb, wo, ln_g, ln_b)


ROUTER_TILE = 256
EXPERT_ROWS = 512
MOE_COMBINE_PARTS = 2
SC_TOKEN_CHUNK = 64
SC_ROW_CHUNK = 64
PICK_LANES = 128
LOWEST = -3.0e38


def _router_kernel(x_ref, rwt_ref, bias_ref, eidx_ref, wts_ref, pos_ref, cnt_ref, carry_ref):
    tm, E = x_ref.shape[0], rwt_ref.shape[0]
    per_group = E // N_GROUPS
    reps = tm // PICK_LANES

    @pl.when(pl.program_id(0) == 0)
    def _():
        carry_ref[...] = jnp.zeros_like(carry_ref)

    scores = _sigmoid(_dot3(rwt_ref[...], x_ref[...], (((1,), (1,)), ((), ()))))
    choice = scores + jnp.concatenate([bias_ref[...]] * reps, axis=1)
    row = lax.broadcasted_iota(jnp.int32, (E, tm), 0)

    def first_max(vals, rows):
        m = jnp.max(vals, axis=0, keepdims=True)
        return m, jnp.min(jnp.where(vals == m, rows, E), axis=0, keepdims=True)

    group_score = []
    for g in range(N_GROUPS):
        rows = slice(g * per_group, (g + 1) * per_group)
        group_row = g * per_group + lax.broadcasted_iota(jnp.int32, (per_group, tm), 0)
        m1, i1 = first_max(choice[rows], group_row)
        m2 = jnp.max(jnp.where(group_row == i1, LOWEST, choice[rows]), axis=0, keepdims=True)
        group_score.append(m1 + m2)
    masked = []
    for g in range(N_GROUPS):
        rank = jnp.zeros((1, tm), jnp.int32)
        for o in range(N_GROUPS):
            if o != g:
                ahead = (group_score[o] > group_score[g]) if o > g else (group_score[o] >= group_score[g])
                rank = rank + ahead.astype(jnp.int32)
        masked.append(jnp.where(rank < TOPK_GROUPS, choice[g * per_group:(g + 1) * per_group], NEG_INF))

    cur = jnp.concatenate(masked, axis=0)
    picks = []
    for _ in range(TOP_K):
        _, idx = first_max(cur, row)
        picks.append(idx)
        cur = jnp.where(row == idx, LOWEST, cur)
    sel = jnp.where(cur == LOWEST, 1.0, 0.0)
    gate = scores * sel
    gate = gate * (ROUTED_SCALE / jnp.sum(gate, axis=0, keepdims=True))

    ti = lax.broadcasted_iota(jnp.int32, (tm, tm), 0)
    tj = lax.broadcasted_iota(jnp.int32, (tm, tm), 1)
    sel_b = sel.astype(BF16)
    before = jnp.dot(sel_b, (ti < tj).astype(BF16), preferred_element_type=F32)
    queue_pos = before + jnp.concatenate([carry_ref[...]] * reps, axis=1)
    carry_ref[...] = carry_ref[...] + jnp.dot(sel_b, jnp.ones((tm, PICK_LANES), BF16),
                                              preferred_element_type=F32)
    cnt_ref[...] = carry_ref[...]

    at_pick = lambda vals, idx: jnp.sum(jnp.where(row == idx, vals, 0.0), axis=0, keepdims=True)
    eidx_ref[...] = jnp.concatenate(picks, axis=0)
    wts_ref[...] = jnp.concatenate([at_pick(gate, idx) for idx in picks], axis=0)
    pos_ref[...] = jnp.concatenate([at_pick(queue_pos, idx) for idx in picks], axis=0).astype(jnp.int32)


def _router(xf, router_w, router_bias):
    T, D = xf.shape
    E = router_w.shape[1]
    tm = ROUTER_TILE
    picks = lambda dt: jax.ShapeDtypeStruct((TOP_K, T), dt)
    pick_spec = pl.BlockSpec((TOP_K, tm), lambda i: (0, i))
    lanes = lambda v: jnp.broadcast_to(v.reshape(E, 1), (E, PICK_LANES))
    return pl.pallas_call(
        _router_kernel,
        grid=(T // tm,),
        in_specs=[pl.BlockSpec((tm, D), lambda i: (i, 0)),
                  pl.BlockSpec((E, D), lambda i: (0, 0)),
                  pl.BlockSpec((E, PICK_LANES), lambda i: (0, 0))],
        out_specs=[pick_spec, pick_spec, pick_spec, pl.BlockSpec((E, PICK_LANES), lambda i: (0, 0))],
        out_shape=[picks(jnp.int32), picks(F32), picks(jnp.int32),
                   jax.ShapeDtypeStruct((E, PICK_LANES), F32)],
        scratch_shapes=[pltpu.VMEM((E, PICK_LANES), F32)],
        compiler_params=pltpu.CompilerParams(
            dimension_semantics=("arbitrary",), vmem_limit_bytes=VMEM_LIMIT),
        name="moe_router",
    )(xf, router_w.T, lanes(router_bias))


def _dest_kernel(eidx_ref, pos_ref, start_ref, dest_ref):
    E = start_ref.shape[0]
    tm = eidx_ref.shape[1]
    row = lax.broadcasted_iota(jnp.int32, (E, tm), 0)
    start = jnp.concatenate([start_ref[...]] * (tm // PICK_LANES), axis=1)
    eidx = eidx_ref[...]
    base = [jnp.sum(jnp.where(row == eidx[kk:kk + 1, :], start, 0), axis=0, keepdims=True)
            for kk in range(TOP_K)]
    dest_ref[...] = jnp.concatenate(base, axis=0) + pos_ref[...]


def _dest_rows(eidx_t, pos_t, pad_start):
    T = eidx_t.shape[1]
    E = pad_start.shape[0]
    tm = ROUTER_TILE
    pick_spec = pl.BlockSpec((TOP_K, tm), lambda i: (0, i))
    return pl.pallas_call(
        _dest_kernel,
        grid=(T // tm,),
        in_specs=[pick_spec, pick_spec, pl.BlockSpec((E, PICK_LANES), lambda i: (0, 0))],
        out_specs=pick_spec,
        out_shape=jax.ShapeDtypeStruct((TOP_K, T), jnp.int32),
        compiler_params=pltpu.CompilerParams(
            dimension_semantics=("parallel",), vmem_limit_bytes=VMEM_LIMIT),
        name="moe_dest_rows",
    )(eidx_t, pos_t, jnp.broadcast_to(pad_start.reshape(E, 1), (E, PICK_LANES)))


def _sc_mesh():
    return plsc.VectorSubcoreMesh(core_axis_name="c", subcore_axis_name="s")


def _sc_scatter_rows(x, dest_t, n_rows):
    T, D = x.shape
    K = dest_t.shape[0]
    mesh = _sc_mesh()
    nc, nw = mesh.num_cores, mesh.num_cores * mesh.num_subcores
    per_w = T // nw
    chunk = min(SC_TOKEN_CHUNK, per_w)
    n_chunks = per_w // chunk
    idx = dest_t.reshape(K, nw, n_chunks, chunk).transpose(1, 2, 0, 3).reshape(nw, n_chunks * K, chunk)

    assert n_chunks % 2 == 0

    @functools.partial(
        pl.kernel, mesh=mesh,
        out_type=jax.ShapeDtypeStruct((n_rows, D), x.dtype),
        scratch_types=[pltpu.VMEM((n_chunks * K, chunk), jnp.int32),
                       pltpu.VMEM((2, chunk, D), x.dtype),
                       pltpu.SemaphoreType.DMA((2,)), pltpu.SemaphoreType.DMA((2,))],
    )
    def scatter(x_hbm, idx_hbm, out_hbm, idx_v, rows_v, load_sem, send_sem):
        wid = lax.axis_index("s") * nc + lax.axis_index("c")
        pltpu.sync_copy(idx_hbm.at[wid], idx_v)

        def load(j, b):
            return pltpu.make_async_copy(x_hbm.at[pl.ds(wid * per_w + j * chunk, chunk)], rows_v.at[b],
                                         load_sem.at[b])

        def sends(j, b):
            return [pltpu.make_async_copy(rows_v.at[b], out_hbm.at[idx_v.at[j * K + kk]], send_sem.at[b])
                    for kk in range(K)]

        load(0, 0).start()

        @pl.loop(0, n_chunks, step=2)
        def _(j0):
            for b in range(2):
                j = j0 + b
                load(j, b).wait()

                @pl.when(j >= 1)
                def _():
                    for c in sends(j - 1, 1 - b):
                        c.wait()

                @pl.when(j + 1 < n_chunks)
                def _():
                    load(j + 1, 1 - b).start()

                for c in sends(j, b):
                    c.start()

        for c in sends(n_chunks - 1, (n_chunks - 1) % 2):
            c.wait()

    return scatter(x, idx)


def _sc_gather_rows(src, idx):
    M = idx.shape[0]
    D = src.shape[1]
    mesh = _sc_mesh()
    nc, nw = mesh.num_cores, mesh.num_cores * mesh.num_subcores
    per_w = M // nw
    chunk = min(SC_ROW_CHUNK, per_w)
    n_chunks = per_w // chunk

    assert n_chunks % 2 == 0

    @functools.partial(
        pl.kernel, mesh=mesh,
        out_type=jax.ShapeDtypeStruct((M, D), src.dtype),
        scratch_types=[pltpu.VMEM((n_chunks, chunk), jnp.int32),
                       pltpu.VMEM((2, chunk, D), src.dtype),
                       pltpu.SemaphoreType.DMA((2,)), pltpu.SemaphoreType.DMA((2,))],
    )
    def gather(src_hbm, idx_hbm, out_hbm, idx_v, rows_v, fetch_sem, store_sem):
        wid = lax.axis_index("s") * nc + lax.axis_index("c")
        pltpu.sync_copy(idx_hbm.at[wid], idx_v)

        def fetch(j, b):
            return pltpu.make_async_copy(src_hbm.at[idx_v.at[j]], rows_v.at[b], fetch_sem.at[b])

        def store(j, b):
            return pltpu.make_async_copy(rows_v.at[b], out_hbm.at[pl.ds(wid * per_w + j * chunk, chunk)],
                                         store_sem.at[b])

        fetch(0, 0).start()

        @pl.loop(0, n_chunks, step=2)
        def _(j0):
            for b in range(2):
                j = j0 + b
                fetch(j, b).wait()

                @pl.when(j >= 1)
                def _():
                    store(j - 1, 1 - b).wait()

                @pl.when(j + 1 < n_chunks)
                def _():
                    fetch(j + 1, 1 - b).start()

                store(j, b).start()

        store(n_chunks - 1, (n_chunks - 1) % 2).wait()

    return gather(src, idx.reshape(nw, n_chunks, chunk))


def _expert_kernel(distinct_e_ref, blk_ord_ref, blk_new_ref, blk_rows_ref, n_used_ref, n_distinct_ref,
                   x_ref, wgu_hbm, wd_hbm, o_ref, wgu_buf, wd_buf, wgu_bf, wd_bf, sem):
    i = pl.program_id(0)
    live = i < n_used_ref[0]
    ordinal = blk_ord_ref[i]
    slot = ordinal % 2

    def weight_copies(k, s):
        e = distinct_e_ref[k]
        return (pltpu.make_async_copy(wgu_hbm.at[e], wgu_buf.at[s], sem.at[0, s]),
                pltpu.make_async_copy(wd_hbm.at[e], wd_buf.at[s], sem.at[1, s]))

    @pl.when(i == 0)
    def _():
        for c in weight_copies(0, 0):
            c.start()

    @pl.when(live & (blk_new_ref[i] == 1))
    def _():
        for c in weight_copies(ordinal, slot):
            c.wait()

        @pl.when(ordinal + 1 < n_distinct_ref[0])
        def _():
            for c in weight_copies(ordinal + 1, 1 - slot):
                c.start()

        wgu_bf[...] = wgu_buf[slot].astype(BF16)
        wd_bf[...] = wd_buf[slot].astype(BF16)

    @pl.when(live)
    def _():
        hidden = wd_bf.shape[0]
        half = x_ref.shape[1]
        row = lax.broadcasted_iota(jnp.int32, x_ref.shape, 0)
        left, right = _unpack_bf16_halves(x_ref[...])
        real = row < blk_rows_ref[i]
        left = jnp.where(real, left, 0.0).astype(BF16)
        right = jnp.where(real, right, 0.0).astype(BF16)
        h = (jnp.dot(left, wgu_bf[:half, :], preferred_element_type=F32)
             + jnp.dot(right, wgu_bf[half:, :], preferred_element_type=F32))
        gate, up = h[:, :hidden], h[:, hidden:]
        act = (gate * _sigmoid(gate) * up).astype(BF16)
        o_ref[...] = _pack_bf16_halves(jnp.dot(act, wd_bf[...], preferred_element_type=F32))

    @pl.when(jnp.logical_not(live))
    def _():
        o_ref[...] = jnp.zeros_like(o_ref)


def _expert_ffn(xs, blk_e, blk_rows, n_used, w_gu, w_down):
    n_rows, half = xs.shape
    E, D, two_h = w_gu.shape
    n_blocks = n_rows // EXPERT_ROWS
    idx = jnp.arange(n_blocks, dtype=jnp.int32)
    is_live = idx < n_used[0]
    blk_new = (is_live & ((idx == 0) | (blk_e != jnp.roll(blk_e, 1)))).astype(jnp.int32)
    blk_ord = (jnp.cumsum(blk_new) - 1).astype(jnp.int32)
    n_distinct = blk_ord[-1:] + 1
    first_of = (blk_new[None, :] == 1) & (blk_ord[None, :] == idx[:, None])
    distinct_e = jnp.sum(jnp.where(first_of, blk_e[None, :], 0), axis=1).astype(jnp.int32)

    live = lambda i, nu: jnp.minimum(i, nu[0] - 1)
    grid_spec = pltpu.PrefetchScalarGridSpec(
        num_scalar_prefetch=6,
        grid=(n_blocks,),
        in_specs=[pl.BlockSpec((EXPERT_ROWS, half), lambda i, de, bo, bn, br, nu, nd: (live(i, nu), 0)),
                  pl.BlockSpec(memory_space=pl.ANY), pl.BlockSpec(memory_space=pl.ANY)],
        out_specs=pl.BlockSpec((EXPERT_ROWS, half), lambda i, de, bo, bn, br, nu, nd: (i, 0)),
        scratch_shapes=[pltpu.VMEM((2, D, two_h), F32), pltpu.VMEM((2, two_h // 2, D), F32),
                        pltpu.VMEM((D, two_h), BF16), pltpu.VMEM((two_h // 2, D), BF16),
                        pltpu.SemaphoreType.DMA((2, 2))],
    )
    return pl.pallas_call(
        _expert_kernel,
        grid_spec=grid_spec,
        out_shape=jax.ShapeDtypeStruct((n_rows, half), jnp.uint32),
        compiler_params=pltpu.CompilerParams(
            dimension_semantics=("arbitrary",), vmem_limit_bytes=VMEM_LIMIT),
        name="moe_experts",
    )(distinct_e, blk_ord, blk_new, blk_rows, n_used, n_distinct, xs, w_gu, w_down)


def _moe_out_kernel(x_ref, yk_ref, wts_ref, sgu_ref, sd_ref, g_ref, b_ref, *rest, alpha, has_prev):
    o_ref = rest[1] if has_prev else rest[0]
    x = x_ref[...]
    hidden = sd_ref.shape[0]
    h = _bdot(x, sgu_ref[...])
    gate, up = h[:, :hidden], h[:, hidden:]
    ffn = _bdot(gate * _sigmoid(gate) * up, sd_ref[...])
    wts = wts_ref[...]
    routed_left = routed_right = None
    for kk in range(TOP_K):
        left, right = _unpack_bf16_halves(yk_ref[kk])
        w = wts[:, kk:kk + 1]
        routed_left = w * left if kk == 0 else routed_left + w * left
        routed_right = w * right if kk == 0 else routed_right + w * right
    ffn = ffn + jnp.concatenate([routed_left, routed_right], axis=-1)
    o_ref[...] = _layer_norm(alpha * x + ffn, g_ref[...], b_ref[...])


def _moe_out(xf, yk_parts, wts, sw_gu, sw_down, ln_g, ln_b, alpha, tm=128):
    T, D = xf.shape
    steps = T // len(yk_parts) // tm
    full = lambda a: pl.BlockSpec(a.shape, lambda i: (0,) * a.ndim)
    ln_g, ln_b = ln_g.reshape(1, D), ln_b.reshape(1, D)
    out = None
    for p, yk in enumerate(yk_parts):
        rows = lambda w, off=p * steps: pl.BlockSpec((tm, w), lambda i: (i + off, 0))
        args = [xf, yk, wts, sw_gu, sw_down, ln_g, ln_b]
        in_specs = [rows(D), pl.BlockSpec((TOP_K, tm, D // 2), lambda i: (0, i, 0)), rows(PICK_LANES),
                    full(sw_gu), full(sw_down), full(ln_g), full(ln_b)]
        if out is not None:
            args.append(out)
            in_specs.append(pl.BlockSpec(memory_space=pl.ANY))
        out = pl.pallas_call(
            functools.partial(_moe_out_kernel, alpha=alpha, has_prev=out is not None),
            grid=(steps,),
            in_specs=in_specs,
            out_specs=rows(D),
            out_shape=jax.ShapeDtypeStruct((T, D), F32),
            input_output_aliases={len(args) - 1: 0} if out is not None else {},
            compiler_params=pltpu.CompilerParams(
                dimension_semantics=("parallel",), vmem_limit_bytes=VMEM_LIMIT),
            name="moe_combine_ln",
        )(*args)
    return out


def _moe_ffn_ln(xf, xp, router_w, router_bias, w_gu, w_down, sw_gu, sw_down, ln_g, ln_b, alpha):
    T, D = xf.shape
    E = router_w.shape[1]
    BM = EXPERT_ROWS
    eidx_t, wts_t, pos_t, cnt = _router(xf, router_w, router_bias)
    counts = cnt[:, 0].astype(jnp.int32)
    padded = (counts + BM - 1) // BM * BM
    pad_end = jnp.cumsum(padded)
    pad_start = pad_end - padded
    n_rows = T * TOP_K + E * BM
    n_blocks = n_rows // BM
    blk_row0 = jnp.arange(n_blocks, dtype=jnp.int32) * BM
    blk_e = jnp.minimum(jnp.sum((pad_end[None, :] <= blk_row0[:, None]).astype(jnp.int32), axis=1), E - 1)
    blk_rows = jnp.clip(pad_start[blk_e] + counts[blk_e] - blk_row0, 0, BM).astype(jnp.int32)
    n_used = (pad_end[-1:] // BM).astype(jnp.int32)
    dest_t = _dest_rows(eidx_t, pos_t, pad_start)
    wts = jnp.pad(wts_t.T, ((0, 0), (0, PICK_LANES - TOP_K)))
    xs = _sc_scatter_rows(xp, dest_t, n_rows)
    ys = _expert_ffn(xs, blk_e, blk_rows, n_used, w_gu, w_down)
    part = T // MOE_COMBINE_PARTS
    yk_parts = [_sc_gather_rows(ys, dest_t[:, p * part:(p + 1) * part].reshape(-1)).reshape(TOP_K, part, D // 2)
                for p in range(MOE_COMBINE_PARTS)]
    return _moe_out(xf, yk_parts, wts, sw_gu, sw_down, ln_g, ln_b, alpha)


def kernel(x, w_in, tshift_mu, rwkv_w0, rwkv_w2, rwkv_a0, rwkv_a2, rwkv_g2, rwkv_k_k, rwkv_k_a, rwkv_r_k, rwkv_lnx_w, rwkv_lnx_b, cmp_pe_k, cmp_w1_k, cmp_w2_k, cmp_pe_v, cmp_w1_v, cmp_w2_v, w_branch_a, w_branch_b, w_out, ln1_g, ln1_b, router_w, router_bias, exp_w_gu, exp_w_down, shared_w_gu, shared_w_down, ln2_g, ln2_b):
    B, S, D = x.shape
    depth = w_in.shape[0]
    alpha = (2 * depth) ** 0.25
    nsa_w = w_in.shape[2] - RWKV_IN_W - 2 * D
    for l in range(depth):
        xf = x.reshape(B * S, D)
        w_l = w_in[l]
        w_a = w_l[:, :RWKV_IN_W].astype(BF16)
        w_b = _nsa_weight_columns(w_l[:, RWKV_IN_W:RWKV_IN_W + nsa_w]).astype(BF16)
        w_g = w_l[:, RWKV_IN_W + nsa_w:].astype(BF16)
        kv_w = 6 * NSA_KV_WIDTH
        p_a = _matmul(xf, w_a, PROJ_ROWS, w_a.shape[1]).reshape(B, S, -1)
        p_kv = _matmul(xf, w_b[:, :kv_w], PROJ_ROWS, kv_w).reshape(B, S, -1)
        qg_t = _matmul_t(x, w_b[:, kv_w:].T, PROJ_ROWS)
        p_g = _matmul(xf, w_g, PROJ_ROWS, w_g.shape[1], BF16)
        y_a = _rwkv_time_mix(p_a, tshift_mu[l], rwkv_w0[l], rwkv_w2[l], rwkv_a0[l], rwkv_a2[l], rwkv_g2[l],
                             rwkv_k_k[l], rwkv_k_a[l], rwkv_r_k[l].reshape(-1), rwkv_lnx_w[l], rwkv_lnx_b[l])
        y_b = _nsa_branch(p_kv, qg_t, cmp_pe_k[l], cmp_w1_k[l], cmp_w2_k[l],
                          cmp_pe_v[l], cmp_w1_v[l], cmp_w2_v[l])
        x1, x1p = _mixer_out(xf, y_a.reshape(B * S, -1), y_b.reshape(B * S, -1), p_g,
                             w_branch_a[l].astype(BF16), w_branch_b[l].astype(BF16), w_out[l].astype(BF16),
                             ln1_g[l], ln1_b[l], alpha)
        x2 = _moe_ffn_ln(x1, x1p, router_w[l], router_bias[l], exp_w_gu[l], exp_w_down[l],
                         shared_w_gu[l].astype(BF16), shared_w_down[l].astype(BF16), ln2_g[l], ln2_b[l], alpha)
        x = x2.reshape(B, S, D)
    return x
```

```python
import functools

import numpy as np
import jax
import jax.numpy as jnp
from jax import lax
from jax.experimental import pallas as pl
from jax.experimental.pallas import tpu as pltpu
from jax.experimental.pallas import tpu_sc as plsc

F32 = jnp.float32
BF16 = jnp.bfloat16

RWKV_HEADS = 8
HEAD_DIM = 64
RWKV_WIDTH = RWKV_HEADS * HEAD_DIM
W_LORA = 64
A_LORA = 64
G_LORA = 128
GN_EPS = 64e-5
NSA_HEADS = 8
NSA_GROUPS = 2
NSA_HPG = NSA_HEADS // NSA_GROUPS
NSA_WIDTH = NSA_HEADS * HEAD_DIM
NSA_KV_WIDTH = NSA_GROUPS * HEAD_DIM
CMP_BLOCK = 32
CMP_STRIDE = 16
SEL_BLOCK = 64
N_SELECT = 16
WINDOW = 512
ROPE_THETA = 10000.0
RWKV_IN_W = 3 * RWKV_WIDTH + W_LORA + A_LORA + G_LORA
TOP_K = 8
N_GROUPS = 8
TOPK_GROUPS = 4
ROUTED_SCALE = 2.5
LN_EPS = 1e-5
NEG_INF = -1e30
FORCE_BONUS = 1e4

RWKV_CHUNK = 64
RWKV_HEAD_GROUP = 4
RWKV_STEP_CHUNKS = 2
VMEM_LIMIT = 56 * 1024 * 1024
PROJ_ROWS = 1024


def _bdot(a, b):
    return jnp.dot(a.astype(BF16), b.astype(BF16), preferred_element_type=F32)


def _bdot_tn(a, b):
    return lax.dot_general(a.astype(BF16), b.astype(BF16), (((0,), (0,)), ((), ())),
                           preferred_element_type=F32)


def _bf16_pieces(x, n):
    pieces = []
    for _ in range(n):
        p = x.astype(BF16)
        pieces.append(p)
        x = x - p.astype(F32)
    return pieces


def _dot3(a, b, dims=(((1,), (0,)), ((), ()))):
    (a_hi, a_lo), (b_hi, b_lo) = _bf16_pieces(a, 2), _bf16_pieces(b, 2)
    dot = lambda p, q: lax.dot_general(p, q, dims, preferred_element_type=F32)
    return dot(a_hi, b_hi) + (dot(a_hi, b_lo) + dot(a_lo, b_hi))


def _sigmoid(x):
    return 1.0 / (1.0 + jnp.exp(-x))


def _matmul_kernel(x_ref, w_ref, o_ref):
    o_ref[...] = jnp.dot(x_ref[...].astype(BF16), w_ref[...], preferred_element_type=F32).astype(o_ref.dtype)


def _matmul(x, w, tm, tn, out_dtype=F32):
    M, K = x.shape
    N = w.shape[1]
    return pl.pallas_call(
        _matmul_kernel,
        grid=(M // tm, N // tn),
        in_specs=[pl.BlockSpec((tm, K), lambda i, j: (i, 0)),
                  pl.BlockSpec((K, tn), lambda i, j: (0, j))],
        out_specs=pl.BlockSpec((tm, tn), lambda i, j: (i, j)),
        out_shape=jax.ShapeDtypeStruct((M, N), out_dtype),
        compiler_params=pltpu.CompilerParams(
            dimension_semantics=("parallel", "parallel"), vmem_limit_bytes=VMEM_LIMIT),
        name="dense_proj",
    )(x, w)


def _matmul_t_kernel(x_ref, wt_ref, o_ref):
    o_ref[...] = lax.dot_general(wt_ref[...], x_ref[...].astype(BF16), (((1,), (1,)), ((), ())),
                                 preferred_element_type=F32)


def _matmul_t(x, w_t, tm):
    B, S, K = x.shape
    N = w_t.shape[0]
    return pl.pallas_call(
        _matmul_t_kernel,
        grid=(B, S // tm),
        in_specs=[pl.BlockSpec((None, tm, K), lambda b, s: (b, s, 0)),
                  pl.BlockSpec((N, K), lambda b, s: (0, 0))],
        out_specs=pl.BlockSpec((None, N, tm), lambda b, s: (b, 0, s)),
        out_shape=jax.ShapeDtypeStruct((B, N, S), F32),
        compiler_params=pltpu.CompilerParams(
            dimension_semantics=("parallel", "parallel"), vmem_limit_bytes=VMEM_LIMIT),
        name="dense_proj_t",
    )(x, w_t)


def _rwkv_kernel(p_ref, mu_ref, w0_ref, w2_ref, a0_ref, a2_ref, g2_ref, kk_ref, ka_ref, rk_ref,
                 lnw_ref, lnb_ref, o_ref, carry_ref, state_ref):
    C, H, N = RWKV_CHUNK, RWKV_HEADS, HEAD_DIM
    W = RWKV_WIDTH
    B = p_ref.shape[0]
    NC = p_ref.shape[1] // C
    L = NC * C
    R = B * L

    @pl.when(pl.program_id(0) == 0)
    def _():
        carry_ref[...] = jnp.zeros_like(carry_ref)
        state_ref[...] = jnp.zeros_like(state_ref)

    def per_block(x, rows):
        return jnp.concatenate(
            [jnp.broadcast_to(x[i].reshape(1, -1), (rows, x.shape[-1])) for i in range(x.shape[0])], axis=0)

    p = p_ref[...].reshape(R, p_ref.shape[-1])
    row = lax.broadcasted_iota(jnp.int32, p.shape, 0)
    prev = jnp.where(row % L == 0, per_block(carry_ref[...], L), pltpu.roll(p, 1, axis=0))
    for b in range(B):
        carry_ref[b] = p[b * L + L - 1:b * L + L, :]
    xs = p + (prev - p) * mu_ref[...]
    r = xs[:, 0:W]
    k = xs[:, W:2 * W]
    v = xs[:, 2 * W:3 * W]
    wl = xs[:, 3 * W:3 * W + W_LORA]
    al = xs[:, 3 * W + W_LORA:3 * W + W_LORA + A_LORA]
    gl = xs[:, 3 * W + W_LORA + A_LORA:]

    z = -(w0_ref[...] + _dot3(jnp.tanh(wl), w2_ref[...]))
    softplus = jnp.maximum(z, 0.0) + jnp.log1p(jnp.exp(-jnp.abs(z)))
    logd = -jnp.exp(-softplus - 0.5)
    a = _sigmoid(a0_ref[...] + _dot3(al, a2_ref[...]))
    g = _dot3(_sigmoid(gl), g2_ref[...])

    kk = k * kk_ref[...]
    knew = k * (1.0 + (a - 1.0) * ka_ref[...])

    HG = RWKV_HEAD_GROUP
    GW = HG * N
    same_head_lanes = (lax.broadcasted_iota(jnp.int32, (GW, GW), 0) // N
                       == lax.broadcasted_iota(jnp.int32, (GW, GW), 1) // N)
    head_ones = jnp.where(same_head_lanes, 1.0, 0.0).astype(BF16)

    def head_sum(x):
        hi = x.astype(BF16)
        lo = (x - hi.astype(F32)).astype(BF16)
        return jnp.concatenate(
            [jnp.dot(hi[:, s:s + GW], head_ones, preferred_element_type=F32)
             + jnp.dot(lo[:, s:s + GW], head_ones, preferred_element_type=F32) for s in range(0, W, GW)],
            axis=-1)

    kk = kk / jnp.maximum(jnp.sqrt(head_sum(kk * kk)), 1e-12)
    lr_kk = kk * a

    ti = lax.broadcasted_iota(jnp.int32, (R, R), 0)
    tj = lax.broadcasted_iota(jnp.int32, (R, R), 1)
    same_chunk = (ti >= tj) & (ti // C == tj // C)
    tri = same_chunk.astype(BF16)
    cl = sum(jnp.dot(tri, piece, preferred_element_type=F32) for piece in reversed(_bf16_pieces(logd, 3)))
    cl_end = per_block(jnp.concatenate([cl[i * C + C - 1:i * C + C, :] for i in range(B * NC)], axis=0), C)
    a_hat = -kk * jnp.exp(cl - logd)
    r_hat = r * jnp.exp(cl)
    inv_gam = jnp.exp(-cl)
    b_til = lr_kk * inv_gam
    k_til = knew * inv_gam
    to_end = jnp.exp(cl_end - cl)
    b_end = lr_kk * to_end
    k_end = knew * to_end
    gam_end = jnp.exp(cl_end)

    gt = lax.broadcasted_iota(jnp.int32, (C, GW), 0)
    gc = lax.broadcasted_iota(jnp.int32, (C, GW), 1) % N
    strict = gt > gc
    incl = gt >= gc
    eye = (gt == gc).astype(F32)
    bi = lax.broadcasted_iota(jnp.int32, (HG * C, GW), 0) // C
    bj = lax.broadcasted_iota(jnp.int32, (HG * C, GW), 1) // N
    same_head = bi == bj

    def block_diag(y):
        yb = y.astype(BF16)
        return jnp.where(same_head, jnp.concatenate([yb] * HG, axis=0), jnp.zeros((), BF16))

    def bd_dot(x, y_bd):
        return jnp.dot(x.astype(BF16), y_bd, preferred_element_type=F32)

    def bd_dot_nt(x, y_bd):
        return lax.dot_general(x.astype(BF16), y_bd, (((1,), (1,)), ((), ())), preferred_element_type=F32)

    n_groups = H // HG
    units = [(b, c, gi) for b in range(B) for c in range(NC) for gi in range(n_groups)]
    n_units = range(len(units))
    cut = lambda x, b, c, gi: x[(b * NC + c) * C:(b * NC + c + 1) * C, gi * GW:(gi + 1) * GW]
    v_u = [cut(v, *un) for un in units]
    v_bd = [block_diag(v_u[i]) for i in n_units]
    ar = [jnp.concatenate([cut(a_hat, *un), cut(r_hat, *un)], axis=0) for un in units]
    mb = [bd_dot_nt(ar[i], block_diag(cut(b_til, *units[i]))) for i in n_units]
    mk = [bd_dot_nt(ar[i], block_diag(cut(k_til, *units[i]))) for i in n_units]
    n_ab = [jnp.where(strict, mb[i][:C], 0.0) for i in n_units]
    m_rb = [jnp.where(incl, mb[i][C:], 0.0) for i in n_units]
    l_ak = [jnp.where(strict, mk[i][:C], 0.0) for i in n_units]
    m_rk = [jnp.where(incl, mk[i][C:], 0.0) for i in n_units]

    pw = list(n_ab)
    pw_bd = [block_diag(pw[i]) for i in n_units]
    tinv = [eye + n_ab[i] for i in n_units]
    step = 2
    while step < C:
        pw = [bd_dot(pw[i], pw_bd[i]) for i in n_units]
        pw_bd = [block_diag(pw[i]) for i in n_units]
        tinv = [tinv[i] + bd_dot(tinv[i], pw_bd[i]) for i in n_units]
        step *= 2
    lv = [bd_dot(l_ak[i], v_bd[i]) for i in n_units]

    state = {(b, gi): state_ref[b * n_groups + gi] for b in range(B) for gi in range(n_groups)}
    outs = {}
    for c in range(NC):
        live = [i for i in n_units if units[i][1] == c]
        s0 = {i: state[(units[i][0], units[i][2])] for i in live}
        ars = {i: bd_dot_nt(ar[i], block_diag(s0[i])) for i in live}
        u = {i: bd_dot(tinv[i], block_diag(ars[i][:C] + lv[i])) for i in live}
        for i in live:
            outs[units[i]] = ars[i][C:] + bd_dot(m_rb[i], block_diag(u[i])) + bd_dot(m_rk[i], v_bd[i])
        for i in live:
            b, _, gi = units[i]
            uv = jnp.concatenate([u[i], v_u[i]], axis=0)
            bk_end = jnp.concatenate([cut(b_end, *units[i]), cut(k_end, *units[i])], axis=0)
            cross = jnp.where(same_head, _bdot_tn(uv, bk_end), 0.0)
            upd = cross[0:N]
            for h in range(1, HG):
                upd = upd + cross[h * N:(h + 1) * N]
            state[(b, gi)] = s0[i] * cut(gam_end, *units[i])[0:1] + upd
    for (b, gi), s_new in state.items():
        state_ref[b * n_groups + gi] = s_new

    o = jnp.concatenate([jnp.concatenate([outs[(b, c, gi)] for gi in range(n_groups)], axis=-1)
                         for b in range(B) for c in range(NC)], axis=0)
    mean = head_sum(o) * (1.0 / N)
    var = head_sum(jnp.square(o - mean)) * (1.0 / N)
    o = (o - mean) * lax.rsqrt(var + GN_EPS) * lnw_ref[...] + lnb_ref[...]
    bonus = head_sum(r * knew * rk_ref[...]) * v
    o_ref[...] = ((o + bonus) * g).reshape(o_ref.shape)


def _rwkv_time_mix(p_a, mu, w0, w2, a0, a2, g2, k_k, k_a, r_k, lnx_w, lnx_b):
    B, S, _ = p_a.shape
    L = RWKV_CHUNK * RWKV_STEP_CHUNKS
    row = lambda t: t.reshape(1, -1)
    full = lambda shape: pl.BlockSpec(shape, lambda s: (0,) * len(shape))
    n_units = B * RWKV_HEADS // RWKV_HEAD_GROUP
    return pl.pallas_call(
        _rwkv_kernel,
        grid=(S // L,),
        in_specs=[pl.BlockSpec((B, L, RWKV_IN_W), lambda s: (0, s, 0)),
                  full((1, RWKV_IN_W)), full((1, RWKV_WIDTH)), full((W_LORA, RWKV_WIDTH)),
                  full((1, RWKV_WIDTH)), full((A_LORA, RWKV_WIDTH)), full((G_LORA, RWKV_WIDTH)),
                  full((1, RWKV_WIDTH)), full((1, RWKV_WIDTH)), full((1, RWKV_WIDTH)),
                  full((1, RWKV_WIDTH)), full((1, RWKV_WIDTH))],
        out_specs=pl.BlockSpec((B, L, RWKV_WIDTH), lambda s: (0, s, 0)),
        out_shape=jax.ShapeDtypeStruct((B, S, RWKV_WIDTH), F32),
        scratch_shapes=[pltpu.VMEM((B, 1, RWKV_IN_W), F32),
                        pltpu.VMEM((n_units, HEAD_DIM, RWKV_HEAD_GROUP * HEAD_DIM), F32)],
        compiler_params=pltpu.CompilerParams(
            dimension_semantics=("arbitrary",), vmem_limit_bytes=VMEM_LIMIT),
        name="rwkv7_chunked",
    )(p_a, row(mu), row(w0), w2, row(a0), a2, g2, row(k_k), row(k_a), row(r_k), row(lnx_w), row(lnx_b))


NSA_KV_TILE = 1024
SEL_KEY_TILE = 1024
NSA_QUERY_TILE = 512
SEL_LANES = 128


def _rope_tables(pos, reps):
    half = HEAD_DIM // 2
    inv = ROPE_THETA ** (-jnp.arange(half, dtype=F32) / half)
    ang = pos.astype(F32)[:, None] * inv
    cos, sin = jnp.cos(ang), jnp.sin(ang)
    cosf = jnp.concatenate([cos, cos], -1)
    sinf = jnp.concatenate([-sin, sin], -1)
    return jnp.tile(cosf, (1, reps)), jnp.tile(sinf, (1, reps))


def _rope(x, cosf, sinf):
    width = x.shape[-1]
    lane = lax.broadcasted_iota(jnp.int32, x.shape, 1)
    first_half = (lane % HEAD_DIM) < HEAD_DIM // 2
    rot = jnp.where(first_half, pltpu.roll(x, width - HEAD_DIM // 2, axis=1),
                    pltpu.roll(x, HEAD_DIM // 2, axis=1))
    return x * cosf + rot * sinf


def _kv_layout_kernel(p_ref, cos_ref, sin_ref, kc_ref, vc_ref, ks_ref, vs_ref, kw_ref, vw_ref):
    ts = p_ref.shape[0]
    for i, o_ref in ((0, kc_ref), (1, vc_ref), (2, ks_ref), (4, kw_ref)):
        t = p_ref[:, i * NSA_KV_WIDTH:(i + 1) * NSA_KV_WIDTH]
        if i >= 2:
            t = _rope(t, cos_ref[...], sin_ref[...])
        for g in range(NSA_GROUPS):
            o_ref[g] = t[:, g * HEAD_DIM:(g + 1) * HEAD_DIM].astype(o_ref.dtype)
    pad_row = lax.broadcasted_iota(jnp.int32, (VT_ROWS - HEAD_DIM, ts), 0)
    ones_row = jnp.where(pad_row == 0, 1.0, 0.0)
    for i, o_ref in ((3, vs_ref), (5, vw_ref)):
        t_t = p_ref[:, i * NSA_KV_WIDTH:(i + 1) * NSA_KV_WIDTH].T
        for g in range(NSA_GROUPS):
            o_ref[g] = jnp.concatenate([t_t[g * HEAD_DIM:(g + 1) * HEAD_DIM], ones_row],
                                       axis=0).astype(o_ref.dtype)


def _kv_layout(p_b, cos2, sin2):
    B, S, _ = p_b.shape
    ts = min(NSA_KV_TILE, S)
    out_spec = pl.BlockSpec((None, NSA_GROUPS, ts, HEAD_DIM), lambda b, s: (b, 0, s, 0))
    vt_spec = pl.BlockSpec((None, NSA_GROUPS, VT_ROWS, ts), lambda b, s: (b, 0, 0, s))
    shp = lambda dt: jax.ShapeDtypeStruct((B, NSA_GROUPS, S, HEAD_DIM), dt)
    vt_shp = jax.ShapeDtypeStruct((B, NSA_GROUPS, VT_ROWS, S), BF16)
    return pl.pallas_call(
        _kv_layout_kernel,
        grid=(B, S // ts),
        in_specs=[pl.BlockSpec((None, ts, 6 * NSA_KV_WIDTH), lambda b, s: (b, s, 0)),
                  pl.BlockSpec((ts, NSA_KV_WIDTH), lambda b, s: (s, 0)),
                  pl.BlockSpec((ts, NSA_KV_WIDTH), lambda b, s: (s, 0))],
        out_specs=[out_spec, out_spec, out_spec, vt_spec, out_spec, vt_spec],
        out_shape=[shp(F32), shp(F32), shp(BF16), vt_shp, shp(BF16), vt_shp],
        compiler_params=pltpu.CompilerParams(
            dimension_semantics=("parallel", "parallel"), vmem_limit_bytes=VMEM_LIMIT),
        name="nsa_kv_layout",
    )(p_b, cos2, sin2)


def _compress_kernel(subk_ref, subv_ref, pek_ref, w1k_ref, w2k_ref, pev_ref, w1v_ref, w2v_ref,
                     cos_ref, sin_ref, kc_ref, vc_ref):
    n_sub = subk_ref.shape[0]
    half = CMP_STRIDE * HEAD_DIM

    def mlp(sub_ref, pe_ref, w1_ref, w2_ref):
        sub = sub_ref[...]
        top = _bdot(sub, w1_ref[:half, :])
        bot = _bdot(sub, w1_ref[half:, :])
        bias = _bdot(jnp.broadcast_to(pe_ref[...], (8, 2 * half)), w1_ref[...])[0:1, :]
        h = top + pltpu.roll(bot, n_sub - 1, axis=0) + bias
        return _bdot(jax.nn.gelu(h), w2_ref[...])

    kc = mlp(subk_ref, pek_ref, w1k_ref, w2k_ref)
    rot = jnp.concatenate([kc[:, HEAD_DIM // 2:], kc[:, :HEAD_DIM // 2]], axis=-1)
    kc_ref[...] = (kc * cos_ref[...] + rot * sin_ref[...]).astype(kc_ref.dtype)
    vc_ref[...] = mlp(subv_ref, pev_ref, w1v_ref, w2v_ref).astype(vc_ref.dtype)


def _compress(subk, subv, pe_k, w1_k, w2_k, pe_v, w1_v, w2_v, cos_c, sin_c):
    B, G, n_sub, width = subk.shape
    sub_spec = pl.BlockSpec((None, None, n_sub, width), lambda b, g: (b, g, 0, 0))
    full = lambda a: pl.BlockSpec(a.shape, lambda b, g: (0,) * a.ndim)
    out_spec = pl.BlockSpec((None, None, n_sub, HEAD_DIM), lambda b, g: (b, g, 0, 0))
    pe_k, pe_v = pe_k.reshape(1, -1), pe_v.reshape(1, -1)
    args = (pe_k, w1_k, w2_k, pe_v, w1_v, w2_v, cos_c, sin_c)
    return pl.pallas_call(
        _compress_kernel,
        grid=(B, G),
        in_specs=[sub_spec, sub_spec] + [full(a) for a in args],
        out_specs=[out_spec, out_spec],
        out_shape=[jax.ShapeDtypeStruct((B, G, n_sub, HEAD_DIM), BF16)] * 2,
        compiler_params=pltpu.CompilerParams(
            dimension_semantics=("parallel", "parallel"), vmem_limit_bytes=VMEM_LIMIT),
        name="nsa_compress",
    )(subk, subv, *args)


MAX_FLOOR = -1e20
MASK_BIG = 2.0 ** 100
LOG2_E = 1.4426950408889634
VT_ROWS = 80


def _nsa_kernel(q_ref, gate_ref, cos_ref, sin_ref, kc_ref, vc_ref, ks_ref, vst_ref, kw_ref, vwt_ref,
                mselt_ref, o_ref, blockbias_ref, *, n_pick):
    QB, HP, D = NSA_QUERY_TILE, NSA_HPG, HEAD_DIM
    qb = pl.program_id(2)
    n_cmp = kc_ref.shape[0]
    lanes4 = lambda x: jnp.concatenate([x] * HP, axis=1)

    heads = []
    for n in range(HP):
        qh = q_ref[n * D:(n + 1) * D, :]
        rot = jnp.concatenate([qh[D // 2:], qh[:D // 2]], axis=0)
        heads.append(qh * cos_ref[...] + rot * sin_ref[...])
    q4 = (jnp.concatenate(heads, axis=1) * (D ** -0.5 * LOG2_E)).astype(BF16)
    t_row = qb * QB + lax.broadcasted_iota(jnp.int32, (1, QB), 1)

    def softmax_cols(s_t, bias_t):
        sm = s_t + lanes4(bias_t)
        m = jnp.maximum(jnp.max(sm, axis=0, keepdims=True), MAX_FLOOR)
        return jnp.exp2(sm - m)

    cmp_end = lax.broadcasted_iota(jnp.int32, (n_cmp, 1), 0) * CMP_STRIDE + (CMP_BLOCK - 1)
    e_c = softmax_cols(jnp.dot(kc_ref[...], q4, preferred_element_type=F32),
                       jnp.where(cmp_end <= t_row, 0.0, -MASK_BIG))
    den_c = jnp.sum(e_c, axis=0, keepdims=True)
    p_c = e_c * (1.0 / jnp.where(den_c > 0.0, den_c, 1.0))
    o_c = _bdot_tn(vc_ref[...], p_c)
    p_sum = p_c[:, 0:QB]
    for n in range(1, HP):
        p_sum = p_sum + p_c[:, n * QB:(n + 1) * QB]
    p_hi = p_sum.astype(BF16)
    p_lo = (p_sum - p_hi.astype(F32)).astype(BF16)
    imp_t = (jnp.dot(mselt_ref[...], p_hi, preferred_element_type=F32)
             + jnp.dot(mselt_ref[...], p_lo, preferred_element_type=F32))

    j = lax.broadcasted_iota(jnp.int32, (SEL_LANES, QB), 0)
    cur = t_row // SEL_BLOCK
    valid = j * SEL_BLOCK <= t_row
    forced = (j == 0) | (j == cur) | (j == cur - 1)
    score = jnp.where(valid, imp_t + jnp.where(forced, FORCE_BONUS, 0.0), -1.0)
    for _ in range(n_pick):
        m = jnp.max(score, axis=0, keepdims=True)
        idx = jnp.min(jnp.where(score == m, j, SEL_LANES), axis=0, keepdims=True)
        score = jnp.where(j == idx, -2.0, score)
    blockbias_ref[...] = jnp.where((score == -2.0) & valid, 0.0, -MASK_BIG)

    KT = SEL_KEY_TILE
    blocks_per_tile = KT // SEL_BLOCK
    n_tiles = (qb * QB + QB + KT - 1) // KT

    def sel_step(kt, carry, causal):
        m_i, acc = carry
        start = pl.multiple_of(kt * KT, KT)
        s_t = jnp.dot(ks_ref[pl.ds(start, KT), :], q4, preferred_element_type=F32)
        bias = jnp.concatenate(
            [jnp.broadcast_to(blockbias_ref[pl.ds(kt * blocks_per_tile + jb, 1), :], (SEL_BLOCK, QB))
             for jb in range(blocks_per_tile)], axis=0)
        if causal:
            kpos = start + lax.broadcasted_iota(jnp.int32, (KT, 1), 0)
            bias = jnp.where(kpos <= t_row, bias, -MASK_BIG)
        sm = s_t + lanes4(bias)
        m_new = jnp.maximum(m_i, jnp.max(sm, axis=0, keepdims=True))
        e = jnp.exp2(sm - m_new).astype(BF16)
        acc_new = jnp.exp2(m_i - m_new) * acc + jnp.dot(vst_ref[:, pl.ds(start, KT)], e,
                                                        preferred_element_type=F32)
        return m_new, acc_new

    init = (jnp.full((1, HP * QB), MAX_FLOOR, F32), jnp.zeros((VT_ROWS, HP * QB), F32))
    carry = lax.fori_loop(0, n_tiles - 1, lambda kt, c: sel_step(kt, c, False), init)
    _, acc_s = sel_step(n_tiles - 1, carry, True)
    den_s = acc_s[D:D + 1]
    o_s = acc_s[:D] * (1.0 / jnp.where(den_s > 0.0, den_s, 1.0))

    span = WINDOW + QB
    w_start = pl.multiple_of(jnp.maximum(qb * QB - WINDOW, 0), QB)
    dist = t_row - (w_start + lax.broadcasted_iota(jnp.int32, (span, 1), 0))
    e_w = softmax_cols(jnp.dot(kw_ref[pl.ds(w_start, span), :], q4, preferred_element_type=F32),
                       jnp.where((dist >= 0) & (dist < WINDOW), 0.0, -MASK_BIG))
    acc_w = jnp.dot(vwt_ref[:, pl.ds(w_start, span)], e_w.astype(BF16), preferred_element_type=F32)
    den_w = acc_w[D:D + 1]
    o_w = acc_w[:D] * (1.0 / jnp.where(den_w > 0.0, den_w, 1.0))

    gates = _sigmoid(gate_ref[...])
    gate_row = lambda br: jnp.concatenate([gates[3 * n + br:3 * n + br + 1, :] for n in range(HP)], axis=1)
    o_t = gate_row(0) * o_c + gate_row(1) * o_s + gate_row(2) * o_w
    for n in range(HP):
        o_ref[:, n * D:(n + 1) * D] = o_t[:, n * QB:(n + 1) * QB].T


def _cmp_to_sel_matrix(n_cmp_rows, n_sel):
    ratio = SEL_BLOCK // CMP_STRIDE
    ci = np.arange(n_cmp_rows)[:, None]
    sj = np.arange(SEL_LANES)[None, :]
    m = sum(((ci + n) // ratio == sj).astype(np.float32) for n in range(CMP_BLOCK // CMP_STRIDE))
    m = m * (sj < n_sel) * (ci < n_cmp_rows - 1)
    return jnp.asarray(m.T, BF16)


def _nsa_attention(qg_t, kc, vc, ks, vst, kw, vwt, cos_t, sin_t):
    B, _, S = qg_t.shape
    n_sub = kc.shape[2]
    n_sel = S // SEL_BLOCK
    gw = NSA_HPG * HEAD_DIM
    gate_row0 = NSA_WIDTH // 128
    msel_t = _cmp_to_sel_matrix(n_sub, n_sel)
    at_bg = lambda shape: pl.BlockSpec((None, None) + shape, lambda b, g, i: (b, g, 0, 0))
    const = lambda a: pl.BlockSpec(a.shape, lambda b, g, i: (0, 0))
    return pl.pallas_call(
        functools.partial(_nsa_kernel, n_pick=min(N_SELECT, n_sel)),
        grid=(B, NSA_GROUPS, S // NSA_QUERY_TILE),
        in_specs=[pl.BlockSpec((None, gw, NSA_QUERY_TILE), lambda b, g, i: (b, g, i)),
                  pl.BlockSpec((None, 128, NSA_QUERY_TILE), lambda b, g, i: (b, gate_row0 + g, i)),
                  pl.BlockSpec((HEAD_DIM, NSA_QUERY_TILE), lambda b, g, i: (0, i)),
                  pl.BlockSpec((HEAD_DIM, NSA_QUERY_TILE), lambda b, g, i: (0, i)),
                  at_bg((n_sub, HEAD_DIM)), at_bg((n_sub, HEAD_DIM)),
                  at_bg((S, HEAD_DIM)), at_bg((VT_ROWS, S)), at_bg((S, HEAD_DIM)), at_bg((VT_ROWS, S)),
                  const(msel_t)],
        out_specs=pl.BlockSpec((None, NSA_QUERY_TILE, gw), lambda b, g, i: (b, i, g)),
        out_shape=jax.ShapeDtypeStruct((B, S, NSA_WIDTH), F32),
        scratch_shapes=[pltpu.VMEM((SEL_LANES, NSA_QUERY_TILE), F32)],
        compiler_params=pltpu.CompilerParams(
            dimension_semantics=("parallel", "parallel", "arbitrary"), vmem_limit_bytes=VMEM_LIMIT),
        name="nsa_attention",
    )(qg_t, qg_t, cos_t, sin_t, kc, vc, ks, vst, kw, vwt, msel_t)


def _nsa_branch(p_kv, qg_t, cmp_pe_k, cmp_w1_k, cmp_w2_k, cmp_pe_v, cmp_w1_v, cmp_w2_v):
    B, S, _ = p_kv.shape
    pos = jnp.arange(S)
    cos2, sin2 = _rope_tables(pos, NSA_GROUPS)
    kc_raw, vc_raw, ks, vst, kw, vwt = _kv_layout(p_kv, cos2, sin2)
    n_sub = S // CMP_STRIDE
    sub = lambda t: t.reshape(B, NSA_GROUPS, n_sub, CMP_STRIDE * HEAD_DIM)
    cos_c, sin_c = _rope_tables(jnp.arange(n_sub) * CMP_STRIDE + CMP_BLOCK - 1, 1)
    kc, vc = _compress(sub(kc_raw), sub(vc_raw), cmp_pe_k, cmp_w1_k, cmp_w2_k,
                       cmp_pe_v, cmp_w1_v, cmp_w2_v, cos_c, sin_c)
    cos_q, sin_q = _rope_tables(pos, 1)
    return _nsa_attention(qg_t, kc, vc, ks, vst, kw, vwt, cos_q.T, sin_q.T)


def _nsa_weight_columns(w_nsa):
    K = w_nsa.shape[0]
    q = w_nsa[:, :NSA_WIDTH]
    kv = w_nsa[:, NSA_WIDTH:NSA_WIDTH + 6 * NSA_KV_WIDTH]
    gates = w_nsa[:, NSA_WIDTH + 6 * NSA_KV_WIDTH:]
    per_group = NSA_HPG * 3
    gate_blocks = [jnp.pad(gates[:, g * per_group:(g + 1) * per_group], ((0, 0), (0, 128 - per_group)))
                   for g in range(NSA_GROUPS)]
    return jnp.concatenate([kv, q] + gate_blocks, axis=1)


def _layer_norm(h, g, b):
    mu = jnp.mean(h, axis=-1, keepdims=True)
    var = jnp.mean(jnp.square(h - mu), axis=-1, keepdims=True)
    return (h - mu) * lax.rsqrt(var + LN_EPS) * g + b


def _pack_bf16_halves(x):
    n = x.shape[-1] // 2
    bits = lax.bitcast_convert_type(x.astype(BF16).astype(F32), jnp.uint32)
    return (bits[:, n:] & jnp.uint32(0xFFFF0000)) | (bits[:, :n] >> 16)


def _unpack_bf16_halves(u):
    left = lax.bitcast_convert_type(u << 16, F32)
    right = lax.bitcast_convert_type(u & jnp.uint32(0xFFFF0000), F32)
    return left, right


def _mixer_out_kernel(x_ref, ya_ref, yb_ref, pg_ref, wa_ref, wb_ref, wo_ref, g_ref, b_ref, o_ref, op_ref,
                      *, alpha):
    d = x_ref.shape[-1]
    gate_a = _sigmoid(pg_ref[:, :d].astype(F32))
    gate_b = _sigmoid(pg_ref[:, d:].astype(F32))
    mixed = gate_a * _bdot(ya_ref[...], wa_ref[...]) + gate_b * _bdot(yb_ref[...], wb_ref[...])
    h = alpha * x_ref[...] + _bdot(mixed, wo_ref[...])
    out = _layer_norm(h, g_ref[...], b_ref[...])
    o_ref[...] = out
    op_ref[...] = _pack_bf16_halves(out)


def _mixer_out(xf, ya, yb, p_g, wa, wb, wo, ln_g, ln_b, alpha, tm=512):
    T, D = xf.shape
    rows = lambda w: pl.BlockSpec((tm, w), lambda i: (i, 0))
    full = lambda a: pl.BlockSpec(a.shape, lambda i: (0,) * a.ndim)
    ln_g, ln_b = ln_g.reshape(1, D), ln_b.reshape(1, D)
    return pl.pallas_call(
        functools.partial(_mixer_out_kernel, alpha=alpha),
        grid=(T // tm,),
        in_specs=[rows(D), rows(ya.shape[1]), rows(yb.shape[1]), rows(2 * D),
                  full(wa), full(wb), full(wo), full(ln_g), full(ln_b)],
        out_specs=[rows(D), rows(D // 2)],
        out_shape=[jax.ShapeDtypeStruct((T, D), F32), jax.ShapeDtypeStruct((T, D // 2), jnp.uint32)],
        compiler_params=pltpu.CompilerParams(
            dimension_semantics=("parallel",), vmem_limit_bytes=VMEM_LIMIT),
        name="mixer_out_ln",
    )(xf, ya, yb, p_g, wa, wb, wo, ln_g, ln_b)


ROUTER_TILE = 256
EXPERT_ROWS = 512
MOE_COMBINE_PARTS = 2
SC_TOKEN_CHUNK = 64
SC_ROW_CHUNK = 64
PICK_LANES = 128
LOWEST = -3.0e38


def _router_kernel(x_ref, rwt_ref, bias_ref, eidx_ref, wts_ref, pos_ref, cnt_ref, carry_ref):
    tm, E = x_ref.shape[0], rwt_ref.shape[0]
    per_group = E // N_GROUPS
    reps = tm // PICK_LANES

    @pl.when(pl.program_id(0) == 0)
    def _():
        carry_ref[...] = jnp.zeros_like(carry_ref)

    scores = _sigmoid(_dot3(rwt_ref[...], x_ref[...], (((1,), (1,)), ((), ()))))
    choice = scores + jnp.concatenate([bias_ref[...]] * reps, axis=1)
    row = lax.broadcasted_iota(jnp.int32, (E, tm), 0)

    def first_max(vals, rows):
        m = jnp.max(vals, axis=0, keepdims=True)
        return m, jnp.min(jnp.where(vals == m, rows, E), axis=0, keepdims=True)

    group_score = []
    for g in range(N_GROUPS):
        rows = slice(g * per_group, (g + 1) * per_group)
        group_row = g * per_group + lax.broadcasted_iota(jnp.int32, (per_group, tm), 0)
        m1, i1 = first_max(choice[rows], group_row)
        m2 = jnp.max(jnp.where(group_row == i1, LOWEST, choice[rows]), axis=0, keepdims=True)
        group_score.append(m1 + m2)
    masked = []
    for g in range(N_GROUPS):
        rank = jnp.zeros((1, tm), jnp.int32)
        for o in range(N_GROUPS):
            if o != g:
                ahead = (group_score[o] > group_score[g]) if o > g else (group_score[o] >= group_score[g])
                rank = rank + ahead.astype(jnp.int32)
        masked.append(jnp.where(rank < TOPK_GROUPS, choice[g * per_group:(g + 1) * per_group], NEG_INF))

    cur = jnp.concatenate(masked, axis=0)
    picks = []
    for _ in range(TOP_K):
        _, idx = first_max(cur, row)
        picks.append(idx)
        cur = jnp.where(row == idx, LOWEST, cur)
    sel = jnp.where(cur == LOWEST, 1.0, 0.0)
    gate = scores * sel
    gate = gate * (ROUTED_SCALE / jnp.sum(gate, axis=0, keepdims=True))

    ti = lax.broadcasted_iota(jnp.int32, (tm, tm), 0)
    tj = lax.broadcasted_iota(jnp.int32, (tm, tm), 1)
    sel_b = sel.astype(BF16)
    before = jnp.dot(sel_b, (ti < tj).astype(BF16), preferred_element_type=F32)
    queue_pos = before + jnp.concatenate([carry_ref[...]] * reps, axis=1)
    carry_ref[...] = carry_ref[...] + jnp.dot(sel_b, jnp.ones((tm, PICK_LANES), BF16),
                                              preferred_element_type=F32)
    cnt_ref[...] = carry_ref[...]

    at_pick = lambda vals, idx: jnp.sum(jnp.where(row == idx, vals, 0.0), axis=0, keepdims=True)
    eidx_ref[...] = jnp.concatenate(picks, axis=0)
    wts_ref[...] = jnp.concatenate([at_pick(gate, idx) for idx in picks], axis=0)
    pos_ref[...] = jnp.concatenate([at_pick(queue_pos, idx) for idx in picks], axis=0).astype(jnp.int32)


def _router(xf, router_w, router_bias):
    T, D = xf.shape
    E = router_w.shape[1]
    tm = ROUTER_TILE
    picks = lambda dt: jax.ShapeDtypeStruct((TOP_K, T), dt)
    pick_spec = pl.BlockSpec((TOP_K, tm), lambda i: (0, i))
    lanes = lambda v: jnp.broadcast_to(v.reshape(E, 1), (E, PICK_LANES))
    return pl.pallas_call(
        _router_kernel,
        grid=(T // tm,),
        in_specs=[pl.BlockSpec((tm, D), lambda i: (i, 0)),
                  pl.BlockSpec((E, D), lambda i: (0, 0)),
                  pl.BlockSpec((E, PICK_LANES), lambda i: (0, 0))],
        out_specs=[pick_spec, pick_spec, pick_spec, pl.BlockSpec((E, PICK_LANES), lambda i: (0, 0))],
        out_shape=[picks(jnp.int32), picks(F32), picks(jnp.int32),
                   jax.ShapeDtypeStruct((E, PICK_LANES), F32)],
        scratch_shapes=[pltpu.VMEM((E, PICK_LANES), F32)],
        compiler_params=pltpu.CompilerParams(
            dimension_semantics=("arbitrary",), vmem_limit_bytes=VMEM_LIMIT),
        name="moe_router",
    )(xf, router_w.T, lanes(router_bias))


def _dest_kernel(eidx_ref, pos_ref, start_ref, dest_ref):
    E = start_ref.shape[0]
    tm = eidx_ref.shape[1]
    row = lax.broadcasted_iota(jnp.int32, (E, tm), 0)
    start = jnp.concatenate([start_ref[...]] * (tm // PICK_LANES), axis=1)
    eidx = eidx_ref[...]
    base = [jnp.sum(jnp.where(row == eidx[kk:kk + 1, :], start, 0), axis=0, keepdims=True)
            for kk in range(TOP_K)]
    dest_ref[...] = jnp.concatenate(base, axis=0) + pos_ref[...]


def _dest_rows(eidx_t, pos_t, pad_start):
    T = eidx_t.shape[1]
    E = pad_start.shape[0]
    tm = ROUTER_TILE
    pick_spec = pl.BlockSpec((TOP_K, tm), lambda i: (0, i))
    return pl.pallas_call(
        _dest_kernel,
        grid=(T // tm,),
        in_specs=[pick_spec, pick_spec, pl.BlockSpec((E, PICK_LANES), lambda i: (0, 0))],
        out_specs=pick_spec,
        out_shape=jax.ShapeDtypeStruct((TOP_K, T), jnp.int32),
        compiler_params=pltpu.CompilerParams(
            dimension_semantics=("parallel",), vmem_limit_bytes=VMEM_LIMIT),
        name="moe_dest_rows",
    )(eidx_t, pos_t, jnp.broadcast_to(pad_start.reshape(E, 1), (E, PICK_LANES)))


def _sc_mesh():
    return plsc.VectorSubcoreMesh(core_axis_name="c", subcore_axis_name="s")


def _sc_scatter_rows(x, dest_t, n_rows):
    T, D = x.shape
    K = dest_t.shape[0]
    mesh = _sc_mesh()
    nc, nw = mesh.num_cores, mesh.num_cores * mesh.num_subcores
    per_w = T // nw
    chunk = min(SC_TOKEN_CHUNK, per_w)
    n_chunks = per_w // chunk
    idx = dest_t.reshape(K, nw, n_chunks, chunk).transpose(1, 2, 0, 3).reshape(nw, n_chunks * K, chunk)

    assert n_chunks % 2 == 0

    @functools.partial(
        pl.kernel, mesh=mesh,
        out_type=jax.ShapeDtypeStruct((n_rows, D), x.dtype),
        scratch_types=[pltpu.VMEM((n_chunks * K, chunk), jnp.int32),
                       pltpu.VMEM((2, chunk, D), x.dtype),
                       pltpu.SemaphoreType.DMA((2,)), pltpu.SemaphoreType.DMA((2,))],
    )
    def scatter(x_hbm, idx_hbm, out_hbm, idx_v, rows_v, load_sem, send_sem):
        wid = lax.axis_index("s") * nc + lax.axis_index("c")
        pltpu.sync_copy(idx_hbm.at[wid], idx_v)

        def load(j, b):
            return pltpu.make_async_copy(x_hbm.at[pl.ds(wid * per_w + j * chunk, chunk)], rows_v.at[b],
                                         load_sem.at[b])

        def sends(j, b):
            return [pltpu.make_async_copy(rows_v.at[b], out_hbm.at[idx_v.at[j * K + kk]], send_sem.at[b])
                    for kk in range(K)]

        load(0, 0).start()

        @pl.loop(0, n_chunks, step=2)
        def _(j0):
            for b in range(2):
                j = j0 + b
                load(j, b).wait()

                @pl.when(j >= 1)
                def _():
                    for c in sends(j - 1, 1 - b):
                        c.wait()

                @pl.when(j + 1 < n_chunks)
                def _():
                    load(j + 1, 1 - b).start()

                for c in sends(j, b):
                    c.start()

        for c in sends(n_chunks - 1, (n_chunks - 1) % 2):
            c.wait()

    return scatter(x, idx)


def _sc_gather_rows(src, idx):
    M = idx.shape[0]
    D = src.shape[1]
    mesh = _sc_mesh()
    nc, nw = mesh.num_cores, mesh.num_cores * mesh.num_subcores
    per_w = M // nw
    chunk = min(SC_ROW_CHUNK, per_w)
    n_chunks = per_w // chunk

    assert n_chunks % 2 == 0

    @functools.partial(
        pl.kernel, mesh=mesh,
        out_type=jax.ShapeDtypeStruct((M, D), src.dtype),
        scratch_types=[pltpu.VMEM((n_chunks, chunk), jnp.int32),
                       pltpu.VMEM((2, chunk, D), src.dtype),
                       pltpu.SemaphoreType.DMA((2,)), pltpu.SemaphoreType.DMA((2,))],
    )
    def gather(src_hbm, idx_hbm, out_hbm, idx_v, rows_v, fetch_sem, store_sem):
        wid = lax.axis_index("s") * nc + lax.axis_index("c")
        pltpu.sync_copy(idx_hbm.at[wid], idx_v)

        def fetch(j, b):
            return pltpu.make_async_copy(src_hbm.at[idx_v.at[j]], rows_v.at[b], fetch_sem.at[b])

        def store(j, b):
            return pltpu.make_async_copy(rows_v.at[b], out_hbm.at[pl.ds(wid * per_w + j * chunk, chunk)],
                                         store_sem.at[b])

        fetch(0, 0).start()

        @pl.loop(0, n_chunks, step=2)
        def _(j0):
            for b in range(2):
                j = j0 + b
                fetch(j, b).wait()

                @pl.when(j >= 1)
                def _():
                    store(j - 1, 1 - b).wait()

                @pl.when(j + 1 < n_chunks)
                def _():
                    fetch(j + 1, 1 - b).start()

                store(j, b).start()

        store(n_chunks - 1, (n_chunks - 1) % 2).wait()

    return gather(src, idx.reshape(nw, n_chunks, chunk))


def _expert_kernel(distinct_e_ref, blk_ord_ref, blk_new_ref, blk_rows_ref, n_used_ref, n_distinct_ref,
                   x_ref, wgu_hbm, wd_hbm, o_ref, wgu_buf, wd_buf, wgu_bf, wd_bf, sem):
    i = pl.program_id(0)
    live = i < n_used_ref[0]
    ordinal = blk_ord_ref[i]
    slot = ordinal % 2

    def weight_copies(k, s):
        e = distinct_e_ref[k]
        return (pltpu.make_async_copy(wgu_hbm.at[e], wgu_buf.at[s], sem.at[0, s]),
                pltpu.make_async_copy(wd_hbm.at[e], wd_buf.at[s], sem.at[1, s]))

    @pl.when(i == 0)
    def _():
        for c in weight_copies(0, 0):
            c.start()

    @pl.when(live & (blk_new_ref[i] == 1))
    def _():
        for c in weight_copies(ordinal, slot):
            c.wait()

        @pl.when(ordinal + 1 < n_distinct_ref[0])
        def _():
            for c in weight_copies(ordinal + 1, 1 - slot):
                c.start()

        wgu_bf[...] = wgu_buf[slot].astype(BF16)
        wd_bf[...] = wd_buf[slot].astype(BF16)

    @pl.when(live)
    def _():
        hidden = wd_bf.shape[0]
        half = x_ref.shape[1]
        row = lax.broadcasted_iota(jnp.int32, x_ref.shape, 0)
        left, right = _unpack_bf16_halves(x_ref[...])
        real = row < blk_rows_ref[i]
        left = jnp.where(real, left, 0.0).astype(BF16)
        right = jnp.where(real, right, 0.0).astype(BF16)
        h = (jnp.dot(left, wgu_bf[:half, :], preferred_element_type=F32)
             + jnp.dot(right, wgu_bf[half:, :], preferred_element_type=F32))
        gate, up = h[:, :hidden], h[:, hidden:]
        act = (gate * _sigmoid(gate) * up).astype(BF16)
        o_ref[...] = _pack_bf16_halves(jnp.dot(act, wd_bf[...], preferred_element_type=F32))

    @pl.when(jnp.logical_not(live))
    def _():
        o_ref[...] = jnp.zeros_like(o_ref)


def _expert_ffn(xs, blk_e, blk_rows, n_used, w_gu, w_down):
    n_rows, half = xs.shape
    E, D, two_h = w_gu.shape
    n_blocks = n_rows // EXPERT_ROWS
    idx = jnp.arange(n_blocks, dtype=jnp.int32)
    is_live = idx < n_used[0]
    blk_new = (is_live & ((idx == 0) | (blk_e != jnp.roll(blk_e, 1)))).astype(jnp.int32)
    blk_ord = (jnp.cumsum(blk_new) - 1).astype(jnp.int32)
    n_distinct = blk_ord[-1:] + 1
    first_of = (blk_new[None, :] == 1) & (blk_ord[None, :] == idx[:, None])
    distinct_e = jnp.sum(jnp.where(first_of, blk_e[None, :], 0), axis=1).astype(jnp.int32)

    live = lambda i, nu: jnp.minimum(i, nu[0] - 1)
    grid_spec = pltpu.PrefetchScalarGridSpec(
        num_scalar_prefetch=6,
        grid=(n_blocks,),
        in_specs=[pl.BlockSpec((EXPERT_ROWS, half), lambda i, de, bo, bn, br, nu, nd: (live(i, nu), 0)),
                  pl.BlockSpec(memory_space=pl.ANY), pl.BlockSpec(memory_space=pl.ANY)],
        out_specs=pl.BlockSpec((EXPERT_ROWS, half), lambda i, de, bo, bn, br, nu, nd: (i, 0)),
        scratch_shapes=[pltpu.VMEM((2, D, two_h), F32), pltpu.VMEM((2, two_h // 2, D), F32),
                        pltpu.VMEM((D, two_h), BF16), pltpu.VMEM((two_h // 2, D), BF16),
                        pltpu.SemaphoreType.DMA((2, 2))],
    )
    return pl.pallas_call(
        _expert_kernel,
        grid_spec=grid_spec,
        out_shape=jax.ShapeDtypeStruct((n_rows, half), jnp.uint32),
        compiler_params=pltpu.CompilerParams(
            dimension_semantics=("arbitrary",), vmem_limit_bytes=VMEM_LIMIT),
        name="moe_experts",
    )(distinct_e, blk_ord, blk_new, blk_rows, n_used, n_distinct, xs, w_gu, w_down)


def _moe_out_kernel(x_ref, yk_ref, wts_ref, sgu_ref, sd_ref, g_ref, b_ref, *rest, alpha, has_prev):
    o_ref = rest[1] if has_prev else rest[0]
    x = x_ref[...]
    hidden = sd_ref.shape[0]
    h = _bdot(x, sgu_ref[...])
    gate, up = h[:, :hidden], h[:, hidden:]
    ffn = _bdot(gate * _sigmoid(gate) * up, sd_ref[...])
    wts = wts_ref[...]
    routed_left = routed_right = None
    for kk in range(TOP_K):
        left, right = _unpack_bf16_halves(yk_ref[kk])
        w = wts[:, kk:kk + 1]
        routed_left = w * left if kk == 0 else routed_left + w * left
        routed_right = w * right if kk == 0 else routed_right + w * right
    ffn = ffn + jnp.concatenate([routed_left, routed_right], axis=-1)
    o_ref[...] = _layer_norm(alpha * x + ffn, g_ref[...], b_ref[...])


def _moe_out(xf, yk_parts, wts, sw_gu, sw_down, ln_g, ln_b, alpha, tm=256):
    T, D = xf.shape
    steps = T // len(yk_parts) // tm
    full = lambda a: pl.BlockSpec(a.shape, lambda i: (0,) * a.ndim)
    ln_g, ln_b = ln_g.reshape(1, D), ln_b.reshape(1, D)
    out = None
    for p, yk in enumerate(yk_parts):
        rows = lambda w, off=p * steps: pl.BlockSpec((tm, w), lambda i: (i + off, 0))
        args = [xf, yk, wts, sw_gu, sw_down, ln_g, ln_b]
        in_specs = [rows(D), pl.BlockSpec((TOP_K, tm, D // 2), lambda i: (0, i, 0)), rows(PICK_LANES),
                    full(sw_gu), full(sw_down), full(ln_g), full(ln_b)]
        if out is not None:
            args.append(out)
            in_specs.append(pl.BlockSpec(memory_space=pl.ANY))
        out = pl.pallas_call(
            functools.partial(_moe_out_kernel, alpha=alpha, has_prev=out is not None),
            grid=(steps,),
            in_specs=in_specs,
            out_specs=rows(D),
            out_shape=jax.ShapeDtypeStruct((T, D), F32),
            input_output_aliases={len(args) - 1: 0} if out is not None else {},
            compiler_params=pltpu.CompilerParams(
                dimension_semantics=("parallel",), vmem_limit_bytes=VMEM_LIMIT),
            name="moe_combine_ln",
        )(*args)
    return out


def _moe_ffn_ln(xf, xp, router_w, router_bias, w_gu, w_down, sw_gu, sw_down, ln_g, ln_b, alpha):
    T, D = xf.shape
    E = router_w.shape[1]
    BM = EXPERT_ROWS
    eidx_t, wts_t, pos_t, cnt = _router(xf, router_w, router_bias)
    counts = cnt[:, 0].astype(jnp.int32)
    padded = (counts + BM - 1) // BM * BM
    pad_end = jnp.cumsum(padded)
    pad_start = pad_end - padded
    n_rows = T * TOP_K + E * BM
    n_blocks = n_rows // BM
    blk_row0 = jnp.arange(n_blocks, dtype=jnp.int32) * BM
    blk_e = jnp.minimum(jnp.sum((pad_end[None, :] <= blk_row0[:, None]).astype(jnp.int32), axis=1), E - 1)
    blk_rows = jnp.clip(pad_start[blk_e] + counts[blk_e] - blk_row0, 0, BM).astype(jnp.int32)
    n_used = (pad_end[-1:] // BM).astype(jnp.int32)
    dest_t = _dest_rows(eidx_t, pos_t, pad_start)
    wts = jnp.pad(wts_t.T, ((0, 0), (0, PICK_LANES - TOP_K)))
    xs = _sc_scatter_rows(xp, dest_t, n_rows)
    ys = _expert_ffn(xs, blk_e, blk_rows, n_used, w_gu, w_down)
    part = T // MOE_COMBINE_PARTS
    yk_parts = [_sc_gather_rows(ys, dest_t[:, p * part:(p + 1) * part].reshape(-1)).reshape(TOP_K, part, D // 2)
                for p in range(MOE_COMBINE_PARTS)]
    return _moe_out(xf, yk_parts, wts, sw_gu, sw_down, ln_g, ln_b, alpha)


def kernel(x, w_in, tshift_mu, rwkv_w0, rwkv_w2, rwkv_a0, rwkv_a2, rwkv_g2, rwkv_k_k, rwkv_k_a, rwkv_r_k, rwkv_lnx_w, rwkv_lnx_b, cmp_pe_k, cmp_w1_k, cmp_w2_k, cmp_pe_v, cmp_w1_v, cmp_w2_v, w_branch_a, w_branch_b, w_out, ln1_g, ln1_b, router_w, router_bias, exp_w_gu, exp_w_down, shared_w_gu, shared_w_down, ln2_g, ln2_b):
    B, S, D = x.shape
    depth = w_in.shape[0]
    alpha = (2 * depth) ** 0.25
    nsa_w = w_in.shape[2] - RWKV_IN_W - 2 * D
    for l in range(depth):
        xf = x.reshape(B * S, D)
        w_l = w_in[l]
        w_a = w_l[:, :RWKV_IN_W].astype(BF16)
        w_b = _nsa_weight_columns(w_l[:, RWKV_IN_W:RWKV_IN_W + nsa_w]).astype(BF16)
        w_g = w_l[:, RWKV_IN_W + nsa_w:].astype(BF16)
        kv_w = 6 * NSA_KV_WIDTH
        p_a = _matmul(xf, w_a, PROJ_ROWS, w_a.shape[1]).reshape(B, S, -1)
        p_kv = _matmul(xf, w_b[:, :kv_w], PROJ_ROWS, kv_w).reshape(B, S, -1)
        qg_t = _matmul_t(x, w_b[:, kv_w:].T, PROJ_ROWS)
        p_g = _matmul(xf, w_g, PROJ_ROWS, w_g.shape[1], BF16)
        y_a = _rwkv_time_mix(p_a, tshift_mu[l], rwkv_w0[l], rwkv_w2[l], rwkv_a0[l], rwkv_a2[l], rwkv_g2[l],
                             rwkv_k_k[l], rwkv_k_a[l], rwkv_r_k[l].reshape(-1), rwkv_lnx_w[l], rwkv_lnx_b[l])
        y_b = _nsa_branch(p_kv, qg_t, cmp_pe_k[l], cmp_w1_k[l], cmp_w2_k[l],
                          cmp_pe_v[l], cmp_w1_v[l], cmp_w2_v[l])
        x1, x1p = _mixer_out(xf, y_a.reshape(B * S, -1), y_b.reshape(B * S, -1), p_g,
                             w_branch_a[l].astype(BF16), w_branch_b[l].astype(BF16), w_out[l].astype(BF16),
                             ln1_g[l], ln1_b[l], alpha)
        x2 = _moe_ffn_ln(x1, x1p, router_w[l], router_bias[l], exp_w_gu[l], exp_w_down[l],
                         shared_w_gu[l].astype(BF16), shared_w_down[l].astype(BF16), ln2_g[l], ln2_b[l], alpha)
        x = x2.reshape(B, S, D)
    return x
```

```python
import functools

import numpy as np
import jax
import jax.numpy as jnp
from jax import lax
from jax.experimental import pallas as pl
from jax.experimental.pallas import tpu as pltpu
from jax.experimental.pallas import tpu_sc as plsc

F32 = jnp.float32
BF16 = jnp.bfloat16

RWKV_HEADS = 8
HEAD_DIM = 64
RWKV_WIDTH = RWKV_HEADS * HEAD_DIM
W_LORA = 64
A_LORA = 64
G_LORA = 128
GN_EPS = 64e-5
NSA_HEADS = 8
NSA_GROUPS = 2
NSA_HPG = NSA_HEADS // NSA_GROUPS
NSA_WIDTH = NSA_HEADS * HEAD_DIM
NSA_KV_WIDTH = NSA_GROUPS * HEAD_DIM
CMP_BLOCK = 32
CMP_STRIDE = 16
SEL_BLOCK = 64
N_SELECT = 16
WINDOW = 512
ROPE_THETA = 10000.0
RWKV_IN_W = 3 * RWKV_WIDTH + W_LORA + A_LORA + G_LORA
TOP_K = 8
N_GROUPS = 8
TOPK_GROUPS = 4
ROUTED_SCALE = 2.5
LN_EPS = 1e-5
NEG_INF = -1e30
FORCE_BONUS = 1e4

RWKV_CHUNK = 64
RWKV_HEAD_GROUP = 4
RWKV_STEP_CHUNKS = 2
VMEM_LIMIT = 56 * 1024 * 1024
PROJ_ROWS = 1024


def _bdot(a, b):
    return jnp.dot(a.astype(BF16), b.astype(BF16), preferred_element_type=F32)


def _bdot_tn(a, b):
    return lax.dot_general(a.astype(BF16), b.astype(BF16), (((0,), (0,)), ((), ())),
                           preferred_element_type=F32)


def _bf16_pieces(x, n):
    pieces = []
    for _ in range(n):
        p = x.astype(BF16)
        pieces.append(p)
        x = x - p.astype(F32)
    return pieces


def _dot3(a, b, dims=(((1,), (0,)), ((), ()))):
    (a_hi, a_lo), (b_hi, b_lo) = _bf16_pieces(a, 2), _bf16_pieces(b, 2)
    dot = lambda p, q: lax.dot_general(p, q, dims, preferred_element_type=F32)
    return dot(a_hi, b_hi) + (dot(a_hi, b_lo) + dot(a_lo, b_hi))


def _sigmoid(x):
    return 1.0 / (1.0 + jnp.exp(-x))


def _matmul_kernel(x_ref, w_ref, o_ref):
    o_ref[...] = jnp.dot(x_ref[...].astype(BF16), w_ref[...], preferred_element_type=F32).astype(o_ref.dtype)


def _matmul(x, w, tm, tn, out_dtype=F32):
    M, K = x.shape
    N = w.shape[1]
    return pl.pallas_call(
        _matmul_kernel,
        grid=(M // tm, N // tn),
        in_specs=[pl.BlockSpec((tm, K), lambda i, j: (i, 0)),
                  pl.BlockSpec((K, tn), lambda i, j: (0, j))],
        out_specs=pl.BlockSpec((tm, tn), lambda i, j: (i, j)),
        out_shape=jax.ShapeDtypeStruct((M, N), out_dtype),
        compiler_params=pltpu.CompilerParams(
            dimension_semantics=("parallel", "parallel"), vmem_limit_bytes=VMEM_LIMIT),
        name="dense_proj",
    )(x, w)


def _matmul_t_kernel(x_ref, wt_ref, o_ref):
    o_ref[...] = lax.dot_general(wt_ref[...], x_ref[...].astype(BF16), (((1,), (1,)), ((), ())),
                                 preferred_element_type=F32)


def _matmul_t(x, w_t, tm):
    B, S, K = x.shape
    N = w_t.shape[0]
    return pl.pallas_call(
        _matmul_t_kernel,
        grid=(B, S // tm),
        in_specs=[pl.BlockSpec((None, tm, K), lambda b, s: (b, s, 0)),
                  pl.BlockSpec((N, K), lambda b, s: (0, 0))],
        out_specs=pl.BlockSpec((None, N, tm), lambda b, s: (b, 0, s)),
        out_shape=jax.ShapeDtypeStruct((B, N, S), F32),
        compiler_params=pltpu.CompilerParams(
            dimension_semantics=("parallel", "parallel"), vmem_limit_bytes=VMEM_LIMIT),
        name="dense_proj_t",
    )(x, w_t)


def _rwkv_kernel(p_ref, mu_ref, w0_ref, w2_ref, a0_ref, a2_ref, g2_ref, kk_ref, ka_ref, rk_ref,
                 lnw_ref, lnb_ref, o_ref, carry_ref, state_ref):
    C, H, N = RWKV_CHUNK, RWKV_HEADS, HEAD_DIM
    W = RWKV_WIDTH
    B = p_ref.shape[0]
    NC = p_ref.shape[1] // C
    L = NC * C
    R = B * L

    @pl.when(pl.program_id(0) == 0)
    def _():
        carry_ref[...] = jnp.zeros_like(carry_ref)
        state_ref[...] = jnp.zeros_like(state_ref)

    def per_block(x, rows):
        return jnp.concatenate(
            [jnp.broadcast_to(x[i].reshape(1, -1), (rows, x.shape[-1])) for i in range(x.shape[0])], axis=0)

    p = p_ref[...].reshape(R, p_ref.shape[-1])
    row = lax.broadcasted_iota(jnp.int32, p.shape, 0)
    prev = jnp.where(row % L == 0, per_block(carry_ref[...], L), pltpu.roll(p, 1, axis=0))
    for b in range(B):
        carry_ref[b] = p[b * L + L - 1:b * L + L, :]
    xs = p + (prev - p) * mu_ref[...]
    r = xs[:, 0:W]
    k = xs[:, W:2 * W]
    v = xs[:, 2 * W:3 * W]
    wl = xs[:, 3 * W:3 * W + W_LORA]
    al = xs[:, 3 * W + W_LORA:3 * W + W_LORA + A_LORA]
    gl = xs[:, 3 * W + W_LORA + A_LORA:]

    z = -(w0_ref[...] + _dot3(jnp.tanh(wl), w2_ref[...]))
    softplus = jnp.maximum(z, 0.0) + jnp.log1p(jnp.exp(-jnp.abs(z)))
    logd = -jnp.exp(-softplus - 0.5)
    a = _sigmoid(a0_ref[...] + _dot3(al, a2_ref[...]))
    g = _dot3(_sigmoid(gl), g2_ref[...])

    kk = k * kk_ref[...]
    knew = k * (1.0 + (a - 1.0) * ka_ref[...])

    HG = RWKV_HEAD_GROUP
    GW = HG * N
    same_head_lanes = (lax.broadcasted_iota(jnp.int32, (GW, GW), 0) // N
                       == lax.broadcasted_iota(jnp.int32, (GW, GW), 1) // N)
    head_ones = jnp.where(same_head_lanes, 1.0, 0.0).astype(BF16)

    def head_sum(x):
        hi = x.astype(BF16)
        lo = (x - hi.astype(F32)).astype(BF16)
        return jnp.concatenate(
            [jnp.dot(hi[:, s:s + GW], head_ones, preferred_element_type=F32)
             + jnp.dot(lo[:, s:s + GW], head_ones, preferred_element_type=F32) for s in range(0, W, GW)],
            axis=-1)

    kk = kk / jnp.maximum(jnp.sqrt(head_sum(kk * kk)), 1e-12)
    lr_kk = kk * a

    ti = lax.broadcasted_iota(jnp.int32, (R, R), 0)
    tj = lax.broadcasted_iota(jnp.int32, (R, R), 1)
    same_chunk = (ti >= tj) & (ti // C == tj // C)
    tri = same_chunk.astype(BF16)
    cl = sum(jnp.dot(tri, piece, preferred_element_type=F32) for piece in reversed(_bf16_pieces(logd, 3)))
    cl_end = per_block(jnp.concatenate([cl[i * C + C - 1:i * C + C, :] for i in range(B * NC)], axis=0), C)
    a_hat = -kk * jnp.exp(cl - logd)
    r_hat = r * jnp.exp(cl)
    inv_gam = jnp.exp(-cl)
    b_til = lr_kk * inv_gam
    k_til = knew * inv_gam
    to_end = jnp.exp(cl_end - cl)
    b_end = lr_kk * to_end
    k_end = knew * to_end
    gam_end = jnp.exp(cl_end)

    gt = lax.broadcasted_iota(jnp.int32, (C, GW), 0)
    gc = lax.broadcasted_iota(jnp.int32, (C, GW), 1) % N
    strict = gt > gc
    incl = gt >= gc
    eye = (gt == gc).astype(F32)
    bi = lax.broadcasted_iota(jnp.int32, (HG * C, GW), 0) // C
    bj = lax.broadcasted_iota(jnp.int32, (HG * C, GW), 1) // N
    same_head = bi == bj

    def block_diag(y):
        yb = y.astype(BF16)
        return jnp.where(same_head, jnp.concatenate([yb] * HG, axis=0), jnp.zeros((), BF16))

    def bd_dot(x, y_bd):
        return jnp.dot(x.astype(BF16), y_bd, preferred_element_type=F32)

    def bd_dot_nt(x, y_bd):
        return lax.dot_general(x.astype(BF16), y_bd, (((1,), (1,)), ((), ())), preferred_element_type=F32)

    n_groups = H // HG
    units = [(b, c, gi) for b in range(B) for c in range(NC) for gi in range(n_groups)]
    n_units = range(len(units))
    cut = lambda x, b, c, gi: x[(b * NC + c) * C:(b * NC + c + 1) * C, gi * GW:(gi + 1) * GW]
    v_u = [cut(v, *un) for un in units]
    v_bd = [block_diag(v_u[i]) for i in n_units]
    ar = [jnp.concatenate([cut(a_hat, *un), cut(r_hat, *un)], axis=0) for un in units]
    mb = [bd_dot_nt(ar[i], block_diag(cut(b_til, *units[i]))) for i in n_units]
    mk = [bd_dot_nt(ar[i], block_diag(cut(k_til, *units[i]))) for i in n_units]
    n_ab = [jnp.where(strict, mb[i][:C], 0.0) for i in n_units]
    m_rb = [jnp.where(incl, mb[i][C:], 0.0) for i in n_units]
    l_ak = [jnp.where(strict, mk[i][:C], 0.0) for i in n_units]
    m_rk = [jnp.where(incl, mk[i][C:], 0.0) for i in n_units]

    pw = list(n_ab)
    pw_bd = [block_diag(pw[i]) for i in n_units]
    tinv = [eye + n_ab[i] for i in n_units]
    step = 2
    while step < C:
        pw = [bd_dot(pw[i], pw_bd[i]) for i in n_units]
        pw_bd = [block_diag(pw[i]) for i in n_units]
        tinv = [tinv[i] + bd_dot(tinv[i], pw_bd[i]) for i in n_units]
        step *= 2
    lv = [bd_dot(l_ak[i], v_bd[i]) for i in n_units]

    state = {(b, gi): state_ref[b * n_groups + gi] for b in range(B) for gi in range(n_groups)}
    outs = {}
    for c in range(NC):
        live = [i for i in n_units if units[i][1] == c]
        s0 = {i: state[(units[i][0], units[i][2])] for i in live}
        ars = {i: bd_dot_nt(ar[i], block_diag(s0[i])) for i in live}
        u = {i: bd_dot(tinv[i], block_diag(ars[i][:C] + lv[i])) for i in live}
        for i in live:
            outs[units[i]] = ars[i][C:] + bd_dot(m_rb[i], block_diag(u[i])) + bd_dot(m_rk[i], v_bd[i])
        for i in live:
            b, _, gi = units[i]
            uv = jnp.concatenate([u[i], v_u[i]], axis=0)
            bk_end = jnp.concatenate([cut(b_end, *units[i]), cut(k_end, *units[i])], axis=0)
            cross = jnp.where(same_head, _bdot_tn(uv, bk_end), 0.0)
            upd = cross[0:N]
            for h in range(1, HG):
                upd = upd + cross[h * N:(h + 1) * N]
            state[(b, gi)] = s0[i] * cut(gam_end, *units[i])[0:1] + upd
    for (b, gi), s_new in state.items():
        state_ref[b * n_groups + gi] = s_new

    o = jnp.concatenate([jnp.concatenate([outs[(b, c, gi)] for gi in range(n_groups)], axis=-1)
                         for b in range(B) for c in range(NC)], axis=0)
    mean = head_sum(o) * (1.0 / N)
    var = head_sum(jnp.square(o - mean)) * (1.0 / N)
    o = (o - mean) * lax.rsqrt(var + GN_EPS) * lnw_ref[...] + lnb_ref[...]
    bonus = head_sum(r * knew * rk_ref[...]) * v
    o_ref[...] = ((o + bonus) * g).reshape(o_ref.shape)


def _rwkv_time_mix(p_a, mu, w0, w2, a0, a2, g2, k_k, k_a, r_k, lnx_w, lnx_b):
    B, S, _ = p_a.shape
    L = RWKV_CHUNK * RWKV_STEP_CHUNKS
    row = lambda t: t.reshape(1, -1)
    full = lambda shape: pl.BlockSpec(shape, lambda s: (0,) * len(shape))
    n_units = B * RWKV_HEADS // RWKV_HEAD_GROUP
    return pl.pallas_call(
        _rwkv_kernel,
        grid=(S // L,),
        in_specs=[pl.BlockSpec((B, L, RWKV_IN_W), lambda s: (0, s, 0)),
                  full((1, RWKV_IN_W)), full((1, RWKV_WIDTH)), full((W_LORA, RWKV_WIDTH)),
                  full((1, RWKV_WIDTH)), full((A_LORA, RWKV_WIDTH)), full((G_LORA, RWKV_WIDTH)),
                  full((1, RWKV_WIDTH)), full((1, RWKV_WIDTH)), full((1, RWKV_WIDTH)),
                  full((1, RWKV_WIDTH)), full((1, RWKV_WIDTH))],
        out_specs=pl.BlockSpec((B, L, RWKV_WIDTH), lambda s: (0, s, 0)),
        out_shape=jax.ShapeDtypeStruct((B, S, RWKV_WIDTH), F32),
        scratch_shapes=[pltpu.VMEM((B, 1, RWKV_IN_W), F32),
                        pltpu.VMEM((n_units, HEAD_DIM, RWKV_HEAD_GROUP * HEAD_DIM), F32)],
        compiler_params=pltpu.CompilerParams(
            dimension_semantics=("arbitrary",), vmem_limit_bytes=VMEM_LIMIT),
        name="rwkv7_chunked",
    )(p_a, row(mu), row(w0), w2, row(a0), a2, g2, row(k_k), row(k_a), row(r_k), row(lnx_w), row(lnx_b))


NSA_KV_TILE = 1024
SEL_KEY_TILE = 1024
NSA_QUERY_TILE = 512
SEL_LANES = 128


def _rope_tables(pos, reps):
    half = HEAD_DIM // 2
    inv = ROPE_THETA ** (-jnp.arange(half, dtype=F32) / half)
    ang = pos.astype(F32)[:, None] * inv
    cos, sin = jnp.cos(ang), jnp.sin(ang)
    cosf = jnp.concatenate([cos, cos], -1)
    sinf = jnp.concatenate([-sin, sin], -1)
    return jnp.tile(cosf, (1, reps)), jnp.tile(sinf, (1, reps))


def _rope(x, cosf, sinf):
    width = x.shape[-1]
    lane = lax.broadcasted_iota(jnp.int32, x.shape, 1)
    first_half = (lane % HEAD_DIM) < HEAD_DIM // 2
    rot = jnp.where(first_half, pltpu.roll(x, width - HEAD_DIM // 2, axis=1),
                    pltpu.roll(x, HEAD_DIM // 2, axis=1))
    return x * cosf + rot * sinf


def _kv_layout_kernel(p_ref, cos_ref, sin_ref, kc_ref, vc_ref, ks_ref, vs_ref, kw_ref, vw_ref):
    ts = p_ref.shape[0]
    for i, o_ref in ((0, kc_ref), (1, vc_ref), (2, ks_ref), (4, kw_ref)):
        t = p_ref[:, i * NSA_KV_WIDTH:(i + 1) * NSA_KV_WIDTH]
        if i >= 2:
            t = _rope(t, cos_ref[...], sin_ref[...])
        for g in range(NSA_GROUPS):
            o_ref[g] = t[:, g * HEAD_DIM:(g + 1) * HEAD_DIM].astype(o_ref.dtype)
    pad_row = lax.broadcasted_iota(jnp.int32, (VT_ROWS - HEAD_DIM, ts), 0)
    ones_row = jnp.where(pad_row == 0, 1.0, 0.0)
    for i, o_ref in ((3, vs_ref), (5, vw_ref)):
        t_t = p_ref[:, i * NSA_KV_WIDTH:(i + 1) * NSA_KV_WIDTH].T
        for g in range(NSA_GROUPS):
            o_ref[g] = jnp.concatenate([t_t[g * HEAD_DIM:(g + 1) * HEAD_DIM], ones_row],
                                       axis=0).astype(o_ref.dtype)


def _kv_layout(p_b, cos2, sin2):
    B, S, _ = p_b.shape
    ts = min(NSA_KV_TILE, S)
    out_spec = pl.BlockSpec((None, NSA_GROUPS, ts, HEAD_DIM), lambda b, s: (b, 0, s, 0))
    vt_spec = pl.BlockSpec((None, NSA_GROUPS, VT_ROWS, ts), lambda b, s: (b, 0, 0, s))
    shp = lambda dt: jax.ShapeDtypeStruct((B, NSA_GROUPS, S, HEAD_DIM), dt)
    vt_shp = jax.ShapeDtypeStruct((B, NSA_GROUPS, VT_ROWS, S), BF16)
    return pl.pallas_call(
        _kv_layout_kernel,
        grid=(B, S // ts),
        in_specs=[pl.BlockSpec((None, ts, 6 * NSA_KV_WIDTH), lambda b, s: (b, s, 0)),
                  pl.BlockSpec((ts, NSA_KV_WIDTH), lambda b, s: (s, 0)),
                  pl.BlockSpec((ts, NSA_KV_WIDTH), lambda b, s: (s, 0))],
        out_specs=[out_spec, out_spec, out_spec, vt_spec, out_spec, vt_spec],
        out_shape=[shp(F32), shp(F32), shp(BF16), vt_shp, shp(BF16), vt_shp],
        compiler_params=pltpu.CompilerParams(
            dimension_semantics=("parallel", "parallel"), vmem_limit_bytes=VMEM_LIMIT),
        name="nsa_kv_layout",
    )(p_b, cos2, sin2)


def _compress_kernel(subk_ref, subv_ref, pek_ref, w1k_ref, w2k_ref, pev_ref, w1v_ref, w2v_ref,
                     cos_ref, sin_ref, kc_ref, vc_ref):
    n_sub = subk_ref.shape[0]
    half = CMP_STRIDE * HEAD_DIM

    def mlp(sub_ref, pe_ref, w1_ref, w2_ref):
        sub = sub_ref[...]
        top = _bdot(sub, w1_ref[:half, :])
        bot = _bdot(sub, w1_ref[half:, :])
        bias = _bdot(jnp.broadcast_to(pe_ref[...], (8, 2 * half)), w1_ref[...])[0:1, :]
        h = top + pltpu.roll(bot, n_sub - 1, axis=0) + bias
        return _bdot(jax.nn.gelu(h), w2_ref[...])

    kc = mlp(subk_ref, pek_ref, w1k_ref, w2k_ref)
    rot = jnp.concatenate([kc[:, HEAD_DIM // 2:], kc[:, :HEAD_DIM // 2]], axis=-1)
    kc_ref[...] = (kc * cos_ref[...] + rot * sin_ref[...]).astype(kc_ref.dtype)
    vc_ref[...] = mlp(subv_ref, pev_ref, w1v_ref, w2v_ref).astype(vc_ref.dtype)


def _compress(subk, subv, pe_k, w1_k, w2_k, pe_v, w1_v, w2_v, cos_c, sin_c):
    B, G, n_sub, width = subk.shape
    sub_spec = pl.BlockSpec((None, None, n_sub, width), lambda b, g: (b, g, 0, 0))
    full = lambda a: pl.BlockSpec(a.shape, lambda b, g: (0,) * a.ndim)
    out_spec = pl.BlockSpec((None, None, n_sub, HEAD_DIM), lambda b, g: (b, g, 0, 0))
    pe_k, pe_v = pe_k.reshape(1, -1), pe_v.reshape(1, -1)
    args = (pe_k, w1_k, w2_k, pe_v, w1_v, w2_v, cos_c, sin_c)
    return pl.pallas_call(
        _compress_kernel,
        grid=(B, G),
        in_specs=[sub_spec, sub_spec] + [full(a) for a in args],
        out_specs=[out_spec, out_spec],
        out_shape=[jax.ShapeDtypeStruct((B, G, n_sub, HEAD_DIM), BF16)] * 2,
        compiler_params=pltpu.CompilerParams(
            dimension_semantics=("parallel", "parallel"), vmem_limit_bytes=VMEM_LIMIT),
        name="nsa_compress",
    )(subk, subv, *args)


MAX_FLOOR = -1e20
MASK_BIG = 2.0 ** 100
LOG2_E = 1.4426950408889634
VT_ROWS = 80


def _nsa_kernel(q_ref, gate_ref, cos_ref, sin_ref, kc_ref, vc_ref, ks_ref, vst_ref, kw_ref, vwt_ref,
                mselt_ref, o_ref, blockbias_ref, *, n_pick):
    QB, HP, D = NSA_QUERY_TILE, NSA_HPG, HEAD_DIM
    qb = pl.program_id(2)
    n_cmp = kc_ref.shape[0]
    lanes4 = lambda x: jnp.concatenate([x] * HP, axis=1)

    heads = []
    for n in range(HP):
        qh = q_ref[n * D:(n + 1) * D, :]
        rot = jnp.concatenate([qh[D // 2:], qh[:D // 2]], axis=0)
        heads.append(qh * cos_ref[...] + rot * sin_ref[...])
    q4 = (jnp.concatenate(heads, axis=1) * (D ** -0.5 * LOG2_E)).astype(BF16)
    t_row = qb * QB + lax.broadcasted_iota(jnp.int32, (1, QB), 1)

    def softmax_cols(s_t, bias_t):
        sm = s_t + lanes4(bias_t)
        m = jnp.maximum(jnp.max(sm, axis=0, keepdims=True), MAX_FLOOR)
        return jnp.exp2(sm - m)

    cmp_end = lax.broadcasted_iota(jnp.int32, (n_cmp, 1), 0) * CMP_STRIDE + (CMP_BLOCK - 1)
    e_c = softmax_cols(jnp.dot(kc_ref[...], q4, preferred_element_type=F32),
                       jnp.where(cmp_end <= t_row, 0.0, -MASK_BIG))
    den_c = jnp.sum(e_c, axis=0, keepdims=True)
    p_c = e_c * (1.0 / jnp.where(den_c > 0.0, den_c, 1.0))
    o_c = _bdot_tn(vc_ref[...], p_c)
    p_sum = p_c[:, 0:QB]
    for n in range(1, HP):
        p_sum = p_sum + p_c[:, n * QB:(n + 1) * QB]
    p_hi = p_sum.astype(BF16)
    p_lo = (p_sum - p_hi.astype(F32)).astype(BF16)
    imp_t = (jnp.dot(mselt_ref[...], p_hi, preferred_element_type=F32)
             + jnp.dot(mselt_ref[...], p_lo, preferred_element_type=F32))

    j = lax.broadcasted_iota(jnp.int32, (SEL_LANES, QB), 0)
    cur = t_row // SEL_BLOCK
    valid = j * SEL_BLOCK <= t_row
    forced = (j == 0) | (j == cur) | (j == cur - 1)
    score = jnp.where(valid, imp_t + jnp.where(forced, FORCE_BONUS, 0.0), -1.0)
    for _ in range(n_pick):
        m = jnp.max(score, axis=0, keepdims=True)
        idx = jnp.min(jnp.where(score == m, j, SEL_LANES), axis=0, keepdims=True)
        score = jnp.where(j == idx, -2.0, score)
    blockbias_ref[...] = jnp.where((score == -2.0) & valid, 0.0, -MASK_BIG)

    KT = SEL_KEY_TILE
    blocks_per_tile = KT // SEL_BLOCK
    n_tiles = (qb * QB + QB + KT - 1) // KT

    def sel_step(kt, carry, causal):
        m_i, acc = carry
        start = pl.multiple_of(kt * KT, KT)
        s_t = jnp.dot(ks_ref[pl.ds(start, KT), :], q4, preferred_element_type=F32)
        bias = jnp.concatenate(
            [jnp.broadcast_to(blockbias_ref[pl.ds(kt * blocks_per_tile + jb, 1), :], (SEL_BLOCK, QB))
             for jb in range(blocks_per_tile)], axis=0)
        if causal:
            kpos = start + lax.broadcasted_iota(jnp.int32, (KT, 1), 0)
            bias = jnp.where(kpos <= t_row, bias, -MASK_BIG)
        sm = s_t + lanes4(bias)
        m_new = jnp.maximum(m_i, jnp.max(sm, axis=0, keepdims=True))
        e = jnp.exp2(sm - m_new).astype(BF16)
        acc_new = jnp.exp2(m_i - m_new) * acc + jnp.dot(vst_ref[:, pl.ds(start, KT)], e,
                                                        preferred_element_type=F32)
        return m_new, acc_new

    init = (jnp.full((1, HP * QB), MAX_FLOOR, F32), jnp.zeros((VT_ROWS, HP * QB), F32))
    carry = lax.fori_loop(0, n_tiles - 1, lambda kt, c: sel_step(kt, c, False), init)
    _, acc_s = sel_step(n_tiles - 1, carry, True)
    den_s = acc_s[D:D + 1]
    o_s = acc_s[:D] * (1.0 / jnp.where(den_s > 0.0, den_s, 1.0))

    span = WINDOW + QB
    w_start = pl.multiple_of(jnp.maximum(qb * QB - WINDOW, 0), QB)
    dist = t_row - (w_start + lax.broadcasted_iota(jnp.int32, (span, 1), 0))
    e_w = softmax_cols(jnp.dot(kw_ref[pl.ds(w_start, span), :], q4, preferred_element_type=F32),
                       jnp.where((dist >= 0) & (dist < WINDOW), 0.0, -MASK_BIG))
    acc_w = jnp.dot(vwt_ref[:, pl.ds(w_start, span)], e_w.astype(BF16), preferred_element_type=F32)
    den_w = acc_w[D:D + 1]
    o_w = acc_w[:D] * (1.0 / jnp.where(den_w > 0.0, den_w, 1.0))

    gates = _sigmoid(gate_ref[...])
    gate_row = lambda br: jnp.concatenate([gates[3 * n + br:3 * n + br + 1, :] for n in range(HP)], axis=1)
    o_t = gate_row(0) * o_c + gate_row(1) * o_s + gate_row(2) * o_w
    for n in range(HP):
        o_ref[:, n * D:(n + 1) * D] = o_t[:, n * QB:(n + 1) * QB].T


def _cmp_to_sel_matrix(n_cmp_rows, n_sel):
    ratio = SEL_BLOCK // CMP_STRIDE
    ci = np.arange(n_cmp_rows)[:, None]
    sj = np.arange(SEL_LANES)[None, :]
    m = sum(((ci + n) // ratio == sj).astype(np.float32) for n in range(CMP_BLOCK // CMP_STRIDE))
    m = m * (sj < n_sel) * (ci < n_cmp_rows - 1)
    return jnp.asarray(m.T, BF16)


def _nsa_attention(qg_t, kc, vc, ks, vst, kw, vwt, cos_t, sin_t):
    B, _, S = qg_t.shape
    n_sub = kc.shape[2]
    n_sel = S // SEL_BLOCK
    gw = NSA_HPG * HEAD_DIM
    gate_row0 = NSA_WIDTH // 128
    msel_t = _cmp_to_sel_matrix(n_sub, n_sel)
    at_bg = lambda shape: pl.BlockSpec((None, None) + shape, lambda b, g, i: (b, g, 0, 0))
    const = lambda a: pl.BlockSpec(a.shape, lambda b, g, i: (0, 0))
    return pl.pallas_call(
        functools.partial(_nsa_kernel, n_pick=min(N_SELECT, n_sel)),
        grid=(B, NSA_GROUPS, S // NSA_QUERY_TILE),
        in_specs=[pl.BlockSpec((None, gw, NSA_QUERY_TILE), lambda b, g, i: (b, g, i)),
                  pl.BlockSpec((None, 128, NSA_QUERY_TILE), lambda b, g, i: (b, gate_row0 + g, i)),
                  pl.BlockSpec((HEAD_DIM, NSA_QUERY_TILE), lambda b, g, i: (0, i)),
                  pl.BlockSpec((HEAD_DIM, NSA_QUERY_TILE), lambda b, g, i: (0, i)),
                  at_bg((n_sub, HEAD_DIM)), at_bg((n_sub, HEAD_DIM)),
                  at_bg((S, HEAD_DIM)), at_bg((VT_ROWS, S)), at_bg((S, HEAD_DIM)), at_bg((VT_ROWS, S)),
                  const(msel_t)],
        out_specs=pl.BlockSpec((None, NSA_QUERY_TILE, gw), lambda b, g, i: (b, i, g)),
        out_shape=jax.ShapeDtypeStruct((B, S, NSA_WIDTH), F32),
        scratch_shapes=[pltpu.VMEM((SEL_LANES, NSA_QUERY_TILE), F32)],
        compiler_params=pltpu.CompilerParams(
            dimension_semantics=("parallel", "parallel", "arbitrary"), vmem_limit_bytes=VMEM_LIMIT),
        name="nsa_attention",
    )(qg_t, qg_t, cos_t, sin_t, kc, vc, ks, vst, kw, vwt, msel_t)


def _nsa_branch(p_kv, qg_t, cmp_pe_k, cmp_w1_k, cmp_w2_k, cmp_pe_v, cmp_w1_v, cmp_w2_v):
    B, S, _ = p_kv.shape
    pos = jnp.arange(S)
    cos2, sin2 = _rope_tables(pos, NSA_GROUPS)
    kc_raw, vc_raw, ks, vst, kw, vwt = _kv_layout(p_kv, cos2, sin2)
    n_sub = S // CMP_STRIDE
    sub = lambda t: t.reshape(B, NSA_GROUPS, n_sub, CMP_STRIDE * HEAD_DIM)
    cos_c, sin_c = _rope_tables(jnp.arange(n_sub) * CMP_STRIDE + CMP_BLOCK - 1, 1)
    kc, vc = _compress(sub(kc_raw), sub(vc_raw), cmp_pe_k, cmp_w1_k, cmp_w2_k,
                       cmp_pe_v, cmp_w1_v, cmp_w2_v, cos_c, sin_c)
    cos_q, sin_q = _rope_tables(pos, 1)
    return _nsa_attention(qg_t, kc, vc, ks, vst, kw, vwt, cos_q.T, sin_q.T)


def _nsa_weight_columns(w_nsa):
    K = w_nsa.shape[0]
    q = w_nsa[:, :NSA_WIDTH]
    kv = w_nsa[:, NSA_WIDTH:NSA_WIDTH + 6 * NSA_KV_WIDTH]
    gates = w_nsa[:, NSA_WIDTH + 6 * NSA_KV_WIDTH:]
    per_group = NSA_HPG * 3
    gate_blocks = [jnp.pad(gates[:, g * per_group:(g + 1) * per_group], ((0, 0), (0, 128 - per_group)))
                   for g in range(NSA_GROUPS)]
    return jnp.concatenate([kv, q] + gate_blocks, axis=1)


def _layer_norm(h, g, b):
    mu = jnp.mean(h, axis=-1, keepdims=True)
    var = jnp.mean(jnp.square(h - mu), axis=-1, keepdims=True)
    return (h - mu) * lax.rsqrt(var + LN_EPS) * g + b


def _pack_bf16_halves(x):
    n = x.shape[-1] // 2
    bits = lax.bitcast_convert_type(x.astype(BF16).astype(F32), jnp.uint32)
    return (bits[:, n:] & jnp.uint32(0xFFFF0000)) | (bits[:, :n] >> 16)


def _unpack_bf16_halves(u):
    left = lax.bitcast_convert_type(u << 16, F32)
    right = lax.bitcast_convert_type(u & jnp.uint32(0xFFFF0000), F32)
    return left, right


def _mixer_out_kernel(x_ref, ya_ref, yb_ref, pg_ref, wa_ref, wb_ref, wo_ref, g_ref, b_ref, o_ref, op_ref,
                      *, alpha):
    d = x_ref.shape[-1]
    gate_a = _sigmoid(pg_ref[:, :d].astype(F32))
    gate_b = _sigmoid(pg_ref[:, d:].astype(F32))
    mixed = gate_a * _bdot(ya_ref[...], wa_ref[...]) + gate_b * _bdot(yb_ref[...], wb_ref[...])
    h = alpha * x_ref[...] + _bdot(mixed, wo_ref[...])
    out = _layer_norm(h, g_ref[...], b_ref[...])
    o_ref[...] = out
    op_ref[...] = _pack_bf16_halves(out)


def _mixer_out(xf, ya, yb, p_g, wa, wb, wo, ln_g, ln_b, alpha, tm=512):
    T, D = xf.shape
    rows = lambda w: pl.BlockSpec((tm, w), lambda i: (i, 0))
    full = lambda a: pl.BlockSpec(a.shape, lambda i: (0,) * a.ndim)
    ln_g, ln_b = ln_g.reshape(1, D), ln_b.reshape(1, D)
    return pl.pallas_call(
        functools.partial(_mixer_out_kernel, alpha=alpha),
        grid=(T // tm,),
        in_specs=[rows(D), rows(ya.shape[1]), rows(yb.shape[1]), rows(2 * D),
                  full(wa), full(wb), full(wo), full(ln_g), full(ln_b)],
        out_specs=[rows(D), rows(D // 2)],
        out_shape=[jax.ShapeDtypeStruct((T, D), F32), jax.ShapeDtypeStruct((T, D // 2), jnp.uint32)],
        compiler_params=pltpu.CompilerParams(
            dimension_semantics=("parallel",), vmem_limit_bytes=VMEM_LIMIT),
        name="mixer_out_ln",
    )(xf, ya, yb, p_g, wa, wb, wo, ln_g, ln_b)


ROUTER_TILE = 512
EXPERT_ROWS = 512
MOE_COMBINE_PARTS = 2
SC_TOKEN_CHUNK = 64
SC_ROW_CHUNK = 64
PICK_LANES = 128
LOWEST = -3.0e38


def _router_kernel(x_ref, rwt_ref, bias_ref, eidx_ref, wts_ref, pos_ref, cnt_ref, carry_ref):
    tm, E = x_ref.shape[0], rwt_ref.shape[0]
    per_group = E // N_GROUPS
    reps = tm // PICK_LANES

    @pl.when(pl.program_id(0) == 0)
    def _():
        carry_ref[...] = jnp.zeros_like(carry_ref)

    scores = _sigmoid(_dot3(rwt_ref[...], x_ref[...], (((1,), (1,)), ((), ()))))
    choice = scores + jnp.concatenate([bias_ref[...]] * reps, axis=1)
    row = lax.broadcasted_iota(jnp.int32, (E, tm), 0)

    def first_max(vals, rows):
        m = jnp.max(vals, axis=0, keepdims=True)
        return m, jnp.min(jnp.where(vals == m, rows, E), axis=0, keepdims=True)

    group_score = []
    for g in range(N_GROUPS):
        rows = slice(g * per_group, (g + 1) * per_group)
        group_row = g * per_group + lax.broadcasted_iota(jnp.int32, (per_group, tm), 0)
        m1, i1 = first_max(choice[rows], group_row)
        m2 = jnp.max(jnp.where(group_row == i1, LOWEST, choice[rows]), axis=0, keepdims=True)
        group_score.append(m1 + m2)
    masked = []
    for g in range(N_GROUPS):
        rank = jnp.zeros((1, tm), jnp.int32)
        for o in range(N_GROUPS):
            if o != g:
                ahead = (group_score[o] > group_score[g]) if o > g else (group_score[o] >= group_score[g])
                rank = rank + ahead.astype(jnp.int32)
        masked.append(jnp.where(rank < TOPK_GROUPS, choice[g * per_group:(g + 1) * per_group], NEG_INF))

    cur = jnp.concatenate(masked, axis=0)
    picks = []
    for _ in range(TOP_K):
        _, idx = first_max(cur, row)
        picks.append(idx)
        cur = jnp.where(row == idx, LOWEST, cur)
    sel = jnp.where(cur == LOWEST, 1.0, 0.0)
    gate = scores * sel
    gate = gate * (ROUTED_SCALE / jnp.sum(gate, axis=0, keepdims=True))

    ti = lax.broadcasted_iota(jnp.int32, (tm, tm), 0)
    tj = lax.broadcasted_iota(jnp.int32, (tm, tm), 1)
    sel_b = sel.astype(BF16)
    before = jnp.dot(sel_b, (ti < tj).astype(BF16), preferred_element_type=F32)
    queue_pos = before + jnp.concatenate([carry_ref[...]] * reps, axis=1)
    carry_ref[...] = carry_ref[...] + jnp.dot(sel_b, jnp.ones((tm, PICK_LANES), BF16),
                                              preferred_element_type=F32)
    cnt_ref[...] = carry_ref[...]

    at_pick = lambda vals, idx: jnp.sum(jnp.where(row == idx, vals, 0.0), axis=0, keepdims=True)
    eidx_ref[...] = jnp.concatenate(picks, axis=0)
    wts_ref[...] = jnp.concatenate([at_pick(gate, idx) for idx in picks], axis=0)
    pos_ref[...] = jnp.concatenate([at_pick(queue_pos, idx) for idx in picks], axis=0).astype(jnp.int32)


def _router(xf, router_w, router_bias):
    T, D = xf.shape
    E = router_w.shape[1]
    tm = ROUTER_TILE
    picks = lambda dt: jax.ShapeDtypeStruct((TOP_K, T), dt)
    pick_spec = pl.BlockSpec((TOP_K, tm), lambda i: (0, i))
    lanes = lambda v: jnp.broadcast_to(v.reshape(E, 1), (E, PICK_LANES))
    return pl.pallas_call(
        _router_kernel,
        grid=(T // tm,),
        in_specs=[pl.BlockSpec((tm, D), lambda i: (i, 0)),
                  pl.BlockSpec((E, D), lambda i: (0, 0)),
                  pl.BlockSpec((E, PICK_LANES), lambda i: (0, 0))],
        out_specs=[pick_spec, pick_spec, pick_spec, pl.BlockSpec((E, PICK_LANES), lambda i: (0, 0))],
        out_shape=[picks(jnp.int32), picks(F32), picks(jnp.int32),
                   jax.ShapeDtypeStruct((E, PICK_LANES), F32)],
        scratch_shapes=[pltpu.VMEM((E, PICK_LANES), F32)],
        compiler_params=pltpu.CompilerParams(
            dimension_semantics=("arbitrary",), vmem_limit_bytes=VMEM_LIMIT),
        name="moe_router",
    )(xf, router_w.T, lanes(router_bias))


def _dest_kernel(eidx_ref, pos_ref, start_ref, dest_ref):
    E = start_ref.shape[0]
    tm = eidx_ref.shape[1]
    row = lax.broadcasted_iota(jnp.int32, (E, tm), 0)
    start = jnp.concatenate([start_ref[...]] * (tm // PICK_LANES), axis=1)
    eidx = eidx_ref[...]
    base = [jnp.sum(jnp.where(row == eidx[kk:kk + 1, :], start, 0), axis=0, keepdims=True)
            for kk in range(TOP_K)]
    dest_ref[...] = jnp.concatenate(base, axis=0) + pos_ref[...]


def _dest_rows(eidx_t, pos_t, pad_start):
    T = eidx_t.shape[1]
    E = pad_start.shape[0]
    tm = ROUTER_TILE
    pick_spec = pl.BlockSpec((TOP_K, tm), lambda i: (0, i))
    return pl.pallas_call(
        _dest_kernel,
        grid=(T // tm,),
        in_specs=[pick_spec, pick_spec, pl.BlockSpec((E, PICK_LANES), lambda i: (0, 0))],
        out_specs=pick_spec,
        out_shape=jax.ShapeDtypeStruct((TOP_K, T), jnp.int32),
        compiler_params=pltpu.CompilerParams(
            dimension_semantics=("parallel",), vmem_limit_bytes=VMEM_LIMIT),
        name="moe_dest_rows",
    )(eidx_t, pos_t, jnp.broadcast_to(pad_start.reshape(E, 1), (E, PICK_LANES)))


def _sc_mesh():
    return plsc.VectorSubcoreMesh(core_axis_name="c", subcore_axis_name="s")


def _sc_scatter_rows(x, dest_t, n_rows):
    T, D = x.shape
    K = dest_t.shape[0]
    mesh = _sc_mesh()
    nc, nw = mesh.num_cores, mesh.num_cores * mesh.num_subcores
    per_w = T // nw
    chunk = min(SC_TOKEN_CHUNK, per_w)
    n_chunks = per_w // chunk
    idx = dest_t.reshape(K, nw, n_chunks, chunk).transpose(1, 2, 0, 3).reshape(nw, n_chunks * K, chunk)

    assert n_chunks % 2 == 0

    @functools.partial(
        pl.kernel, mesh=mesh,
        out_type=jax.ShapeDtypeStruct((n_rows, D), x.dtype),
        scratch_types=[pltpu.VMEM((n_chunks * K, chunk), jnp.int32),
                       pltpu.VMEM((2, chunk, D), x.dtype),
                       pltpu.SemaphoreType.DMA((2,)), pltpu.SemaphoreType.DMA((2,))],
    )
    def scatter(x_hbm, idx_hbm, out_hbm, idx_v, rows_v, load_sem, send_sem):
        wid = lax.axis_index("s") * nc + lax.axis_index("c")
        pltpu.sync_copy(idx_hbm.at[wid], idx_v)

        def load(j, b):
            return pltpu.make_async_copy(x_hbm.at[pl.ds(wid * per_w + j * chunk, chunk)], rows_v.at[b],
                                         load_sem.at[b])

        def sends(j, b):
            return [pltpu.make_async_copy(rows_v.at[b], out_hbm.at[idx_v.at[j * K + kk]], send_sem.at[b])
                    for kk in range(K)]

        load(0, 0).start()

        @pl.loop(0, n_chunks, step=2)
        def _(j0):
            for b in range(2):
                j = j0 + b
                load(j, b).wait()

                @pl.when(j >= 1)
                def _():
                    for c in sends(j - 1, 1 - b):
                        c.wait()

                @pl.when(j + 1 < n_chunks)
                def _():
                    load(j + 1, 1 - b).start()

                for c in sends(j, b):
                    c.start()

        for c in sends(n_chunks - 1, (n_chunks - 1) % 2):
            c.wait()

    return scatter(x, idx)


def _sc_gather_rows(src, idx):
    M = idx.shape[0]
    D = src.shape[1]
    mesh = _sc_mesh()
    nc, nw = mesh.num_cores, mesh.num_cores * mesh.num_subcores
    per_w = M // nw
    chunk = min(SC_ROW_CHUNK, per_w)
    n_chunks = per_w // chunk

    assert n_chunks % 2 == 0

    @functools.partial(
        pl.kernel, mesh=mesh,
        out_type=jax.ShapeDtypeStruct((M, D), src.dtype),
        scratch_types=[pltpu.VMEM((n_chunks, chunk), jnp.int32),
                       pltpu.VMEM((2, chunk, D), src.dtype),
                       pltpu.SemaphoreType.DMA((2,)), pltpu.SemaphoreType.DMA((2,))],
    )
    def gather(src_hbm, idx_hbm, out_hbm, idx_v, rows_v, fetch_sem, store_sem):
        wid = lax.axis_index("s") * nc + lax.axis_index("c")
        pltpu.sync_copy(idx_hbm.at[wid], idx_v)

        def fetch(j, b):
            return pltpu.make_async_copy(src_hbm.at[idx_v.at[j]], rows_v.at[b], fetch_sem.at[b])

        def store(j, b):
            return pltpu.make_async_copy(rows_v.at[b], out_hbm.at[pl.ds(wid * per_w + j * chunk, chunk)],
                                         store_sem.at[b])

        fetch(0, 0).start()

        @pl.loop(0, n_chunks, step=2)
        def _(j0):
            for b in range(2):
                j = j0 + b
                fetch(j, b).wait()

                @pl.when(j >= 1)
                def _():
                    store(j - 1, 1 - b).wait()

                @pl.when(j + 1 < n_chunks)
                def _():
                    fetch(j + 1, 1 - b).start()

                store(j, b).start()

        store(n_chunks - 1, (n_chunks - 1) % 2).wait()

    return gather(src, idx.reshape(nw, n_chunks, chunk))


def _expert_kernel(distinct_e_ref, blk_ord_ref, blk_new_ref, blk_rows_ref, n_used_ref, n_distinct_ref,
                   x_ref, wgu_hbm, wd_hbm, o_ref, wgu_buf, wd_buf, wgu_bf, wd_bf, sem):
    i = pl.program_id(0)
    live = i < n_used_ref[0]
    ordinal = blk_ord_ref[i]
    slot = ordinal % 2

    def weight_copies(k, s):
        e = distinct_e_ref[k]
        return (pltpu.make_async_copy(wgu_hbm.at[e], wgu_buf.at[s], sem.at[0, s]),
                pltpu.make_async_copy(wd_hbm.at[e], wd_buf.at[s], sem.at[1, s]))

    @pl.when(i == 0)
    def _():
        for c in weight_copies(0, 0):
            c.start()

    @pl.when(live & (blk_new_ref[i] == 1))
    def _():
        for c in weight_copies(ordinal, slot):
            c.wait()

        @pl.when(ordinal + 1 < n_distinct_ref[0])
        def _():
            for c in weight_copies(ordinal + 1, 1 - slot):
                c.start()

        wgu_bf[...] = wgu_buf[slot].astype(BF16)
        wd_bf[...] = wd_buf[slot].astype(BF16)

    @pl.when(live)
    def _():
        hidden = wd_bf.shape[0]
        half = x_ref.shape[1]
        row = lax.broadcasted_iota(jnp.int32, x_ref.shape, 0)
        left, right = _unpack_bf16_halves(x_ref[...])
        real = row < blk_rows_ref[i]
        left = jnp.where(real, left, 0.0).astype(BF16)
        right = jnp.where(real, right, 0.0).astype(BF16)
        h = (jnp.dot(left, wgu_bf[:half, :], preferred_element_type=F32)
             + jnp.dot(right, wgu_bf[half:, :], preferred_element_type=F32))
        gate, up = h[:, :hidden], h[:, hidden:]
        act = (gate * _sigmoid(gate) * up).astype(BF16)
        o_ref[...] = _pack_bf16_halves(jnp.dot(act, wd_bf[...], preferred_element_type=F32))

    @pl.when(jnp.logical_not(live))
    def _():
        o_ref[...] = jnp.zeros_like(o_ref)


def _expert_ffn(xs, blk_e, blk_rows, n_used, w_gu, w_down):
    n_rows, half = xs.shape
    E, D, two_h = w_gu.shape
    n_blocks = n_rows // EXPERT_ROWS
    idx = jnp.arange(n_blocks, dtype=jnp.int32)
    is_live = idx < n_used[0]
    blk_new = (is_live & ((idx == 0) | (blk_e != jnp.roll(blk_e, 1)))).astype(jnp.int32)
    blk_ord = (jnp.cumsum(blk_new) - 1).astype(jnp.int32)
    n_distinct = blk_ord[-1:] + 1
    first_of = (blk_new[None, :] == 1) & (blk_ord[None, :] == idx[:, None])
    distinct_e = jnp.sum(jnp.where(first_of, blk_e[None, :], 0), axis=1).astype(jnp.int32)

    live = lambda i, nu: jnp.minimum(i, nu[0] - 1)
    grid_spec = pltpu.PrefetchScalarGridSpec(
        num_scalar_prefetch=6,
        grid=(n_blocks,),
        in_specs=[pl.BlockSpec((EXPERT_ROWS, half), lambda i, de, bo, bn, br, nu, nd: (live(i, nu), 0)),
                  pl.BlockSpec(memory_space=pl.ANY), pl.BlockSpec(memory_space=pl.ANY)],
        out_specs=pl.BlockSpec((EXPERT_ROWS, half), lambda i, de, bo, bn, br, nu, nd: (i, 0)),
        scratch_shapes=[pltpu.VMEM((2, D, two_h), F32), pltpu.VMEM((2, two_h // 2, D), F32),
                        pltpu.VMEM((D, two_h), BF16), pltpu.VMEM((two_h // 2, D), BF16),
                        pltpu.SemaphoreType.DMA((2, 2))],
    )
    return pl.pallas_call(
        _expert_kernel,
        grid_spec=grid_spec,
        out_shape=jax.ShapeDtypeStruct((n_rows, half), jnp.uint32),
        compiler_params=pltpu.CompilerParams(
            dimension_semantics=("arbitrary",), vmem_limit_bytes=VMEM_LIMIT),
        name="moe_experts",
    )(distinct_e, blk_ord, blk_new, blk_rows, n_used, n_distinct, xs, w_gu, w_down)


def _moe_out_kernel(x_ref, yk_ref, wts_ref, sgu_ref, sd_ref, g_ref, b_ref, *rest, alpha, has_prev):
    o_ref = rest[1] if has_prev else rest[0]
    x = x_ref[...]
    hidden = sd_ref.shape[0]
    h = _bdot(x, sgu_ref[...])
    gate, up = h[:, :hidden], h[:, hidden:]
    ffn = _bdot(gate * _sigmoid(gate) * up, sd_ref[...])
    wts = wts_ref[...]
    routed_left = routed_right = None
    for kk in range(TOP_K):
        left, right = _unpack_bf16_halves(yk_ref[kk])
        w = wts[:, kk:kk + 1]
        routed_left = w * left if kk == 0 else routed_left + w * left
        routed_right = w * right if kk == 0 else routed_right + w * right
    ffn = ffn + jnp.concatenate([routed_left, routed_right], axis=-1)
    o_ref[...] = _layer_norm(alpha * x + ffn, g_ref[...], b_ref[...])


def _moe_out(xf, yk_parts, wts, sw_gu, sw_down, ln_g, ln_b, alpha, tm=256):
    T, D = xf.shape
    steps = T // len(yk_parts) // tm
    full = lambda a: pl.BlockSpec(a.shape, lambda i: (0,) * a.ndim)
    ln_g, ln_b = ln_g.reshape(1, D), ln_b.reshape(1, D)
    out = None
    for p, yk in enumerate(yk_parts):
        rows = lambda w, off=p * steps: pl.BlockSpec((tm, w), lambda i: (i + off, 0))
        args = [xf, yk, wts, sw_gu, sw_down, ln_g, ln_b]
        in_specs = [rows(D), pl.BlockSpec((TOP_K, tm, D // 2), lambda i: (0, i, 0)), rows(PICK_LANES),
                    full(sw_gu), full(sw_down), full(ln_g), full(ln_b)]
        if out is not None:
            args.append(out)
            in_specs.append(pl.BlockSpec(memory_space=pl.ANY))
        out = pl.pallas_call(
            functools.partial(_moe_out_kernel, alpha=alpha, has_prev=out is not None),
            grid=(steps,),
            in_specs=in_specs,
            out_specs=rows(D),
            out_shape=jax.ShapeDtypeStruct((T, D), F32),
            input_output_aliases={len(args) - 1: 0} if out is not None else {},
            compiler_params=pltpu.CompilerParams(
                dimension_semantics=("parallel",), vmem_limit_bytes=VMEM_LIMIT),
            name="moe_combine_ln",
        )(*args)
    return out


def _moe_ffn_ln(xf, xp, router_w, router_bias, w_gu, w_down, sw_gu, sw_down, ln_g, ln_b, alpha):
    T, D = xf.shape
    E = router_w.shape[1]
    BM = EXPERT_ROWS
    eidx_t, wts_t, pos_t, cnt = _router(xf, router_w, router_bias)
    counts = cnt[:, 0].astype(jnp.int32)
    padded = (counts + BM - 1) // BM * BM
    pad_end = jnp.cumsum(padded)
    pad_start = pad_end - padded
    n_rows = T * TOP_K + E * BM
    n_blocks = n_rows // BM
    blk_row0 = jnp.arange(n_blocks, dtype=jnp.int32) * BM
    blk_e = jnp.minimum(jnp.sum((pad_end[None, :] <= blk_row0[:, None]).astype(jnp.int32), axis=1), E - 1)
    blk_rows = jnp.clip(pad_start[blk_e] + counts[blk_e] - blk_row0, 0, BM).astype(jnp.int32)
    n_used = (pad_end[-1:] // BM).astype(jnp.int32)
    dest_t = _dest_rows(eidx_t, pos_t, pad_start)
    wts = jnp.pad(wts_t.T, ((0, 0), (0, PICK_LANES - TOP_K)))
    xs = _sc_scatter_rows(xp, dest_t, n_rows)
    ys = _expert_ffn(xs, blk_e, blk_rows, n_used, w_gu, w_down)
    part = T // MOE_COMBINE_PARTS
    yk_parts = [_sc_gather_rows(ys, dest_t[:, p * part:(p + 1) * part].reshape(-1)).reshape(TOP_K, part, D // 2)
                for p in range(MOE_COMBINE_PARTS)]
    return _moe_out(xf, yk_parts, wts, sw_gu, sw_down, ln_g, ln_b, alpha)


def kernel(x, w_in, tshift_mu, rwkv_w0, rwkv_w2, rwkv_a0, rwkv_a2, rwkv_g2, rwkv_k_k, rwkv_k_a, rwkv_r_k, rwkv_lnx_w, rwkv_lnx_b, cmp_pe_k, cmp_w1_k, cmp_w2_k, cmp_pe_v, cmp_w1_v, cmp_w2_v, w_branch_a, w_branch_b, w_out, ln1_g, ln1_b, router_w, router_bias, exp_w_gu, exp_w_down, shared_w_gu, shared_w_down, ln2_g, ln2_b):
    B, S, D = x.shape
    depth = w_in.shape[0]
    alpha = (2 * depth) ** 0.25
    nsa_w = w_in.shape[2] - RWKV_IN_W - 2 * D
    for l in range(depth):
        xf = x.reshape(B * S, D)
        w_l = w_in[l]
        w_a = w_l[:, :RWKV_IN_W].astype(BF16)
        w_b = _nsa_weight_columns(w_l[:, RWKV_IN_W:RWKV_IN_W + nsa_w]).astype(BF16)
        w_g = w_l[:, RWKV_IN_W + nsa_w:].astype(BF16)
        kv_w = 6 * NSA_KV_WIDTH
        p_a = _matmul(xf, w_a, PROJ_ROWS, w_a.shape[1]).reshape(B, S, -1)
        p_kv = _matmul(xf, w_b[:, :kv_w], PROJ_ROWS, kv_w).reshape(B, S, -1)
        qg_t = _matmul_t(x, w_b[:, kv_w:].T, PROJ_ROWS)
        p_g = _matmul(xf, w_g, PROJ_ROWS, w_g.shape[1], BF16)
        y_a = _rwkv_time_mix(p_a, tshift_mu[l], rwkv_w0[l], rwkv_w2[l], rwkv_a0[l], rwkv_a2[l], rwkv_g2[l],
                             rwkv_k_k[l], rwkv_k_a[l], rwkv_r_k[l].reshape(-1), rwkv_lnx_w[l], rwkv_lnx_b[l])
        y_b = _nsa_branch(p_kv, qg_t, cmp_pe_k[l], cmp_w1_k[l], cmp_w2_k[l],
                          cmp_pe_v[l], cmp_w1_v[l], cmp_w2_v[l])
        x1, x1p = _mixer_out(xf, y_a.reshape(B * S, -1), y_b.reshape(B * S, -1), p_g,
                             w_branch_a[l].astype(BF16), w_branch_b[l].astype(BF16), w_out[l].astype(BF16),
                             ln1_g[l], ln1_b[l], alpha)
        x2 = _moe_ffn_ln(x1, x1p, router_w[l], router_bias[l], exp_w_gu[l], exp_w_down[l],
                         shared_w_gu[l].astype(BF16), shared_w_down[l].astype(BF16), ln2_g[l], ln2_b[l], alpha)
        x = x2.reshape(B, S, D)
    return x
```

```python
import functools

import numpy as np
import jax
import jax.numpy as jnp
from jax import lax
from jax.experimental import pallas as pl
from jax.experimental.pallas import tpu as pltpu
from jax.experimental.pallas import tpu_sc as plsc

F32 = jnp.float32
BF16 = jnp.bfloat16

RWKV_HEADS = 8
HEAD_DIM = 64
RWKV_WIDTH = RWKV_HEADS * HEAD_DIM
W_LORA = 64
A_LORA = 64
G_LORA = 128
GN_EPS = 64e-5
NSA_HEADS = 8
NSA_GROUPS = 2
NSA_HPG = NSA_HEADS // NSA_GROUPS
NSA_WIDTH = NSA_HEADS * HEAD_DIM
NSA_KV_WIDTH = NSA_GROUPS * HEAD_DIM
CMP_BLOCK = 32
CMP_STRIDE = 16
SEL_BLOCK = 64
N_SELECT = 16
WINDOW = 512
ROPE_THETA = 10000.0
RWKV_IN_W = 3 * RWKV_WIDTH + W_LORA + A_LORA + G_LORA
TOP_K = 8
N_GROUPS = 8
TOPK_GROUPS = 4
ROUTED_SCALE = 2.5
LN_EPS = 1e-5
NEG_INF = -1e30
FORCE_BONUS = 1e4

RWKV_CHUNK = 64
RWKV_HEAD_GROUP = 4
RWKV_STEP_CHUNKS = 2
VMEM_LIMIT = 56 * 1024 * 1024
PROJ_ROWS = 1024


def _bdot(a, b):
    return jnp.dot(a.astype(BF16), b.astype(BF16), preferred_element_type=F32)


def _bdot_tn(a, b):
    return lax.dot_general(a.astype(BF16), b.astype(BF16), (((0,), (0,)), ((), ())),
                           preferred_element_type=F32)


def _bf16_pieces(x, n):
    pieces = []
    for _ in range(n):
        p = x.astype(BF16)
        pieces.append(p)
        x = x - p.astype(F32)
    return pieces


def _dot3(a, b, dims=(((1,), (0,)), ((), ()))):
    (a_hi, a_lo), (b_hi, b_lo) = _bf16_pieces(a, 2), _bf16_pieces(b, 2)
    dot = lambda p, q: lax.dot_general(p, q, dims, preferred_element_type=F32)
    return dot(a_hi, b_hi) + (dot(a_hi, b_lo) + dot(a_lo, b_hi))


def _sigmoid(x):
    return 1.0 / (1.0 + jnp.exp(-x))


def _matmul_kernel(x_ref, w_ref, o_ref):
    o_ref[...] = jnp.dot(x_ref[...].astype(BF16), w_ref[...], preferred_element_type=F32).astype(o_ref.dtype)


def _matmul(x, w, tm, tn, out_dtype=F32):
    M, K = x.shape
    N = w.shape[1]
    return pl.pallas_call(
        _matmul_kernel,
        grid=(M // tm, N // tn),
        in_specs=[pl.BlockSpec((tm, K), lambda i, j: (i, 0)),
                  pl.BlockSpec((K, tn), lambda i, j: (0, j))],
        out_specs=pl.BlockSpec((tm, tn), lambda i, j: (i, j)),
        out_shape=jax.ShapeDtypeStruct((M, N), out_dtype),
        compiler_params=pltpu.CompilerParams(
            dimension_semantics=("parallel", "parallel"), vmem_limit_bytes=VMEM_LIMIT),
        name="dense_proj",
    )(x, w)


def _matmul_t_kernel(x_ref, wt_ref, o_ref):
    o_ref[...] = lax.dot_general(wt_ref[...], x_ref[...].astype(BF16), (((1,), (1,)), ((), ())),
                                 preferred_element_type=F32)


def _matmul_t(x, w_t, tm):
    B, S, K = x.shape
    N = w_t.shape[0]
    return pl.pallas_call(
        _matmul_t_kernel,
        grid=(B, S // tm),
        in_specs=[pl.BlockSpec((None, tm, K), lambda b, s: (b, s, 0)),
                  pl.BlockSpec((N, K), lambda b, s: (0, 0))],
        out_specs=pl.BlockSpec((None, N, tm), lambda b, s: (b, 0, s)),
        out_shape=jax.ShapeDtypeStruct((B, N, S), F32),
        compiler_params=pltpu.CompilerParams(
            dimension_semantics=("parallel", "parallel"), vmem_limit_bytes=VMEM_LIMIT),
        name="dense_proj_t",
    )(x, w_t)


def _rwkv_kernel(p_ref, mu_ref, w0_ref, w2_ref, a0_ref, a2_ref, g2_ref, kk_ref, ka_ref, rk_ref,
                 lnw_ref, lnb_ref, o_ref, carry_ref, state_ref):
    C, H, N = RWKV_CHUNK, RWKV_HEADS, HEAD_DIM
    W = RWKV_WIDTH
    B = p_ref.shape[0]
    NC = p_ref.shape[1] // C
    L = NC * C
    R = B * L

    @pl.when(pl.program_id(0) == 0)
    def _():
        carry_ref[...] = jnp.zeros_like(carry_ref)
        state_ref[...] = jnp.zeros_like(state_ref)

    def per_block(x, rows):
        return jnp.concatenate(
            [jnp.broadcast_to(x[i].reshape(1, -1), (rows, x.shape[-1])) for i in range(x.shape[0])], axis=0)

    p = p_ref[...].reshape(R, p_ref.shape[-1])
    row = lax.broadcasted_iota(jnp.int32, p.shape, 0)
    prev = jnp.where(row % L == 0, per_block(carry_ref[...], L), pltpu.roll(p, 1, axis=0))
    for b in range(B):
        carry_ref[b] = p[b * L + L - 1:b * L + L, :]
    xs = p + (prev - p) * mu_ref[...]
    r = xs[:, 0:W]
    k = xs[:, W:2 * W]
    v = xs[:, 2 * W:3 * W]
    wl = xs[:, 3 * W:3 * W + W_LORA]
    al = xs[:, 3 * W + W_LORA:3 * W + W_LORA + A_LORA]
    gl = xs[:, 3 * W + W_LORA + A_LORA:]

    z = -(w0_ref[...] + _dot3(jnp.tanh(wl), w2_ref[...]))
    softplus = jnp.maximum(z, 0.0) + jnp.log1p(jnp.exp(-jnp.abs(z)))
    logd = -jnp.exp(-softplus - 0.5)
    a = _sigmoid(a0_ref[...] + _dot3(al, a2_ref[...]))
    g = _dot3(_sigmoid(gl), g2_ref[...])

    kk = k * kk_ref[...]
    knew = k * (1.0 + (a - 1.0) * ka_ref[...])

    HG = RWKV_HEAD_GROUP
    GW = HG * N
    same_head_lanes = (lax.broadcasted_iota(jnp.int32, (GW, GW), 0) // N
                       == lax.broadcasted_iota(jnp.int32, (GW, GW), 1) // N)
    head_ones = jnp.where(same_head_lanes, 1.0, 0.0).astype(BF16)

    def head_sum(x):
        hi = x.astype(BF16)
        lo = (x - hi.astype(F32)).astype(BF16)
        return jnp.concatenate(
            [jnp.dot(hi[:, s:s + GW], head_ones, preferred_element_type=F32)
             + jnp.dot(lo[:, s:s + GW], head_ones, preferred_element_type=F32) for s in range(0, W, GW)],
            axis=-1)

    kk = kk / jnp.maximum(jnp.sqrt(head_sum(kk * kk)), 1e-12)
    lr_kk = kk * a

    ti = lax.broadcasted_iota(jnp.int32, (R, R), 0)
    tj = lax.broadcasted_iota(jnp.int32, (R, R), 1)
    same_chunk = (ti >= tj) & (ti // C == tj // C)
    tri = same_chunk.astype(BF16)
    cl = sum(jnp.dot(tri, piece, preferred_element_type=F32) for piece in reversed(_bf16_pieces(logd, 3)))
    cl_end = per_block(jnp.concatenate([cl[i * C + C - 1:i * C + C, :] for i in range(B * NC)], axis=0), C)
    a_hat = -kk * jnp.exp(cl - logd)
    r_hat = r * jnp.exp(cl)
    inv_gam = jnp.exp(-cl)
    b_til = lr_kk * inv_gam
    k_til = knew * inv_gam
    to_end = jnp.exp(cl_end - cl)
    b_end = lr_kk * to_end
    k_end = knew * to_end
    gam_end = jnp.exp(cl_end)

    gt = lax.broadcasted_iota(jnp.int32, (C, GW), 0)
    gc = lax.broadcasted_iota(jnp.int32, (C, GW), 1) % N
    strict = gt > gc
    incl = gt >= gc
    eye = (gt == gc).astype(F32)
    bi = lax.broadcasted_iota(jnp.int32, (HG * C, GW), 0) // C
    bj = lax.broadcasted_iota(jnp.int32, (HG * C, GW), 1) // N
    same_head = bi == bj

    def block_diag(y):
        yb = y.astype(BF16)
        return jnp.where(same_head, jnp.concatenate([yb] * HG, axis=0), jnp.zeros((), BF16))

    def bd_dot(x, y_bd):
        return jnp.dot(x.astype(BF16), y_bd, preferred_element_type=F32)

    def bd_dot_nt(x, y_bd):
        return lax.dot_general(x.astype(BF16), y_bd, (((1,), (1,)), ((), ())), preferred_element_type=F32)

    n_groups = H // HG
    units = [(b, c, gi) for b in range(B) for c in range(NC) for gi in range(n_groups)]
    n_units = range(len(units))
    cut = lambda x, b, c, gi: x[(b * NC + c) * C:(b * NC + c + 1) * C, gi * GW:(gi + 1) * GW]
    v_u = [cut(v, *un) for un in units]
    v_bd = [block_diag(v_u[i]) for i in n_units]
    ar = [jnp.concatenate([cut(a_hat, *un), cut(r_hat, *un)], axis=0) for un in units]
    mb = [bd_dot_nt(ar[i], block_diag(cut(b_til, *units[i]))) for i in n_units]
    mk = [bd_dot_nt(ar[i], block_diag(cut(k_til, *units[i]))) for i in n_units]
    n_ab = [jnp.where(strict, mb[i][:C], 0.0) for i in n_units]
    m_rb = [jnp.where(incl, mb[i][C:], 0.0) for i in n_units]
    l_ak = [jnp.where(strict, mk[i][:C], 0.0) for i in n_units]
    m_rk = [jnp.where(incl, mk[i][C:], 0.0) for i in n_units]

    pw = list(n_ab)
    pw_bd = [block_diag(pw[i]) for i in n_units]
    tinv = [eye + n_ab[i] for i in n_units]
    step = 2
    while step < C:
        pw = [bd_dot(pw[i], pw_bd[i]) for i in n_units]
        pw_bd = [block_diag(pw[i]) for i in n_units]
        tinv = [tinv[i] + bd_dot(tinv[i], pw_bd[i]) for i in n_units]
        step *= 2
    lv = [bd_dot(l_ak[i], v_bd[i]) for i in n_units]

    state = {(b, gi): state_ref[b * n_groups + gi] for b in range(B) for gi in range(n_groups)}
    outs = {}
    for c in range(NC):
        live = [i for i in n_units if units[i][1] == c]
        s0 = {i: state[(units[i][0], units[i][2])] for i in live}
        ars = {i: bd_dot_nt(ar[i], block_diag(s0[i])) for i in live}
        u = {i: bd_dot(tinv[i], block_diag(ars[i][:C] + lv[i])) for i in live}
        for i in live:
            outs[units[i]] = ars[i][C:] + bd_dot(m_rb[i], block_diag(u[i])) + bd_dot(m_rk[i], v_bd[i])
        for i in live:
            b, _, gi = units[i]
            uv = jnp.concatenate([u[i], v_u[i]], axis=0)
            bk_end = jnp.concatenate([cut(b_end, *units[i]), cut(k_end, *units[i])], axis=0)
            cross = jnp.where(same_head, _bdot_tn(uv, bk_end), 0.0)
            upd = cross[0:N]
            for h in range(1, HG):
                upd = upd + cross[h * N:(h + 1) * N]
            state[(b, gi)] = s0[i] * cut(gam_end, *units[i])[0:1] + upd
    for (b, gi), s_new in state.items():
        state_ref[b * n_groups + gi] = s_new

    o = jnp.concatenate([jnp.concatenate([outs[(b, c, gi)] for gi in range(n_groups)], axis=-1)
                         for b in range(B) for c in range(NC)], axis=0)
    mean = head_sum(o) * (1.0 / N)
    var = head_sum(jnp.square(o - mean)) * (1.0 / N)
    o = (o - mean) * lax.rsqrt(var + GN_EPS) * lnw_ref[...] + lnb_ref[...]
    bonus = head_sum(r * knew * rk_ref[...]) * v
    o_ref[...] = ((o + bonus) * g).reshape(o_ref.shape)


def _rwkv_time_mix(p_a, mu, w0, w2, a0, a2, g2, k_k, k_a, r_k, lnx_w, lnx_b):
    B, S, _ = p_a.shape
    L = RWKV_CHUNK * RWKV_STEP_CHUNKS
    row = lambda t: t.reshape(1, -1)
    full = lambda shape: pl.BlockSpec(shape, lambda s: (0,) * len(shape))
    n_units = B * RWKV_HEADS // RWKV_HEAD_GROUP
    return pl.pallas_call(
        _rwkv_kernel,
        grid=(S // L,),
        in_specs=[pl.BlockSpec((B, L, RWKV_IN_W), lambda s: (0, s, 0)),
                  full((1, RWKV_IN_W)), full((1, RWKV_WIDTH)), full((W_LORA, RWKV_WIDTH)),
                  full((1, RWKV_WIDTH)), full((A_LORA, RWKV_WIDTH)), full((G_LORA, RWKV_WIDTH)),
                  full((1, RWKV_WIDTH)), full((1, RWKV_WIDTH)), full((1, RWKV_WIDTH)),
                  full((1, RWKV_WIDTH)), full((1, RWKV_WIDTH))],
        out_specs=pl.BlockSpec((B, L, RWKV_WIDTH), lambda s: (0, s, 0)),
        out_shape=jax.ShapeDtypeStruct((B, S, RWKV_WIDTH), F32),
        scratch_shapes=[pltpu.VMEM((B, 1, RWKV_IN_W), F32),
                        pltpu.VMEM((n_units, HEAD_DIM, RWKV_HEAD_GROUP * HEAD_DIM), F32)],
        compiler_params=pltpu.CompilerParams(
            dimension_semantics=("arbitrary",), vmem_limit_bytes=VMEM_LIMIT),
        name="rwkv7_chunked",
    )(p_a, row(mu), row(w0), w2, row(a0), a2, g2, row(k_k), row(k_a), row(r_k), row(lnx_w), row(lnx_b))


NSA_KV_TILE = 1024
SEL_KEY_TILE = 1024
NSA_QUERY_TILE = 512
SEL_LANES = 128


def _rope_tables(pos, reps):
    half = HEAD_DIM // 2
    inv = ROPE_THETA ** (-jnp.arange(half, dtype=F32) / half)
    ang = pos.astype(F32)[:, None] * inv
    cos, sin = jnp.cos(ang), jnp.sin(ang)
    cosf = jnp.concatenate([cos, cos], -1)
    sinf = jnp.concatenate([-sin, sin], -1)
    return jnp.tile(cosf, (1, reps)), jnp.tile(sinf, (1, reps))


def _rope(x, cosf, sinf):
    width = x.shape[-1]
    lane = lax.broadcasted_iota(jnp.int32, x.shape, 1)
    first_half = (lane % HEAD_DIM) < HEAD_DIM // 2
    rot = jnp.where(first_half, pltpu.roll(x, width - HEAD_DIM // 2, axis=1),
                    pltpu.roll(x, HEAD_DIM // 2, axis=1))
    return x * cosf + rot * sinf


def _kv_layout_kernel(p_ref, cos_ref, sin_ref, kc_ref, vc_ref, ks_ref, vs_ref, kw_ref, vw_ref):
    ts = p_ref.shape[0]
    for i, o_ref in ((0, kc_ref), (1, vc_ref), (2, ks_ref), (4, kw_ref)):
        t = p_ref[:, i * NSA_KV_WIDTH:(i + 1) * NSA_KV_WIDTH]
        if i >= 2:
            t = _rope(t, cos_ref[...], sin_ref[...])
        for g in range(NSA_GROUPS):
            o_ref[g] = t[:, g * HEAD_DIM:(g + 1) * HEAD_DIM].astype(o_ref.dtype)
    pad_row = lax.broadcasted_iota(jnp.int32, (VT_ROWS - HEAD_DIM, ts), 0)
    ones_row = jnp.where(pad_row == 0, 1.0, 0.0)
    for i, o_ref in ((3, vs_ref), (5, vw_ref)):
        t_t = p_ref[:, i * NSA_KV_WIDTH:(i + 1) * NSA_KV_WIDTH].T
        for g in range(NSA_GROUPS):
            o_ref[g] = jnp.concatenate([t_t[g * HEAD_DIM:(g + 1) * HEAD_DIM], ones_row],
                                       axis=0).astype(o_ref.dtype)


def _kv_layout(p_b, cos2, sin2):
    B, S, _ = p_b.shape
    ts = min(NSA_KV_TILE, S)
    out_spec = pl.BlockSpec((None, NSA_GROUPS, ts, HEAD_DIM), lambda b, s: (b, 0, s, 0))
    vt_spec = pl.BlockSpec((None, NSA_GROUPS, VT_ROWS, ts), lambda b, s: (b, 0, 0, s))
    shp = lambda dt: jax.ShapeDtypeStruct((B, NSA_GROUPS, S, HEAD_DIM), dt)
    vt_shp = jax.ShapeDtypeStruct((B, NSA_GROUPS, VT_ROWS, S), BF16)
    return pl.pallas_call(
        _kv_layout_kernel,
        grid=(B, S // ts),
        in_specs=[pl.BlockSpec((None, ts, 6 * NSA_KV_WIDTH), lambda b, s: (b, s, 0)),
                  pl.BlockSpec((ts, NSA_KV_WIDTH), lambda b, s: (s, 0)),
                  pl.BlockSpec((ts, NSA_KV_WIDTH), lambda b, s: (s, 0))],
        out_specs=[out_spec, out_spec, out_spec, vt_spec, out_spec, vt_spec],
        out_shape=[shp(F32), shp(F32), shp(BF16), vt_shp, shp(BF16), vt_shp],
        compiler_params=pltpu.CompilerParams(
            dimension_semantics=("parallel", "parallel"), vmem_limit_bytes=VMEM_LIMIT),
        name="nsa_kv_layout",
    )(p_b, cos2, sin2)


def _compress_kernel(subk_ref, subv_ref, pek_ref, w1k_ref, w2k_ref, pev_ref, w1v_ref, w2v_ref,
                     cos_ref, sin_ref, kc_ref, vc_ref):
    n_sub = subk_ref.shape[0]
    half = CMP_STRIDE * HEAD_DIM

    def mlp(sub_ref, pe_ref, w1_ref, w2_ref):
        sub = sub_ref[...]
        top = _bdot(sub, w1_ref[:half, :])
        bot = _bdot(sub, w1_ref[half:, :])
        bias = _bdot(jnp.broadcast_to(pe_ref[...], (8, 2 * half)), w1_ref[...])[0:1, :]
        h = top + pltpu.roll(bot, n_sub - 1, axis=0) + bias
        return _bdot(jax.nn.gelu(h), w2_ref[...])

    kc = mlp(subk_ref, pek_ref, w1k_ref, w2k_ref)
    rot = jnp.concatenate([kc[:, HEAD_DIM // 2:], kc[:, :HEAD_DIM // 2]], axis=-1)
    kc_ref[...] = (kc * cos_ref[...] + rot * sin_ref[...]).astype(kc_ref.dtype)
    vc_ref[...] = mlp(subv_ref, pev_ref, w1v_ref, w2v_ref).astype(vc_ref.dtype)


def _compress(subk, subv, pe_k, w1_k, w2_k, pe_v, w1_v, w2_v, cos_c, sin_c):
    B, G, n_sub, width = subk.shape
    sub_spec = pl.BlockSpec((None, None, n_sub, width), lambda b, g: (b, g, 0, 0))
    full = lambda a: pl.BlockSpec(a.shape, lambda b, g: (0,) * a.ndim)
    out_spec = pl.BlockSpec((None, None, n_sub, HEAD_DIM), lambda b, g: (b, g, 0, 0))
    pe_k, pe_v = pe_k.reshape(1, -1), pe_v.reshape(1, -1)
    args = (pe_k, w1_k, w2_k, pe_v, w1_v, w2_v, cos_c, sin_c)
    return pl.pallas_call(
        _compress_kernel,
        grid=(B, G),
        in_specs=[sub_spec, sub_spec] + [full(a) for a in args],
        out_specs=[out_spec, out_spec],
        out_shape=[jax.ShapeDtypeStruct((B, G, n_sub, HEAD_DIM), BF16)] * 2,
        compiler_params=pltpu.CompilerParams(
            dimension_semantics=("parallel", "parallel"), vmem_limit_bytes=VMEM_LIMIT),
        name="nsa_compress",
    )(subk, subv, *args)


MAX_FLOOR = -1e20
MASK_BIG = 2.0 ** 100
LOG2_E = 1.4426950408889634
VT_ROWS = 80


def _nsa_kernel(q_ref, gate_ref, cos_ref, sin_ref, kc_ref, vc_ref, ks_ref, vst_ref, kw_ref, vwt_ref,
                mselt_ref, o_ref, blockbias_ref, *, n_pick):
    QB, HP, D = NSA_QUERY_TILE, NSA_HPG, HEAD_DIM
    qb = pl.program_id(2)
    n_cmp = kc_ref.shape[0]
    lanes4 = lambda x: jnp.concatenate([x] * HP, axis=1)

    heads = []
    for n in range(HP):
        qh = q_ref[n * D:(n + 1) * D, :]
        rot = jnp.concatenate([qh[D // 2:], qh[:D // 2]], axis=0)
        heads.append(qh * cos_ref[...] + rot * sin_ref[...])
    q4 = (jnp.concatenate(heads, axis=1) * (D ** -0.5 * LOG2_E)).astype(BF16)
    t_row = qb * QB + lax.broadcasted_iota(jnp.int32, (1, QB), 1)

    def softmax_cols(s_t, bias_t):
        sm = s_t + lanes4(bias_t)
        m = jnp.maximum(jnp.max(sm, axis=0, keepdims=True), MAX_FLOOR)
        return jnp.exp2(sm - m)

    cmp_end = lax.broadcasted_iota(jnp.int32, (n_cmp, 1), 0) * CMP_STRIDE + (CMP_BLOCK - 1)
    e_c = softmax_cols(jnp.dot(kc_ref[...], q4, preferred_element_type=F32),
                       jnp.where(cmp_end <= t_row, 0.0, -MASK_BIG))
    den_c = jnp.sum(e_c, axis=0, keepdims=True)
    p_c = e_c * (1.0 / jnp.where(den_c > 0.0, den_c, 1.0))
    o_c = _bdot_tn(vc_ref[...], p_c)
    p_sum = p_c[:, 0:QB]
    for n in range(1, HP):
        p_sum = p_sum + p_c[:, n * QB:(n + 1) * QB]
    p_hi = p_sum.astype(BF16)
    p_lo = (p_sum - p_hi.astype(F32)).astype(BF16)
    imp_t = (jnp.dot(mselt_ref[...], p_hi, preferred_element_type=F32)
             + jnp.dot(mselt_ref[...], p_lo, preferred_element_type=F32))

    j = lax.broadcasted_iota(jnp.int32, (SEL_LANES, QB), 0)
    cur = t_row // SEL_BLOCK
    valid = j * SEL_BLOCK <= t_row
    forced = (j == 0) | (j == cur) | (j == cur - 1)
    score = jnp.where(valid, imp_t + jnp.where(forced, FORCE_BONUS, 0.0), -1.0)
    for _ in range(n_pick):
        m = jnp.max(score, axis=0, keepdims=True)
        idx = jnp.min(jnp.where(score == m, j, SEL_LANES), axis=0, keepdims=True)
        score = jnp.where(j == idx, -2.0, score)
    blockbias_ref[...] = jnp.where((score == -2.0) & valid, 0.0, -MASK_BIG)

    KT = SEL_KEY_TILE
    blocks_per_tile = KT // SEL_BLOCK
    n_tiles = (qb * QB + QB + KT - 1) // KT

    def sel_step(kt, carry, causal):
        m_i, acc = carry
        start = pl.multiple_of(kt * KT, KT)
        s_t = jnp.dot(ks_ref[pl.ds(start, KT), :], q4, preferred_element_type=F32)
        bias = jnp.concatenate(
            [jnp.broadcast_to(blockbias_ref[pl.ds(kt * blocks_per_tile + jb, 1), :], (SEL_BLOCK, QB))
             for jb in range(blocks_per_tile)], axis=0)
        if causal:
            kpos = start + lax.broadcasted_iota(jnp.int32, (KT, 1), 0)
            bias = jnp.where(kpos <= t_row, bias, -MASK_BIG)
        sm = s_t + lanes4(bias)
        m_new = jnp.maximum(m_i, jnp.max(sm, axis=0, keepdims=True))
        e = jnp.exp2(sm - m_new).astype(BF16)
        acc_new = jnp.exp2(m_i - m_new) * acc + jnp.dot(vst_ref[:, pl.ds(start, KT)], e,
                                                        preferred_element_type=F32)
        return m_new, acc_new

    init = (jnp.full((1, HP * QB), MAX_FLOOR, F32), jnp.zeros((VT_ROWS, HP * QB), F32))
    carry = lax.fori_loop(0, n_tiles - 1, lambda kt, c: sel_step(kt, c, False), init)
    _, acc_s = sel_step(n_tiles - 1, carry, True)
    den_s = acc_s[D:D + 1]
    o_s = acc_s[:D] * (1.0 / jnp.where(den_s > 0.0, den_s, 1.0))

    span = WINDOW + QB
    w_start = pl.multiple_of(jnp.maximum(qb * QB - WINDOW, 0), QB)
    dist = t_row - (w_start + lax.broadcasted_iota(jnp.int32, (span, 1), 0))
    e_w = softmax_cols(jnp.dot(kw_ref[pl.ds(w_start, span), :], q4, preferred_element_type=F32),
                       jnp.where((dist >= 0) & (dist < WINDOW), 0.0, -MASK_BIG))
    acc_w = jnp.dot(vwt_ref[:, pl.ds(w_start, span)], e_w.astype(BF16), preferred_element_type=F32)
    den_w = acc_w[D:D + 1]
    o_w = acc_w[:D] * (1.0 / jnp.where(den_w > 0.0, den_w, 1.0))

    gates = _sigmoid(gate_ref[...])
    gate_row = lambda br: jnp.concatenate([gates[3 * n + br:3 * n + br + 1, :] for n in range(HP)], axis=1)
    o_t = gate_row(0) * o_c + gate_row(1) * o_s + gate_row(2) * o_w
    for n in range(HP):
        o_ref[:, n * D:(n + 1) * D] = o_t[:, n * QB:(n + 1) * QB].T


def _cmp_to_sel_matrix(n_cmp_rows, n_sel):
    ratio = SEL_BLOCK // CMP_STRIDE
    ci = np.arange(n_cmp_rows)[:, None]
    sj = np.arange(SEL_LANES)[None, :]
    m = sum(((ci + n) // ratio == sj).astype(np.float32) for n in range(CMP_BLOCK // CMP_STRIDE))
    m = m * (sj < n_sel) * (ci < n_cmp_rows - 1)
    return jnp.asarray(m.T, BF16)


def _nsa_attention(qg_t, kc, vc, ks, vst, kw, vwt, cos_t, sin_t):
    B, _, S = qg_t.shape
    n_sub = kc.shape[2]
    n_sel = S // SEL_BLOCK
    gw = NSA_HPG * HEAD_DIM
    gate_row0 = NSA_WIDTH // 128
    msel_t = _cmp_to_sel_matrix(n_sub, n_sel)
    at_bg = lambda shape: pl.BlockSpec((None, None) + shape, lambda b, g, i: (b, g, 0, 0))
    const = lambda a: pl.BlockSpec(a.shape, lambda b, g, i: (0, 0))
    return pl.pallas_call(
        functools.partial(_nsa_kernel, n_pick=min(N_SELECT, n_sel)),
        grid=(B, NSA_GROUPS, S // NSA_QUERY_TILE),
        in_specs=[pl.BlockSpec((None, gw, NSA_QUERY_TILE), lambda b, g, i: (b, g, i)),
                  pl.BlockSpec((None, 128, NSA_QUERY_TILE), lambda b, g, i: (b, gate_row0 + g, i)),
                  pl.BlockSpec((HEAD_DIM, NSA_QUERY_TILE), lambda b, g, i: (0, i)),
                  pl.BlockSpec((HEAD_DIM, NSA_QUERY_TILE), lambda b, g, i: (0, i)),
                  at_bg((n_sub, HEAD_DIM)), at_bg((n_sub, HEAD_DIM)),
                  at_bg((S, HEAD_DIM)), at_bg((VT_ROWS, S)), at_bg((S, HEAD_DIM)), at_bg((VT_ROWS, S)),
                  const(msel_t)],
        out_specs=pl.BlockSpec((None, NSA_QUERY_TILE, gw), lambda b, g, i: (b, i, g)),
        out_shape=jax.ShapeDtypeStruct((B, S, NSA_WIDTH), F32),
        scratch_shapes=[pltpu.VMEM((SEL_LANES, NSA_QUERY_TILE), F32)],
        compiler_params=pltpu.CompilerParams(
            dimension_semantics=("parallel", "parallel", "arbitrary"), vmem_limit_bytes=VMEM_LIMIT),
        name="nsa_attention",
    )(qg_t, qg_t, cos_t, sin_t, kc, vc, ks, vst, kw, vwt, msel_t)


def _nsa_branch(p_kv, qg_t, cmp_pe_k, cmp_w1_k, cmp_w2_k, cmp_pe_v, cmp_w1_v, cmp_w2_v):
    B, S, _ = p_kv.shape
    pos = jnp.arange(S)
    cos2, sin2 = _rope_tables(pos, NSA_GROUPS)
    kc_raw, vc_raw, ks, vst, kw, vwt = _kv_layout(p_kv, cos2, sin2)
    n_sub = S // CMP_STRIDE
    sub = lambda t: t.reshape(B, NSA_GROUPS, n_sub, CMP_STRIDE * HEAD_DIM)
    cos_c, sin_c = _rope_tables(jnp.arange(n_sub) * CMP_STRIDE + CMP_BLOCK - 1, 1)
    kc, vc = _compress(sub(kc_raw), sub(vc_raw), cmp_pe_k, cmp_w1_k, cmp_w2_k,
                       cmp_pe_v, cmp_w1_v, cmp_w2_v, cos_c, sin_c)
    cos_q, sin_q = _rope_tables(pos, 1)
    return _nsa_attention(qg_t, kc, vc, ks, vst, kw, vwt, cos_q.T, sin_q.T)


def _nsa_weight_columns(w_nsa):
    K = w_nsa.shape[0]
    q = w_nsa[:, :NSA_WIDTH]
    kv = w_nsa[:, NSA_WIDTH:NSA_WIDTH + 6 * NSA_KV_WIDTH]
    gates = w_nsa[:, NSA_WIDTH + 6 * NSA_KV_WIDTH:]
    per_group = NSA_HPG * 3
    gate_blocks = [jnp.pad(gates[:, g * per_group:(g + 1) * per_group], ((0, 0), (0, 128 - per_group)))
                   for g in range(NSA_GROUPS)]
    return jnp.concatenate([kv, q] + gate_blocks, axis=1)


def _layer_norm(h, g, b):
    mu = jnp.mean(h, axis=-1, keepdims=True)
    var = jnp.mean(jnp.square(h - mu), axis=-1, keepdims=True)
    return (h - mu) * lax.rsqrt(var + LN_EPS) * g + b


def _pack_bf16_halves(x):
    n = x.shape[-1] // 2
    bits = lax.bitcast_convert_type(x.astype(BF16).astype(F32), jnp.uint32)
    return (bits[:, n:] & jnp.uint32(0xFFFF0000)) | (bits[:, :n] >> 16)


def _unpack_bf16_halves(u):
    left = lax.bitcast_convert_type(u << 16, F32)
    right = lax.bitcast_convert_type(u & jnp.uint32(0xFFFF0000), F32)
    return left, right


def _mixer_out_kernel(x_ref, ya_ref, yb_ref, pg_ref, wa_ref, wb_ref, wo_ref, g_ref, b_ref, o_ref, op_ref,
                      *, alpha):
    d = x_ref.shape[-1]
    gate_a = _sigmoid(pg_ref[:, :d].astype(F32))
    gate_b = _sigmoid(pg_ref[:, d:].astype(F32))
    mixed = gate_a * _bdot(ya_ref[...], wa_ref[...]) + gate_b * _bdot(yb_ref[...], wb_ref[...])
    h = alpha * x_ref[...] + _bdot(mixed, wo_ref[...])
    out = _layer_norm(h, g_ref[...], b_ref[...])
    o_ref[...] = out
    op_ref[...] = _pack_bf16_halves(out)


def _mixer_out(xf, ya, yb, p_g, wa, wb, wo, ln_g, ln_b, alpha, tm=512):
    T, D = xf.shape
    rows = lambda w: pl.BlockSpec((tm, w), lambda i: (i, 0))
    full = lambda a: pl.BlockSpec(a.shape, lambda i: (0,) * a.ndim)
    ln_g, ln_b = ln_g.reshape(1, D), ln_b.reshape(1, D)
    return pl.pallas_call(
        functools.partial(_mixer_out_kernel, alpha=alpha),
        grid=(T // tm,),
        in_specs=[rows(D), rows(ya.shape[1]), rows(yb.shape[1]), rows(2 * D),
                  full(wa), full(wb), full(wo), full(ln_g), full(ln_b)],
        out_specs=[rows(D), rows(D // 2)],
        out_shape=[jax.ShapeDtypeStruct((T, D), F32), jax.ShapeDtypeStruct((T, D // 2), jnp.uint32)],
        compiler_params=pltpu.CompilerParams(
            dimension_semantics=("parallel",), vmem_limit_bytes=VMEM_LIMIT),
        name="mixer_out_ln",
    )(xf, ya, yb, p_g, wa, wb, wo, ln_g, ln_b)


ROUTER_TILE = 512
EXPERT_ROWS = 512
MOE_COMBINE_PARTS = 2
SC_TOKEN_CHUNK = 64
SC_ROW_CHUNK = 64
PICK_LANES = 128
LOWEST = -3.0e38


def _router_kernel(x_ref, rwt_ref, bias_ref, eidx_ref, wts_ref, pos_ref, cnt_ref, carry_ref):
    tm, E = x_ref.shape[0], rwt_ref.shape[0]
    per_group = E // N_GROUPS
    reps = tm // PICK_LANES

    @pl.when(pl.program_id(0) == 0)
    def _():
        carry_ref[...] = jnp.zeros_like(carry_ref)

    scores = _sigmoid(_dot3(rwt_ref[...], x_ref[...], (((1,), (1,)), ((), ()))))
    choice = scores + jnp.concatenate([bias_ref[...]] * reps, axis=1)
    row = lax.broadcasted_iota(jnp.int32, (E, tm), 0)

    def first_max(vals, rows):
        m = jnp.max(vals, axis=0, keepdims=True)
        return m, jnp.min(jnp.where(vals == m, rows, E), axis=0, keepdims=True)

    group_score = []
    for g in range(N_GROUPS):
        rows = slice(g * per_group, (g + 1) * per_group)
        group_row = g * per_group + lax.broadcasted_iota(jnp.int32, (per_group, tm), 0)
        m1, i1 = first_max(choice[rows], group_row)
        m2 = jnp.max(jnp.where(group_row == i1, LOWEST, choice[rows]), axis=0, keepdims=True)
        group_score.append(m1 + m2)
    masked = []
    for g in range(N_GROUPS):
        rank = jnp.zeros((1, tm), jnp.int32)
        for o in range(N_GROUPS):
            if o != g:
                ahead = (group_score[o] > group_score[g]) if o > g else (group_score[o] >= group_score[g])
                rank = rank + ahead.astype(jnp.int32)
        masked.append(jnp.where(rank < TOPK_GROUPS, choice[g * per_group:(g + 1) * per_group], NEG_INF))

    cur = jnp.concatenate(masked, axis=0)
    picks = []
    for _ in range(TOP_K):
        _, idx = first_max(cur, row)
        picks.append(idx)
        cur = jnp.where(row == idx, LOWEST, cur)
    sel = jnp.where(cur == LOWEST, 1.0, 0.0)
    gate = scores * sel
    gate = gate * (ROUTED_SCALE / jnp.sum(gate, axis=0, keepdims=True))

    ti = lax.broadcasted_iota(jnp.int32, (tm, tm), 0)
    tj = lax.broadcasted_iota(jnp.int32, (tm, tm), 1)
    sel_b = sel.astype(BF16)
    before = jnp.dot(sel_b, (ti < tj).astype(BF16), preferred_element_type=F32)
    queue_pos = before + jnp.concatenate([carry_ref[...]] * reps, axis=1)
    carry_ref[...] = carry_ref[...] + jnp.dot(sel_b, jnp.ones((tm, PICK_LANES), BF16),
                                              preferred_element_type=F32)
    cnt_ref[...] = carry_ref[...]

    at_pick = lambda vals, idx: jnp.sum(jnp.where(row == idx, vals, 0.0), axis=0, keepdims=True)
    eidx_ref[...] = jnp.concatenate(picks, axis=0)
    wts_ref[...] = jnp.concatenate([at_pick(gate, idx) for idx in picks], axis=0)
    pos_ref[...] = jnp.concatenate([at_pick(queue_pos, idx) for idx in picks], axis=0).astype(jnp.int32)


def _router(xf, router_w, router_bias):
    T, D = xf.shape
    E = router_w.shape[1]
    tm = ROUTER_TILE
    picks = lambda dt: jax.ShapeDtypeStruct((TOP_K, T), dt)
    pick_spec = pl.BlockSpec((TOP_K, tm), lambda i: (0, i))
    lanes = lambda v: jnp.broadcast_to(v.reshape(E, 1), (E, PICK_LANES))
    return pl.pallas_call(
        _router_kernel,
        grid=(T // tm,),
        in_specs=[pl.BlockSpec((tm, D), lambda i: (i, 0)),
                  pl.BlockSpec((E, D), lambda i: (0, 0)),
                  pl.BlockSpec((E, PICK_LANES), lambda i: (0, 0))],
        out_specs=[pick_spec, pick_spec, pick_spec, pl.BlockSpec((E, PICK_LANES), lambda i: (0, 0))],
        out_shape=[picks(jnp.int32), picks(F32), picks(jnp.int32),
                   jax.ShapeDtypeStruct((E, PICK_LANES), F32)],
        scratch_shapes=[pltpu.VMEM((E, PICK_LANES), F32)],
        compiler_params=pltpu.CompilerParams(
            dimension_semantics=("arbitrary",), vmem_limit_bytes=VMEM_LIMIT),
        name="moe_router",
    )(xf, router_w.T, lanes(router_bias))


def _dest_kernel(eidx_ref, pos_ref, start_ref, dest_ref):
    E = start_ref.shape[0]
    tm = eidx_ref.shape[1]
    row = lax.broadcasted_iota(jnp.int32, (E, tm), 0)
    start = jnp.concatenate([start_ref[...]] * (tm // PICK_LANES), axis=1)
    eidx = eidx_ref[...]
    base = [jnp.sum(jnp.where(row == eidx[kk:kk + 1, :], start, 0), axis=0, keepdims=True)
            for kk in range(TOP_K)]
    dest_ref[...] = jnp.concatenate(base, axis=0) + pos_ref[...]


def _dest_rows(eidx_t, pos_t, pad_start):
    T = eidx_t.shape[1]
    E = pad_start.shape[0]
    tm = ROUTER_TILE
    pick_spec = pl.BlockSpec((TOP_K, tm), lambda i: (0, i))
    return pl.pallas_call(
        _dest_kernel,
        grid=(T // tm,),
        in_specs=[pick_spec, pick_spec, pl.BlockSpec((E, PICK_LANES), lambda i: (0, 0))],
        out_specs=pick_spec,
        out_shape=jax.ShapeDtypeStruct((TOP_K, T), jnp.int32),
        compiler_params=pltpu.CompilerParams(
            dimension_semantics=("parallel",), vmem_limit_bytes=VMEM_LIMIT),
        name="moe_dest_rows",
    )(eidx_t, pos_t, jnp.broadcast_to(pad_start.reshape(E, 1), (E, PICK_LANES)))


def _sc_mesh():
    return plsc.VectorSubcoreMesh(core_axis_name="c", subcore_axis_name="s")


def _sc_scatter_rows(x, dest_t, n_rows):
    T, D = x.shape
    K = dest_t.shape[0]
    mesh = _sc_mesh()
    nc, nw = mesh.num_cores, mesh.num_cores * mesh.num_subcores
    per_w = T // nw
    chunk = min(SC_TOKEN_CHUNK, per_w)
    n_chunks = per_w // chunk
    idx = dest_t.reshape(K, nw, n_chunks, chunk).transpose(1, 2, 0, 3).reshape(nw, n_chunks * K, chunk)

    assert n_chunks % 2 == 0

    @functools.partial(
        pl.kernel, mesh=mesh,
        out_type=jax.ShapeDtypeStruct((n_rows, D), x.dtype),
        scratch_types=[pltpu.VMEM((n_chunks * K, chunk), jnp.int32),
                       pltpu.VMEM((2, chunk, D), x.dtype),
                       pltpu.SemaphoreType.DMA((2,)), pltpu.SemaphoreType.DMA((2,))],
    )
    def scatter(x_hbm, idx_hbm, out_hbm, idx_v, rows_v, load_sem, send_sem):
        wid = lax.axis_index("s") * nc + lax.axis_index("c")
        pltpu.sync_copy(idx_hbm.at[wid], idx_v)

        def load(j, b):
            return pltpu.make_async_copy(x_hbm.at[pl.ds(wid * per_w + j * chunk, chunk)], rows_v.at[b],
                                         load_sem.at[b])

        def sends(j, b):
            return [pltpu.make_async_copy(rows_v.at[b], out_hbm.at[idx_v.at[j * K + kk]], send_sem.at[b])
                    for kk in range(K)]

        load(0, 0).start()

        @pl.loop(0, n_chunks, step=2)
        def _(j0):
            for b in range(2):
                j = j0 + b
                load(j, b).wait()

                @pl.when(j >= 1)
                def _():
                    for c in sends(j - 1, 1 - b):
                        c.wait()

                @pl.when(j + 1 < n_chunks)
                def _():
                    load(j + 1, 1 - b).start()

                for c in sends(j, b):
                    c.start()

        for c in sends(n_chunks - 1, (n_chunks - 1) % 2):
            c.wait()

    return scatter(x, idx)


def _sc_gather_rows(src, idx):
    M = idx.shape[0]
    D = src.shape[1]
    mesh = _sc_mesh()
    nc, nw = mesh.num_cores, mesh.num_cores * mesh.num_subcores
    per_w = M // nw
    chunk = min(SC_ROW_CHUNK, per_w)
    n_chunks = per_w // chunk

    assert n_chunks % 2 == 0

    @functools.partial(
        pl.kernel, mesh=mesh,
        out_type=jax.ShapeDtypeStruct((M, D), src.dtype),
        scratch_types=[pltpu.VMEM((n_chunks, chunk), jnp.int32),
                       pltpu.VMEM((2, chunk, D), src.dtype),
                       pltpu.SemaphoreType.DMA((2,)), pltpu.SemaphoreType.DMA((2,))],
    )
    def gather(src_hbm, idx_hbm, out_hbm, idx_v, rows_v, fetch_sem, store_sem):
        wid = lax.axis_index("s") * nc + lax.axis_index("c")
        pltpu.sync_copy(idx_hbm.at[wid], idx_v)

        def fetch(j, b):
            return pltpu.make_async_copy(src_hbm.at[idx_v.at[j]], rows_v.at[b], fetch_sem.at[b])

        def store(j, b):
            return pltpu.make_async_copy(rows_v.at[b], out_hbm.at[pl.ds(wid * per_w + j * chunk, chunk)],
                                         store_sem.at[b])

        fetch(0, 0).start()

        @pl.loop(0, n_chunks, step=2)
        def _(j0):
            for b in range(2):
                j = j0 + b
                fetch(j, b).wait()

                @pl.when(j >= 1)
                def _():
                    store(j - 1, 1 - b).wait()

                @pl.when(j + 1 < n_chunks)
                def _():
                    fetch(j + 1, 1 - b).start()

                store(j, b).start()

        store(n_chunks - 1, (n_chunks - 1) % 2).wait()

    return gather(src, idx.reshape(nw, n_chunks, chunk))


def _expert_kernel(distinct_e_ref, blk_ord_ref, blk_new_ref, blk_rows_ref, n_used_ref, n_distinct_ref,
                   x_ref, wgu_hbm, wd_hbm, o_ref, wgu_buf, wd_buf, wgu_bf, wd_bf, sem):
    i = pl.program_id(0)
    live = i < n_used_ref[0]
    ordinal = blk_ord_ref[i]
    slot = ordinal % 2

    def weight_copies(k, s):
        e = distinct_e_ref[k]
        return (pltpu.make_async_copy(wgu_hbm.at[e], wgu_buf.at[s], sem.at[0, s]),
                pltpu.make_async_copy(wd_hbm.at[e], wd_buf.at[s], sem.at[1, s]))

    @pl.when(i == 0)
    def _():
        for c in weight_copies(0, 0):
            c.start()

    @pl.when(live & (blk_new_ref[i] == 1))
    def _():
        for c in weight_copies(ordinal, slot):
            c.wait()

        @pl.when(ordinal + 1 < n_distinct_ref[0])
        def _():
            for c in weight_copies(ordinal + 1, 1 - slot):
                c.start()

        wgu_bf[...] = wgu_buf[slot].astype(BF16)
        wd_bf[...] = wd_buf[slot].astype(BF16)

    @pl.when(live)
    def _():
        hidden = wd_bf.shape[0]
        half = x_ref.shape[1]
        row = lax.broadcasted_iota(jnp.int32, x_ref.shape, 0)
        left, right = _unpack_bf16_halves(x_ref[...])
        real = row < blk_rows_ref[i]
        left = jnp.where(real, left, 0.0).astype(BF16)
        right = jnp.where(real, right, 0.0).astype(BF16)
        h = (jnp.dot(left, wgu_bf[:half, :], preferred_element_type=F32)
             + jnp.dot(right, wgu_bf[half:, :], preferred_element_type=F32))
        gate, up = h[:, :hidden], h[:, hidden:]
        act = (gate * _sigmoid(gate) * up).astype(BF16)
        o_ref[...] = _pack_bf16_halves(jnp.dot(act, wd_bf[...], preferred_element_type=F32))

    @pl.when(jnp.logical_not(live))
    def _():
        o_ref[...] = jnp.zeros_like(o_ref)


def _expert_ffn(xs, blk_e, blk_rows, n_used, w_gu, w_down):
    n_rows, half = xs.shape
    E, D, two_h = w_gu.shape
    n_blocks = n_rows // EXPERT_ROWS
    idx = jnp.arange(n_blocks, dtype=jnp.int32)
    is_live = idx < n_used[0]
    blk_new = (is_live & ((idx == 0) | (blk_e != jnp.roll(blk_e, 1)))).astype(jnp.int32)
    blk_ord = (jnp.cumsum(blk_new) - 1).astype(jnp.int32)
    n_distinct = blk_ord[-1:] + 1
    first_of = (blk_new[None, :] == 1) & (blk_ord[None, :] == idx[:, None])
    distinct_e = jnp.sum(jnp.where(first_of, blk_e[None, :], 0), axis=1).astype(jnp.int32)

    live = lambda i, nu: jnp.minimum(i, nu[0] - 1)
    grid_spec = pltpu.PrefetchScalarGridSpec(
        num_scalar_prefetch=6,
        grid=(n_blocks,),
        in_specs=[pl.BlockSpec((EXPERT_ROWS, half), lambda i, de, bo, bn, br, nu, nd: (live(i, nu), 0)),
                  pl.BlockSpec(memory_space=pl.ANY), pl.BlockSpec(memory_space=pl.ANY)],
        out_specs=pl.BlockSpec((EXPERT_ROWS, half), lambda i, de, bo, bn, br, nu, nd: (i, 0)),
        scratch_shapes=[pltpu.VMEM((2, D, two_h), F32), pltpu.VMEM((2, two_h // 2, D), F32),
                        pltpu.VMEM((D, two_h), BF16), pltpu.VMEM((two_h // 2, D), BF16),
                        pltpu.SemaphoreType.DMA((2, 2))],
    )
    return pl.pallas_call(
        _expert_kernel,
        grid_spec=grid_spec,
        out_shape=jax.ShapeDtypeStruct((n_rows, half), jnp.uint32),
        compiler_params=pltpu.CompilerParams(
            dimension_semantics=("arbitrary",), vmem_limit_bytes=VMEM_LIMIT),
        name="moe_experts",
    )(distinct_e, blk_ord, blk_new, blk_rows, n_used, n_distinct, xs, w_gu, w_down)


def _moe_out_kernel(x_ref, yk_ref, wts_ref, sgu_ref, sd_ref, g_ref, b_ref, *rest, alpha, has_prev):
    o_ref = rest[1] if has_prev else rest[0]
    x = x_ref[...]
    hidden = sd_ref.shape[0]
    h = _bdot(x, sgu_ref[...])
    gate, up = h[:, :hidden], h[:, hidden:]
    ffn = _bdot(gate * _sigmoid(gate) * up, sd_ref[...])
    wts = wts_ref[...]
    routed_left = routed_right = None
    for kk in range(TOP_K):
        left, right = _unpack_bf16_halves(yk_ref[kk])
        w = wts[:, kk:kk + 1]
        routed_left = w * left if kk == 0 else routed_left + w * left
        routed_right = w * right if kk == 0 else routed_right + w * right
    ffn = ffn + jnp.concatenate([routed_left, routed_right], axis=-1)
    o_ref[...] = _layer_norm(alpha * x + ffn, g_ref[...], b_ref[...])


def _moe_out(xf, yk_parts, wts, sw_gu, sw_down, ln_g, ln_b, alpha, tm=512):
    T, D = xf.shape
    steps = T // len(yk_parts) // tm
    full = lambda a: pl.BlockSpec(a.shape, lambda i: (0,) * a.ndim)
    ln_g, ln_b = ln_g.reshape(1, D), ln_b.reshape(1, D)
    out = None
    for p, yk in enumerate(yk_parts):
        rows = lambda w, off=p * steps: pl.BlockSpec((tm, w), lambda i: (i + off, 0))
        args = [xf, yk, wts, sw_gu, sw_down, ln_g, ln_b]
        in_specs = [rows(D), pl.BlockSpec((TOP_K, tm, D // 2), lambda i: (0, i, 0)), rows(PICK_LANES),
                    full(sw_gu), full(sw_down), full(ln_g), full(ln_b)]
        if out is not None:
            args.append(out)
            in_specs.append(pl.BlockSpec(memory_space=pl.ANY))
        out = pl.pallas_call(
            functools.partial(_moe_out_kernel, alpha=alpha, has_prev=out is not None),
            grid=(steps,),
            in_specs=in_specs,
            out_specs=rows(D),
            out_shape=jax.ShapeDtypeStruct((T, D), F32),
            input_output_aliases={len(args) - 1: 0} if out is not None else {},
            compiler_params=pltpu.CompilerParams(
                dimension_semantics=("parallel",), vmem_limit_bytes=VMEM_LIMIT),
            name="moe_combine_ln",
        )(*args)
    return out


def _moe_ffn_ln(xf, xp, router_w, router_bias, w_gu, w_down, sw_gu, sw_down, ln_g, ln_b, alpha):
    T, D = xf.shape
    E = router_w.shape[1]
    BM = EXPERT_ROWS
    eidx_t, wts_t, pos_t, cnt = _router(xf, router_w, router_bias)
    counts = cnt[:, 0].astype(jnp.int32)
    padded = (counts + BM - 1) // BM * BM
    pad_end = jnp.cumsum(padded)
    pad_start = pad_end - padded
    n_rows = T * TOP_K + E * BM
    n_blocks = n_rows // BM
    blk_row0 = jnp.arange(n_blocks, dtype=jnp.int32) * BM
    blk_e = jnp.minimum(jnp.sum((pad_end[None, :] <= blk_row0[:, None]).astype(jnp.int32), axis=1), E - 1)
    blk_rows = jnp.clip(pad_start[blk_e] + counts[blk_e] - blk_row0, 0, BM).astype(jnp.int32)
    n_used = (pad_end[-1:] // BM).astype(jnp.int32)
    dest_t = _dest_rows(eidx_t, pos_t, pad_start)
    wts = jnp.pad(wts_t.T, ((0, 0), (0, PICK_LANES - TOP_K)))
    xs = _sc_scatter_rows(xp, dest_t, n_rows)
    ys = _expert_ffn(xs, blk_e, blk_rows, n_used, w_gu, w_down)
    part = T // MOE_COMBINE_PARTS
    yk_parts = [_sc_gather_rows(ys, dest_t[:, p * part:(p + 1) * part].reshape(-1)).reshape(TOP_K, part, D // 2)
                for p in range(MOE_COMBINE_PARTS)]
    return _moe_out(xf, yk_parts, wts, sw_gu, sw_down, ln_g, ln_b, alpha)


def kernel(x, w_in, tshift_mu, rwkv_w0, rwkv_w2, rwkv_a0, rwkv_a2, rwkv_g2, rwkv_k_k, rwkv_k_a, rwkv_r_k, rwkv_lnx_w, rwkv_lnx_b, cmp_pe_k, cmp_w1_k, cmp_w2_k, cmp_pe_v, cmp_w1_v, cmp_w2_v, w_branch_a, w_branch_b, w_out, ln1_g, ln1_b, router_w, router_bias, exp_w_gu, exp_w_down, shared_w_gu, shared_w_down, ln2_g, ln2_b):
    B, S, D = x.shape
    depth = w_in.shape[0]
    alpha = (2 * depth) ** 0.25
    nsa_w = w_in.shape[2] - RWKV_IN_W - 2 * D
    for l in range(depth):
        xf = x.reshape(B * S, D)
        w_l = w_in[l]
        w_a = w_l[:, :RWKV_IN_W].astype(BF16)
        w_b = _nsa_weight_columns(w_l[:, RWKV_IN_W:RWKV_IN_W + nsa_w]).astype(BF16)
        w_g = w_l[:, RWKV_IN_W + nsa_w:].astype(BF16)
        kv_w = 6 * NSA_KV_WIDTH
        p_a = _matmul(xf, w_a, PROJ_ROWS, w_a.shape[1]).reshape(B, S, -1)
        p_kv = _matmul(xf, w_b[:, :kv_w], PROJ_ROWS, kv_w).reshape(B, S, -1)
        qg_t = _matmul_t(x, w_b[:, kv_w:].T, PROJ_ROWS)
        p_g = _matmul(xf, w_g, PROJ_ROWS, w_g.shape[1], BF16)
        y_a = _rwkv_time_mix(p_a, tshift_mu[l], rwkv_w0[l], rwkv_w2[l], rwkv_a0[l], rwkv_a2[l], rwkv_g2[l],
                             rwkv_k_k[l], rwkv_k_a[l], rwkv_r_k[l].reshape(-1), rwkv_lnx_w[l], rwkv_lnx_b[l])
        y_b = _nsa_branch(p_kv, qg_t, cmp_pe_k[l], cmp_w1_k[l], cmp_w2_k[l],
                          cmp_pe_v[l], cmp_w1_v[l], cmp_w2_v[l])
        x1, x1p = _mixer_out(xf, y_a.reshape(B * S, -1), y_b.reshape(B * S, -1), p_g,
                             w_branch_a[l].astype(BF16), w_branch_b[l].astype(BF16), w_out[l].astype(BF16),
                             ln1_g[l], ln1_b[l], alpha)
        x2 = _moe_ffn_ln(x1, x1p, router_w[l], router_bias[l], exp_w_gu[l], exp_w_down[l],
                         shared_w_gu[l].astype(BF16), shared_w_down[l].astype(BF16), ln2_g[l], ln2_b[l], alpha)
        x = x2.reshape(B, S, D)
    return x
```
